```python
import jax, jax.numpy as jnp
from jax import lax
import numpy as np

D_MODEL = 1024
BATCH = 8
SEQ = 8192
DEPTH = 2

CHUNK = 64
CONV_W = 4
LRU_WIDTH = D_MODEL
LRU_BLOCKS = 16
LRU_BLOCK_DIM = LRU_WIDTH // LRU_BLOCKS
LRU_C = 8.0
SSD_EXPAND = 2
SSD_INNER = SSD_EXPAND * D_MODEL
SSD_HEAD_DIM = 64
SSD_HEADS = SSD_INNER // SSD_HEAD_DIM
SSD_GROUPS = 4
SSD_HEADS_PER_GROUP = SSD_HEADS // SSD_GROUPS
SSD_STATE = 128
SSD_CONV_DIM = SSD_INNER + 2 * SSD_GROUPS * SSD_STATE
N_BRANCH = 2
D_FF = ((8 * D_MODEL + 3 * 256 - 1) // (3 * 256)) * 256
EPS = 1e-6
IN_WIDTHS = (LRU_WIDTH, LRU_WIDTH, SSD_INNER, SSD_CONV_DIM, SSD_HEADS, N_BRANCH * D_MODEL)
IN_DIM = sum(IN_WIDTHS)

kernel_name = "hybrid_rglru_ssd_parallel_gated_block"


def _split(t, widths):
    offs = np.cumsum(widths)[:-1].tolist()
    return jnp.split(t, offs, axis=-1)


def rms_norm(x, g):
    xf = x.astype(jnp.float32)
    y = xf * lax.rsqrt(jnp.mean(xf * xf, axis=-1, keepdims=True) + EPS)
    return (y * g.astype(jnp.float32)).astype(x.dtype)


def causal_dw_conv(x, w, b):
    y = lax.conv_general_dilated(
        x, w[:, None, :].astype(x.dtype), window_strides=(1,), padding=[(CONV_W - 1, 0)],
        dimension_numbers=('NWC', 'WIO', 'NWC'), feature_group_count=x.shape[-1])
    return y + b.astype(x.dtype)


def rg_lru(x, w_a, b_a, w_x, b_x, lam):
    bsz, s, w = x.shape
    f32 = jnp.float32
    xf = x.astype(f32)
    xb = xf.reshape(bsz, s, LRU_BLOCKS, LRU_BLOCK_DIM)
    r = jax.nn.sigmoid(jnp.einsum('bshi,hij->bshj', xb, w_a.astype(f32)).reshape(bsz, s, w) + b_a.astype(f32))
    i = jax.nn.sigmoid(jnp.einsum('bshi,hij->bshj', xb, w_x.astype(f32)).reshape(bsz, s, w) + b_x.astype(f32))
    log_a = -LRU_C * r * jax.nn.softplus(-lam.astype(f32))
    a = jnp.exp(log_a)
    u = jnp.sqrt(-jnp.expm1(2.0 * log_a)) * (i * xf)

    def combine(lhs, rhs):
        a1, b1 = lhs
        a2, b2 = rhs
        return a1 * a2, a2 * b1 + b2

    _, h = lax.associative_scan(combine, (a, u), axis=1)
    return h.astype(x.dtype)


def ssd_scan(x, dt, A, Bm, Cm):
    b, s = x.shape[:2]
    c = s // CHUNK
    G, K, P, N = SSD_GROUPS, SSD_HEADS_PER_GROUP, SSD_HEAD_DIM, SSD_STATE
    xdt = (x * dt[..., None]).reshape(b, c, CHUNK, G, K, P)
    a = (dt * A).reshape(b, c, CHUNK, G, K)
    Bc = Bm.reshape(b, c, CHUNK, G, N)
    Cc = Cm.reshape(b, c, CHUNK, G, N)
    a_cs = jnp.cumsum(a, axis=2)
    seg = a_cs[:, :, :, None] - a_cs[:, :, None, :]
    causal = jnp.tril(jnp.ones((CHUNK, CHUNK), dtype=bool))[None, None, :, :, None, None]
    decay = jnp.exp(jnp.where(causal, seg, -jnp.inf))
    scores = jnp.einsum('bclgn,bcsgn->bclsg', Cc, Bc)
    y_diag = jnp.einsum('bclsg,bclsgk,bcsgkp->bclgkp', scores, decay, xdt)
    decay_to_end = jnp.exp(a_cs[:, :, -1:] - a_cs)
    states = jnp.einsum('bclgn,bclgk,bclgkp->bcgkpn', Bc, decay_to_end, xdt)
    chunk_decay = jnp.exp(a_cs[:, :, -1])

    def step(h, inp):
        st, dc = inp
        return h * dc[..., None, None] + st, h

    h0 = jnp.zeros((b, G, K, P, N), x.dtype)
    _, prev = lax.scan(step, h0, (jnp.moveaxis(states, 1, 0), jnp.moveaxis(chunk_decay, 1, 0)))
    prev = jnp.moveaxis(prev, 0, 1)
    y_off = jnp.einsum('bclgn,bcgkpn,bclgk->bclgkp', Cc, prev, jnp.exp(a_cs))
    return (y_diag + y_off).reshape(b, s, G * K, P)


def hybrid_mixer(xn, w_in, b_gate, lru_conv_w, lru_conv_b, lru_w_a, lru_b_a, lru_w_x, lru_b_x,
                 lru_lambda, ssd_conv_w, ssd_conv_b, ssd_dt_bias, ssd_A_log, ssd_D, ssd_norm_g,
                 w_branch, w_out):
    bsz, s, _ = xn.shape
    f32 = jnp.float32
    proj = xn @ w_in
    lru_x, lru_gate, z, xbc, dt_raw, gates = _split(proj, IN_WIDTHS)
    u = causal_dw_conv(lru_x, lru_conv_w, lru_conv_b)
    h = rg_lru(u, lru_w_a, lru_b_a, lru_w_x, lru_b_x, lru_lambda)
    y_a = jax.nn.gelu(lru_gate) * h
    xbc = jax.nn.silu(causal_dw_conv(xbc, ssd_conv_w, ssd_conv_b))
    xs, Bm, Cm = _split(xbc, (SSD_INNER, SSD_GROUPS * SSD_STATE, SSD_GROUPS * SSD_STATE))
    dt = jax.nn.softplus(dt_raw.astype(f32) + ssd_dt_bias.astype(f32))
    A = -jnp.exp(ssd_A_log.astype(f32))
    xh = xs.astype(f32).reshape(bsz, s, SSD_HEADS, SSD_HEAD_DIM)
    y = ssd_scan(xh, dt, A,
                 Bm.astype(f32).reshape(bsz, s, SSD_GROUPS, SSD_STATE),
                 Cm.astype(f32).reshape(bsz, s, SSD_GROUPS, SSD_STATE))
    y = y + ssd_D.astype(f32)[:, None] * xh
    y = y.reshape(bsz, s, SSD_INNER) * jax.nn.silu(z.astype(f32))
    yg = y.reshape(bsz, s, SSD_GROUPS, SSD_INNER // SSD_GROUPS)
    yg = yg * lax.rsqrt(jnp.mean(yg * yg, axis=-1, keepdims=True) + EPS)
    y_b = (yg.reshape(bsz, s, SSD_INNER) * ssd_norm_g.astype(f32)).astype(xn.dtype)
    g = jax.nn.sigmoid(gates + b_gate)
    g_a, g_b = _split(g, (D_MODEL, D_MODEL))
    merged = g_a * (y_a @ w_branch[:LRU_WIDTH]) + g_b * (y_b @ w_branch[LRU_WIDTH:])
    return merged @ w_out


def swiglu(xn, w_ffn_in, w_ffn_out):
    gate, up = _split(xn @ w_ffn_in, (D_FF, D_FF))
    return (jax.nn.silu(gate) * up) @ w_ffn_out


def _fwd_setup_inputs(seed: int = 0) -> dict:
    key = jax.random.key(seed)
    ks = jax.random.split(key, 24)
    nrm = lambda k, shape, scale: jax.random.normal(k, shape, jnp.float32) * scale
    L = DEPTH
    a_c = jax.random.uniform(ks[9], (L, LRU_WIDTH), jnp.float32, 0.9, 0.999)
    sig = a_c ** (1.0 / LRU_C)
    lru_lambda = jnp.log(sig) - jnp.log1p(-sig)
    dt0 = jnp.exp(jax.random.uniform(ks[12], (L, SSD_HEADS), jnp.float32, np.log(1e-3), np.log(1e-1)))
    ssd_dt_bias = dt0 + jnp.log(-jnp.expm1(-dt0))
    ssd_A_log = jnp.log(jax.random.uniform(ks[13], (L, SSD_HEADS), jnp.float32, 1.0, 16.0))
    w_branch = jnp.concatenate([
        nrm(ks[16], (L, LRU_WIDTH, D_MODEL), LRU_WIDTH ** -0.5),
        nrm(ks[17], (L, SSD_INNER, D_MODEL), SSD_INNER ** -0.5)], axis=1)
    return {
        "x": nrm(ks[0], (BATCH, SEQ, D_MODEL), 1.0),
        "norm1_g": 1.0 + nrm(ks[1], (L, D_MODEL), 0.02),
        "w_in": nrm(ks[2], (L, D_MODEL, IN_DIM), D_MODEL ** -0.5),
        "b_gate": nrm(ks[3], (L, N_BRANCH * D_MODEL), 0.02),
        "lru_conv_w": nrm(ks[4], (L, CONV_W, LRU_WIDTH), CONV_W ** -0.5),
        "lru_conv_b": nrm(ks[5], (L, LRU_WIDTH), 0.02),
        "lru_w_a": nrm(ks[6], (L, LRU_BLOCKS, LRU_BLOCK_DIM, LRU_BLOCK_DIM), LRU_BLOCK_DIM ** -0.5),
        "lru_b_a": nrm(ks[7], (L, LRU_WIDTH), 0.02),
        "lru_w_x": nrm(ks[8], (L, LRU_BLOCKS, LRU_BLOCK_DIM, LRU_BLOCK_DIM), LRU_BLOCK_DIM ** -0.5),
        "lru_b_x": nrm(ks[18], (L, LRU_WIDTH), 0.02),
        "lru_lambda": lru_lambda,
        "ssd_conv_w": nrm(ks[10], (L, CONV_W, SSD_CONV_DIM), CONV_W ** -0.5),
        "ssd_conv_b": nrm(ks[11], (L, SSD_CONV_DIM), 0.02),
        "ssd_dt_bias": ssd_dt_bias,
        "ssd_A_log": ssd_A_log,
        "ssd_D": 1.0 + nrm(ks[14], (L, SSD_HEADS), 0.02),
        "ssd_norm_g": 1.0 + nrm(ks[15], (L, SSD_INNER), 0.02),
        "w_branch": w_branch,
        "w_out": nrm(ks[19], (L, D_MODEL, D_MODEL), D_MODEL ** -0.5),
        "norm2_g": 1.0 + nrm(ks[20], (L, D_MODEL), 0.02),
        "w_ffn_in": nrm(ks[21], (L, D_MODEL, 2 * D_FF), D_MODEL ** -0.5),
        "w_ffn_out": nrm(ks[22], (L, D_FF, D_MODEL), D_FF ** -0.5),
        "norm_f": 1.0 + nrm(ks[23], (D_MODEL,), 0.02),
    }


def _fwd_reference(x, norm1_g, w_in, b_gate, lru_conv_w, lru_conv_b, lru_w_a, lru_b_a, lru_w_x, lru_b_x,
              lru_lambda, ssd_conv_w, ssd_conv_b, ssd_dt_bias, ssd_A_log, ssd_D, ssd_norm_g,
              w_branch, w_out, norm2_g, w_ffn_in, w_ffn_out, norm_f):
    h = x
    for l in range(DEPTH):
        h = h + hybrid_mixer(rms_norm(h, norm1_g[l]), w_in[l], b_gate[l], lru_conv_w[l], lru_conv_b[l],
                             lru_w_a[l], lru_b_a[l], lru_w_x[l], lru_b_x[l], lru_lambda[l],
                             ssd_conv_w[l], ssd_conv_b[l], ssd_dt_bias[l], ssd_A_log[l], ssd_D[l],
                             ssd_norm_g[l], w_branch[l], w_out[l])
        h = h + swiglu(rms_norm(h, norm2_g[l]), w_ffn_in[l], w_ffn_out[l])
    return rms_norm(h, norm_f)


import jax as _jax
import jax.numpy as _jnp

TWIN_FORMAT = 'train_step'
FWD_PARAMS = ['x', 'norm1_g', 'w_in', 'b_gate', 'lru_conv_w', 'lru_conv_b', 'lru_w_a', 'lru_b_a', 'lru_w_x', 'lru_b_x', 'lru_lambda', 'ssd_conv_w', 'ssd_conv_b', 'ssd_dt_bias', 'ssd_A_log', 'ssd_D', 'ssd_norm_g', 'w_branch', 'w_out', 'norm2_g', 'w_ffn_in', 'w_ffn_out', 'norm_f']
TWIN_WEIGHTS = ['norm1_g', 'w_in', 'b_gate', 'lru_conv_w', 'lru_conv_b', 'lru_w_a', 'lru_b_a', 'lru_w_x', 'lru_b_x', 'lru_lambda', 'ssd_conv_w', 'ssd_conv_b', 'ssd_dt_bias', 'ssd_A_log', 'ssd_D', 'ssd_norm_g', 'w_branch', 'w_out', 'norm2_g', 'w_ffn_in', 'w_ffn_out', 'norm_f']
TWIN_DIFF_INPUT = 'x'
TWIN_INPUTS = ['x', 'norm1_g', 'w_in', 'b_gate', 'lru_conv_w', 'lru_conv_b', 'lru_w_a', 'lru_b_a', 'lru_w_x', 'lru_b_x', 'lru_lambda', 'ssd_conv_w', 'ssd_conv_b', 'ssd_dt_bias', 'ssd_A_log', 'ssd_D', 'ssd_norm_g', 'w_branch', 'w_out', 'norm2_g', 'w_ffn_in', 'w_ffn_out', 'norm_f', 'loss_target', 'm_norm1_g', 'm_w_in', 'm_b_gate', 'm_lru_conv_w', 'm_lru_conv_b', 'm_lru_w_a', 'm_lru_b_a', 'm_lru_w_x', 'm_lru_b_x', 'm_lru_lambda', 'm_ssd_conv_w', 'm_ssd_conv_b', 'm_ssd_dt_bias', 'm_ssd_A_log', 'm_ssd_D', 'm_ssd_norm_g', 'm_w_branch', 'm_w_out', 'm_norm2_g', 'm_w_ffn_in', 'm_w_ffn_out', 'm_norm_f', 'v_norm1_g', 'v_w_in', 'v_b_gate', 'v_lru_conv_w', 'v_lru_conv_b', 'v_lru_w_a', 'v_lru_b_a', 'v_lru_w_x', 'v_lru_b_x', 'v_lru_lambda', 'v_ssd_conv_w', 'v_ssd_conv_b', 'v_ssd_dt_bias', 'v_ssd_A_log', 'v_ssd_D', 'v_ssd_norm_g', 'v_w_branch', 'v_w_out', 'v_norm2_g', 'v_w_ffn_in', 'v_w_ffn_out', 'v_norm_f']
TWIN_OUTPUTS = ['loss', 'grad_x', 'grad_norm1_g', 'grad_w_in', 'grad_b_gate', 'grad_lru_conv_w', 'grad_lru_conv_b', 'grad_lru_w_a', 'grad_lru_b_a', 'grad_lru_w_x', 'grad_lru_b_x', 'grad_lru_lambda', 'grad_ssd_conv_w', 'grad_ssd_conv_b', 'grad_ssd_dt_bias', 'grad_ssd_A_log', 'grad_ssd_D', 'grad_ssd_norm_g', 'grad_w_branch', 'grad_w_out', 'grad_norm2_g', 'grad_w_ffn_in', 'grad_w_ffn_out', 'grad_norm_f', 'delta_norm1_g', 'delta_w_in', 'delta_b_gate', 'delta_lru_conv_w', 'delta_lru_conv_b', 'delta_lru_w_a', 'delta_lru_b_a', 'delta_lru_w_x', 'delta_lru_b_x', 'delta_lru_lambda', 'delta_ssd_conv_w', 'delta_ssd_conv_b', 'delta_ssd_dt_bias', 'delta_ssd_A_log', 'delta_ssd_D', 'delta_ssd_norm_g', 'delta_w_branch', 'delta_w_out', 'delta_norm2_g', 'delta_w_ffn_in', 'delta_w_ffn_out', 'delta_norm_f', 'new_m_norm1_g', 'new_m_w_in', 'new_m_b_gate', 'new_m_lru_conv_w', 'new_m_lru_conv_b', 'new_m_lru_w_a', 'new_m_lru_b_a', 'new_m_lru_w_x', 'new_m_lru_b_x', 'new_m_lru_lambda', 'new_m_ssd_conv_w', 'new_m_ssd_conv_b', 'new_m_ssd_dt_bias', 'new_m_ssd_A_log', 'new_m_ssd_D', 'new_m_ssd_norm_g', 'new_m_w_branch', 'new_m_w_out', 'new_m_norm2_g', 'new_m_w_ffn_in', 'new_m_w_ffn_out', 'new_m_norm_f', 'new_v_norm1_g', 'new_v_w_in', 'new_v_b_gate', 'new_v_lru_conv_w', 'new_v_lru_conv_b', 'new_v_lru_w_a', 'new_v_lru_b_a', 'new_v_lru_w_x', 'new_v_lru_b_x', 'new_v_lru_lambda', 'new_v_ssd_conv_w', 'new_v_ssd_conv_b', 'new_v_ssd_dt_bias', 'new_v_ssd_A_log', 'new_v_ssd_D', 'new_v_ssd_norm_g', 'new_v_w_branch', 'new_v_w_out', 'new_v_norm2_g', 'new_v_w_ffn_in', 'new_v_w_ffn_out', 'new_v_norm_f']
TWIN_LEAF_KINDS = {'loss': 'loss', 'grad_x': 'grad_x', 'grad_norm1_g': 'grad_w', 'grad_w_in': 'grad_w', 'grad_b_gate': 'grad_w', 'grad_lru_conv_w': 'grad_w', 'grad_lru_conv_b': 'grad_w', 'grad_lru_w_a': 'grad_w', 'grad_lru_b_a': 'grad_w', 'grad_lru_w_x': 'grad_w', 'grad_lru_b_x': 'grad_w', 'grad_lru_lambda': 'grad_w', 'grad_ssd_conv_w': 'grad_w', 'grad_ssd_conv_b': 'grad_w', 'grad_ssd_dt_bias': 'grad_w', 'grad_ssd_A_log': 'grad_w', 'grad_ssd_D': 'grad_w', 'grad_ssd_norm_g': 'grad_w', 'grad_w_branch': 'grad_w', 'grad_w_out': 'grad_w', 'grad_norm2_g': 'grad_w', 'grad_w_ffn_in': 'grad_w', 'grad_w_ffn_out': 'grad_w', 'grad_norm_f': 'grad_w', 'delta_norm1_g': 'delta_w', 'delta_w_in': 'delta_w', 'delta_b_gate': 'delta_w', 'delta_lru_conv_w': 'delta_w', 'delta_lru_conv_b': 'delta_w', 'delta_lru_w_a': 'delta_w', 'delta_lru_b_a': 'delta_w', 'delta_lru_w_x': 'delta_w', 'delta_lru_b_x': 'delta_w', 'delta_lru_lambda': 'delta_w', 'delta_ssd_conv_w': 'delta_w', 'delta_ssd_conv_b': 'delta_w', 'delta_ssd_dt_bias': 'delta_w', 'delta_ssd_A_log': 'delta_w', 'delta_ssd_D': 'delta_w', 'delta_ssd_norm_g': 'delta_w', 'delta_w_branch': 'delta_w', 'delta_w_out': 'delta_w', 'delta_norm2_g': 'delta_w', 'delta_w_ffn_in': 'delta_w', 'delta_w_ffn_out': 'delta_w', 'delta_norm_f': 'delta_w', 'new_m_norm1_g': 'new_m', 'new_m_w_in': 'new_m', 'new_m_b_gate': 'new_m', 'new_m_lru_conv_w': 'new_m', 'new_m_lru_conv_b': 'new_m', 'new_m_lru_w_a': 'new_m', 'new_m_lru_b_a': 'new_m', 'new_m_lru_w_x': 'new_m', 'new_m_lru_b_x': 'new_m', 'new_m_lru_lambda': 'new_m', 'new_m_ssd_conv_w': 'new_m', 'new_m_ssd_conv_b': 'new_m', 'new_m_ssd_dt_bias': 'new_m', 'new_m_ssd_A_log': 'new_m', 'new_m_ssd_D': 'new_m', 'new_m_ssd_norm_g': 'new_m', 'new_m_w_branch': 'new_m', 'new_m_w_out': 'new_m', 'new_m_norm2_g': 'new_m', 'new_m_w_ffn_in': 'new_m', 'new_m_w_ffn_out': 'new_m', 'new_m_norm_f': 'new_m', 'new_v_norm1_g': 'new_v', 'new_v_w_in': 'new_v', 'new_v_b_gate': 'new_v', 'new_v_lru_conv_w': 'new_v', 'new_v_lru_conv_b': 'new_v', 'new_v_lru_w_a': 'new_v', 'new_v_lru_b_a': 'new_v', 'new_v_lru_w_x': 'new_v', 'new_v_lru_b_x': 'new_v', 'new_v_lru_lambda': 'new_v', 'new_v_ssd_conv_w': 'new_v', 'new_v_ssd_conv_b': 'new_v', 'new_v_ssd_dt_bias': 'new_v', 'new_v_ssd_A_log': 'new_v', 'new_v_ssd_D': 'new_v', 'new_v_ssd_norm_g': 'new_v', 'new_v_w_branch': 'new_v', 'new_v_w_out': 'new_v', 'new_v_norm2_g': 'new_v', 'new_v_w_ffn_in': 'new_v', 'new_v_w_ffn_out': 'new_v', 'new_v_norm_f': 'new_v'}


def _forward(args):
    return _fwd_reference(*[args[k] for k in FWD_PARAMS])


def _output_shape():
    def fwd():
        inp = _fwd_setup_inputs(0)
        return _fwd_reference(*[inp[k] for k in FWD_PARAMS])
    out = _jax.eval_shape(fwd)
    return out.shape, out.dtype

N_MICROBATCH = 1
ADAM_LR = 0.001
ADAM_B1 = 0.9
ADAM_B2 = 0.999
ADAM_EPS = 1e-08
ADAM_WD = 0.01
ADAM_STEP = 10
PER_EXAMPLE_BATCH_AXIS = {'x': 0, 'loss_target': 0}
SHARED_INPUTS = []
_WEIGHT_DTYPES = {'norm1_g': _jnp.float32, 'w_in': _jnp.float32, 'b_gate': _jnp.float32, 'lru_conv_w': _jnp.float32, 'lru_conv_b': _jnp.float32, 'lru_w_a': _jnp.float32, 'lru_b_a': _jnp.float32, 'lru_w_x': _jnp.float32, 'lru_b_x': _jnp.float32, 'lru_lambda': _jnp.float32, 'ssd_conv_w': _jnp.float32, 'ssd_conv_b': _jnp.float32, 'ssd_dt_bias': _jnp.float32, 'ssd_A_log': _jnp.float32, 'ssd_D': _jnp.float32, 'ssd_norm_g': _jnp.float32, 'w_branch': _jnp.float32, 'w_out': _jnp.float32, 'norm2_g': _jnp.float32, 'w_ffn_in': _jnp.float32, 'w_ffn_out': _jnp.float32, 'norm_f': _jnp.float32}
MOMENT_SCALE = {'norm1_g': 2.232564e-01, 'w_in': 7.418026e-02, 'b_gate': 4.135554e-02, 'lru_conv_w': 7.827745e-02, 'lru_conv_b': 7.076711e-01, 'lru_w_a': 2.639636e-02, 'lru_b_a': 2.090950e-02, 'lru_w_x': 4.736201e-02, 'lru_b_x': 2.697792e-02, 'lru_lambda': 3.971400e-02, 'ssd_conv_w': 8.169894e-02, 'ssd_conv_b': 1.088752e-01, 'ssd_dt_bias': 2.622111e-01, 'ssd_A_log': 2.319831e-01, 'ssd_D': 4.895215e-01, 'ssd_norm_g': 9.477798e-02, 'w_branch': 1.149693e-01, 'w_out': 1.500944e-01, 'norm2_g': 1.724759e-01, 'w_ffn_in': 7.308928e-02, 'w_ffn_out': 1.192472e-01, 'norm_f': 6.393208e+01}


def _to_microbatches(a, axis):
    t = _jnp.moveaxis(a, axis, 0)
    t = t.reshape((N_MICROBATCH, t.shape[0] // N_MICROBATCH) + t.shape[1:])
    return _jnp.moveaxis(t, 1, axis + 1)


def setup_inputs(seed: int = 0) -> dict:
    inp = _fwd_setup_inputs(seed)
    key = _jax.random.fold_in(_jax.random.key(seed), 7919)
    shape, _ = _output_shape()
    out = dict(inp)
    out["loss_target"] = _jax.random.normal(_jax.random.fold_in(key, 0), shape, _jnp.float32)
    for i, name in enumerate(TWIN_WEIGHTS):
        w = inp[name].astype(_jnp.float32)
        if MOMENT_SCALE is None:
            s = _jnp.sqrt(_jnp.mean(_jnp.square(w)) + 1e-30)
        else:
            s = MOMENT_SCALE[name]
        km, kv = _jax.random.split(_jax.random.fold_in(key, i + 1))
        out[name] = w
        out["m_" + name] = s * _jax.random.normal(km, w.shape, _jnp.float32)
        out["v_" + name] = (s * s) * _jax.random.uniform(kv, w.shape, _jnp.float32, 0.5, 1.5)
    if N_MICROBATCH > 1:
        for name, axis in PER_EXAMPLE_BATCH_AXIS.items():
            out[name] = _to_microbatches(out[name], axis)
    return {'x': out['x'], 'norm1_g': out['norm1_g'], 'w_in': out['w_in'], 'b_gate': out['b_gate'], 'lru_conv_w': out['lru_conv_w'], 'lru_conv_b': out['lru_conv_b'], 'lru_w_a': out['lru_w_a'], 'lru_b_a': out['lru_b_a'], 'lru_w_x': out['lru_w_x'], 'lru_b_x': out['lru_b_x'], 'lru_lambda': out['lru_lambda'], 'ssd_conv_w': out['ssd_conv_w'], 'ssd_conv_b': out['ssd_conv_b'], 'ssd_dt_bias': out['ssd_dt_bias'], 'ssd_A_log': out['ssd_A_log'], 'ssd_D': out['ssd_D'], 'ssd_norm_g': out['ssd_norm_g'], 'w_branch': out['w_branch'], 'w_out': out['w_out'], 'norm2_g': out['norm2_g'], 'w_ffn_in': out['w_ffn_in'], 'w_ffn_out': out['w_ffn_out'], 'norm_f': out['norm_f'], 'loss_target': out['loss_target'], 'm_norm1_g': out['m_norm1_g'], 'm_w_in': out['m_w_in'], 'm_b_gate': out['m_b_gate'], 'm_lru_conv_w': out['m_lru_conv_w'], 'm_lru_conv_b': out['m_lru_conv_b'], 'm_lru_w_a': out['m_lru_w_a'], 'm_lru_b_a': out['m_lru_b_a'], 'm_lru_w_x': out['m_lru_w_x'], 'm_lru_b_x': out['m_lru_b_x'], 'm_lru_lambda': out['m_lru_lambda'], 'm_ssd_conv_w': out['m_ssd_conv_w'], 'm_ssd_conv_b': out['m_ssd_conv_b'], 'm_ssd_dt_bias': out['m_ssd_dt_bias'], 'm_ssd_A_log': out['m_ssd_A_log'], 'm_ssd_D': out['m_ssd_D'], 'm_ssd_norm_g': out['m_ssd_norm_g'], 'm_w_branch': out['m_w_branch'], 'm_w_out': out['m_w_out'], 'm_norm2_g': out['m_norm2_g'], 'm_w_ffn_in': out['m_w_ffn_in'], 'm_w_ffn_out': out['m_w_ffn_out'], 'm_norm_f': out['m_norm_f'], 'v_norm1_g': out['v_norm1_g'], 'v_w_in': out['v_w_in'], 'v_b_gate': out['v_b_gate'], 'v_lru_conv_w': out['v_lru_conv_w'], 'v_lru_conv_b': out['v_lru_conv_b'], 'v_lru_w_a': out['v_lru_w_a'], 'v_lru_b_a': out['v_lru_b_a'], 'v_lru_w_x': out['v_lru_w_x'], 'v_lru_b_x': out['v_lru_b_x'], 'v_lru_lambda': out['v_lru_lambda'], 'v_ssd_conv_w': out['v_ssd_conv_w'], 'v_ssd_conv_b': out['v_ssd_conv_b'], 'v_ssd_dt_bias': out['v_ssd_dt_bias'], 'v_ssd_A_log': out['v_ssd_A_log'], 'v_ssd_D': out['v_ssd_D'], 'v_ssd_norm_g': out['v_ssd_norm_g'], 'v_w_branch': out['v_w_branch'], 'v_w_out': out['v_w_out'], 'v_norm2_g': out['v_norm2_g'], 'v_w_ffn_in': out['v_w_ffn_in'], 'v_w_ffn_out': out['v_w_ffn_out'], 'v_norm_f': out['v_norm_f']}


def _loss(weights, diff, rest, loss_target):
    with _jax.named_scope("forward"):
        args = {**rest, TWIN_DIFF_INPUT: diff, **{k: w.astype(_WEIGHT_DTYPES[k]) for k, w in weights.items()}}
        y = _forward(args)
    with _jax.named_scope("loss_head"):
        err = _jnp.square(y.astype(_jnp.float32) - loss_target)
        return 0.5 * _jnp.sum(_jnp.mean(err, axis=-1)) if err.ndim else 0.5 * err


def _adamw(w, g, m, v):
    m = ADAM_B1 * m + (1.0 - ADAM_B1) * g
    v = ADAM_B2 * v + (1.0 - ADAM_B2) * _jnp.square(g)
    m_hat = m / (1.0 - ADAM_B1 ** ADAM_STEP)
    v_hat = v / (1.0 - ADAM_B2 ** ADAM_STEP)
    delta = -ADAM_LR * (m_hat / (_jnp.sqrt(v_hat) + ADAM_EPS) + ADAM_WD * w)
    return delta, m, v


def reference(x, norm1_g, w_in, b_gate, lru_conv_w, lru_conv_b, lru_w_a, lru_b_a, lru_w_x, lru_b_x, lru_lambda, ssd_conv_w, ssd_conv_b, ssd_dt_bias, ssd_A_log, ssd_D, ssd_norm_g, w_branch, w_out, norm2_g, w_ffn_in, w_ffn_out, norm_f, loss_target, m_norm1_g, m_w_in, m_b_gate, m_lru_conv_w, m_lru_conv_b, m_lru_w_a, m_lru_b_a, m_lru_w_x, m_lru_b_x, m_lru_lambda, m_ssd_conv_w, m_ssd_conv_b, m_ssd_dt_bias, m_ssd_A_log, m_ssd_D, m_ssd_norm_g, m_w_branch, m_w_out, m_norm2_g, m_w_ffn_in, m_w_ffn_out, m_norm_f, v_norm1_g, v_w_in, v_b_gate, v_lru_conv_w, v_lru_conv_b, v_lru_w_a, v_lru_b_a, v_lru_w_x, v_lru_b_x, v_lru_lambda, v_ssd_conv_w, v_ssd_conv_b, v_ssd_dt_bias, v_ssd_A_log, v_ssd_D, v_ssd_norm_g, v_w_branch, v_w_out, v_norm2_g, v_w_ffn_in, v_w_ffn_out, v_norm_f):
    given = dict(x=x, norm1_g=norm1_g, w_in=w_in, b_gate=b_gate, lru_conv_w=lru_conv_w, lru_conv_b=lru_conv_b, lru_w_a=lru_w_a, lru_b_a=lru_b_a, lru_w_x=lru_w_x, lru_b_x=lru_b_x, lru_lambda=lru_lambda, ssd_conv_w=ssd_conv_w, ssd_conv_b=ssd_conv_b, ssd_dt_bias=ssd_dt_bias, ssd_A_log=ssd_A_log, ssd_D=ssd_D, ssd_norm_g=ssd_norm_g, w_branch=w_branch, w_out=w_out, norm2_g=norm2_g, w_ffn_in=w_ffn_in, w_ffn_out=w_ffn_out, norm_f=norm_f, loss_target=loss_target, m_norm1_g=m_norm1_g, m_w_in=m_w_in, m_b_gate=m_b_gate, m_lru_conv_w=m_lru_conv_w, m_lru_conv_b=m_lru_conv_b, m_lru_w_a=m_lru_w_a, m_lru_b_a=m_lru_b_a, m_lru_w_x=m_lru_w_x, m_lru_b_x=m_lru_b_x, m_lru_lambda=m_lru_lambda, m_ssd_conv_w=m_ssd_conv_w, m_ssd_conv_b=m_ssd_conv_b, m_ssd_dt_bias=m_ssd_dt_bias, m_ssd_A_log=m_ssd_A_log, m_ssd_D=m_ssd_D, m_ssd_norm_g=m_ssd_norm_g, m_w_branch=m_w_branch, m_w_out=m_w_out, m_norm2_g=m_norm2_g, m_w_ffn_in=m_w_ffn_in, m_w_ffn_out=m_w_ffn_out, m_norm_f=m_norm_f, v_norm1_g=v_norm1_g, v_w_in=v_w_in, v_b_gate=v_b_gate, v_lru_conv_w=v_lru_conv_w, v_lru_conv_b=v_lru_conv_b, v_lru_w_a=v_lru_w_a, v_lru_b_a=v_lru_b_a, v_lru_w_x=v_lru_w_x, v_lru_b_x=v_lru_b_x, v_lru_lambda=v_lru_lambda, v_ssd_conv_w=v_ssd_conv_w, v_ssd_conv_b=v_ssd_conv_b, v_ssd_dt_bias=v_ssd_dt_bias, v_ssd_A_log=v_ssd_A_log, v_ssd_D=v_ssd_D, v_ssd_norm_g=v_ssd_norm_g, v_w_branch=v_w_branch, v_w_out=v_w_out, v_norm2_g=v_norm2_g, v_w_ffn_in=v_w_ffn_in, v_w_ffn_out=v_w_ffn_out, v_norm_f=v_norm_f)
    weights = {n: given[n] for n in TWIN_WEIGHTS}
    shared = {n: given[n] for n in SHARED_INPUTS}
    per_example = {n: given[n] for n in ['x']}
    grad_fn = _jax.value_and_grad(_loss, argnums=(0, 1))

    def one_microbatch(ex, loss_target):
        ex = dict(ex)
        diff = ex.pop(TWIN_DIFF_INPUT)
        return grad_fn(weights, diff, {**shared, **ex}, loss_target)

    if N_MICROBATCH == 1:
        loss, (grad_w, grad_x) = one_microbatch(per_example, given["loss_target"])
    else:
        def body(carry, xs):
            loss_sum, grad_sum = carry
            l_k, (gw_k, gx_k) = one_microbatch(xs[0], xs[1])
            with _jax.named_scope("update"):
                return (loss_sum + l_k, _jax.tree.map(_jnp.add, grad_sum, gw_k)), gx_k

        init = (_jnp.zeros((), _jnp.float32), _jax.tree.map(_jnp.zeros_like, weights))
        (loss, grad_w), grad_x = _jax.lax.scan(body, init, (per_example, given["loss_target"]))
    with _jax.named_scope("update"):
        delta_w, new_m, new_v = {}, {}, {}
        for n in TWIN_WEIGHTS:
            delta_w[n], new_m[n], new_v[n] = _adamw(weights[n], grad_w[n], given["m_" + n], given["v_" + n])
    return (loss, grad_x, *[grad_w[n] for n in TWIN_WEIGHTS], *[delta_w[n] for n in TWIN_WEIGHTS],
            *[new_m[n] for n in TWIN_WEIGHTS], *[new_v[n] for n in TWIN_WEIGHTS])
```

```python
import functools
import math

import numpy as np
import jax
import jax.numpy as jnp
from jax import lax
from jax.experimental import pallas as pl
from jax.experimental.pallas import tpu as pltpu

F32 = jnp.float32
BF16 = jnp.bfloat16
_MXU = jnp.bfloat16
_HI = lax.Precision.HIGHEST

D = 1024
EPS = 1e-6
N_LAYERS = 2
LRU_C = 8.0
N_HEADS = 32
HEAD_P = 64
N_GROUPS = 4
N_STATE = 128
SSD_INNER = 2048
XBC = 3072
D_FF = 2816
CHUNK = 64
IN_DIM = 9248

NP = 9216
ZX_W = 5120
G0 = 6144
LBLK = 512
DT_PAD = 128

VMEM_LIMIT_BYTES_V7X = 56 * 1024 * 1024

ADAM_LR, ADAM_B1, ADAM_B2, ADAM_EPS, ADAM_WD, ADAM_STEP = 0.001, 0.9, 0.999, 1e-08, 0.01, 10
MESH = pl.DeviceIdType.MESH


def _cp(sem):
    return pltpu.CompilerParams(dimension_semantics=sem, vmem_limit_bytes=VMEM_LIMIT_BYTES_V7X)


def _lblk_col(j):
    return 10 + j + 4 * (j // 2)


def _sigmoid(x):
    return 1.0 / (1.0 + jnp.exp(-x))


def _softplus(x):
    return jnp.maximum(x, 0.0) + jnp.log(1.0 + jnp.exp(-jnp.abs(x)))


def _silu(x):
    return x * _sigmoid(x)


def _dsilu(x):
    s = _sigmoid(x)
    return s * (1.0 + x * (1.0 - s))


_GELU_C0 = math.sqrt(2.0 / math.pi)
_GELU_C1 = 0.044715


def _gelu_and_grad(x):
    t = jnp.tanh(_GELU_C0 * (x + _GELU_C1 * x * x * x))
    g = 0.5 * x * (1.0 + t)
    dg = 0.5 * (1.0 + t) + 0.5 * x * (1.0 - t * t) * _GELU_C0 * (1.0 + 3.0 * _GELU_C1 * x * x)
    return g, dg


def _one_minus_exp(x):
    p = 1.0 + x * (1.0 / 7.0)
    p = 1.0 + x * (1.0 / 6.0) * p
    p = 1.0 + x * (1.0 / 5.0) * p
    p = 1.0 + x * (1.0 / 4.0) * p
    p = 1.0 + x * (1.0 / 3.0) * p
    p = 1.0 + x * (1.0 / 2.0) * p
    return jnp.where(x > -0.3, -x * p, 1.0 - jnp.exp(x))


def _dot(a, b):
    return jnp.dot(a.astype(_MXU), b.astype(_MXU), preferred_element_type=F32)


def _dot_nt(a, b):
    return lax.dot_general(a.astype(_MXU), b.astype(_MXU), (((1,), (1,)), ((), ())), preferred_element_type=F32)


def _dot_tn(a, b):
    return lax.dot_general(a.astype(_MXU), b.astype(_MXU), (((0,), (0,)), ((), ())), preferred_element_type=F32)


def _shift_down(x, prev8, k):
    xr = pltpu.roll(x, k, 0)
    pr = pltpu.roll(prev8, k, 0)
    row = lax.broadcasted_iota(jnp.int32, prev8.shape, 0)
    head = jnp.where(row < k, pr, xr[0:8])
    return jnp.concatenate([head, xr[8:]], axis=0)


def _shift_up(x, next8, k):
    r = x.shape[0]
    xr = pltpu.roll(x, r - k, 0)
    nr = pltpu.roll(next8, 8 - k, 0)
    row = lax.broadcasted_iota(jnp.int32, next8.shape, 0)
    tail = jnp.where(row >= 8 - k, nr, xr[r - 8:r])
    return jnp.concatenate([xr[:r - 8], tail], axis=0)


def _conv4(x, prev8, w_ref, b_ref):
    acc = x * w_ref[3:4, :] + b_ref[0:1, :]
    for k in (1, 2, 3):
        acc = acc + _shift_down(x, prev8, k) * w_ref[3 - k:4 - k, :]
    return acc


def _conv4_bwd_x(dy, next8, w_ref):
    acc = dy * w_ref[3:4, :]
    for k in (1, 2, 3):
        acc = acc + _shift_up(dy, next8, k) * w_ref[3 - k:4 - k, :]
    return acc


def _lin_scan(a, b, reverse):
    r = a.shape[0]
    row = lax.broadcasted_iota(jnp.int32, a.shape, 0)
    d = 1
    while d < r:
        sh = (r - d) if reverse else d
        a_s = pltpu.roll(a, sh, 0)
        b_s = pltpu.roll(b, sh, 0)
        m = (row < r - d) if reverse else (row >= d)
        b = jnp.where(m, a * b_s + b, b)
        a = jnp.where(m, a * a_s, a)
        d *= 2
    return a, b


def _rsum(x):
    return jnp.sum(x, axis=0, keepdims=True)


def _norm_mm(h, gamma, w, *, tm, tn, name):
    m, k = h.shape
    n = w.shape[1]

    def body(h_ref, g_ref, w_ref, xn_ref, o_ref):
        @pl.when(pl.program_id(1) == 0)
        def _():
            x = h_ref[...]
            r = lax.rsqrt(jnp.mean(x * x, axis=-1, keepdims=True) + EPS)
            xn_ref[...] = ((x * r) * g_ref[...]).astype(xn_ref.dtype)
        o_ref[...] = jnp.dot(xn_ref[...], w_ref[...], preferred_element_type=F32)

    return pl.pallas_call(
        body, name=name, grid=(m // tm, n // tn),
        in_specs=[pl.BlockSpec((tm, k), lambda i, j: (i, 0)), pl.BlockSpec((1, k), lambda i, j: (0, 0)),
                  pl.BlockSpec((k, tn), lambda i, j: (0, j))],
        out_specs=[pl.BlockSpec((tm, k), lambda i, j: (i, 0)), pl.BlockSpec((tm, tn), lambda i, j: (i, j))],
        out_shape=[jax.ShapeDtypeStruct((m, k), _MXU), jax.ShapeDtypeStruct((m, n), F32)],
        compiler_params=_cp(("parallel", "arbitrary")),
    )(h, gamma, w)


def _mm_nn(a, w, *, tm, tn, name, residual=None):
    m, k = a.shape
    n = w.shape[1]

    def body(*refs):
        if residual is None:
            a_ref, w_ref, o_ref = refs
            o_ref[...] = _dot(a_ref[...], w_ref[...])
        else:
            a_ref, w_ref, r_ref, o_ref = refs
            o_ref[...] = _dot(a_ref[...], w_ref[...]) + r_ref[...]

    in_specs = [pl.BlockSpec((tm, k), lambda i, j: (i, 0)), pl.BlockSpec((k, tn), lambda i, j: (0, j))]
    args = [a, w]
    if residual is not None:
        in_specs.append(pl.BlockSpec((tm, tn), lambda i, j: (i, j)))
        args.append(residual)
    return pl.pallas_call(
        body, name=name, grid=(m // tm, n // tn), in_specs=in_specs,
        out_specs=pl.BlockSpec((tm, tn), lambda i, j: (i, j)),
        out_shape=jax.ShapeDtypeStruct((m, n), F32),
        compiler_params=_cp(("parallel", "parallel")),
    )(*args)


def _mm_tn(a, b, *, tt, tn, name):
    t, ka = a.shape
    nb = b.shape[1]

    def body(a_ref, b_ref, o_ref):
        @pl.when(pl.program_id(1) == 0)
        def _():
            o_ref[...] = jnp.zeros_like(o_ref)
        o_ref[...] += _dot_tn(a_ref[...], b_ref[...])

    return pl.pallas_call(
        body, name=name, grid=(nb // tn, t // tt),
        in_specs=[pl.BlockSpec((tt, ka), lambda j, i: (i, 0)), pl.BlockSpec((tt, tn), lambda j, i: (i, j))],
        out_specs=pl.BlockSpec((ka, tn), lambda j, i: (0, j)),
        out_shape=jax.ShapeDtypeStruct((ka, nb), F32),
        compiler_params=_cp(("parallel", "arbitrary")),
    )(a, b)


def _mm_nt(a, w, *, tm, name):
    m, kc = a.shape
    n = w.shape[0]

    def body(a_ref, w_ref, o_ref):
        o_ref[...] = _dot_nt(a_ref[...], w_ref[...])

    return pl.pallas_call(
        body, name=name, grid=(m // tm,),
        in_specs=[pl.BlockSpec((tm, kc), lambda i: (i, 0)), pl.BlockSpec((n, kc), lambda i: (0, 0))],
        out_specs=pl.BlockSpec((tm, n), lambda i: (i, 0)),
        out_shape=jax.ShapeDtypeStruct((m, n), F32),
        compiler_params=_cp(("parallel",)),
    )(a, w)


def _mm_nt_rmsbwd(dy, w, x, gamma, dres, *, tm, tk, name, extra=None):
    m, kc = dy.shape
    d = w.shape[0]
    nk = kc // tk

    def body(*refs):
        if extra is None:
            dy_ref, w_ref, x_ref, g_ref, r_ref, dx_ref, dg_ref, acc_ref = refs
        else:
            dy_ref, w_ref, x_ref, g_ref, r_ref, dy2_ref, w2_ref, dx_ref, dg_ref, acc_ref = refs
        i, kk = pl.program_id(0), pl.program_id(1)

        @pl.when(kk == 0)
        def _():
            acc_ref[...] = jnp.zeros_like(acc_ref)

        @pl.when((i == 0) & (kk == 0))
        def _():
            dg_ref[...] = jnp.zeros_like(dg_ref)

        acc_ref[...] += _dot_nt(dy_ref[...], w_ref[...])

        @pl.when(kk == nk - 1)
        def _():
            dxn = acc_ref[...]
            if extra is not None:
                dxn = dxn + _dot_nt(dy2_ref[...], w2_ref[...])
            xv = x_ref[...]
            r = lax.rsqrt(jnp.mean(xv * xv, axis=-1, keepdims=True) + EPS)
            xh = xv * r
            dg_ref[0:1, :] += _rsum(dxn * xh)
            dxh = dxn * g_ref[...]
            dx_ref[...] = r_ref[...] + r * (dxh - xh * jnp.mean(dxh * xh, axis=-1, keepdims=True))

    in_specs = [pl.BlockSpec((tm, tk), lambda i, k: (i, k)), pl.BlockSpec((d, tk), lambda i, k: (0, k)),
                pl.BlockSpec((tm, d), lambda i, k: (i, 0)), pl.BlockSpec((1, d), lambda i, k: (0, 0)),
                pl.BlockSpec((tm, d), lambda i, k: (i, 0))]
    args = [dy, w, x, gamma, dres]
    if extra is not None:
        k2 = extra[0].shape[1]
        in_specs += [pl.BlockSpec((tm, k2), lambda i, k: (i, 0)), pl.BlockSpec((d, k2), lambda i, k: (0, 0))]
        args += list(extra)
    return pl.pallas_call(
        body, name=name, grid=(m // tm, nk), in_specs=in_specs,
        out_specs=[pl.BlockSpec((tm, d), lambda i, k: (i, 0)), pl.BlockSpec((8, d), lambda i, k: (0, 0))],
        out_shape=[jax.ShapeDtypeStruct((m, d), F32), jax.ShapeDtypeStruct((8, d), F32)],
        scratch_shapes=[pltpu.VMEM((tm, d), F32)],
        compiler_params=_cp(("arbitrary", "arbitrary")),
    )(*args)


def _lru_gates(x, prev8, cw_ref, cb_ref, wa_ref, wx_ref, ba_ref, bx_ref, lam_ref):
    u = _conv4(x, prev8, cw_ref, cb_ref)
    ra = _sigmoid(_dot(u, wa_ref[0]) + ba_ref[...])
    ia = _sigmoid(_dot(u, wx_ref[0]) + bx_ref[...])
    sp = _softplus(-lam_ref[...])
    log_a = -LRU_C * ra * sp
    a = jnp.exp(log_a)
    m2 = _one_minus_exp(2.0 * log_a)
    mult = jnp.sqrt(m2)
    return u, ra, ia, sp, a, m2, mult


def _lru_fwd(proj, lw, *, r, name):
    t = proj.shape[0]
    nt = t // r

    def body(xg_ref, xp_ref, cw_ref, cb_ref, wa_ref, wx_ref, ba_ref, bx_ref, lam_ref, hl_ref, ya_ref, carry_ref):
        i = pl.program_id(1)

        @pl.when(i == 0)
        def _():
            carry_ref[...] = jnp.zeros_like(carry_ref)

        x = xg_ref[:, 0:256]
        lg = xg_ref[:, 256:512]
        prev8 = jnp.where(i == 0, 0.0, xp_ref[:, 0:256])
        u, ra, ia, sp, a, m2, mult = _lru_gates(x, prev8, cw_ref, cb_ref, wa_ref, wx_ref, ba_ref, bx_ref, lam_ref)
        ac, hc = _lin_scan(a, mult * ia * u, False)
        h = hc + ac * carry_ref[0:1, :]
        hl_ref[...] = h
        carry_ref[0:1, :] = hl_ref[r - 1:r, :]
        g, _ = _gelu_and_grad(lg)
        ya_ref[...] = (g * h).astype(ya_ref.dtype)

    small = lambda rows: pl.BlockSpec((rows, 256), lambda j, i: (0, j))
    return pl.pallas_call(
        body, name=name, grid=(4, nt),
        in_specs=[pl.BlockSpec((r, LBLK), lambda j, i: (i, _lblk_col(j))),
                  pl.BlockSpec((8, LBLK), lambda j, i: (jnp.maximum(i * (r // 8) - 1, 0), _lblk_col(j))),
                  small(4), small(1),
                  pl.BlockSpec((1, 256, 256), lambda j, i: (j, 0, 0)), pl.BlockSpec((1, 256, 256), lambda j, i: (j, 0, 0)),
                  small(1), small(1), small(1)],
        out_specs=[pl.BlockSpec((r, 256), lambda j, i: (i, j)), pl.BlockSpec((r, 256), lambda j, i: (i, j))],
        out_shape=[jax.ShapeDtypeStruct((t, D), F32), jax.ShapeDtypeStruct((t, D), _MXU)],
        scratch_shapes=[pltpu.VMEM((8, 256), F32)],
        compiler_params=_cp(("parallel", "arbitrary")),
    )(proj, proj, lw["cw"], lw["cb"], lw["wa"], lw["wx"], lw["ba"], lw["bx"], lw["lam"])


def _lru_bwd(proj, hl, dya, dproj, lw, *, r, name):
    t = proj.shape[0]
    nt = t // r

    def body(xg_ref, xp_ref, hl_ref, hp_ref, dya_ref, cw_ref, cb_ref, wa_ref, wx_ref, ba_ref, bx_ref, lam_ref, dproj_in,
             dproj_ref, sm_ref, dwa_ref, dwx_ref, carry_ref, du8_ref, lam_scr, a_scr, du_scr):
        del dproj_in
        i = pl.program_id(1)
        first = i == 0

        @pl.when(first)
        def _():
            carry_ref[...] = jnp.zeros_like(carry_ref)
            du8_ref[...] = jnp.zeros_like(du8_ref)
            sm_ref[...] = jnp.zeros_like(sm_ref)
            dwa_ref[...] = jnp.zeros_like(dwa_ref)
            dwx_ref[...] = jnp.zeros_like(dwx_ref)

        x = xg_ref[:, 0:256]
        lg = xg_ref[:, 256:512]
        tile0 = i == nt - 1
        prev8 = jnp.where(tile0, 0.0, xp_ref[:, 0:256])
        hprev8 = jnp.where(tile0, 0.0, hp_ref[...])
        u, ra, ia, sp, a, m2, mult = _lru_gates(x, prev8, cw_ref, cb_ref, wa_ref, wx_ref, ba_ref, bx_ref, lam_ref)
        h = hl_ref[...]
        hprev = _shift_down(h, hprev8, 1)
        dya_v = dya_ref[...]
        g, dg = _gelu_and_grad(lg)
        dh = dya_v * g
        dlg = dya_v * h * dg
        a_next = _shift_up(a, carry_ref[...], 1)
        ac, lc = _lin_scan(a_next, dh, True)
        lam_v = lc + ac * carry_ref[1:2, :]
        lam_scr[...] = lam_v
        a_scr[...] = a
        carry_ref[1:2, :] = lam_scr[0:1, :]
        carry_ref[0:1, :] = a_scr[0:1, :]
        da = lam_v * hprev
        dmult = lam_v * ia * u
        dia = lam_v * mult * u
        du = lam_v * mult * ia
        dlog = da * a - dmult * (1.0 - m2) / mult
        dra = -LRU_C * sp * dlog
        dsp = _rsum(-LRU_C * ra * dlog)
        dpa = dra * ra * (1.0 - ra)
        dpx = dia * ia * (1.0 - ia)
        du = du + _dot_nt(dpa, wa_ref[0]) + _dot_nt(dpx, wx_ref[0])
        dwa_ref[0] += _dot_tn(u, dpa)
        dwx_ref[0] += _dot_tn(u, dpx)
        dlx = _conv4_bwd_x(du, du8_ref[...], cw_ref)
        du_scr[...] = du
        du8_ref[...] = du_scr[0:8, :]
        dproj_ref[:, 0:256] = dlx.astype(dproj_ref.dtype)
        dproj_ref[:, 256:512] = dlg.astype(dproj_ref.dtype)
        sm_ref[3:4, :] += _rsum(du * x)
        for k in (1, 2, 3):
            sm_ref[3 - k:4 - k, :] += _rsum(du * _shift_down(x, prev8, k))
        sm_ref[4:5, :] += _rsum(du)
        sm_ref[5:6, :] += _rsum(dpa)
        sm_ref[6:7, :] += _rsum(dpx)
        sm_ref[7:8, :] += dsp * (-_sigmoid(-lam_ref[...]))

    rev = lambda i: nt - 1 - i
    small = lambda rows: pl.BlockSpec((rows, 256), lambda j, i: (0, j))
    wblk = pl.BlockSpec((1, 256, 256), lambda j, i: (j, 0, 0))
    n_in = 13
    return pl.pallas_call(
        body, name=name, grid=(4, nt),
        in_specs=[pl.BlockSpec((r, LBLK), lambda j, i: (rev(i), _lblk_col(j))),
                  pl.BlockSpec((8, LBLK), lambda j, i: (jnp.maximum(rev(i) * (r // 8) - 1, 0), _lblk_col(j))),
                  pl.BlockSpec((r, 256), lambda j, i: (rev(i), j)),
                  pl.BlockSpec((8, 256), lambda j, i: (jnp.maximum(rev(i) * (r // 8) - 1, 0), j)),
                  pl.BlockSpec((r, 256), lambda j, i: (rev(i), j)),
                  small(4), small(1), wblk, wblk, small(1), small(1), small(1),
                  pl.BlockSpec(memory_space=pl.ANY)],
        out_specs=[pl.BlockSpec((r, LBLK), lambda j, i: (rev(i), _lblk_col(j))),
                   pl.BlockSpec((8, 256), lambda j, i: (0, j)), wblk, wblk],
        out_shape=[jax.ShapeDtypeStruct(dproj.shape, dproj.dtype), jax.ShapeDtypeStruct((8, D), F32),
                   jax.ShapeDtypeStruct((4, 256, 256), F32), jax.ShapeDtypeStruct((4, 256, 256), F32)],
        scratch_shapes=[pltpu.VMEM((8, 256), F32), pltpu.VMEM((8, 256), F32), pltpu.VMEM((r, 256), F32),
                        pltpu.VMEM((r, 256), F32), pltpu.VMEM((r, 256), F32)],
        input_output_aliases={n_in - 1: 0},
        compiler_params=_cp(("parallel", "arbitrary")),
    )(proj, proj, hl, hl, dya, lw["cw"], lw["cb"], lw["wa"], lw["wx"], lw["ba"], lw["bx"], lw["lam"], dproj)


def _head_cols(x):
    lane = lax.broadcasted_iota(jnp.int32, x.shape, 1)
    return [jnp.sum(jnp.where(lane == h, x, 0.0), axis=1, keepdims=True) for h in range(N_HEADS)]


def _compact_heads(blocks):
    lane = lax.broadcasted_iota(jnp.int32, blocks[0].shape, 1)
    lo = lane < HEAD_P
    out = jnp.zeros_like(blocks[0])
    for j, blk in enumerate(blocks):
        s_lo = jnp.sum(jnp.where(lo, blk, 0.0), axis=1, keepdims=True)
        s_hi = jnp.sum(jnp.where(lo, 0.0, blk), axis=1, keepdims=True)
        out = jnp.where(lane == 2 * j, s_lo, out)
        out = jnp.where(lane == 2 * j + 1, s_hi, out)
    return out


def _ssd_prelude(dtraw_ref, dtb_ref, alog_ref, dt_scr, a_scr):
    lane = lax.broadcasted_iota(jnp.int32, dt_scr.shape, 1)
    dt = jnp.where(lane < N_HEADS, _softplus(dtraw_ref[...] + dtb_ref[0:1, :]), 0.0)
    dt_scr[...] = dt
    a_scr[...] = dt * (-jnp.exp(alog_ref[0:1, :]))


def _ssd_chunk_scalars(dt_scr, a_scr, r_scr, r0):
    a_c = a_scr[pl.ds(r0, CHUNK), :]
    dt_c = dt_scr[pl.ds(r0, CHUNK), :]
    i0 = lax.broadcasted_iota(jnp.int32, (CHUNK, CHUNK), 0)
    i1 = lax.broadcasted_iota(jnp.int32, (CHUNK, CHUNK), 1)
    tri = jnp.where(i0 >= i1, 1.0, 0.0).astype(F32)
    cs = jnp.dot(tri, a_c, precision=_HI, preferred_element_type=F32)
    lane = lax.broadcasted_iota(jnp.int32, (CHUNK, 128), 1)
    srow = lax.broadcasted_iota(jnp.int32, (CHUNK, 128), 0)
    t_lo = jnp.where((lane < HEAD_P) & (srow <= lane), 1.0, 0.0).astype(F32)
    t_hi = jnp.where((lane >= HEAD_P) & (srow <= lane - HEAD_P), 1.0, 0.0).astype(F32)
    even = (lane % 2) == 0
    tn = (((0,), (0,)), ((), ()))
    r_scr[...] = (lax.dot_general(jnp.where(even, a_c, 0.0), t_lo, tn, precision=_HI, preferred_element_type=F32)
                  + lax.dot_general(jnp.where(even, 0.0, a_c), t_hi, tn, precision=_HI, preferred_element_type=F32))
    return cs, dt_c, _head_cols(cs), _head_cols(dt_c)


def _block_diag2(v):
    lo = lax.broadcasted_iota(jnp.int32, v.shape, 1) < HEAD_P
    return jnp.concatenate([jnp.where(lo, v, 0.0), jnp.where(lo, 0.0, v)], axis=0).astype(_MXU)


def _ssd_pair(xc_scr, r_scr, cs_cols, dt_cols, s2, r0, j, s2t=None):
    lane = lax.broadcasted_iota(jnp.int32, (CHUNK, 128), 1)
    srow = lax.broadcasted_iota(jnp.int32, (CHUNK, 128), 0)
    lo = lane < HEAD_P
    csc = jnp.where(lo, cs_cols[2 * j], cs_cols[2 * j + 1])
    dtc = jnp.where(lo, dt_cols[2 * j], dt_cols[2 * j + 1])
    csr = r_scr[2 * j:2 * j + 1, :] + r_scr[2 * j + 1:2 * j + 2, :]
    dm = jnp.where((lane & (HEAD_P - 1)) <= srow, jnp.exp(jnp.minimum(csc - csr, 0.0)), 0.0)
    xs = xc_scr[pl.ds(r0, CHUNK), j * 128:(j + 1) * 128]
    xd = xs * dtc
    csl = jnp.sum(jnp.where(srow == CHUNK - 1, csc, 0.0), axis=0, keepdims=True)
    out = dict(csc=csc, dtc=dtc, dm=dm, m2=s2 * dm, xs=xs, xd=xd, rhs=_block_diag2(xd), e=jnp.exp(csc),
               w=jnp.exp(csl - csc), dec=jnp.exp(csl))
    if s2t is not None:
        out["mt2"] = s2t * jnp.where((lane & (HEAD_P - 1)) >= srow, jnp.exp(jnp.minimum(csr - csc, 0.0)), 0.0)
    return out


def _cat(parts):
    return jnp.concatenate(parts, axis=1)


def _ssd_fwd(proj, dtraw, sw, *, rb, name):
    t = proj.shape[0]
    ns, cb = t // rb, rb // CHUNK

    def body(zx_ref, zp_ref, dtraw_ref, cw_ref, cbias_ref, dtb_ref, alog_ref, dsk_ref, ng_ref,
             yssd_ref, yb_ref, st_ref, h_scr, xc_scr, dt_scr, a_scr, r_scr):
        i = pl.program_id(0)

        @pl.when(i == 0)
        def _():
            h_scr[...] = jnp.zeros_like(h_scr)

        prev8 = jnp.where(i == 0, 0.0, zp_ref[:, 2048:ZX_W])
        xc_scr[...] = _silu(_conv4(zx_ref[:, 2048:ZX_W], prev8, cw_ref, cbias_ref))
        _ssd_prelude(dtraw_ref, dtb_ref, alog_ref, dt_scr, a_scr)

        def chunk(c, carry):
            r0 = pl.multiple_of(c * CHUNK, CHUNK)
            _, _, cs_cols, dt_cols = _ssd_chunk_scalars(dt_scr, a_scr, r_scr, r0)
            st_ref[c] = h_scr[...]
            for g in range(N_GROUPS):
                bg = xc_scr[pl.ds(r0, CHUNK), 2048 + 128 * g:2048 + 128 * (g + 1)]
                cg = xc_scr[pl.ds(r0, CHUNK), 2560 + 128 * g:2560 + 128 * (g + 1)]
                s2 = _dot_nt(cg, jnp.concatenate([bg, bg], axis=0))
                hp = h_scr[:, 512 * g:512 * (g + 1)]
                yoff = _dot(cg, hp)
                xdw, dec = [], []
                for jj in range(4):
                    j = 4 * g + jj
                    p = _ssd_pair(xc_scr, r_scr, cs_cols, dt_cols, s2, r0, j)
                    y = _dot(p["m2"], p["rhs"]) + yoff[:, 128 * jj:128 * (jj + 1)] * p["e"]
                    yssd_ref[pl.ds(r0, CHUNK), 128 * j:128 * (j + 1)] = y + dsk_ref[0:1, 128 * j:128 * (j + 1)] * p["xs"]
                    xdw.append(p["xd"] * p["w"])
                    dec.append(p["dec"])
                h_scr[:, 512 * g:512 * (g + 1)] = hp * _cat(dec) + _dot_tn(bg, _cat(xdw))
            return carry

        lax.fori_loop(0, cb, chunk, 0)
        for g in range(N_GROUPS):
            sl = slice(512 * g, 512 * (g + 1))
            yz = yssd_ref[:, sl] * _silu(zx_ref[:, sl])
            rg = lax.rsqrt(jnp.mean(yz * yz, axis=-1, keepdims=True) + EPS)
            yb_ref[:, sl] = (yz * rg * ng_ref[0:1, sl]).astype(yb_ref.dtype)

    full = lambda rows, cols: pl.BlockSpec((rows, cols), lambda i: (0, 0))
    return pl.pallas_call(
        body, name=name, grid=(ns,),
        in_specs=[pl.BlockSpec((rb, ZX_W), lambda i: (i, 0)),
                  pl.BlockSpec((8, ZX_W), lambda i: (jnp.maximum(i * (rb // 8) - 1, 0), 0)),
                  pl.BlockSpec((rb, DT_PAD), lambda i: (i, 0)),
                  full(4, XBC), full(1, XBC), full(1, DT_PAD), full(1, DT_PAD), full(1, SSD_INNER), full(1, SSD_INNER)],
        out_specs=[pl.BlockSpec((rb, SSD_INNER), lambda i: (i, 0)), pl.BlockSpec((rb, SSD_INNER), lambda i: (i, 0)),
                   pl.BlockSpec((cb, N_STATE, SSD_INNER), lambda i: (i, 0, 0))],
        out_shape=[jax.ShapeDtypeStruct((t, SSD_INNER), F32), jax.ShapeDtypeStruct((t, SSD_INNER), _MXU),
                   jax.ShapeDtypeStruct((t // CHUNK, N_STATE, SSD_INNER), F32)],
        scratch_shapes=[pltpu.VMEM((N_STATE, SSD_INNER), F32), pltpu.VMEM((rb, XBC), F32), pltpu.VMEM((rb, DT_PAD), F32),
                        pltpu.VMEM((rb, DT_PAD), F32), pltpu.VMEM((128, 128), F32)],
        compiler_params=_cp(("arbitrary",)),
    )(proj, proj, dtraw, sw["cw"], sw["cb"], sw["dtb"], sw["alog"], sw["dsk"], sw["ng"])


def _ssd_bwd(proj, dtraw, yssd, states, dyb, dproj, sw, *, rb, name):
    t = proj.shape[0]
    ns, cb = t // rb, rb // CHUNK

    def body(zx_ref, zp_ref, dtraw_ref, yssd_ref, st_ref, dyb_ref, cw_ref, cbias_ref, dtb_ref, alog_ref, dsk_ref, ng_ref,
             dproj_in, dzx_ref, ddt_ref, gconv_ref, gch_ref, ghd_ref,
             dht_scr, xc_scr, pre_scr, dy_scr, dxc_scr, dt_scr, a_scr, r_scr, dp8_scr):
        del dproj_in
        i = pl.program_id(0)

        @pl.when(i == 0)
        def _():
            dht_scr[...] = jnp.zeros_like(dht_scr)
            dp8_scr[...] = jnp.zeros_like(dp8_scr)
            gconv_ref[...] = jnp.zeros_like(gconv_ref)
            gch_ref[...] = jnp.zeros_like(gch_ref)
            ghd_ref[...] = jnp.zeros_like(ghd_ref)

        tile0 = i == ns - 1
        prev8 = jnp.where(tile0, 0.0, zp_ref[:, 2048:ZX_W])
        xraw = zx_ref[:, 2048:ZX_W]
        pre_scr[...] = _conv4(xraw, prev8, cw_ref, cbias_ref)
        xc_scr[...] = _silu(pre_scr[...])
        _ssd_prelude(dtraw_ref, dtb_ref, alog_ref, dt_scr, a_scr)

        for g in range(N_GROUPS):
            sl = slice(512 * g, 512 * (g + 1))
            zv = zx_ref[:, sl]
            ys = yssd_ref[:, sl]
            sz = _silu(zv)
            yz = ys * sz
            rg = lax.rsqrt(jnp.mean(yz * yz, axis=-1, keepdims=True) + EPS)
            yn = yz * rg
            dyb_v = dyb_ref[:, sl]
            gch_ref[0:1, sl] += _rsum(dyb_v * yn)
            dyn = dyb_v * ng_ref[0:1, sl]
            dyz = rg * (dyn - yn * jnp.mean(dyn * yn, axis=-1, keepdims=True))
            dy_scr[:, sl] = dyz * sz
            dzx_ref[:, sl] = (dyz * ys * _dsilu(zv)).astype(dzx_ref.dtype)

        a_row = -jnp.exp(alog_ref[0:1, :])

        def chunk(cc, carry):
            c = cb - 1 - cc
            r0 = pl.multiple_of(c * CHUNK, CHUNK)
            rows = pl.ds(r0, CHUNK)
            _, dt_c, cs_cols, dt_cols = _ssd_chunk_scalars(dt_scr, a_scr, r_scr, r0)
            lane = lax.broadcasted_iota(jnp.int32, (CHUNK, 128), 1)
            srow = lax.broadcasted_iota(jnp.int32, (CHUNK, 128), 0)
            lo = lane < HEAD_P
            last = srow == CHUNK - 1
            p1_blocks, p3_blocks = [], []
            for g in range(N_GROUPS):
                gs = slice(512 * g, 512 * (g + 1))
                bg = xc_scr[rows, 2048 + 128 * g:2048 + 128 * (g + 1)]
                cg = xc_scr[rows, 2560 + 128 * g:2560 + 128 * (g + 1)]
                b2 = jnp.concatenate([bg, bg], axis=0)
                s2 = _dot_nt(cg, b2)
                s2t = _dot_nt(bg, jnp.concatenate([cg, cg], axis=0))
                hp = st_ref[c, :, gs]
                dht = dht_scr[:, gs]
                yoff = _dot(cg, hp)
                ps = [_ssd_pair(xc_scr, r_scr, cs_cols, dt_cols, s2, r0, 4 * g + jj, s2t) for jj in range(4)]
                dys = [dy_scr[rows, 128 * (4 * g + jj):128 * (4 * g + jj + 1)] for jj in range(4)]
                dye = _cat([dys[jj] * ps[jj]["e"] for jj in range(4)])
                w_g = _cat([p["w"] for p in ps])
                dcg = _dot_nt(dye, hp)
                dht_scr[:, gs] = _dot_tn(cg, dye) + _cat([p["dec"] for p in ps]) * dht
                dxd_state = w_g * _dot(bg, dht)
                dbg = _dot_nt(_cat([p["xd"] for p in ps]) * w_g, dht)
                tsum = _rsum(dht * hp)
                ds2 = jnp.zeros((CHUNK, 128), F32)
                for jj in range(4):
                    j = 4 * g + jj
                    ls = slice(128 * j, 128 * (j + 1))
                    p, dy2 = ps[jj], dys[jj]
                    dy_bd = _block_diag2(dy2)
                    dm2 = _dot_nt(dy2, p["rhs"])
                    ds2 = ds2 + dm2 * p["dm"]
                    gdiff = dm2 * p["m2"] - _dot_nt(p["xd"], dy_bd) * p["mt2"]
                    dxs = dxd_state[:, 128 * jj:128 * (jj + 1)]
                    dxd = _dot(p["mt2"], dy_bd) + dxs
                    end_row = _rsum(p["xd"] * dxs) + p["dec"] * tsum[:, 128 * jj:128 * (jj + 1)]
                    p1_blocks.append(gdiff + dy2 * yoff[:, 128 * jj:128 * (jj + 1)] * p["e"] - p["xd"] * dxs
                                     + jnp.where(last, end_row, 0.0))
                    p3_blocks.append(dxd * p["xs"])
                    dxc_scr[rows, ls] = dxd * p["dtc"] + dy2 * dsk_ref[0:1, ls]
                    gch_ref[1:2, ls] += _rsum(dy2 * p["xs"])
                dcg = dcg + _dot(ds2, b2)
                rb2 = _dot_tn(ds2, cg)
                dxc_scr[rows, 2048 + 128 * g:2048 + 128 * (g + 1)] = dbg + rb2[0:CHUNK] + rb2[CHUNK:2 * CHUNK]
                dxc_scr[rows, 2560 + 128 * g:2560 + 128 * (g + 1)] = dcg
            dcs = _compact_heads(p1_blocks)
            i0 = lax.broadcasted_iota(jnp.int32, (CHUNK, CHUNK), 0)
            i1 = lax.broadcasted_iota(jnp.int32, (CHUNK, CHUNK), 1)
            triu = jnp.where(i1 >= i0, 1.0, 0.0).astype(F32)
            da = jnp.dot(triu, dcs, precision=_HI, preferred_element_type=F32)
            ddt = _compact_heads(p3_blocks) + da * a_row
            ddtraw = jnp.where(lane < N_HEADS, ddt * _sigmoid(dtraw_ref[rows, :] + dtb_ref[0:1, :]), 0.0)
            ddt_ref[rows, :] = ddtraw.astype(ddt_ref.dtype)
            ghd_ref[0:1, :] += _rsum(ddtraw)
            ghd_ref[1:2, :] += _rsum(da * dt_c) * a_row
            return carry

        lax.fori_loop(0, cb, chunk, 0)
        dpre = dxc_scr[...] * _dsilu(pre_scr[...])
        dzx_ref[:, 2048:ZX_W] = _conv4_bwd_x(dpre, dp8_scr[...], cw_ref).astype(dzx_ref.dtype)
        dxc_scr[...] = dpre
        dp8_scr[...] = dxc_scr[0:8, :]
        gconv_ref[3:4, :] += _rsum(dpre * xraw)
        for k in (1, 2, 3):
            gconv_ref[3 - k:4 - k, :] += _rsum(dpre * _shift_down(xraw, prev8, k))
        gconv_ref[4:5, :] += _rsum(dpre)

    rev = lambda i: ns - 1 - i
    full = lambda rows, cols: pl.BlockSpec((rows, cols), lambda i: (0, 0))
    n_in = 13
    return pl.pallas_call(
        body, name=name, grid=(ns,),
        in_specs=[pl.BlockSpec((rb, ZX_W), lambda i: (rev(i), 0)),
                  pl.BlockSpec((8, ZX_W), lambda i: (jnp.maximum(rev(i) * (rb // 8) - 1, 0), 0)),
                  pl.BlockSpec((rb, DT_PAD), lambda i: (rev(i), 0)),
                  pl.BlockSpec((rb, SSD_INNER), lambda i: (rev(i), 0)),
                  pl.BlockSpec((cb, N_STATE, SSD_INNER), lambda i: (rev(i), 0, 0)),
                  pl.BlockSpec((rb, SSD_INNER), lambda i: (rev(i), 0)),
                  full(4, XBC), full(1, XBC), full(1, DT_PAD), full(1, DT_PAD), full(1, SSD_INNER), full(1, SSD_INNER),
                  pl.BlockSpec(memory_space=pl.ANY)],
        out_specs=[pl.BlockSpec((rb, ZX_W), lambda i: (rev(i), 0)), pl.BlockSpec((rb, DT_PAD), lambda i: (rev(i), 0)),
                   full(8, XBC), full(8, SSD_INNER), full(8, DT_PAD)],
        out_shape=[jax.ShapeDtypeStruct(dproj.shape, dproj.dtype), jax.ShapeDtypeStruct((t, DT_PAD), _MXU),
                   jax.ShapeDtypeStruct((8, XBC), F32), jax.ShapeDtypeStruct((8, SSD_INNER), F32),
                   jax.ShapeDtypeStruct((8, DT_PAD), F32)],
        scratch_shapes=[pltpu.VMEM((N_STATE, SSD_INNER), F32), pltpu.VMEM((rb, XBC), F32), pltpu.VMEM((rb, XBC), F32),
                        pltpu.VMEM((rb, SSD_INNER), F32), pltpu.VMEM((rb, XBC), F32), pltpu.VMEM((rb, DT_PAD), F32),
                        pltpu.VMEM((rb, DT_PAD), F32), pltpu.VMEM((128, 128), F32), pltpu.VMEM((8, XBC), F32)],
        input_output_aliases={n_in - 1: 0},
        compiler_params=_cp(("arbitrary",)),
    )(proj, proj, dtraw, yssd, states, dyb, sw["cw"], sw["cb"], sw["dtb"], sw["alog"], sw["dsk"], sw["ng"], dproj)


def _branch_merge(ya, yb, proj, wba, wbb, bgate, *, tm, tn, name):
    t = ya.shape[0]
    nj = D // tn

    def body(ya_ref, yb_ref, ga_ref, gb_ref, wba_ref, wbb_ref, ba_ref, bb_ref, ta_ref, tb_ref, mg_ref):
        ta = _dot(ya_ref[...], wba_ref[...])
        tb = _dot(yb_ref[...], wbb_ref[...])
        ta_ref[...] = ta
        tb_ref[...] = tb
        ga = _sigmoid(ga_ref[...] + ba_ref[...])
        gb = _sigmoid(gb_ref[...] + bb_ref[...])
        mg_ref[...] = (ga * ta + gb * tb).astype(mg_ref.dtype)

    tile = pl.BlockSpec((tm, tn), lambda i, j: (i, j))
    return pl.pallas_call(
        body, name=name, grid=(t // tm, nj),
        in_specs=[pl.BlockSpec((tm, D), lambda i, j: (i, 0)), pl.BlockSpec((tm, SSD_INNER), lambda i, j: (i, 0)),
                  pl.BlockSpec((tm, tn), lambda i, j: (i, G0 // tn + j)),
                  pl.BlockSpec((tm, tn), lambda i, j: (i, (G0 + D) // tn + j)),
                  pl.BlockSpec((D, tn), lambda i, j: (0, j)), pl.BlockSpec((SSD_INNER, tn), lambda i, j: (0, j)),
                  pl.BlockSpec((1, tn), lambda i, j: (0, j)), pl.BlockSpec((1, tn), lambda i, j: (0, nj + j))],
        out_specs=[tile, tile, tile],
        out_shape=[jax.ShapeDtypeStruct((t, D), F32), jax.ShapeDtypeStruct((t, D), F32), jax.ShapeDtypeStruct((t, D), _MXU)],
        compiler_params=_cp(("parallel", "parallel")),
    )(ya, yb, proj, proj, wba, wbb, bgate, bgate)


def _swiglu_mm(gu, wfo, residual, *, tm, tn, name):
    t = gu.shape[0]

    def body(gu_ref, w_ref, r_ref, act_ref, o_ref):
        @pl.when(pl.program_id(1) == 0)
        def _():
            act_ref[...] = (_silu(gu_ref[:, 0:D_FF]) * gu_ref[:, D_FF:2 * D_FF]).astype(act_ref.dtype)
        o_ref[...] = jnp.dot(act_ref[...], w_ref[...], preferred_element_type=F32) + r_ref[...]

    return pl.pallas_call(
        body, name=name, grid=(t // tm, D // tn),
        in_specs=[pl.BlockSpec((tm, 2 * D_FF), lambda i, j: (i, 0)), pl.BlockSpec((D_FF, tn), lambda i, j: (0, j)),
                  pl.BlockSpec((tm, tn), lambda i, j: (i, j))],
        out_specs=[pl.BlockSpec((tm, D_FF), lambda i, j: (i, 0)), pl.BlockSpec((tm, tn), lambda i, j: (i, j))],
        out_shape=[jax.ShapeDtypeStruct((t, D_FF), _MXU), jax.ShapeDtypeStruct((t, D), F32)],
        compiler_params=_cp(("parallel", "arbitrary")),
    )(gu, wfo, residual)


def _ffn_bwd_act(dh, wfo, gu, *, tm, name):
    t = dh.shape[0]

    def body(dh_ref, w_ref, gu_ref, o_ref):
        dact = _dot_nt(dh_ref[...], w_ref[...])
        g = gu_ref[:, 0:D_FF]
        u = gu_ref[:, D_FF:2 * D_FF]
        o_ref[:, 0:D_FF] = (dact * u * _dsilu(g)).astype(o_ref.dtype)
        o_ref[:, D_FF:2 * D_FF] = (dact * _silu(g)).astype(o_ref.dtype)

    return pl.pallas_call(
        body, name=name, grid=(t // tm,),
        in_specs=[pl.BlockSpec((tm, D), lambda i: (i, 0)), pl.BlockSpec((D_FF, D), lambda i: (0, 0)),
                  pl.BlockSpec((tm, 2 * D_FF), lambda i: (i, 0))],
        out_specs=pl.BlockSpec((tm, 2 * D_FF), lambda i: (i, 0)),
        out_shape=jax.ShapeDtypeStruct((t, 2 * D_FF), _MXU),
        compiler_params=_cp(("parallel",)),
    )(dh, wfo, gu)


def _outproj_bwd(dh, wout, ta, tb, proj, bgate, dproj, *, tm, name):
    t = dh.shape[0]

    def body(dh_ref, w_ref, ta_ref, tb_ref, g_ref, b_ref, dta_ref, dtb_ref, dg_ref, db_ref):
        @pl.when(pl.program_id(0) == 0)
        def _():
            db_ref[...] = jnp.zeros_like(db_ref)
        dm = _dot_nt(dh_ref[...], w_ref[...])
        ga = _sigmoid(g_ref[:, 0:D] + b_ref[:, 0:D])
        gb = _sigmoid(g_ref[:, D:2 * D] + b_ref[:, D:2 * D])
        dta_ref[...] = (dm * ga).astype(dta_ref.dtype)
        dtb_ref[...] = (dm * gb).astype(dtb_ref.dtype)
        dga = dm * ta_ref[...] * ga * (1.0 - ga)
        dgb = dm * tb_ref[...] * gb * (1.0 - gb)
        dg_ref[:, 0:D] = dga.astype(dg_ref.dtype)
        dg_ref[:, D:2 * D] = dgb.astype(dg_ref.dtype)
        db_ref[0:1, 0:D] += _rsum(dga)
        db_ref[0:1, D:2 * D] += _rsum(dgb)

    row = lambda cols: pl.BlockSpec((tm, cols), lambda i: (i, 0))
    return pl.pallas_call(
        body, name=name, grid=(t // tm,),
        in_specs=[row(D), pl.BlockSpec((D, D), lambda i: (0, 0)), row(D), row(D),
                  pl.BlockSpec((tm, 2 * D), lambda i: (i, G0 // (2 * D))), pl.BlockSpec((1, 2 * D), lambda i: (0, 0))],
        out_specs=[row(D), row(D), pl.BlockSpec((tm, 2 * D), lambda i: (i, G0 // (2 * D))),
                   pl.BlockSpec((8, 2 * D), lambda i: (0, 0))],
        out_shape=[jax.ShapeDtypeStruct((t, D), _MXU), jax.ShapeDtypeStruct((t, D), _MXU),
                   jax.ShapeDtypeStruct(dproj, _MXU), jax.ShapeDtypeStruct((8, 2 * D), F32)],
        compiler_params=_cp(("arbitrary",)),
    )(dh, wout, ta, tb, proj, bgate)


def _loss_head(h, gf, target, *, tm, name):
    t = h.shape[0]

    def body(h_ref, g_ref, t_ref, loss_ref, dg_ref, dh_ref):
        @pl.when(pl.program_id(0) == 0)
        def _():
            loss_ref[...] = jnp.zeros_like(loss_ref)
            dg_ref[...] = jnp.zeros_like(dg_ref)
        x = h_ref[...]
        r = lax.rsqrt(jnp.mean(x * x, axis=-1, keepdims=True) + EPS)
        xh = x * r
        err = xh * g_ref[...] - t_ref[...]
        loss_ref[...] += 0.5 * jnp.sum(jnp.mean(err * err, axis=-1, keepdims=True), axis=0, keepdims=True)
        dy = err * (1.0 / D)
        dg_ref[0:1, :] += _rsum(dy * xh)
        dxh = dy * g_ref[...]
        dh_ref[...] = r * (dxh - xh * jnp.mean(dxh * xh, axis=-1, keepdims=True))

    row = pl.BlockSpec((tm, D), lambda i: (i, 0))
    return pl.pallas_call(
        body, name=name, grid=(t // tm,),
        in_specs=[row, pl.BlockSpec((1, D), lambda i: (0, 0)), row],
        out_specs=[pl.BlockSpec((8, 128), lambda i: (0, 0)), pl.BlockSpec((8, D), lambda i: (0, 0)), row],
        out_shape=[jax.ShapeDtypeStruct((8, 128), F32), jax.ShapeDtypeStruct((8, D), F32), jax.ShapeDtypeStruct((t, D), F32)],
        compiler_params=_cp(("arbitrary",)),
    )(h, gf, target)


def _row_tile(rows, cols, limit_bytes=1 << 20):
    best = None
    for tr in range(8, rows + 1, 8):
        if rows % tr == 0 and tr * cols * 4 <= limit_bytes:
            best = tr
    return best if best is not None else rows


def _adamw(w, g, m, v, *, name):
    rows, cols = w.shape
    tr = _row_tile(rows, cols)

    def body(w_ref, g_ref, m_ref, v_ref, d_ref, nm_ref, nv_ref):
        gv = g_ref[...]
        nm = ADAM_B1 * m_ref[...] + (1.0 - ADAM_B1) * gv
        nv = ADAM_B2 * v_ref[...] + (1.0 - ADAM_B2) * (gv * gv)
        m_hat = nm / (1.0 - ADAM_B1 ** ADAM_STEP)
        v_hat = nv / (1.0 - ADAM_B2 ** ADAM_STEP)
        d_ref[...] = -ADAM_LR * (m_hat / (jnp.sqrt(v_hat) + ADAM_EPS) + ADAM_WD * w_ref[...])
        nm_ref[...] = nm
        nv_ref[...] = nv

    blk = pl.BlockSpec((tr, cols), lambda i: (i, 0))
    shp = jax.ShapeDtypeStruct((rows, cols), F32)
    return pl.pallas_call(
        body, name=name, grid=(rows // tr,), in_specs=[blk] * 4, out_specs=[blk] * 3, out_shape=[shp] * 3,
        compiler_params=_cp(("parallel",)),
    )(w, g, m, v)


def _cast_rows(x, dtype, *, name):
    rows, cols = x.shape
    tr = _row_tile(rows, cols, 2 << 20)

    def body(x_ref, o_ref):
        o_ref[...] = x_ref[...].astype(o_ref.dtype)

    blk = pl.BlockSpec((tr, cols), lambda i: (i, 0))
    return pl.pallas_call(body, name=name, grid=(rows // tr,), in_specs=[blk], out_specs=blk,
                          out_shape=jax.ShapeDtypeStruct((rows, cols), dtype), compiler_params=_cp(("parallel",)))(x)


def _bd256(w):
    w4 = w.reshape(4, 4, 64, 64)
    eye = jnp.eye(4, dtype=w.dtype)
    return (w4[:, :, :, None, :] * eye[None, :, None, :, None]).reshape(4, 256, 256)


def _bd256_diag(g):
    g5 = g.reshape(4, 4, 64, 4, 64)
    return jnp.stack([g5[:, a, :, a, :] for a in range(4)], axis=1).reshape(16, 64, 64)


def _layer_weights(w, small, l):
    win = w["w_in"][l]
    lx, lg = win[:, 0:D], win[:, D:2 * D]
    lblk = [jnp.concatenate([lx[:, 256 * j:256 * (j + 1)], lg[:, 256 * j:256 * (j + 1)]], axis=1) for j in range(4)]
    wp = jnp.concatenate([win[:, 2048:4096], win[:, 4096:7168], lblk[0], lblk[1], win[:, 7200:9248], lblk[2], lblk[3]], axis=1)
    wdt = jnp.pad(win[:, 7168:7200], ((0, 0), (0, DT_PAD - N_HEADS)))
    row = lambda v: v.reshape(1, -1)
    pad_h = lambda v: jnp.pad(v.reshape(1, -1), ((0, 0), (0, DT_PAD - N_HEADS)))
    lw = dict(cw=w["lru_conv_w"][l], cb=row(small["lru_conv_b"][l]),
              wa=_bd256(small["lru_w_a"][l]).astype(_MXU), wx=_bd256(small["lru_w_x"][l]).astype(_MXU),
              ba=row(small["lru_b_a"][l]), bx=row(small["lru_b_x"][l]), lam=row(small["lru_lambda"][l]))
    sw = dict(cw=w["ssd_conv_w"][l], cb=row(small["ssd_conv_b"][l]), dtb=pad_h(small["ssd_dt_bias"][l]),
              alog=pad_h(small["ssd_A_log"][l]), dsk=row(jnp.repeat(small["ssd_D"][l], HEAD_P)),
              ng=row(small["ssd_norm_g"][l]))
    return dict(wp=wp, wdt=wdt, lw=lw, sw=sw, wba=w["w_branch"][l][0:D], wbb=w["w_branch"][l][D:3 * D],
                wout=w["w_out"][l], wfi=w["w_ffn_in"][l], wfo=w["w_ffn_out"][l],
                g1=row(small["norm1_g"][l]), g2=row(small["norm2_g"][l]), bgate=row(small["b_gate"][l]))


def _tiles(t):
    return dict(tm=min(512, t), tm2=min(256, t), r=min(256, t), rb=min(128, t))


def _layer_fwd(h, lwt, l):
    tl = _tiles(h.shape[0])
    n = f"l{l}_"
    xn, proj = _norm_mm(h, lwt["g1"], lwt["wp"], tm=tl["tm"], tn=1024, name=n + "in_proj")
    dtraw = _mm_nn(xn, lwt["wdt"], tm=tl["tm"], tn=DT_PAD, name=n + "dt_proj")
    hl, ya = _lru_fwd(proj, lwt["lw"], r=tl["r"], name=n + "lru_fwd")
    yssd, yb, states = _ssd_fwd(proj, dtraw, lwt["sw"], rb=tl["rb"], name=n + "ssd_fwd")
    ta, tb, merged = _branch_merge(ya, yb, proj, lwt["wba"], lwt["wbb"], lwt["bgate"], tm=tl["tm"], tn=512, name=n + "merge")
    hmid = _mm_nn(merged, lwt["wout"], tm=tl["tm"], tn=512, name=n + "out_proj", residual=h)
    xn2, gu = _norm_mm(hmid, lwt["g2"], lwt["wfi"], tm=tl["tm"], tn=512, name=n + "ffn_in")
    act, hout = _swiglu_mm(gu, lwt["wfo"], hmid, tm=tl["tm2"], tn=512, name=n + "ffn_out")
    saved = dict(h=h, xn=xn, proj=proj, dtraw=dtraw, hl=hl, ya=ya, yssd=yssd, yb=yb, states=states, ta=ta, tb=tb,
                 merged=merged, hmid=hmid, xn2=xn2, gu=gu, act=act)
    return hout, saved


def _layer_bwd(dh, s, lwt, l):
    t = dh.shape[0]
    tl = _tiles(t)
    n = f"l{l}_"
    tn = dict(tt=tl["tm"], tn=512)
    dgu = _ffn_bwd_act(dh, lwt["wfo"], s["gu"], tm=tl["tm2"], name=n + "ffn_act_bwd")
    dwfo = _mm_tn(s["act"], dh, name=n + "ffn_out_wgrad", **tn)
    dwfi = _mm_tn(s["xn2"], dgu, name=n + "ffn_in_wgrad", **tn)
    dh1, dg2 = _mm_nt_rmsbwd(dgu, lwt["wfi"], s["hmid"], lwt["g2"], dh, tm=tl["tm"], tk=512, name=n + "ffn_in_dgrad")
    dta, dtb, dproj, dbg = _outproj_bwd(dh1, lwt["wout"], s["ta"], s["tb"], s["proj"], lwt["bgate"], (t, NP),
                                        tm=tl["tm2"], name=n + "out_proj_bwd")
    dwout = _mm_tn(s["merged"], dh1, name=n + "out_proj_wgrad", **tn)
    dya = _mm_nt(dta, lwt["wba"], tm=tl["tm"], name=n + "branch_a_dgrad")
    dyb = _mm_nt(dtb, lwt["wbb"], tm=tl["tm"], name=n + "branch_b_dgrad")
    dwba = _mm_tn(s["ya"], dta, name=n + "branch_a_wgrad", **tn)
    dwbb = _mm_tn(s["yb"], dtb, name=n + "branch_b_wgrad", **tn)
    dproj, lsm, dwa, dwx = _lru_bwd(s["proj"], s["hl"], dya, dproj, lwt["lw"], r=tl["r"], name=n + "lru_bwd")
    dproj, ddt, gconv, gch, ghd = _ssd_bwd(s["proj"], s["dtraw"], s["yssd"], s["states"], dyb, dproj, lwt["sw"],
                                           rb=tl["rb"], name=n + "ssd_bwd")
    dwp = _mm_tn(s["xn"], dproj, tt=tl["tm"], tn=1024, name=n + "in_proj_wgrad")
    dwdt = _mm_tn(s["xn"], ddt, tt=tl["tm"], tn=DT_PAD, name=n + "dt_proj_wgrad")
    dh0, dg1 = _mm_nt_rmsbwd(dproj, lwt["wp"], s["h"], lwt["g1"], dh1, tm=tl["tm"], tk=1024, name=n + "in_proj_dgrad",
                             extra=(ddt, lwt["wdt"]))
    lcol = [LBLK * _lblk_col(j) for j in range(4)]
    dlx = jnp.concatenate([dwp[:, c:c + 256] for c in lcol], axis=1)
    dlg = jnp.concatenate([dwp[:, c + 256:c + 512] for c in lcol], axis=1)
    grads = dict(
        w_in=jnp.concatenate([dlx, dlg, dwp[:, 0:2048], dwp[:, 2048:ZX_W], dwdt[:, 0:N_HEADS], dwp[:, G0:G0 + 2048]], axis=1),
        w_branch=jnp.concatenate([dwba, dwbb], axis=0), w_out=dwout, w_ffn_in=dwfi, w_ffn_out=dwfo,
        lru_conv_w=lsm[0:4], lru_conv_b=lsm[4], lru_b_a=lsm[5], lru_b_x=lsm[6], lru_lambda=lsm[7],
        lru_w_a=_bd256_diag(dwa), lru_w_x=_bd256_diag(dwx),
        ssd_conv_w=gconv[0:4], ssd_conv_b=gconv[4], ssd_norm_g=gch[0], ssd_D=gch[1].reshape(N_HEADS, HEAD_P).sum(axis=-1),
        ssd_dt_bias=ghd[0, 0:N_HEADS], ssd_A_log=ghd[1, 0:N_HEADS],
        b_gate=dbg[0], norm1_g=dg1[0], norm2_g=dg2[0])
    return dh0, grads


def _local_step(x, target, w, small):
    h = x
    lwts, saved = [], []
    for l in range(N_LAYERS):
        lwt = _layer_weights(w, small, l)
        h, s = _layer_fwd(h, lwt, l)
        lwts.append(lwt)
        saved.append(s)
    loss_blk, dgf, dh = _loss_head(h, small["norm_f"].reshape(1, D), target, tm=_tiles(x.shape[0])["tm"], name="loss_head")
    per_layer = [None] * N_LAYERS
    for l in reversed(range(N_LAYERS)):
        dh, per_layer[l] = _layer_bwd(dh, saved[l], lwts[l], l)
    grads = {k: jnp.stack([per_layer[l][k] for l in range(N_LAYERS)], axis=0) for k in per_layer[0]}
    grads["norm_f"] = dgf[0]
    return loss_blk, dh, grads


PACK_W = 1024
BIG = (("w_in", 2), ("w_branch", 1), ("w_out", 1), ("w_ffn_in", 2), ("w_ffn_out", 1), ("lru_conv_w", 2), ("ssd_conv_w", 2))
CONV = ("lru_conv_w", "ssd_conv_w")
SMALL = ("norm1_g", "b_gate", "lru_conv_b", "lru_w_a", "lru_b_a", "lru_w_x", "lru_b_x", "lru_lambda", "ssd_conv_b",
         "ssd_dt_bias", "ssd_A_log", "ssd_D", "ssd_norm_g", "norm2_g", "norm_f")
_WIRE = jnp.bfloat16
N_CHIPS = 4
N_DEV = 8


def _rows_of(shape):
    return int(np.prod(shape)) // PACK_W


def _pack_rows(parts, total_rows):
    flat = [p.reshape(-1, PACK_W) for p in parts]
    have = sum(f.shape[0] for f in flat)
    if have < total_rows:
        flat.append(jnp.zeros((total_rows - have, PACK_W), flat[0].dtype))
    return jnp.concatenate(flat, axis=0)


def _round16(n):
    return -(-n // 32) * 32


def _mesh_pos():
    return lax.axis_index("x"), lax.axis_index("y"), lax.axis_index("c")


HBM_SPEC = pl.BlockSpec(memory_space=pltpu.HBM)


def _remote(src, dst, send_sems, recv_sems, k, to):
    return pltpu.make_async_remote_copy(src_ref=src, dst_ref=dst, send_sem=send_sems.at[k], recv_sem=recv_sems.at[k],
                                        device_id=to, device_id_type=MESH)


def _allgather_chips(wpack):
    _, rh, wd = wpack.shape

    def body(w_ref, o_ref, send_sems, recv_sems, local_sem):
        x, y, c = _mesh_pos()
        s = 2 * x + y
        sib = (x, y, 1 - c)
        chips = [(1 - x, y), (x, 1 - y), (1 - x, 1 - y)]
        own = pltpu.make_async_copy(w_ref, o_ref.at[s], local_sem)
        own.start()
        first = [_remote(w_ref.at[c], o_ref.at[s, c], send_sems, recv_sems, j, (px, py, c)) for j, (px, py) in enumerate(chips)]
        for cp in first:
            cp.start()
        passed = []
        for j, (px, py) in enumerate(chips):
            p = 2 * px + py
            _remote(w_ref.at[c], o_ref.at[p, c], send_sems, recv_sems, j, (px, py, c)).wait_recv()
            cp = _remote(o_ref.at[p, c], o_ref.at[p, c], send_sems, recv_sems, 3 + j, sib)
            cp.start()
            passed.append(cp)
        for j, (px, py) in enumerate(chips):
            p = 2 * px + py
            _remote(o_ref.at[p, 1 - c], o_ref.at[p, 1 - c], send_sems, recv_sems, 3 + j, sib).wait_recv()
        for cp in first + passed:
            cp.wait_send()
        own.wait()

    return pl.pallas_call(
        body, name="allgather_weights", in_specs=[HBM_SPEC], out_specs=HBM_SPEC,
        out_shape=jax.ShapeDtypeStruct((N_CHIPS, 2, rh, wd), wpack.dtype),
        scratch_shapes=[pltpu.SemaphoreType.DMA((6,)), pltpu.SemaphoreType.DMA((6,)), pltpu.SemaphoreType.DMA(())],
    )(wpack)


def _sibling_exchange(gpack):
    n, _, rh, wd = gpack.shape

    def body(g_ref, o_ref, send_sems, recv_sems):
        x, y, c = _mesh_pos()
        cp = _remote(g_ref.at[:, 1 - c], o_ref, send_sems, recv_sems, 0, (x, y, 1 - c))
        cp.start()
        cp.wait()

    return pl.pallas_call(
        body, name="grad_sibling_exchange", in_specs=[HBM_SPEC], out_specs=HBM_SPEC,
        out_shape=jax.ShapeDtypeStruct((n, rh, wd), gpack.dtype),
        scratch_shapes=[pltpu.SemaphoreType.DMA((1,)), pltpu.SemaphoreType.DMA((1,))],
    )(gpack)


def _add_sibling(gpack, recv, c_idx, *, tr):
    n, _, rh, wd = gpack.shape

    def body(c_ref, g_ref, r_ref, o_ref):
        del c_ref
        o_ref[...] = (g_ref[...] + r_ref[...]).astype(o_ref.dtype)

    return pl.pallas_call(
        body, name="grad_add_sibling",
        grid_spec=pltpu.PrefetchScalarGridSpec(
            num_scalar_prefetch=1, grid=(n, rh // tr),
            in_specs=[pl.BlockSpec((None, None, tr, wd), lambda p, i, c_ref: (p, c_ref[0], i, 0)),
                      pl.BlockSpec((None, tr, wd), lambda p, i, c_ref: (p, i, 0))],
            out_specs=pl.BlockSpec((None, tr, wd), lambda p, i, c_ref: (p, i, 0))),
        out_shape=jax.ShapeDtypeStruct((n, rh, wd), _WIRE),
        compiler_params=_cp(("parallel", "parallel")),
    )(c_idx, gpack, recv)


def _chip_exchange(part):
    n, rh, wd = part.shape

    def body(s_ref, o_ref, send_sems, recv_sems, local_sem):
        x, y, c = _mesh_pos()
        s = 2 * x + y
        chips = [(1 - x, y), (x, 1 - y), (1 - x, 1 - y)]
        own = pltpu.make_async_copy(s_ref.at[s], o_ref.at[s], local_sem)
        own.start()
        sends = [_remote(s_ref.at[2 * px + py], o_ref.at[s], send_sems, recv_sems, j, (px, py, c))
                 for j, (px, py) in enumerate(chips)]
        for cp in sends:
            cp.start()
        for j, (px, py) in enumerate(chips):
            p = 2 * px + py
            _remote(s_ref.at[p], o_ref.at[p], send_sems, recv_sems, j, (px, py, c)).wait_recv()
        for cp in sends:
            cp.wait_send()
        own.wait()

    return pl.pallas_call(
        body, name="grad_chip_exchange", in_specs=[HBM_SPEC], out_specs=HBM_SPEC,
        out_shape=jax.ShapeDtypeStruct((n, rh, wd), part.dtype),
        scratch_shapes=[pltpu.SemaphoreType.DMA((3,)), pltpu.SemaphoreType.DMA((3,)), pltpu.SemaphoreType.DMA(())],
    )(part)


def _sum_slots(slots, *, tr, name):
    n, rows, wd = slots.shape

    def body(s_ref, o_ref):
        acc = s_ref[0].astype(F32)
        for p in range(1, n):
            acc = acc + s_ref[p].astype(F32)
        o_ref[...] = acc

    return pl.pallas_call(
        body, name=name, grid=(rows // tr,),
        in_specs=[pl.BlockSpec((n, tr, wd), lambda i: (0, i, 0))], out_specs=pl.BlockSpec((tr, wd), lambda i: (i, 0)),
        out_shape=jax.ShapeDtypeStruct((rows, wd), F32), compiler_params=_cp(("parallel",)),
    )(slots)


def _sibling_share(half):
    rh, wd = half.shape

    def body(h_ref, o_ref, send_sems, recv_sems, local_sem):
        x, y, c = _mesh_pos()
        own = pltpu.make_async_copy(h_ref, o_ref.at[c], local_sem)
        own.start()
        cp = _remote(h_ref, o_ref.at[c], send_sems, recv_sems, 0, (x, y, 1 - c))
        cp.start()
        _remote(h_ref, o_ref.at[1 - c], send_sems, recv_sems, 0, (x, y, 1 - c)).wait_recv()
        cp.wait_send()
        own.wait()

    return pl.pallas_call(
        body, name="grad_sibling_share", in_specs=[HBM_SPEC], out_specs=HBM_SPEC,
        out_shape=jax.ShapeDtypeStruct((2, rh, wd), half.dtype),
        scratch_shapes=[pltpu.SemaphoreType.DMA((1,)), pltpu.SemaphoreType.DMA((1,)), pltpu.SemaphoreType.DMA(())],
    )(half)


def _allgather_devices(part):
    rows, wd = part.shape

    def body(p_ref, o_ref, send_sems, recv_sems, local_sem):
        x, y, c = _mesh_pos()
        me = 4 * x + 2 * y + c
        own = pltpu.make_async_copy(p_ref, o_ref.at[me], local_sem)
        own.start()
        peers = []
        for k in range(1, N_DEV):
            dx, dy, dc = (k >> 2) & 1, (k >> 1) & 1, k & 1
            peers.append(((x + dx) % 2, (y + dy) % 2, (c + dc) % 2))
        sends = [_remote(p_ref, o_ref.at[me], send_sems, recv_sems, k, to) for k, to in enumerate(peers)]
        for cp in sends:
            cp.start()
        for k, (px, py, pc) in enumerate(peers):
            _remote(p_ref, o_ref.at[4 * px + 2 * py + pc], send_sems, recv_sems, k, (px, py, pc)).wait_recv()
        for cp in sends:
            cp.wait_send()
        own.wait()

    return pl.pallas_call(
        body, name="small_grad_allgather", in_specs=[HBM_SPEC], out_specs=HBM_SPEC,
        out_shape=jax.ShapeDtypeStruct((N_DEV, rows, wd), part.dtype),
        scratch_shapes=[pltpu.SemaphoreType.DMA((N_DEV - 1,)), pltpu.SemaphoreType.DMA((N_DEV - 1,)),
                        pltpu.SemaphoreType.DMA(())],
    )(part)


def _shard_to_full(stack, axis):
    _, nl, a, b = stack.shape
    if axis == 2:
        return stack.transpose(1, 2, 0, 3).reshape(nl, a, N_CHIPS * b)
    return stack.transpose(1, 0, 2, 3).reshape(nl, N_CHIPS * a, b)


def _full_to_shards(full, axis):
    nl, a, b = full.shape
    if axis == 2:
        return full.reshape(nl, a, N_CHIPS, b // N_CHIPS).transpose(2, 0, 1, 3)
    return full.reshape(nl, N_CHIPS, a // N_CHIPS, b).transpose(1, 0, 2, 3)


def _sharded_step(a):
    x = a["x"][0]
    target = a["loss_target"][0]
    cx, cy, cc = _mesh_pos()
    shard_shapes = {n: a[n].shape for n, _ in BIG}

    parts = []
    for n, _ in BIG:
        if n in CONV:
            hi = a[n].astype(_MXU).astype(F32)
            parts += [hi, a[n] - hi]
        else:
            parts.append(a[n])
    total = _round16(sum(_rows_of(p.shape) for p in parts))
    rh = total // 2
    wpack = _cast_rows(_pack_rows(parts, total), _MXU, name="pack_weights_cast").reshape(2, rh, PACK_W)
    allw = _allgather_chips(wpack).reshape(N_CHIPS, total, PACK_W)
    w, off = {}, 0
    for n, ax in BIG:
        rows = _rows_of(shard_shapes[n])
        piece = lambda o: allw[:, o:o + rows].reshape((N_CHIPS,) + shard_shapes[n])
        if n in CONV:
            w[n] = _shard_to_full(piece(off).astype(F32) + piece(off + rows).astype(F32), ax)
            off += 2 * rows
        else:
            w[n] = _shard_to_full(piece(off), ax)
            off += rows
    small = {n: a[n] for n in SMALL}

    loss_blk, grad_x, grads = _local_step(x, target, w, small)
    loss = lax.psum(loss_blk[0, 0], ("x", "y", "c"))

    gtotal = _round16(sum(_rows_of(shard_shapes[n]) for n, _ in BIG))
    grh = gtotal // 2
    gparts = [_full_to_shards(grads[n], ax).reshape(N_CHIPS, -1, PACK_W) for n, ax in BIG]
    have = sum(g.shape[1] for g in gparts)
    if have < gtotal:
        gparts.append(jnp.zeros((N_CHIPS, gtotal - have, PACK_W), F32))
    gpack = jnp.concatenate(gparts, axis=1).reshape(N_CHIPS, 2, grh, PACK_W)
    tr = _row_tile(grh, PACK_W, 2 << 20)
    recv_c = _sibling_exchange(gpack)
    part = _add_sibling(gpack, recv_c, cc.astype(jnp.int32).reshape(1), tr=tr)
    slots = _chip_exchange(part)
    half = _sum_slots(slots, tr=tr, name="grad_sum_chips")
    gshard = _sibling_share(half).reshape(gtotal, PACK_W)
    g_big, off = {}, 0
    for n, _ in BIG:
        rows = _rows_of(shard_shapes[n])
        g_big[n] = gshard[off:off + rows].reshape(shard_shapes[n])
        off += rows

    srows = -(-sum(int(np.prod(a[n].shape)) for n in SMALL) // (8 * PACK_W)) * 8
    flat =lambda d: jnp.concatenate([d[n].reshape(-1) for n in SMALL])
    padto = lambda v: jnp.pad(v, (0, srows * PACK_W - v.shape[0])).reshape(srows, PACK_W)
    g_small_pack = _sum_slots(_allgather_devices(padto(flat(grads))), tr=srows, name="small_grad_sum")

    out_g, out_d, out_m, out_v = {}, {}, {}, {}
    for n, _ in BIG:
        shp = shard_shapes[n]
        two_d = (shp[0] * shp[1], shp[2])
        d_, m_, v_ = _adamw(a[n].reshape(two_d), g_big[n].reshape(two_d), a["m_" + n].reshape(two_d),
                            a["v_" + n].reshape(two_d), name="adamw_" + n)
        out_g[n], out_d[n], out_m[n], out_v[n] = g_big[n], d_.reshape(shp), m_.reshape(shp), v_.reshape(shp)
    d_, m_, v_ = _adamw(padto(flat(a)), g_small_pack, padto(flat({n: a["m_" + n] for n in SMALL})),
                        padto(flat({n: a["v_" + n] for n in SMALL})), name="adamw_small")
    off = 0
    for n in SMALL:
        size = int(np.prod(a[n].shape))
        cut = lambda v: v.reshape(-1)[off:off + size].reshape(a[n].shape)
        out_g[n], out_d[n], out_m[n], out_v[n] = cut(g_small_pack), cut(d_), cut(m_), cut(v_)
        off += size
    return loss, grad_x[None], out_g, out_d, out_m, out_v


WEIGHTS = ("norm1_g", "w_in", "b_gate", "lru_conv_w", "lru_conv_b", "lru_w_a", "lru_b_a", "lru_w_x", "lru_b_x", "lru_lambda",
           "ssd_conv_w", "ssd_conv_b", "ssd_dt_bias", "ssd_A_log", "ssd_D", "ssd_norm_g", "w_branch", "w_out", "norm2_g",
           "w_ffn_in", "w_ffn_out", "norm_f")
INPUTS = ("x",) + WEIGHTS + ("loss_target",) + tuple("m_" + n for n in WEIGHTS) + tuple("v_" + n for n in WEIGHTS)


def kernel(x, norm1_g, w_in, b_gate, lru_conv_w, lru_conv_b, lru_w_a, lru_b_a, lru_w_x, lru_b_x, lru_lambda, ssd_conv_w, ssd_conv_b, ssd_dt_bias, ssd_A_log, ssd_D, ssd_norm_g, w_branch, w_out, norm2_g, w_ffn_in, w_ffn_out, norm_f, loss_target, m_norm1_g, m_w_in, m_b_gate, m_lru_conv_w, m_lru_conv_b, m_lru_w_a, m_lru_b_a, m_lru_w_x, m_lru_b_x, m_lru_lambda, m_ssd_conv_w, m_ssd_conv_b, m_ssd_dt_bias, m_ssd_A_log, m_ssd_D, m_ssd_norm_g, m_w_branch, m_w_out, m_norm2_g, m_w_ffn_in, m_w_ffn_out, m_norm_f, v_norm1_g, v_w_in, v_b_gate, v_lru_conv_w, v_lru_conv_b, v_lru_w_a, v_lru_b_a, v_lru_w_x, v_lru_b_x, v_lru_lambda, v_ssd_conv_w, v_ssd_conv_b, v_ssd_dt_bias, v_ssd_A_log, v_ssd_D, v_ssd_norm_g, v_w_branch, v_w_out, v_norm2_g, v_w_ffn_in, v_w_ffn_out, v_norm_f):
    args = (x, norm1_g, w_in, b_gate, lru_conv_w, lru_conv_b, lru_w_a, lru_b_a, lru_w_x, lru_b_x, lru_lambda, ssd_conv_w, ssd_conv_b, ssd_dt_bias, ssd_A_log, ssd_D, ssd_norm_g, w_branch, w_out, norm2_g, w_ffn_in, w_ffn_out, norm_f, loss_target, m_norm1_g, m_w_in, m_b_gate, m_lru_conv_w, m_lru_conv_b, m_lru_w_a, m_lru_b_a, m_lru_w_x, m_lru_b_x, m_lru_lambda, m_ssd_conv_w, m_ssd_conv_b, m_ssd_dt_bias, m_ssd_A_log, m_ssd_D, m_ssd_norm_g, m_w_branch, m_w_out, m_norm2_g, m_w_ffn_in, m_w_ffn_out, m_norm_f, v_norm1_g, v_w_in, v_b_gate, v_lru_conv_w, v_lru_conv_b, v_lru_w_a, v_lru_b_a, v_lru_w_x, v_lru_b_x, v_lru_lambda, v_ssd_conv_w, v_ssd_conv_b, v_ssd_dt_bias, v_ssd_A_log, v_ssd_D, v_ssd_norm_g, v_w_branch, v_w_out, v_norm2_g, v_w_ffn_in, v_w_ffn_out, v_norm_f)
    assert len(args) == len(INPUTS)
    loss, grad_x, g, d, m, v = _sharded_step(dict(zip(INPUTS, args)))
    return (loss, grad_x, *[g[n] for n in WEIGHTS], *[d[n] for n in WEIGHTS], *[m[n] for n in WEIGHTS],
            *[v[n] for n in WEIGHTS])
```

```python
import functools
import math

import numpy as np
import jax
import jax.numpy as jnp
from jax import lax
from jax.experimental import pallas as pl
from jax.experimental.pallas import tpu as pltpu

F32 = jnp.float32
BF16 = jnp.bfloat16
_MXU = jnp.bfloat16
_HI = lax.Precision.HIGHEST

D = 1024
EPS = 1e-6
N_LAYERS = 2
LRU_C = 8.0
N_HEADS = 32
HEAD_P = 64
N_GROUPS = 4
N_STATE = 128
SSD_INNER = 2048
XBC = 3072
D_FF = 2816
CHUNK = 64
NORM_ROWS = 32
IN_DIM = 9248

NP = 9216
ZX_W = 5120
G0 = 6144
LBLK = 512
DT_PAD = 128

VMEM_LIMIT_BYTES_V7X = 56 * 1024 * 1024

ADAM_LR, ADAM_B1, ADAM_B2, ADAM_EPS, ADAM_WD, ADAM_STEP = 0.001, 0.9, 0.999, 1e-08, 0.01, 10
MESH = pl.DeviceIdType.MESH


def _cp(sem):
    return pltpu.CompilerParams(dimension_semantics=sem, vmem_limit_bytes=VMEM_LIMIT_BYTES_V7X)


def _lblk_col(j):
    return 10 + j + 4 * (j // 2)


def _sigmoid(x):
    return 0.5 * jnp.tanh(0.5 * x) + 0.5


def _softplus(x):
    return jnp.maximum(x, 0.0) + jnp.log(1.0 + jnp.exp(-jnp.abs(x)))


def _silu(x):
    return x * _sigmoid(x)


def _dsilu(x):
    s = _sigmoid(x)
    return s * (1.0 + x * (1.0 - s))


_GELU_C0 = math.sqrt(2.0 / math.pi)
_GELU_C1 = 0.044715


def _gelu_and_grad(x):
    t = jnp.tanh(_GELU_C0 * (x + _GELU_C1 * x * x * x))
    g = 0.5 * x * (1.0 + t)
    dg = 0.5 * (1.0 + t) + 0.5 * x * (1.0 - t * t) * _GELU_C0 * (1.0 + 3.0 * _GELU_C1 * x * x)
    return g, dg


def _one_minus_exp(x):
    p = 1.0 + x * (1.0 / 7.0)
    p = 1.0 + x * (1.0 / 6.0) * p
    p = 1.0 + x * (1.0 / 5.0) * p
    p = 1.0 + x * (1.0 / 4.0) * p
    p = 1.0 + x * (1.0 / 3.0) * p
    p = 1.0 + x * (1.0 / 2.0) * p
    return jnp.where(x > -0.3, -x * p, 1.0 - jnp.exp(x))


def _dot(a, b):
    return jnp.dot(a.astype(_MXU), b.astype(_MXU), preferred_element_type=F32)


def _dot_nt(a, b):
    return lax.dot_general(a.astype(_MXU), b.astype(_MXU), (((1,), (1,)), ((), ())), preferred_element_type=F32)


def _dot_tn(a, b):
    return lax.dot_general(a.astype(_MXU), b.astype(_MXU), (((0,), (0,)), ((), ())), preferred_element_type=F32)


def _shift_down(x, prev8, k):
    xr = pltpu.roll(x, k, 0)
    pr = pltpu.roll(prev8, k, 0)
    row = lax.broadcasted_iota(jnp.int32, prev8.shape, 0)
    head = jnp.where(row < k, pr, xr[0:8])
    return jnp.concatenate([head, xr[8:]], axis=0)


def _shift_up(x, next8, k):
    r = x.shape[0]
    xr = pltpu.roll(x, r - k, 0)
    nr = pltpu.roll(next8, 8 - k, 0)
    row = lax.broadcasted_iota(jnp.int32, next8.shape, 0)
    tail = jnp.where(row >= 8 - k, nr, xr[r - 8:r])
    return jnp.concatenate([xr[:r - 8], tail], axis=0)


def _conv4(x, prev8, w_ref, b_ref, cols=slice(None)):
    acc = x * w_ref[3:4, cols] + b_ref[0:1, cols]
    for k in (1, 2, 3):
        acc = acc + _shift_down(x, prev8, k) * w_ref[3 - k:4 - k, cols]
    return acc


def _conv4_bwd_x(dy, next8, w_ref, cols=slice(None)):
    acc = dy * w_ref[3:4, cols]
    for k in (1, 2, 3):
        acc = acc + _shift_up(dy, next8, k) * w_ref[3 - k:4 - k, cols]
    return acc


def _lin_scan(a, b, reverse):
    r = a.shape[0]
    row = lax.broadcasted_iota(jnp.int32, a.shape, 0)
    d = 1
    while d < r:
        sh = (r - d) if reverse else d
        a_s = pltpu.roll(a, sh, 0)
        b_s = pltpu.roll(b, sh, 0)
        m = (row < r - d) if reverse else (row >= d)
        b = jnp.where(m, a * b_s + b, b)
        a = jnp.where(m, a * a_s, a)
        d *= 2
    return a, b


def _rsum(x):
    return jnp.sum(x, axis=0, keepdims=True)


def _norm_mm(h, gamma, w, *, tm, tn, name):
    m, k = h.shape
    n = w.shape[1]

    def body(h_ref, g_ref, w_ref, xn_ref, o_ref):
        @pl.when(pl.program_id(1) == 0)
        def _():
            x = h_ref[...]
            r = lax.rsqrt(jnp.mean(x * x, axis=-1, keepdims=True) + EPS)
            xn_ref[...] = ((x * r) * g_ref[...]).astype(xn_ref.dtype)
        o_ref[...] = jnp.dot(xn_ref[...], w_ref[...], preferred_element_type=F32)

    return pl.pallas_call(
        body, name=name, grid=(m // tm, n // tn),
        in_specs=[pl.BlockSpec((tm, k), lambda i, j: (i, 0)), pl.BlockSpec((1, k), lambda i, j: (0, 0)),
                  pl.BlockSpec((k, tn), lambda i, j: (0, j))],
        out_specs=[pl.BlockSpec((tm, k), lambda i, j: (i, 0)), pl.BlockSpec((tm, tn), lambda i, j: (i, j))],
        out_shape=[jax.ShapeDtypeStruct((m, k), _MXU), jax.ShapeDtypeStruct((m, n), F32)],
        compiler_params=_cp(("parallel", "arbitrary")),
    )(h, gamma, w)


def _mm_nn(a, w, *, tm, tn, name, residual=None):
    m, k = a.shape
    n = w.shape[1]

    def body(*refs):
        if residual is None:
            a_ref, w_ref, o_ref = refs
            o_ref[...] = _dot(a_ref[...], w_ref[...])
        else:
            a_ref, w_ref, r_ref, o_ref = refs
            o_ref[...] = _dot(a_ref[...], w_ref[...]) + r_ref[...]

    in_specs = [pl.BlockSpec((tm, k), lambda i, j: (i, 0)), pl.BlockSpec((k, tn), lambda i, j: (0, j))]
    args = [a, w]
    if residual is not None:
        in_specs.append(pl.BlockSpec((tm, tn), lambda i, j: (i, j)))
        args.append(residual)
    return pl.pallas_call(
        body, name=name, grid=(m // tm, n // tn), in_specs=in_specs,
        out_specs=pl.BlockSpec((tm, tn), lambda i, j: (i, j)),
        out_shape=jax.ShapeDtypeStruct((m, n), F32),
        compiler_params=_cp(("parallel", "parallel")),
    )(*args)


def _mm_tn(a, b, *, tt, tn, name):
    t, ka = a.shape
    nb = b.shape[1]

    def body(a_ref, b_ref, o_ref):
        @pl.when(pl.program_id(1) == 0)
        def _():
            o_ref[...] = jnp.zeros_like(o_ref)
        o_ref[...] += _dot_tn(a_ref[...], b_ref[...])

    return pl.pallas_call(
        body, name=name, grid=(nb // tn, t // tt),
        in_specs=[pl.BlockSpec((tt, ka), lambda j, i: (i, 0)), pl.BlockSpec((tt, tn), lambda j, i: (i, j))],
        out_specs=pl.BlockSpec((ka, tn), lambda j, i: (0, j)),
        out_shape=jax.ShapeDtypeStruct((ka, nb), F32),
        compiler_params=_cp(("parallel", "arbitrary")),
    )(a, b)


def _mm_nt(a, w, *, tm, name):
    m, kc = a.shape
    n = w.shape[0]

    def body(a_ref, w_ref, o_ref):
        o_ref[...] = _dot_nt(a_ref[...], w_ref[...])

    return pl.pallas_call(
        body, name=name, grid=(m // tm,),
        in_specs=[pl.BlockSpec((tm, kc), lambda i: (i, 0)), pl.BlockSpec((n, kc), lambda i: (0, 0))],
        out_specs=pl.BlockSpec((tm, n), lambda i: (i, 0)),
        out_shape=jax.ShapeDtypeStruct((m, n), F32),
        compiler_params=_cp(("parallel",)),
    )(a, w)


def _mm_nt_rmsbwd(dy, w, x, gamma, dres, *, tm, tk, name, extra=None):
    m, kc = dy.shape
    d = w.shape[0]
    nk = kc // tk

    def body(*refs):
        if extra is None:
            dy_ref, w_ref, x_ref, g_ref, r_ref, dx_ref, dg_ref, acc_ref = refs
        else:
            dy_ref, w_ref, x_ref, g_ref, r_ref, dy2_ref, w2_ref, dx_ref, dg_ref, acc_ref = refs
        i, kk = pl.program_id(0), pl.program_id(1)

        @pl.when(kk == 0)
        def _():
            acc_ref[...] = jnp.zeros_like(acc_ref)

        @pl.when((i == 0) & (kk == 0))
        def _():
            dg_ref[...] = jnp.zeros_like(dg_ref)

        acc_ref[...] += _dot_nt(dy_ref[...], w_ref[...])

        @pl.when(kk == nk - 1)
        def _():
            dxn = acc_ref[...]
            if extra is not None:
                dxn = dxn + _dot_nt(dy2_ref[...], w2_ref[...])
            xv = x_ref[...]
            r = lax.rsqrt(jnp.mean(xv * xv, axis=-1, keepdims=True) + EPS)
            xh = xv * r
            dg_ref[0:1, :] += _rsum(dxn * xh)
            dxh = dxn * g_ref[...]
            dx_ref[...] = r_ref[...] + r * (dxh - xh * jnp.mean(dxh * xh, axis=-1, keepdims=True))

    in_specs = [pl.BlockSpec((tm, tk), lambda i, k: (i, k)), pl.BlockSpec((d, tk), lambda i, k: (0, k)),
                pl.BlockSpec((tm, d), lambda i, k: (i, 0)), pl.BlockSpec((1, d), lambda i, k: (0, 0)),
                pl.BlockSpec((tm, d), lambda i, k: (i, 0))]
    args = [dy, w, x, gamma, dres]
    if extra is not None:
        k2 = extra[0].shape[1]
        in_specs += [pl.BlockSpec((tm, k2), lambda i, k: (i, 0)), pl.BlockSpec((d, k2), lambda i, k: (0, 0))]
        args += list(extra)
    return pl.pallas_call(
        body, name=name, grid=(m // tm, nk), in_specs=in_specs,
        out_specs=[pl.BlockSpec((tm, d), lambda i, k: (i, 0)), pl.BlockSpec((8, d), lambda i, k: (0, 0))],
        out_shape=[jax.ShapeDtypeStruct((m, d), F32), jax.ShapeDtypeStruct((8, d), F32)],
        scratch_shapes=[pltpu.VMEM((tm, d), F32)],
        compiler_params=_cp(("arbitrary", "arbitrary")),
    )(*args)


def _lru_gates(x, prev8, cw_ref, cb_ref, wa_ref, wx_ref, ba_ref, bx_ref, lam_ref):
    u = _conv4(x, prev8, cw_ref, cb_ref)
    ra = _sigmoid(_dot(u, wa_ref[0]) + ba_ref[...])
    ia = _sigmoid(_dot(u, wx_ref[0]) + bx_ref[...])
    sp = _softplus(-lam_ref[...])
    log_a = -LRU_C * ra * sp
    a = jnp.exp(log_a)
    m2 = _one_minus_exp(2.0 * log_a)
    mult = jnp.sqrt(m2)
    return u, ra, ia, sp, a, m2, mult


def _lru_fwd(proj, lw, *, r, name):
    t = proj.shape[0]
    nt = t // r

    def body(xg_ref, xp_ref, cw_ref, cb_ref, wa_ref, wx_ref, ba_ref, bx_ref, lam_ref, hl_ref, ya_ref, carry_ref):
        i = pl.program_id(1)

        @pl.when(i == 0)
        def _():
            carry_ref[...] = jnp.zeros_like(carry_ref)

        x = xg_ref[:, 0:256]
        lg = xg_ref[:, 256:512]
        prev8 = jnp.where(i == 0, 0.0, xp_ref[:, 0:256])
        u, ra, ia, sp, a, m2, mult = _lru_gates(x, prev8, cw_ref, cb_ref, wa_ref, wx_ref, ba_ref, bx_ref, lam_ref)
        ac, hc = _lin_scan(a, mult * ia * u, False)
        h = hc + ac * carry_ref[0:1, :]
        hl_ref[...] = h
        carry_ref[0:1, :] = hl_ref[r - 1:r, :]
        g, _ = _gelu_and_grad(lg)
        ya_ref[...] = (g * h).astype(ya_ref.dtype)

    small = lambda rows: pl.BlockSpec((rows, 256), lambda j, i: (0, j))
    return pl.pallas_call(
        body, name=name, grid=(4, nt),
        in_specs=[pl.BlockSpec((r, LBLK), lambda j, i: (i, _lblk_col(j))),
                  pl.BlockSpec((8, LBLK), lambda j, i: (jnp.maximum(i * (r // 8) - 1, 0), _lblk_col(j))),
                  small(4), small(1),
                  pl.BlockSpec((1, 256, 256), lambda j, i: (j, 0, 0)), pl.BlockSpec((1, 256, 256), lambda j, i: (j, 0, 0)),
                  small(1), small(1), small(1)],
        out_specs=[pl.BlockSpec((r, 256), lambda j, i: (i, j)), pl.BlockSpec((r, 256), lambda j, i: (i, j))],
        out_shape=[jax.ShapeDtypeStruct((t, D), F32), jax.ShapeDtypeStruct((t, D), _MXU)],
        scratch_shapes=[pltpu.VMEM((8, 256), F32)],
        compiler_params=_cp(("parallel", "arbitrary")),
    )(proj, proj, lw["cw"], lw["cb"], lw["wa"], lw["wx"], lw["ba"], lw["bx"], lw["lam"])


def _lru_bwd(proj, hl, dya, dproj, lw, *, r, name):
    t = proj.shape[0]
    nt = t // r

    def body(xg_ref, xp_ref, hl_ref, hp_ref, dya_ref, cw_ref, cb_ref, wa_ref, wx_ref, ba_ref, bx_ref, lam_ref, dproj_in,
             dproj_ref, sm_ref, dwa_ref, dwx_ref, carry_ref, du8_ref, lam_scr, a_scr, du_scr):
        del dproj_in
        i = pl.program_id(1)
        first = i == 0

        @pl.when(first)
        def _():
            carry_ref[...] = jnp.zeros_like(carry_ref)
            du8_ref[...] = jnp.zeros_like(du8_ref)
            sm_ref[...] = jnp.zeros_like(sm_ref)
            dwa_ref[...] = jnp.zeros_like(dwa_ref)
            dwx_ref[...] = jnp.zeros_like(dwx_ref)

        x = xg_ref[:, 0:256]
        lg = xg_ref[:, 256:512]
        tile0 = i == nt - 1
        prev8 = jnp.where(tile0, 0.0, xp_ref[:, 0:256])
        hprev8 = jnp.where(tile0, 0.0, hp_ref[...])
        u, ra, ia, sp, a, m2, mult = _lru_gates(x, prev8, cw_ref, cb_ref, wa_ref, wx_ref, ba_ref, bx_ref, lam_ref)
        h = hl_ref[...]
        hprev = _shift_down(h, hprev8, 1)
        dya_v = dya_ref[...]
        g, dg = _gelu_and_grad(lg)
        dh = dya_v * g
        dlg = dya_v * h * dg
        a_next = _shift_up(a, carry_ref[...], 1)
        ac, lc = _lin_scan(a_next, dh, True)
        lam_v = lc + ac * carry_ref[1:2, :]
        lam_scr[...] = lam_v
        a_scr[...] = a
        carry_ref[1:2, :] = lam_scr[0:1, :]
        carry_ref[0:1, :] = a_scr[0:1, :]
        da = lam_v * hprev
        dmult = lam_v * ia * u
        dia = lam_v * mult * u
        du = lam_v * mult * ia
        dlog = da * a - dmult * (1.0 - m2) / mult
        dra = -LRU_C * sp * dlog
        dsp = _rsum(-LRU_C * ra * dlog)
        dpa = dra * ra * (1.0 - ra)
        dpx = dia * ia * (1.0 - ia)
        du = du + _dot_nt(dpa, wa_ref[0]) + _dot_nt(dpx, wx_ref[0])
        dwa_ref[0] += _dot_tn(u, dpa)
        dwx_ref[0] += _dot_tn(u, dpx)
        dlx = _conv4_bwd_x(du, du8_ref[...], cw_ref)
        du_scr[...] = du
        du8_ref[...] = du_scr[0:8, :]
        dproj_ref[:, 0:256] = dlx.astype(dproj_ref.dtype)
        dproj_ref[:, 256:512] = dlg.astype(dproj_ref.dtype)
        sm_ref[3:4, :] += _rsum(du * x)
        for k in (1, 2, 3):
            sm_ref[3 - k:4 - k, :] += _rsum(du * _shift_down(x, prev8, k))
        sm_ref[4:5, :] += _rsum(du)
        sm_ref[5:6, :] += _rsum(dpa)
        sm_ref[6:7, :] += _rsum(dpx)
        sm_ref[7:8, :] += dsp * (-_sigmoid(-lam_ref[...]))

    rev = lambda i: nt - 1 - i
    small = lambda rows: pl.BlockSpec((rows, 256), lambda j, i: (0, j))
    wblk = pl.BlockSpec((1, 256, 256), lambda j, i: (j, 0, 0))
    n_in = 13
    return pl.pallas_call(
        body, name=name, grid=(4, nt),
        in_specs=[pl.BlockSpec((r, LBLK), lambda j, i: (rev(i), _lblk_col(j))),
                  pl.BlockSpec((8, LBLK), lambda j, i: (jnp.maximum(rev(i) * (r // 8) - 1, 0), _lblk_col(j))),
                  pl.BlockSpec((r, 256), lambda j, i: (rev(i), j)),
                  pl.BlockSpec((8, 256), lambda j, i: (jnp.maximum(rev(i) * (r // 8) - 1, 0), j)),
                  pl.BlockSpec((r, 256), lambda j, i: (rev(i), j)),
                  small(4), small(1), wblk, wblk, small(1), small(1), small(1),
                  pl.BlockSpec(memory_space=pl.ANY)],
        out_specs=[pl.BlockSpec((r, LBLK), lambda j, i: (rev(i), _lblk_col(j))),
                   pl.BlockSpec((8, 256), lambda j, i: (0, j)), wblk, wblk],
        out_shape=[jax.ShapeDtypeStruct(dproj.shape, dproj.dtype), jax.ShapeDtypeStruct((8, D), F32),
                   jax.ShapeDtypeStruct((4, 256, 256), F32), jax.ShapeDtypeStruct((4, 256, 256), F32)],
        scratch_shapes=[pltpu.VMEM((8, 256), F32), pltpu.VMEM((8, 256), F32), pltpu.VMEM((r, 256), F32),
                        pltpu.VMEM((r, 256), F32), pltpu.VMEM((r, 256), F32)],
        input_output_aliases={n_in - 1: 0},
        compiler_params=_cp(("parallel", "arbitrary")),
    )(proj, proj, hl, hl, dya, lw["cw"], lw["cb"], lw["wa"], lw["wx"], lw["ba"], lw["bx"], lw["lam"], dproj)


def _head_cols(x):
    lane = lax.broadcasted_iota(jnp.int32, x.shape, 1)
    return [jnp.sum(jnp.where(lane == h, x, 0.0), axis=1, keepdims=True) for h in range(N_HEADS)]


def _compact_heads(blocks):
    lane = lax.broadcasted_iota(jnp.int32, blocks[0].shape, 1)
    lo = lane < HEAD_P
    out = jnp.zeros_like(blocks[0])
    for j, blk in enumerate(blocks):
        s_lo = jnp.sum(jnp.where(lo, blk, 0.0), axis=1, keepdims=True)
        s_hi = jnp.sum(jnp.where(lo, 0.0, blk), axis=1, keepdims=True)
        out = jnp.where(lane == 2 * j, s_lo, out)
        out = jnp.where(lane == 2 * j + 1, s_hi, out)
    return out


def _ssd_prelude(dtraw_ref, dtb_ref, alog_ref, dt_scr, a_scr):
    lane = lax.broadcasted_iota(jnp.int32, dt_scr.shape, 1)
    dt = jnp.where(lane < N_HEADS, _softplus(dtraw_ref[...] + dtb_ref[0:1, :]), 0.0)
    dt_scr[...] = dt
    a_scr[...] = dt * (-jnp.exp(alog_ref[0:1, :]))


def _ssd_chunk_scalars(dt_scr, a_scr, r_scr, r0):
    a_c = a_scr[pl.ds(r0, CHUNK), :]
    dt_c = dt_scr[pl.ds(r0, CHUNK), :]
    i0 = lax.broadcasted_iota(jnp.int32, (CHUNK, CHUNK), 0)
    i1 = lax.broadcasted_iota(jnp.int32, (CHUNK, CHUNK), 1)
    tri = jnp.where(i0 >= i1, 1.0, 0.0).astype(F32)
    cs = jnp.dot(tri, a_c, precision=_HI, preferred_element_type=F32)
    lane = lax.broadcasted_iota(jnp.int32, (CHUNK, 128), 1)
    srow = lax.broadcasted_iota(jnp.int32, (CHUNK, 128), 0)
    t_lo = jnp.where((lane < HEAD_P) & (srow <= lane), 1.0, 0.0).astype(F32)
    t_hi = jnp.where((lane >= HEAD_P) & (srow <= lane - HEAD_P), 1.0, 0.0).astype(F32)
    even = (lane % 2) == 0
    tn = (((0,), (0,)), ((), ()))
    r_scr[...] = (lax.dot_general(jnp.where(even, a_c, 0.0), t_lo, tn, precision=_HI, preferred_element_type=F32)
                  + lax.dot_general(jnp.where(even, 0.0, a_c), t_hi, tn, precision=_HI, preferred_element_type=F32))
    return cs, dt_c, _head_cols(cs), _head_cols(dt_c)


def _block_diag2(v):
    lo = lax.broadcasted_iota(jnp.int32, v.shape, 1) < HEAD_P
    return jnp.concatenate([jnp.where(lo, v, 0.0), jnp.where(lo, 0.0, v)], axis=0).astype(_MXU)


def _ssd_pair(xc_scr, r_scr, cs_cols, dt_cols, s2, r0, j, s2t=None):
    lane = lax.broadcasted_iota(jnp.int32, (CHUNK, 128), 1)
    srow = lax.broadcasted_iota(jnp.int32, (CHUNK, 128), 0)
    lo = lane < HEAD_P
    csc = jnp.where(lo, cs_cols[2 * j], cs_cols[2 * j + 1])
    dtc = jnp.where(lo, dt_cols[2 * j], dt_cols[2 * j + 1])
    csr = r_scr[2 * j:2 * j + 1, :] + r_scr[2 * j + 1:2 * j + 2, :]
    dm = jnp.where((lane & (HEAD_P - 1)) <= srow, jnp.exp(jnp.minimum(csc - csr, 0.0)), 0.0)
    xs = xc_scr[pl.ds(r0, CHUNK), j * 128:(j + 1) * 128]
    xd = xs * dtc
    csl = jnp.sum(jnp.where(srow == CHUNK - 1, csc, 0.0), axis=0, keepdims=True)
    out = dict(csc=csc, dtc=dtc, dm=dm, m2=s2 * dm, xs=xs, xd=xd, rhs=_block_diag2(xd), e=jnp.exp(csc),
               w=jnp.exp(csl - csc), dec=jnp.exp(csl))
    if s2t is not None:
        out["mt2"] = s2t * jnp.where((lane & (HEAD_P - 1)) >= srow, jnp.exp(jnp.minimum(csr - csc, 0.0)), 0.0)
    return out


def _cat(parts):
    return jnp.concatenate(parts, axis=1)


def _ssd_fwd(proj, dtraw, sw, *, rb, name):
    t = proj.shape[0]
    ns, cb = t // rb, rb // CHUNK

    def body(zx_ref, zp_ref, dtraw_ref, cw_ref, cbias_ref, dtb_ref, alog_ref, dsk_ref, ng_ref,
             yssd_ref, yb_ref, st_ref, h_scr, xc_scr, dt_scr, a_scr, r_scr):
        i = pl.program_id(0)

        @pl.when(i == 0)
        def _():
            h_scr[...] = jnp.zeros_like(h_scr)

        for j in range(XBC // 128):
            cs_, zc = slice(128 * j, 128 * (j + 1)), slice(2048 + 128 * j, 2048 + 128 * (j + 1))
            pre = _conv4(zx_ref[:, zc], jnp.where(i == 0, 0.0, zp_ref[:, zc]), cw_ref, cbias_ref, cs_)
            xc_scr[:, cs_] = pre * _sigmoid(pre)
        _ssd_prelude(dtraw_ref, dtb_ref, alog_ref, dt_scr, a_scr)

        def chunk(c, carry):
            r0 = pl.multiple_of(c * CHUNK, CHUNK)
            _, _, cs_cols, dt_cols = _ssd_chunk_scalars(dt_scr, a_scr, r_scr, r0)
            st_ref[c] = h_scr[...]
            for g in range(N_GROUPS):
                bg = xc_scr[pl.ds(r0, CHUNK), 2048 + 128 * g:2048 + 128 * (g + 1)]
                cg = xc_scr[pl.ds(r0, CHUNK), 2560 + 128 * g:2560 + 128 * (g + 1)]
                s2 = _dot_nt(cg, jnp.concatenate([bg, bg], axis=0))
                hp = h_scr[:, 512 * g:512 * (g + 1)]
                yoff = _dot(cg, hp)
                xdw, dec = [], []
                for jj in range(4):
                    j = 4 * g + jj
                    p = _ssd_pair(xc_scr, r_scr, cs_cols, dt_cols, s2, r0, j)
                    y = _dot(p["m2"], p["rhs"]) + yoff[:, 128 * jj:128 * (jj + 1)] * p["e"]
                    yssd_ref[pl.ds(r0, CHUNK), 128 * j:128 * (j + 1)] = y + dsk_ref[0:1, 128 * j:128 * (j + 1)] * p["xs"]
                    xdw.append(p["xd"] * p["w"])
                    dec.append(p["dec"])
                h_scr[:, 512 * g:512 * (g + 1)] = hp * _cat(dec) + _dot_tn(bg, _cat(xdw))
            return carry

        lax.fori_loop(0, cb, chunk, 0)
        for g in range(N_GROUPS):
            sl = slice(512 * g, 512 * (g + 1))
            for q in range(rb // NORM_ROWS):
                rw = slice(NORM_ROWS * q, NORM_ROWS * (q + 1))
                yz = yssd_ref[rw, sl] * _silu(zx_ref[rw, sl])
                rg = lax.rsqrt(jnp.mean(yz * yz, axis=-1, keepdims=True) + EPS)
                yb_ref[rw, sl] = (yz * rg * ng_ref[0:1, sl]).astype(yb_ref.dtype)

    full = lambda rows, cols: pl.BlockSpec((rows, cols), lambda i: (0, 0))
    return pl.pallas_call(
        body, name=name, grid=(ns,),
        in_specs=[pl.BlockSpec((rb, ZX_W), lambda i: (i, 0)),
                  pl.BlockSpec((8, ZX_W), lambda i: (jnp.maximum(i * (rb // 8) - 1, 0), 0)),
                  pl.BlockSpec((rb, DT_PAD), lambda i: (i, 0)),
                  full(4, XBC), full(1, XBC), full(1, DT_PAD), full(1, DT_PAD), full(1, SSD_INNER), full(1, SSD_INNER)],
        out_specs=[pl.BlockSpec((rb, SSD_INNER), lambda i: (i, 0)), pl.BlockSpec((rb, SSD_INNER), lambda i: (i, 0)),
                   pl.BlockSpec((cb, N_STATE, SSD_INNER), lambda i: (i, 0, 0))],
        out_shape=[jax.ShapeDtypeStruct((t, SSD_INNER), F32), jax.ShapeDtypeStruct((t, SSD_INNER), _MXU),
                   jax.ShapeDtypeStruct((t // CHUNK, N_STATE, SSD_INNER), F32)],
        scratch_shapes=[pltpu.VMEM((N_STATE, SSD_INNER), F32), pltpu.VMEM((rb, XBC), F32), pltpu.VMEM((rb, DT_PAD), F32),
                        pltpu.VMEM((rb, DT_PAD), F32), pltpu.VMEM((128, 128), F32)],
        compiler_params=_cp(("arbitrary",)),
    )(proj, proj, dtraw, sw["cw"], sw["cb"], sw["dtb"], sw["alog"], sw["dsk"], sw["ng"])


def _ssd_bwd(proj, dtraw, yssd, states, dyb, dproj, sw, *, rb, name):
    t = proj.shape[0]
    ns, cb = t // rb, rb // CHUNK

    def body(zx_ref, zp_ref, dtraw_ref, yssd_ref, st_ref, dyb_ref, cw_ref, cbias_ref, dtb_ref, alog_ref, dsk_ref, ng_ref,
             dproj_in, dzx_ref, ddt_ref, gconv_ref, gch_ref, ghd_ref,
             dht_scr, xc_scr, dsl_scr, dy_scr, dxc_scr, dt_scr, a_scr, r_scr, dp8_scr):
        del dproj_in
        i = pl.program_id(0)

        @pl.when(i == 0)
        def _():
            dht_scr[...] = jnp.zeros_like(dht_scr)
            dp8_scr[...] = jnp.zeros_like(dp8_scr)
            gconv_ref[...] = jnp.zeros_like(gconv_ref)
            gch_ref[...] = jnp.zeros_like(gch_ref)
            ghd_ref[...] = jnp.zeros_like(ghd_ref)

        tile0 = i == ns - 1
        for j in range(XBC // 128):
            cs_, zc = slice(128 * j, 128 * (j + 1)), slice(2048 + 128 * j, 2048 + 128 * (j + 1))
            pre = _conv4(zx_ref[:, zc], jnp.where(tile0, 0.0, zp_ref[:, zc]), cw_ref, cbias_ref, cs_)
            sg = _sigmoid(pre)
            xc_scr[:, cs_] = pre * sg
            dsl_scr[:, cs_] = sg * (1.0 + pre * (1.0 - sg))
        _ssd_prelude(dtraw_ref, dtb_ref, alog_ref, dt_scr, a_scr)

        for g in range(N_GROUPS):
            sl = slice(512 * g, 512 * (g + 1))
            for q in range(rb // NORM_ROWS):
                rw = slice(NORM_ROWS * q, NORM_ROWS * (q + 1))
                zv = zx_ref[rw, sl]
                ys = yssd_ref[rw, sl]
                sg = _sigmoid(zv)
                sz = zv * sg
                yz = ys * sz
                rg = lax.rsqrt(jnp.mean(yz * yz, axis=-1, keepdims=True) + EPS)
                yn = yz * rg
                dyb_v = dyb_ref[rw, sl]
                gch_ref[0:1, sl] += _rsum(dyb_v * yn)
                dyn = dyb_v * ng_ref[0:1, sl]
                dyz = rg * (dyn - yn * jnp.mean(dyn * yn, axis=-1, keepdims=True))
                dy_scr[rw, sl] = dyz * sz
                dzx_ref[rw, sl] = (dyz * ys * (sg * (1.0 + zv * (1.0 - sg)))).astype(dzx_ref.dtype)

        a_row = -jnp.exp(alog_ref[0:1, :])

        def chunk(cc, carry):
            c = cb - 1 - cc
            r0 = pl.multiple_of(c * CHUNK, CHUNK)
            rows = pl.ds(r0, CHUNK)
            _, dt_c, cs_cols, dt_cols = _ssd_chunk_scalars(dt_scr, a_scr, r_scr, r0)
            lane = lax.broadcasted_iota(jnp.int32, (CHUNK, 128), 1)
            srow = lax.broadcasted_iota(jnp.int32, (CHUNK, 128), 0)
            lo = lane < HEAD_P
            last = srow == CHUNK - 1
            p1_blocks, p3_blocks = [], []
            for g in range(N_GROUPS):
                gs = slice(512 * g, 512 * (g + 1))
                bg = xc_scr[rows, 2048 + 128 * g:2048 + 128 * (g + 1)]
                cg = xc_scr[rows, 2560 + 128 * g:2560 + 128 * (g + 1)]
                b2 = jnp.concatenate([bg, bg], axis=0)
                s2 = _dot_nt(cg, b2)
                s2t = _dot_nt(bg, jnp.concatenate([cg, cg], axis=0))
                hp = st_ref[c, :, gs]
                dht = dht_scr[:, gs]
                yoff = _dot(cg, hp)
                ps = [_ssd_pair(xc_scr, r_scr, cs_cols, dt_cols, s2, r0, 4 * g + jj, s2t) for jj in range(4)]
                dys = [dy_scr[rows, 128 * (4 * g + jj):128 * (4 * g + jj + 1)] for jj in range(4)]
                dye = _cat([dys[jj] * ps[jj]["e"] for jj in range(4)])
                w_g = _cat([p["w"] for p in ps])
                dcg = _dot_nt(dye, hp)
                dht_scr[:, gs] = _dot_tn(cg, dye) + _cat([p["dec"] for p in ps]) * dht
                dxd_state = w_g * _dot(bg, dht)
                dbg = _dot_nt(_cat([p["xd"] for p in ps]) * w_g, dht)
                tsum = _rsum(dht * hp)
                ds2 = jnp.zeros((CHUNK, 128), F32)
                for jj in range(4):
                    j = 4 * g + jj
                    ls = slice(128 * j, 128 * (j + 1))
                    p, dy2 = ps[jj], dys[jj]
                    dy_bd = _block_diag2(dy2)
                    dm2 = _dot_nt(dy2, p["rhs"])
                    ds2 = ds2 + dm2 * p["dm"]
                    gdiff = dm2 * p["m2"] - _dot_nt(p["xd"], dy_bd) * p["mt2"]
                    dxs = dxd_state[:, 128 * jj:128 * (jj + 1)]
                    dxd = _dot(p["mt2"], dy_bd) + dxs
                    end_row = _rsum(p["xd"] * dxs) + p["dec"] * tsum[:, 128 * jj:128 * (jj + 1)]
                    p1_blocks.append(gdiff + dy2 * yoff[:, 128 * jj:128 * (jj + 1)] * p["e"] - p["xd"] * dxs
                                     + jnp.where(last, end_row, 0.0))
                    p3_blocks.append(dxd * p["xs"])
                    dxc_scr[rows, ls] = dxd * p["dtc"] + dy2 * dsk_ref[0:1, ls]
                    gch_ref[1:2, ls] += _rsum(dy2 * p["xs"])
                dcg = dcg + _dot(ds2, b2)
                rb2 = _dot_tn(ds2, cg)
                dxc_scr[rows, 2048 + 128 * g:2048 + 128 * (g + 1)] = dbg + rb2[0:CHUNK] + rb2[CHUNK:2 * CHUNK]
                dxc_scr[rows, 2560 + 128 * g:2560 + 128 * (g + 1)] = dcg
            dcs = _compact_heads(p1_blocks)
            i0 = lax.broadcasted_iota(jnp.int32, (CHUNK, CHUNK), 0)
            i1 = lax.broadcasted_iota(jnp.int32, (CHUNK, CHUNK), 1)
            triu = jnp.where(i1 >= i0, 1.0, 0.0).astype(F32)
            da = jnp.dot(triu, dcs, precision=_HI, preferred_element_type=F32)
            ddt = _compact_heads(p3_blocks) + da * a_row
            ddtraw = jnp.where(lane < N_HEADS, ddt * _sigmoid(dtraw_ref[rows, :] + dtb_ref[0:1, :]), 0.0)
            ddt_ref[rows, :] = ddtraw.astype(ddt_ref.dtype)
            ghd_ref[0:1, :] += _rsum(ddtraw)
            ghd_ref[1:2, :] += _rsum(da * dt_c) * a_row
            return carry

        lax.fori_loop(0, cb, chunk, 0)
        for j in range(XBC // 128):
            cs_, zc = slice(128 * j, 128 * (j + 1)), slice(2048 + 128 * j, 2048 + 128 * (j + 1))
            dpre = dxc_scr[:, cs_] * dsl_scr[:, cs_]
            dzx_ref[:, zc] = _conv4_bwd_x(dpre, dp8_scr[:, cs_], cw_ref, cs_).astype(dzx_ref.dtype)
            dp8_scr[:, cs_] = dpre[0:8]
            xraw = zx_ref[:, zc]
            prev8 = jnp.where(tile0, 0.0, zp_ref[:, zc])
            gconv_ref[3:4, cs_] += _rsum(dpre * xraw)
            for k in (1, 2, 3):
                gconv_ref[3 - k:4 - k, cs_] += _rsum(dpre * _shift_down(xraw, prev8, k))
            gconv_ref[4:5, cs_] += _rsum(dpre)

    rev = lambda i: ns - 1 - i
    full = lambda rows, cols: pl.BlockSpec((rows, cols), lambda i: (0, 0))
    n_in = 13
    return pl.pallas_call(
        body, name=name, grid=(ns,),
        in_specs=[pl.BlockSpec((rb, ZX_W), lambda i: (rev(i), 0)),
                  pl.BlockSpec((8, ZX_W), lambda i: (jnp.maximum(rev(i) * (rb // 8) - 1, 0), 0)),
                  pl.BlockSpec((rb, DT_PAD), lambda i: (rev(i), 0)),
                  pl.BlockSpec((rb, SSD_INNER), lambda i: (rev(i), 0)),
                  pl.BlockSpec((cb, N_STATE, SSD_INNER), lambda i: (rev(i), 0, 0)),
                  pl.BlockSpec((rb, SSD_INNER), lambda i: (rev(i), 0)),
                  full(4, XBC), full(1, XBC), full(1, DT_PAD), full(1, DT_PAD), full(1, SSD_INNER), full(1, SSD_INNER),
                  pl.BlockSpec(memory_space=pl.ANY)],
        out_specs=[pl.BlockSpec((rb, ZX_W), lambda i: (rev(i), 0)), pl.BlockSpec((rb, DT_PAD), lambda i: (rev(i), 0)),
                   full(8, XBC), full(8, SSD_INNER), full(8, DT_PAD)],
        out_shape=[jax.ShapeDtypeStruct(dproj.shape, dproj.dtype), jax.ShapeDtypeStruct((t, DT_PAD), _MXU),
                   jax.ShapeDtypeStruct((8, XBC), F32), jax.ShapeDtypeStruct((8, SSD_INNER), F32),
                   jax.ShapeDtypeStruct((8, DT_PAD), F32)],
        scratch_shapes=[pltpu.VMEM((N_STATE, SSD_INNER), F32), pltpu.VMEM((rb, XBC), F32), pltpu.VMEM((rb, XBC), F32),
                        pltpu.VMEM((rb, SSD_INNER), F32), pltpu.VMEM((rb, XBC), F32), pltpu.VMEM((rb, DT_PAD), F32),
                        pltpu.VMEM((rb, DT_PAD), F32), pltpu.VMEM((128, 128), F32), pltpu.VMEM((8, XBC), F32)],
        input_output_aliases={n_in - 1: 0},
        compiler_params=_cp(("arbitrary",)),
    )(proj, proj, dtraw, yssd, states, dyb, sw["cw"], sw["cb"], sw["dtb"], sw["alog"], sw["dsk"], sw["ng"], dproj)


def _branch_merge(ya, yb, proj, wba, wbb, bgate, *, tm, tn, name):
    t = ya.shape[0]
    nj = D // tn

    def body(ya_ref, yb_ref, ga_ref, gb_ref, wba_ref, wbb_ref, ba_ref, bb_ref, ta_ref, tb_ref, mg_ref):
        ta = _dot(ya_ref[...], wba_ref[...])
        tb = _dot(yb_ref[...], wbb_ref[...])
        ta_ref[...] = ta
        tb_ref[...] = tb
        ga = _sigmoid(ga_ref[...] + ba_ref[...])
        gb = _sigmoid(gb_ref[...] + bb_ref[...])
        mg_ref[...] = (ga * ta + gb * tb).astype(mg_ref.dtype)

    tile = pl.BlockSpec((tm, tn), lambda i, j: (i, j))
    return pl.pallas_call(
        body, name=name, grid=(t // tm, nj),
        in_specs=[pl.BlockSpec((tm, D), lambda i, j: (i, 0)), pl.BlockSpec((tm, SSD_INNER), lambda i, j: (i, 0)),
                  pl.BlockSpec((tm, tn), lambda i, j: (i, G0 // tn + j)),
                  pl.BlockSpec((tm, tn), lambda i, j: (i, (G0 + D) // tn + j)),
                  pl.BlockSpec((D, tn), lambda i, j: (0, j)), pl.BlockSpec((SSD_INNER, tn), lambda i, j: (0, j)),
                  pl.BlockSpec((1, tn), lambda i, j: (0, j)), pl.BlockSpec((1, tn), lambda i, j: (0, nj + j))],
        out_specs=[tile, tile, tile],
        out_shape=[jax.ShapeDtypeStruct((t, D), F32), jax.ShapeDtypeStruct((t, D), F32), jax.ShapeDtypeStruct((t, D), _MXU)],
        compiler_params=_cp(("parallel", "parallel")),
    )(ya, yb, proj, proj, wba, wbb, bgate, bgate)


def _swiglu_mm(gu, wfo, residual, *, tm, tn, name):
    t = gu.shape[0]

    def body(gu_ref, w_ref, r_ref, act_ref, o_ref):
        @pl.when(pl.program_id(1) == 0)
        def _():
            act_ref[...] = (_silu(gu_ref[:, 0:D_FF]) * gu_ref[:, D_FF:2 * D_FF]).astype(act_ref.dtype)
        o_ref[...] = jnp.dot(act_ref[...], w_ref[...], preferred_element_type=F32) + r_ref[...]

    return pl.pallas_call(
        body, name=name, grid=(t // tm, D // tn),
        in_specs=[pl.BlockSpec((tm, 2 * D_FF), lambda i, j: (i, 0)), pl.BlockSpec((D_FF, tn), lambda i, j: (0, j)),
                  pl.BlockSpec((tm, tn), lambda i, j: (i, j))],
        out_specs=[pl.BlockSpec((tm, D_FF), lambda i, j: (i, 0)), pl.BlockSpec((tm, tn), lambda i, j: (i, j))],
        out_shape=[jax.ShapeDtypeStruct((t, D_FF), _MXU), jax.ShapeDtypeStruct((t, D), F32)],
        compiler_params=_cp(("parallel", "arbitrary")),
    )(gu, wfo, residual)


def _ffn_bwd_act(dh, wfo, gu, *, tm, name):
    t = dh.shape[0]

    def body(dh_ref, w_ref, gu_ref, o_ref):
        dact = _dot_nt(dh_ref[...], w_ref[...])
        g = gu_ref[:, 0:D_FF]
        u = gu_ref[:, D_FF:2 * D_FF]
        o_ref[:, 0:D_FF] = (dact * u * _dsilu(g)).astype(o_ref.dtype)
        o_ref[:, D_FF:2 * D_FF] = (dact * _silu(g)).astype(o_ref.dtype)

    return pl.pallas_call(
        body, name=name, grid=(t // tm,),
        in_specs=[pl.BlockSpec((tm, D), lambda i: (i, 0)), pl.BlockSpec((D_FF, D), lambda i: (0, 0)),
                  pl.BlockSpec((tm, 2 * D_FF), lambda i: (i, 0))],
        out_specs=pl.BlockSpec((tm, 2 * D_FF), lambda i: (i, 0)),
        out_shape=jax.ShapeDtypeStruct((t, 2 * D_FF), _MXU),
        compiler_params=_cp(("parallel",)),
    )(dh, wfo, gu)


def _outproj_bwd(dh, wout, ta, tb, proj, bgate, dproj, *, tm, name):
    t = dh.shape[0]

    def body(dh_ref, w_ref, ta_ref, tb_ref, g_ref, b_ref, dta_ref, dtb_ref, dg_ref, db_ref):
        @pl.when(pl.program_id(0) == 0)
        def _():
            db_ref[...] = jnp.zeros_like(db_ref)
        dm = _dot_nt(dh_ref[...], w_ref[...])
        ga = _sigmoid(g_ref[:, 0:D] + b_ref[:, 0:D])
        gb = _sigmoid(g_ref[:, D:2 * D] + b_ref[:, D:2 * D])
        dta_ref[...] = (dm * ga).astype(dta_ref.dtype)
        dtb_ref[...] = (dm * gb).astype(dtb_ref.dtype)
        dga = dm * ta_ref[...] * ga * (1.0 - ga)
        dgb = dm * tb_ref[...] * gb * (1.0 - gb)
        dg_ref[:, 0:D] = dga.astype(dg_ref.dtype)
        dg_ref[:, D:2 * D] = dgb.astype(dg_ref.dtype)
        db_ref[0:1, 0:D] += _rsum(dga)
        db_ref[0:1, D:2 * D] += _rsum(dgb)

    row = lambda cols: pl.BlockSpec((tm, cols), lambda i: (i, 0))
    return pl.pallas_call(
        body, name=name, grid=(t // tm,),
        in_specs=[row(D), pl.BlockSpec((D, D), lambda i: (0, 0)), row(D), row(D),
                  pl.BlockSpec((tm, 2 * D), lambda i: (i, G0 // (2 * D))), pl.BlockSpec((1, 2 * D), lambda i: (0, 0))],
        out_specs=[row(D), row(D), pl.BlockSpec((tm, 2 * D), lambda i: (i, G0 // (2 * D))),
                   pl.BlockSpec((8, 2 * D), lambda i: (0, 0))],
        out_shape=[jax.ShapeDtypeStruct((t, D), _MXU), jax.ShapeDtypeStruct((t, D), _MXU),
                   jax.ShapeDtypeStruct(dproj, _MXU), jax.ShapeDtypeStruct((8, 2 * D), F32)],
        compiler_params=_cp(("arbitrary",)),
    )(dh, wout, ta, tb, proj, bgate)


def _loss_head(h, gf, target, *, tm, name):
    t = h.shape[0]

    def body(h_ref, g_ref, t_ref, loss_ref, dg_ref, dh_ref):
        @pl.when(pl.program_id(0) == 0)
        def _():
            loss_ref[...] = jnp.zeros_like(loss_ref)
            dg_ref[...] = jnp.zeros_like(dg_ref)
        x = h_ref[...]
        r = lax.rsqrt(jnp.mean(x * x, axis=-1, keepdims=True) + EPS)
        xh = x * r
        err = xh * g_ref[...] - t_ref[...]
        loss_ref[...] += 0.5 * jnp.sum(jnp.mean(err * err, axis=-1, keepdims=True), axis=0, keepdims=True)
        dy = err * (1.0 / D)
        dg_ref[0:1, :] += _rsum(dy * xh)
        dxh = dy * g_ref[...]
        dh_ref[...] = r * (dxh - xh * jnp.mean(dxh * xh, axis=-1, keepdims=True))

    row = pl.BlockSpec((tm, D), lambda i: (i, 0))
    return pl.pallas_call(
        body, name=name, grid=(t // tm,),
        in_specs=[row, pl.BlockSpec((1, D), lambda i: (0, 0)), row],
        out_specs=[pl.BlockSpec((8, 128), lambda i: (0, 0)), pl.BlockSpec((8, D), lambda i: (0, 0)), row],
        out_shape=[jax.ShapeDtypeStruct((8, 128), F32), jax.ShapeDtypeStruct((8, D), F32), jax.ShapeDtypeStruct((t, D), F32)],
        compiler_params=_cp(("arbitrary",)),
    )(h, gf, target)


def _row_tile(rows, cols, limit_bytes=1 << 20):
    best = None
    for tr in range(8, rows + 1, 8):
        if rows % tr == 0 and tr * cols * 4 <= limit_bytes:
            best = tr
    return best if best is not None else rows


def _adamw(w, g, m, v, *, name):
    rows, cols = w.shape
    tr = _row_tile(rows, cols)

    def body(w_ref, g_ref, m_ref, v_ref, d_ref, nm_ref, nv_ref):
        gv = g_ref[...]
        nm = ADAM_B1 * m_ref[...] + (1.0 - ADAM_B1) * gv
        nv = ADAM_B2 * v_ref[...] + (1.0 - ADAM_B2) * (gv * gv)
        m_hat = nm / (1.0 - ADAM_B1 ** ADAM_STEP)
        v_hat = nv / (1.0 - ADAM_B2 ** ADAM_STEP)
        d_ref[...] = -ADAM_LR * (m_hat / (jnp.sqrt(v_hat) + ADAM_EPS) + ADAM_WD * w_ref[...])
        nm_ref[...] = nm
        nv_ref[...] = nv

    blk = pl.BlockSpec((tr, cols), lambda i: (i, 0))
    shp = jax.ShapeDtypeStruct((rows, cols), F32)
    return pl.pallas_call(
        body, name=name, grid=(rows // tr,), in_specs=[blk] * 4, out_specs=[blk] * 3, out_shape=[shp] * 3,
        compiler_params=_cp(("parallel",)),
    )(w, g, m, v)


def _bd256(w):
    w4 = w.reshape(4, 4, 64, 64)
    eye = jnp.eye(4, dtype=w.dtype)
    return (w4[:, :, :, None, :] * eye[None, :, None, :, None]).reshape(4, 256, 256)


def _bd256_diag(g):
    g5 = g.reshape(4, 4, 64, 4, 64)
    return jnp.stack([g5[:, a, :, a, :] for a in range(4)], axis=1).reshape(16, 64, 64)


def _layer_weights(w, small, l):
    win = w["w_in"][l]
    lx, lg = win[:, 0:D], win[:, D:2 * D]
    lblk = [jnp.concatenate([lx[:, 256 * j:256 * (j + 1)], lg[:, 256 * j:256 * (j + 1)]], axis=1) for j in range(4)]
    wp = jnp.concatenate([win[:, 2048:4096], win[:, 4096:7168], lblk[0], lblk[1], win[:, 7200:9248], lblk[2], lblk[3]], axis=1)
    wdt = jnp.pad(win[:, 7168:7200], ((0, 0), (0, DT_PAD - N_HEADS)))
    row = lambda v: v.reshape(1, -1)
    pad_h = lambda v: jnp.pad(v.reshape(1, -1), ((0, 0), (0, DT_PAD - N_HEADS)))
    lw = dict(cw=w["lru_conv_w"][l], cb=row(small["lru_conv_b"][l]),
              wa=_bd256(small["lru_w_a"][l]).astype(_MXU), wx=_bd256(small["lru_w_x"][l]).astype(_MXU),
              ba=row(small["lru_b_a"][l]), bx=row(small["lru_b_x"][l]), lam=row(small["lru_lambda"][l]))
    sw = dict(cw=w["ssd_conv_w"][l], cb=row(small["ssd_conv_b"][l]), dtb=pad_h(small["ssd_dt_bias"][l]),
              alog=pad_h(small["ssd_A_log"][l]), dsk=row(jnp.repeat(small["ssd_D"][l], HEAD_P)),
              ng=row(small["ssd_norm_g"][l]))
    return dict(wp=wp, wdt=wdt, lw=lw, sw=sw, wba=w["w_branch"][l][0:D], wbb=w["w_branch"][l][D:3 * D],
                wout=w["w_out"][l], wfi=w["w_ffn_in"][l], wfo=w["w_ffn_out"][l],
                g1=row(small["norm1_g"][l]), g2=row(small["norm2_g"][l]), bgate=row(small["b_gate"][l]))


def _tiles(t):
    return dict(tmn=min(1024, t), tm=min(512, t), tm2=min(256, t), r=min(256, t), rb=min(128, t))


def _layer_fwd(h, lwt, l):
    tl = _tiles(h.shape[0])
    n = f"l{l}_"
    xn, proj = _norm_mm(h, lwt["g1"], lwt["wp"], tm=tl["tmn"], tn=1024, name=n + "in_proj")
    dtraw = _mm_nn(xn, lwt["wdt"], tm=tl["tm"], tn=DT_PAD, name=n + "dt_proj")
    hl, ya = _lru_fwd(proj, lwt["lw"], r=tl["r"], name=n + "lru_fwd")
    yssd, yb, states = _ssd_fwd(proj, dtraw, lwt["sw"], rb=tl["rb"], name=n + "ssd_fwd")
    ta, tb, merged = _branch_merge(ya, yb, proj, lwt["wba"], lwt["wbb"], lwt["bgate"], tm=tl["tm"], tn=512, name=n + "merge")
    hmid = _mm_nn(merged, lwt["wout"], tm=tl["tm"], tn=512, name=n + "out_proj", residual=h)
    xn2, gu = _norm_mm(hmid, lwt["g2"], lwt["wfi"], tm=tl["tmn"], tn=512, name=n + "ffn_in")
    act, hout = _swiglu_mm(gu, lwt["wfo"], hmid, tm=tl["tm2"], tn=512, name=n + "ffn_out")
    saved = dict(h=h, xn=xn, proj=proj, dtraw=dtraw, hl=hl, ya=ya, yssd=yssd, yb=yb, states=states, ta=ta, tb=tb,
                 merged=merged, hmid=hmid, xn2=xn2, gu=gu, act=act)
    return hout, saved


def _layer_bwd(dh, s, lwt, l):
    t = dh.shape[0]
    tl = _tiles(t)
    n = f"l{l}_"
    tn = dict(tt=tl["tm"], tn=512)
    dgu = _ffn_bwd_act(dh, lwt["wfo"], s["gu"], tm=tl["tm2"], name=n + "ffn_act_bwd")
    dwfo = _mm_tn(s["act"], dh, name=n + "ffn_out_wgrad", **tn)
    dwfi = _mm_tn(s["xn2"], dgu, name=n + "ffn_in_wgrad", **tn)
    dh1, dg2 = _mm_nt_rmsbwd(dgu, lwt["wfi"], s["hmid"], lwt["g2"], dh, tm=tl["tm"], tk=512, name=n + "ffn_in_dgrad")
    dta, dtb, dproj, dbg = _outproj_bwd(dh1, lwt["wout"], s["ta"], s["tb"], s["proj"], lwt["bgate"], (t, NP),
                                        tm=tl["tm2"], name=n + "out_proj_bwd")
    dwout = _mm_tn(s["merged"], dh1, name=n + "out_proj_wgrad", **tn)
    dya = _mm_nt(dta, lwt["wba"], tm=tl["tm"], name=n + "branch_a_dgrad")
    dyb = _mm_nt(dtb, lwt["wbb"], tm=tl["tm"], name=n + "branch_b_dgrad")
    dwba = _mm_tn(s["ya"], dta, name=n + "branch_a_wgrad", **tn)
    dwbb = _mm_tn(s["yb"], dtb, name=n + "branch_b_wgrad", **tn)
    dproj, lsm, dwa, dwx = _lru_bwd(s["proj"], s["hl"], dya, dproj, lwt["lw"], r=tl["r"], name=n + "lru_bwd")
    dproj, ddt, gconv, gch, ghd = _ssd_bwd(s["proj"], s["dtraw"], s["yssd"], s["states"], dyb, dproj, lwt["sw"],
                                           rb=tl["rb"], name=n + "ssd_bwd")
    dwp = _mm_tn(s["xn"], dproj, tt=tl["tm"], tn=1024, name=n + "in_proj_wgrad")
    dwdt = _mm_tn(s["xn"], ddt, tt=tl["tm"], tn=DT_PAD, name=n + "dt_proj_wgrad")
    dh0, dg1 = _mm_nt_rmsbwd(dproj, lwt["wp"], s["h"], lwt["g1"], dh1, tm=tl["tm"], tk=1024, name=n + "in_proj_dgrad",
                             extra=(ddt, lwt["wdt"]))
    lcol = [LBLK * _lblk_col(j) for j in range(4)]
    dlx = jnp.concatenate([dwp[:, c:c + 256] for c in lcol], axis=1)
    dlg = jnp.concatenate([dwp[:, c + 256:c + 512] for c in lcol], axis=1)
    grads = dict(
        w_in=jnp.concatenate([dlx, dlg, dwp[:, 0:2048], dwp[:, 2048:ZX_W], dwdt[:, 0:N_HEADS], dwp[:, G0:G0 + 2048]], axis=1),
        w_branch=jnp.concatenate([dwba, dwbb], axis=0), w_out=dwout, w_ffn_in=dwfi, w_ffn_out=dwfo,
        lru_conv_w=lsm[0:4], lru_conv_b=lsm[4], lru_b_a=lsm[5], lru_b_x=lsm[6], lru_lambda=lsm[7],
        lru_w_a=_bd256_diag(dwa), lru_w_x=_bd256_diag(dwx),
        ssd_conv_w=gconv[0:4], ssd_conv_b=gconv[4], ssd_norm_g=gch[0], ssd_D=gch[1].reshape(N_HEADS, HEAD_P).sum(axis=-1),
        ssd_dt_bias=ghd[0, 0:N_HEADS], ssd_A_log=ghd[1, 0:N_HEADS],
        b_gate=dbg[0], norm1_g=dg1[0], norm2_g=dg2[0])
    return dh0, grads


def _local_step(x, target, w, small):
    h = x
    lwts, saved = [], []
    for l in range(N_LAYERS):
        lwt = _layer_weights(w, small, l)
        h, s = _layer_fwd(h, lwt, l)
        lwts.append(lwt)
        saved.append(s)
    loss_blk, dgf, dh = _loss_head(h, small["norm_f"].reshape(1, D), target, tm=_tiles(x.shape[0])["tm"], name="loss_head")
    per_layer = [None] * N_LAYERS
    for l in reversed(range(N_LAYERS)):
        dh, per_layer[l] = _layer_bwd(dh, saved[l], lwts[l], l)
    grads = {k: jnp.stack([per_layer[l][k] for l in range(N_LAYERS)], axis=0) for k in per_layer[0]}
    grads["norm_f"] = dgf[0]
    return loss_blk, dh, grads


PACK_W = 1024
BIG = (("w_in", 2), ("w_branch", 1), ("w_out", 1), ("w_ffn_in", 2), ("w_ffn_out", 1), ("lru_conv_w", 2), ("ssd_conv_w", 2))
CONV = ("lru_conv_w", "ssd_conv_w")
SMALL = ("norm1_g", "b_gate", "lru_conv_b", "lru_w_a", "lru_b_a", "lru_w_x", "lru_b_x", "lru_lambda", "ssd_conv_b",
         "ssd_dt_bias", "ssd_A_log", "ssd_D", "ssd_norm_g", "norm2_g", "norm_f")
_WIRE = jnp.bfloat16
N_CHIPS = 4
N_DEV = 8


def _rows_of(shape):
    return int(np.prod(shape)) // PACK_W


def _pack_rows(parts, total_rows):
    flat = [p.reshape(-1, PACK_W) for p in parts]
    have = sum(f.shape[0] for f in flat)
    if have < total_rows:
        flat.append(jnp.zeros((total_rows - have, PACK_W), flat[0].dtype))
    return jnp.concatenate(flat, axis=0)


def _round16(n):
    return -(-n // 32) * 32


def _mesh_pos():
    return lax.axis_index("x"), lax.axis_index("y"), lax.axis_index("c")


HBM_SPEC = pl.BlockSpec(memory_space=pltpu.HBM)


def _remote(src, dst, send_sems, recv_sems, k, to):
    return pltpu.make_async_remote_copy(src_ref=src, dst_ref=dst, send_sem=send_sems.at[k], recv_sem=recv_sems.at[k],
                                        device_id=to, device_id_type=MESH)


def _allgather_chips(wpack):
    _, rh, wd = wpack.shape

    def body(w_ref, o_ref, send_sems, recv_sems):
        x, y, c = _mesh_pos()
        s = 2 * x + y
        sib = (x, y, 1 - c)
        chips = [(1 - x, y), (x, 1 - y), (1 - x, 1 - y)]
        first =[_remote(w_ref.at[c], o_ref.at[s, c], send_sems, recv_sems, j, (px, py, c)) for j, (px, py) in enumerate(chips)]
        for cp in first:
            cp.start()
        passed = []
        for j, (px, py) in enumerate(chips):
            p = 2 * px + py
            _remote(w_ref.at[c], o_ref.at[p, c], send_sems, recv_sems, j, (px, py, c)).wait_recv()
            cp = _remote(o_ref.at[p, c], o_ref.at[p, c], send_sems, recv_sems, 3 + j, sib)
            cp.start()
            passed.append(cp)
        for j, (px, py) in enumerate(chips):
            p = 2 * px + py
            _remote(o_ref.at[p, 1 - c], o_ref.at[p, 1 - c], send_sems, recv_sems, 3 + j, sib).wait_recv()
        for cp in first + passed:
            cp.wait_send()

    return pl.pallas_call(
        body, name="allgather_weights", in_specs=[HBM_SPEC], out_specs=HBM_SPEC,
        out_shape=jax.ShapeDtypeStruct((N_CHIPS, 2, rh, wd), wpack.dtype),
        scratch_shapes=[pltpu.SemaphoreType.DMA((6,)), pltpu.SemaphoreType.DMA((6,))],
    )(wpack)


def _sibling_exchange(gpack):
    n, _, rh, wd = gpack.shape

    def body(g_ref, o_ref, send_sems, recv_sems):
        x, y, c = _mesh_pos()
        cp = _remote(g_ref.at[:, 1 - c], o_ref, send_sems, recv_sems, 0, (x, y, 1 - c))
        cp.start()
        cp.wait()

    return pl.pallas_call(
        body, name="grad_sibling_exchange", in_specs=[HBM_SPEC], out_specs=HBM_SPEC,
        out_shape=jax.ShapeDtypeStruct((n, rh, wd), gpack.dtype),
        scratch_shapes=[pltpu.SemaphoreType.DMA((1,)), pltpu.SemaphoreType.DMA((1,))],
    )(gpack)


def _add_sibling(gpack, recv, c_idx, *, tr):
    n, _, rh, wd = gpack.shape

    def body(c_ref, g_ref, r_ref, o_ref):
        del c_ref
        o_ref[...] = (g_ref[...] + r_ref[...]).astype(o_ref.dtype)

    return pl.pallas_call(
        body, name="grad_add_sibling",
        grid_spec=pltpu.PrefetchScalarGridSpec(
            num_scalar_prefetch=1, grid=(n, rh // tr),
            in_specs=[pl.BlockSpec((None, None, tr, wd), lambda p, i, c_ref: (p, c_ref[0], i, 0)),
                      pl.BlockSpec((None, tr, wd), lambda p, i, c_ref: (p, i, 0))],
            out_specs=pl.BlockSpec((None, tr, wd), lambda p, i, c_ref: (p, i, 0))),
        out_shape=jax.ShapeDtypeStruct((n, rh, wd), _WIRE),
        compiler_params=_cp(("parallel", "parallel")),
    )(c_idx, gpack, recv)


def _chip_exchange(part):
    n, rh, wd = part.shape

    def body(s_ref, o_ref, send_sems, recv_sems):
        x, y, c = _mesh_pos()
        s = 2 * x + y
        chips = [(1 - x, y), (x, 1 - y), (1 - x, 1 - y)]
        sends = [_remote(s_ref.at[2 * px + py], o_ref.at[s], send_sems, recv_sems, j, (px, py, c))
                 for j, (px, py) in enumerate(chips)]
        for cp in sends:
            cp.start()
        for j, (px, py) in enumerate(chips):
            p = 2 * px + py
            _remote(s_ref.at[p], o_ref.at[p], send_sems, recv_sems, j, (px, py, c)).wait_recv()
        for cp in sends:
            cp.wait_send()

    return pl.pallas_call(
        body, name="grad_chip_exchange", in_specs=[HBM_SPEC], out_specs=HBM_SPEC,
        out_shape=jax.ShapeDtypeStruct((n, rh, wd), part.dtype),
        scratch_shapes=[pltpu.SemaphoreType.DMA((3,)), pltpu.SemaphoreType.DMA((3,))],
    )(part)


def _sum_slots(slots, own, sel, *, tr, name, halves=False):
    n, rows, wd = slots.shape
    k = own.shape[0]

    def body(sel_ref, s_ref, own_ref, o_ref):
        mine = sel_ref[0]
        acc = jnp.zeros((tr, wd), F32)
        for p in range(n):
            acc = acc + jnp.where(mine == p, own_ref[...].astype(F32), s_ref[p].astype(F32))
        o_ref[...] = acc

    if halves:
        out_spec = pl.BlockSpec((None, tr, wd), lambda i, sel_ref: (sel_ref[1], i, 0))
        out_shape = jax.ShapeDtypeStruct((2, rows, wd), F32)
    else:
        out_spec = pl.BlockSpec((tr, wd), lambda i, sel_ref: (i, 0))
        out_shape = jax.ShapeDtypeStruct((rows, wd), F32)
    return pl.pallas_call(
        body, name=name,
        grid_spec=pltpu.PrefetchScalarGridSpec(
            num_scalar_prefetch=1, grid=(rows // tr,),
            in_specs=[pl.BlockSpec((n, tr, wd), lambda i, sel_ref: (0, i, 0)),
                      pl.BlockSpec((None, tr, wd), lambda i, sel_ref: (sel_ref[0] if k > 1 else 0, i, 0))],
            out_specs=out_spec),
        out_shape=out_shape, compiler_params=_cp(("parallel",)),
    )(sel, slots, own)


def _sibling_share(both):
    def body(b_ref, o_ref, send_sems, recv_sems):
        del b_ref
        x, y, c = _mesh_pos()
        cp = _remote(o_ref.at[c], o_ref.at[c], send_sems, recv_sems, 0, (x, y, 1 - c))
        cp.start()
        _remote(o_ref.at[1 - c], o_ref.at[1 - c], send_sems, recv_sems, 0, (x, y, 1 - c)).wait_recv()
        cp.wait_send()

    return pl.pallas_call(
        body, name="grad_sibling_share", in_specs=[HBM_SPEC], out_specs=HBM_SPEC,
        out_shape=jax.ShapeDtypeStruct(both.shape, both.dtype), input_output_aliases={0: 0},
        scratch_shapes=[pltpu.SemaphoreType.DMA((1,)), pltpu.SemaphoreType.DMA((1,))],
    )(both)


def _allgather_devices(part):
    rows, wd = part.shape

    def body(p_ref, o_ref, send_sems, recv_sems):
        x, y, c = _mesh_pos()
        me = 4 * x + 2 * y + c
        peers = []
        for k in range(1, N_DEV):
            dx, dy, dc = (k >> 2) & 1, (k >> 1) & 1, k & 1
            peers.append(((x + dx) % 2, (y + dy) % 2, (c + dc) % 2))
        sends = [_remote(p_ref, o_ref.at[me], send_sems, recv_sems, k, to) for k, to in enumerate(peers)]
        for cp in sends:
            cp.start()
        for k, (px, py, pc) in enumerate(peers):
            _remote(p_ref, o_ref.at[4 * px + 2 * py + pc], send_sems, recv_sems, k, (px, py, pc)).wait_recv()
        for cp in sends:
            cp.wait_send()

    return pl.pallas_call(
        body, name="small_grad_allgather", in_specs=[HBM_SPEC], out_specs=HBM_SPEC,
        out_shape=jax.ShapeDtypeStruct((N_DEV, rows, wd), part.dtype),
        scratch_shapes=[pltpu.SemaphoreType.DMA((N_DEV - 1,)), pltpu.SemaphoreType.DMA((N_DEV - 1,))],
    )(part)


def _shard_to_full(stack, axis):
    _, nl, a, b = stack.shape
    if axis == 2:
        return stack.transpose(1, 2, 0, 3).reshape(nl, a, N_CHIPS * b)
    return stack.transpose(1, 0, 2, 3).reshape(nl, N_CHIPS * a, b)


def _full_to_shards(full, axis):
    nl, a, b = full.shape
    if axis == 2:
        return full.reshape(nl, a, N_CHIPS, b // N_CHIPS).transpose(2, 0, 1, 3)
    return full.reshape(nl, N_CHIPS, a // N_CHIPS, b).transpose(1, 0, 2, 3)


def _sharded_step(a):
    x = a["x"][0]
    target = a["loss_target"][0]
    cx, cy, cc = _mesh_pos()
    shard_shapes = {n: a[n].shape for n, _ in BIG}

    parts = []
    for n, _ in BIG:
        if n in CONV:
            hi = a[n].astype(_MXU).astype(F32)
            parts += [hi, a[n] - hi]
        else:
            parts.append(a[n])
    total = _round16(sum(_rows_of(p.shape) for p in parts))
    rh = total // 2
    wpack = _pack_rows([p.astype(_MXU) for p in parts], total).reshape(2, rh, PACK_W)
    chip = (2 * cx + cy).astype(jnp.int32)
    zero = jnp.zeros((), jnp.int32)
    allw = lax.dynamic_update_slice(_allgather_chips(wpack), wpack[None], (chip, zero, zero, zero))
    allw = allw.reshape(N_CHIPS, total, PACK_W)
    w, off = {}, 0
    for n, ax in BIG:
        rows = _rows_of(shard_shapes[n])
        piece = lambda o: allw[:, o:o + rows].reshape((N_CHIPS,) + shard_shapes[n])
        if n in CONV:
            w[n] = _shard_to_full(piece(off).astype(F32) + piece(off + rows).astype(F32), ax)
            off += 2 * rows
        else:
            w[n] = _shard_to_full(piece(off), ax)
            off += rows
    small = {n: a[n] for n in SMALL}

    loss_blk, grad_x, grads = _local_step(x, target, w, small)
    loss = lax.psum(loss_blk[0, 0], ("x", "y", "c"))

    gtotal = _round16(sum(_rows_of(shard_shapes[n]) for n, _ in BIG))
    grh = gtotal // 2
    gparts = [_full_to_shards(grads[n], ax).reshape(N_CHIPS, -1, PACK_W) for n, ax in BIG]
    have = sum(g.shape[1] for g in gparts)
    if have < gtotal:
        gparts.append(jnp.zeros((N_CHIPS, gtotal - have, PACK_W), F32))
    gpack = jnp.concatenate(gparts, axis=1).reshape(N_CHIPS, 2, grh, PACK_W)
    tr = _row_tile(grh, PACK_W, 2 << 20)
    recv_c = _sibling_exchange(gpack)
    part = _add_sibling(gpack, recv_c, cc.astype(jnp.int32).reshape(1), tr=tr)
    slots = _chip_exchange(part)
    both = _sum_slots(slots, part, jnp.stack([chip, cc.astype(jnp.int32)]), tr=tr, name="grad_sum_chips", halves=True)
    gshard = _sibling_share(both).reshape(gtotal, PACK_W)
    g_big, off = {}, 0
    for n, _ in BIG:
        rows = _rows_of(shard_shapes[n])
        g_big[n] = gshard[off:off + rows].reshape(shard_shapes[n])
        off += rows

    srows = -(-sum(int(np.prod(a[n].shape)) for n in SMALL) // (8 * PACK_W)) * 8
    flat =lambda d: jnp.concatenate([d[n].reshape(-1) for n in SMALL])
    padto = lambda v: jnp.pad(v, (0, srows * PACK_W - v.shape[0])).reshape(srows, PACK_W)
    g_small_own = padto(flat(grads))
    me = (4 * cx + 2 * cy + cc).astype(jnp.int32)
    g_small_pack = _sum_slots(_allgather_devices(g_small_own), g_small_own[None], jnp.stack([me, zero]), tr=srows,
                              name="small_grad_sum")

    out_g, out_d, out_m, out_v = {}, {}, {}, {}
    for n, _ in BIG:
        shp = shard_shapes[n]
        two_d = (shp[0] * shp[1], shp[2])
        d_, m_, v_ = _adamw(a[n].reshape(two_d), g_big[n].reshape(two_d), a["m_" + n].reshape(two_d),
                            a["v_" + n].reshape(two_d), name="adamw_" + n)
        out_g[n], out_d[n], out_m[n], out_v[n] = g_big[n], d_.reshape(shp), m_.reshape(shp), v_.reshape(shp)
    d_, m_, v_ = _adamw(padto(flat(a)), g_small_pack, padto(flat({n: a["m_" + n] for n in SMALL})),
                        padto(flat({n: a["v_" + n] for n in SMALL})), name="adamw_small")
    off = 0
    for n in SMALL:
        size = int(np.prod(a[n].shape))
        cut = lambda v: v.reshape(-1)[off:off + size].reshape(a[n].shape)
        out_g[n], out_d[n], out_m[n], out_v[n] = cut(g_small_pack), cut(d_), cut(m_), cut(v_)
        off += size
    return loss, grad_x[None], out_g, out_d, out_m, out_v


WEIGHTS = ("norm1_g", "w_in", "b_gate", "lru_conv_w", "lru_conv_b", "lru_w_a", "lru_b_a", "lru_w_x", "lru_b_x", "lru_lambda",
           "ssd_conv_w", "ssd_conv_b", "ssd_dt_bias", "ssd_A_log", "ssd_D", "ssd_norm_g", "w_branch", "w_out", "norm2_g",
           "w_ffn_in", "w_ffn_out", "norm_f")
INPUTS = ("x",) + WEIGHTS + ("loss_target",) + tuple("m_" + n for n in WEIGHTS) + tuple("v_" + n for n in WEIGHTS)


def kernel(x, norm1_g, w_in, b_gate, lru_conv_w, lru_conv_b, lru_w_a, lru_b_a, lru_w_x, lru_b_x, lru_lambda, ssd_conv_w, ssd_conv_b, ssd_dt_bias, ssd_A_log, ssd_D, ssd_norm_g, w_branch, w_out, norm2_g, w_ffn_in, w_ffn_out, norm_f, loss_target, m_norm1_g, m_w_in, m_b_gate, m_lru_conv_w, m_lru_conv_b, m_lru_w_a, m_lru_b_a, m_lru_w_x, m_lru_b_x, m_lru_lambda, m_ssd_conv_w, m_ssd_conv_b, m_ssd_dt_bias, m_ssd_A_log, m_ssd_D, m_ssd_norm_g, m_w_branch, m_w_out, m_norm2_g, m_w_ffn_in, m_w_ffn_out, m_norm_f, v_norm1_g, v_w_in, v_b_gate, v_lru_conv_w, v_lru_conv_b, v_lru_w_a, v_lru_b_a, v_lru_w_x, v_lru_b_x, v_lru_lambda, v_ssd_conv_w, v_ssd_conv_b, v_ssd_dt_bias, v_ssd_A_log, v_ssd_D, v_ssd_norm_g, v_w_branch, v_w_out, v_norm2_g, v_w_ffn_in, v_w_ffn_out, v_norm_f):
    args = (x, norm1_g, w_in, b_gate, lru_conv_w, lru_conv_b, lru_w_a, lru_b_a, lru_w_x, lru_b_x, lru_lambda, ssd_conv_w, ssd_conv_b, ssd_dt_bias, ssd_A_log, ssd_D, ssd_norm_g, w_branch, w_out, norm2_g, w_ffn_in, w_ffn_out, norm_f, loss_target, m_norm1_g, m_w_in, m_b_gate, m_lru_conv_w, m_lru_conv_b, m_lru_w_a, m_lru_b_a, m_lru_w_x, m_lru_b_x, m_lru_lambda, m_ssd_conv_w, m_ssd_conv_b, m_ssd_dt_bias, m_ssd_A_log, m_ssd_D, m_ssd_norm_g, m_w_branch, m_w_out, m_norm2_g, m_w_ffn_in, m_w_ffn_out, m_norm_f, v_norm1_g, v_w_in, v_b_gate, v_lru_conv_w, v_lru_conv_b, v_lru_w_a, v_lru_b_a, v_lru_w_x, v_lru_b_x, v_lru_lambda, v_ssd_conv_w, v_ssd_conv_b, v_ssd_dt_bias, v_ssd_A_log, v_ssd_D, v_ssd_norm_g, v_w_branch, v_w_out, v_norm2_g, v_w_ffn_in, v_w_ffn_out, v_norm_f)
    assert len(args) == len(INPUTS)
    loss, grad_x, g, d, m, v = _sharded_step(dict(zip(INPUTS, args)))
    return (loss, grad_x, *[g[n] for n in WEIGHTS], *[d[n] for n in WEIGHTS], *[m[n] for n in WEIGHTS],
            *[v[n] for n in WEIGHTS])
```

```python
import functools
import math

import numpy as np
import jax
import jax.numpy as jnp
from jax import lax
from jax.experimental import pallas as pl
from jax.experimental.pallas import tpu as pltpu

F32 = jnp.float32
BF16 = jnp.bfloat16
_MXU = jnp.bfloat16
_HI = lax.Precision.HIGHEST

D = 1024
EPS = 1e-6
N_LAYERS = 2
LRU_C = 8.0
N_HEADS = 32
HEAD_P = 64
N_GROUPS = 4
N_STATE = 128
SSD_INNER = 2048
XBC = 3072
D_FF = 2816
CHUNK = 64
NORM_ROWS = 32
IN_DIM = 9248

NP = 9216
ZX_W = 5120
G0 = 6144
LBLK = 512
DT_PAD = 128

VMEM_LIMIT_BYTES_V7X = 56 * 1024 * 1024

ADAM_LR, ADAM_B1, ADAM_B2, ADAM_EPS, ADAM_WD, ADAM_STEP = 0.001, 0.9, 0.999, 1e-08, 0.01, 10
MESH = pl.DeviceIdType.MESH


def _cp(sem):
    return pltpu.CompilerParams(dimension_semantics=sem, vmem_limit_bytes=VMEM_LIMIT_BYTES_V7X)


def _lblk_col(j):
    return 10 + j + 4 * (j // 2)


def _sigmoid(x):
    return 0.5 * jnp.tanh(0.5 * x) + 0.5


def _softplus(x):
    return jnp.maximum(x, 0.0) + jnp.log(1.0 + jnp.exp(-jnp.abs(x)))


def _silu(x):
    return x * _sigmoid(x)


def _dsilu(x):
    s = _sigmoid(x)
    return s * (1.0 + x * (1.0 - s))


_GELU_C0 = math.sqrt(2.0 / math.pi)
_GELU_C1 = 0.044715


def _gelu_and_grad(x):
    t = jnp.tanh(_GELU_C0 * (x + _GELU_C1 * x * x * x))
    g = 0.5 * x * (1.0 + t)
    dg = 0.5 * (1.0 + t) + 0.5 * x * (1.0 - t * t) * _GELU_C0 * (1.0 + 3.0 * _GELU_C1 * x * x)
    return g, dg


def _one_minus_exp(x):
    p = 1.0 + x * (1.0 / 7.0)
    p = 1.0 + x * (1.0 / 6.0) * p
    p = 1.0 + x * (1.0 / 5.0) * p
    p = 1.0 + x * (1.0 / 4.0) * p
    p = 1.0 + x * (1.0 / 3.0) * p
    p = 1.0 + x * (1.0 / 2.0) * p
    return jnp.where(x > -0.3, -x * p, 1.0 - jnp.exp(x))


def _dot(a, b):
    return jnp.dot(a.astype(_MXU), b.astype(_MXU), preferred_element_type=F32)


def _dot_nt(a, b):
    return lax.dot_general(a.astype(_MXU), b.astype(_MXU), (((1,), (1,)), ((), ())), preferred_element_type=F32)


def _dot_tn(a, b):
    return lax.dot_general(a.astype(_MXU), b.astype(_MXU), (((0,), (0,)), ((), ())), preferred_element_type=F32)


def _shift_down(x, prev8, k):
    xr = pltpu.roll(x, k, 0)
    pr = pltpu.roll(prev8, k, 0)
    row = lax.broadcasted_iota(jnp.int32, prev8.shape, 0)
    head = jnp.where(row < k, pr, xr[0:8])
    return jnp.concatenate([head, xr[8:]], axis=0)


def _shift_up(x, next8, k):
    r = x.shape[0]
    xr = pltpu.roll(x, r - k, 0)
    nr = pltpu.roll(next8, 8 - k, 0)
    row = lax.broadcasted_iota(jnp.int32, next8.shape, 0)
    tail = jnp.where(row >= 8 - k, nr, xr[r - 8:r])
    return jnp.concatenate([xr[:r - 8], tail], axis=0)


def _conv4(x, prev8, w_ref, b_ref, cols=slice(None)):
    acc = x * w_ref[3:4, cols] + b_ref[0:1, cols]
    for k in (1, 2, 3):
        acc = acc + _shift_down(x, prev8, k) * w_ref[3 - k:4 - k, cols]
    return acc


def _conv4_bwd_x(dy, next8, w_ref, cols=slice(None)):
    acc = dy * w_ref[3:4, cols]
    for k in (1, 2, 3):
        acc = acc + _shift_up(dy, next8, k) * w_ref[3 - k:4 - k, cols]
    return acc


def _lin_scan(a, b, reverse):
    r = a.shape[0]
    row = lax.broadcasted_iota(jnp.int32, a.shape, 0)
    d = 1
    while d < r:
        sh = (r - d) if reverse else d
        a_s = pltpu.roll(a, sh, 0)
        b_s = pltpu.roll(b, sh, 0)
        m = (row < r - d) if reverse else (row >= d)
        b = jnp.where(m, a * b_s + b, b)
        a = jnp.where(m, a * a_s, a)
        d *= 2
    return a, b


def _rsum(x):
    return jnp.sum(x, axis=0, keepdims=True)


def _norm_mm(h, gamma, w, *, tm, tn, name):
    m, k = h.shape
    if w.ndim == 3:
        assert w.shape[2] == tn
        n = w.shape[0] * tn
        w_spec = pl.BlockSpec((None, k, tn), lambda i, j: (j, 0, 0))
    else:
        n = w.shape[1]
        w_spec = pl.BlockSpec((k, tn), lambda i, j: (0, j))

    def body(h_ref, g_ref, w_ref, xn_ref, o_ref):
        @pl.when(pl.program_id(1) == 0)
        def _():
            x = h_ref[...]
            r = lax.rsqrt(jnp.mean(x * x, axis=-1, keepdims=True) + EPS)
            xn_ref[...] = ((x * r) * g_ref[...]).astype(xn_ref.dtype)
        o_ref[...] = jnp.dot(xn_ref[...], w_ref[...], preferred_element_type=F32)

    return pl.pallas_call(
        body, name=name, grid=(m // tm, n // tn),
        in_specs=[pl.BlockSpec((tm, k), lambda i, j: (i, 0)), pl.BlockSpec((1, k), lambda i, j: (0, 0)), w_spec],
        out_specs=[pl.BlockSpec((tm, k), lambda i, j: (i, 0)), pl.BlockSpec((tm, tn), lambda i, j: (i, j))],
        out_shape=[jax.ShapeDtypeStruct((m, k), _MXU), jax.ShapeDtypeStruct((m, n), F32)],
        compiler_params=_cp(("parallel", "arbitrary")),
    )(h, gamma, w)


def _mm_nn(a, w, *, tm, tn, name, residual=None):
    m, k = a.shape
    n = w.shape[1]

    def body(*refs):
        if residual is None:
            a_ref, w_ref, o_ref = refs
            o_ref[...] = _dot(a_ref[...], w_ref[...])
        else:
            a_ref, w_ref, r_ref, o_ref = refs
            o_ref[...] = _dot(a_ref[...], w_ref[...]) + r_ref[...]

    in_specs = [pl.BlockSpec((tm, k), lambda i, j: (i, 0)), pl.BlockSpec((k, tn), lambda i, j: (0, j))]
    args = [a, w]
    if residual is not None:
        in_specs.append(pl.BlockSpec((tm, tn), lambda i, j: (i, j)))
        args.append(residual)
    return pl.pallas_call(
        body, name=name, grid=(m // tm, n // tn), in_specs=in_specs,
        out_specs=pl.BlockSpec((tm, tn), lambda i, j: (i, j)),
        out_shape=jax.ShapeDtypeStruct((m, n), F32),
        compiler_params=_cp(("parallel", "parallel")),
    )(*args)


def _wgrad(a, b, *, tt, ta, tn, name, out_shape, out_block, out_index, a_tab=None, o_tab=None, into=None):
    t = a.shape[0]
    a_tab = list(range(a.shape[1] // ta)) if a_tab is None else a_tab
    o_tab = a_tab if o_tab is None else o_tab
    nb = b.shape[1] // tn

    def body(at_ref, ot_ref, a_ref, b_ref, *rest):
        del at_ref, ot_ref
        o_ref = rest[-1]

        @pl.when(pl.program_id(2) == 0)
        def _():
            o_ref[...] = jnp.zeros_like(o_ref)
        o_ref[...] += _dot_tn(a_ref[...], b_ref[...])

    in_specs = [pl.BlockSpec((tt, ta), lambda r, j, i, at, ot: (i, at[r])),
                pl.BlockSpec((tt, tn), lambda r, j, i, at, ot: (i, j))]
    args = [jnp.asarray(a_tab, jnp.int32), jnp.asarray(o_tab, jnp.int32), a, b]
    aliases = {}
    if into is not None:
        in_specs.append(pl.BlockSpec(memory_space=pl.ANY))
        args.append(into)
        aliases = {4: 0}
    return pl.pallas_call(
        body, name=name,
        grid_spec=pltpu.PrefetchScalarGridSpec(
            num_scalar_prefetch=2, grid=(len(a_tab), nb, t // tt), in_specs=in_specs,
            out_specs=pl.BlockSpec(out_block, lambda r, j, i, at, ot: out_index(ot[r], j))),
        out_shape=jax.ShapeDtypeStruct(out_shape, F32), input_output_aliases=aliases,
        compiler_params=_cp(("parallel", "parallel", "arbitrary")),
    )(*args)


def _mm_nt(a, w, *, tm, name):
    m, kc = a.shape
    n = w.shape[0]

    def body(a_ref, w_ref, o_ref):
        o_ref[...] = _dot_nt(a_ref[...], w_ref[...])

    return pl.pallas_call(
        body, name=name, grid=(m // tm,),
        in_specs=[pl.BlockSpec((tm, kc), lambda i: (i, 0)), pl.BlockSpec((n, kc), lambda i: (0, 0))],
        out_specs=pl.BlockSpec((tm, n), lambda i: (i, 0)),
        out_shape=jax.ShapeDtypeStruct((m, n), F32),
        compiler_params=_cp(("parallel",)),
    )(a, w)


def _mm_nt_rmsbwd(dy, w, x, gamma, dres, *, tm, tk, name, extra=None):
    m, kc = dy.shape
    nk = kc // tk
    if w.ndim == 3:
        assert w.shape[0] == nk and w.shape[2] == tk
        d = w.shape[1]
        w_spec = pl.BlockSpec((None, d, tk), lambda i, k: (k, 0, 0))
    else:
        d = w.shape[0]
        w_spec = pl.BlockSpec((d, tk), lambda i, k: (0, k))

    def body(*refs):
        if extra is None:
            dy_ref, w_ref, x_ref, g_ref, r_ref, dx_ref, dg_ref, acc_ref = refs
        else:
            dy_ref, w_ref, x_ref, g_ref, r_ref, dy2_ref, w2_ref, dx_ref, dg_ref, acc_ref = refs
        i, kk = pl.program_id(0), pl.program_id(1)

        @pl.when(kk == 0)
        def _():
            acc_ref[...] = jnp.zeros_like(acc_ref)

        @pl.when((i == 0) & (kk == 0))
        def _():
            dg_ref[...] = jnp.zeros_like(dg_ref)

        acc_ref[...] += _dot_nt(dy_ref[...], w_ref[...])

        @pl.when(kk == nk - 1)
        def _():
            dxn = acc_ref[...]
            if extra is not None:
                dxn = dxn + _dot_nt(dy2_ref[...], w2_ref[...])
            xv = x_ref[...]
            r = lax.rsqrt(jnp.mean(xv * xv, axis=-1, keepdims=True) + EPS)
            xh = xv * r
            dg_ref[0:1, :] += _rsum(dxn * xh)
            dxh = dxn * g_ref[...]
            dx_ref[...] = r_ref[...] + r * (dxh - xh * jnp.mean(dxh * xh, axis=-1, keepdims=True))

    in_specs = [pl.BlockSpec((tm, tk), lambda i, k: (i, k)), w_spec,
                pl.BlockSpec((tm, d), lambda i, k: (i, 0)), pl.BlockSpec((1, d), lambda i, k: (0, 0)),
                pl.BlockSpec((tm, d), lambda i, k: (i, 0))]
    args = [dy, w, x, gamma, dres]
    if extra is not None:
        k2 = extra[0].shape[1]
        in_specs += [pl.BlockSpec((tm, k2), lambda i, k: (i, 0)), pl.BlockSpec((d, k2), lambda i, k: (0, 0))]
        args += list(extra)
    return pl.pallas_call(
        body, name=name, grid=(m // tm, nk), in_specs=in_specs,
        out_specs=[pl.BlockSpec((tm, d), lambda i, k: (i, 0)), pl.BlockSpec((8, d), lambda i, k: (0, 0))],
        out_shape=[jax.ShapeDtypeStruct((m, d), F32), jax.ShapeDtypeStruct((8, d), F32)],
        scratch_shapes=[pltpu.VMEM((tm, d), F32)],
        compiler_params=_cp(("arbitrary", "arbitrary")),
    )(*args)


def _rsum8(x):
    acc = x[0:8]
    for g in range(1, x.shape[0] // 8):
        acc = acc + x[8 * g:8 * (g + 1)]
    return acc


def _lru_gates(x, prev8, cw_ref, cb_ref, wa_ref, wx_ref, ba_ref, bx_ref, lam_ref):
    u = _conv4(x, prev8, cw_ref, cb_ref)
    ra =_sigmoid(_dot(u, wa_ref[0]) + ba_ref[...])
    ia = _sigmoid(_dot(u, wx_ref[0]) + bx_ref[...])
    sp = _softplus(-lam_ref[...])
    log_a = -LRU_C * ra * sp
    a = jnp.exp(log_a)
    m2 = _one_minus_exp(2.0 * log_a)
    mult = jnp.sqrt(m2)
    return u, ra, ia, sp, a, m2, mult


def _lru_fwd(proj, lw, *, r, name):
    t = proj.shape[0]
    nt = t // r

    def body(xg_ref, xp_ref, cw_ref, cb_ref, wa_ref, wx_ref, ba_ref, bx_ref, lam_ref, hl_ref, ya_ref, carry_ref):
        i = pl.program_id(1)

        @pl.when(i == 0)
        def _():
            carry_ref[...] = jnp.zeros_like(carry_ref)

        x = xg_ref[:, 0:256]
        lg = xg_ref[:, 256:512]
        prev8 = jnp.where(i == 0, 0.0, xp_ref[:, 0:256])
        u, ra, ia, sp, a, m2, mult = _lru_gates(x, prev8, cw_ref, cb_ref, wa_ref, wx_ref, ba_ref, bx_ref, lam_ref)
        ac, hc = _lin_scan(a, mult * ia * u, False)
        h = hc + ac * carry_ref[0:1, :]
        hl_ref[...] = h
        carry_ref[0:1, :] = hl_ref[r - 1:r, :]
        g, _ = _gelu_and_grad(lg)
        ya_ref[...] = (g * h).astype(ya_ref.dtype)

    small = lambda rows: pl.BlockSpec((rows, 256), lambda j, i: (0, j))
    return pl.pallas_call(
        body, name=name, grid=(4, nt),
        in_specs=[pl.BlockSpec((r, LBLK), lambda j, i: (i, _lblk_col(j))),
                  pl.BlockSpec((8, LBLK), lambda j, i: (jnp.maximum(i * (r // 8) - 1, 0), _lblk_col(j))),
                  small(4), small(1),
                  pl.BlockSpec((1, 256, 256), lambda j, i: (j, 0, 0)), pl.BlockSpec((1, 256, 256), lambda j, i: (j, 0, 0)),
                  small(1), small(1), small(1)],
        out_specs=[pl.BlockSpec((r, 256), lambda j, i: (i, j)), pl.BlockSpec((r, 256), lambda j, i: (i, j))],
        out_shape=[jax.ShapeDtypeStruct((t, D), F32), jax.ShapeDtypeStruct((t, D), _MXU)],
        scratch_shapes=[pltpu.VMEM((8, 256), F32)],
        compiler_params=_cp(("parallel", "arbitrary")),
    )(proj, proj, lw["cw"], lw["cb"], lw["wa"], lw["wx"], lw["ba"], lw["bx"], lw["lam"])


def _lru_bwd(proj, hl, dya, dproj, lw, *, r, name):
    t = proj.shape[0]
    nt = t // r

    def body(xg_ref, xp_ref, hl_ref, hp_ref, dya_ref, cw_ref, cb_ref, wa_ref, wx_ref, ba_ref, bx_ref, lam_ref, dproj_in,
             dproj_ref, sm_ref, dwa_ref, dwx_ref, carry_ref, du8_ref, row_scr):
        del dproj_in
        i = pl.program_id(1)

        @pl.when(i == 0)
        def _():
            carry_ref[...] = jnp.zeros_like(carry_ref)
            du8_ref[...] = jnp.zeros_like(du8_ref)
            sm_ref[...] = jnp.zeros_like(sm_ref)
            dwa_ref[...] = jnp.zeros_like(dwa_ref)
            dwx_ref[...] = jnp.zeros_like(dwx_ref)

        tile0 = i == nt - 1
        xp = xg_ref[:, 0:256]
        lg = xg_ref[:, 256:512]
        prev8 = jnp.where(tile0, 0.0, xp_ref[:, 0:256])
        u, ra, ia, sp, a, m2, mult = _lru_gates(xp, prev8, cw_ref, cb_ref, wa_ref, wx_ref, ba_ref, bx_ref, lam_ref)
        h = hl_ref[...]
        hprev = _shift_down(h, jnp.where(tile0, 0.0, hp_ref[...]), 1)
        dya_v = dya_ref[...]
        g, dg = _gelu_and_grad(lg)
        ac, lc = _lin_scan(_shift_up(a, carry_ref[...], 1), dya_v * g, True)
        lam_v = lc + ac * carry_ref[1:2, :]
        row_scr[0:8, :] = lam_v[0:8]
        row_scr[8:16, :] = a[0:8]
        carry_ref[1:2, :] = row_scr[0:1, :]
        carry_ref[0:1, :] = row_scr[8:9, :]
        da = lam_v * hprev
        dmult = lam_v * ia * u
        dia = lam_v * mult * u
        dlog = da * a - dmult * (1.0 - m2) / mult
        dra = -LRU_C * sp * dlog
        dpa = dra * ra * (1.0 - ra)
        dpx = dia * ia * (1.0 - ia)
        du = lam_v * mult * ia + _dot_nt(dpa, wa_ref[0]) + _dot_nt(dpx, wx_ref[0])
        dwa_ref[0] += _dot_tn(u, dpa)
        dwx_ref[0] += _dot_tn(u, dpx)
        dlx = du * cw_ref[3:4, :]
        sm_ref[24:32, :] += _rsum8(du * xp)
        for k in (1, 2, 3):
            du_k = _shift_up(du, du8_ref[...], k)
            dlx = dlx + du_k * cw_ref[3 - k:4 - k, :]
            sm_ref[8 * (3 - k):8 * (4 - k), :] += _rsum8(du_k * xp)
        du8_ref[...] = du[0:8]
        dproj_ref[:, 0:256] = dlx.astype(dproj_ref.dtype)
        dproj_ref[:, 256:512] = (dya_v * h * dg).astype(dproj_ref.dtype)
        sm_ref[32:40, :] += _rsum8(du)
        sm_ref[40:48, :] += _rsum8(dpa)
        sm_ref[48:56, :] += _rsum8(dpx)
        sm_ref[56:64, :] += _rsum8(-LRU_C * ra * dlog) * (-_sigmoid(-lam_ref[...]))

    rev = lambda i: nt - 1 - i
    small = lambda rows: pl.BlockSpec((rows, 256), lambda j, i: (0, j))
    wblk = pl.BlockSpec((1, 256, 256), lambda j, i: (j, 0, 0))
    n_in = 13
    return pl.pallas_call(
        body, name=name, grid=(4, nt),
        in_specs=[pl.BlockSpec((r, LBLK), lambda j, i: (rev(i), _lblk_col(j))),
                  pl.BlockSpec((8, LBLK), lambda j, i: (jnp.maximum(rev(i) * (r // 8) - 1, 0), _lblk_col(j))),
                  pl.BlockSpec((r, 256), lambda j, i: (rev(i), j)),
                  pl.BlockSpec((8, 256), lambda j, i: (jnp.maximum(rev(i) * (r // 8) - 1, 0), j)),
                  pl.BlockSpec((r, 256), lambda j, i: (rev(i), j)),
                  small(4), small(1), wblk, wblk, small(1), small(1), small(1),
                  pl.BlockSpec(memory_space=pl.ANY)],
        out_specs=[pl.BlockSpec((r, LBLK), lambda j, i: (rev(i), _lblk_col(j))),
                   pl.BlockSpec((64, 256), lambda j, i: (0, j)), wblk, wblk],
        out_shape=[jax.ShapeDtypeStruct(dproj.shape, dproj.dtype), jax.ShapeDtypeStruct((64, D), F32),
                   jax.ShapeDtypeStruct((4, 256, 256), F32), jax.ShapeDtypeStruct((4, 256, 256), F32)],
        scratch_shapes=[pltpu.VMEM((8, 256), F32), pltpu.VMEM((8, 256), F32), pltpu.VMEM((16, 256), F32)],
        input_output_aliases={n_in - 1: 0},
        compiler_params=_cp(("parallel", "arbitrary")),
    )(proj, proj, hl, hl, dya, lw["cw"], lw["cb"], lw["wa"], lw["wx"], lw["ba"], lw["bx"], lw["lam"], dproj)


def _head_cols(x):
    lane = lax.broadcasted_iota(jnp.int32, x.shape, 1)
    return [jnp.sum(jnp.where(lane == h, x, 0.0), axis=1, keepdims=True) for h in range(N_HEADS)]


def _compact_heads(blocks):
    lane = lax.broadcasted_iota(jnp.int32, blocks[0].shape, 1)
    lo = lane < HEAD_P
    out = jnp.zeros_like(blocks[0])
    for j, blk in enumerate(blocks):
        s_lo = jnp.sum(jnp.where(lo, blk, 0.0), axis=1, keepdims=True)
        s_hi = jnp.sum(jnp.where(lo, 0.0, blk), axis=1, keepdims=True)
        out = jnp.where(lane == 2 * j, s_lo, out)
        out = jnp.where(lane == 2 * j + 1, s_hi, out)
    return out


def _ssd_prelude(dtraw_ref, dtb_ref, alog_ref, dt_scr, a_scr):
    lane = lax.broadcasted_iota(jnp.int32, dt_scr.shape, 1)
    dt = jnp.where(lane < N_HEADS, _softplus(dtraw_ref[...] + dtb_ref[0:1, :]), 0.0)
    dt_scr[...] = dt
    a_scr[...] = dt * (-jnp.exp(alog_ref[0:1, :]))


def _ssd_chunk_scalars(dt_scr, a_scr, r_scr, r0):
    a_c = a_scr[pl.ds(r0, CHUNK), :]
    dt_c = dt_scr[pl.ds(r0, CHUNK), :]
    i0 = lax.broadcasted_iota(jnp.int32, (CHUNK, CHUNK), 0)
    i1 = lax.broadcasted_iota(jnp.int32, (CHUNK, CHUNK), 1)
    tri = jnp.where(i0 >= i1, 1.0, 0.0).astype(F32)
    cs = jnp.dot(tri, a_c, precision=_HI, preferred_element_type=F32)
    lane = lax.broadcasted_iota(jnp.int32, (CHUNK, 128), 1)
    srow = lax.broadcasted_iota(jnp.int32, (CHUNK, 128), 0)
    t_lo = jnp.where((lane < HEAD_P) & (srow <= lane), 1.0, 0.0).astype(F32)
    t_hi = jnp.where((lane >= HEAD_P) & (srow <= lane - HEAD_P), 1.0, 0.0).astype(F32)
    even = (lane % 2) == 0
    tn = (((0,), (0,)), ((), ()))
    r_scr[...] = (lax.dot_general(jnp.where(even, a_c, 0.0), t_lo, tn, precision=_HI, preferred_element_type=F32)
                  + lax.dot_general(jnp.where(even, 0.0, a_c), t_hi, tn, precision=_HI, preferred_element_type=F32))
    return cs, dt_c, _head_cols(cs), _head_cols(dt_c)


def _block_diag2(v):
    lo = lax.broadcasted_iota(jnp.int32, v.shape, 1) < HEAD_P
    return jnp.concatenate([jnp.where(lo, v, 0.0), jnp.where(lo, 0.0, v)], axis=0).astype(_MXU)


def _ssd_pair(xc_scr, r_scr, cs_cols, dt_cols, s2, r0, j, s2t=None):
    lane = lax.broadcasted_iota(jnp.int32, (CHUNK, 128), 1)
    srow = lax.broadcasted_iota(jnp.int32, (CHUNK, 128), 0)
    lo = lane < HEAD_P
    csc = jnp.where(lo, cs_cols[2 * j], cs_cols[2 * j + 1])
    dtc = jnp.where(lo, dt_cols[2 * j], dt_cols[2 * j + 1])
    csr = r_scr[2 * j:2 * j + 1, :] + r_scr[2 * j + 1:2 * j + 2, :]
    dm = jnp.where((lane & (HEAD_P - 1)) <= srow, jnp.exp(jnp.minimum(csc - csr, 0.0)), 0.0)
    xs = xc_scr[pl.ds(r0, CHUNK), j * 128:(j + 1) * 128]
    xd = xs * dtc
    csl = jnp.sum(jnp.where(srow == CHUNK - 1, csc, 0.0), axis=0, keepdims=True)
    out = dict(csc=csc, dtc=dtc, dm=dm, m2=s2 * dm, xs=xs, xd=xd, rhs=_block_diag2(xd), e=jnp.exp(csc),
               w=jnp.exp(csl - csc), dec=jnp.exp(csl))
    if s2t is not None:
        out["mt2"] = s2t * jnp.where((lane & (HEAD_P - 1)) >= srow, jnp.exp(jnp.minimum(csr - csc, 0.0)), 0.0)
    return out


def _cat(parts):
    return jnp.concatenate(parts, axis=1)


def _ssd_fwd(proj, dtraw, sw, *, rb, name):
    t = proj.shape[0]
    ns, cb = t // rb, rb // CHUNK

    def body(zx_ref, zp_ref, dtraw_ref, cw_ref, cbias_ref, dtb_ref, alog_ref, dsk_ref, ng_ref,
             yssd_ref, yb_ref, st_ref, h_scr, xc_scr, dt_scr, a_scr, r_scr):
        i = pl.program_id(0)

        @pl.when(i == 0)
        def _():
            h_scr[...] = jnp.zeros_like(h_scr)

        for j in range(XBC // 128):
            cs_, zc = slice(128 * j, 128 * (j + 1)), slice(2048 + 128 * j, 2048 + 128 * (j + 1))
            pre = _conv4(zx_ref[:, zc], jnp.where(i == 0, 0.0, zp_ref[:, zc]), cw_ref, cbias_ref, cs_)
            xc_scr[:, cs_] = pre * _sigmoid(pre)
        _ssd_prelude(dtraw_ref, dtb_ref, alog_ref, dt_scr, a_scr)

        def chunk(c, carry):
            r0 = pl.multiple_of(c * CHUNK, CHUNK)
            _, _, cs_cols, dt_cols = _ssd_chunk_scalars(dt_scr, a_scr, r_scr, r0)
            st_ref[c] = h_scr[...]
            for g in range(N_GROUPS):
                bg = xc_scr[pl.ds(r0, CHUNK), 2048 + 128 * g:2048 + 128 * (g + 1)]
                cg = xc_scr[pl.ds(r0, CHUNK), 2560 + 128 * g:2560 + 128 * (g + 1)]
                s2 = _dot_nt(cg, jnp.concatenate([bg, bg], axis=0))
                hp = h_scr[:, 512 * g:512 * (g + 1)]
                yoff = _dot(cg, hp)
                xdw, dec = [], []
                for jj in range(4):
                    j = 4 * g + jj
                    p = _ssd_pair(xc_scr, r_scr, cs_cols, dt_cols, s2, r0, j)
                    y = _dot(p["m2"], p["rhs"]) + yoff[:, 128 * jj:128 * (jj + 1)] * p["e"]
                    yssd_ref[pl.ds(r0, CHUNK), 128 * j:128 * (j + 1)] = y + dsk_ref[0:1, 128 * j:128 * (j + 1)] * p["xs"]
                    xdw.append(p["xd"] * p["w"])
                    dec.append(p["dec"])
                h_scr[:, 512 * g:512 * (g + 1)] = hp * _cat(dec) + _dot_tn(bg, _cat(xdw))
            return carry

        lax.fori_loop(0, cb, chunk, 0)
        for g in range(N_GROUPS):
            sl = slice(512 * g, 512 * (g + 1))
            for q in range(rb // NORM_ROWS):
                rw = slice(NORM_ROWS * q, NORM_ROWS * (q + 1))
                yz = yssd_ref[rw, sl] * _silu(zx_ref[rw, sl])
                rg = lax.rsqrt(jnp.mean(yz * yz, axis=-1, keepdims=True) + EPS)
                yb_ref[rw, sl] = (yz * rg * ng_ref[0:1, sl]).astype(yb_ref.dtype)

    full = lambda rows, cols: pl.BlockSpec((rows, cols), lambda i: (0, 0))
    return pl.pallas_call(
        body, name=name, grid=(ns,),
        in_specs=[pl.BlockSpec((rb, ZX_W), lambda i: (i, 0)),
                  pl.BlockSpec((8, ZX_W), lambda i: (jnp.maximum(i * (rb // 8) - 1, 0), 0)),
                  pl.BlockSpec((rb, DT_PAD), lambda i: (i, 0)),
                  full(4, XBC), full(1, XBC), full(1, DT_PAD), full(1, DT_PAD), full(1, SSD_INNER), full(1, SSD_INNER)],
        out_specs=[pl.BlockSpec((rb, SSD_INNER), lambda i: (i, 0)), pl.BlockSpec((rb, SSD_INNER), lambda i: (i, 0)),
                   pl.BlockSpec((cb, N_STATE, SSD_INNER), lambda i: (i, 0, 0))],
        out_shape=[jax.ShapeDtypeStruct((t, SSD_INNER), F32), jax.ShapeDtypeStruct((t, SSD_INNER), _MXU),
                   jax.ShapeDtypeStruct((t // CHUNK, N_STATE, SSD_INNER), F32)],
        scratch_shapes=[pltpu.VMEM((N_STATE, SSD_INNER), F32), pltpu.VMEM((rb, XBC), F32), pltpu.VMEM((rb, DT_PAD), F32),
                        pltpu.VMEM((rb, DT_PAD), F32), pltpu.VMEM((128, 128), F32)],
        compiler_params=_cp(("arbitrary",)),
    )(proj, proj, dtraw, sw["cw"], sw["cb"], sw["dtb"], sw["alog"], sw["dsk"], sw["ng"])


def _ssd_bwd(proj, dtraw, yssd, states, dyb, dproj, sw, *, rb, name):
    t = proj.shape[0]
    ns, cb = t // rb, rb // CHUNK

    def body(zx_ref, zp_ref, dtraw_ref, yssd_ref, st_ref, dyb_ref, cw_ref, cbias_ref, dtb_ref, alog_ref, dsk_ref, ng_ref,
             dproj_in, dzx_ref, ddt_ref, gconv_ref, gch_ref, ghd_ref,
             dht_scr, xc_scr, dsl_scr, dy_scr, dxc_scr, dt_scr, a_scr, r_scr, dp8_scr):
        del dproj_in
        i = pl.program_id(0)

        @pl.when(i == 0)
        def _():
            dht_scr[...] = jnp.zeros_like(dht_scr)
            dp8_scr[...] = jnp.zeros_like(dp8_scr)
            gconv_ref[...] = jnp.zeros_like(gconv_ref)
            gch_ref[...] = jnp.zeros_like(gch_ref)
            ghd_ref[...] = jnp.zeros_like(ghd_ref)

        tile0 = i == ns - 1
        for j in range(XBC // 128):
            cs_, zc = slice(128 * j, 128 * (j + 1)), slice(2048 + 128 * j, 2048 + 128 * (j + 1))
            pre = _conv4(zx_ref[:, zc], jnp.where(tile0, 0.0, zp_ref[:, zc]), cw_ref, cbias_ref, cs_)
            sg = _sigmoid(pre)
            xc_scr[:, cs_] = pre * sg
            dsl_scr[:, cs_] = sg * (1.0 + pre * (1.0 - sg))
        _ssd_prelude(dtraw_ref, dtb_ref, alog_ref, dt_scr, a_scr)

        for g in range(N_GROUPS):
            sl = slice(512 * g, 512 * (g + 1))
            for q in range(rb // NORM_ROWS):
                rw = slice(NORM_ROWS * q, NORM_ROWS * (q + 1))
                zv = zx_ref[rw, sl]
                ys = yssd_ref[rw, sl]
                sg = _sigmoid(zv)
                sz = zv * sg
                yz = ys * sz
                rg = lax.rsqrt(jnp.mean(yz * yz, axis=-1, keepdims=True) + EPS)
                yn = yz * rg
                dyb_v = dyb_ref[rw, sl]
                gch_ref[0:8, sl] += _rsum8(dyb_v * yn)
                dyn = dyb_v * ng_ref[0:1, sl]
                dyz = rg * (dyn - yn * jnp.mean(dyn * yn, axis=-1, keepdims=True))
                dy_scr[rw, sl] = dyz * sz
                dzx_ref[rw, sl] = (dyz * ys * (sg * (1.0 + zv * (1.0 - sg)))).astype(dzx_ref.dtype)

        a_row = -jnp.exp(alog_ref[0:1, :])

        def chunk(cc, carry):
            c = cb - 1 - cc
            r0 = pl.multiple_of(c * CHUNK, CHUNK)
            rows = pl.ds(r0, CHUNK)
            _, dt_c, cs_cols, dt_cols = _ssd_chunk_scalars(dt_scr, a_scr, r_scr, r0)
            lane = lax.broadcasted_iota(jnp.int32, (CHUNK, 128), 1)
            srow = lax.broadcasted_iota(jnp.int32, (CHUNK, 128), 0)
            lo = lane < HEAD_P
            last = srow == CHUNK - 1
            p1_blocks, p3_blocks = [], []
            for g in range(N_GROUPS):
                gs = slice(512 * g, 512 * (g + 1))
                bg = xc_scr[rows, 2048 + 128 * g:2048 + 128 * (g + 1)]
                cg = xc_scr[rows, 2560 + 128 * g:2560 + 128 * (g + 1)]
                b2 = jnp.concatenate([bg, bg], axis=0)
                s2 = _dot_nt(cg, b2)
                s2t = _dot_nt(bg, jnp.concatenate([cg, cg], axis=0))
                hp = st_ref[c, :, gs]
                dht = dht_scr[:, gs]
                yoff = _dot(cg, hp)
                ps = [_ssd_pair(xc_scr, r_scr, cs_cols, dt_cols, s2, r0, 4 * g + jj, s2t) for jj in range(4)]
                dys = [dy_scr[rows, 128 * (4 * g + jj):128 * (4 * g + jj + 1)] for jj in range(4)]
                dye = _cat([dys[jj] * ps[jj]["e"] for jj in range(4)])
                w_g = _cat([p["w"] for p in ps])
                dcg = _dot_nt(dye, hp)
                dht_scr[:, gs] = _dot_tn(cg, dye) + _cat([p["dec"] for p in ps]) * dht
                dxd_state = w_g * _dot(bg, dht)
                dbg = _dot_nt(_cat([p["xd"] for p in ps]) * w_g, dht)
                tsum = _rsum(dht * hp)
                ds2 = jnp.zeros((CHUNK, 128), F32)
                for jj in range(4):
                    j = 4 * g + jj
                    ls = slice(128 * j, 128 * (j + 1))
                    p, dy2 = ps[jj], dys[jj]
                    dy_bd = _block_diag2(dy2)
                    dm2 = _dot_nt(dy2, p["rhs"])
                    ds2 = ds2 + dm2 * p["dm"]
                    gdiff = dm2 * p["m2"] - _dot_nt(p["xd"], dy_bd) * p["mt2"]
                    dxs = dxd_state[:, 128 * jj:128 * (jj + 1)]
                    dxd = _dot(p["mt2"], dy_bd) + dxs
                    end_row = _rsum(p["xd"] * dxs) + p["dec"] * tsum[:, 128 * jj:128 * (jj + 1)]
                    p1_blocks.append(gdiff + dy2 * yoff[:, 128 * jj:128 * (jj + 1)] * p["e"] - p["xd"] * dxs
                                     + jnp.where(last, end_row, 0.0))
                    p3_blocks.append(dxd * p["xs"])
                    dxc_scr[rows, ls] = dxd * p["dtc"] + dy2 * dsk_ref[0:1, ls]
                    gch_ref[8:16, ls] += _rsum8(dy2 * p["xs"])
                dcg = dcg + _dot(ds2, b2)
                rb2 = _dot_tn(ds2, cg)
                dxc_scr[rows, 2048 + 128 * g:2048 + 128 * (g + 1)] = dbg + rb2[0:CHUNK] + rb2[CHUNK:2 * CHUNK]
                dxc_scr[rows, 2560 + 128 * g:2560 + 128 * (g + 1)] = dcg
            dcs = _compact_heads(p1_blocks)
            i0 = lax.broadcasted_iota(jnp.int32, (CHUNK, CHUNK), 0)
            i1 = lax.broadcasted_iota(jnp.int32, (CHUNK, CHUNK), 1)
            triu = jnp.where(i1 >= i0, 1.0, 0.0).astype(F32)
            da = jnp.dot(triu, dcs, precision=_HI, preferred_element_type=F32)
            ddt = _compact_heads(p3_blocks) + da * a_row
            ddtraw = jnp.where(lane < N_HEADS, ddt * _sigmoid(dtraw_ref[rows, :] + dtb_ref[0:1, :]), 0.0)
            ddt_ref[rows, :] = ddtraw.astype(ddt_ref.dtype)
            ghd_ref[0:1, :] += _rsum(ddtraw)
            ghd_ref[1:2, :] += _rsum(da * dt_c) * a_row
            return carry

        lax.fori_loop(0, cb, chunk, 0)
        for j in range(XBC // 128):
            cs_, zc = slice(128 * j, 128 * (j + 1)), slice(2048 + 128 * j, 2048 + 128 * (j + 1))
            dpre = dxc_scr[:, cs_] * dsl_scr[:, cs_]
            xraw = zx_ref[:, zc]
            dx = dpre * cw_ref[3:4, cs_]
            gconv_ref[24:32, cs_] += _rsum8(dpre * xraw)
            for k in (1, 2, 3):
                dpre_k = _shift_up(dpre, dp8_scr[:, cs_], k)
                dx = dx + dpre_k * cw_ref[3 - k:4 - k, cs_]
                gconv_ref[8 * (3 - k):8 * (4 - k), cs_] += _rsum8(dpre_k * xraw)
            dzx_ref[:, zc] = dx.astype(dzx_ref.dtype)
            dp8_scr[:, cs_] = dpre[0:8]
            gconv_ref[32:40, cs_] += _rsum8(dpre)

    rev = lambda i: ns - 1 - i
    full = lambda rows, cols: pl.BlockSpec((rows, cols), lambda i: (0, 0))
    n_in = 13
    return pl.pallas_call(
        body, name=name, grid=(ns,),
        in_specs=[pl.BlockSpec((rb, ZX_W), lambda i: (rev(i), 0)),
                  pl.BlockSpec((8, ZX_W), lambda i: (jnp.maximum(rev(i) * (rb // 8) - 1, 0), 0)),
                  pl.BlockSpec((rb, DT_PAD), lambda i: (rev(i), 0)),
                  pl.BlockSpec((rb, SSD_INNER), lambda i: (rev(i), 0)),
                  pl.BlockSpec((cb, N_STATE, SSD_INNER), lambda i: (rev(i), 0, 0)),
                  pl.BlockSpec((rb, SSD_INNER), lambda i: (rev(i), 0)),
                  full(4, XBC), full(1, XBC), full(1, DT_PAD), full(1, DT_PAD), full(1, SSD_INNER), full(1, SSD_INNER),
                  pl.BlockSpec(memory_space=pl.ANY)],
        out_specs=[pl.BlockSpec((rb, ZX_W), lambda i: (rev(i), 0)), pl.BlockSpec((rb, DT_PAD), lambda i: (rev(i), 0)),
                   full(40, XBC), full(16, SSD_INNER), full(8, DT_PAD)],
        out_shape=[jax.ShapeDtypeStruct(dproj.shape, dproj.dtype), jax.ShapeDtypeStruct((t, DT_PAD), _MXU),
                   jax.ShapeDtypeStruct((40, XBC), F32), jax.ShapeDtypeStruct((16, SSD_INNER), F32),
                   jax.ShapeDtypeStruct((8, DT_PAD), F32)],
        scratch_shapes=[pltpu.VMEM((N_STATE, SSD_INNER), F32), pltpu.VMEM((rb, XBC), F32), pltpu.VMEM((rb, XBC), F32),
                        pltpu.VMEM((rb, SSD_INNER), F32), pltpu.VMEM((rb, XBC), F32), pltpu.VMEM((rb, DT_PAD), F32),
                        pltpu.VMEM((rb, DT_PAD), F32), pltpu.VMEM((128, 128), F32), pltpu.VMEM((8, XBC), F32)],
        input_output_aliases={n_in - 1: 0},
        compiler_params=_cp(("arbitrary",)),
    )(proj, proj, dtraw, yssd, states, dyb, sw["cw"], sw["cb"], sw["dtb"], sw["alog"], sw["dsk"], sw["ng"], dproj)


def _branch_merge(ya, yb, proj, wba, wbb, bgate, *, tm, tn, name):
    t = ya.shape[0]
    nj = D // tn

    def body(ya_ref, yb_ref, ga_ref, gb_ref, wba_ref, wbb_ref, ba_ref, bb_ref, ta_ref, tb_ref, mg_ref):
        ta = _dot(ya_ref[...], wba_ref[...])
        tb = _dot(yb_ref[...], wbb_ref[...])
        ta_ref[...] = ta
        tb_ref[...] = tb
        ga = _sigmoid(ga_ref[...] + ba_ref[...])
        gb = _sigmoid(gb_ref[...] + bb_ref[...])
        mg_ref[...] = (ga * ta + gb * tb).astype(mg_ref.dtype)

    tile = pl.BlockSpec((tm, tn), lambda i, j: (i, j))
    return pl.pallas_call(
        body, name=name, grid=(t // tm, nj),
        in_specs=[pl.BlockSpec((tm, D), lambda i, j: (i, 0)), pl.BlockSpec((tm, SSD_INNER), lambda i, j: (i, 0)),
                  pl.BlockSpec((tm, tn), lambda i, j: (i, G0 // tn + j)),
                  pl.BlockSpec((tm, tn), lambda i, j: (i, (G0 + D) // tn + j)),
                  pl.BlockSpec((D, tn), lambda i, j: (0, j)), pl.BlockSpec((SSD_INNER, tn), lambda i, j: (0, j)),
                  pl.BlockSpec((1, tn), lambda i, j: (0, j)), pl.BlockSpec((1, tn), lambda i, j: (0, nj + j))],
        out_specs=[tile, tile, tile],
        out_shape=[jax.ShapeDtypeStruct((t, D), F32), jax.ShapeDtypeStruct((t, D), F32), jax.ShapeDtypeStruct((t, D), _MXU)],
        compiler_params=_cp(("parallel", "parallel")),
    )(ya, yb, proj, proj, wba, wbb, bgate, bgate)


def _swiglu_mm(gu, wfo, residual, *, tm, tn, name):
    t = gu.shape[0]

    def body(gu_ref, w_ref, r_ref, act_ref, o_ref):
        @pl.when(pl.program_id(1) == 0)
        def _():
            act_ref[...] = (_silu(gu_ref[:, 0:D_FF]) * gu_ref[:, D_FF:2 * D_FF]).astype(act_ref.dtype)
        o_ref[...] = jnp.dot(act_ref[...], w_ref[...], preferred_element_type=F32) + r_ref[...]

    return pl.pallas_call(
        body, name=name, grid=(t // tm, D // tn),
        in_specs=[pl.BlockSpec((tm, 2 * D_FF), lambda i, j: (i, 0)), pl.BlockSpec((D_FF, tn), lambda i, j: (0, j)),
                  pl.BlockSpec((tm, tn), lambda i, j: (i, j))],
        out_specs=[pl.BlockSpec((tm, D_FF), lambda i, j: (i, 0)), pl.BlockSpec((tm, tn), lambda i, j: (i, j))],
        out_shape=[jax.ShapeDtypeStruct((t, D_FF), _MXU), jax.ShapeDtypeStruct((t, D), F32)],
        compiler_params=_cp(("parallel", "arbitrary")),
    )(gu, wfo, residual)


def _ffn_bwd_act(dh, wfo, gu, *, tm, name):
    t = dh.shape[0]

    def body(dh_ref, w_ref, gu_ref, o_ref):
        dact = _dot_nt(dh_ref[...], w_ref[...])
        g = gu_ref[:, 0:D_FF]
        u = gu_ref[:, D_FF:2 * D_FF]
        o_ref[:, 0:D_FF] = (dact * u * _dsilu(g)).astype(o_ref.dtype)
        o_ref[:, D_FF:2 * D_FF] = (dact * _silu(g)).astype(o_ref.dtype)

    return pl.pallas_call(
        body, name=name, grid=(t // tm,),
        in_specs=[pl.BlockSpec((tm, D), lambda i: (i, 0)), pl.BlockSpec((D_FF, D), lambda i: (0, 0)),
                  pl.BlockSpec((tm, 2 * D_FF), lambda i: (i, 0))],
        out_specs=pl.BlockSpec((tm, 2 * D_FF), lambda i: (i, 0)),
        out_shape=jax.ShapeDtypeStruct((t, 2 * D_FF), _MXU),
        compiler_params=_cp(("parallel",)),
    )(dh, wfo, gu)


def _outproj_bwd(dh, wout, ta, tb, proj, bgate, dproj, *, tm, name):
    t = dh.shape[0]

    def body(dh_ref, w_ref, ta_ref, tb_ref, g_ref, b_ref, dta_ref, dtb_ref, dg_ref, db_ref):
        @pl.when(pl.program_id(0) == 0)
        def _():
            db_ref[...] = jnp.zeros_like(db_ref)
        dm = _dot_nt(dh_ref[...], w_ref[...])
        ga = _sigmoid(g_ref[:, 0:D] + b_ref[:, 0:D])
        gb = _sigmoid(g_ref[:, D:2 * D] + b_ref[:, D:2 * D])
        dta_ref[...] = (dm * ga).astype(dta_ref.dtype)
        dtb_ref[...] = (dm * gb).astype(dtb_ref.dtype)
        dga = dm * ta_ref[...] * ga * (1.0 - ga)
        dgb = dm * tb_ref[...] * gb * (1.0 - gb)
        dg_ref[:, 0:D] = dga.astype(dg_ref.dtype)
        dg_ref[:, D:2 * D] = dgb.astype(dg_ref.dtype)
        db_ref[0:1, 0:D] += _rsum(dga)
        db_ref[0:1, D:2 * D] += _rsum(dgb)

    row = lambda cols: pl.BlockSpec((tm, cols), lambda i: (i, 0))
    return pl.pallas_call(
        body, name=name, grid=(t // tm,),
        in_specs=[row(D), pl.BlockSpec((D, D), lambda i: (0, 0)), row(D), row(D),
                  pl.BlockSpec((tm, 2 * D), lambda i: (i, G0 // (2 * D))), pl.BlockSpec((1, 2 * D), lambda i: (0, 0))],
        out_specs=[row(D), row(D), pl.BlockSpec((tm, 2 * D), lambda i: (i, G0 // (2 * D))),
                   pl.BlockSpec((8, 2 * D), lambda i: (0, 0))],
        out_shape=[jax.ShapeDtypeStruct((t, D), _MXU), jax.ShapeDtypeStruct((t, D), _MXU),
                   jax.ShapeDtypeStruct(dproj, _MXU), jax.ShapeDtypeStruct((8, 2 * D), F32)],
        compiler_params=_cp(("arbitrary",)),
    )(dh, wout, ta, tb, proj, bgate)


def _loss_head(h, gf, target, *, tm, name):
    t = h.shape[0]

    def body(h_ref, g_ref, t_ref, loss_ref, dg_ref, dh_ref):
        @pl.when(pl.program_id(0) == 0)
        def _():
            loss_ref[...] = jnp.zeros_like(loss_ref)
            dg_ref[...] = jnp.zeros_like(dg_ref)
        x = h_ref[...]
        r = lax.rsqrt(jnp.mean(x * x, axis=-1, keepdims=True) + EPS)
        xh = x * r
        err = xh * g_ref[...] - t_ref[...]
        loss_ref[...] += 0.5 * jnp.sum(jnp.mean(err * err, axis=-1, keepdims=True), axis=0, keepdims=True)
        dy = err * (1.0 / D)
        dg_ref[0:1, :] += _rsum(dy * xh)
        dxh = dy * g_ref[...]
        dh_ref[...] = r * (dxh - xh * jnp.mean(dxh * xh, axis=-1, keepdims=True))

    row = pl.BlockSpec((tm, D), lambda i: (i, 0))
    return pl.pallas_call(
        body, name=name, grid=(t // tm,),
        in_specs=[row, pl.BlockSpec((1, D), lambda i: (0, 0)), row],
        out_specs=[pl.BlockSpec((8, 128), lambda i: (0, 0)), pl.BlockSpec((8, D), lambda i: (0, 0)), row],
        out_shape=[jax.ShapeDtypeStruct((8, 128), F32), jax.ShapeDtypeStruct((8, D), F32), jax.ShapeDtypeStruct((t, D), F32)],
        compiler_params=_cp(("arbitrary",)),
    )(h, gf, target)


def _row_tile(rows, cols, limit_bytes=1 << 20):
    best = None
    for tr in range(8, rows + 1, 8):
        if rows % tr == 0 and tr * cols * 4 <= limit_bytes:
            best = tr
    return best if best is not None else rows


def _adamw(w, g, m, v, *, name):
    rows, cols = w.shape
    tr = _row_tile(rows, cols)

    def body(w_ref, g_ref, m_ref, v_ref, d_ref, nm_ref, nv_ref):
        gv = g_ref[...]
        nm = ADAM_B1 * m_ref[...] + (1.0 - ADAM_B1) * gv
        nv = ADAM_B2 * v_ref[...] + (1.0 - ADAM_B2) * (gv * gv)
        m_hat = nm / (1.0 - ADAM_B1 ** ADAM_STEP)
        v_hat = nv / (1.0 - ADAM_B2 ** ADAM_STEP)
        d_ref[...] = -ADAM_LR * (m_hat / (jnp.sqrt(v_hat) + ADAM_EPS) + ADAM_WD * w_ref[...])
        nm_ref[...] = nm
        nv_ref[...] = nv

    blk = pl.BlockSpec((tr, cols), lambda i: (i, 0))
    shp = jax.ShapeDtypeStruct((rows, cols), F32)
    return pl.pallas_call(
        body, name=name, grid=(rows // tr,), in_specs=[blk] * 4, out_specs=[blk] * 3, out_shape=[shp] * 3,
        compiler_params=_cp(("parallel",)),
    )(w, g, m, v)


def _bd256(w):
    w4 = w.reshape(4, 4, 64, 64)
    eye = jnp.eye(4, dtype=w.dtype)
    return (w4[:, :, :, None, :] * eye[None, :, None, :, None]).reshape(4, 256, 256)


def _bd256_diag(g):
    g5 = g.reshape(4, 4, 64, 4, 64)
    return jnp.stack([g5[:, a, :, a, :] for a in range(4)], axis=1).reshape(16, 64, 64)


FFN_SHARD = 2 * D_FF // 4
W_IN_SHARD = IN_DIM // 4
W_IN_ROWS = 9344


def _w_in_cols(shards, c0, c1):
    out = []
    for p in range(4):
        lo, hi = max(c0, W_IN_SHARD * p), min(c1, W_IN_SHARD * (p + 1))
        if lo < hi:
            out.append(shards[p][:, lo - W_IN_SHARD * p:hi - W_IN_SHARD * p])
    return out


def _layer_weights(w, small, l):
    win = w["w_in"][:, l]
    lblk = [_w_in_cols(win, 256 * j, 256 * (j + 1)) + _w_in_cols(win, D + 256 * j, D + 256 * (j + 1)) for j in range(4)]
    wp = jnp.concatenate(_w_in_cols(win, 2048, 4096) + _w_in_cols(win, 4096, 7168) + lblk[0] + lblk[1]
                         + _w_in_cols(win, 7200, 9248) + lblk[2] + lblk[3], axis=1)
    wdt = jnp.pad(jnp.concatenate(_w_in_cols(win, 7168, 7200), axis=1), ((0, 0), (0, DT_PAD - N_HEADS)))
    row = lambda v: v.reshape(1, -1)
    pad_h = lambda v: jnp.pad(v.reshape(1, -1), ((0, 0), (0, DT_PAD - N_HEADS)))
    lw = dict(cw=w["lru_conv_w"][l], cb=row(small["lru_conv_b"][l]),
              wa=_bd256(small["lru_w_a"][l]).astype(_MXU), wx=_bd256(small["lru_w_x"][l]).astype(_MXU),
              ba=row(small["lru_b_a"][l]), bx=row(small["lru_b_x"][l]), lam=row(small["lru_lambda"][l]))
    sw = dict(cw=w["ssd_conv_w"][l], cb=row(small["ssd_conv_b"][l]), dtb=pad_h(small["ssd_dt_bias"][l]),
              alog=pad_h(small["ssd_A_log"][l]), dsk=row(jnp.repeat(small["ssd_D"][l], HEAD_P)),
              ng=row(small["ssd_norm_g"][l]))
    return dict(wp=wp, wdt=wdt, lw=lw, sw=sw, wba=w["w_branch"][l][0:D], wbb=w["w_branch"][l][D:3 * D],
                wout=w["w_out"][l], wfi=w["w_ffn_in"][l], wfo=w["w_ffn_out"][l],
                g1=row(small["norm1_g"][l]), g2=row(small["norm2_g"][l]), bgate=row(small["b_gate"][l]))


def _tiles(t):
    return dict(tmn=min(1024, t), tm=min(512, t), tm2=min(256, t), r=min(256, t), rb=min(128, t))


def _layer_fwd(h, lwt, l):
    tl = _tiles(h.shape[0])
    n = f"l{l}_"
    xn, proj = _norm_mm(h, lwt["g1"], lwt["wp"], tm=tl["tmn"], tn=1024, name=n + "in_proj")
    dtraw = _mm_nn(xn, lwt["wdt"], tm=tl["tm"], tn=DT_PAD, name=n + "dt_proj")
    hl, ya = _lru_fwd(proj, lwt["lw"], r=tl["r"], name=n + "lru_fwd")
    yssd, yb, states = _ssd_fwd(proj, dtraw, lwt["sw"], rb=tl["rb"], name=n + "ssd_fwd")
    ta, tb, merged = _branch_merge(ya, yb, proj, lwt["wba"], lwt["wbb"], lwt["bgate"], tm=tl["tm"], tn=512, name=n + "merge")
    hmid = _mm_nn(merged, lwt["wout"], tm=tl["tm"], tn=512, name=n + "out_proj", residual=h)
    xn2, gu = _norm_mm(hmid, lwt["g2"], lwt["wfi"], tm=tl["tmn"], tn=FFN_SHARD, name=n + "ffn_in")
    act, hout = _swiglu_mm(gu, lwt["wfo"], hmid, tm=tl["tm2"], tn=512, name=n + "ffn_out")
    saved = dict(h=h, xn=xn, proj=proj, dtraw=dtraw, hl=hl, ya=ya, yssd=yssd, yb=yb, states=states, ta=ta, tb=tb,
                 merged=merged, hmid=hmid, xn2=xn2, gu=gu, act=act)
    return hout, saved


def _layer_bwd(dh, s, lwt, l, big):
    t = dh.shape[0]
    tl = _tiles(t)
    n = f"l{l}_"
    tt = tl["tm"]
    big = dict(big)

    def wgrad(key, a, b, name, **kw):
        big[key] = _wgrad(a, b, tt=tt, name=n + name, into=big.get(key), **kw)

    dgu = _ffn_bwd_act(dh, lwt["wfo"], s["gu"], tm=tl["tm2"], name=n + "ffn_act_bwd")
    wgrad("w_ffn_out", s["act"], dh, "ffn_out_wgrad", ta=D_FF, tn=1024, out_shape=(N_LAYERS, D_FF, D),
          out_block=(None, D_FF, 1024), out_index=lambda o, j: (l, o, j))
    wgrad("w_ffn_in", s["xn2"], dgu, "ffn_in_wgrad", ta=D, tn=FFN_SHARD, out_shape=(N_LAYERS, 4, D, FFN_SHARD),
          out_block=(None, None, D, FFN_SHARD), out_index=lambda o, j: (l, j, o, 0))
    dh1, dg2 = _mm_nt_rmsbwd(dgu, lwt["wfi"], s["hmid"], lwt["g2"], dh, tm=tl["tm"], tk=FFN_SHARD, name=n + "ffn_in_dgrad")
    dta, dtb, dproj, dbg = _outproj_bwd(dh1, lwt["wout"], s["ta"], s["tb"], s["proj"], lwt["bgate"], (t, NP),
                                        tm=tl["tm2"], name=n + "out_proj_bwd")
    rows_d = dict(ta=D, tn=512, out_block=(None, D, 512), out_index=lambda o, j: (l, o, j))
    wgrad("w_out", s["merged"], dh1, "out_proj_wgrad", out_shape=(N_LAYERS, D, D), **rows_d)
    dya = _mm_nt(dta, lwt["wba"], tm=tl["tm"], name=n + "branch_a_dgrad")
    dyb = _mm_nt(dtb, lwt["wbb"], tm=tl["tm"], name=n + "branch_b_dgrad")
    wgrad("w_branch", s["ya"], dta, "branch_a_wgrad", out_shape=(N_LAYERS, 3 * D, D), a_tab=[0], o_tab=[0], **rows_d)
    wgrad("w_branch", s["yb"], dtb, "branch_b_wgrad", out_shape=(N_LAYERS, 3 * D, D), a_tab=[0, 1], o_tab=[1, 2], **rows_d)
    dproj, lsm, dwa, dwx = _lru_bwd(s["proj"], s["hl"], dya, dproj, lwt["lw"], r=tl["r"], name=n + "lru_bwd")
    dproj, ddt, gconv, gch, ghd = _ssd_bwd(s["proj"], s["dtraw"], s["yssd"], s["states"], dyb, dproj, lwt["sw"],
                                           rb=tl["rb"], name=n + "ssd_bwd")
    lsm = lsm.reshape(8, 8, D).sum(axis=1)
    gconv = gconv.reshape(5, 8, XBC).sum(axis=1)
    gch = gch.reshape(2, 8, SSD_INNER).sum(axis=1)
    w_in = dict(tn=D, out_shape=(N_LAYERS, W_IN_ROWS, D), out_index=lambda o, j: (l, o, j))
    wgrad("w_in", dproj, s["xn"], "in_proj_wgrad", ta=1024, out_block=(None, 1024, D),
          a_tab=[0, 1, 2, 3, 4, 6, 7], o_tab=[2, 3, 4, 5, 6, 7, 8], **w_in)
    wgrad("w_in", dproj, s["xn"], "in_proj_lru_wgrad", ta=256, out_block=(None, 256, D),
          a_tab=[2 * _lblk_col(j) + part for j in range(4) for part in range(2)],
          o_tab=[4 * part + j for j in range(4) for part in range(2)], **w_in)
    wgrad("w_in", ddt, s["xn"], "dt_proj_wgrad", ta=DT_PAD, out_block=(None, DT_PAD, D), a_tab=[0],
          o_tab=[NP // DT_PAD], **w_in)
    dh0, dg1 = _mm_nt_rmsbwd(dproj, lwt["wp"], s["h"], lwt["g1"], dh1, tm=tl["tm"], tk=2304, name=n + "in_proj_dgrad",
                             extra=(ddt, lwt["wdt"]))
    grads = dict(
        lru_conv_w=lsm[0:4], lru_conv_b=lsm[4], lru_b_a=lsm[5], lru_b_x=lsm[6], lru_lambda=lsm[7],
        lru_w_a=_bd256_diag(dwa), lru_w_x=_bd256_diag(dwx),
        ssd_conv_w=gconv[0:4], ssd_conv_b=gconv[4], ssd_norm_g=gch[0], ssd_D=gch[1].reshape(N_HEADS, HEAD_P).sum(axis=-1),
        ssd_dt_bias=ghd[0, 0:N_HEADS], ssd_A_log=ghd[1, 0:N_HEADS],
        b_gate=dbg[0], norm1_g=dg1[0], norm2_g=dg2[0])
    return dh0, grads, big


def _local_step(x, target, w, small):
    h = x
    lwts, saved = [], []
    for l in range(N_LAYERS):
        lwt = _layer_weights(w, small, l)
        h, s = _layer_fwd(h, lwt, l)
        lwts.append(lwt)
        saved.append(s)
    loss_blk, dgf, dh = _loss_head(h, small["norm_f"].reshape(1, D), target, tm=_tiles(x.shape[0])["tm"], name="loss_head")
    per_layer, big = [None] * N_LAYERS, {}
    for l in reversed(range(N_LAYERS)):
        dh, per_layer[l], big = _layer_bwd(dh, saved[l], lwts[l], l, big)
    grads = {k: jnp.stack([per_layer[l][k] for l in range(N_LAYERS)], axis=0) for k in per_layer[0]}
    grads["norm_f"] = dgf[0]
    return loss_blk, dh, grads, big


PACK_W = 1024
BIG = (("w_in", W_IN_SHARD, D, W_IN_SHARD, 256), ("w_branch", 768, D, 256, D), ("w_out", 256, D, 256, D),
       ("w_ffn_in", D, FFN_SHARD, 256, FFN_SHARD), ("w_ffn_out", 704, D, 352, D))
CONV = ("lru_conv_w", "ssd_conv_w")
SMALL = ("norm1_g", "b_gate", "lru_conv_b", "lru_w_a", "lru_b_a", "lru_w_x", "lru_b_x", "lru_lambda", "ssd_conv_b",
         "ssd_dt_bias", "ssd_A_log", "ssd_D", "ssd_norm_g", "norm2_g", "norm_f")
_WIRE = jnp.bfloat16
N_CHIPS = 4
N_DEV = 8


def _mesh_pos():
    return lax.axis_index("x"), lax.axis_index("y"), lax.axis_index("c")


HBM_SPEC = pl.BlockSpec(memory_space=pltpu.HBM)


def _remote(src, dst, send_sems, recv_sems, k, to):
    return pltpu.make_async_remote_copy(src_ref=src, dst_ref=dst, send_sem=send_sems.at[k], recv_sem=recv_sems.at[k],
                                        device_id=to, device_id_type=MESH)


def _other_chips(x, y):
    return [(1 - x, y), (x, 1 - y), (1 - x, 1 - y)]


def _gather_weights(loc):
    n = len(loc)
    rows = [None, loc[1].shape[1], loc[2].shape[1], None, loc[4].shape[1]]
    shapes = [(N_CHIPS,) + loc[0].shape, (2, N_CHIPS * rows[1], D), (2, N_CHIPS * rows[2], D),
              (2, N_CHIPS) + loc[3].shape[1:], (2, N_CHIPS * rows[4], D)]

    def place(o_refs, k, chip, layer):
        if k == 0:
            return o_refs[0].at[chip, layer]
        if k == 3:
            return o_refs[3].at[layer, chip]
        return o_refs[k].at[layer, pl.ds(pl.multiple_of(chip * rows[k], 16), rows[k]), :]

    def body(*refs):
        in_refs, o_refs, (send_sems, recv_sems) = refs[:n], refs[n:2 * n], refs[2 * n:]
        x, y, c = _mesh_pos()
        s = 2 * x + y
        sib = (x, y, 1 - c)
        chips = _other_chips(x, y)
        first = [_remote(in_refs[k].at[c], place(o_refs, k, s, c), send_sems, recv_sems, n * j + k, (px, py, c))
                 for j, (px, py) in enumerate(chips) for k in range(n)]
        for cp in first:
            cp.start()
        passed = []
        for j, (px, py) in enumerate(chips):
            for k in range(n):
                landed = place(o_refs, k, 2 * px + py, c)
                _remote(in_refs[k].at[c], landed, send_sems, recv_sems, n * j + k, (px, py, c)).wait_recv()
                cp = _remote(landed, landed, send_sems, recv_sems, n * (3 + j) + k, sib)
                cp.start()
                passed.append(cp)
        for j, (px, py) in enumerate(chips):
            for k in range(n):
                landed = place(o_refs, k, 2 * px + py, 1 - c)
                _remote(landed, landed, send_sems, recv_sems, n * (3 + j) + k, sib).wait_recv()
        for cp in first + passed:
            cp.wait_send()

    return pl.pallas_call(
        body, name="allgather_weights", in_specs=[HBM_SPEC] * n, out_specs=[HBM_SPEC] * n,
        out_shape=[jax.ShapeDtypeStruct(shp, v.dtype) for shp, v in zip(shapes, loc)],
        scratch_shapes=[pltpu.SemaphoreType.DMA((6 * n,)), pltpu.SemaphoreType.DMA((6 * n,))],
    )(*loc)


def _sibling_exchange(gbufs):
    n = len(gbufs)

    def body(*refs):
        g_refs, o_refs, (send_sems, recv_sems) = refs[:n], refs[n:2 * n], refs[2 * n:]
        x, y, c = _mesh_pos()
        copies = [_remote(g_refs[k].at[1 - c], o_refs[k], send_sems, recv_sems, k, (x, y, 1 - c)) for k in range(n)]
        for cp in copies:
            cp.start()
        for cp in copies:
            cp.wait()

    return pl.pallas_call(
        body, name="grad_sibling_exchange", in_specs=[HBM_SPEC] * n, out_specs=[HBM_SPEC] * n,
        out_shape=[jax.ShapeDtypeStruct(g.shape[1:], g.dtype) for g in gbufs],
        scratch_shapes=[pltpu.SemaphoreType.DMA((n,)), pltpu.SemaphoreType.DMA((n,))],
    )(*gbufs)


def _add_layer(g, recv, c_idx, *, a, tr, tc, name):
    wd = g.shape[2]
    nr = a // tr

    def body(c_ref, g_ref, r_ref, o_ref):
        del c_ref
        o_ref[...] = (g_ref[...] + r_ref[...]).astype(o_ref.dtype)

    return pl.pallas_call(
        body, name=name,
        grid_spec=pltpu.PrefetchScalarGridSpec(
            num_scalar_prefetch=1, grid=(N_CHIPS, nr, wd // tc),
            in_specs=[pl.BlockSpec((None, tr, tc), lambda p, i, j, c_ref: (c_ref[0], p * nr + i, j)),
                      pl.BlockSpec((tr, tc), lambda p, i, j, c_ref: (p * nr + i, j))],
            out_specs=pl.BlockSpec((None, tr, tc), lambda p, i, j, c_ref: (p, i, j))),
        out_shape=jax.ShapeDtypeStruct((N_CHIPS, a, wd), _WIRE),
        compiler_params=_cp(("parallel", "parallel", "parallel")),
    )(c_idx, g, recv)


def _chip_exchange(parts):
    n = len(parts)

    def body(*refs):
        s_refs, o_refs, (send_sems, recv_sems) = refs[:n], refs[n:2 * n], refs[2 * n:]
        x, y, c = _mesh_pos()
        s = 2 * x + y
        chips = _other_chips(x, y)
        sends = [_remote(s_refs[k].at[2 * px + py], o_refs[k].at[s], send_sems, recv_sems, n * j + k, (px, py, c))
                 for j, (px, py) in enumerate(chips) for k in range(n)]
        for cp in sends:
            cp.start()
        for j, (px, py) in enumerate(chips):
            for k in range(n):
                p = 2 * px + py
                _remote(s_refs[k].at[p], o_refs[k].at[p], send_sems, recv_sems, n * j + k, (px, py, c)).wait_recv()
        for cp in sends:
            cp.wait_send()

    return pl.pallas_call(
        body, name="grad_chip_exchange", in_specs=[HBM_SPEC] * n, out_specs=[HBM_SPEC] * n,
        out_shape=[jax.ShapeDtypeStruct(p.shape, p.dtype) for p in parts],
        scratch_shapes=[pltpu.SemaphoreType.DMA((3 * n,)), pltpu.SemaphoreType.DMA((3 * n,))],
    )(*parts)


def _sum_slots(slots, own, sel, *, tr, tc, name, halves=False):
    n, rows, wd = slots.shape
    k = own.shape[0]

    def body(sel_ref, s_ref, own_ref, o_ref):
        mine = sel_ref[0]
        acc = jnp.zeros((tr, tc), F32)
        for p in range(n):
            acc = acc + jnp.where(mine == p, own_ref[...].astype(F32), s_ref[p].astype(F32))
        o_ref[...] = acc

    if halves:
        out_spec = pl.BlockSpec((None, tr, tc), lambda i, j, sel_ref: (sel_ref[1], i, j))
        out_shape = jax.ShapeDtypeStruct((2, rows, wd), F32)
    else:
        out_spec = pl.BlockSpec((tr, tc), lambda i, j, sel_ref: (i, j))
        out_shape = jax.ShapeDtypeStruct((rows, wd), F32)
    return pl.pallas_call(
        body, name=name,
        grid_spec=pltpu.PrefetchScalarGridSpec(
            num_scalar_prefetch=1, grid=(rows // tr, wd // tc),
            in_specs=[pl.BlockSpec((n, tr, tc), lambda i, j, sel_ref: (0, i, j)),
                      pl.BlockSpec((None, tr, tc), lambda i, j, sel_ref: (sel_ref[0] if k > 1 else 0, i, j))],
            out_specs=out_spec),
        out_shape=out_shape, compiler_params=_cp(("parallel", "parallel")),
    )(sel, slots, own)


def _sibling_share(both):
    n = len(both)

    def body(*refs):
        o_refs, (send_sems, recv_sems) = refs[n:2 * n], refs[2 * n:]
        x, y, c = _mesh_pos()
        sends = [_remote(o_refs[k].at[c], o_refs[k].at[c], send_sems, recv_sems, k, (x, y, 1 - c)) for k in range(n)]
        for cp in sends:
            cp.start()
        for k in range(n):
            _remote(o_refs[k].at[1 - c], o_refs[k].at[1 - c], send_sems, recv_sems, k, (x, y, 1 - c)).wait_recv()
        for cp in sends:
            cp.wait_send()

    return pl.pallas_call(
        body, name="grad_sibling_share", in_specs=[HBM_SPEC] * n, out_specs=[HBM_SPEC] * n,
        out_shape=[jax.ShapeDtypeStruct(b.shape, b.dtype) for b in both], input_output_aliases={k: k for k in range(n)},
        scratch_shapes=[pltpu.SemaphoreType.DMA((n,)), pltpu.SemaphoreType.DMA((n,))],
    )(*both)


def _allgather_devices(part, name):
    rows, wd = part.shape

    def body(p_ref, o_ref, send_sems, recv_sems):
        x, y, c = _mesh_pos()
        me = 4 * x + 2 * y + c
        peers = []
        for k in range(1, N_DEV):
            dx, dy, dc = (k >> 2) & 1, (k >> 1) & 1, k & 1
            peers.append(((x + dx) % 2, (y + dy) % 2, (c + dc) % 2))
        sends = [_remote(p_ref, o_ref.at[me], send_sems, recv_sems, k, to) for k, to in enumerate(peers)]
        for cp in sends:
            cp.start()
        for k, (px, py, pc) in enumerate(peers):
            _remote(p_ref, o_ref.at[4 * px + 2 * py + pc], send_sems, recv_sems, k, (px, py, pc)).wait_recv()
        for cp in sends:
            cp.wait_send()

    return pl.pallas_call(
        body, name=name, in_specs=[HBM_SPEC], out_specs=HBM_SPEC,
        out_shape=jax.ShapeDtypeStruct((N_DEV, rows, wd), part.dtype),
        scratch_shapes=[pltpu.SemaphoreType.DMA((N_DEV - 1,)), pltpu.SemaphoreType.DMA((N_DEV - 1,))],
    )(part)


def _by_chip_to_full(stack):
    _, nl, r, b = stack.shape
    return stack.transpose(1, 2, 0, 3).reshape(nl, r, N_CHIPS * b)


def _sharded_step(a):
    x = a["x"][0]
    target = a["loss_target"][0]
    cx, cy, cc = _mesh_pos()
    chip = (2 * cx + cy).astype(jnp.int32)
    core = cc.astype(jnp.int32)
    me = (4 * cx + 2 * cy + cc).astype(jnp.int32)
    zero = jnp.zeros((), jnp.int32)
    dus = lax.dynamic_update_slice

    loc = [a[n].astype(_MXU) for n, *_ in BIG]
    got = _gather_weights(loc)
    w = {"w_in": dus(got[0], loc[0][None], (chip, zero, zero, zero)),
         "w_branch": dus(got[1], loc[1], (zero, chip * loc[1].shape[1], zero)),
         "w_out": dus(got[2], loc[2], (zero, chip * loc[2].shape[1], zero)),
         "w_ffn_in": dus(got[3], loc[3][:, None], (zero, chip, zero, zero)),
         "w_ffn_out": dus(got[4], loc[4], (zero, chip * loc[4].shape[1], zero))}
    conv_loc = jnp.concatenate([a[n].reshape(-1, PACK_W) for n in CONV], axis=0)
    conv_all = dus(_allgather_devices(conv_loc, "conv_weight_allgather"), conv_loc[None], (me, zero, zero))[0::2]
    off = 0
    for n in CONV:
        rows = a[n].size // PACK_W
        w[n] = _by_chip_to_full(conv_all[:, off:off + rows].reshape((N_CHIPS,) + a[n].shape))
        off += rows
    small = {n: a[n] for n in SMALL}

    loss_blk, grad_x, grads, big = _local_step(x, target, w, small)
    loss = lax.psum(loss_blk[0, 0], ("x", "y", "c"))

    views = [big[n].reshape(N_LAYERS, -1, wd) for n, _, wd, _, _ in BIG]
    recv = _sibling_exchange(views)
    c_idx = core.reshape(1)
    parts = [_add_layer(v, r, c_idx, a=rows, tr=tr, tc=tc, name="grad_add_sibling_" + n)
             for v, r, (n, rows, _, tr, tc) in zip(views, recv, BIG)]
    slots = _chip_exchange(parts)
    sel = jnp.stack([chip, core])
    both = [_sum_slots(s, p, sel, tr=tr, tc=tc, name="grad_sum_chips_" + n, halves=True)
            for s, p, (n, _, _, tr, tc) in zip(slots, parts, BIG)]
    done = dict(zip([n for n, *_ in BIG], _sibling_share(both)))
    g_big = {n: done[n].reshape(a[n].shape) for n in ("w_branch", "w_out", "w_ffn_in", "w_ffn_out")}
    gt = done["w_in"].transpose(0, 2, 1)
    tail = W_IN_SHARD - (IN_DIM - 7168)
    last = jnp.concatenate([gt[..., :tail], gt[..., W_IN_SHARD - N_HEADS:], gt[..., tail:W_IN_SHARD - N_HEADS]], axis=-1)
    g_big["w_in"] = jnp.where(chip == N_CHIPS - 1, last, gt)

    names = SMALL + CONV
    srows = -(-sum(grads[n].size for n in names) // (8 * PACK_W)) * 8
    flat = lambda d, ns: jnp.concatenate([d[n].reshape(-1) for n in ns])
    padto = lambda v: jnp.pad(v, (0, srows * PACK_W - v.shape[0])).reshape(srows, PACK_W)
    g_own = padto(flat(grads, names))
    g_sum = _sum_slots(_allgather_devices(g_own, "small_grad_allgather"), g_own[None], jnp.stack([me, zero]),
                       tr=srows, tc=PACK_W, name="small_grad_sum")
    off, g_small = 0, {}
    for n in names:
        g_small[n] = g_sum.reshape(-1)[off:off + grads[n].size].reshape(grads[n].shape)
        off += grads[n].size
    for n in CONV:
        width = a[n].shape[2]
        g_big[n] = lax.dynamic_slice(g_small.pop(n), (zero, zero, chip * width), a[n].shape)

    out_g, out_d, out_m, out_v = {}, {}, {}, {}
    for n in g_big:
        shp = a[n].shape
        two_d = (shp[0] * shp[1], shp[2])
        d_, m_, v_ = _adamw(a[n].reshape(two_d), g_big[n].reshape(two_d), a["m_" + n].reshape(two_d),
                            a["v_" + n].reshape(two_d), name="adamw_" + n)
        out_g[n], out_d[n], out_m[n], out_v[n] = g_big[n], d_.reshape(shp), m_.reshape(shp), v_.reshape(shp)
    d_, m_, v_ = _adamw(padto(flat(a, SMALL)), padto(flat(g_small, SMALL)), padto(flat({n: a["m_" + n] for n in SMALL}, SMALL)),
                        padto(flat({n: a["v_" + n] for n in SMALL}, SMALL)), name="adamw_small")
    off = 0
    for n in SMALL:
        cut = lambda v: v.reshape(-1)[off:off + a[n].size].reshape(a[n].shape)
        out_g[n], out_d[n], out_m[n], out_v[n] = g_small[n], cut(d_), cut(m_), cut(v_)
        off += a[n].size
    return loss, grad_x[None], out_g, out_d, out_m, out_v


WEIGHTS = ("norm1_g", "w_in", "b_gate", "lru_conv_w", "lru_conv_b", "lru_w_a", "lru_b_a", "lru_w_x", "lru_b_x", "lru_lambda",
           "ssd_conv_w", "ssd_conv_b", "ssd_dt_bias", "ssd_A_log", "ssd_D", "ssd_norm_g", "w_branch", "w_out", "norm2_g",
           "w_ffn_in", "w_ffn_out", "norm_f")
INPUTS = ("x",) + WEIGHTS + ("loss_target",) + tuple("m_" + n for n in WEIGHTS) + tuple("v_" + n for n in WEIGHTS)


def kernel(x, norm1_g, w_in, b_gate, lru_conv_w, lru_conv_b, lru_w_a, lru_b_a, lru_w_x, lru_b_x, lru_lambda, ssd_conv_w, ssd_conv_b, ssd_dt_bias, ssd_A_log, ssd_D, ssd_norm_g, w_branch, w_out, norm2_g, w_ffn_in, w_ffn_out, norm_f, loss_target, m_norm1_g, m_w_in, m_b_gate, m_lru_conv_w, m_lru_conv_b, m_lru_w_a, m_lru_b_a, m_lru_w_x, m_lru_b_x, m_lru_lambda, m_ssd_conv_w, m_ssd_conv_b, m_ssd_dt_bias, m_ssd_A_log, m_ssd_D, m_ssd_norm_g, m_w_branch, m_w_out, m_norm2_g, m_w_ffn_in, m_w_ffn_out, m_norm_f, v_norm1_g, v_w_in, v_b_gate, v_lru_conv_w, v_lru_conv_b, v_lru_w_a, v_lru_b_a, v_lru_w_x, v_lru_b_x, v_lru_lambda, v_ssd_conv_w, v_ssd_conv_b, v_ssd_dt_bias, v_ssd_A_log, v_ssd_D, v_ssd_norm_g, v_w_branch, v_w_out, v_norm2_g, v_w_ffn_in, v_w_ffn_out, v_norm_f):
    args = (x, norm1_g, w_in, b_gate, lru_conv_w, lru_conv_b, lru_w_a, lru_b_a, lru_w_x, lru_b_x, lru_lambda, ssd_conv_w, ssd_conv_b, ssd_dt_bias, ssd_A_log, ssd_D, ssd_norm_g, w_branch, w_out, norm2_g, w_ffn_in, w_ffn_out, norm_f, loss_target, m_norm1_g, m_w_in, m_b_gate, m_lru_conv_w, m_lru_conv_b, m_lru_w_a, m_lru_b_a, m_lru_w_x, m_lru_b_x, m_lru_lambda, m_ssd_conv_w, m_ssd_conv_b, m_ssd_dt_bias, m_ssd_A_log, m_ssd_D, m_ssd_norm_g, m_w_branch, m_w_out, m_norm2_g, m_w_ffn_in, m_w_ffn_out, m_norm_f, v_norm1_g, v_w_in, v_b_gate, v_lru_conv_w, v_lru_conv_b, v_lru_w_a, v_lru_b_a, v_lru_w_x, v_lru_b_x, v_lru_lambda, v_ssd_conv_w, v_ssd_conv_b, v_ssd_dt_bias, v_ssd_A_log, v_ssd_D, v_ssd_norm_g, v_w_branch, v_w_out, v_norm2_g, v_w_ffn_in, v_w_ffn_out, v_norm_f)
    assert len(args) == len(INPUTS)
    loss, grad_x, g, d, m, v = _sharded_step(dict(zip(INPUTS, args)))
    return (loss, grad_x, *[g[n] for n in WEIGHTS], *[d[n] for n in WEIGHTS], *[m[n] for n in WEIGHTS],
            *[v[n] for n in WEIGHTS])
```

```python
import functools
import math

import numpy as np
import jax
import jax.numpy as jnp
from jax import lax
from jax.experimental import pallas as pl
from jax.experimental.pallas import tpu as pltpu

F32 = jnp.float32
BF16 = jnp.bfloat16
_MXU = jnp.bfloat16
_HI = lax.Precision.HIGHEST

D = 1024
EPS = 1e-6
N_LAYERS = 2
LRU_C = 8.0
N_HEADS = 32
HEAD_P = 64
N_GROUPS = 4
N_STATE = 128
SSD_INNER = 2048
XBC = 3072
D_FF = 2816
CHUNK = 64
NORM_ROWS = 32
IN_DIM = 9248

NP = 9216
ZX_W = 5120
G0 = 6144
LBLK = 512
DT_PAD = 128

VMEM_LIMIT_BYTES_V7X = 56 * 1024 * 1024

ADAM_LR, ADAM_B1, ADAM_B2, ADAM_EPS, ADAM_WD, ADAM_STEP = 0.001, 0.9, 0.999, 1e-08, 0.01, 10
MESH = pl.DeviceIdType.MESH


def _cp(sem):
    return pltpu.CompilerParams(dimension_semantics=sem, vmem_limit_bytes=VMEM_LIMIT_BYTES_V7X)


def _lblk_col(j):
    return 10 + j + 4 * (j // 2)


def _sigmoid(x):
    return 0.5 * jnp.tanh(0.5 * x) + 0.5


def _softplus(x):
    return jnp.maximum(x, 0.0) + jnp.log(1.0 + jnp.exp(-jnp.abs(x)))


def _silu(x):
    return x * _sigmoid(x)


def _dsilu(x):
    s = _sigmoid(x)
    return s * (1.0 + x * (1.0 - s))


_GELU_C0 = math.sqrt(2.0 / math.pi)
_GELU_C1 = 0.044715


def _gelu_and_grad(x):
    t = jnp.tanh(_GELU_C0 * (x + _GELU_C1 * x * x * x))
    g = 0.5 * x * (1.0 + t)
    dg = 0.5 * (1.0 + t) + 0.5 * x * (1.0 - t * t) * _GELU_C0 * (1.0 + 3.0 * _GELU_C1 * x * x)
    return g, dg


def _one_minus_exp(x):
    p = 1.0 + x * (1.0 / 7.0)
    p = 1.0 + x * (1.0 / 6.0) * p
    p = 1.0 + x * (1.0 / 5.0) * p
    p = 1.0 + x * (1.0 / 4.0) * p
    p = 1.0 + x * (1.0 / 3.0) * p
    p = 1.0 + x * (1.0 / 2.0) * p
    return jnp.where(x > -0.3, -x * p, 1.0 - jnp.exp(x))


def _dot(a, b):
    return jnp.dot(a.astype(_MXU), b.astype(_MXU), preferred_element_type=F32)


def _dot_nt(a, b):
    return lax.dot_general(a.astype(_MXU), b.astype(_MXU), (((1,), (1,)), ((), ())), preferred_element_type=F32)


def _dot_tn(a, b):
    return lax.dot_general(a.astype(_MXU), b.astype(_MXU), (((0,), (0,)), ((), ())), preferred_element_type=F32)


def _shift_down(x, prev8, k):
    xr = pltpu.roll(x, k, 0)
    pr = pltpu.roll(prev8, k, 0)
    row = lax.broadcasted_iota(jnp.int32, prev8.shape, 0)
    head = jnp.where(row < k, pr, xr[0:8])
    return jnp.concatenate([head, xr[8:]], axis=0)


def _shift_up(x, next8, k):
    r = x.shape[0]
    xr = pltpu.roll(x, r - k, 0)
    nr = pltpu.roll(next8, 8 - k, 0)
    row = lax.broadcasted_iota(jnp.int32, next8.shape, 0)
    tail = jnp.where(row >= 8 - k, nr, xr[r - 8:r])
    return jnp.concatenate([xr[:r - 8], tail], axis=0)


def _conv4(x, prev8, w_ref, b_ref, cols=slice(None)):
    acc = x * w_ref[3:4, cols] + b_ref[0:1, cols]
    for k in (1, 2, 3):
        acc = acc + _shift_down(x, prev8, k) * w_ref[3 - k:4 - k, cols]
    return acc


def _conv4_bwd_x(dy, next8, w_ref, cols=slice(None)):
    acc = dy * w_ref[3:4, cols]
    for k in (1, 2, 3):
        acc = acc + _shift_up(dy, next8, k) * w_ref[3 - k:4 - k, cols]
    return acc


def _lin_scan(a, b, reverse):
    r = a.shape[0]
    row = lax.broadcasted_iota(jnp.int32, a.shape, 0)
    d = 1
    while d < r:
        sh = (r - d) if reverse else d
        a_s = pltpu.roll(a, sh, 0)
        b_s = pltpu.roll(b, sh, 0)
        m = (row < r - d) if reverse else (row >= d)
        b = jnp.where(m, a * b_s + b, b)
        a = jnp.where(m, a * a_s, a)
        d *= 2
    return a, b


def _rsum(x):
    return jnp.sum(x, axis=0, keepdims=True)


def _norm_mm(h, gamma, w, *, tm, tn, name, out_dtype=F32):
    m, k = h.shape
    if w.ndim == 3:
        assert w.shape[2] == tn
        n = w.shape[0] * tn
        w_spec = pl.BlockSpec((None, k, tn), lambda i, j: (j, 0, 0))
    else:
        n = w.shape[1]
        w_spec = pl.BlockSpec((k, tn), lambda i, j: (0, j))

    def body(h_ref, g_ref, w_ref, xn_ref, o_ref):
        @pl.when(pl.program_id(1) == 0)
        def _():
            x = h_ref[...]
            r = lax.rsqrt(jnp.mean(x * x, axis=-1, keepdims=True) + EPS)
            xn_ref[...] = ((x * r) * g_ref[...]).astype(xn_ref.dtype)
        o_ref[...] = jnp.dot(xn_ref[...], w_ref[...], preferred_element_type=F32).astype(o_ref.dtype)

    return pl.pallas_call(
        body, name=name, grid=(m // tm, n // tn),
        in_specs=[pl.BlockSpec((tm, k), lambda i, j: (i, 0)), pl.BlockSpec((1, k), lambda i, j: (0, 0)), w_spec],
        out_specs=[pl.BlockSpec((tm, k), lambda i, j: (i, 0)), pl.BlockSpec((tm, tn), lambda i, j: (i, j))],
        out_shape=[jax.ShapeDtypeStruct((m, k), _MXU), jax.ShapeDtypeStruct((m, n), out_dtype)],
        compiler_params=_cp(("parallel", "arbitrary")),
    )(h, gamma, w)


def _mm_nn(a, w, *, tm, tn, name, residual=None):
    m, k = a.shape
    n = w.shape[1]

    def body(*refs):
        if residual is None:
            a_ref, w_ref, o_ref = refs
            o_ref[...] = _dot(a_ref[...], w_ref[...])
        else:
            a_ref, w_ref, r_ref, o_ref = refs
            o_ref[...] = _dot(a_ref[...], w_ref[...]) + r_ref[...]

    in_specs = [pl.BlockSpec((tm, k), lambda i, j: (i, 0)), pl.BlockSpec((k, tn), lambda i, j: (0, j))]
    args = [a, w]
    if residual is not None:
        in_specs.append(pl.BlockSpec((tm, tn), lambda i, j: (i, j)))
        args.append(residual)
    return pl.pallas_call(
        body, name=name, grid=(m // tm, n // tn), in_specs=in_specs,
        out_specs=pl.BlockSpec((tm, tn), lambda i, j: (i, j)),
        out_shape=jax.ShapeDtypeStruct((m, n), F32),
        compiler_params=_cp(("parallel", "parallel")),
    )(*args)


def _wgrad(a, b, *, tt, ta, tn, name, out_shape, out_block, out_index, a_tab=None, o_tab=None, into=None):
    t = a.shape[0]
    a_tab = list(range(a.shape[1] // ta)) if a_tab is None else a_tab
    o_tab = a_tab if o_tab is None else o_tab
    nb = b.shape[1] // tn

    def body(at_ref, ot_ref, a_ref, b_ref, *rest):
        del at_ref, ot_ref
        o_ref = rest[-1]

        @pl.when(pl.program_id(2) == 0)
        def _():
            o_ref[...] = jnp.zeros_like(o_ref)
        o_ref[...] += _dot_tn(a_ref[...], b_ref[...])

    in_specs = [pl.BlockSpec((tt, ta), lambda r, j, i, at, ot: (i, at[r])),
                pl.BlockSpec((tt, tn), lambda r, j, i, at, ot: (i, j))]
    args = [jnp.asarray(a_tab, jnp.int32), jnp.asarray(o_tab, jnp.int32), a, b]
    aliases = {}
    if into is not None:
        in_specs.append(pl.BlockSpec(memory_space=pl.ANY))
        args.append(into)
        aliases = {4: 0}
    return pl.pallas_call(
        body, name=name,
        grid_spec=pltpu.PrefetchScalarGridSpec(
            num_scalar_prefetch=2, grid=(len(a_tab), nb, t // tt), in_specs=in_specs,
            out_specs=pl.BlockSpec(out_block, lambda r, j, i, at, ot: out_index(ot[r], j))),
        out_shape=jax.ShapeDtypeStruct(out_shape, F32), input_output_aliases=aliases,
        compiler_params=_cp(("parallel", "parallel", "arbitrary")),
    )(*args)


def _mm_nt(a, w, *, tm, name):
    m, kc = a.shape
    n = w.shape[0]

    def body(a_ref, w_ref, o_ref):
        o_ref[...] = _dot_nt(a_ref[...], w_ref[...])

    return pl.pallas_call(
        body, name=name, grid=(m // tm,),
        in_specs=[pl.BlockSpec((tm, kc), lambda i: (i, 0)), pl.BlockSpec((n, kc), lambda i: (0, 0))],
        out_specs=pl.BlockSpec((tm, n), lambda i: (i, 0)),
        out_shape=jax.ShapeDtypeStruct((m, n), F32),
        compiler_params=_cp(("parallel",)),
    )(a, w)


def _mm_nt_rmsbwd(dy, w, x, gamma, dres, *, tm, tk, name, extra=None):
    m, kc = dy.shape
    nk = kc // tk
    if w.ndim == 3:
        assert w.shape[0] == nk and w.shape[2] == tk
        d = w.shape[1]
        w_spec = pl.BlockSpec((None, d, tk), lambda i, k: (k, 0, 0))
    else:
        d = w.shape[0]
        w_spec = pl.BlockSpec((d, tk), lambda i, k: (0, k))

    def body(*refs):
        if extra is None:
            dy_ref, w_ref, x_ref, g_ref, r_ref, dx_ref, dg_ref, acc_ref = refs
        else:
            dy_ref, w_ref, x_ref, g_ref, r_ref, dy2_ref, w2_ref, dx_ref, dg_ref, acc_ref = refs
        i, kk = pl.program_id(0), pl.program_id(1)

        @pl.when(kk == 0)
        def _():
            acc_ref[...] = jnp.zeros_like(acc_ref)

        @pl.when((i == 0) & (kk == 0))
        def _():
            dg_ref[...] = jnp.zeros_like(dg_ref)

        acc_ref[...] += _dot_nt(dy_ref[...], w_ref[...])

        @pl.when(kk == nk - 1)
        def _():
            dxn = acc_ref[...]
            if extra is not None:
                dxn = dxn + _dot_nt(dy2_ref[...], w2_ref[...])
            xv = x_ref[...]
            r = lax.rsqrt(jnp.mean(xv * xv, axis=-1, keepdims=True) + EPS)
            xh = xv * r
            dg_ref[0:1, :] += _rsum(dxn * xh)
            dxh = dxn * g_ref[...]
            dx_ref[...] = r_ref[...] + r * (dxh - xh * jnp.mean(dxh * xh, axis=-1, keepdims=True))

    in_specs = [pl.BlockSpec((tm, tk), lambda i, k: (i, k)), w_spec,
                pl.BlockSpec((tm, d), lambda i, k: (i, 0)), pl.BlockSpec((1, d), lambda i, k: (0, 0)),
                pl.BlockSpec((tm, d), lambda i, k: (i, 0))]
    args = [dy, w, x, gamma, dres]
    if extra is not None:
        k2 = extra[0].shape[1]
        in_specs += [pl.BlockSpec((tm, k2), lambda i, k: (i, 0)), pl.BlockSpec((d, k2), lambda i, k: (0, 0))]
        args += list(extra)
    return pl.pallas_call(
        body, name=name, grid=(m // tm, nk), in_specs=in_specs,
        out_specs=[pl.BlockSpec((tm, d), lambda i, k: (i, 0)), pl.BlockSpec((8, d), lambda i, k: (0, 0))],
        out_shape=[jax.ShapeDtypeStruct((m, d), F32), jax.ShapeDtypeStruct((8, d), F32)],
        scratch_shapes=[pltpu.VMEM((tm, d), F32)],
        compiler_params=_cp(("arbitrary", "arbitrary")),
    )(*args)


def _rsum8(x):
    acc = x[0:8]
    for g in range(1, x.shape[0] // 8):
        acc = acc + x[8 * g:8 * (g + 1)]
    return acc


def _lru_gates(x, prev8, cw_ref, cb_ref, wa_ref, wx_ref, ba_ref, bx_ref, lam_ref):
    u = _conv4(x, prev8, cw_ref, cb_ref)
    ra =_sigmoid(_dot(u, wa_ref[0]) + ba_ref[...])
    ia = _sigmoid(_dot(u, wx_ref[0]) + bx_ref[...])
    sp = _softplus(-lam_ref[...])
    log_a = -LRU_C * ra * sp
    a = jnp.exp(log_a)
    m2 = _one_minus_exp(2.0 * log_a)
    mult = jnp.sqrt(m2)
    return u, ra, ia, sp, a, m2, mult


def _lru_fwd(proj, lw, *, r, name):
    t = proj.shape[0]
    nt = t // r

    def body(xg_ref, xp_ref, cw_ref, cb_ref, wa_ref, wx_ref, ba_ref, bx_ref, lam_ref, hl_ref, ya_ref, carry_ref):
        i = pl.program_id(1)

        @pl.when(i == 0)
        def _():
            carry_ref[...] = jnp.zeros_like(carry_ref)

        x = xg_ref[:, 0:256]
        lg = xg_ref[:, 256:512]
        prev8 = jnp.where(i == 0, 0.0, xp_ref[:, 0:256])
        u, ra, ia, sp, a, m2, mult = _lru_gates(x, prev8, cw_ref, cb_ref, wa_ref, wx_ref, ba_ref, bx_ref, lam_ref)
        ac, hc = _lin_scan(a, mult * ia * u, False)
        h = hc + ac * carry_ref[0:1, :]
        hl_ref[...] = h
        carry_ref[0:1, :] = hl_ref[r - 1:r, :]
        g, _ = _gelu_and_grad(lg)
        ya_ref[...] = (g * h).astype(ya_ref.dtype)

    small = lambda rows: pl.BlockSpec((rows, 256), lambda j, i: (0, j))
    return pl.pallas_call(
        body, name=name, grid=(4, nt),
        in_specs=[pl.BlockSpec((r, LBLK), lambda j, i: (i, _lblk_col(j))),
                  pl.BlockSpec((8, LBLK), lambda j, i: (jnp.maximum(i * (r // 8) - 1, 0), _lblk_col(j))),
                  small(4), small(1),
                  pl.BlockSpec((1, 256, 256), lambda j, i: (j, 0, 0)), pl.BlockSpec((1, 256, 256), lambda j, i: (j, 0, 0)),
                  small(1), small(1), small(1)],
        out_specs=[pl.BlockSpec((r, 256), lambda j, i: (i, j)), pl.BlockSpec((r, 256), lambda j, i: (i, j))],
        out_shape=[jax.ShapeDtypeStruct((t, D), F32), jax.ShapeDtypeStruct((t, D), _MXU)],
        scratch_shapes=[pltpu.VMEM((8, 256), F32)],
        compiler_params=_cp(("parallel", "arbitrary")),
    )(proj, proj, lw["cw"], lw["cb"], lw["wa"], lw["wx"], lw["ba"], lw["bx"], lw["lam"])


def _lru_bwd(proj, hl, dya, dproj, lw, *, r, name):
    t = proj.shape[0]
    nt = t // r

    def body(xg_ref, xp_ref, hl_ref, hp_ref, dya_ref, cw_ref, cb_ref, wa_ref, wx_ref, ba_ref, bx_ref, lam_ref, dproj_in,
             dproj_ref, sm_ref, dwa_ref, dwx_ref, carry_ref, du8_ref, row_scr):
        del dproj_in
        i = pl.program_id(1)

        @pl.when(i == 0)
        def _():
            carry_ref[...] = jnp.zeros_like(carry_ref)
            du8_ref[...] = jnp.zeros_like(du8_ref)
            sm_ref[...] = jnp.zeros_like(sm_ref)
            dwa_ref[...] = jnp.zeros_like(dwa_ref)
            dwx_ref[...] = jnp.zeros_like(dwx_ref)

        tile0 = i == nt - 1
        xp = xg_ref[:, 0:256]
        lg = xg_ref[:, 256:512]
        prev8 = jnp.where(tile0, 0.0, xp_ref[:, 0:256])
        u, ra, ia, sp, a, m2, mult = _lru_gates(xp, prev8, cw_ref, cb_ref, wa_ref, wx_ref, ba_ref, bx_ref, lam_ref)
        h = hl_ref[...]
        hprev = _shift_down(h, jnp.where(tile0, 0.0, hp_ref[...]), 1)
        dya_v = dya_ref[...]
        g, dg = _gelu_and_grad(lg)
        ac, lc = _lin_scan(_shift_up(a, carry_ref[...], 1), dya_v * g, True)
        lam_v = lc + ac * carry_ref[1:2, :]
        row_scr[0:8, :] = lam_v[0:8]
        row_scr[8:16, :] = a[0:8]
        carry_ref[1:2, :] = row_scr[0:1, :]
        carry_ref[0:1, :] = row_scr[8:9, :]
        da = lam_v * hprev
        dmult = lam_v * ia * u
        dia = lam_v * mult * u
        dlog = da * a - dmult * (1.0 - m2) / mult
        dra = -LRU_C * sp * dlog
        dpa = dra * ra * (1.0 - ra)
        dpx = dia * ia * (1.0 - ia)
        du = lam_v * mult * ia + _dot_nt(dpa, wa_ref[0]) + _dot_nt(dpx, wx_ref[0])
        dwa_ref[0] += _dot_tn(u, dpa)
        dwx_ref[0] += _dot_tn(u, dpx)
        dlx = du * cw_ref[3:4, :]
        sm_ref[24:32, :] += _rsum8(du * xp)
        for k in (1, 2, 3):
            du_k = _shift_up(du, du8_ref[...], k)
            dlx = dlx + du_k * cw_ref[3 - k:4 - k, :]
            sm_ref[8 * (3 - k):8 * (4 - k), :] += _rsum8(du_k * xp)
        du8_ref[...] = du[0:8]
        dproj_ref[:, 0:256] = dlx.astype(dproj_ref.dtype)
        dproj_ref[:, 256:512] = (dya_v * h * dg).astype(dproj_ref.dtype)
        sm_ref[32:40, :] += _rsum8(du)
        sm_ref[40:48, :] += _rsum8(dpa)
        sm_ref[48:56, :] += _rsum8(dpx)
        sm_ref[56:64, :] += _rsum8(-LRU_C * ra * dlog) * (-_sigmoid(-lam_ref[...]))

    rev = lambda i: nt - 1 - i
    small = lambda rows: pl.BlockSpec((rows, 256), lambda j, i: (0, j))
    wblk = pl.BlockSpec((1, 256, 256), lambda j, i: (j, 0, 0))
    n_in = 13
    return pl.pallas_call(
        body, name=name, grid=(4, nt),
        in_specs=[pl.BlockSpec((r, LBLK), lambda j, i: (rev(i), _lblk_col(j))),
                  pl.BlockSpec((8, LBLK), lambda j, i: (jnp.maximum(rev(i) * (r // 8) - 1, 0), _lblk_col(j))),
                  pl.BlockSpec((r, 256), lambda j, i: (rev(i), j)),
                  pl.BlockSpec((8, 256), lambda j, i: (jnp.maximum(rev(i) * (r // 8) - 1, 0), j)),
                  pl.BlockSpec((r, 256), lambda j, i: (rev(i), j)),
                  small(4), small(1), wblk, wblk, small(1), small(1), small(1),
                  pl.BlockSpec(memory_space=pl.ANY)],
        out_specs=[pl.BlockSpec((r, LBLK), lambda j, i: (rev(i), _lblk_col(j))),
                   pl.BlockSpec((64, 256), lambda j, i: (0, j)), wblk, wblk],
        out_shape=[jax.ShapeDtypeStruct(dproj.shape, dproj.dtype), jax.ShapeDtypeStruct((64, D), F32),
                   jax.ShapeDtypeStruct((4, 256, 256), F32), jax.ShapeDtypeStruct((4, 256, 256), F32)],
        scratch_shapes=[pltpu.VMEM((8, 256), F32), pltpu.VMEM((8, 256), F32), pltpu.VMEM((16, 256), F32)],
        input_output_aliases={n_in - 1: 0},
        compiler_params=_cp(("parallel", "arbitrary")),
    )(proj, proj, hl, hl, dya, lw["cw"], lw["cb"], lw["wa"], lw["wx"], lw["ba"], lw["bx"], lw["lam"], dproj)


def _head_cols(x):
    lane = lax.broadcasted_iota(jnp.int32, x.shape, 1)
    return [jnp.sum(jnp.where(lane == h, x, 0.0), axis=1, keepdims=True) for h in range(N_HEADS)]


def _compact_heads(blocks):
    lane = lax.broadcasted_iota(jnp.int32, blocks[0].shape, 1)
    lo = lane < HEAD_P
    out = jnp.zeros_like(blocks[0])
    for j, blk in enumerate(blocks):
        s_lo = jnp.sum(jnp.where(lo, blk, 0.0), axis=1, keepdims=True)
        s_hi = jnp.sum(jnp.where(lo, 0.0, blk), axis=1, keepdims=True)
        out = jnp.where(lane == 2 * j, s_lo, out)
        out = jnp.where(lane == 2 * j + 1, s_hi, out)
    return out


def _ssd_prelude(dtraw_ref, dtb_ref, alog_ref, dt_scr, a_scr):
    lane = lax.broadcasted_iota(jnp.int32, dt_scr.shape, 1)
    dt = jnp.where(lane < N_HEADS, _softplus(dtraw_ref[...] + dtb_ref[0:1, :]), 0.0)
    dt_scr[...] = dt
    a_scr[...] = dt * (-jnp.exp(alog_ref[0:1, :]))


def _ssd_chunk_scalars(dt_scr, a_scr, r_scr, r0):
    a_c = a_scr[pl.ds(r0, CHUNK), :]
    dt_c = dt_scr[pl.ds(r0, CHUNK), :]
    i0 = lax.broadcasted_iota(jnp.int32, (CHUNK, CHUNK), 0)
    i1 = lax.broadcasted_iota(jnp.int32, (CHUNK, CHUNK), 1)
    tri = jnp.where(i0 >= i1, 1.0, 0.0).astype(F32)
    cs = jnp.dot(tri, a_c, precision=_HI, preferred_element_type=F32)
    lane = lax.broadcasted_iota(jnp.int32, (CHUNK, 128), 1)
    srow = lax.broadcasted_iota(jnp.int32, (CHUNK, 128), 0)
    t_lo = jnp.where((lane < HEAD_P) & (srow <= lane), 1.0, 0.0).astype(F32)
    t_hi = jnp.where((lane >= HEAD_P) & (srow <= lane - HEAD_P), 1.0, 0.0).astype(F32)
    even = (lane % 2) == 0
    tn = (((0,), (0,)), ((), ()))
    r_scr[...] = (lax.dot_general(jnp.where(even, a_c, 0.0), t_lo, tn, precision=_HI, preferred_element_type=F32)
                  + lax.dot_general(jnp.where(even, 0.0, a_c), t_hi, tn, precision=_HI, preferred_element_type=F32))
    return cs, dt_c, _head_cols(cs), _head_cols(dt_c)


def _block_diag2(v):
    lo = lax.broadcasted_iota(jnp.int32, v.shape, 1) < HEAD_P
    return jnp.concatenate([jnp.where(lo, v, 0.0), jnp.where(lo, 0.0, v)], axis=0).astype(_MXU)


def _ssd_pair(xc_scr, r_scr, cs_cols, dt_cols, s2, r0, j, s2t=None):
    lane = lax.broadcasted_iota(jnp.int32, (CHUNK, 128), 1)
    srow = lax.broadcasted_iota(jnp.int32, (CHUNK, 128), 0)
    lo = lane < HEAD_P
    csc = jnp.where(lo, cs_cols[2 * j], cs_cols[2 * j + 1])
    dtc = jnp.where(lo, dt_cols[2 * j], dt_cols[2 * j + 1])
    csr = r_scr[2 * j:2 * j + 1, :] + r_scr[2 * j + 1:2 * j + 2, :]
    dm = jnp.where((lane & (HEAD_P - 1)) <= srow, jnp.exp(jnp.minimum(csc - csr, 0.0)), 0.0)
    xs = xc_scr[pl.ds(r0, CHUNK), j * 128:(j + 1) * 128]
    xd = xs * dtc
    csl = jnp.sum(jnp.where(srow == CHUNK - 1, csc, 0.0), axis=0, keepdims=True)
    out = dict(csc=csc, dtc=dtc, dm=dm, m2=s2 * dm, xs=xs, xd=xd, rhs=_block_diag2(xd), e=jnp.exp(csc),
               w=jnp.exp(csl - csc), dec=jnp.exp(csl))
    if s2t is not None:
        out["mt2"] = s2t * jnp.where((lane & (HEAD_P - 1)) >= srow, jnp.exp(jnp.minimum(csr - csc, 0.0)), 0.0)
    return out


def _cat(parts):
    return jnp.concatenate(parts, axis=1)


def _ssd_fwd(proj, dtraw, sw, *, rb, name):
    t = proj.shape[0]
    ns, cb = t // rb, rb // CHUNK

    def body(zx_ref, zp_ref, dtraw_ref, cw_ref, cbias_ref, dtb_ref, alog_ref, dsk_ref, ng_ref,
             yssd_ref, yb_ref, st_ref, h_scr, xc_scr, dt_scr, a_scr, r_scr):
        i = pl.program_id(0)

        @pl.when(i == 0)
        def _():
            h_scr[...] = jnp.zeros_like(h_scr)

        for j in range(XBC // 128):
            cs_, zc = slice(128 * j, 128 * (j + 1)), slice(2048 + 128 * j, 2048 + 128 * (j + 1))
            pre = _conv4(zx_ref[:, zc], jnp.where(i == 0, 0.0, zp_ref[:, zc]), cw_ref, cbias_ref, cs_)
            xc_scr[:, cs_] = pre * _sigmoid(pre)
        _ssd_prelude(dtraw_ref, dtb_ref, alog_ref, dt_scr, a_scr)

        def chunk(c, carry):
            r0 = pl.multiple_of(c * CHUNK, CHUNK)
            _, _, cs_cols, dt_cols = _ssd_chunk_scalars(dt_scr, a_scr, r_scr, r0)
            st_ref[c] = h_scr[...]
            for g in range(N_GROUPS):
                bg = xc_scr[pl.ds(r0, CHUNK), 2048 + 128 * g:2048 + 128 * (g + 1)]
                cg = xc_scr[pl.ds(r0, CHUNK), 2560 + 128 * g:2560 + 128 * (g + 1)]
                s2 = _dot_nt(cg, jnp.concatenate([bg, bg], axis=0))
                hp = h_scr[:, 512 * g:512 * (g + 1)]
                yoff = _dot(cg, hp)
                xdw, dec = [], []
                for jj in range(4):
                    j = 4 * g + jj
                    p = _ssd_pair(xc_scr, r_scr, cs_cols, dt_cols, s2, r0, j)
                    y = _dot(p["m2"], p["rhs"]) + yoff[:, 128 * jj:128 * (jj + 1)] * p["e"]
                    yssd_ref[pl.ds(r0, CHUNK), 128 * j:128 * (j + 1)] = y + dsk_ref[0:1, 128 * j:128 * (j + 1)] * p["xs"]
                    xdw.append(p["xd"] * p["w"])
                    dec.append(p["dec"])
                h_scr[:, 512 * g:512 * (g + 1)] = hp * _cat(dec) + _dot_tn(bg, _cat(xdw))
            return carry

        lax.fori_loop(0, cb, chunk, 0)
        for g in range(N_GROUPS):
            sl = slice(512 * g, 512 * (g + 1))
            for q in range(rb // NORM_ROWS):
                rw = slice(NORM_ROWS * q, NORM_ROWS * (q + 1))
                yz = yssd_ref[rw, sl] * _silu(zx_ref[rw, sl])
                rg = lax.rsqrt(jnp.mean(yz * yz, axis=-1, keepdims=True) + EPS)
                yb_ref[rw, sl] = (yz * rg * ng_ref[0:1, sl]).astype(yb_ref.dtype)

    full = lambda rows, cols: pl.BlockSpec((rows, cols), lambda i: (0, 0))
    return pl.pallas_call(
        body, name=name, grid=(ns,),
        in_specs=[pl.BlockSpec((rb, ZX_W), lambda i: (i, 0)),
                  pl.BlockSpec((8, ZX_W), lambda i: (jnp.maximum(i * (rb // 8) - 1, 0), 0)),
                  pl.BlockSpec((rb, DT_PAD), lambda i: (i, 0)),
                  full(4, XBC), full(1, XBC), full(1, DT_PAD), full(1, DT_PAD), full(1, SSD_INNER), full(1, SSD_INNER)],
        out_specs=[pl.BlockSpec((rb, SSD_INNER), lambda i: (i, 0)), pl.BlockSpec((rb, SSD_INNER), lambda i: (i, 0)),
                   pl.BlockSpec((cb, N_STATE, SSD_INNER), lambda i: (i, 0, 0))],
        out_shape=[jax.ShapeDtypeStruct((t, SSD_INNER), F32), jax.ShapeDtypeStruct((t, SSD_INNER), _MXU),
                   jax.ShapeDtypeStruct((t // CHUNK, N_STATE, SSD_INNER), F32)],
        scratch_shapes=[pltpu.VMEM((N_STATE, SSD_INNER), F32), pltpu.VMEM((rb, XBC), F32), pltpu.VMEM((rb, DT_PAD), F32),
                        pltpu.VMEM((rb, DT_PAD), F32), pltpu.VMEM((128, 128), F32)],
        compiler_params=_cp(("arbitrary",)),
    )(proj, proj, dtraw, sw["cw"], sw["cb"], sw["dtb"], sw["alog"], sw["dsk"], sw["ng"])


def _ssd_bwd(proj, dtraw, yssd, states, dyb, dproj, sw, *, rb, name):
    t = proj.shape[0]
    ns, cb = t // rb, rb // CHUNK

    def body(zx_ref, zp_ref, dtraw_ref, yssd_ref, st_ref, dyb_ref, cw_ref, cbias_ref, dtb_ref, alog_ref, dsk_ref, ng_ref,
             dproj_in, dzx_ref, ddt_ref, gconv_ref, gch_ref, ghd_ref,
             dht_scr, xc_scr, dsl_scr, dy_scr, dxc_scr, dt_scr, a_scr, r_scr, dp8_scr):
        del dproj_in
        i = pl.program_id(0)

        @pl.when(i == 0)
        def _():
            dht_scr[...] = jnp.zeros_like(dht_scr)
            dp8_scr[...] = jnp.zeros_like(dp8_scr)
            gconv_ref[...] = jnp.zeros_like(gconv_ref)
            gch_ref[...] = jnp.zeros_like(gch_ref)
            ghd_ref[...] = jnp.zeros_like(ghd_ref)

        tile0 = i == ns - 1
        for j in range(XBC // 128):
            cs_, zc = slice(128 * j, 128 * (j + 1)), slice(2048 + 128 * j, 2048 + 128 * (j + 1))
            pre = _conv4(zx_ref[:, zc], jnp.where(tile0, 0.0, zp_ref[:, zc]), cw_ref, cbias_ref, cs_)
            sg = _sigmoid(pre)
            xc_scr[:, cs_] = pre * sg
            dsl_scr[:, cs_] = sg * (1.0 + pre * (1.0 - sg))
        _ssd_prelude(dtraw_ref, dtb_ref, alog_ref, dt_scr, a_scr)

        for g in range(N_GROUPS):
            sl = slice(512 * g, 512 * (g + 1))
            for q in range(rb // NORM_ROWS):
                rw = slice(NORM_ROWS * q, NORM_ROWS * (q + 1))
                zv = zx_ref[rw, sl]
                ys = yssd_ref[rw, sl]
                sg = _sigmoid(zv)
                sz = zv * sg
                yz = ys * sz
                rg = lax.rsqrt(jnp.mean(yz * yz, axis=-1, keepdims=True) + EPS)
                yn = yz * rg
                dyb_v = dyb_ref[rw, sl]
                gch_ref[0:8, sl] += _rsum8(dyb_v * yn)
                dyn = dyb_v * ng_ref[0:1, sl]
                dyz = rg * (dyn - yn * jnp.mean(dyn * yn, axis=-1, keepdims=True))
                dy_scr[rw, sl] = dyz * sz
                dzx_ref[rw, sl] = (dyz * ys * (sg * (1.0 + zv * (1.0 - sg)))).astype(dzx_ref.dtype)

        a_row = -jnp.exp(alog_ref[0:1, :])

        def chunk(cc, carry):
            c = cb - 1 - cc
            r0 = pl.multiple_of(c * CHUNK, CHUNK)
            rows = pl.ds(r0, CHUNK)
            _, dt_c, cs_cols, dt_cols = _ssd_chunk_scalars(dt_scr, a_scr, r_scr, r0)
            lane = lax.broadcasted_iota(jnp.int32, (CHUNK, 128), 1)
            srow = lax.broadcasted_iota(jnp.int32, (CHUNK, 128), 0)
            lo = lane < HEAD_P
            last = srow == CHUNK - 1
            p1_blocks, p3_blocks = [], []
            for g in range(N_GROUPS):
                gs = slice(512 * g, 512 * (g + 1))
                bg = xc_scr[rows, 2048 + 128 * g:2048 + 128 * (g + 1)]
                cg = xc_scr[rows, 2560 + 128 * g:2560 + 128 * (g + 1)]
                b2 = jnp.concatenate([bg, bg], axis=0)
                s2 = _dot_nt(cg, b2)
                s2t = _dot_nt(bg, jnp.concatenate([cg, cg], axis=0))
                hp = st_ref[c, :, gs]
                dht = dht_scr[:, gs]
                yoff = _dot(cg, hp)
                ps = [_ssd_pair(xc_scr, r_scr, cs_cols, dt_cols, s2, r0, 4 * g + jj, s2t) for jj in range(4)]
                dys = [dy_scr[rows, 128 * (4 * g + jj):128 * (4 * g + jj + 1)] for jj in range(4)]
                dye = _cat([dys[jj] * ps[jj]["e"] for jj in range(4)])
                w_g = _cat([p["w"] for p in ps])
                dcg = _dot_nt(dye, hp)
                dht_scr[:, gs] = _dot_tn(cg, dye) + _cat([p["dec"] for p in ps]) * dht
                dxd_state = w_g * _dot(bg, dht)
                dbg = _dot_nt(_cat([p["xd"] for p in ps]) * w_g, dht)
                tsum = _rsum(dht * hp)
                ds2 = jnp.zeros((CHUNK, 128), F32)
                for jj in range(4):
                    j = 4 * g + jj
                    ls = slice(128 * j, 128 * (j + 1))
                    p, dy2 = ps[jj], dys[jj]
                    dy_bd = _block_diag2(dy2)
                    dm2 = _dot_nt(dy2, p["rhs"])
                    ds2 = ds2 + dm2 * p["dm"]
                    gdiff = dm2 * p["m2"] - _dot_nt(p["xd"], dy_bd) * p["mt2"]
                    dxs = dxd_state[:, 128 * jj:128 * (jj + 1)]
                    dxd = _dot(p["mt2"], dy_bd) + dxs
                    end_row = _rsum(p["xd"] * dxs) + p["dec"] * tsum[:, 128 * jj:128 * (jj + 1)]
                    p1_blocks.append(gdiff + dy2 * yoff[:, 128 * jj:128 * (jj + 1)] * p["e"] - p["xd"] * dxs
                                     + jnp.where(last, end_row, 0.0))
                    p3_blocks.append(dxd * p["xs"])
                    dxc_scr[rows, ls] = dxd * p["dtc"] + dy2 * dsk_ref[0:1, ls]
                    gch_ref[8:16, ls] += _rsum8(dy2 * p["xs"])
                dcg = dcg + _dot(ds2, b2)
                rb2 = _dot_tn(ds2, cg)
                dxc_scr[rows, 2048 + 128 * g:2048 + 128 * (g + 1)] = dbg + rb2[0:CHUNK] + rb2[CHUNK:2 * CHUNK]
                dxc_scr[rows, 2560 + 128 * g:2560 + 128 * (g + 1)] = dcg
            dcs = _compact_heads(p1_blocks)
            i0 = lax.broadcasted_iota(jnp.int32, (CHUNK, CHUNK), 0)
            i1 = lax.broadcasted_iota(jnp.int32, (CHUNK, CHUNK), 1)
            triu = jnp.where(i1 >= i0, 1.0, 0.0).astype(F32)
            da = jnp.dot(triu, dcs, precision=_HI, preferred_element_type=F32)
            ddt = _compact_heads(p3_blocks) + da * a_row
            ddtraw = jnp.where(lane < N_HEADS, ddt * _sigmoid(dtraw_ref[rows, :] + dtb_ref[0:1, :]), 0.0)
            ddt_ref[rows, :] = ddtraw.astype(ddt_ref.dtype)
            ghd_ref[0:1, :] += _rsum(ddtraw)
            ghd_ref[1:2, :] += _rsum(da * dt_c) * a_row
            return carry

        lax.fori_loop(0, cb, chunk, 0)
        for j in range(XBC // 128):
            cs_, zc = slice(128 * j, 128 * (j + 1)), slice(2048 + 128 * j, 2048 + 128 * (j + 1))
            dpre = dxc_scr[:, cs_] * dsl_scr[:, cs_]
            xraw = zx_ref[:, zc]
            dx = dpre * cw_ref[3:4, cs_]
            gconv_ref[24:32, cs_] += _rsum8(dpre * xraw)
            for k in (1, 2, 3):
                dpre_k = _shift_up(dpre, dp8_scr[:, cs_], k)
                dx = dx + dpre_k * cw_ref[3 - k:4 - k, cs_]
                gconv_ref[8 * (3 - k):8 * (4 - k), cs_] += _rsum8(dpre_k * xraw)
            dzx_ref[:, zc] = dx.astype(dzx_ref.dtype)
            dp8_scr[:, cs_] = dpre[0:8]
            gconv_ref[32:40, cs_] += _rsum8(dpre)

    rev = lambda i: ns - 1 - i
    full = lambda rows, cols: pl.BlockSpec((rows, cols), lambda i: (0, 0))
    n_in = 13
    return pl.pallas_call(
        body, name=name, grid=(ns,),
        in_specs=[pl.BlockSpec((rb, ZX_W), lambda i: (rev(i), 0)),
                  pl.BlockSpec((8, ZX_W), lambda i: (jnp.maximum(rev(i) * (rb // 8) - 1, 0), 0)),
                  pl.BlockSpec((rb, DT_PAD), lambda i: (rev(i), 0)),
                  pl.BlockSpec((rb, SSD_INNER), lambda i: (rev(i), 0)),
                  pl.BlockSpec((cb, N_STATE, SSD_INNER), lambda i: (rev(i), 0, 0)),
                  pl.BlockSpec((rb, SSD_INNER), lambda i: (rev(i), 0)),
                  full(4, XBC), full(1, XBC), full(1, DT_PAD), full(1, DT_PAD), full(1, SSD_INNER), full(1, SSD_INNER),
                  pl.BlockSpec(memory_space=pl.ANY)],
        out_specs=[pl.BlockSpec((rb, ZX_W), lambda i: (rev(i), 0)), pl.BlockSpec((rb, DT_PAD), lambda i: (rev(i), 0)),
                   full(40, XBC), full(16, SSD_INNER), full(8, DT_PAD)],
        out_shape=[jax.ShapeDtypeStruct(dproj.shape, dproj.dtype), jax.ShapeDtypeStruct((t, DT_PAD), _MXU),
                   jax.ShapeDtypeStruct((40, XBC), F32), jax.ShapeDtypeStruct((16, SSD_INNER), F32),
                   jax.ShapeDtypeStruct((8, DT_PAD), F32)],
        scratch_shapes=[pltpu.VMEM((N_STATE, SSD_INNER), F32), pltpu.VMEM((rb, XBC), F32), pltpu.VMEM((rb, XBC), F32),
                        pltpu.VMEM((rb, SSD_INNER), F32), pltpu.VMEM((rb, XBC), F32), pltpu.VMEM((rb, DT_PAD), F32),
                        pltpu.VMEM((rb, DT_PAD), F32), pltpu.VMEM((128, 128), F32), pltpu.VMEM((8, XBC), F32)],
        input_output_aliases={n_in - 1: 0},
        compiler_params=_cp(("arbitrary",)),
    )(proj, proj, dtraw, yssd, states, dyb, sw["cw"], sw["cb"], sw["dtb"], sw["alog"], sw["dsk"], sw["ng"], dproj)


def _branch_merge(ya, yb, proj, wba, wbb, bgate, *, tm, tn, name):
    t = ya.shape[0]
    nj = D // tn

    def body(ya_ref, yb_ref, ga_ref, gb_ref, wba_ref, wbb_ref, ba_ref, bb_ref, ta_ref, tb_ref, mg_ref):
        ta = _dot(ya_ref[...], wba_ref[...])
        tb = _dot(yb_ref[...], wbb_ref[...])
        ta_ref[...] = ta.astype(ta_ref.dtype)
        tb_ref[...] = tb.astype(tb_ref.dtype)
        ga = _sigmoid(ga_ref[...] + ba_ref[...])
        gb = _sigmoid(gb_ref[...] + bb_ref[...])
        mg_ref[...] = (ga * ta + gb * tb).astype(mg_ref.dtype)

    tile = pl.BlockSpec((tm, tn), lambda i, j: (i, j))
    return pl.pallas_call(
        body, name=name, grid=(t // tm, nj),
        in_specs=[pl.BlockSpec((tm, D), lambda i, j: (i, 0)), pl.BlockSpec((tm, SSD_INNER), lambda i, j: (i, 0)),
                  pl.BlockSpec((tm, tn), lambda i, j: (i, G0 // tn + j)),
                  pl.BlockSpec((tm, tn), lambda i, j: (i, (G0 + D) // tn + j)),
                  pl.BlockSpec((D, tn), lambda i, j: (0, j)), pl.BlockSpec((SSD_INNER, tn), lambda i, j: (0, j)),
                  pl.BlockSpec((1, tn), lambda i, j: (0, j)), pl.BlockSpec((1, tn), lambda i, j: (0, nj + j))],
        out_specs=[tile, tile, tile],
        out_shape=[jax.ShapeDtypeStruct((t, D), _MXU)] * 3,
        compiler_params=_cp(("parallel", "parallel")),
    )(ya, yb, proj, proj, wba, wbb, bgate, bgate)


def _swiglu_mm(gu, wfo, residual, *, tm, tn, name):
    t = gu.shape[0]

    def body(gu_ref, w_ref, r_ref, act_ref, o_ref):
        @pl.when(pl.program_id(1) == 0)
        def _():
            gate = gu_ref[:, 0:D_FF].astype(F32)
            act_ref[...] = (_silu(gate) * gu_ref[:, D_FF:2 * D_FF].astype(F32)).astype(act_ref.dtype)
        o_ref[...] = jnp.dot(act_ref[...], w_ref[...], preferred_element_type=F32) + r_ref[...]

    return pl.pallas_call(
        body, name=name, grid=(t // tm, D // tn),
        in_specs=[pl.BlockSpec((tm, 2 * D_FF), lambda i, j: (i, 0)), pl.BlockSpec((D_FF, tn), lambda i, j: (0, j)),
                  pl.BlockSpec((tm, tn), lambda i, j: (i, j))],
        out_specs=[pl.BlockSpec((tm, D_FF), lambda i, j: (i, 0)), pl.BlockSpec((tm, tn), lambda i, j: (i, j))],
        out_shape=[jax.ShapeDtypeStruct((t, D_FF), _MXU), jax.ShapeDtypeStruct((t, D), F32)],
        compiler_params=_cp(("parallel", "arbitrary")),
    )(gu, wfo, residual)


def _ffn_bwd_act(dh, wfo, gu, *, tm, name):
    t = dh.shape[0]

    def body(dh_ref, w_ref, gu_ref, o_ref):
        dact = _dot_nt(dh_ref[...], w_ref[...])
        g = gu_ref[:, 0:D_FF].astype(F32)
        u = gu_ref[:, D_FF:2 * D_FF].astype(F32)
        sg = _sigmoid(g)
        o_ref[:, 0:D_FF] = (dact * u * (sg * (1.0 + g * (1.0 - sg)))).astype(o_ref.dtype)
        o_ref[:, D_FF:2 * D_FF] = (dact * (g * sg)).astype(o_ref.dtype)

    return pl.pallas_call(
        body, name=name, grid=(t // tm,),
        in_specs=[pl.BlockSpec((tm, D), lambda i: (i, 0)), pl.BlockSpec((D_FF, D), lambda i: (0, 0)),
                  pl.BlockSpec((tm, 2 * D_FF), lambda i: (i, 0))],
        out_specs=pl.BlockSpec((tm, 2 * D_FF), lambda i: (i, 0)),
        out_shape=jax.ShapeDtypeStruct((t, 2 * D_FF), _MXU),
        compiler_params=_cp(("parallel",)),
    )(dh, wfo, gu)


def _outproj_bwd(dh, wout, ta, tb, proj, bgate, dproj, *, tm, name):
    t = dh.shape[0]

    def body(dh_ref, w_ref, ta_ref, tb_ref, g_ref, b_ref, dta_ref, dtb_ref, dg_ref, db_ref):
        @pl.when(pl.program_id(0) == 0)
        def _():
            db_ref[...] = jnp.zeros_like(db_ref)
        dm = _dot_nt(dh_ref[...], w_ref[...])
        ga = _sigmoid(g_ref[:, 0:D] + b_ref[:, 0:D])
        gb = _sigmoid(g_ref[:, D:2 * D] + b_ref[:, D:2 * D])
        dta_ref[...] = (dm * ga).astype(dta_ref.dtype)
        dtb_ref[...] = (dm * gb).astype(dtb_ref.dtype)
        dga = dm * ta_ref[...].astype(F32) * ga * (1.0 - ga)
        dgb = dm * tb_ref[...].astype(F32) * gb * (1.0 - gb)
        dg_ref[:, 0:D] = dga.astype(dg_ref.dtype)
        dg_ref[:, D:2 * D] = dgb.astype(dg_ref.dtype)
        db_ref[0:1, 0:D] += _rsum(dga)
        db_ref[0:1, D:2 * D] += _rsum(dgb)

    row = lambda cols: pl.BlockSpec((tm, cols), lambda i: (i, 0))
    return pl.pallas_call(
        body, name=name, grid=(t // tm,),
        in_specs=[row(D), pl.BlockSpec((D, D), lambda i: (0, 0)), row(D), row(D),
                  pl.BlockSpec((tm, 2 * D), lambda i: (i, G0 // (2 * D))), pl.BlockSpec((1, 2 * D), lambda i: (0, 0))],
        out_specs=[row(D), row(D), pl.BlockSpec((tm, 2 * D), lambda i: (i, G0 // (2 * D))),
                   pl.BlockSpec((8, 2 * D), lambda i: (0, 0))],
        out_shape=[jax.ShapeDtypeStruct((t, D), _MXU), jax.ShapeDtypeStruct((t, D), _MXU),
                   jax.ShapeDtypeStruct(dproj, _MXU), jax.ShapeDtypeStruct((8, 2 * D), F32)],
        compiler_params=_cp(("arbitrary",)),
    )(dh, wout, ta, tb, proj, bgate)


def _loss_head(h, gf, target, *, tm, name):
    t = h.shape[0]

    def body(h_ref, g_ref, t_ref, loss_ref, dg_ref, dh_ref):
        @pl.when(pl.program_id(0) == 0)
        def _():
            loss_ref[...] = jnp.zeros_like(loss_ref)
            dg_ref[...] = jnp.zeros_like(dg_ref)
        x = h_ref[...]
        r = lax.rsqrt(jnp.mean(x * x, axis=-1, keepdims=True) + EPS)
        xh = x * r
        err = xh * g_ref[...] - t_ref[...]
        loss_ref[...] += 0.5 * jnp.sum(jnp.mean(err * err, axis=-1, keepdims=True), axis=0, keepdims=True)
        dy = err * (1.0 / D)
        dg_ref[0:1, :] += _rsum(dy * xh)
        dxh = dy * g_ref[...]
        dh_ref[...] = r * (dxh - xh * jnp.mean(dxh * xh, axis=-1, keepdims=True))

    row = pl.BlockSpec((tm, D), lambda i: (i, 0))
    return pl.pallas_call(
        body, name=name, grid=(t // tm,),
        in_specs=[row, pl.BlockSpec((1, D), lambda i: (0, 0)), row],
        out_specs=[pl.BlockSpec((8, 128), lambda i: (0, 0)), pl.BlockSpec((8, D), lambda i: (0, 0)), row],
        out_shape=[jax.ShapeDtypeStruct((8, 128), F32), jax.ShapeDtypeStruct((8, D), F32), jax.ShapeDtypeStruct((t, D), F32)],
        compiler_params=_cp(("arbitrary",)),
    )(h, gf, target)


def _row_tile(rows, cols, limit_bytes=1 << 20):
    best = None
    for tr in range(8, rows + 1, 8):
        if rows % tr == 0 and tr * cols * 4 <= limit_bytes:
            best = tr
    return best if best is not None else rows


def _adamw(w, g, m, v, *, name):
    rows, cols = w.shape
    tr = _row_tile(rows, cols)

    def body(w_ref, g_ref, m_ref, v_ref, d_ref, nm_ref, nv_ref):
        gv = g_ref[...]
        nm = ADAM_B1 * m_ref[...] + (1.0 - ADAM_B1) * gv
        nv = ADAM_B2 * v_ref[...] + (1.0 - ADAM_B2) * (gv * gv)
        m_hat = nm / (1.0 - ADAM_B1 ** ADAM_STEP)
        v_hat = nv / (1.0 - ADAM_B2 ** ADAM_STEP)
        d_ref[...] = -ADAM_LR * (m_hat / (jnp.sqrt(v_hat) + ADAM_EPS) + ADAM_WD * w_ref[...])
        nm_ref[...] = nm
        nv_ref[...] = nv

    blk = pl.BlockSpec((tr, cols), lambda i: (i, 0))
    shp = jax.ShapeDtypeStruct((rows, cols), F32)
    return pl.pallas_call(
        body, name=name, grid=(rows // tr,), in_specs=[blk] * 4, out_specs=[blk] * 3, out_shape=[shp] * 3,
        compiler_params=_cp(("parallel",)),
    )(w, g, m, v)


def _bd256(w):
    w4 = w.reshape(4, 4, 64, 64)
    eye = jnp.eye(4, dtype=w.dtype)
    return (w4[:, :, :, None, :] * eye[None, :, None, :, None]).reshape(4, 256, 256)


def _bd256_diag(g):
    g5 = g.reshape(4, 4, 64, 4, 64)
    return jnp.stack([g5[:, a, :, a, :] for a in range(4)], axis=1).reshape(16, 64, 64)


FFN_SHARD = 2 * D_FF // 4
W_IN_SHARD = IN_DIM // 4
W_IN_ROWS = 9344


def _w_in_cols(shards, c0, c1):
    out = []
    for p in range(4):
        lo, hi = max(c0, W_IN_SHARD * p), min(c1, W_IN_SHARD * (p + 1))
        if lo < hi:
            out.append(shards[p][:, lo - W_IN_SHARD * p:hi - W_IN_SHARD * p])
    return out


def _layer_weights(w, small, l):
    win = w["w_in"][:, l]
    lblk = [_w_in_cols(win, 256 * j, 256 * (j + 1)) + _w_in_cols(win, D + 256 * j, D + 256 * (j + 1)) for j in range(4)]
    wp = jnp.concatenate(_w_in_cols(win, 2048, 4096) + _w_in_cols(win, 4096, 7168) + lblk[0] + lblk[1]
                         + _w_in_cols(win, 7200, 9248) + lblk[2] + lblk[3], axis=1)
    wdt = jnp.pad(jnp.concatenate(_w_in_cols(win, 7168, 7200), axis=1), ((0, 0), (0, DT_PAD - N_HEADS)))
    row = lambda v: v.reshape(1, -1)
    pad_h = lambda v: jnp.pad(v.reshape(1, -1), ((0, 0), (0, DT_PAD - N_HEADS)))
    lw = dict(cw=w["lru_conv_w"][l], cb=row(small["lru_conv_b"][l]),
              wa=_bd256(small["lru_w_a"][l]).astype(_MXU), wx=_bd256(small["lru_w_x"][l]).astype(_MXU),
              ba=row(small["lru_b_a"][l]), bx=row(small["lru_b_x"][l]), lam=row(small["lru_lambda"][l]))
    sw = dict(cw=w["ssd_conv_w"][l], cb=row(small["ssd_conv_b"][l]), dtb=pad_h(small["ssd_dt_bias"][l]),
              alog=pad_h(small["ssd_A_log"][l]), dsk=row(jnp.repeat(small["ssd_D"][l], HEAD_P)),
              ng=row(small["ssd_norm_g"][l]))
    return dict(wp=wp, wdt=wdt, lw=lw, sw=sw, wba=w["w_branch"][l][0:D], wbb=w["w_branch"][l][D:3 * D],
                wout=w["w_out"][l], wfi=w["w_ffn_in"][l], wfo=w["w_ffn_out"][l],
                g1=row(small["norm1_g"][l]), g2=row(small["norm2_g"][l]), bgate=row(small["b_gate"][l]))


def _tiles(t):
    return dict(tmn=min(1024, t), tm=min(512, t), tm2=min(256, t), r=min(256, t), rb=min(128, t))


def _layer_fwd(h, lwt, l):
    tl = _tiles(h.shape[0])
    n = f"l{l}_"
    xn, proj = _norm_mm(h, lwt["g1"], lwt["wp"], tm=tl["tmn"], tn=1024, name=n + "in_proj")
    dtraw = _mm_nn(xn, lwt["wdt"], tm=tl["tm"], tn=DT_PAD, name=n + "dt_proj")
    hl, ya = _lru_fwd(proj, lwt["lw"], r=tl["r"], name=n + "lru_fwd")
    yssd, yb, states = _ssd_fwd(proj, dtraw, lwt["sw"], rb=tl["rb"], name=n + "ssd_fwd")
    ta, tb, merged = _branch_merge(ya, yb, proj, lwt["wba"], lwt["wbb"], lwt["bgate"], tm=tl["tm"], tn=512, name=n + "merge")
    hmid = _mm_nn(merged, lwt["wout"], tm=tl["tm"], tn=512, name=n + "out_proj", residual=h)
    xn2, gu = _norm_mm(hmid, lwt["g2"], lwt["wfi"], tm=tl["tmn"], tn=FFN_SHARD, name=n + "ffn_in", out_dtype=_MXU)
    act, hout = _swiglu_mm(gu, lwt["wfo"], hmid, tm=tl["tm"], tn=512, name=n + "ffn_out")
    saved = dict(h=h, xn=xn, proj=proj, dtraw=dtraw, hl=hl, ya=ya, yssd=yssd, yb=yb, states=states, ta=ta, tb=tb,
                 merged=merged, hmid=hmid, xn2=xn2, gu=gu, act=act)
    return hout, saved


def _layer_bwd(dh, s, lwt, l, big):
    t = dh.shape[0]
    tl = _tiles(t)
    n = f"l{l}_"
    tt = tl["tm"]
    big = dict(big)

    def wgrad(key, a, b, name, **kw):
        big[key] = _wgrad(a, b, tt=tt, name=n + name, into=big.get(key), **kw)

    dgu = _ffn_bwd_act(dh, lwt["wfo"], s["gu"], tm=tl["tm2"], name=n + "ffn_act_bwd")
    wgrad("w_ffn_out", s["act"], dh, "ffn_out_wgrad", ta=D_FF, tn=1024, out_shape=(N_LAYERS, D_FF, D),
          out_block=(None, D_FF, 1024), out_index=lambda o, j: (l, o, j))
    wgrad("w_ffn_in", s["xn2"], dgu, "ffn_in_wgrad", ta=D, tn=FFN_SHARD, out_shape=(N_LAYERS, 4, D, FFN_SHARD),
          out_block=(None, None, D, FFN_SHARD), out_index=lambda o, j: (l, j, o, 0))
    dh1, dg2 = _mm_nt_rmsbwd(dgu, lwt["wfi"], s["hmid"], lwt["g2"], dh, tm=tl["tm"], tk=FFN_SHARD, name=n + "ffn_in_dgrad")
    dta, dtb, dproj, dbg = _outproj_bwd(dh1, lwt["wout"], s["ta"], s["tb"], s["proj"], lwt["bgate"], (t, NP),
                                        tm=tl["tm2"], name=n + "out_proj_bwd")
    rows_d = dict(ta=D, tn=512, out_block=(None, D, 512), out_index=lambda o, j: (l, o, j))
    wgrad("w_out", s["merged"], dh1, "out_proj_wgrad", out_shape=(N_LAYERS, D, D), **rows_d)
    dya = _mm_nt(dta, lwt["wba"], tm=tl["tm"], name=n + "branch_a_dgrad")
    dyb = _mm_nt(dtb, lwt["wbb"], tm=tl["tm"], name=n + "branch_b_dgrad")
    wgrad("w_branch", s["ya"], dta, "branch_a_wgrad", out_shape=(N_LAYERS, 3 * D, D), a_tab=[0], o_tab=[0], **rows_d)
    wgrad("w_branch", s["yb"], dtb, "branch_b_wgrad", out_shape=(N_LAYERS, 3 * D, D), a_tab=[0, 1], o_tab=[1, 2], **rows_d)
    dproj, lsm, dwa, dwx = _lru_bwd(s["proj"], s["hl"], dya, dproj, lwt["lw"], r=tl["r"], name=n + "lru_bwd")
    dproj, ddt, gconv, gch, ghd = _ssd_bwd(s["proj"], s["dtraw"], s["yssd"], s["states"], dyb, dproj, lwt["sw"],
                                           rb=tl["rb"], name=n + "ssd_bwd")
    lsm = lsm.reshape(8, 8, D).sum(axis=1)
    gconv = gconv.reshape(5, 8, XBC).sum(axis=1)
    gch = gch.reshape(2, 8, SSD_INNER).sum(axis=1)
    w_in = dict(tn=D, out_shape=(N_LAYERS, W_IN_ROWS, D), out_index=lambda o, j: (l, o, j))
    wgrad("w_in", dproj, s["xn"], "in_proj_wgrad", ta=1024, out_block=(None, 1024, D),
          a_tab=list(range(9)), o_tab=[2, 3, 4, 5, 6, 0, 7, 8, 1], **w_in)
    wgrad("w_in", ddt, s["xn"], "dt_proj_wgrad", ta=DT_PAD, out_block=(None, DT_PAD, D), a_tab=[0],
          o_tab=[NP // DT_PAD], **w_in)
    dh0, dg1 = _mm_nt_rmsbwd(dproj, lwt["wp"], s["h"], lwt["g1"], dh1, tm=tl["tm"], tk=2304, name=n + "in_proj_dgrad",
                             extra=(ddt, lwt["wdt"]))
    grads = dict(
        lru_conv_w=lsm[0:4], lru_conv_b=lsm[4], lru_b_a=lsm[5], lru_b_x=lsm[6], lru_lambda=lsm[7],
        lru_w_a=_bd256_diag(dwa), lru_w_x=_bd256_diag(dwx),
        ssd_conv_w=gconv[0:4], ssd_conv_b=gconv[4], ssd_norm_g=gch[0], ssd_D=gch[1].reshape(N_HEADS, HEAD_P).sum(axis=-1),
        ssd_dt_bias=ghd[0, 0:N_HEADS], ssd_A_log=ghd[1, 0:N_HEADS],
        b_gate=dbg[0], norm1_g=dg1[0], norm2_g=dg2[0])
    return dh0, grads, big


def _local_step(x, target, w, small):
    h = x
    lwts, saved = [], []
    for l in range(N_LAYERS):
        lwt = _layer_weights(w, small, l)
        h, s = _layer_fwd(h, lwt, l)
        lwts.append(lwt)
        saved.append(s)
    loss_blk, dgf, dh = _loss_head(h, small["norm_f"].reshape(1, D), target, tm=_tiles(x.shape[0])["tm"], name="loss_head")
    per_layer, big = [None] * N_LAYERS, {}
    for l in reversed(range(N_LAYERS)):
        dh, per_layer[l], big = _layer_bwd(dh, saved[l], lwts[l], l, big)
    grads = {k: jnp.stack([per_layer[l][k] for l in range(N_LAYERS)], axis=0) for k in per_layer[0]}
    grads["norm_f"] = dgf[0]
    return loss_blk, dh, grads, big


PACK_W = 1024
BIG = (("w_in", W_IN_SHARD, D, W_IN_SHARD, 256), ("w_branch", 768, D, 256, D), ("w_out", 256, D, 256, D),
       ("w_ffn_in", D, FFN_SHARD, 256, FFN_SHARD), ("w_ffn_out", 704, D, 352, D))
CONV = ("lru_conv_w", "ssd_conv_w")
SMALL = ("norm1_g", "b_gate", "lru_conv_b", "lru_w_a", "lru_b_a", "lru_w_x", "lru_b_x", "lru_lambda", "ssd_conv_b",
         "ssd_dt_bias", "ssd_A_log", "ssd_D", "ssd_norm_g", "norm2_g", "norm_f")
_WIRE = jnp.bfloat16
N_CHIPS = 4
N_DEV = 8


def _mesh_pos():
    return lax.axis_index("x"), lax.axis_index("y"), lax.axis_index("c")


HBM_SPEC = pl.BlockSpec(memory_space=pltpu.HBM)


def _remote(src, dst, send_sems, recv_sems, k, to):
    return pltpu.make_async_remote_copy(src_ref=src, dst_ref=dst, send_sem=send_sems.at[k], recv_sem=recv_sems.at[k],
                                        device_id=to, device_id_type=MESH)


def _other_chips(x, y):
    return [(1 - x, y), (x, 1 - y), (1 - x, 1 - y)]


def _gather_weights(loc):
    n = len(loc)
    rows = [None, loc[1].shape[1], loc[2].shape[1], None, loc[4].shape[1]]
    shapes = [(N_CHIPS,) + loc[0].shape, (2, N_CHIPS * rows[1], D), (2, N_CHIPS * rows[2], D),
              (2, N_CHIPS) + loc[3].shape[1:], (2, N_CHIPS * rows[4], D)]

    def place(o_refs, k, chip, layer):
        if k == 0:
            return o_refs[0].at[chip, layer]
        if k == 3:
            return o_refs[3].at[layer, chip]
        return o_refs[k].at[layer, pl.ds(pl.multiple_of(chip * rows[k], 16), rows[k]), :]

    def body(*refs):
        in_refs, o_refs, (send_sems, recv_sems) = refs[:n], refs[n:2 * n], refs[2 * n:]
        x, y, c = _mesh_pos()
        s = 2 * x + y
        sib = (x, y, 1 - c)
        chips = _other_chips(x, y)
        first = [_remote(in_refs[k].at[c], place(o_refs, k, s, c), send_sems, recv_sems, n * j + k, (px, py, c))
                 for j, (px, py) in enumerate(chips) for k in range(n)]
        for cp in first:
            cp.start()
        passed = []
        for j, (px, py) in enumerate(chips):
            for k in range(n):
                landed = place(o_refs, k, 2 * px + py, c)
                _remote(in_refs[k].at[c], landed, send_sems, recv_sems, n * j + k, (px, py, c)).wait_recv()
                cp = _remote(landed, landed, send_sems, recv_sems, n * (3 + j) + k, sib)
                cp.start()
                passed.append(cp)
        for j, (px, py) in enumerate(chips):
            for k in range(n):
                landed = place(o_refs, k, 2 * px + py, 1 - c)
                _remote(landed, landed, send_sems, recv_sems, n * (3 + j) + k, sib).wait_recv()
        for cp in first + passed:
            cp.wait_send()

    return pl.pallas_call(
        body, name="allgather_weights", in_specs=[HBM_SPEC] * n, out_specs=[HBM_SPEC] * n,
        out_shape=[jax.ShapeDtypeStruct(shp, v.dtype) for shp, v in zip(shapes, loc)],
        scratch_shapes=[pltpu.SemaphoreType.DMA((6 * n,)), pltpu.SemaphoreType.DMA((6 * n,))],
    )(*loc)


def _sibling_exchange(gbufs):
    n = len(gbufs)

    def body(*refs):
        g_refs, o_refs, (send_sems, recv_sems) = refs[:n], refs[n:2 * n], refs[2 * n:]
        x, y, c = _mesh_pos()
        copies = [_remote(g_refs[k].at[1 - c], o_refs[k], send_sems, recv_sems, k, (x, y, 1 - c)) for k in range(n)]
        for cp in copies:
            cp.start()
        for cp in copies:
            cp.wait()

    return pl.pallas_call(
        body, name="grad_sibling_exchange", in_specs=[HBM_SPEC] * n, out_specs=[HBM_SPEC] * n,
        out_shape=[jax.ShapeDtypeStruct(g.shape[1:], g.dtype) for g in gbufs],
        scratch_shapes=[pltpu.SemaphoreType.DMA((n,)), pltpu.SemaphoreType.DMA((n,))],
    )(*gbufs)


def _add_layer(g, recv, c_idx, *, a, tr, tc, name):
    wd = g.shape[2]
    nr = a // tr

    def body(c_ref, g_ref, r_ref, o_ref):
        del c_ref
        o_ref[...] = (g_ref[...] + r_ref[...]).astype(o_ref.dtype)

    return pl.pallas_call(
        body, name=name,
        grid_spec=pltpu.PrefetchScalarGridSpec(
            num_scalar_prefetch=1, grid=(N_CHIPS, nr, wd // tc),
            in_specs=[pl.BlockSpec((None, tr, tc), lambda p, i, j, c_ref: (c_ref[0], p * nr + i, j)),
                      pl.BlockSpec((tr, tc), lambda p, i, j, c_ref: (p * nr + i, j))],
            out_specs=pl.BlockSpec((None, tr, tc), lambda p, i, j, c_ref: (p, i, j))),
        out_shape=jax.ShapeDtypeStruct((N_CHIPS, a, wd), _WIRE),
        compiler_params=_cp(("parallel", "parallel", "parallel")),
    )(c_idx, g, recv)


def _chip_exchange(parts):
    n = len(parts)

    def body(*refs):
        s_refs, o_refs, (send_sems, recv_sems) = refs[:n], refs[n:2 * n], refs[2 * n:]
        x, y, c = _mesh_pos()
        s = 2 * x + y
        chips = _other_chips(x, y)
        sends = [_remote(s_refs[k].at[2 * px + py], o_refs[k].at[s], send_sems, recv_sems, n * j + k, (px, py, c))
                 for j, (px, py) in enumerate(chips) for k in range(n)]
        for cp in sends:
            cp.start()
        for j, (px, py) in enumerate(chips):
            for k in range(n):
                p = 2 * px + py
                _remote(s_refs[k].at[p], o_refs[k].at[p], send_sems, recv_sems, n * j + k, (px, py, c)).wait_recv()
        for cp in sends:
            cp.wait_send()

    return pl.pallas_call(
        body, name="grad_chip_exchange", in_specs=[HBM_SPEC] * n, out_specs=[HBM_SPEC] * n,
        out_shape=[jax.ShapeDtypeStruct(p.shape, p.dtype) for p in parts],
        scratch_shapes=[pltpu.SemaphoreType.DMA((3 * n,)), pltpu.SemaphoreType.DMA((3 * n,))],
    )(*parts)


def _sum_slots(slots, own, sel, *, tr, tc, name, halves=False):
    n, rows, wd = slots.shape
    k = own.shape[0]

    def body(sel_ref, s_ref, own_ref, o_ref):
        mine = sel_ref[0]
        acc = jnp.zeros((tr, tc), F32)
        for p in range(n):
            acc = acc + jnp.where(mine == p, own_ref[...].astype(F32), s_ref[p].astype(F32))
        o_ref[...] = acc

    if halves:
        out_spec = pl.BlockSpec((None, tr, tc), lambda i, j, sel_ref: (sel_ref[1], i, j))
        out_shape = jax.ShapeDtypeStruct((2, rows, wd), F32)
    else:
        out_spec = pl.BlockSpec((tr, tc), lambda i, j, sel_ref: (i, j))
        out_shape = jax.ShapeDtypeStruct((rows, wd), F32)
    return pl.pallas_call(
        body, name=name,
        grid_spec=pltpu.PrefetchScalarGridSpec(
            num_scalar_prefetch=1, grid=(rows // tr, wd // tc),
            in_specs=[pl.BlockSpec((n, tr, tc), lambda i, j, sel_ref: (0, i, j)),
                      pl.BlockSpec((None, tr, tc), lambda i, j, sel_ref: (sel_ref[0] if k > 1 else 0, i, j))],
            out_specs=out_spec),
        out_shape=out_shape, compiler_params=_cp(("parallel", "parallel")),
    )(sel, slots, own)


def _sibling_share(both):
    n = len(both)

    def body(*refs):
        o_refs, (send_sems, recv_sems) = refs[n:2 * n], refs[2 * n:]
        x, y, c = _mesh_pos()
        sends = [_remote(o_refs[k].at[c], o_refs[k].at[c], send_sems, recv_sems, k, (x, y, 1 - c)) for k in range(n)]
        for cp in sends:
            cp.start()
        for k in range(n):
            _remote(o_refs[k].at[1 - c], o_refs[k].at[1 - c], send_sems, recv_sems, k, (x, y, 1 - c)).wait_recv()
        for cp in sends:
            cp.wait_send()

    return pl.pallas_call(
        body, name="grad_sibling_share", in_specs=[HBM_SPEC] * n, out_specs=[HBM_SPEC] * n,
        out_shape=[jax.ShapeDtypeStruct(b.shape, b.dtype) for b in both], input_output_aliases={k: k for k in range(n)},
        scratch_shapes=[pltpu.SemaphoreType.DMA((n,)), pltpu.SemaphoreType.DMA((n,))],
    )(*both)


def _allgather_devices(part, name):
    rows, wd = part.shape

    def body(p_ref, o_ref, send_sems, recv_sems):
        x, y, c = _mesh_pos()
        sib = (x, y, 1 - c)
        chips = _other_chips(x, y)
        slot = lambda px, py, pc: o_ref.at[4 * px + 2 * py + pc]
        first = [_remote(p_ref, slot(x, y, c), send_sems, recv_sems, 0, sib)]
        first += [_remote(p_ref, slot(x, y, c), send_sems, recv_sems, 1 + j, (px, py, c)) for j, (px, py) in enumerate(chips)]
        for cp in first:
            cp.start()
        passed = []
        for j, (px, py) in enumerate(chips):
            _remote(p_ref, slot(px, py, c), send_sems, recv_sems, 1 + j, (px, py, c)).wait_recv()
            cp = _remote(slot(px, py, c), slot(px, py, c), send_sems, recv_sems, 4 + j, sib)
            cp.start()
            passed.append(cp)
        _remote(p_ref, slot(x, y, 1 - c), send_sems, recv_sems, 0, sib).wait_recv()
        for j, (px, py) in enumerate(chips):
            _remote(slot(px, py, 1 - c), slot(px, py, 1 - c), send_sems, recv_sems, 4 + j, sib).wait_recv()
        for cp in first + passed:
            cp.wait_send()

    return pl.pallas_call(
        body, name=name, in_specs=[HBM_SPEC], out_specs=HBM_SPEC,
        out_shape=jax.ShapeDtypeStruct((N_DEV, rows, wd), part.dtype),
        scratch_shapes=[pltpu.SemaphoreType.DMA((N_DEV - 1,)), pltpu.SemaphoreType.DMA((N_DEV - 1,))],
    )(part)


def _by_chip_to_full(stack):
    _, nl, r, b = stack.shape
    return stack.transpose(1, 2, 0, 3).reshape(nl, r, N_CHIPS * b)


def _sharded_step(a):
    x = a["x"][0]
    target = a["loss_target"][0]
    cx, cy, cc = _mesh_pos()
    chip = (2 * cx + cy).astype(jnp.int32)
    core = cc.astype(jnp.int32)
    me = (4 * cx + 2 * cy + cc).astype(jnp.int32)
    zero = jnp.zeros((), jnp.int32)
    dus = lax.dynamic_update_slice

    loc = [a[n].astype(_MXU) for n, *_ in BIG]
    got = _gather_weights(loc)
    w = {"w_in": dus(got[0], loc[0][None], (chip, zero, zero, zero)),
         "w_branch": dus(got[1], loc[1], (zero, chip * loc[1].shape[1], zero)),
         "w_out": dus(got[2], loc[2], (zero, chip * loc[2].shape[1], zero)),
         "w_ffn_in": dus(got[3], loc[3][:, None], (zero, chip, zero, zero)),
         "w_ffn_out": dus(got[4], loc[4], (zero, chip * loc[4].shape[1], zero))}
    conv_loc = jnp.concatenate([a[n].reshape(-1, PACK_W) for n in CONV], axis=0)
    conv_all = dus(_allgather_devices(conv_loc, "conv_weight_allgather"), conv_loc[None], (me, zero, zero))[0::2]
    off = 0
    for n in CONV:
        rows = a[n].size // PACK_W
        w[n] = _by_chip_to_full(conv_all[:, off:off + rows].reshape((N_CHIPS,) + a[n].shape))
        off += rows
    small = {n: a[n] for n in SMALL}

    loss_blk, grad_x, grads, big = _local_step(x, target, w, small)
    loss = lax.psum(loss_blk[0, 0], ("x", "y", "c"))

    views = [big[n].reshape(N_LAYERS, -1, wd) for n, _, wd, _, _ in BIG]
    recv = _sibling_exchange(views)
    c_idx = core.reshape(1)
    parts = [_add_layer(v, r, c_idx, a=rows, tr=tr, tc=tc, name="grad_add_sibling_" + n)
             for v, r, (n, rows, _, tr, tc) in zip(views, recv, BIG)]
    slots = _chip_exchange(parts)
    sel = jnp.stack([chip, core])
    both = [_sum_slots(s, p, sel, tr=tr, tc=tc, name="grad_sum_chips_" + n, halves=True)
            for s, p, (n, _, _, tr, tc) in zip(slots, parts, BIG)]
    done = dict(zip([n for n, *_ in BIG], _sibling_share(both)))
    g_big = {n: done[n].reshape(a[n].shape) for n in ("w_branch", "w_out", "w_ffn_in", "w_ffn_out")}
    gt = done["w_in"].transpose(0, 2, 1)
    first = jnp.concatenate([gt[..., 512 * j + 256 * part:512 * j + 256 * (part + 1)] for part in range(2) for j in range(4)]
                            + [gt[..., 2 * D:]], axis=-1)
    tail = W_IN_SHARD - (IN_DIM - 7168)
    last = jnp.concatenate([gt[..., :tail], gt[..., W_IN_SHARD - N_HEADS:], gt[..., tail:W_IN_SHARD - N_HEADS]], axis=-1)
    g_big["w_in"] = jnp.where(chip == 0, first, jnp.where(chip == N_CHIPS - 1, last, gt))

    names = SMALL + CONV
    srows = -(-sum(grads[n].size for n in names) // (8 * PACK_W)) * 8
    flat = lambda d, ns: jnp.concatenate([d[n].reshape(-1) for n in ns])
    padto = lambda v: jnp.pad(v, (0, srows * PACK_W - v.shape[0])).reshape(srows, PACK_W)
    g_own = padto(flat(grads, names))
    g_sum = _sum_slots(_allgather_devices(g_own, "small_grad_allgather"), g_own[None], jnp.stack([me, zero]),
                       tr=srows, tc=PACK_W, name="small_grad_sum")
    off, g_small = 0, {}
    for n in names:
        g_small[n] = g_sum.reshape(-1)[off:off + grads[n].size].reshape(grads[n].shape)
        off += grads[n].size
    for n in CONV:
        width = a[n].shape[2]
        g_big[n] = lax.dynamic_slice(g_small.pop(n), (zero, zero, chip * width), a[n].shape)

    out_g, out_d, out_m, out_v = {}, {}, {}, {}
    for n in g_big:
        shp = a[n].shape
        two_d = (shp[0] * shp[1], shp[2])
        d_, m_, v_ = _adamw(a[n].reshape(two_d), g_big[n].reshape(two_d), a["m_" + n].reshape(two_d),
                            a["v_" + n].reshape(two_d), name="adamw_" + n)
        out_g[n], out_d[n], out_m[n], out_v[n] = g_big[n], d_.reshape(shp), m_.reshape(shp), v_.reshape(shp)
    d_, m_, v_ = _adamw(padto(flat(a, SMALL)), padto(flat(g_small, SMALL)), padto(flat({n: a["m_" + n] for n in SMALL}, SMALL)),
                        padto(flat({n: a["v_" + n] for n in SMALL}, SMALL)), name="adamw_small")
    off = 0
    for n in SMALL:
        cut = lambda v: v.reshape(-1)[off:off + a[n].size].reshape(a[n].shape)
        out_g[n], out_d[n], out_m[n], out_v[n] = g_small[n], cut(d_), cut(m_), cut(v_)
        off += a[n].size
    return loss, grad_x[None], out_g, out_d, out_m, out_v


WEIGHTS = ("norm1_g", "w_in", "b_gate", "lru_conv_w", "lru_conv_b", "lru_w_a", "lru_b_a", "lru_w_x", "lru_b_x", "lru_lambda",
           "ssd_conv_w", "ssd_conv_b", "ssd_dt_bias", "ssd_A_log", "ssd_D", "ssd_norm_g", "w_branch", "w_out", "norm2_g",
           "w_ffn_in", "w_ffn_out", "norm_f")
INPUTS = ("x",) + WEIGHTS + ("loss_target",) + tuple("m_" + n for n in WEIGHTS) + tuple("v_" + n for n in WEIGHTS)


def kernel(x, norm1_g, w_in, b_gate, lru_conv_w, lru_conv_b, lru_w_a, lru_b_a, lru_w_x, lru_b_x, lru_lambda, ssd_conv_w, ssd_conv_b, ssd_dt_bias, ssd_A_log, ssd_D, ssd_norm_g, w_branch, w_out, norm2_g, w_ffn_in, w_ffn_out, norm_f, loss_target, m_norm1_g, m_w_in, m_b_gate, m_lru_conv_w, m_lru_conv_b, m_lru_w_a, m_lru_b_a, m_lru_w_x, m_lru_b_x, m_lru_lambda, m_ssd_conv_w, m_ssd_conv_b, m_ssd_dt_bias, m_ssd_A_log, m_ssd_D, m_ssd_norm_g, m_w_branch, m_w_out, m_norm2_g, m_w_ffn_in, m_w_ffn_out, m_norm_f, v_norm1_g, v_w_in, v_b_gate, v_lru_conv_w, v_lru_conv_b, v_lru_w_a, v_lru_b_a, v_lru_w_x, v_lru_b_x, v_lru_lambda, v_ssd_conv_w, v_ssd_conv_b, v_ssd_dt_bias, v_ssd_A_log, v_ssd_D, v_ssd_norm_g, v_w_branch, v_w_out, v_norm2_g, v_w_ffn_in, v_w_ffn_out, v_norm_f):
    args = (x, norm1_g, w_in, b_gate, lru_conv_w, lru_conv_b, lru_w_a, lru_b_a, lru_w_x, lru_b_x, lru_lambda, ssd_conv_w, ssd_conv_b, ssd_dt_bias, ssd_A_log, ssd_D, ssd_norm_g, w_branch, w_out, norm2_g, w_ffn_in, w_ffn_out, norm_f, loss_target, m_norm1_g, m_w_in, m_b_gate, m_lru_conv_w, m_lru_conv_b, m_lru_w_a, m_lru_b_a, m_lru_w_x, m_lru_b_x, m_lru_lambda, m_ssd_conv_w, m_ssd_conv_b, m_ssd_dt_bias, m_ssd_A_log, m_ssd_D, m_ssd_norm_g, m_w_branch, m_w_out, m_norm2_g, m_w_ffn_in, m_w_ffn_out, m_norm_f, v_norm1_g, v_w_in, v_b_gate, v_lru_conv_w, v_lru_conv_b, v_lru_w_a, v_lru_b_a, v_lru_w_x, v_lru_b_x, v_lru_lambda, v_ssd_conv_w, v_ssd_conv_b, v_ssd_dt_bias, v_ssd_A_log, v_ssd_D, v_ssd_norm_g, v_w_branch, v_w_out, v_norm2_g, v_w_ffn_in, v_w_ffn_out, v_norm_f)
    assert len(args) == len(INPUTS)
    loss, grad_x, g, d, m, v = _sharded_step(dict(zip(INPUTS, args)))
    return (loss, grad_x, *[g[n] for n in WEIGHTS], *[d[n] for n in WEIGHTS], *[m[n] for n in WEIGHTS],
            *[v[n] for n in WEIGHTS])
```

```python
import functools
import math

import numpy as np
import jax
import jax.numpy as jnp
from jax import lax
from jax.experimental import pallas as pl
from jax.experimental.pallas import tpu as pltpu

F32 = jnp.float32
BF16 = jnp.bfloat16
_MXU = jnp.bfloat16
_HI = lax.Precision.HIGHEST

D = 1024
EPS = 1e-6
N_LAYERS = 2
LRU_C = 8.0
N_HEADS = 32
HEAD_P = 64
N_GROUPS = 4
N_STATE = 128
SSD_INNER = 2048
XBC = 3072
D_FF = 2816
CHUNK = 64
NORM_ROWS = 32
IN_DIM = 9248

NP = 9216
ZX_W = 5120
G0 = 6144
LBLK = 512
DT_PAD = 128

VMEM_LIMIT_BYTES_V7X = 56 * 1024 * 1024

ADAM_LR, ADAM_B1, ADAM_B2, ADAM_EPS, ADAM_WD, ADAM_STEP = 0.001, 0.9, 0.999, 1e-08, 0.01, 10
MESH = pl.DeviceIdType.MESH


def _cp(sem):
    return pltpu.CompilerParams(dimension_semantics=sem, vmem_limit_bytes=VMEM_LIMIT_BYTES_V7X)


def _lblk_col(j):
    return 10 + j + 4 * (j // 2)


def _sigmoid(x):
    return 0.5 * jnp.tanh(0.5 * x) + 0.5


def _softplus(x):
    return jnp.maximum(x, 0.0) + jnp.log(1.0 + jnp.exp(-jnp.abs(x)))


def _silu(x):
    return x * _sigmoid(x)


def _dsilu(x):
    s = _sigmoid(x)
    return s * (1.0 + x * (1.0 - s))


_GELU_C0 = math.sqrt(2.0 / math.pi)
_GELU_C1 = 0.044715


def _gelu_and_grad(x):
    t = jnp.tanh(_GELU_C0 * (x + _GELU_C1 * x * x * x))
    g = 0.5 * x * (1.0 + t)
    dg = 0.5 * (1.0 + t) + 0.5 * x * (1.0 - t * t) * _GELU_C0 * (1.0 + 3.0 * _GELU_C1 * x * x)
    return g, dg


def _one_minus_exp(x):
    p = 1.0 + x * (1.0 / 7.0)
    p = 1.0 + x * (1.0 / 6.0) * p
    p = 1.0 + x * (1.0 / 5.0) * p
    p = 1.0 + x * (1.0 / 4.0) * p
    p = 1.0 + x * (1.0 / 3.0) * p
    p = 1.0 + x * (1.0 / 2.0) * p
    return jnp.where(x > -0.3, -x * p, 1.0 - jnp.exp(x))


def _dot(a, b):
    return jnp.dot(a.astype(_MXU), b.astype(_MXU), preferred_element_type=F32)


def _dot_nt(a, b):
    return lax.dot_general(a.astype(_MXU), b.astype(_MXU), (((1,), (1,)), ((), ())), preferred_element_type=F32)


def _dot_tn(a, b):
    return lax.dot_general(a.astype(_MXU), b.astype(_MXU), (((0,), (0,)), ((), ())), preferred_element_type=F32)


def _shift_down(x, prev8, k):
    xr = pltpu.roll(x, k, 0)
    pr = pltpu.roll(prev8, k, 0)
    row = lax.broadcasted_iota(jnp.int32, prev8.shape, 0)
    head = jnp.where(row < k, pr, xr[0:8])
    return jnp.concatenate([head, xr[8:]], axis=0)


def _shift_up(x, next8, k):
    r = x.shape[0]
    xr = pltpu.roll(x, r - k, 0)
    nr = pltpu.roll(next8, 8 - k, 0)
    row = lax.broadcasted_iota(jnp.int32, next8.shape, 0)
    tail = jnp.where(row >= 8 - k, nr, xr[r - 8:r])
    return jnp.concatenate([xr[:r - 8], tail], axis=0)


def _conv4(x, prev8, w_ref, b_ref, cols=slice(None)):
    acc = x * w_ref[3:4, cols] + b_ref[0:1, cols]
    for k in (1, 2, 3):
        acc = acc + _shift_down(x, prev8, k) * w_ref[3 - k:4 - k, cols]
    return acc


def _conv4_bwd_x(dy, next8, w_ref, cols=slice(None)):
    acc = dy * w_ref[3:4, cols]
    for k in (1, 2, 3):
        acc = acc + _shift_up(dy, next8, k) * w_ref[3 - k:4 - k, cols]
    return acc


def _lin_scan(a, b, reverse):
    r = a.shape[0]
    row = lax.broadcasted_iota(jnp.int32, a.shape, 0)
    d = 1
    while d < r:
        sh = (r - d) if reverse else d
        a_s = pltpu.roll(a, sh, 0)
        b_s = pltpu.roll(b, sh, 0)
        m = (row < r - d) if reverse else (row >= d)
        b = jnp.where(m, a * b_s + b, b)
        a = jnp.where(m, a * a_s, a)
        d *= 2
    return a, b


def _rsum(x):
    return jnp.sum(x, axis=0, keepdims=True)


def _comm_specs(comm):
    if comm is None:
        return [], [], [], [], []
    n = len(comm["inputs"])
    return list(comm["inputs"]), [HBM_SPEC] * n, [HBM_SPEC] * len(comm["out_shapes"]), list(comm["out_shapes"]), comm["sems"]


def _comm_steps(comm, refs, n_in, n_out, first, mid, last):
    ni, no = len(comm["inputs"]), len(comm["out_shapes"])
    parts = (refs[n_in:n_in + ni], refs[n_out:n_out + no]) + tuple(refs[len(refs) - len(comm["sems"]):])
    for when, what in ((first, "start"), (mid, "mid"), (last, "end")):
        @pl.when(when)
        def _():
            comm[what](*parts)


def _norm_mm(h, gamma, w, *, tm, tn, name, out_dtype=F32, comm=None):
    m, k = h.shape
    if w.ndim == 3:
        assert w.shape[2] == tn
        n = w.shape[0] * tn
        w_spec = pl.BlockSpec((None, k, tn), lambda i, j: (j, 0, 0))
    else:
        n = w.shape[1]
        w_spec = pl.BlockSpec((k, tn), lambda i, j: (0, j))

    c_args, c_in_specs, c_out_specs, c_out_shapes, c_sems = _comm_specs(comm)
    ni, nj = m // tm, n // tn

    def body(*refs):
        h_ref, g_ref, w_ref = refs[:3]
        xn_ref, o_ref = refs[3 + len(c_args):5 + len(c_args)]
        i, j = pl.program_id(0), pl.program_id(1)
        if comm is not None:
            _comm_steps(comm, refs, 3, 5 + len(c_args), (i == 0) & (j == 0), (i == (3 * ni) // 4) & (j == 0),
                        (i == ni - 1) & (j == nj - 1))

        @pl.when(j == 0)
        def _():
            x = h_ref[...]
            r = lax.rsqrt(jnp.mean(x * x, axis=-1, keepdims=True) + EPS)
            xn_ref[...] = ((x * r) * g_ref[...]).astype(xn_ref.dtype)
        o_ref[...] = jnp.dot(xn_ref[...], w_ref[...], preferred_element_type=F32).astype(o_ref.dtype)

    return pl.pallas_call(
        body, name=name, grid=(ni, nj),
        in_specs=[pl.BlockSpec((tm, k), lambda i, j: (i, 0)), pl.BlockSpec((1, k), lambda i, j: (0, 0)), w_spec] + c_in_specs,
        out_specs=[pl.BlockSpec((tm, k), lambda i, j: (i, 0)), pl.BlockSpec((tm, tn), lambda i, j: (i, j))] + c_out_specs,
        out_shape=[jax.ShapeDtypeStruct((m, k), _MXU), jax.ShapeDtypeStruct((m, n), out_dtype)] + c_out_shapes,
        scratch_shapes=c_sems,
        compiler_params=_cp(("arbitrary", "arbitrary") if comm is not None else ("parallel", "arbitrary")),
    )(h, gamma, w, *c_args)


def _mm_nn(a, w, *, tm, tn, name, residual=None):
    m, k = a.shape
    n = w.shape[1]

    def body(*refs):
        if residual is None:
            a_ref, w_ref, o_ref = refs
            o_ref[...] = _dot(a_ref[...], w_ref[...])
        else:
            a_ref, w_ref, r_ref, o_ref = refs
            o_ref[...] = _dot(a_ref[...], w_ref[...]) + r_ref[...]

    in_specs = [pl.BlockSpec((tm, k), lambda i, j: (i, 0)), pl.BlockSpec((k, tn), lambda i, j: (0, j))]
    args = [a, w]
    if residual is not None:
        in_specs.append(pl.BlockSpec((tm, tn), lambda i, j: (i, j)))
        args.append(residual)
    return pl.pallas_call(
        body, name=name, grid=(m // tm, n // tn), in_specs=in_specs,
        out_specs=pl.BlockSpec((tm, tn), lambda i, j: (i, j)),
        out_shape=jax.ShapeDtypeStruct((m, n), F32),
        compiler_params=_cp(("parallel", "parallel")),
    )(*args)


def _wgrad(a, b, *, tt, ta, tn, name, out_shape, out_block, out_index, a_tab=None, o_tab=None, into=None):
    t = a.shape[0]
    a_tab = list(range(a.shape[1] // ta)) if a_tab is None else a_tab
    o_tab = a_tab if o_tab is None else o_tab
    nb = b.shape[1] // tn

    def body(at_ref, ot_ref, a_ref, b_ref, *rest):
        del at_ref, ot_ref
        o_ref = rest[-1]

        @pl.when(pl.program_id(2) == 0)
        def _():
            o_ref[...] = jnp.zeros_like(o_ref)
        o_ref[...] += _dot_tn(a_ref[...], b_ref[...])

    in_specs = [pl.BlockSpec((tt, ta), lambda r, j, i, at, ot: (i, at[r])),
                pl.BlockSpec((tt, tn), lambda r, j, i, at, ot: (i, j))]
    args = [jnp.asarray(a_tab, jnp.int32), jnp.asarray(o_tab, jnp.int32), a, b]
    aliases = {}
    if into is not None:
        in_specs.append(pl.BlockSpec(memory_space=pl.ANY))
        args.append(into)
        aliases = {4: 0}
    return pl.pallas_call(
        body, name=name,
        grid_spec=pltpu.PrefetchScalarGridSpec(
            num_scalar_prefetch=2, grid=(len(a_tab), nb, t // tt), in_specs=in_specs,
            out_specs=pl.BlockSpec(out_block, lambda r, j, i, at, ot: out_index(ot[r], j))),
        out_shape=jax.ShapeDtypeStruct(out_shape, F32), input_output_aliases=aliases,
        compiler_params=_cp(("parallel", "parallel", "arbitrary")),
    )(*args)


def _mm_nt(a, w, *, tm, name):
    m, kc = a.shape
    n = w.shape[0]

    def body(a_ref, w_ref, o_ref):
        o_ref[...] = _dot_nt(a_ref[...], w_ref[...])

    return pl.pallas_call(
        body, name=name, grid=(m // tm,),
        in_specs=[pl.BlockSpec((tm, kc), lambda i: (i, 0)), pl.BlockSpec((n, kc), lambda i: (0, 0))],
        out_specs=pl.BlockSpec((tm, n), lambda i: (i, 0)),
        out_shape=jax.ShapeDtypeStruct((m, n), F32),
        compiler_params=_cp(("parallel",)),
    )(a, w)


def _mm_nt_rmsbwd(dy, w, x, gamma, dres, *, tm, tk, name, extra=None):
    m, kc = dy.shape
    nk = kc // tk
    if w.ndim == 3:
        assert w.shape[0] == nk and w.shape[2] == tk
        d = w.shape[1]
        w_spec = pl.BlockSpec((None, d, tk), lambda i, k: (k, 0, 0))
    else:
        d = w.shape[0]
        w_spec = pl.BlockSpec((d, tk), lambda i, k: (0, k))

    def body(*refs):
        if extra is None:
            dy_ref, w_ref, x_ref, g_ref, r_ref, dx_ref, dg_ref, acc_ref = refs
        else:
            dy_ref, w_ref, x_ref, g_ref, r_ref, dy2_ref, w2_ref, dx_ref, dg_ref, acc_ref = refs
        i, kk = pl.program_id(0), pl.program_id(1)

        @pl.when(kk == 0)
        def _():
            acc_ref[...] = jnp.zeros_like(acc_ref)

        @pl.when((i == 0) & (kk == 0))
        def _():
            dg_ref[...] = jnp.zeros_like(dg_ref)

        acc_ref[...] += _dot_nt(dy_ref[...], w_ref[...])

        @pl.when(kk == nk - 1)
        def _():
            dxn = acc_ref[...]
            if extra is not None:
                dxn = dxn + _dot_nt(dy2_ref[...], w2_ref[...])
            xv = x_ref[...]
            r = lax.rsqrt(jnp.mean(xv * xv, axis=-1, keepdims=True) + EPS)
            xh = xv * r
            dg_ref[0:1, :] += _rsum(dxn * xh)
            dxh = dxn * g_ref[...]
            dx_ref[...] = r_ref[...] + r * (dxh - xh * jnp.mean(dxh * xh, axis=-1, keepdims=True))

    in_specs = [pl.BlockSpec((tm, tk), lambda i, k: (i, k)), w_spec,
                pl.BlockSpec((tm, d), lambda i, k: (i, 0)), pl.BlockSpec((1, d), lambda i, k: (0, 0)),
                pl.BlockSpec((tm, d), lambda i, k: (i, 0))]
    args = [dy, w, x, gamma, dres]
    if extra is not None:
        k2 = extra[0].shape[1]
        in_specs += [pl.BlockSpec((tm, k2), lambda i, k: (i, 0)), pl.BlockSpec((d, k2), lambda i, k: (0, 0))]
        args += list(extra)
    return pl.pallas_call(
        body, name=name, grid=(m // tm, nk), in_specs=in_specs,
        out_specs=[pl.BlockSpec((tm, d), lambda i, k: (i, 0)), pl.BlockSpec((8, d), lambda i, k: (0, 0))],
        out_shape=[jax.ShapeDtypeStruct((m, d), F32), jax.ShapeDtypeStruct((8, d), F32)],
        scratch_shapes=[pltpu.VMEM((tm, d), F32)],
        compiler_params=_cp(("arbitrary", "arbitrary")),
    )(*args)


def _rsum8(x):
    acc = x[0:8]
    for g in range(1, x.shape[0] // 8):
        acc = acc + x[8 * g:8 * (g + 1)]
    return acc


def _lru_gates(x, prev8, cw_ref, cb_ref, wa_ref, wx_ref, ba_ref, bx_ref, lam_ref):
    u = _conv4(x, prev8, cw_ref, cb_ref)
    ra =_sigmoid(_dot(u, wa_ref[0]) + ba_ref[...])
    ia = _sigmoid(_dot(u, wx_ref[0]) + bx_ref[...])
    sp = _softplus(-lam_ref[...])
    log_a = -LRU_C * ra * sp
    a = jnp.exp(log_a)
    m2 = _one_minus_exp(2.0 * log_a)
    mult = jnp.sqrt(m2)
    return u, ra, ia, sp, a, m2, mult


def _lru_fwd(proj, lw, *, r, name):
    t = proj.shape[0]
    nt = t // r

    def body(xg_ref, xp_ref, cw_ref, cb_ref, wa_ref, wx_ref, ba_ref, bx_ref, lam_ref, hl_ref, ya_ref, carry_ref):
        i = pl.program_id(1)

        @pl.when(i == 0)
        def _():
            carry_ref[...] = jnp.zeros_like(carry_ref)

        x = xg_ref[:, 0:256]
        lg = xg_ref[:, 256:512]
        prev8 = jnp.where(i == 0, 0.0, xp_ref[:, 0:256])
        u, ra, ia, sp, a, m2, mult = _lru_gates(x, prev8, cw_ref, cb_ref, wa_ref, wx_ref, ba_ref, bx_ref, lam_ref)
        ac, hc = _lin_scan(a, mult * ia * u, False)
        h = hc + ac * carry_ref[0:1, :]
        hl_ref[...] = h
        carry_ref[0:1, :] = hl_ref[r - 1:r, :]
        g, _ = _gelu_and_grad(lg)
        ya_ref[...] = (g * h).astype(ya_ref.dtype)

    small = lambda rows: pl.BlockSpec((rows, 256), lambda j, i: (0, j))
    return pl.pallas_call(
        body, name=name, grid=(4, nt),
        in_specs=[pl.BlockSpec((r, LBLK), lambda j, i: (i, _lblk_col(j))),
                  pl.BlockSpec((8, LBLK), lambda j, i: (jnp.maximum(i * (r // 8) - 1, 0), _lblk_col(j))),
                  small(4), small(1),
                  pl.BlockSpec((1, 256, 256), lambda j, i: (j, 0, 0)), pl.BlockSpec((1, 256, 256), lambda j, i: (j, 0, 0)),
                  small(1), small(1), small(1)],
        out_specs=[pl.BlockSpec((r, 256), lambda j, i: (i, j)), pl.BlockSpec((r, 256), lambda j, i: (i, j))],
        out_shape=[jax.ShapeDtypeStruct((t, D), F32), jax.ShapeDtypeStruct((t, D), _MXU)],
        scratch_shapes=[pltpu.VMEM((8, 256), F32)],
        compiler_params=_cp(("parallel", "arbitrary")),
    )(proj, proj, lw["cw"], lw["cb"], lw["wa"], lw["wx"], lw["ba"], lw["bx"], lw["lam"])


def _lru_bwd(proj, hl, dya, dproj, lw, *, r, name):
    t = proj.shape[0]
    nt = t // r

    def body(xg_ref, xp_ref, hl_ref, hp_ref, dya_ref, cw_ref, cb_ref, wa_ref, wx_ref, ba_ref, bx_ref, lam_ref, dproj_in,
             dproj_ref, sm_ref, dwa_ref, dwx_ref, carry_ref, du8_ref, row_scr):
        del dproj_in
        i = pl.program_id(1)

        @pl.when(i == 0)
        def _():
            carry_ref[...] = jnp.zeros_like(carry_ref)
            du8_ref[...] = jnp.zeros_like(du8_ref)
            sm_ref[...] = jnp.zeros_like(sm_ref)
            dwa_ref[...] = jnp.zeros_like(dwa_ref)
            dwx_ref[...] = jnp.zeros_like(dwx_ref)

        tile0 = i == nt - 1
        xp = xg_ref[:, 0:256]
        lg = xg_ref[:, 256:512]
        prev8 = jnp.where(tile0, 0.0, xp_ref[:, 0:256])
        u, ra, ia, sp, a, m2, mult = _lru_gates(xp, prev8, cw_ref, cb_ref, wa_ref, wx_ref, ba_ref, bx_ref, lam_ref)
        h = hl_ref[...]
        hprev = _shift_down(h, jnp.where(tile0, 0.0, hp_ref[...]), 1)
        dya_v = dya_ref[...]
        g, dg = _gelu_and_grad(lg)
        ac, lc = _lin_scan(_shift_up(a, carry_ref[...], 1), dya_v * g, True)
        lam_v = lc + ac * carry_ref[1:2, :]
        row_scr[0:8, :] = lam_v[0:8]
        row_scr[8:16, :] = a[0:8]
        carry_ref[1:2, :] = row_scr[0:1, :]
        carry_ref[0:1, :] = row_scr[8:9, :]
        da = lam_v * hprev
        dmult = lam_v * ia * u
        dia = lam_v * mult * u
        dlog = da * a - dmult * (1.0 - m2) / mult
        dra = -LRU_C * sp * dlog
        dpa = dra * ra * (1.0 - ra)
        dpx = dia * ia * (1.0 - ia)
        du = lam_v * mult * ia + _dot_nt(dpa, wa_ref[0]) + _dot_nt(dpx, wx_ref[0])
        dwa_ref[0] += _dot_tn(u, dpa)
        dwx_ref[0] += _dot_tn(u, dpx)
        dlx = du * cw_ref[3:4, :]
        sm_ref[24:32, :] += _rsum8(du * xp)
        for k in (1, 2, 3):
            du_k = _shift_up(du, du8_ref[...], k)
            dlx = dlx + du_k * cw_ref[3 - k:4 - k, :]
            sm_ref[8 * (3 - k):8 * (4 - k), :] += _rsum8(du_k * xp)
        du8_ref[...] = du[0:8]
        dproj_ref[:, 0:256] = dlx.astype(dproj_ref.dtype)
        dproj_ref[:, 256:512] = (dya_v * h * dg).astype(dproj_ref.dtype)
        sm_ref[32:40, :] += _rsum8(du)
        sm_ref[40:48, :] += _rsum8(dpa)
        sm_ref[48:56, :] += _rsum8(dpx)
        sm_ref[56:64, :] += _rsum8(-LRU_C * ra * dlog) * (-_sigmoid(-lam_ref[...]))

    rev = lambda i: nt - 1 - i
    small = lambda rows: pl.BlockSpec((rows, 256), lambda j, i: (0, j))
    wblk = pl.BlockSpec((1, 256, 256), lambda j, i: (j, 0, 0))
    n_in = 13
    return pl.pallas_call(
        body, name=name, grid=(4, nt),
        in_specs=[pl.BlockSpec((r, LBLK), lambda j, i: (rev(i), _lblk_col(j))),
                  pl.BlockSpec((8, LBLK), lambda j, i: (jnp.maximum(rev(i) * (r // 8) - 1, 0), _lblk_col(j))),
                  pl.BlockSpec((r, 256), lambda j, i: (rev(i), j)),
                  pl.BlockSpec((8, 256), lambda j, i: (jnp.maximum(rev(i) * (r // 8) - 1, 0), j)),
                  pl.BlockSpec((r, 256), lambda j, i: (rev(i), j)),
                  small(4), small(1), wblk, wblk, small(1), small(1), small(1),
                  pl.BlockSpec(memory_space=pl.ANY)],
        out_specs=[pl.BlockSpec((r, LBLK), lambda j, i: (rev(i), _lblk_col(j))),
                   pl.BlockSpec((64, 256), lambda j, i: (0, j)), wblk, wblk],
        out_shape=[jax.ShapeDtypeStruct(dproj.shape, dproj.dtype), jax.ShapeDtypeStruct((64, D), F32),
                   jax.ShapeDtypeStruct((4, 256, 256), F32), jax.ShapeDtypeStruct((4, 256, 256), F32)],
        scratch_shapes=[pltpu.VMEM((8, 256), F32), pltpu.VMEM((8, 256), F32), pltpu.VMEM((16, 256), F32)],
        input_output_aliases={n_in - 1: 0},
        compiler_params=_cp(("parallel", "arbitrary")),
    )(proj, proj, hl, hl, dya, lw["cw"], lw["cb"], lw["wa"], lw["wx"], lw["ba"], lw["bx"], lw["lam"], dproj)


def _head_cols(x):
    lane = lax.broadcasted_iota(jnp.int32, x.shape, 1)
    return [jnp.sum(jnp.where(lane == h, x, 0.0), axis=1, keepdims=True) for h in range(N_HEADS)]


def _compact_heads(blocks):
    lane = lax.broadcasted_iota(jnp.int32, blocks[0].shape, 1)
    lo = lane < HEAD_P
    out = jnp.zeros_like(blocks[0])
    for j, blk in enumerate(blocks):
        s_lo = jnp.sum(jnp.where(lo, blk, 0.0), axis=1, keepdims=True)
        s_hi = jnp.sum(jnp.where(lo, 0.0, blk), axis=1, keepdims=True)
        out = jnp.where(lane == 2 * j, s_lo, out)
        out = jnp.where(lane == 2 * j + 1, s_hi, out)
    return out


def _ssd_prelude(dtraw_ref, dtb_ref, alog_ref, dt_scr, a_scr):
    lane = lax.broadcasted_iota(jnp.int32, dt_scr.shape, 1)
    dt = jnp.where(lane < N_HEADS, _softplus(dtraw_ref[...] + dtb_ref[0:1, :]), 0.0)
    dt_scr[...] = dt
    a_scr[...] = dt * (-jnp.exp(alog_ref[0:1, :]))


def _ssd_chunk_scalars(dt_scr, a_scr, r_scr, r0):
    a_c = a_scr[pl.ds(r0, CHUNK), :]
    dt_c = dt_scr[pl.ds(r0, CHUNK), :]
    i0 = lax.broadcasted_iota(jnp.int32, (CHUNK, CHUNK), 0)
    i1 = lax.broadcasted_iota(jnp.int32, (CHUNK, CHUNK), 1)
    tri = jnp.where(i0 >= i1, 1.0, 0.0).astype(F32)
    cs = jnp.dot(tri, a_c, precision=_HI, preferred_element_type=F32)
    lane = lax.broadcasted_iota(jnp.int32, (CHUNK, 128), 1)
    srow = lax.broadcasted_iota(jnp.int32, (CHUNK, 128), 0)
    t_lo = jnp.where((lane < HEAD_P) & (srow <= lane), 1.0, 0.0).astype(F32)
    t_hi = jnp.where((lane >= HEAD_P) & (srow <= lane - HEAD_P), 1.0, 0.0).astype(F32)
    even = (lane % 2) == 0
    tn = (((0,), (0,)), ((), ()))
    r_scr[...] = (lax.dot_general(jnp.where(even, a_c, 0.0), t_lo, tn, precision=_HI, preferred_element_type=F32)
                  + lax.dot_general(jnp.where(even, 0.0, a_c), t_hi, tn, precision=_HI, preferred_element_type=F32))
    return cs, dt_c, _head_cols(cs), _head_cols(dt_c)


def _block_diag2(v):
    lo = lax.broadcasted_iota(jnp.int32, v.shape, 1) < HEAD_P
    return jnp.concatenate([jnp.where(lo, v, 0.0), jnp.where(lo, 0.0, v)], axis=0).astype(_MXU)


def _ssd_pair(xc_scr, r_scr, cs_cols, dt_cols, s2, r0, j, s2t=None):
    lane = lax.broadcasted_iota(jnp.int32, (CHUNK, 128), 1)
    srow = lax.broadcasted_iota(jnp.int32, (CHUNK, 128), 0)
    lo = lane < HEAD_P
    csc = jnp.where(lo, cs_cols[2 * j], cs_cols[2 * j + 1])
    dtc = jnp.where(lo, dt_cols[2 * j], dt_cols[2 * j + 1])
    csr = r_scr[2 * j:2 * j + 1, :] + r_scr[2 * j + 1:2 * j + 2, :]
    dm = jnp.where((lane & (HEAD_P - 1)) <= srow, jnp.exp(jnp.minimum(csc - csr, 0.0)), 0.0)
    xs = xc_scr[pl.ds(r0, CHUNK), j * 128:(j + 1) * 128]
    xd = xs * dtc
    csl = jnp.sum(jnp.where(srow == CHUNK - 1, csc, 0.0), axis=0, keepdims=True)
    out = dict(csc=csc, dtc=dtc, dm=dm, m2=s2 * dm, xs=xs, xd=xd, rhs=_block_diag2(xd), e=jnp.exp(csc),
               w=jnp.exp(csl - csc), dec=jnp.exp(csl))
    if s2t is not None:
        out["mt2"] = s2t * jnp.where((lane & (HEAD_P - 1)) >= srow, jnp.exp(jnp.minimum(csr - csc, 0.0)), 0.0)
    return out


def _cat(parts):
    return jnp.concatenate(parts, axis=1)


def _ssd_fwd(proj, dtraw, sw, *, rb, name, comm=None):
    t = proj.shape[0]
    ns, cb = t // rb, rb // CHUNK
    c_args, c_in_specs, c_out_specs, c_out_shapes, c_sems = _comm_specs(comm)

    def body(*refs):
        zx_ref, zp_ref, dtraw_ref, cw_ref, cbias_ref, dtb_ref, alog_ref, dsk_ref, ng_ref = refs[:9]
        yssd_ref, yb_ref, st_ref = refs[9 + len(c_args):12 + len(c_args)]
        n_scr = 12 + len(c_args) + len(c_out_shapes)
        h_scr, xc_scr, dt_scr, a_scr, r_scr = refs[n_scr:n_scr + 5]
        i = pl.program_id(0)
        if comm is not None:
            _comm_steps(comm, refs, 9, 12 + len(c_args), i == 0, i == (3 * ns) // 4, i == ns - 1)

        @pl.when(i == 0)
        def _():
            h_scr[...] = jnp.zeros_like(h_scr)

        for j in range(XBC // 128):
            cs_, zc = slice(128 * j, 128 * (j + 1)), slice(2048 + 128 * j, 2048 + 128 * (j + 1))
            pre = _conv4(zx_ref[:, zc], jnp.where(i == 0, 0.0, zp_ref[:, zc]), cw_ref, cbias_ref, cs_)
            xc_scr[:, cs_] = pre * _sigmoid(pre)
        _ssd_prelude(dtraw_ref, dtb_ref, alog_ref, dt_scr, a_scr)

        def chunk(c, carry):
            r0 = pl.multiple_of(c * CHUNK, CHUNK)
            _, _, cs_cols, dt_cols = _ssd_chunk_scalars(dt_scr, a_scr, r_scr, r0)
            st_ref[c] = h_scr[...]
            for g in range(N_GROUPS):
                bg = xc_scr[pl.ds(r0, CHUNK), 2048 + 128 * g:2048 + 128 * (g + 1)]
                cg = xc_scr[pl.ds(r0, CHUNK), 2560 + 128 * g:2560 + 128 * (g + 1)]
                s2 = _dot_nt(cg, jnp.concatenate([bg, bg], axis=0))
                hp = h_scr[:, 512 * g:512 * (g + 1)]
                yoff = _dot(cg, hp)
                xdw, dec = [], []
                for jj in range(4):
                    j = 4 * g + jj
                    p = _ssd_pair(xc_scr, r_scr, cs_cols, dt_cols, s2, r0, j)
                    y = _dot(p["m2"], p["rhs"]) + yoff[:, 128 * jj:128 * (jj + 1)] * p["e"]
                    yssd_ref[pl.ds(r0, CHUNK), 128 * j:128 * (j + 1)] = y + dsk_ref[0:1, 128 * j:128 * (j + 1)] * p["xs"]
                    xdw.append(p["xd"] * p["w"])
                    dec.append(p["dec"])
                h_scr[:, 512 * g:512 * (g + 1)] = hp * _cat(dec) + _dot_tn(bg, _cat(xdw))
            return carry

        lax.fori_loop(0, cb, chunk, 0)
        for g in range(N_GROUPS):
            sl = slice(512 * g, 512 * (g + 1))
            for q in range(rb // NORM_ROWS):
                rw = slice(NORM_ROWS * q, NORM_ROWS * (q + 1))
                yz = yssd_ref[rw, sl] * _silu(zx_ref[rw, sl])
                rg = lax.rsqrt(jnp.mean(yz * yz, axis=-1, keepdims=True) + EPS)
                yb_ref[rw, sl] = (yz * rg * ng_ref[0:1, sl]).astype(yb_ref.dtype)

    full = lambda rows, cols: pl.BlockSpec((rows, cols), lambda i: (0, 0))
    return pl.pallas_call(
        body, name=name, grid=(ns,),
        in_specs=[pl.BlockSpec((rb, ZX_W), lambda i: (i, 0)),
                  pl.BlockSpec((8, ZX_W), lambda i: (jnp.maximum(i * (rb // 8) - 1, 0), 0)),
                  pl.BlockSpec((rb, DT_PAD), lambda i: (i, 0)),
                  full(4, XBC), full(1, XBC), full(1, DT_PAD), full(1, DT_PAD), full(1, SSD_INNER), full(1, SSD_INNER)]
        + c_in_specs,
        out_specs=[pl.BlockSpec((rb, SSD_INNER), lambda i: (i, 0)), pl.BlockSpec((rb, SSD_INNER), lambda i: (i, 0)),
                   pl.BlockSpec((cb, N_STATE, SSD_INNER), lambda i: (i, 0, 0))] + c_out_specs,
        out_shape=[jax.ShapeDtypeStruct((t, SSD_INNER), F32), jax.ShapeDtypeStruct((t, SSD_INNER), _MXU),
                   jax.ShapeDtypeStruct((t // CHUNK, N_STATE, SSD_INNER), F32)] + c_out_shapes,
        scratch_shapes=[pltpu.VMEM((N_STATE, SSD_INNER), F32), pltpu.VMEM((rb, XBC), F32), pltpu.VMEM((rb, DT_PAD), F32),
                        pltpu.VMEM((rb, DT_PAD), F32), pltpu.VMEM((128, 128), F32)] + c_sems,
        compiler_params=_cp(("arbitrary",)),
    )(proj, proj, dtraw, sw["cw"], sw["cb"], sw["dtb"], sw["alog"], sw["dsk"], sw["ng"], *c_args)


def _ssd_bwd(proj, dtraw, yssd, states, dyb, dproj, sw, *, rb, name):
    t = proj.shape[0]
    ns, cb = t // rb, rb // CHUNK

    def body(zx_ref, zp_ref, dtraw_ref, yssd_ref, st_ref, dyb_ref, cw_ref, cbias_ref, dtb_ref, alog_ref, dsk_ref, ng_ref,
             dproj_in, dzx_ref, ddt_ref, gconv_ref, gch_ref, ghd_ref,
             dht_scr, xc_scr, dsl_scr, dy_scr, dxc_scr, dt_scr, a_scr, r_scr, dp8_scr):
        del dproj_in
        i = pl.program_id(0)

        @pl.when(i == 0)
        def _():
            dht_scr[...] = jnp.zeros_like(dht_scr)
            dp8_scr[...] = jnp.zeros_like(dp8_scr)
            gconv_ref[...] = jnp.zeros_like(gconv_ref)
            gch_ref[...] = jnp.zeros_like(gch_ref)
            ghd_ref[...] = jnp.zeros_like(ghd_ref)

        tile0 = i == ns - 1
        for j in range(XBC // 128):
            cs_, zc = slice(128 * j, 128 * (j + 1)), slice(2048 + 128 * j, 2048 + 128 * (j + 1))
            pre = _conv4(zx_ref[:, zc], jnp.where(tile0, 0.0, zp_ref[:, zc]), cw_ref, cbias_ref, cs_)
            sg = _sigmoid(pre)
            xc_scr[:, cs_] = pre * sg
            dsl_scr[:, cs_] = sg * (1.0 + pre * (1.0 - sg))
        _ssd_prelude(dtraw_ref, dtb_ref, alog_ref, dt_scr, a_scr)

        for g in range(N_GROUPS):
            sl = slice(512 * g, 512 * (g + 1))
            for q in range(rb // NORM_ROWS):
                rw = slice(NORM_ROWS * q, NORM_ROWS * (q + 1))
                zv = zx_ref[rw, sl]
                ys = yssd_ref[rw, sl]
                sg = _sigmoid(zv)
                sz = zv * sg
                yz = ys * sz
                rg = lax.rsqrt(jnp.mean(yz * yz, axis=-1, keepdims=True) + EPS)
                yn = yz * rg
                dyb_v = dyb_ref[rw, sl]
                gch_ref[0:8, sl] += _rsum8(dyb_v * yn)
                dyn = dyb_v * ng_ref[0:1, sl]
                dyz = rg * (dyn - yn * jnp.mean(dyn * yn, axis=-1, keepdims=True))
                dy_scr[rw, sl] = dyz * sz
                dzx_ref[rw, sl] = (dyz * ys * (sg * (1.0 + zv * (1.0 - sg)))).astype(dzx_ref.dtype)

        a_row = -jnp.exp(alog_ref[0:1, :])

        def chunk(cc, carry):
            c = cb - 1 - cc
            r0 = pl.multiple_of(c * CHUNK, CHUNK)
            rows = pl.ds(r0, CHUNK)
            _, dt_c, cs_cols, dt_cols = _ssd_chunk_scalars(dt_scr, a_scr, r_scr, r0)
            lane = lax.broadcasted_iota(jnp.int32, (CHUNK, 128), 1)
            srow = lax.broadcasted_iota(jnp.int32, (CHUNK, 128), 0)
            lo = lane < HEAD_P
            last = srow == CHUNK - 1
            p1_blocks, p3_blocks = [], []
            for g in range(N_GROUPS):
                gs = slice(512 * g, 512 * (g + 1))
                bg = xc_scr[rows, 2048 + 128 * g:2048 + 128 * (g + 1)]
                cg = xc_scr[rows, 2560 + 128 * g:2560 + 128 * (g + 1)]
                b2 = jnp.concatenate([bg, bg], axis=0)
                s2 = _dot_nt(cg, b2)
                s2t = _dot_nt(bg, jnp.concatenate([cg, cg], axis=0))
                hp = st_ref[c, :, gs]
                dht = dht_scr[:, gs]
                yoff = _dot(cg, hp)
                ps = [_ssd_pair(xc_scr, r_scr, cs_cols, dt_cols, s2, r0, 4 * g + jj, s2t) for jj in range(4)]
                dys = [dy_scr[rows, 128 * (4 * g + jj):128 * (4 * g + jj + 1)] for jj in range(4)]
                dye = _cat([dys[jj] * ps[jj]["e"] for jj in range(4)])
                w_g = _cat([p["w"] for p in ps])
                dcg = _dot_nt(dye, hp)
                dht_scr[:, gs] = _dot_tn(cg, dye) + _cat([p["dec"] for p in ps]) * dht
                dxd_state = w_g * _dot(bg, dht)
                dbg = _dot_nt(_cat([p["xd"] for p in ps]) * w_g, dht)
                tsum = _rsum(dht * hp)
                ds2 = jnp.zeros((CHUNK, 128), F32)
                for jj in range(4):
                    j = 4 * g + jj
                    ls = slice(128 * j, 128 * (j + 1))
                    p, dy2 = ps[jj], dys[jj]
                    dy_bd = _block_diag2(dy2)
                    dm2 = _dot_nt(dy2, p["rhs"])
                    ds2 = ds2 + dm2 * p["dm"]
                    gdiff = dm2 * p["m2"] - _dot_nt(p["xd"], dy_bd) * p["mt2"]
                    dxs = dxd_state[:, 128 * jj:128 * (jj + 1)]
                    dxd = _dot(p["mt2"], dy_bd) + dxs
                    end_row = _rsum(p["xd"] * dxs) + p["dec"] * tsum[:, 128 * jj:128 * (jj + 1)]
                    p1_blocks.append(gdiff + dy2 * yoff[:, 128 * jj:128 * (jj + 1)] * p["e"] - p["xd"] * dxs
                                     + jnp.where(last, end_row, 0.0))
                    p3_blocks.append(dxd * p["xs"])
                    dxc_scr[rows, ls] = dxd * p["dtc"] + dy2 * dsk_ref[0:1, ls]
                    gch_ref[8:16, ls] += _rsum8(dy2 * p["xs"])
                dcg = dcg + _dot(ds2, b2)
                rb2 = _dot_tn(ds2, cg)
                dxc_scr[rows, 2048 + 128 * g:2048 + 128 * (g + 1)] = dbg + rb2[0:CHUNK] + rb2[CHUNK:2 * CHUNK]
                dxc_scr[rows, 2560 + 128 * g:2560 + 128 * (g + 1)] = dcg
            dcs = _compact_heads(p1_blocks)
            i0 = lax.broadcasted_iota(jnp.int32, (CHUNK, CHUNK), 0)
            i1 = lax.broadcasted_iota(jnp.int32, (CHUNK, CHUNK), 1)
            triu = jnp.where(i1 >= i0, 1.0, 0.0).astype(F32)
            da = jnp.dot(triu, dcs, precision=_HI, preferred_element_type=F32)
            ddt = _compact_heads(p3_blocks) + da * a_row
            ddtraw = jnp.where(lane < N_HEADS, ddt * _sigmoid(dtraw_ref[rows, :] + dtb_ref[0:1, :]), 0.0)
            ddt_ref[rows, :] = ddtraw.astype(ddt_ref.dtype)
            ghd_ref[0:1, :] += _rsum(ddtraw)
            ghd_ref[1:2, :] += _rsum(da * dt_c) * a_row
            return carry

        lax.fori_loop(0, cb, chunk, 0)
        for j in range(XBC // 128):
            cs_, zc = slice(128 * j, 128 * (j + 1)), slice(2048 + 128 * j, 2048 + 128 * (j + 1))
            dpre = dxc_scr[:, cs_] * dsl_scr[:, cs_]
            xraw = zx_ref[:, zc]
            dx = dpre * cw_ref[3:4, cs_]
            gconv_ref[24:32, cs_] += _rsum8(dpre * xraw)
            for k in (1, 2, 3):
                dpre_k = _shift_up(dpre, dp8_scr[:, cs_], k)
                dx = dx + dpre_k * cw_ref[3 - k:4 - k, cs_]
                gconv_ref[8 * (3 - k):8 * (4 - k), cs_] += _rsum8(dpre_k * xraw)
            dzx_ref[:, zc] = dx.astype(dzx_ref.dtype)
            dp8_scr[:, cs_] = dpre[0:8]
            gconv_ref[32:40, cs_] += _rsum8(dpre)

    rev = lambda i: ns - 1 - i
    full = lambda rows, cols: pl.BlockSpec((rows, cols), lambda i: (0, 0))
    n_in = 13
    return pl.pallas_call(
        body, name=name, grid=(ns,),
        in_specs=[pl.BlockSpec((rb, ZX_W), lambda i: (rev(i), 0)),
                  pl.BlockSpec((8, ZX_W), lambda i: (jnp.maximum(rev(i) * (rb // 8) - 1, 0), 0)),
                  pl.BlockSpec((rb, DT_PAD), lambda i: (rev(i), 0)),
                  pl.BlockSpec((rb, SSD_INNER), lambda i: (rev(i), 0)),
                  pl.BlockSpec((cb, N_STATE, SSD_INNER), lambda i: (rev(i), 0, 0)),
                  pl.BlockSpec((rb, SSD_INNER), lambda i: (rev(i), 0)),
                  full(4, XBC), full(1, XBC), full(1, DT_PAD), full(1, DT_PAD), full(1, SSD_INNER), full(1, SSD_INNER),
                  pl.BlockSpec(memory_space=pl.ANY)],
        out_specs=[pl.BlockSpec((rb, ZX_W), lambda i: (rev(i), 0)), pl.BlockSpec((rb, DT_PAD), lambda i: (rev(i), 0)),
                   full(40, XBC), full(16, SSD_INNER), full(8, DT_PAD)],
        out_shape=[jax.ShapeDtypeStruct(dproj.shape, dproj.dtype), jax.ShapeDtypeStruct((t, DT_PAD), _MXU),
                   jax.ShapeDtypeStruct((40, XBC), F32), jax.ShapeDtypeStruct((16, SSD_INNER), F32),
                   jax.ShapeDtypeStruct((8, DT_PAD), F32)],
        scratch_shapes=[pltpu.VMEM((N_STATE, SSD_INNER), F32), pltpu.VMEM((rb, XBC), F32), pltpu.VMEM((rb, XBC), F32),
                        pltpu.VMEM((rb, SSD_INNER), F32), pltpu.VMEM((rb, XBC), F32), pltpu.VMEM((rb, DT_PAD), F32),
                        pltpu.VMEM((rb, DT_PAD), F32), pltpu.VMEM((128, 128), F32), pltpu.VMEM((8, XBC), F32)],
        input_output_aliases={n_in - 1: 0},
        compiler_params=_cp(("arbitrary",)),
    )(proj, proj, dtraw, yssd, states, dyb, sw["cw"], sw["cb"], sw["dtb"], sw["alog"], sw["dsk"], sw["ng"], dproj)


def _branch_merge(ya, yb, proj, wba, wbb, bgate, *, tm, tn, name):
    t = ya.shape[0]
    nj = D // tn

    def body(ya_ref, yb_ref, ga_ref, gb_ref, wba_ref, wbb_ref, ba_ref, bb_ref, ta_ref, tb_ref, mg_ref):
        ta = _dot(ya_ref[...], wba_ref[...])
        tb = _dot(yb_ref[...], wbb_ref[...])
        ta_ref[...] = ta.astype(ta_ref.dtype)
        tb_ref[...] = tb.astype(tb_ref.dtype)
        ga = _sigmoid(ga_ref[...] + ba_ref[...])
        gb = _sigmoid(gb_ref[...] + bb_ref[...])
        mg_ref[...] = (ga * ta + gb * tb).astype(mg_ref.dtype)

    tile = pl.BlockSpec((tm, tn), lambda i, j: (i, j))
    return pl.pallas_call(
        body, name=name, grid=(t // tm, nj),
        in_specs=[pl.BlockSpec((tm, D), lambda i, j: (i, 0)), pl.BlockSpec((tm, SSD_INNER), lambda i, j: (i, 0)),
                  pl.BlockSpec((tm, tn), lambda i, j: (i, G0 // tn + j)),
                  pl.BlockSpec((tm, tn), lambda i, j: (i, (G0 + D) // tn + j)),
                  pl.BlockSpec((D, tn), lambda i, j: (0, j)), pl.BlockSpec((SSD_INNER, tn), lambda i, j: (0, j)),
                  pl.BlockSpec((1, tn), lambda i, j: (0, j)), pl.BlockSpec((1, tn), lambda i, j: (0, nj + j))],
        out_specs=[tile, tile, tile],
        out_shape=[jax.ShapeDtypeStruct((t, D), _MXU)] * 3,
        compiler_params=_cp(("parallel", "parallel")),
    )(ya, yb, proj, proj, wba, wbb, bgate, bgate)


def _swiglu_mm(gu, wfo, residual, *, tm, tn, name):
    t = gu.shape[0]

    def body(gu_ref, w_ref, r_ref, act_ref, o_ref):
        @pl.when(pl.program_id(1) == 0)
        def _():
            gate = gu_ref[:, 0:D_FF].astype(F32)
            act_ref[...] = (_silu(gate) * gu_ref[:, D_FF:2 * D_FF].astype(F32)).astype(act_ref.dtype)
        o_ref[...] = jnp.dot(act_ref[...], w_ref[...], preferred_element_type=F32) + r_ref[...]

    return pl.pallas_call(
        body, name=name, grid=(t // tm, D // tn),
        in_specs=[pl.BlockSpec((tm, 2 * D_FF), lambda i, j: (i, 0)), pl.BlockSpec((D_FF, tn), lambda i, j: (0, j)),
                  pl.BlockSpec((tm, tn), lambda i, j: (i, j))],
        out_specs=[pl.BlockSpec((tm, D_FF), lambda i, j: (i, 0)), pl.BlockSpec((tm, tn), lambda i, j: (i, j))],
        out_shape=[jax.ShapeDtypeStruct((t, D_FF), _MXU), jax.ShapeDtypeStruct((t, D), F32)],
        compiler_params=_cp(("parallel", "arbitrary")),
    )(gu, wfo, residual)


def _ffn_bwd_act(dh, wfo, gu, *, tm, name):
    t = dh.shape[0]

    def body(dh_ref, w_ref, gu_ref, o_ref):
        dact = _dot_nt(dh_ref[...], w_ref[...])
        g = gu_ref[:, 0:D_FF].astype(F32)
        u = gu_ref[:, D_FF:2 * D_FF].astype(F32)
        sg = _sigmoid(g)
        o_ref[:, 0:D_FF] = (dact * u * (sg * (1.0 + g * (1.0 - sg)))).astype(o_ref.dtype)
        o_ref[:, D_FF:2 * D_FF] = (dact * (g * sg)).astype(o_ref.dtype)

    return pl.pallas_call(
        body, name=name, grid=(t // tm,),
        in_specs=[pl.BlockSpec((tm, D), lambda i: (i, 0)), pl.BlockSpec((D_FF, D), lambda i: (0, 0)),
                  pl.BlockSpec((tm, 2 * D_FF), lambda i: (i, 0))],
        out_specs=pl.BlockSpec((tm, 2 * D_FF), lambda i: (i, 0)),
        out_shape=jax.ShapeDtypeStruct((t, 2 * D_FF), _MXU),
        compiler_params=_cp(("parallel",)),
    )(dh, wfo, gu)


def _outproj_bwd(dh, wout, ta, tb, proj, bgate, dproj, *, tm, name):
    t = dh.shape[0]

    def body(dh_ref, w_ref, ta_ref, tb_ref, g_ref, b_ref, dta_ref, dtb_ref, dg_ref, db_ref):
        @pl.when(pl.program_id(0) == 0)
        def _():
            db_ref[...] = jnp.zeros_like(db_ref)
        dm = _dot_nt(dh_ref[...], w_ref[...])
        ga = _sigmoid(g_ref[:, 0:D] + b_ref[:, 0:D])
        gb = _sigmoid(g_ref[:, D:2 * D] + b_ref[:, D:2 * D])
        dta_ref[...] = (dm * ga).astype(dta_ref.dtype)
        dtb_ref[...] = (dm * gb).astype(dtb_ref.dtype)
        dga = dm * ta_ref[...].astype(F32) * ga * (1.0 - ga)
        dgb = dm * tb_ref[...].astype(F32) * gb * (1.0 - gb)
        dg_ref[:, 0:D] = dga.astype(dg_ref.dtype)
        dg_ref[:, D:2 * D] = dgb.astype(dg_ref.dtype)
        db_ref[0:1, 0:D] += _rsum(dga)
        db_ref[0:1, D:2 * D] += _rsum(dgb)

    row = lambda cols: pl.BlockSpec((tm, cols), lambda i: (i, 0))
    return pl.pallas_call(
        body, name=name, grid=(t // tm,),
        in_specs=[row(D), pl.BlockSpec((D, D), lambda i: (0, 0)), row(D), row(D),
                  pl.BlockSpec((tm, 2 * D), lambda i: (i, G0 // (2 * D))), pl.BlockSpec((1, 2 * D), lambda i: (0, 0))],
        out_specs=[row(D), row(D), pl.BlockSpec((tm, 2 * D), lambda i: (i, G0 // (2 * D))),
                   pl.BlockSpec((8, 2 * D), lambda i: (0, 0))],
        out_shape=[jax.ShapeDtypeStruct((t, D), _MXU), jax.ShapeDtypeStruct((t, D), _MXU),
                   jax.ShapeDtypeStruct(dproj, _MXU), jax.ShapeDtypeStruct((8, 2 * D), F32)],
        compiler_params=_cp(("arbitrary",)),
    )(dh, wout, ta, tb, proj, bgate)


def _loss_head(h, gf, target, *, tm, name):
    t = h.shape[0]

    def body(h_ref, g_ref, t_ref, loss_ref, dg_ref, dh_ref):
        @pl.when(pl.program_id(0) == 0)
        def _():
            loss_ref[...] = jnp.zeros_like(loss_ref)
            dg_ref[...] = jnp.zeros_like(dg_ref)
        x = h_ref[...]
        r = lax.rsqrt(jnp.mean(x * x, axis=-1, keepdims=True) + EPS)
        xh = x * r
        err = xh * g_ref[...] - t_ref[...]
        loss_ref[...] += 0.5 * jnp.sum(jnp.mean(err * err, axis=-1, keepdims=True), axis=0, keepdims=True)
        dy = err * (1.0 / D)
        dg_ref[0:1, :] += _rsum(dy * xh)
        dxh = dy * g_ref[...]
        dh_ref[...] = r * (dxh - xh * jnp.mean(dxh * xh, axis=-1, keepdims=True))

    row = pl.BlockSpec((tm, D), lambda i: (i, 0))
    return pl.pallas_call(
        body, name=name, grid=(t // tm,),
        in_specs=[row, pl.BlockSpec((1, D), lambda i: (0, 0)), row],
        out_specs=[pl.BlockSpec((8, 128), lambda i: (0, 0)), pl.BlockSpec((8, D), lambda i: (0, 0)), row],
        out_shape=[jax.ShapeDtypeStruct((8, 128), F32), jax.ShapeDtypeStruct((8, D), F32), jax.ShapeDtypeStruct((t, D), F32)],
        compiler_params=_cp(("arbitrary",)),
    )(h, gf, target)


def _row_tile(rows, cols, limit_bytes=1 << 20):
    best = None
    for tr in range(8, rows + 1, 8):
        if rows % tr == 0 and tr * cols * 4 <= limit_bytes:
            best = tr
    return best if best is not None else rows


def _adamw(w, g, m, v, *, name):
    rows, cols = w.shape
    tr = _row_tile(rows, cols)

    def body(w_ref, g_ref, m_ref, v_ref, d_ref, nm_ref, nv_ref):
        gv = g_ref[...]
        nm = ADAM_B1 * m_ref[...] + (1.0 - ADAM_B1) * gv
        nv = ADAM_B2 * v_ref[...] + (1.0 - ADAM_B2) * (gv * gv)
        m_hat = nm / (1.0 - ADAM_B1 ** ADAM_STEP)
        v_hat = nv / (1.0 - ADAM_B2 ** ADAM_STEP)
        d_ref[...] = -ADAM_LR * (m_hat / (jnp.sqrt(v_hat) + ADAM_EPS) + ADAM_WD * w_ref[...])
        nm_ref[...] = nm
        nv_ref[...] = nv

    blk = pl.BlockSpec((tr, cols), lambda i: (i, 0))
    shp = jax.ShapeDtypeStruct((rows, cols), F32)
    return pl.pallas_call(
        body, name=name, grid=(rows // tr,), in_specs=[blk] * 4, out_specs=[blk] * 3, out_shape=[shp] * 3,
        compiler_params=_cp(("parallel",)),
    )(w, g, m, v)


def _bd256(w):
    w4 = w.reshape(4, 4, 64, 64)
    eye = jnp.eye(4, dtype=w.dtype)
    return (w4[:, :, :, None, :] * eye[None, :, None, :, None]).reshape(4, 256, 256)


def _bd256_diag(g):
    g5 = g.reshape(4, 4, 64, 4, 64)
    return jnp.stack([g5[:, a, :, a, :] for a in range(4)], axis=1).reshape(16, 64, 64)


FFN_SHARD = 2 * D_FF // 4
W_IN_SHARD = IN_DIM // 4
W_IN_ROWS = 9344


def _w_in_cols(shards, c0, c1):
    out = []
    for p in range(4):
        lo, hi = max(c0, W_IN_SHARD * p), min(c1, W_IN_SHARD * (p + 1))
        if lo < hi:
            out.append(shards[p][:, lo - W_IN_SHARD * p:hi - W_IN_SHARD * p])
    return out


def _layer_weights(w, conv, small, l):
    win = w["w_in"]
    lblk = [_w_in_cols(win, 256 * j, 256 * (j + 1)) + _w_in_cols(win, D + 256 * j, D + 256 * (j + 1)) for j in range(4)]
    wp = jnp.concatenate(_w_in_cols(win, 2048, 4096) + _w_in_cols(win, 4096, 7168) + lblk[0] + lblk[1]
                         + _w_in_cols(win, 7200, 9248) + lblk[2] + lblk[3], axis=1)
    wdt = jnp.pad(jnp.concatenate(_w_in_cols(win, 7168, 7200), axis=1), ((0, 0), (0, DT_PAD - N_HEADS)))
    row = lambda v: v.reshape(1, -1)
    pad_h = lambda v: jnp.pad(v.reshape(1, -1), ((0, 0), (0, DT_PAD - N_HEADS)))
    lw = dict(cw=conv["lru_conv_w"][l], cb=row(small["lru_conv_b"][l]),
              wa=_bd256(small["lru_w_a"][l]).astype(_MXU), wx=_bd256(small["lru_w_x"][l]).astype(_MXU),
              ba=row(small["lru_b_a"][l]), bx=row(small["lru_b_x"][l]), lam=row(small["lru_lambda"][l]))
    sw = dict(cw=conv["ssd_conv_w"][l], cb=row(small["ssd_conv_b"][l]), dtb=pad_h(small["ssd_dt_bias"][l]),
              alog=pad_h(small["ssd_A_log"][l]), dsk=row(jnp.repeat(small["ssd_D"][l], HEAD_P)),
              ng=row(small["ssd_norm_g"][l]))
    return dict(wp=wp, wdt=wdt, lw=lw, sw=sw, wba=w["w_branch"][0:D], wbb=w["w_branch"][D:3 * D],
                wout=w["w_out"], wfi=w["w_ffn_in"], wfo=w["w_ffn_out"],
                g1=row(small["norm1_g"][l]), g2=row(small["norm2_g"][l]), bgate=row(small["b_gate"][l]))


def _tiles(t):
    return dict(tmn=min(1024, t), tm=min(512, t), tm2=min(256, t), r=min(256, t), rb=min(128, t))


def _layer_fwd(h, lwt, l, comms=(None, None)):
    tl = _tiles(h.shape[0])
    n = f"l{l}_"
    xn, proj, *got_a = _norm_mm(h, lwt["g1"], lwt["wp"], tm=tl["tmn"], tn=1024, name=n + "in_proj", comm=comms[0])
    dtraw = _mm_nn(xn, lwt["wdt"], tm=tl["tm"], tn=DT_PAD, name=n + "dt_proj")
    hl, ya = _lru_fwd(proj, lwt["lw"], r=tl["r"], name=n + "lru_fwd")
    yssd, yb, states, *got_b = _ssd_fwd(proj, dtraw, lwt["sw"], rb=tl["rb"], name=n + "ssd_fwd", comm=comms[1])
    ta, tb, merged = _branch_merge(ya, yb, proj, lwt["wba"], lwt["wbb"], lwt["bgate"], tm=tl["tm"], tn=512, name=n + "merge")
    hmid = _mm_nn(merged, lwt["wout"], tm=tl["tm"], tn=512, name=n + "out_proj", residual=h)
    xn2, gu = _norm_mm(hmid, lwt["g2"], lwt["wfi"], tm=tl["tmn"], tn=FFN_SHARD, name=n + "ffn_in", out_dtype=_MXU)
    act, hout = _swiglu_mm(gu, lwt["wfo"], hmid, tm=tl["tm"], tn=512, name=n + "ffn_out")
    saved = dict(h=h, xn=xn, proj=proj, dtraw=dtraw, hl=hl, ya=ya, yssd=yssd, yb=yb, states=states, ta=ta, tb=tb,
                 merged=merged, hmid=hmid, xn2=xn2, gu=gu, act=act)
    return hout, saved, got_a, got_b


def _layer_bwd(dh, s, lwt, l, big):
    t = dh.shape[0]
    tl = _tiles(t)
    n = f"l{l}_"
    tt = tl["tm"]
    big = dict(big)

    def wgrad(key, a, b, name, **kw):
        big[key] = _wgrad(a, b, tt=tt, name=n + name, into=big.get(key), **kw)

    dgu = _ffn_bwd_act(dh, lwt["wfo"], s["gu"], tm=tl["tm2"], name=n + "ffn_act_bwd")
    wgrad("w_ffn_out", s["act"], dh, "ffn_out_wgrad", ta=D_FF, tn=1024, out_shape=(N_LAYERS, D_FF, D),
          out_block=(None, D_FF, 1024), out_index=lambda o, j: (l, o, j))
    wgrad("w_ffn_in", s["xn2"], dgu, "ffn_in_wgrad", ta=D, tn=FFN_SHARD, out_shape=(N_LAYERS, 4, D, FFN_SHARD),
          out_block=(None, None, D, FFN_SHARD), out_index=lambda o, j: (l, j, o, 0))
    dh1, dg2 = _mm_nt_rmsbwd(dgu, lwt["wfi"], s["hmid"], lwt["g2"], dh, tm=tl["tm"], tk=FFN_SHARD, name=n + "ffn_in_dgrad")
    dta, dtb, dproj, dbg = _outproj_bwd(dh1, lwt["wout"], s["ta"], s["tb"], s["proj"], lwt["bgate"], (t, NP),
                                        tm=tl["tm2"], name=n + "out_proj_bwd")
    rows_d = dict(ta=D, tn=512, out_block=(None, D, 512), out_index=lambda o, j: (l, o, j))
    wgrad("w_out", s["merged"], dh1, "out_proj_wgrad", out_shape=(N_LAYERS, D, D), **rows_d)
    dya = _mm_nt(dta, lwt["wba"], tm=tl["tm"], name=n + "branch_a_dgrad")
    dyb = _mm_nt(dtb, lwt["wbb"], tm=tl["tm"], name=n + "branch_b_dgrad")
    wgrad("w_branch", s["ya"], dta, "branch_a_wgrad", out_shape=(N_LAYERS, 3 * D, D), a_tab=[0], o_tab=[0], **rows_d)
    wgrad("w_branch", s["yb"], dtb, "branch_b_wgrad", out_shape=(N_LAYERS, 3 * D, D), a_tab=[0, 1], o_tab=[1, 2], **rows_d)
    dproj, lsm, dwa, dwx = _lru_bwd(s["proj"], s["hl"], dya, dproj, lwt["lw"], r=tl["r"], name=n + "lru_bwd")
    dproj, ddt, gconv, gch, ghd = _ssd_bwd(s["proj"], s["dtraw"], s["yssd"], s["states"], dyb, dproj, lwt["sw"],
                                           rb=tl["rb"], name=n + "ssd_bwd")
    lsm = lsm.reshape(8, 8, D).sum(axis=1)
    gconv = gconv.reshape(5, 8, XBC).sum(axis=1)
    gch = gch.reshape(2, 8, SSD_INNER).sum(axis=1)
    w_in = dict(tn=D, out_shape=(N_LAYERS, W_IN_ROWS, D), out_index=lambda o, j: (l, o, j))
    wgrad("w_in", dproj, s["xn"], "in_proj_wgrad", ta=1024, out_block=(None, 1024, D),
          a_tab=list(range(9)), o_tab=[2, 3, 4, 5, 6, 0, 7, 8, 1], **w_in)
    wgrad("w_in", ddt, s["xn"], "dt_proj_wgrad", ta=DT_PAD, out_block=(None, DT_PAD, D), a_tab=[0],
          o_tab=[NP // DT_PAD], **w_in)
    dh0, dg1 = _mm_nt_rmsbwd(dproj, lwt["wp"], s["h"], lwt["g1"], dh1, tm=tl["tm"], tk=2304, name=n + "in_proj_dgrad",
                             extra=(ddt, lwt["wdt"]))
    grads = dict(
        lru_conv_w=lsm[0:4], lru_conv_b=lsm[4], lru_b_a=lsm[5], lru_b_x=lsm[6], lru_lambda=lsm[7],
        lru_w_a=_bd256_diag(dwa), lru_w_x=_bd256_diag(dwx),
        ssd_conv_w=gconv[0:4], ssd_conv_b=gconv[4], ssd_norm_g=gch[0], ssd_D=gch[1].reshape(N_HEADS, HEAD_P).sum(axis=-1),
        ssd_dt_bias=ghd[0, 0:N_HEADS], ssd_A_log=ghd[1, 0:N_HEADS],
        b_gate=dbg[0], norm1_g=dg1[0], norm2_g=dg2[0])
    return dh0, grads, big


def _local_step(x, target, w, conv, small, prefetch=None):
    h = x
    w = list(w)
    lwts, saved = [], []
    for l in range(N_LAYERS):
        lwt = _layer_weights(w[l], conv, small, l)
        h, s, got_a, got_b = _layer_fwd(h, lwt, l, prefetch[:2] if (prefetch is not None and l == 0) else (None, None))
        if prefetch is not None and l == 0:
            w.append(prefetch[2](got_a, got_b))
        lwts.append(lwt)
        saved.append(s)
    loss_blk, dgf, dh = _loss_head(h, small["norm_f"].reshape(1, D), target, tm=_tiles(x.shape[0])["tm"], name="loss_head")
    per_layer, big = [None] * N_LAYERS, {}
    for l in reversed(range(N_LAYERS)):
        dh, per_layer[l], big = _layer_bwd(dh, saved[l], lwts[l], l, big)
    grads = {k: jnp.stack([per_layer[l][k] for l in range(N_LAYERS)], axis=0) for k in per_layer[0]}
    grads["norm_f"] = dgf[0]
    return loss_blk, dh, grads, big


PACK_W = 1024
BIG = (("w_in", W_IN_SHARD, D, W_IN_SHARD, 256), ("w_branch", 768, D, 256, D), ("w_out", 256, D, 256, D),
       ("w_ffn_in", D, FFN_SHARD, 256, FFN_SHARD), ("w_ffn_out", 704, D, 352, D))
CONV = ("lru_conv_w", "ssd_conv_w")
SMALL = ("norm1_g", "b_gate", "lru_conv_b", "lru_w_a", "lru_b_a", "lru_w_x", "lru_b_x", "lru_lambda", "ssd_conv_b",
         "ssd_dt_bias", "ssd_A_log", "ssd_D", "ssd_norm_g", "norm2_g", "norm_f")
_WIRE = jnp.bfloat16
N_CHIPS = 4
N_DEV = 8


def _mesh_pos():
    return lax.axis_index("x"), lax.axis_index("y"), lax.axis_index("c")


HBM_SPEC = pl.BlockSpec(memory_space=pltpu.HBM)


def _remote(src, dst, send_sems, recv_sems, k, to):
    return pltpu.make_async_remote_copy(src_ref=src, dst_ref=dst, send_sem=send_sems.at[k], recv_sem=recv_sems.at[k],
                                        device_id=to, device_id_type=MESH)


def _other_chips(x, y):
    return [(1 - x, y), (x, 1 - y), (1 - x, 1 - y)]


def _weight_fetch(loc, layer, owner):
    names = list(owner)
    rows = {n: loc[n].shape[1] for n in names}
    by_chip = ("w_in", "w_ffn_in")
    shapes = [((N_CHIPS,) + loc[n].shape[1:]) if n in by_chip else (N_CHIPS * rows[n], D) for n in names]

    def place(o_ref, n, chip):
        if n in by_chip:
            return o_ref.at[chip]
        return o_ref.at[pl.ds(pl.multiple_of(chip * rows[n], 16), rows[n]), :]

    def step(which, in_refs, o_refs, send_sems, recv_sems):
        x, y, c = _mesh_pos()
        s = 2 * x + y
        sib = (x, y, 1 - c)
        chips = _other_chips(x, y)
        for core in (0, 1):
            @pl.when(c == core)
            def _():
                for k, n in enumerate(names):
                    for j, (px, py) in enumerate(chips):
                        landed = place(o_refs[k], n, 2 * px + py)
                        sent = _remote(in_refs[k].at[layer], place(o_refs[k], n, s), send_sems, recv_sems, 3 * k + j,
                                       (px, py, c))
                        arrives = _remote(in_refs[k].at[layer], landed, send_sems, recv_sems, 3 * k + j, (px, py, c))
                        passed = _remote(landed, landed, send_sems, recv_sems, 3 * (len(names) + k) + j, sib)
                        if owner[n] == core:
                            if which == "start":
                                sent.start()
                            elif which == "mid":
                                arrives.wait_recv()
                                passed.start()
                            else:
                                sent.wait_send()
                                passed.wait_send()
                        elif which == "end":
                            passed.wait_recv()

    return dict(inputs=[loc[n] for n in names], names=names,
                out_shapes=[jax.ShapeDtypeStruct(shp, loc[n].dtype) for shp, n in zip(shapes, names)],
                sems=[pltpu.SemaphoreType.DMA((6 * len(names),)), pltpu.SemaphoreType.DMA((6 * len(names),))],
                start=functools.partial(step, "start"), mid=functools.partial(step, "mid"),
                end=functools.partial(step, "end"))


def _fetch_now(fetch, name):
    n = len(fetch["inputs"])

    def body(*refs):
        parts = (refs[:n], refs[n:2 * n]) + tuple(refs[2 * n:])
        fetch["start"](*parts)
        fetch["mid"](*parts)
        fetch["end"](*parts)

    return pl.pallas_call(
        body, name=name, in_specs=[HBM_SPEC] * n, out_specs=[HBM_SPEC] * n, out_shape=fetch["out_shapes"],
        scratch_shapes=fetch["sems"],
    )(*fetch["inputs"])


def _sibling_exchange(gbufs):
    n = len(gbufs)

    def body(*refs):
        g_refs, o_refs, (send_sems, recv_sems) = refs[:n], refs[n:2 * n], refs[2 * n:]
        x, y, c = _mesh_pos()
        copies = [_remote(g_refs[k].at[1 - c], o_refs[k], send_sems, recv_sems, k, (x, y, 1 - c)) for k in range(n)]
        for cp in copies:
            cp.start()
        for cp in copies:
            cp.wait()

    return pl.pallas_call(
        body, name="grad_sibling_exchange", in_specs=[HBM_SPEC] * n, out_specs=[HBM_SPEC] * n,
        out_shape=[jax.ShapeDtypeStruct(g.shape[1:], g.dtype) for g in gbufs],
        scratch_shapes=[pltpu.SemaphoreType.DMA((n,)), pltpu.SemaphoreType.DMA((n,))],
    )(*gbufs)


def _add_layer(g, recv, c_idx, *, a, tr, tc, name):
    wd = g.shape[2]
    nr = a // tr

    def body(c_ref, g_ref, r_ref, o_ref):
        del c_ref
        o_ref[...] = (g_ref[...] + r_ref[...]).astype(o_ref.dtype)

    return pl.pallas_call(
        body, name=name,
        grid_spec=pltpu.PrefetchScalarGridSpec(
            num_scalar_prefetch=1, grid=(N_CHIPS, nr, wd // tc),
            in_specs=[pl.BlockSpec((None, tr, tc), lambda p, i, j, c_ref: (c_ref[0], p * nr + i, j)),
                      pl.BlockSpec((tr, tc), lambda p, i, j, c_ref: (p * nr + i, j))],
            out_specs=pl.BlockSpec((None, tr, tc), lambda p, i, j, c_ref: (p, i, j))),
        out_shape=jax.ShapeDtypeStruct((N_CHIPS, a, wd), _WIRE),
        compiler_params=_cp(("parallel", "parallel", "parallel")),
    )(c_idx, g, recv)


def _chip_exchange(parts):
    n = len(parts)

    def body(*refs):
        s_refs, o_refs, (send_sems, recv_sems) = refs[:n], refs[n:2 * n], refs[2 * n:]
        x, y, c = _mesh_pos()
        s = 2 * x + y
        chips = _other_chips(x, y)
        sends = [_remote(s_refs[k].at[2 * px + py], o_refs[k].at[s], send_sems, recv_sems, n * j + k, (px, py, c))
                 for j, (px, py) in enumerate(chips) for k in range(n)]
        for cp in sends:
            cp.start()
        for j, (px, py) in enumerate(chips):
            for k in range(n):
                p = 2 * px + py
                _remote(s_refs[k].at[p], o_refs[k].at[p], send_sems, recv_sems, n * j + k, (px, py, c)).wait_recv()
        for cp in sends:
            cp.wait_send()

    return pl.pallas_call(
        body, name="grad_chip_exchange", in_specs=[HBM_SPEC] * n, out_specs=[HBM_SPEC] * n,
        out_shape=[jax.ShapeDtypeStruct(p.shape, p.dtype) for p in parts],
        scratch_shapes=[pltpu.SemaphoreType.DMA((3 * n,)), pltpu.SemaphoreType.DMA((3 * n,))],
    )(*parts)


def _sum_slots(slots, own, sel, *, tr, tc, name, halves=False):
    n, rows, wd = slots.shape
    k = own.shape[0]

    def body(sel_ref, s_ref, own_ref, o_ref):
        mine = sel_ref[0]
        acc = jnp.zeros((tr, tc), F32)
        for p in range(n):
            acc = acc + jnp.where(mine == p, own_ref[...].astype(F32), s_ref[p].astype(F32))
        o_ref[...] = acc

    if halves:
        out_spec = pl.BlockSpec((None, tr, tc), lambda i, j, sel_ref: (sel_ref[1], i, j))
        out_shape = jax.ShapeDtypeStruct((2, rows, wd), F32)
    else:
        out_spec = pl.BlockSpec((tr, tc), lambda i, j, sel_ref: (i, j))
        out_shape = jax.ShapeDtypeStruct((rows, wd), F32)
    return pl.pallas_call(
        body, name=name,
        grid_spec=pltpu.PrefetchScalarGridSpec(
            num_scalar_prefetch=1, grid=(rows // tr, wd // tc),
            in_specs=[pl.BlockSpec((n, tr, tc), lambda i, j, sel_ref: (0, i, j)),
                      pl.BlockSpec((None, tr, tc), lambda i, j, sel_ref: (sel_ref[0] if k > 1 else 0, i, j))],
            out_specs=out_spec),
        out_shape=out_shape, compiler_params=_cp(("parallel", "parallel")),
    )(sel, slots, own)


def _sibling_share(both):
    n = len(both)

    def body(*refs):
        o_refs, (send_sems, recv_sems) = refs[n:2 * n], refs[2 * n:]
        x, y, c = _mesh_pos()
        sends = [_remote(o_refs[k].at[c], o_refs[k].at[c], send_sems, recv_sems, k, (x, y, 1 - c)) for k in range(n)]
        for cp in sends:
            cp.start()
        for k in range(n):
            _remote(o_refs[k].at[1 - c], o_refs[k].at[1 - c], send_sems, recv_sems, k, (x, y, 1 - c)).wait_recv()
        for cp in sends:
            cp.wait_send()

    return pl.pallas_call(
        body, name="grad_sibling_share", in_specs=[HBM_SPEC] * n, out_specs=[HBM_SPEC] * n,
        out_shape=[jax.ShapeDtypeStruct(b.shape, b.dtype) for b in both], input_output_aliases={k: k for k in range(n)},
        scratch_shapes=[pltpu.SemaphoreType.DMA((n,)), pltpu.SemaphoreType.DMA((n,))],
    )(*both)


def _allgather_devices(part, name):
    rows, wd = part.shape

    def body(p_ref, o_ref, send_sems, recv_sems):
        x, y, c = _mesh_pos()
        sib = (x, y, 1 - c)
        chips = _other_chips(x, y)
        slot = lambda px, py, pc: o_ref.at[4 * px + 2 * py + pc]
        first = [_remote(p_ref, slot(x, y, c), send_sems, recv_sems, 0, sib)]
        first += [_remote(p_ref, slot(x, y, c), send_sems, recv_sems, 1 + j, (px, py, c)) for j, (px, py) in enumerate(chips)]
        for cp in first:
            cp.start()
        passed = []
        for j, (px, py) in enumerate(chips):
            _remote(p_ref, slot(px, py, c), send_sems, recv_sems, 1 + j, (px, py, c)).wait_recv()
            cp = _remote(slot(px, py, c), slot(px, py, c), send_sems, recv_sems, 4 + j, sib)
            cp.start()
            passed.append(cp)
        _remote(p_ref, slot(x, y, 1 - c), send_sems, recv_sems, 0, sib).wait_recv()
        for j, (px, py) in enumerate(chips):
            _remote(slot(px, py, 1 - c), slot(px, py, 1 - c), send_sems, recv_sems, 4 + j, sib).wait_recv()
        for cp in first + passed:
            cp.wait_send()

    return pl.pallas_call(
        body, name=name, in_specs=[HBM_SPEC], out_specs=HBM_SPEC,
        out_shape=jax.ShapeDtypeStruct((N_DEV, rows, wd), part.dtype),
        scratch_shapes=[pltpu.SemaphoreType.DMA((N_DEV - 1,)), pltpu.SemaphoreType.DMA((N_DEV - 1,))],
    )(part)


def _by_chip_to_full(stack):
    _, nl, r, b = stack.shape
    return stack.transpose(1, 2, 0, 3).reshape(nl, r, N_CHIPS * b)


def _sharded_step(a):
    x = a["x"][0]
    target = a["loss_target"][0]
    cx, cy, cc = _mesh_pos()
    chip = (2 * cx + cy).astype(jnp.int32)
    core = cc.astype(jnp.int32)
    me = (4 * cx + 2 * cy + cc).astype(jnp.int32)
    zero = jnp.zeros((), jnp.int32)
    dus = lax.dynamic_update_slice

    loc = {n: a[n].astype(_MXU) for n, *_ in BIG}

    def with_own(got, names, layer):
        out = {}
        for g, n in zip(got, names):
            mine = loc[n][layer]
            out[n] = (dus(g, mine[None], (chip, zero, zero)) if g.ndim == 3 else dus(g, mine, (chip * mine.shape[0], zero)))
        return out

    now = _weight_fetch(loc, 0, {"w_in": 0, "w_out": 0, "w_branch": 1, "w_ffn_in": 1, "w_ffn_out": 1})
    w0 = with_own(_fetch_now(now, "allgather_weights"), now["names"], 0)
    later_a = _weight_fetch(loc, 1, {"w_in": 0})
    later_b = _weight_fetch(loc, 1, {"w_ffn_in": 0, "w_branch": 1, "w_out": 1, "w_ffn_out": 1})
    second = lambda got_a, got_b: {**with_own(got_a, later_a["names"], 1), **with_own(got_b, later_b["names"], 1)}
    conv_loc = jnp.concatenate([a[n].reshape(-1, PACK_W) for n in CONV], axis=0)
    conv_all = dus(_allgather_devices(conv_loc, "conv_weight_allgather"), conv_loc[None], (me, zero, zero))[0::2]
    conv, off = {}, 0
    for n in CONV:
        rows = a[n].size // PACK_W
        conv[n] = _by_chip_to_full(conv_all[:, off:off + rows].reshape((N_CHIPS,) + a[n].shape))
        off += rows
    small = {n: a[n] for n in SMALL}

    loss_blk, grad_x, grads, big = _local_step(x, target, [w0], conv, small, (later_a, later_b, second))
    loss = lax.psum(loss_blk[0, 0], ("x", "y", "c"))

    views = [big[n].reshape(N_LAYERS, -1, wd) for n, _, wd, _, _ in BIG]
    recv = _sibling_exchange(views)
    c_idx = core.reshape(1)
    parts = [_add_layer(v, r, c_idx, a=rows, tr=tr, tc=tc, name="grad_add_sibling_" + n)
             for v, r, (n, rows, _, tr, tc) in zip(views, recv, BIG)]
    slots = _chip_exchange(parts)
    sel = jnp.stack([chip, core])
    both = [_sum_slots(s, p, sel, tr=tr, tc=tc, name="grad_sum_chips_" + n, halves=True)
            for s, p, (n, _, _, tr, tc) in zip(slots, parts, BIG)]
    done = dict(zip([n for n, *_ in BIG], _sibling_share(both)))
    g_big = {n: done[n].reshape(a[n].shape) for n in ("w_branch", "w_out", "w_ffn_in", "w_ffn_out")}
    gt = done["w_in"].transpose(0, 2, 1)
    first = jnp.concatenate([gt[..., 512 * j + 256 * part:512 * j + 256 * (part + 1)] for part in range(2) for j in range(4)]
                            + [gt[..., 2 * D:]], axis=-1)
    tail = W_IN_SHARD - (IN_DIM - 7168)
    last = jnp.concatenate([gt[..., :tail], gt[..., W_IN_SHARD - N_HEADS:], gt[..., tail:W_IN_SHARD - N_HEADS]], axis=-1)
    g_big["w_in"] = jnp.where(chip == 0, first, jnp.where(chip == N_CHIPS - 1, last, gt))

    names = SMALL + CONV
    srows = -(-sum(grads[n].size for n in names) // (8 * PACK_W)) * 8
    flat = lambda d, ns: jnp.concatenate([d[n].reshape(-1) for n in ns])
    padto = lambda v: jnp.pad(v, (0, srows * PACK_W - v.shape[0])).reshape(srows, PACK_W)
    g_own = padto(flat(grads, names))
    g_sum = _sum_slots(_allgather_devices(g_own, "small_grad_allgather"), g_own[None], jnp.stack([me, zero]),
                       tr=srows, tc=PACK_W, name="small_grad_sum")
    off, g_small = 0, {}
    for n in names:
        g_small[n] = g_sum.reshape(-1)[off:off + grads[n].size].reshape(grads[n].shape)
        off += grads[n].size
    for n in CONV:
        width = a[n].shape[2]
        g_big[n] = lax.dynamic_slice(g_small.pop(n), (zero, zero, chip * width), a[n].shape)

    out_g, out_d, out_m, out_v = {}, {}, {}, {}
    for n in g_big:
        shp = a[n].shape
        two_d = (shp[0] * shp[1], shp[2])
        d_, m_, v_ = _adamw(a[n].reshape(two_d), g_big[n].reshape(two_d), a["m_" + n].reshape(two_d),
                            a["v_" + n].reshape(two_d), name="adamw_" + n)
        out_g[n], out_d[n], out_m[n], out_v[n] = g_big[n], d_.reshape(shp), m_.reshape(shp), v_.reshape(shp)
    d_, m_, v_ = _adamw(padto(flat(a, SMALL)), padto(flat(g_small, SMALL)), padto(flat({n: a["m_" + n] for n in SMALL}, SMALL)),
                        padto(flat({n: a["v_" + n] for n in SMALL}, SMALL)), name="adamw_small")
    off = 0
    for n in SMALL:
        cut = lambda v: v.reshape(-1)[off:off + a[n].size].reshape(a[n].shape)
        out_g[n], out_d[n], out_m[n], out_v[n] = g_small[n], cut(d_), cut(m_), cut(v_)
        off += a[n].size
    return loss, grad_x[None], out_g, out_d, out_m, out_v


WEIGHTS = ("norm1_g", "w_in", "b_gate", "lru_conv_w", "lru_conv_b", "lru_w_a", "lru_b_a", "lru_w_x", "lru_b_x", "lru_lambda",
           "ssd_conv_w", "ssd_conv_b", "ssd_dt_bias", "ssd_A_log", "ssd_D", "ssd_norm_g", "w_branch", "w_out", "norm2_g",
           "w_ffn_in", "w_ffn_out", "norm_f")
INPUTS = ("x",) + WEIGHTS + ("loss_target",) + tuple("m_" + n for n in WEIGHTS) + tuple("v_" + n for n in WEIGHTS)


def kernel(x, norm1_g, w_in, b_gate, lru_conv_w, lru_conv_b, lru_w_a, lru_b_a, lru_w_x, lru_b_x, lru_lambda, ssd_conv_w, ssd_conv_b, ssd_dt_bias, ssd_A_log, ssd_D, ssd_norm_g, w_branch, w_out, norm2_g, w_ffn_in, w_ffn_out, norm_f, loss_target, m_norm1_g, m_w_in, m_b_gate, m_lru_conv_w, m_lru_conv_b, m_lru_w_a, m_lru_b_a, m_lru_w_x, m_lru_b_x, m_lru_lambda, m_ssd_conv_w, m_ssd_conv_b, m_ssd_dt_bias, m_ssd_A_log, m_ssd_D, m_ssd_norm_g, m_w_branch, m_w_out, m_norm2_g, m_w_ffn_in, m_w_ffn_out, m_norm_f, v_norm1_g, v_w_in, v_b_gate, v_lru_conv_w, v_lru_conv_b, v_lru_w_a, v_lru_b_a, v_lru_w_x, v_lru_b_x, v_lru_lambda, v_ssd_conv_w, v_ssd_conv_b, v_ssd_dt_bias, v_ssd_A_log, v_ssd_D, v_ssd_norm_g, v_w_branch, v_w_out, v_norm2_g, v_w_ffn_in, v_w_ffn_out, v_norm_f):
    args = (x, norm1_g, w_in, b_gate, lru_conv_w, lru_conv_b, lru_w_a, lru_b_a, lru_w_x, lru_b_x, lru_lambda, ssd_conv_w, ssd_conv_b, ssd_dt_bias, ssd_A_log, ssd_D, ssd_norm_g, w_branch, w_out, norm2_g, w_ffn_in, w_ffn_out, norm_f, loss_target, m_norm1_g, m_w_in, m_b_gate, m_lru_conv_w, m_lru_conv_b, m_lru_w_a, m_lru_b_a, m_lru_w_x, m_lru_b_x, m_lru_lambda, m_ssd_conv_w, m_ssd_conv_b, m_ssd_dt_bias, m_ssd_A_log, m_ssd_D, m_ssd_norm_g, m_w_branch, m_w_out, m_norm2_g, m_w_ffn_in, m_w_ffn_out, m_norm_f, v_norm1_g, v_w_in, v_b_gate, v_lru_conv_w, v_lru_conv_b, v_lru_w_a, v_lru_b_a, v_lru_w_x, v_lru_b_x, v_lru_lambda, v_ssd_conv_w, v_ssd_conv_b, v_ssd_dt_bias, v_ssd_A_log, v_ssd_D, v_ssd_norm_g, v_w_branch, v_w_out, v_norm2_g, v_w_ffn_in, v_w_ffn_out, v_norm_f)
    assert len(args) == len(INPUTS)
    loss, grad_x, g, d, m, v = _sharded_step(dict(zip(INPUTS, args)))
    return (loss, grad_x, *[g[n] for n in WEIGHTS], *[d[n] for n in WEIGHTS], *[m[n] for n in WEIGHTS],
            *[v[n] for n in WEIGHTS])
```

```python
import functools
import math

import numpy as np
import jax
import jax.numpy as jnp
from jax import lax
from jax.experimental import pallas as pl
from jax.experimental.pallas import tpu as pltpu

F32 = jnp.float32
BF16 = jnp.bfloat16
_MXU = jnp.bfloat16
_HI = lax.Precision.HIGHEST

D = 1024
EPS = 1e-6
N_LAYERS = 2
LRU_C = 8.0
N_HEADS = 32
HEAD_P = 64
N_GROUPS = 4
N_STATE = 128
SSD_INNER = 2048
XBC = 3072
D_FF = 2816
CHUNK = 64
NORM_ROWS = 32
IN_DIM = 9248

NP = 9216
ZX_W = 5120
G0 = 6144
LBLK = 512
DT_PAD = 128

VMEM_LIMIT_BYTES_V7X = 56 * 1024 * 1024

ADAM_LR, ADAM_B1, ADAM_B2, ADAM_EPS, ADAM_WD, ADAM_STEP = 0.001, 0.9, 0.999, 1e-08, 0.01, 10
MESH = pl.DeviceIdType.MESH


def _cp(sem):
    return pltpu.CompilerParams(dimension_semantics=sem, vmem_limit_bytes=VMEM_LIMIT_BYTES_V7X)


def _lblk_col(j):
    return 10 + j + 4 * (j // 2)


def _sigmoid(x):
    return 0.5 * jnp.tanh(0.5 * x) + 0.5


def _softplus(x):
    return jnp.maximum(x, 0.0) + jnp.log(1.0 + jnp.exp(-jnp.abs(x)))


def _silu(x):
    return x * _sigmoid(x)


def _dsilu(x):
    s = _sigmoid(x)
    return s * (1.0 + x * (1.0 - s))


_GELU_C0 = math.sqrt(2.0 / math.pi)
_GELU_C1 = 0.044715


def _gelu_and_grad(x):
    t = jnp.tanh(_GELU_C0 * (x + _GELU_C1 * x * x * x))
    g = 0.5 * x * (1.0 + t)
    dg = 0.5 * (1.0 + t) + 0.5 * x * (1.0 - t * t) * _GELU_C0 * (1.0 + 3.0 * _GELU_C1 * x * x)
    return g, dg


def _one_minus_exp(x):
    p = 1.0 + x * (1.0 / 7.0)
    p = 1.0 + x * (1.0 / 6.0) * p
    p = 1.0 + x * (1.0 / 5.0) * p
    p = 1.0 + x * (1.0 / 4.0) * p
    p = 1.0 + x * (1.0 / 3.0) * p
    p = 1.0 + x * (1.0 / 2.0) * p
    return jnp.where(x > -0.3, -x * p, 1.0 - jnp.exp(x))


def _dot(a, b):
    return jnp.dot(a.astype(_MXU), b.astype(_MXU), preferred_element_type=F32)


def _dot_nt(a, b):
    return lax.dot_general(a.astype(_MXU), b.astype(_MXU), (((1,), (1,)), ((), ())), preferred_element_type=F32)


def _dot_tn(a, b):
    return lax.dot_general(a.astype(_MXU), b.astype(_MXU), (((0,), (0,)), ((), ())), preferred_element_type=F32)


def _shift_down(x, prev8, k):
    xr = pltpu.roll(x, k, 0)
    pr = pltpu.roll(prev8, k, 0)
    row = lax.broadcasted_iota(jnp.int32, prev8.shape, 0)
    head = jnp.where(row < k, pr, xr[0:8])
    return jnp.concatenate([head, xr[8:]], axis=0)


def _shift_up(x, next8, k):
    r = x.shape[0]
    xr = pltpu.roll(x, r - k, 0)
    nr = pltpu.roll(next8, 8 - k, 0)
    row = lax.broadcasted_iota(jnp.int32, next8.shape, 0)
    tail = jnp.where(row >= 8 - k, nr, xr[r - 8:r])
    return jnp.concatenate([xr[:r - 8], tail], axis=0)


def _conv4(x, prev8, w_ref, b_ref, cols=slice(None)):
    acc = x * w_ref[3:4, cols] + b_ref[0:1, cols]
    for k in (1, 2, 3):
        acc = acc + _shift_down(x, prev8, k) * w_ref[3 - k:4 - k, cols]
    return acc


def _conv4_bwd_x(dy, next8, w_ref, cols=slice(None)):
    acc = dy * w_ref[3:4, cols]
    for k in (1, 2, 3):
        acc = acc + _shift_up(dy, next8, k) * w_ref[3 - k:4 - k, cols]
    return acc


def _lin_scan(a, b, reverse):
    r = a.shape[0]
    row = lax.broadcasted_iota(jnp.int32, a.shape, 0)
    d = 1
    while d < r:
        sh = (r - d) if reverse else d
        a_s = pltpu.roll(a, sh, 0)
        b_s = pltpu.roll(b, sh, 0)
        m = (row < r - d) if reverse else (row >= d)
        b = jnp.where(m, a * b_s + b, b)
        a = jnp.where(m, a * a_s, a)
        d *= 2
    return a, b


def _rsum(x):
    return jnp.sum(x, axis=0, keepdims=True)


def _comm_specs(comm):
    if comm is None:
        return [], [], [], [], []
    n = len(comm["inputs"])
    return list(comm["inputs"]), [HBM_SPEC] * n, [HBM_SPEC] * len(comm["out_shapes"]), list(comm["out_shapes"]), comm["sems"]


def _comm_steps(comm, refs, n_in, n_out, first, mid, last):
    ni, no = len(comm["inputs"]), len(comm["out_shapes"])
    parts = (refs[n_in:n_in + ni], refs[n_out:n_out + no]) + tuple(refs[len(refs) - len(comm["sems"]):])
    for when, what in ((first, "start"), (mid, "mid"), (last, "end")):
        @pl.when(when)
        def _():
            comm[what](*parts)


def _norm_mm(h, gamma, w, *, tm, tn, name, out_dtype=F32, comm=None):
    m, k = h.shape
    if w.ndim == 3:
        assert w.shape[2] == tn
        n = w.shape[0] * tn
        w_spec = pl.BlockSpec((None, k, tn), lambda i, j: (j, 0, 0))
    else:
        n = w.shape[1]
        w_spec = pl.BlockSpec((k, tn), lambda i, j: (0, j))

    c_args, c_in_specs, c_out_specs, c_out_shapes, c_sems = _comm_specs(comm)
    ni, nj = m // tm, n // tn

    def body(*refs):
        h_ref, g_ref, w_ref = refs[:3]
        xn_ref, o_ref = refs[3 + len(c_args):5 + len(c_args)]
        i, j = pl.program_id(0), pl.program_id(1)
        if comm is not None:
            _comm_steps(comm, refs, 3, 5 + len(c_args), (i == 0) & (j == 0), (i == (3 * ni) // 4) & (j == 0),
                        (i == ni - 1) & (j == nj - 1))

        @pl.when(j == 0)
        def _():
            x = h_ref[...]
            r = lax.rsqrt(jnp.mean(x * x, axis=-1, keepdims=True) + EPS)
            xn_ref[...] = ((x * r) * g_ref[...]).astype(xn_ref.dtype)
        o_ref[...] = jnp.dot(xn_ref[...], w_ref[...], preferred_element_type=F32).astype(o_ref.dtype)

    return pl.pallas_call(
        body, name=name, grid=(ni, nj),
        in_specs=[pl.BlockSpec((tm, k), lambda i, j: (i, 0)), pl.BlockSpec((1, k), lambda i, j: (0, 0)), w_spec] + c_in_specs,
        out_specs=[pl.BlockSpec((tm, k), lambda i, j: (i, 0)), pl.BlockSpec((tm, tn), lambda i, j: (i, j))] + c_out_specs,
        out_shape=[jax.ShapeDtypeStruct((m, k), _MXU), jax.ShapeDtypeStruct((m, n), out_dtype)] + c_out_shapes,
        scratch_shapes=c_sems,
        compiler_params=_cp(("arbitrary", "arbitrary") if comm is not None else ("parallel", "arbitrary")),
    )(h, gamma, w, *c_args)


def _mm_nn(a, w, *, tm, tn, name, residual=None):
    m, k = a.shape
    n = w.shape[1]

    def body(*refs):
        if residual is None:
            a_ref, w_ref, o_ref = refs
            o_ref[...] = _dot(a_ref[...], w_ref[...])
        else:
            a_ref, w_ref, r_ref, o_ref = refs
            o_ref[...] = _dot(a_ref[...], w_ref[...]) + r_ref[...]

    in_specs = [pl.BlockSpec((tm, k), lambda i, j: (i, 0)), pl.BlockSpec((k, tn), lambda i, j: (0, j))]
    args = [a, w]
    if residual is not None:
        in_specs.append(pl.BlockSpec((tm, tn), lambda i, j: (i, j)))
        args.append(residual)
    return pl.pallas_call(
        body, name=name, grid=(m // tm, n // tn), in_specs=in_specs,
        out_specs=pl.BlockSpec((tm, tn), lambda i, j: (i, j)),
        out_shape=jax.ShapeDtypeStruct((m, n), F32),
        compiler_params=_cp(("parallel", "parallel")),
    )(*args)


def _wgrad(a, b, *, tt, ta, tn, name, out_shape, out_block, out_index, a_tab=None, o_tab=None, into=None):
    t = a.shape[0]
    a_tab = list(range(a.shape[1] // ta)) if a_tab is None else a_tab
    o_tab = a_tab if o_tab is None else o_tab
    nb = b.shape[1] // tn

    def body(at_ref, ot_ref, a_ref, b_ref, *rest):
        del at_ref, ot_ref
        o_ref = rest[-1]

        @pl.when(pl.program_id(2) == 0)
        def _():
            o_ref[...] = jnp.zeros_like(o_ref)
        o_ref[...] += _dot_tn(a_ref[...], b_ref[...])

    in_specs = [pl.BlockSpec((tt, ta), lambda r, j, i, at, ot: (i, at[r])),
                pl.BlockSpec((tt, tn), lambda r, j, i, at, ot: (i, j))]
    args = [jnp.asarray(a_tab, jnp.int32), jnp.asarray(o_tab, jnp.int32), a, b]
    aliases = {}
    if into is not None:
        in_specs.append(pl.BlockSpec(memory_space=pl.ANY))
        args.append(into)
        aliases = {4: 0}
    return pl.pallas_call(
        body, name=name,
        grid_spec=pltpu.PrefetchScalarGridSpec(
            num_scalar_prefetch=2, grid=(len(a_tab), nb, t // tt), in_specs=in_specs,
            out_specs=pl.BlockSpec(out_block, lambda r, j, i, at, ot: out_index(ot[r], j))),
        out_shape=jax.ShapeDtypeStruct(out_shape, F32), input_output_aliases=aliases,
        compiler_params=_cp(("parallel", "parallel", "arbitrary")),
    )(*args)


def _mm_nt(a, w, *, tm, name):
    m, kc = a.shape
    n = w.shape[0]

    def body(a_ref, w_ref, o_ref):
        o_ref[...] = _dot_nt(a_ref[...], w_ref[...])

    return pl.pallas_call(
        body, name=name, grid=(m // tm,),
        in_specs=[pl.BlockSpec((tm, kc), lambda i: (i, 0)), pl.BlockSpec((n, kc), lambda i: (0, 0))],
        out_specs=pl.BlockSpec((tm, n), lambda i: (i, 0)),
        out_shape=jax.ShapeDtypeStruct((m, n), F32),
        compiler_params=_cp(("parallel",)),
    )(a, w)


def _mm_nt_rmsbwd(dy, w, x, gamma, dres, *, tm, tk, name, extra=None, comm=None):
    m, kc = dy.shape
    nk = kc // tk
    ni = m // tm
    n_x = 5 if extra is None else 7
    c_args, c_in_specs, c_out_specs, c_out_shapes, c_sems = _comm_specs(comm)
    if w.ndim == 3:
        assert w.shape[0] == nk and w.shape[2] == tk
        d = w.shape[1]
        w_spec = pl.BlockSpec((None, d, tk), lambda i, k: (k, 0, 0))
    else:
        d = w.shape[0]
        w_spec = pl.BlockSpec((d, tk), lambda i, k: (0, k))

    def body(*refs):
        dy_ref, w_ref, x_ref, g_ref, r_ref = refs[:5]
        if extra is not None:
            dy2_ref, w2_ref = refs[5:7]
        n_out = n_x + len(c_args)
        dx_ref, dg_ref = refs[n_out:n_out + 2]
        acc_ref = refs[n_out + 2 + len(c_out_shapes)]
        i, kk = pl.program_id(0), pl.program_id(1)
        if comm is not None:
            _comm_steps(comm, refs, n_x, n_out + 2, (i == 0) & (kk == 0), (i == (3 * ni) // 4) & (kk == 0),
                        (i == ni - 1) & (kk == nk - 1))

        @pl.when(kk == 0)
        def _():
            acc_ref[...] = jnp.zeros_like(acc_ref)

        @pl.when((i == 0) & (kk == 0))
        def _():
            dg_ref[...] = jnp.zeros_like(dg_ref)

        acc_ref[...] += _dot_nt(dy_ref[...], w_ref[...])

        @pl.when(kk == nk - 1)
        def _():
            dxn = acc_ref[...]
            if extra is not None:
                dxn = dxn + _dot_nt(dy2_ref[...], w2_ref[...])
            xv = x_ref[...]
            r = lax.rsqrt(jnp.mean(xv * xv, axis=-1, keepdims=True) + EPS)
            xh = xv * r
            dg_ref[0:1, :] += _rsum(dxn * xh)
            dxh = dxn * g_ref[...]
            dx_ref[...] = r_ref[...] + r * (dxh - xh * jnp.mean(dxh * xh, axis=-1, keepdims=True))

    in_specs = [pl.BlockSpec((tm, tk), lambda i, k: (i, k)), w_spec,
                pl.BlockSpec((tm, d), lambda i, k: (i, 0)), pl.BlockSpec((1, d), lambda i, k: (0, 0)),
                pl.BlockSpec((tm, d), lambda i, k: (i, 0))]
    args = [dy, w, x, gamma, dres]
    if extra is not None:
        k2 = extra[0].shape[1]
        in_specs += [pl.BlockSpec((tm, k2), lambda i, k: (i, 0)), pl.BlockSpec((d, k2), lambda i, k: (0, 0))]
        args += list(extra)
    return pl.pallas_call(
        body, name=name, grid=(ni, nk), in_specs=in_specs + c_in_specs,
        out_specs=[pl.BlockSpec((tm, d), lambda i, k: (i, 0)), pl.BlockSpec((8, d), lambda i, k: (0, 0))] + c_out_specs,
        out_shape=[jax.ShapeDtypeStruct((m, d), F32), jax.ShapeDtypeStruct((8, d), F32)] + c_out_shapes,
        scratch_shapes=[pltpu.VMEM((tm, d), F32)] + c_sems,
        compiler_params=_cp(("arbitrary", "arbitrary")),
    )(*args, *c_args)


def _rsum8(x):
    acc = x[0:8]
    for g in range(1, x.shape[0] // 8):
        acc = acc + x[8 * g:8 * (g + 1)]
    return acc


def _lru_gates(x, prev8, cw_ref, cb_ref, wa_ref, wx_ref, ba_ref, bx_ref, lam_ref):
    u = _conv4(x, prev8, cw_ref, cb_ref)
    ra =_sigmoid(_dot(u, wa_ref[0]) + ba_ref[...])
    ia = _sigmoid(_dot(u, wx_ref[0]) + bx_ref[...])
    sp = _softplus(-lam_ref[...])
    log_a = -LRU_C * ra * sp
    a = jnp.exp(log_a)
    m2 = _one_minus_exp(2.0 * log_a)
    mult = jnp.sqrt(m2)
    return u, ra, ia, sp, a, m2, mult


def _lru_fwd(proj, lw, *, r, name):
    t = proj.shape[0]
    nt = t // r

    def body(xg_ref, xp_ref, cw_ref, cb_ref, wa_ref, wx_ref, ba_ref, bx_ref, lam_ref, hl_ref, ya_ref, carry_ref):
        i = pl.program_id(1)

        @pl.when(i == 0)
        def _():
            carry_ref[...] = jnp.zeros_like(carry_ref)

        x = xg_ref[:, 0:256]
        lg = xg_ref[:, 256:512]
        prev8 = jnp.where(i == 0, 0.0, xp_ref[:, 0:256])
        u, ra, ia, sp, a, m2, mult = _lru_gates(x, prev8, cw_ref, cb_ref, wa_ref, wx_ref, ba_ref, bx_ref, lam_ref)
        ac, hc = _lin_scan(a, mult * ia * u, False)
        h = hc + ac * carry_ref[0:1, :]
        hl_ref[...] = h
        carry_ref[0:1, :] = hl_ref[r - 1:r, :]
        g, _ = _gelu_and_grad(lg)
        ya_ref[...] = (g * h).astype(ya_ref.dtype)

    small = lambda rows: pl.BlockSpec((rows, 256), lambda j, i: (0, j))
    return pl.pallas_call(
        body, name=name, grid=(4, nt),
        in_specs=[pl.BlockSpec((r, LBLK), lambda j, i: (i, _lblk_col(j))),
                  pl.BlockSpec((8, LBLK), lambda j, i: (jnp.maximum(i * (r // 8) - 1, 0), _lblk_col(j))),
                  small(4), small(1),
                  pl.BlockSpec((1, 256, 256), lambda j, i: (j, 0, 0)), pl.BlockSpec((1, 256, 256), lambda j, i: (j, 0, 0)),
                  small(1), small(1), small(1)],
        out_specs=[pl.BlockSpec((r, 256), lambda j, i: (i, j)), pl.BlockSpec((r, 256), lambda j, i: (i, j))],
        out_shape=[jax.ShapeDtypeStruct((t, D), F32), jax.ShapeDtypeStruct((t, D), _MXU)],
        scratch_shapes=[pltpu.VMEM((8, 256), F32)],
        compiler_params=_cp(("parallel", "arbitrary")),
    )(proj, proj, lw["cw"], lw["cb"], lw["wa"], lw["wx"], lw["ba"], lw["bx"], lw["lam"])


def _lru_bwd(proj, hl, dya, dproj, lw, *, r, name):
    t = proj.shape[0]
    nt = t // r

    def body(xg_ref, xp_ref, hl_ref, hp_ref, dya_ref, cw_ref, cb_ref, wa_ref, wx_ref, ba_ref, bx_ref, lam_ref, dproj_in,
             dproj_ref, sm_ref, dwa_ref, dwx_ref, carry_ref, du8_ref, row_scr):
        del dproj_in
        i = pl.program_id(1)

        @pl.when(i == 0)
        def _():
            carry_ref[...] = jnp.zeros_like(carry_ref)
            du8_ref[...] = jnp.zeros_like(du8_ref)
            sm_ref[...] = jnp.zeros_like(sm_ref)
            dwa_ref[...] = jnp.zeros_like(dwa_ref)
            dwx_ref[...] = jnp.zeros_like(dwx_ref)

        tile0 = i == nt - 1
        xp = xg_ref[:, 0:256]
        lg = xg_ref[:, 256:512]
        prev8 = jnp.where(tile0, 0.0, xp_ref[:, 0:256])
        u, ra, ia, sp, a, m2, mult = _lru_gates(xp, prev8, cw_ref, cb_ref, wa_ref, wx_ref, ba_ref, bx_ref, lam_ref)
        h = hl_ref[...]
        hprev = _shift_down(h, jnp.where(tile0, 0.0, hp_ref[...]), 1)
        dya_v = dya_ref[...]
        g, dg = _gelu_and_grad(lg)
        ac, lc = _lin_scan(_shift_up(a, carry_ref[...], 1), dya_v * g, True)
        lam_v = lc + ac * carry_ref[1:2, :]
        row_scr[0:8, :] = lam_v[0:8]
        row_scr[8:16, :] = a[0:8]
        carry_ref[1:2, :] = row_scr[0:1, :]
        carry_ref[0:1, :] = row_scr[8:9, :]
        da = lam_v * hprev
        dmult = lam_v * ia * u
        dia = lam_v * mult * u
        dlog = da * a - dmult * (1.0 - m2) / mult
        dra = -LRU_C * sp * dlog
        dpa = dra * ra * (1.0 - ra)
        dpx = dia * ia * (1.0 - ia)
        du = lam_v * mult * ia + _dot_nt(dpa, wa_ref[0]) + _dot_nt(dpx, wx_ref[0])
        dwa_ref[0] += _dot_tn(u, dpa)
        dwx_ref[0] += _dot_tn(u, dpx)
        dlx = du * cw_ref[3:4, :]
        sm_ref[24:32, :] += _rsum8(du * xp)
        for k in (1, 2, 3):
            du_k = _shift_up(du, du8_ref[...], k)
            dlx = dlx + du_k * cw_ref[3 - k:4 - k, :]
            sm_ref[8 * (3 - k):8 * (4 - k), :] += _rsum8(du_k * xp)
        du8_ref[...] = du[0:8]
        dproj_ref[:, 0:256] = dlx.astype(dproj_ref.dtype)
        dproj_ref[:, 256:512] = (dya_v * h * dg).astype(dproj_ref.dtype)
        sm_ref[32:40, :] += _rsum8(du)
        sm_ref[40:48, :] += _rsum8(dpa)
        sm_ref[48:56, :] += _rsum8(dpx)
        sm_ref[56:64, :] += _rsum8(-LRU_C * ra * dlog) * (-_sigmoid(-lam_ref[...]))

    rev = lambda i: nt - 1 - i
    small = lambda rows: pl.BlockSpec((rows, 256), lambda j, i: (0, j))
    wblk = pl.BlockSpec((1, 256, 256), lambda j, i: (j, 0, 0))
    n_in = 13
    return pl.pallas_call(
        body, name=name, grid=(4, nt),
        in_specs=[pl.BlockSpec((r, LBLK), lambda j, i: (rev(i), _lblk_col(j))),
                  pl.BlockSpec((8, LBLK), lambda j, i: (jnp.maximum(rev(i) * (r // 8) - 1, 0), _lblk_col(j))),
                  pl.BlockSpec((r, 256), lambda j, i: (rev(i), j)),
                  pl.BlockSpec((8, 256), lambda j, i: (jnp.maximum(rev(i) * (r // 8) - 1, 0), j)),
                  pl.BlockSpec((r, 256), lambda j, i: (rev(i), j)),
                  small(4), small(1), wblk, wblk, small(1), small(1), small(1),
                  pl.BlockSpec(memory_space=pl.ANY)],
        out_specs=[pl.BlockSpec((r, LBLK), lambda j, i: (rev(i), _lblk_col(j))),
                   pl.BlockSpec((64, 256), lambda j, i: (0, j)), wblk, wblk],
        out_shape=[jax.ShapeDtypeStruct(dproj.shape, dproj.dtype), jax.ShapeDtypeStruct((64, D), F32),
                   jax.ShapeDtypeStruct((4, 256, 256), F32), jax.ShapeDtypeStruct((4, 256, 256), F32)],
        scratch_shapes=[pltpu.VMEM((8, 256), F32), pltpu.VMEM((8, 256), F32), pltpu.VMEM((16, 256), F32)],
        input_output_aliases={n_in - 1: 0},
        compiler_params=_cp(("parallel", "arbitrary")),
    )(proj, proj, hl, hl, dya, lw["cw"], lw["cb"], lw["wa"], lw["wx"], lw["ba"], lw["bx"], lw["lam"], dproj)


def _head_cols(x):
    lane = lax.broadcasted_iota(jnp.int32, x.shape, 1)
    return [jnp.sum(jnp.where(lane == h, x, 0.0), axis=1, keepdims=True) for h in range(N_HEADS)]


def _compact_heads(blocks):
    lane = lax.broadcasted_iota(jnp.int32, blocks[0].shape, 1)
    lo = lane < HEAD_P
    out = jnp.zeros_like(blocks[0])
    for j, blk in enumerate(blocks):
        s_lo = jnp.sum(jnp.where(lo, blk, 0.0), axis=1, keepdims=True)
        s_hi = jnp.sum(jnp.where(lo, 0.0, blk), axis=1, keepdims=True)
        out = jnp.where(lane == 2 * j, s_lo, out)
        out = jnp.where(lane == 2 * j + 1, s_hi, out)
    return out


def _ssd_prelude(dtraw_ref, dtb_ref, alog_ref, dt_scr, a_scr):
    lane = lax.broadcasted_iota(jnp.int32, dt_scr.shape, 1)
    dt = jnp.where(lane < N_HEADS, _softplus(dtraw_ref[...] + dtb_ref[0:1, :]), 0.0)
    dt_scr[...] = dt
    a_scr[...] = dt * (-jnp.exp(alog_ref[0:1, :]))


def _ssd_chunk_scalars(dt_scr, a_scr, r_scr, r0):
    a_c = a_scr[pl.ds(r0, CHUNK), :]
    dt_c = dt_scr[pl.ds(r0, CHUNK), :]
    i0 = lax.broadcasted_iota(jnp.int32, (CHUNK, CHUNK), 0)
    i1 = lax.broadcasted_iota(jnp.int32, (CHUNK, CHUNK), 1)
    tri = jnp.where(i0 >= i1, 1.0, 0.0).astype(F32)
    cs = jnp.dot(tri, a_c, precision=_HI, preferred_element_type=F32)
    lane = lax.broadcasted_iota(jnp.int32, (CHUNK, 128), 1)
    srow = lax.broadcasted_iota(jnp.int32, (CHUNK, 128), 0)
    t_lo = jnp.where((lane < HEAD_P) & (srow <= lane), 1.0, 0.0).astype(F32)
    t_hi = jnp.where((lane >= HEAD_P) & (srow <= lane - HEAD_P), 1.0, 0.0).astype(F32)
    even = (lane % 2) == 0
    tn = (((0,), (0,)), ((), ()))
    r_scr[...] = (lax.dot_general(jnp.where(even, a_c, 0.0), t_lo, tn, precision=_HI, preferred_element_type=F32)
                  + lax.dot_general(jnp.where(even, 0.0, a_c), t_hi, tn, precision=_HI, preferred_element_type=F32))
    return cs, dt_c, _head_cols(cs), _head_cols(dt_c)


def _block_diag2(v):
    lo = lax.broadcasted_iota(jnp.int32, v.shape, 1) < HEAD_P
    return jnp.concatenate([jnp.where(lo, v, 0.0), jnp.where(lo, 0.0, v)], axis=0).astype(_MXU)


def _ssd_pair(xc_scr, r_scr, cs_cols, dt_cols, s2, r0, j, s2t=None):
    lane = lax.broadcasted_iota(jnp.int32, (CHUNK, 128), 1)
    srow = lax.broadcasted_iota(jnp.int32, (CHUNK, 128), 0)
    lo = lane < HEAD_P
    csc = jnp.where(lo, cs_cols[2 * j], cs_cols[2 * j + 1])
    dtc = jnp.where(lo, dt_cols[2 * j], dt_cols[2 * j + 1])
    csr = r_scr[2 * j:2 * j + 1, :] + r_scr[2 * j + 1:2 * j + 2, :]
    dm = jnp.where((lane & (HEAD_P - 1)) <= srow, jnp.exp(jnp.minimum(csc - csr, 0.0)), 0.0)
    xs = xc_scr[pl.ds(r0, CHUNK), j * 128:(j + 1) * 128]
    xd = xs * dtc
    csl = jnp.sum(jnp.where(srow == CHUNK - 1, csc, 0.0), axis=0, keepdims=True)
    out = dict(csc=csc, dtc=dtc, dm=dm, m2=s2 * dm, xs=xs, xd=xd, rhs=_block_diag2(xd), e=jnp.exp(csc),
               w=jnp.exp(csl - csc), dec=jnp.exp(csl))
    if s2t is not None:
        out["mt2"] = s2t * jnp.where((lane & (HEAD_P - 1)) >= srow, jnp.exp(jnp.minimum(csr - csc, 0.0)), 0.0)
    return out


def _cat(parts):
    return jnp.concatenate(parts, axis=1)


def _ssd_fwd(proj, dtraw, sw, *, rb, name, comm=None):
    t = proj.shape[0]
    ns, cb = t // rb, rb // CHUNK
    c_args, c_in_specs, c_out_specs, c_out_shapes, c_sems = _comm_specs(comm)

    def body(*refs):
        zx_ref, zp_ref, dtraw_ref, cw_ref, cbias_ref, dtb_ref, alog_ref, dsk_ref, ng_ref = refs[:9]
        yssd_ref, yb_ref, st_ref = refs[9 + len(c_args):12 + len(c_args)]
        n_scr = 12 + len(c_args) + len(c_out_shapes)
        h_scr, xc_scr, dt_scr, a_scr, r_scr = refs[n_scr:n_scr + 5]
        i = pl.program_id(0)
        if comm is not None:
            _comm_steps(comm, refs, 9, 12 + len(c_args), i == 0, i == (3 * ns) // 4, i == ns - 1)

        @pl.when(i == 0)
        def _():
            h_scr[...] = jnp.zeros_like(h_scr)

        for j in range(XBC // 128):
            cs_, zc = slice(128 * j, 128 * (j + 1)), slice(2048 + 128 * j, 2048 + 128 * (j + 1))
            pre = _conv4(zx_ref[:, zc], jnp.where(i == 0, 0.0, zp_ref[:, zc]), cw_ref, cbias_ref, cs_)
            xc_scr[:, cs_] = pre * _sigmoid(pre)
        _ssd_prelude(dtraw_ref, dtb_ref, alog_ref, dt_scr, a_scr)

        def chunk(c, carry):
            r0 = pl.multiple_of(c * CHUNK, CHUNK)
            _, _, cs_cols, dt_cols = _ssd_chunk_scalars(dt_scr, a_scr, r_scr, r0)
            st_ref[c] = h_scr[...]
            for g in range(N_GROUPS):
                bg = xc_scr[pl.ds(r0, CHUNK), 2048 + 128 * g:2048 + 128 * (g + 1)]
                cg = xc_scr[pl.ds(r0, CHUNK), 2560 + 128 * g:2560 + 128 * (g + 1)]
                s2 = _dot_nt(cg, jnp.concatenate([bg, bg], axis=0))
                hp = h_scr[:, 512 * g:512 * (g + 1)]
                yoff = _dot(cg, hp)
                xdw, dec = [], []
                for jj in range(4):
                    j = 4 * g + jj
                    p = _ssd_pair(xc_scr, r_scr, cs_cols, dt_cols, s2, r0, j)
                    y = _dot(p["m2"], p["rhs"]) + yoff[:, 128 * jj:128 * (jj + 1)] * p["e"]
                    yssd_ref[pl.ds(r0, CHUNK), 128 * j:128 * (j + 1)] = y + dsk_ref[0:1, 128 * j:128 * (j + 1)] * p["xs"]
                    xdw.append(p["xd"] * p["w"])
                    dec.append(p["dec"])
                h_scr[:, 512 * g:512 * (g + 1)] = hp * _cat(dec) + _dot_tn(bg, _cat(xdw))
            return carry

        lax.fori_loop(0, cb, chunk, 0)
        for g in range(N_GROUPS):
            sl = slice(512 * g, 512 * (g + 1))
            for q in range(rb // NORM_ROWS):
                rw = slice(NORM_ROWS * q, NORM_ROWS * (q + 1))
                yz = yssd_ref[rw, sl] * _silu(zx_ref[rw, sl])
                rg = lax.rsqrt(jnp.mean(yz * yz, axis=-1, keepdims=True) + EPS)
                yb_ref[rw, sl] = (yz * rg * ng_ref[0:1, sl]).astype(yb_ref.dtype)

    full = lambda rows, cols: pl.BlockSpec((rows, cols), lambda i: (0, 0))
    return pl.pallas_call(
        body, name=name, grid=(ns,),
        in_specs=[pl.BlockSpec((rb, ZX_W), lambda i: (i, 0)),
                  pl.BlockSpec((8, ZX_W), lambda i: (jnp.maximum(i * (rb // 8) - 1, 0), 0)),
                  pl.BlockSpec((rb, DT_PAD), lambda i: (i, 0)),
                  full(4, XBC), full(1, XBC), full(1, DT_PAD), full(1, DT_PAD), full(1, SSD_INNER), full(1, SSD_INNER)]
        + c_in_specs,
        out_specs=[pl.BlockSpec((rb, SSD_INNER), lambda i: (i, 0)), pl.BlockSpec((rb, SSD_INNER), lambda i: (i, 0)),
                   pl.BlockSpec((cb, N_STATE, SSD_INNER), lambda i: (i, 0, 0))] + c_out_specs,
        out_shape=[jax.ShapeDtypeStruct((t, SSD_INNER), F32), jax.ShapeDtypeStruct((t, SSD_INNER), _MXU),
                   jax.ShapeDtypeStruct((t // CHUNK, N_STATE, SSD_INNER), F32)] + c_out_shapes,
        scratch_shapes=[pltpu.VMEM((N_STATE, SSD_INNER), F32), pltpu.VMEM((rb, XBC), F32), pltpu.VMEM((rb, DT_PAD), F32),
                        pltpu.VMEM((rb, DT_PAD), F32), pltpu.VMEM((128, 128), F32)] + c_sems,
        compiler_params=_cp(("arbitrary",)),
    )(proj, proj, dtraw, sw["cw"], sw["cb"], sw["dtb"], sw["alog"], sw["dsk"], sw["ng"], *c_args)


def _ssd_bwd(proj, dtraw, yssd, states, dyb, dproj, sw, *, rb, name, comm=None):
    t = proj.shape[0]
    ns, cb = t // rb, rb // CHUNK
    c_args, c_in_specs, c_out_specs, c_out_shapes, c_sems = _comm_specs(comm)
    n_in = 13

    def body(*refs):
        zx_ref, zp_ref, dtraw_ref, yssd_ref, st_ref, dyb_ref, cw_ref, cbias_ref, dtb_ref, alog_ref, dsk_ref, ng_ref = refs[:12]
        n_out = n_in + len(c_args)
        dzx_ref, ddt_ref, gconv_ref, gch_ref, ghd_ref = refs[n_out:n_out + 5]
        n_scr = n_out + 5 + len(c_out_shapes)
        dht_scr, xc_scr, dsl_scr, dy_scr, dxc_scr, dt_scr, a_scr, r_scr, dp8_scr = refs[n_scr:n_scr + 9]
        i = pl.program_id(0)
        if comm is not None:
            _comm_steps(comm, refs, n_in, n_out + 5, i == 0, i == (3 * ns) // 4, i == ns - 1)

        @pl.when(i == 0)
        def _():
            dht_scr[...] = jnp.zeros_like(dht_scr)
            dp8_scr[...] = jnp.zeros_like(dp8_scr)
            gconv_ref[...] = jnp.zeros_like(gconv_ref)
            gch_ref[...] = jnp.zeros_like(gch_ref)
            ghd_ref[...] = jnp.zeros_like(ghd_ref)

        tile0 = i == ns - 1
        for j in range(XBC // 128):
            cs_, zc = slice(128 * j, 128 * (j + 1)), slice(2048 + 128 * j, 2048 + 128 * (j + 1))
            pre = _conv4(zx_ref[:, zc], jnp.where(tile0, 0.0, zp_ref[:, zc]), cw_ref, cbias_ref, cs_)
            sg = _sigmoid(pre)
            xc_scr[:, cs_] = pre * sg
            dsl_scr[:, cs_] = sg * (1.0 + pre * (1.0 - sg))
        _ssd_prelude(dtraw_ref, dtb_ref, alog_ref, dt_scr, a_scr)

        for g in range(N_GROUPS):
            sl = slice(512 * g, 512 * (g + 1))
            for q in range(rb // NORM_ROWS):
                rw = slice(NORM_ROWS * q, NORM_ROWS * (q + 1))
                zv = zx_ref[rw, sl]
                ys = yssd_ref[rw, sl]
                sg = _sigmoid(zv)
                sz = zv * sg
                yz = ys * sz
                rg = lax.rsqrt(jnp.mean(yz * yz, axis=-1, keepdims=True) + EPS)
                yn = yz * rg
                dyb_v = dyb_ref[rw, sl]
                gch_ref[0:8, sl] += _rsum8(dyb_v * yn)
                dyn = dyb_v * ng_ref[0:1, sl]
                dyz = rg * (dyn - yn * jnp.mean(dyn * yn, axis=-1, keepdims=True))
                dy_scr[rw, sl] = dyz * sz
                dzx_ref[rw, sl] = (dyz * ys * (sg * (1.0 + zv * (1.0 - sg)))).astype(dzx_ref.dtype)

        a_row = -jnp.exp(alog_ref[0:1, :])

        def chunk(cc, carry):
            c = cb - 1 - cc
            r0 = pl.multiple_of(c * CHUNK, CHUNK)
            rows = pl.ds(r0, CHUNK)
            _, dt_c, cs_cols, dt_cols = _ssd_chunk_scalars(dt_scr, a_scr, r_scr, r0)
            lane = lax.broadcasted_iota(jnp.int32, (CHUNK, 128), 1)
            srow = lax.broadcasted_iota(jnp.int32, (CHUNK, 128), 0)
            lo = lane < HEAD_P
            last = srow == CHUNK - 1
            p1_blocks, p3_blocks = [], []
            for g in range(N_GROUPS):
                gs = slice(512 * g, 512 * (g + 1))
                bg = xc_scr[rows, 2048 + 128 * g:2048 + 128 * (g + 1)]
                cg = xc_scr[rows, 2560 + 128 * g:2560 + 128 * (g + 1)]
                b2 = jnp.concatenate([bg, bg], axis=0)
                s2 = _dot_nt(cg, b2)
                s2t = _dot_nt(bg, jnp.concatenate([cg, cg], axis=0))
                hp = st_ref[c, :, gs]
                dht = dht_scr[:, gs]
                yoff = _dot(cg, hp)
                ps = [_ssd_pair(xc_scr, r_scr, cs_cols, dt_cols, s2, r0, 4 * g + jj, s2t) for jj in range(4)]
                dys = [dy_scr[rows, 128 * (4 * g + jj):128 * (4 * g + jj + 1)] for jj in range(4)]
                dye = _cat([dys[jj] * ps[jj]["e"] for jj in range(4)])
                w_g = _cat([p["w"] for p in ps])
                dcg = _dot_nt(dye, hp)
                dht_scr[:, gs] = _dot_tn(cg, dye) + _cat([p["dec"] for p in ps]) * dht
                dxd_state = w_g * _dot(bg, dht)
                dbg = _dot_nt(_cat([p["xd"] for p in ps]) * w_g, dht)
                tsum = _rsum(dht * hp)
                ds2 = jnp.zeros((CHUNK, 128), F32)
                for jj in range(4):
                    j = 4 * g + jj
                    ls = slice(128 * j, 128 * (j + 1))
                    p, dy2 = ps[jj], dys[jj]
                    dy_bd = _block_diag2(dy2)
                    dm2 = _dot_nt(dy2, p["rhs"])
                    ds2 = ds2 + dm2 * p["dm"]
                    gdiff = dm2 * p["m2"] - _dot_nt(p["xd"], dy_bd) * p["mt2"]
                    dxs = dxd_state[:, 128 * jj:128 * (jj + 1)]
                    dxd = _dot(p["mt2"], dy_bd) + dxs
                    end_row = _rsum(p["xd"] * dxs) + p["dec"] * tsum[:, 128 * jj:128 * (jj + 1)]
                    p1_blocks.append(gdiff + dy2 * yoff[:, 128 * jj:128 * (jj + 1)] * p["e"] - p["xd"] * dxs
                                     + jnp.where(last, end_row, 0.0))
                    p3_blocks.append(dxd * p["xs"])
                    dxc_scr[rows, ls] = dxd * p["dtc"] + dy2 * dsk_ref[0:1, ls]
                    gch_ref[8:16, ls] += _rsum8(dy2 * p["xs"])
                dcg = dcg + _dot(ds2, b2)
                rb2 = _dot_tn(ds2, cg)
                dxc_scr[rows, 2048 + 128 * g:2048 + 128 * (g + 1)] = dbg + rb2[0:CHUNK] + rb2[CHUNK:2 * CHUNK]
                dxc_scr[rows, 2560 + 128 * g:2560 + 128 * (g + 1)] = dcg
            dcs = _compact_heads(p1_blocks)
            i0 = lax.broadcasted_iota(jnp.int32, (CHUNK, CHUNK), 0)
            i1 = lax.broadcasted_iota(jnp.int32, (CHUNK, CHUNK), 1)
            triu = jnp.where(i1 >= i0, 1.0, 0.0).astype(F32)
            da = jnp.dot(triu, dcs, precision=_HI, preferred_element_type=F32)
            ddt = _compact_heads(p3_blocks) + da * a_row
            ddtraw = jnp.where(lane < N_HEADS, ddt * _sigmoid(dtraw_ref[rows, :] + dtb_ref[0:1, :]), 0.0)
            ddt_ref[rows, :] = ddtraw.astype(ddt_ref.dtype)
            ghd_ref[0:1, :] += _rsum(ddtraw)
            ghd_ref[1:2, :] += _rsum(da * dt_c) * a_row
            return carry

        lax.fori_loop(0, cb, chunk, 0)
        for j in range(XBC // 128):
            cs_, zc = slice(128 * j, 128 * (j + 1)), slice(2048 + 128 * j, 2048 + 128 * (j + 1))
            dpre = dxc_scr[:, cs_] * dsl_scr[:, cs_]
            xraw = zx_ref[:, zc]
            dx = dpre * cw_ref[3:4, cs_]
            gconv_ref[24:32, cs_] += _rsum8(dpre * xraw)
            for k in (1, 2, 3):
                dpre_k = _shift_up(dpre, dp8_scr[:, cs_], k)
                dx = dx + dpre_k * cw_ref[3 - k:4 - k, cs_]
                gconv_ref[8 * (3 - k):8 * (4 - k), cs_] += _rsum8(dpre_k * xraw)
            dzx_ref[:, zc] = dx.astype(dzx_ref.dtype)
            dp8_scr[:, cs_] = dpre[0:8]
            gconv_ref[32:40, cs_] += _rsum8(dpre)

    rev = lambda i: ns - 1 - i
    full = lambda rows, cols: pl.BlockSpec((rows, cols), lambda i: (0, 0))
    return pl.pallas_call(
        body, name=name, grid=(ns,),
        in_specs=[pl.BlockSpec((rb, ZX_W), lambda i: (rev(i), 0)),
                  pl.BlockSpec((8, ZX_W), lambda i: (jnp.maximum(rev(i) * (rb // 8) - 1, 0), 0)),
                  pl.BlockSpec((rb, DT_PAD), lambda i: (rev(i), 0)),
                  pl.BlockSpec((rb, SSD_INNER), lambda i: (rev(i), 0)),
                  pl.BlockSpec((cb, N_STATE, SSD_INNER), lambda i: (rev(i), 0, 0)),
                  pl.BlockSpec((rb, SSD_INNER), lambda i: (rev(i), 0)),
                  full(4, XBC), full(1, XBC), full(1, DT_PAD), full(1, DT_PAD), full(1, SSD_INNER), full(1, SSD_INNER),
                  pl.BlockSpec(memory_space=pl.ANY)] + c_in_specs,
        out_specs=[pl.BlockSpec((rb, ZX_W), lambda i: (rev(i), 0)), pl.BlockSpec((rb, DT_PAD), lambda i: (rev(i), 0)),
                   full(40, XBC), full(16, SSD_INNER), full(8, DT_PAD)] + c_out_specs,
        out_shape=[jax.ShapeDtypeStruct(dproj.shape, dproj.dtype), jax.ShapeDtypeStruct((t, DT_PAD), _MXU),
                   jax.ShapeDtypeStruct((40, XBC), F32), jax.ShapeDtypeStruct((16, SSD_INNER), F32),
                   jax.ShapeDtypeStruct((8, DT_PAD), F32)] + c_out_shapes,
        scratch_shapes=[pltpu.VMEM((N_STATE, SSD_INNER), F32), pltpu.VMEM((rb, XBC), F32), pltpu.VMEM((rb, XBC), F32),
                        pltpu.VMEM((rb, SSD_INNER), F32), pltpu.VMEM((rb, XBC), F32), pltpu.VMEM((rb, DT_PAD), F32),
                        pltpu.VMEM((rb, DT_PAD), F32), pltpu.VMEM((128, 128), F32), pltpu.VMEM((8, XBC), F32)] + c_sems,
        input_output_aliases={n_in - 1: 0},
        compiler_params=_cp(("arbitrary",)),
    )(proj, proj, dtraw, yssd, states, dyb, sw["cw"], sw["cb"], sw["dtb"], sw["alog"], sw["dsk"], sw["ng"], dproj, *c_args)


def _branch_merge(ya, yb, proj, wba, wbb, bgate, *, tm, tn, name):
    t = ya.shape[0]
    nj = D // tn

    def body(ya_ref, yb_ref, ga_ref, gb_ref, wba_ref, wbb_ref, ba_ref, bb_ref, ta_ref, tb_ref, mg_ref):
        ta = _dot(ya_ref[...], wba_ref[...])
        tb = _dot(yb_ref[...], wbb_ref[...])
        ta_ref[...] = ta.astype(ta_ref.dtype)
        tb_ref[...] = tb.astype(tb_ref.dtype)
        ga = _sigmoid(ga_ref[...] + ba_ref[...])
        gb = _sigmoid(gb_ref[...] + bb_ref[...])
        mg_ref[...] = (ga * ta + gb * tb).astype(mg_ref.dtype)

    tile = pl.BlockSpec((tm, tn), lambda i, j: (i, j))
    return pl.pallas_call(
        body, name=name, grid=(t // tm, nj),
        in_specs=[pl.BlockSpec((tm, D), lambda i, j: (i, 0)), pl.BlockSpec((tm, SSD_INNER), lambda i, j: (i, 0)),
                  pl.BlockSpec((tm, tn), lambda i, j: (i, G0 // tn + j)),
                  pl.BlockSpec((tm, tn), lambda i, j: (i, (G0 + D) // tn + j)),
                  pl.BlockSpec((D, tn), lambda i, j: (0, j)), pl.BlockSpec((SSD_INNER, tn), lambda i, j: (0, j)),
                  pl.BlockSpec((1, tn), lambda i, j: (0, j)), pl.BlockSpec((1, tn), lambda i, j: (0, nj + j))],
        out_specs=[tile, tile, tile],
        out_shape=[jax.ShapeDtypeStruct((t, D), _MXU)] * 3,
        compiler_params=_cp(("parallel", "parallel")),
    )(ya, yb, proj, proj, wba, wbb, bgate, bgate)


def _swiglu_mm(gu, wfo, residual, *, tm, tn, name):
    t = gu.shape[0]

    def body(gu_ref, w_ref, r_ref, act_ref, o_ref):
        @pl.when(pl.program_id(1) == 0)
        def _():
            gate = gu_ref[:, 0:D_FF].astype(F32)
            act_ref[...] = (_silu(gate) * gu_ref[:, D_FF:2 * D_FF].astype(F32)).astype(act_ref.dtype)
        o_ref[...] = jnp.dot(act_ref[...], w_ref[...], preferred_element_type=F32) + r_ref[...]

    return pl.pallas_call(
        body, name=name, grid=(t // tm, D // tn),
        in_specs=[pl.BlockSpec((tm, 2 * D_FF), lambda i, j: (i, 0)), pl.BlockSpec((D_FF, tn), lambda i, j: (0, j)),
                  pl.BlockSpec((tm, tn), lambda i, j: (i, j))],
        out_specs=[pl.BlockSpec((tm, D_FF), lambda i, j: (i, 0)), pl.BlockSpec((tm, tn), lambda i, j: (i, j))],
        out_shape=[jax.ShapeDtypeStruct((t, D_FF), _MXU), jax.ShapeDtypeStruct((t, D), F32)],
        compiler_params=_cp(("parallel", "arbitrary")),
    )(gu, wfo, residual)


def _ffn_bwd_act(dh, wfo, gu, *, tm, name, comm=None):
    t = dh.shape[0]
    ni = t // tm
    c_args, c_in_specs, c_out_specs, c_out_shapes, c_sems = _comm_specs(comm)

    def body(*refs):
        dh_ref, w_ref, gu_ref = refs[:3]
        o_ref = refs[3 + len(c_args)]
        if comm is not None:
            i = pl.program_id(0)
            _comm_steps(comm, refs, 3, 4 + len(c_args), i == 0, i == (3 * ni) // 4, i == ni - 1)
        dact = _dot_nt(dh_ref[...], w_ref[...])
        g = gu_ref[:, 0:D_FF].astype(F32)
        u = gu_ref[:, D_FF:2 * D_FF].astype(F32)
        sg = _sigmoid(g)
        o_ref[:, 0:D_FF] = (dact * u * (sg * (1.0 + g * (1.0 - sg)))).astype(o_ref.dtype)
        o_ref[:, D_FF:2 * D_FF] = (dact * (g * sg)).astype(o_ref.dtype)

    return pl.pallas_call(
        body, name=name, grid=(ni,),
        in_specs=[pl.BlockSpec((tm, D), lambda i: (i, 0)), pl.BlockSpec((D_FF, D), lambda i: (0, 0)),
                  pl.BlockSpec((tm, 2 * D_FF), lambda i: (i, 0))] + c_in_specs,
        out_specs=[pl.BlockSpec((tm, 2 * D_FF), lambda i: (i, 0))] + c_out_specs,
        out_shape=[jax.ShapeDtypeStruct((t, 2 * D_FF), _MXU)] + c_out_shapes,
        scratch_shapes=c_sems,
        compiler_params=_cp(("arbitrary",) if comm is not None else ("parallel",)),
    )(dh, wfo, gu, *c_args)


def _outproj_bwd(dh, wout, ta, tb, proj, bgate, dproj, *, tm, name):
    t = dh.shape[0]

    def body(dh_ref, w_ref, ta_ref, tb_ref, g_ref, b_ref, dta_ref, dtb_ref, dg_ref, db_ref):
        @pl.when(pl.program_id(0) == 0)
        def _():
            db_ref[...] = jnp.zeros_like(db_ref)
        dm = _dot_nt(dh_ref[...], w_ref[...])
        ga = _sigmoid(g_ref[:, 0:D] + b_ref[:, 0:D])
        gb = _sigmoid(g_ref[:, D:2 * D] + b_ref[:, D:2 * D])
        dta_ref[...] = (dm * ga).astype(dta_ref.dtype)
        dtb_ref[...] = (dm * gb).astype(dtb_ref.dtype)
        dga = dm * ta_ref[...].astype(F32) * ga * (1.0 - ga)
        dgb = dm * tb_ref[...].astype(F32) * gb * (1.0 - gb)
        dg_ref[:, 0:D] = dga.astype(dg_ref.dtype)
        dg_ref[:, D:2 * D] = dgb.astype(dg_ref.dtype)
        db_ref[0:1, 0:D] += _rsum(dga)
        db_ref[0:1, D:2 * D] += _rsum(dgb)

    row = lambda cols: pl.BlockSpec((tm, cols), lambda i: (i, 0))
    return pl.pallas_call(
        body, name=name, grid=(t // tm,),
        in_specs=[row(D), pl.BlockSpec((D, D), lambda i: (0, 0)), row(D), row(D),
                  pl.BlockSpec((tm, 2 * D), lambda i: (i, G0 // (2 * D))), pl.BlockSpec((1, 2 * D), lambda i: (0, 0))],
        out_specs=[row(D), row(D), pl.BlockSpec((tm, 2 * D), lambda i: (i, G0 // (2 * D))),
                   pl.BlockSpec((8, 2 * D), lambda i: (0, 0))],
        out_shape=[jax.ShapeDtypeStruct((t, D), _MXU), jax.ShapeDtypeStruct((t, D), _MXU),
                   jax.ShapeDtypeStruct(dproj, _MXU), jax.ShapeDtypeStruct((8, 2 * D), F32)],
        compiler_params=_cp(("arbitrary",)),
    )(dh, wout, ta, tb, proj, bgate)


def _loss_head(h, gf, target, *, tm, name):
    t = h.shape[0]

    def body(h_ref, g_ref, t_ref, loss_ref, dg_ref, dh_ref):
        @pl.when(pl.program_id(0) == 0)
        def _():
            loss_ref[...] = jnp.zeros_like(loss_ref)
            dg_ref[...] = jnp.zeros_like(dg_ref)
        x = h_ref[...]
        r = lax.rsqrt(jnp.mean(x * x, axis=-1, keepdims=True) + EPS)
        xh = x * r
        err = xh * g_ref[...] - t_ref[...]
        loss_ref[...] += 0.5 * jnp.sum(jnp.mean(err * err, axis=-1, keepdims=True), axis=0, keepdims=True)
        dy = err * (1.0 / D)
        dg_ref[0:1, :] += _rsum(dy * xh)
        dxh = dy * g_ref[...]
        dh_ref[...] = r * (dxh - xh * jnp.mean(dxh * xh, axis=-1, keepdims=True))

    row = pl.BlockSpec((tm, D), lambda i: (i, 0))
    return pl.pallas_call(
        body, name=name, grid=(t // tm,),
        in_specs=[row, pl.BlockSpec((1, D), lambda i: (0, 0)), row],
        out_specs=[pl.BlockSpec((8, 128), lambda i: (0, 0)), pl.BlockSpec((8, D), lambda i: (0, 0)), row],
        out_shape=[jax.ShapeDtypeStruct((8, 128), F32), jax.ShapeDtypeStruct((8, D), F32), jax.ShapeDtypeStruct((t, D), F32)],
        compiler_params=_cp(("arbitrary",)),
    )(h, gf, target)


def _row_tile(rows, cols, limit_bytes=1 << 20):
    best = None
    for tr in range(8, rows + 1, 8):
        if rows % tr == 0 and tr * cols * 4 <= limit_bytes:
            best = tr
    return best if best is not None else rows


def _adamw(w, g, m, v, *, name):
    rows, cols = w.shape
    tr = _row_tile(rows, cols)

    def body(w_ref, g_ref, m_ref, v_ref, d_ref, nm_ref, nv_ref):
        gv = g_ref[...]
        nm = ADAM_B1 * m_ref[...] + (1.0 - ADAM_B1) * gv
        nv = ADAM_B2 * v_ref[...] + (1.0 - ADAM_B2) * (gv * gv)
        m_hat = nm / (1.0 - ADAM_B1 ** ADAM_STEP)
        v_hat = nv / (1.0 - ADAM_B2 ** ADAM_STEP)
        d_ref[...] = -ADAM_LR * (m_hat / (jnp.sqrt(v_hat) + ADAM_EPS) + ADAM_WD * w_ref[...])
        nm_ref[...] = nm
        nv_ref[...] = nv

    blk = pl.BlockSpec((tr, cols), lambda i: (i, 0))
    shp = jax.ShapeDtypeStruct((rows, cols), F32)
    return pl.pallas_call(
        body, name=name, grid=(rows // tr,), in_specs=[blk] * 4, out_specs=[blk] * 3, out_shape=[shp] * 3,
        compiler_params=_cp(("parallel",)),
    )(w, g, m, v)


def _bd256(w):
    w4 = w.reshape(4, 4, 64, 64)
    eye = jnp.eye(4, dtype=w.dtype)
    return (w4[:, :, :, None, :] * eye[None, :, None, :, None]).reshape(4, 256, 256)


def _bd256_diag(g):
    g5 = g.reshape(4, 4, 64, 4, 64)
    return jnp.stack([g5[:, a, :, a, :] for a in range(4)], axis=1).reshape(16, 64, 64)


FFN_SHARD = 2 * D_FF // 4
W_IN_SHARD = IN_DIM // 4
W_IN_ROWS = 9344


def _w_in_cols(shards, c0, c1):
    out = []
    for p in range(4):
        lo, hi = max(c0, W_IN_SHARD * p), min(c1, W_IN_SHARD * (p + 1))
        if lo < hi:
            out.append(shards[p][:, lo - W_IN_SHARD * p:hi - W_IN_SHARD * p])
    return out


def _layer_weights(w, conv, small, l):
    win = w["w_in"]
    lblk = [_w_in_cols(win, 256 * j, 256 * (j + 1)) + _w_in_cols(win, D + 256 * j, D + 256 * (j + 1)) for j in range(4)]
    wp = jnp.concatenate(_w_in_cols(win, 2048, 4096) + _w_in_cols(win, 4096, 7168) + lblk[0] + lblk[1]
                         + _w_in_cols(win, 7200, 9248) + lblk[2] + lblk[3], axis=1)
    wdt = jnp.pad(jnp.concatenate(_w_in_cols(win, 7168, 7200), axis=1), ((0, 0), (0, DT_PAD - N_HEADS)))
    row = lambda v: v.reshape(1, -1)
    pad_h = lambda v: jnp.pad(v.reshape(1, -1), ((0, 0), (0, DT_PAD - N_HEADS)))
    lw = dict(cw=conv["lru_conv_w"][l], cb=row(small["lru_conv_b"][l]),
              wa=_bd256(small["lru_w_a"][l]).astype(_MXU), wx=_bd256(small["lru_w_x"][l]).astype(_MXU),
              ba=row(small["lru_b_a"][l]), bx=row(small["lru_b_x"][l]), lam=row(small["lru_lambda"][l]))
    sw = dict(cw=conv["ssd_conv_w"][l], cb=row(small["ssd_conv_b"][l]), dtb=pad_h(small["ssd_dt_bias"][l]),
              alog=pad_h(small["ssd_A_log"][l]), dsk=row(jnp.repeat(small["ssd_D"][l], HEAD_P)),
              ng=row(small["ssd_norm_g"][l]))
    return dict(wp=wp, wdt=wdt, lw=lw, sw=sw, wba=w["w_branch"][0:D], wbb=w["w_branch"][D:3 * D],
                wout=w["w_out"], wfi=w["w_ffn_in"], wfo=w["w_ffn_out"],
                g1=row(small["norm1_g"][l]), g2=row(small["norm2_g"][l]), bgate=row(small["b_gate"][l]))


def _tiles(t):
    return dict(tmn=min(1024, t), tm=min(512, t), tm2=min(256, t), r=min(256, t), rb=min(128, t))


def _layer_fwd(h, lwt, l, comms=(None, None)):
    tl = _tiles(h.shape[0])
    n = f"l{l}_"
    xn, proj, *got_a = _norm_mm(h, lwt["g1"], lwt["wp"], tm=tl["tmn"], tn=1024, name=n + "in_proj", comm=comms[0])
    dtraw = _mm_nn(xn, lwt["wdt"], tm=tl["tm"], tn=DT_PAD, name=n + "dt_proj")
    hl, ya = _lru_fwd(proj, lwt["lw"], r=tl["r"], name=n + "lru_fwd")
    yssd, yb, states, *got_b = _ssd_fwd(proj, dtraw, lwt["sw"], rb=tl["rb"], name=n + "ssd_fwd", comm=comms[1])
    ta, tb, merged = _branch_merge(ya, yb, proj, lwt["wba"], lwt["wbb"], lwt["bgate"], tm=tl["tm"], tn=512, name=n + "merge")
    hmid = _mm_nn(merged, lwt["wout"], tm=tl["tm"], tn=512, name=n + "out_proj", residual=h)
    xn2, gu = _norm_mm(hmid, lwt["g2"], lwt["wfi"], tm=tl["tmn"], tn=FFN_SHARD, name=n + "ffn_in", out_dtype=_MXU)
    act, hout = _swiglu_mm(gu, lwt["wfo"], hmid, tm=tl["tm"], tn=512, name=n + "ffn_out")
    saved = dict(h=h, xn=xn, proj=proj, dtraw=dtraw, hl=hl, ya=ya, yssd=yssd, yb=yb, states=states, ta=ta, tb=tb,
                 merged=merged, hmid=hmid, xn2=xn2, gu=gu, act=act)
    return hout, saved, got_a, got_b


def _layer_bwd(dh, s, lwt, l, hooks=None):
    t = dh.shape[0]
    tl = _tiles(t)
    n = f"l{l}_"
    tt = tl["tm"]
    big = {}
    hooks = hooks or {}

    def wgrad(key, a, b, name, **kw):
        big[key] = _wgrad(a, b, tt=tt, name=n + name, into=big.get(key), **kw)

    dgu, *got_1 = _ffn_bwd_act(dh, lwt["wfo"], s["gu"], tm=tl["tm2"], name=n + "ffn_act_bwd", comm=hooks.get("ffn_act"))
    wgrad("w_ffn_out", s["act"], dh, "ffn_out_wgrad", ta=D_FF, tn=1024, out_shape=(D_FF, D),
          out_block=(D_FF, 1024), out_index=lambda o, j: (o, j))
    wgrad("w_ffn_in", s["xn2"], dgu, "ffn_in_wgrad", ta=D, tn=FFN_SHARD, out_shape=(4, D, FFN_SHARD),
          out_block=(None, D, FFN_SHARD), out_index=lambda o, j: (j, o, 0))
    dh1, dg2 = _mm_nt_rmsbwd(dgu, lwt["wfi"], s["hmid"], lwt["g2"], dh, tm=tl["tm"], tk=FFN_SHARD, name=n + "ffn_in_dgrad")
    dta, dtb, dproj, dbg = _outproj_bwd(dh1, lwt["wout"], s["ta"], s["tb"], s["proj"], lwt["bgate"], (t, NP),
                                        tm=tl["tm2"], name=n + "out_proj_bwd")
    rows_d = dict(ta=D, tn=512, out_block=(D, 512), out_index=lambda o, j: (o, j))
    wgrad("w_out", s["merged"], dh1, "out_proj_wgrad", out_shape=(D, D), **rows_d)
    dya = _mm_nt(dta, lwt["wba"], tm=tl["tm"], name=n + "branch_a_dgrad")
    dyb = _mm_nt(dtb, lwt["wbb"], tm=tl["tm"], name=n + "branch_b_dgrad")
    wgrad("w_branch", s["ya"], dta, "branch_a_wgrad", out_shape=(3 * D, D), a_tab=[0], o_tab=[0], **rows_d)
    wgrad("w_branch", s["yb"], dtb, "branch_b_wgrad", out_shape=(3 * D, D), a_tab=[0, 1], o_tab=[1, 2], **rows_d)
    dproj, lsm, dwa, dwx = _lru_bwd(s["proj"], s["hl"], dya, dproj, lwt["lw"], r=tl["r"], name=n + "lru_bwd")
    comm_2 = hooks["ssd"](got_1) if "ssd" in hooks else None
    dproj, ddt, gconv, gch, ghd, *got_2 = _ssd_bwd(s["proj"], s["dtraw"], s["yssd"], s["states"], dyb, dproj, lwt["sw"],
                                                   rb=tl["rb"], name=n + "ssd_bwd", comm=comm_2)
    lsm = lsm.reshape(8, 8, D).sum(axis=1)
    gconv = gconv.reshape(5, 8, XBC).sum(axis=1)
    gch = gch.reshape(2, 8, SSD_INNER).sum(axis=1)
    w_in = dict(tn=D, out_shape=(W_IN_ROWS, D), out_index=lambda o, j: (o, j))
    wgrad("w_in", dproj, s["xn"], "in_proj_wgrad", ta=1024, out_block=(1024, D),
          a_tab=list(range(9)), o_tab=[2, 3, 4, 5, 6, 0, 7, 8, 1], **w_in)
    wgrad("w_in", ddt, s["xn"], "dt_proj_wgrad", ta=DT_PAD, out_block=(DT_PAD, D), a_tab=[0],
          o_tab=[NP // DT_PAD], **w_in)
    comm_3 = hooks["in_dgrad"](big) if "in_dgrad" in hooks else None
    dh0, dg1, *got_3 = _mm_nt_rmsbwd(dproj, lwt["wp"], s["h"], lwt["g1"], dh1, tm=tl["tm"], tk=2304,
                                     name=n + "in_proj_dgrad", extra=(ddt, lwt["wdt"]), comm=comm_3)
    grads = dict(
        lru_conv_w=lsm[0:4], lru_conv_b=lsm[4], lru_b_a=lsm[5], lru_b_x=lsm[6], lru_lambda=lsm[7],
        lru_w_a=_bd256_diag(dwa), lru_w_x=_bd256_diag(dwx),
        ssd_conv_w=gconv[0:4], ssd_conv_b=gconv[4], ssd_norm_g=gch[0], ssd_D=gch[1].reshape(N_HEADS, HEAD_P).sum(axis=-1),
        ssd_dt_bias=ghd[0, 0:N_HEADS], ssd_A_log=ghd[1, 0:N_HEADS],
        b_gate=dbg[0], norm1_g=dg1[0], norm2_g=dg2[0])
    return dh0, grads, big, (got_2, got_3)


def _local_step(x, target, w, conv, small, prefetch=None, early_reduce=None):
    h = x
    w = list(w)
    lwts, saved = [], []
    for l in range(N_LAYERS):
        lwt = _layer_weights(w[l], conv, small, l)
        h, s, got_a, got_b = _layer_fwd(h, lwt, l, prefetch[:2] if (prefetch is not None and l == 0) else (None, None))
        if prefetch is not None and l == 0:
            w.append(prefetch[2](got_a, got_b))
        lwts.append(lwt)
        saved.append(s)
    loss_blk, dgf, dh = _loss_head(h, small["norm_f"].reshape(1, D), target, tm=_tiles(x.shape[0])["tm"], name="loss_head")
    per_layer, big, carried = [None] * N_LAYERS, [None] * N_LAYERS, None
    for l in reversed(range(N_LAYERS)):
        hooks = early_reduce(big[1]) if (early_reduce is not None and l == 0) else None
        dh, per_layer[l], big[l], carried = _layer_bwd(dh, saved[l], lwts[l], l, hooks)
    grads = {k: jnp.stack([per_layer[l][k] for l in range(N_LAYERS)], axis=0) for k in per_layer[0]}
    grads["norm_f"] = dgf[0]
    return loss_blk, dh, grads, big, carried


PACK_W = 1024
BIG = (("w_in", W_IN_SHARD, D, W_IN_SHARD, 256), ("w_branch", 768, D, 256, D), ("w_out", 256, D, 256, D),
       ("w_ffn_in", D, FFN_SHARD, 256, FFN_SHARD), ("w_ffn_out", 704, D, 352, D))
CONV = ("lru_conv_w", "ssd_conv_w")
SMALL = ("norm1_g", "b_gate", "lru_conv_b", "lru_w_a", "lru_b_a", "lru_w_x", "lru_b_x", "lru_lambda", "ssd_conv_b",
         "ssd_dt_bias", "ssd_A_log", "ssd_D", "ssd_norm_g", "norm2_g", "norm_f")
_WIRE = jnp.bfloat16
N_CHIPS = 4
N_DEV = 8


def _mesh_pos():
    return lax.axis_index("x"), lax.axis_index("y"), lax.axis_index("c")


HBM_SPEC = pl.BlockSpec(memory_space=pltpu.HBM)


def _remote(src, dst, send_sems, recv_sems, k, to):
    return pltpu.make_async_remote_copy(src_ref=src, dst_ref=dst, send_sem=send_sems.at[k], recv_sem=recv_sems.at[k],
                                        device_id=to, device_id_type=MESH)


def _other_chips(x, y):
    return [(1 - x, y), (x, 1 - y), (1 - x, 1 - y)]


def _weight_fetch(loc, layer, owner):
    names = list(owner)
    rows = {n: loc[n].shape[1] for n in names}
    by_chip = ("w_in", "w_ffn_in")
    shapes = [((N_CHIPS,) + loc[n].shape[1:]) if n in by_chip else (N_CHIPS * rows[n], D) for n in names]

    def place(o_ref, n, chip):
        if n in by_chip:
            return o_ref.at[chip]
        return o_ref.at[pl.ds(pl.multiple_of(chip * rows[n], 16), rows[n]), :]

    def step(which, in_refs, o_refs, send_sems, recv_sems):
        x, y, c = _mesh_pos()
        s = 2 * x + y
        sib = (x, y, 1 - c)
        chips = _other_chips(x, y)
        for core in (0, 1):
            @pl.when(c == core)
            def _():
                for k, n in enumerate(names):
                    for j, (px, py) in enumerate(chips):
                        landed = place(o_refs[k], n, 2 * px + py)
                        sent = _remote(in_refs[k].at[layer], place(o_refs[k], n, s), send_sems, recv_sems, 3 * k + j,
                                       (px, py, c))
                        arrives = _remote(in_refs[k].at[layer], landed, send_sems, recv_sems, 3 * k + j, (px, py, c))
                        passed = _remote(landed, landed, send_sems, recv_sems, 3 * (len(names) + k) + j, sib)
                        if owner[n] == core:
                            if which == "start":
                                sent.start()
                            elif which == "mid":
                                arrives.wait_recv()
                                passed.start()
                            else:
                                sent.wait_send()
                                passed.wait_send()
                        elif which == "end":
                            passed.wait_recv()

    return dict(inputs=[loc[n] for n in names], names=names,
                out_shapes=[jax.ShapeDtypeStruct(shp, loc[n].dtype) for shp, n in zip(shapes, names)],
                sems=[pltpu.SemaphoreType.DMA((6 * len(names),)), pltpu.SemaphoreType.DMA((6 * len(names),))],
                start=functools.partial(step, "start"), mid=functools.partial(step, "mid"),
                end=functools.partial(step, "end"))


def _comm_now(comm, name):
    n, no = len(comm["inputs"]), len(comm["out_shapes"])

    def body(*refs):
        parts = (refs[:n], refs[n:n + no]) + tuple(refs[n + no:])
        comm["start"](*parts)
        comm["mid"](*parts)
        comm["end"](*parts)

    return pl.pallas_call(
        body, name=name, in_specs=[HBM_SPEC] * n, out_specs=[HBM_SPEC] * no, out_shape=comm["out_shapes"],
        scratch_shapes=comm["sems"],
    )(*comm["inputs"])


def _sibling_send(bufs, layer):
    n = len(bufs)

    def step(which, in_refs, o_refs, send_sems, recv_sems):
        x, y, c = _mesh_pos()
        copies = [_remote(in_refs[k], o_refs[k], send_sems, recv_sems, k, (x, y, 1 - c)) for k in range(n)]

        @pl.when(c != layer)
        def _():
            for cp in copies:
                if which == "start":
                    cp.start()
                elif which == "end":
                    cp.wait_send()

        @pl.when(c == layer)
        def _():
            for cp in copies:
                if which == "end":
                    cp.wait_recv()

    return dict(inputs=list(bufs), out_shapes=[jax.ShapeDtypeStruct(b.shape, b.dtype) for b in bufs],
                sems=[pltpu.SemaphoreType.DMA((n,)), pltpu.SemaphoreType.DMA((n,))],
                start=functools.partial(step, "start"), mid=functools.partial(step, "mid"),
                end=functools.partial(step, "end"))


def _add_cast(g, recv, *, a, tr, tc, name):
    wd = g.shape[1]
    nr = a // tr

    def body(g_ref, r_ref, o_ref):
        o_ref[...] = (g_ref[...] + r_ref[...]).astype(o_ref.dtype)

    blk = pl.BlockSpec((tr, tc), lambda p, i, j: (p * nr + i, j))
    return pl.pallas_call(
        body, name=name, grid=(N_CHIPS, nr, wd // tc), in_specs=[blk, blk],
        out_specs=pl.BlockSpec((None, tr, tc), lambda p, i, j: (p, i, j)),
        out_shape=jax.ShapeDtypeStruct((N_CHIPS, a, wd), _WIRE),
        compiler_params=_cp(("parallel", "parallel", "parallel")),
    )(g, recv)


def _chip_exchange(parts, layer):
    n = len(parts)

    def step(which, s_refs, o_refs, send_sems, recv_sems):
        x, y, c = _mesh_pos()
        s = 2 * x + y

        @pl.when(c == layer)
        def _():
            for j, (px, py) in enumerate(_other_chips(x, y)):
                for k in range(n):
                    p = 2 * px + py
                    sent = _remote(s_refs[k].at[p], o_refs[k].at[s], send_sems, recv_sems, n * j + k, (px, py, c))
                    if which == "start":
                        sent.start()
                    elif which == "end":
                        _remote(s_refs[k].at[p], o_refs[k].at[p], send_sems, recv_sems, n * j + k, (px, py, c)).wait_recv()
                        sent.wait_send()

    return dict(inputs=list(parts), out_shapes=[jax.ShapeDtypeStruct(p.shape, p.dtype) for p in parts],
                sems=[pltpu.SemaphoreType.DMA((3 * n,)), pltpu.SemaphoreType.DMA((3 * n,))],
                start=functools.partial(step, "start"), mid=functools.partial(step, "mid"),
                end=functools.partial(step, "end"))


def _sum_slots(slots, own, sel, *, tr, tc, name, layer=None, into=None):
    n, rows, wd = slots.shape
    k = own.shape[0]

    def body(sel_ref, s_ref, own_ref, *rest):
        o_ref = rest[-1]
        mine = sel_ref[0]
        acc = jnp.zeros((tr, tc), F32)
        for p in range(n):
            acc = acc + jnp.where(mine == p, own_ref[...].astype(F32), s_ref[p].astype(F32))
        o_ref[...] = acc

    if layer is not None:
        out_spec = pl.BlockSpec((None, tr, tc), lambda i, j, sel_ref: (layer, i, j))
        out_shape = jax.ShapeDtypeStruct((N_LAYERS, rows, wd), F32)
    else:
        out_spec = pl.BlockSpec((tr, tc), lambda i, j, sel_ref: (i, j))
        out_shape = jax.ShapeDtypeStruct((rows, wd), F32)
    in_specs = [pl.BlockSpec((n, tr, tc), lambda i, j, sel_ref: (0, i, j)),
                pl.BlockSpec((None, tr, tc), lambda i, j, sel_ref: (sel_ref[0] if k > 1 else 0, i, j))]
    args = [sel, slots, own]
    if into is not None:
        in_specs.append(pl.BlockSpec(memory_space=pl.ANY))
        args.append(into)
    return pl.pallas_call(
        body, name=name,
        grid_spec=pltpu.PrefetchScalarGridSpec(num_scalar_prefetch=1, grid=(rows // tr, wd // tc), in_specs=in_specs,
                                               out_specs=out_spec),
        out_shape=out_shape, input_output_aliases={3: 0} if into is not None else {},
        compiler_params=_cp(("parallel", "parallel")),
    )(*args)


def _sibling_share(both):
    n = len(both)

    def body(*refs):
        o_refs, (send_sems, recv_sems) = refs[n:2 * n], refs[2 * n:]
        x, y, c = _mesh_pos()
        sends = [_remote(o_refs[k].at[c], o_refs[k].at[c], send_sems, recv_sems, k, (x, y, 1 - c)) for k in range(n)]
        for cp in sends:
            cp.start()
        for k in range(n):
            _remote(o_refs[k].at[1 - c], o_refs[k].at[1 - c], send_sems, recv_sems, k, (x, y, 1 - c)).wait_recv()
        for cp in sends:
            cp.wait_send()

    return pl.pallas_call(
        body, name="grad_sibling_share", in_specs=[HBM_SPEC] * n, out_specs=[HBM_SPEC] * n,
        out_shape=[jax.ShapeDtypeStruct(b.shape, b.dtype) for b in both], input_output_aliases={k: k for k in range(n)},
        scratch_shapes=[pltpu.SemaphoreType.DMA((n,)), pltpu.SemaphoreType.DMA((n,))],
    )(*both)


def _allgather_devices(part, name):
    rows, wd = part.shape

    def body(p_ref, o_ref, send_sems, recv_sems):
        x, y, c = _mesh_pos()
        sib = (x, y, 1 - c)
        chips = _other_chips(x, y)
        slot = lambda px, py, pc: o_ref.at[4 * px + 2 * py + pc]
        first = [_remote(p_ref, slot(x, y, c), send_sems, recv_sems, 0, sib)]
        first += [_remote(p_ref, slot(x, y, c), send_sems, recv_sems, 1 + j, (px, py, c)) for j, (px, py) in enumerate(chips)]
        for cp in first:
            cp.start()
        passed = []
        for j, (px, py) in enumerate(chips):
            _remote(p_ref, slot(px, py, c), send_sems, recv_sems, 1 + j, (px, py, c)).wait_recv()
            cp = _remote(slot(px, py, c), slot(px, py, c), send_sems, recv_sems, 4 + j, sib)
            cp.start()
            passed.append(cp)
        _remote(p_ref, slot(x, y, 1 - c), send_sems, recv_sems, 0, sib).wait_recv()
        for j, (px, py) in enumerate(chips):
            _remote(slot(px, py, 1 - c), slot(px, py, 1 - c), send_sems, recv_sems, 4 + j, sib).wait_recv()
        for cp in first + passed:
            cp.wait_send()

    return pl.pallas_call(
        body, name=name, in_specs=[HBM_SPEC], out_specs=HBM_SPEC,
        out_shape=jax.ShapeDtypeStruct((N_DEV, rows, wd), part.dtype),
        scratch_shapes=[pltpu.SemaphoreType.DMA((N_DEV - 1,)), pltpu.SemaphoreType.DMA((N_DEV - 1,))],
    )(part)


def _by_chip_to_full(stack):
    _, nl, r, b = stack.shape
    return stack.transpose(1, 2, 0, 3).reshape(nl, r, N_CHIPS * b)


def _sharded_step(a):
    x = a["x"][0]
    target = a["loss_target"][0]
    cx, cy, cc = _mesh_pos()
    chip = (2 * cx + cy).astype(jnp.int32)
    core = cc.astype(jnp.int32)
    me = (4 * cx + 2 * cy + cc).astype(jnp.int32)
    zero = jnp.zeros((), jnp.int32)
    dus = lax.dynamic_update_slice

    loc = {n: a[n].astype(_MXU) for n, *_ in BIG}

    def with_own(got, names, layer):
        out = {}
        for g, n in zip(got, names):
            mine = loc[n][layer]
            out[n] = (dus(g, mine[None], (chip, zero, zero)) if g.ndim == 3 else dus(g, mine, (chip * mine.shape[0], zero)))
        return out

    now = _weight_fetch(loc, 0, {"w_in": 0, "w_out": 0, "w_branch": 1, "w_ffn_in": 1, "w_ffn_out": 1})
    w0 = with_own(_comm_now(now, "allgather_weights"), now["names"], 0)
    later_a = _weight_fetch(loc, 1, {"w_in": 0})
    later_b = _weight_fetch(loc, 1, {"w_ffn_in": 0, "w_branch": 1, "w_out": 1, "w_ffn_out": 1})
    second = lambda got_a, got_b: {**with_own(got_a, later_a["names"], 1), **with_own(got_b, later_b["names"], 1)}
    conv_loc = jnp.concatenate([a[n].reshape(-1, PACK_W) for n in CONV], axis=0)
    conv_all = dus(_allgather_devices(conv_loc, "conv_weight_allgather"), conv_loc[None], (me, zero, zero))[0::2]
    conv, off = {}, 0
    for n in CONV:
        rows = a[n].size // PACK_W
        conv[n] = _by_chip_to_full(conv_all[:, off:off + rows].reshape((N_CHIPS,) + a[n].shape))
        off += rows
    small = {n: a[n] for n in SMALL}

    sel = chip.reshape(1)
    views = lambda big_l: [big_l[n].reshape(-1, wd) for n, _, wd, _, _ in BIG]

    def partial_sums(big_l, recv, layer):
        return [_add_cast(v, r, a=rows, tr=tr, tc=tc, name=f"grad_add_sibling_l{layer}_{n}")
                for v, r, (n, rows, _, tr, tc) in zip(views(big_l), recv, BIG)]

    def reduced(slots, parts, layer, into):
        return [_sum_slots(s, p, sel, tr=tr, tc=tc, name=f"grad_sum_chips_l{layer}_{n}", layer=layer, into=buf)
                for s, p, buf, (n, _, _, tr, tc) in zip(slots, parts, into, BIG)]

    kept = {}

    def early_reduce(big_1):
        def exchange_second(recv_1):
            kept["parts_1"] = partial_sums(big_1, recv_1, 1)
            return _chip_exchange(kept["parts_1"], 1)
        return dict(ffn_act=_sibling_send(views(big_1), 1), ssd=exchange_second,
                    in_dgrad=lambda big_0: _sibling_send(views(big_0), 0))

    loss_blk, grad_x, grads, big, (slots_1, recv_0) = _local_step(x, target, [w0], conv, small,
                                                                   (later_a, later_b, second), early_reduce)
    loss = lax.psum(loss_blk[0, 0], ("x", "y", "c"))
    both = reduced(slots_1, kept["parts_1"], 1, [None] * len(BIG))
    parts_0 = partial_sums(big[0], recv_0, 0)
    both = reduced(_comm_now(_chip_exchange(parts_0, 0), "grad_chip_exchange"), parts_0, 0, both)
    done = dict(zip([n for n, *_ in BIG], _sibling_share(both)))
    g_big = {n: done[n].reshape(a[n].shape) for n in ("w_branch", "w_out", "w_ffn_in", "w_ffn_out")}
    gt = done["w_in"].transpose(0, 2, 1)
    first = jnp.concatenate([gt[..., 512 * j + 256 * part:512 * j + 256 * (part + 1)] for part in range(2) for j in range(4)]
                            + [gt[..., 2 * D:]], axis=-1)
    tail = W_IN_SHARD - (IN_DIM - 7168)
    last = jnp.concatenate([gt[..., :tail], gt[..., W_IN_SHARD - N_HEADS:], gt[..., tail:W_IN_SHARD - N_HEADS]], axis=-1)
    g_big["w_in"] = jnp.where(chip == 0, first, jnp.where(chip == N_CHIPS - 1, last, gt))

    names = SMALL + CONV
    srows = -(-sum(grads[n].size for n in names) // (8 * PACK_W)) * 8
    flat = lambda d, ns: jnp.concatenate([d[n].reshape(-1) for n in ns])
    padto = lambda v: jnp.pad(v, (0, srows * PACK_W - v.shape[0])).reshape(srows, PACK_W)
    g_own = padto(flat(grads, names))
    g_sum = _sum_slots(_allgather_devices(g_own, "small_grad_allgather"), g_own[None], jnp.stack([me, zero]),
                       tr=srows, tc=PACK_W, name="small_grad_sum")
    off, g_small = 0, {}
    for n in names:
        g_small[n] = g_sum.reshape(-1)[off:off + grads[n].size].reshape(grads[n].shape)
        off += grads[n].size
    for n in CONV:
        width = a[n].shape[2]
        g_big[n] = lax.dynamic_slice(g_small.pop(n), (zero, zero, chip * width), a[n].shape)

    out_g, out_d, out_m, out_v = {}, {}, {}, {}
    for n in g_big:
        shp = a[n].shape
        two_d = (shp[0] * shp[1], shp[2])
        d_, m_, v_ = _adamw(a[n].reshape(two_d), g_big[n].reshape(two_d), a["m_" + n].reshape(two_d),
                            a["v_" + n].reshape(two_d), name="adamw_" + n)
        out_g[n], out_d[n], out_m[n], out_v[n] = g_big[n], d_.reshape(shp), m_.reshape(shp), v_.reshape(shp)
    d_, m_, v_ = _adamw(padto(flat(a, SMALL)), padto(flat(g_small, SMALL)), padto(flat({n: a["m_" + n] for n in SMALL}, SMALL)),
                        padto(flat({n: a["v_" + n] for n in SMALL}, SMALL)), name="adamw_small")
    off = 0
    for n in SMALL:
        cut = lambda v: v.reshape(-1)[off:off + a[n].size].reshape(a[n].shape)
        out_g[n], out_d[n], out_m[n], out_v[n] = g_small[n], cut(d_), cut(m_), cut(v_)
        off += a[n].size
    return loss, grad_x[None], out_g, out_d, out_m, out_v


WEIGHTS = ("norm1_g", "w_in", "b_gate", "lru_conv_w", "lru_conv_b", "lru_w_a", "lru_b_a", "lru_w_x", "lru_b_x", "lru_lambda",
           "ssd_conv_w", "ssd_conv_b", "ssd_dt_bias", "ssd_A_log", "ssd_D", "ssd_norm_g", "w_branch", "w_out", "norm2_g",
           "w_ffn_in", "w_ffn_out", "norm_f")
INPUTS = ("x",) + WEIGHTS + ("loss_target",) + tuple("m_" + n for n in WEIGHTS) + tuple("v_" + n for n in WEIGHTS)


def kernel(x, norm1_g, w_in, b_gate, lru_conv_w, lru_conv_b, lru_w_a, lru_b_a, lru_w_x, lru_b_x, lru_lambda, ssd_conv_w, ssd_conv_b, ssd_dt_bias, ssd_A_log, ssd_D, ssd_norm_g, w_branch, w_out, norm2_g, w_ffn_in, w_ffn_out, norm_f, loss_target, m_norm1_g, m_w_in, m_b_gate, m_lru_conv_w, m_lru_conv_b, m_lru_w_a, m_lru_b_a, m_lru_w_x, m_lru_b_x, m_lru_lambda, m_ssd_conv_w, m_ssd_conv_b, m_ssd_dt_bias, m_ssd_A_log, m_ssd_D, m_ssd_norm_g, m_w_branch, m_w_out, m_norm2_g, m_w_ffn_in, m_w_ffn_out, m_norm_f, v_norm1_g, v_w_in, v_b_gate, v_lru_conv_w, v_lru_conv_b, v_lru_w_a, v_lru_b_a, v_lru_w_x, v_lru_b_x, v_lru_lambda, v_ssd_conv_w, v_ssd_conv_b, v_ssd_dt_bias, v_ssd_A_log, v_ssd_D, v_ssd_norm_g, v_w_branch, v_w_out, v_norm2_g, v_w_ffn_in, v_w_ffn_out, v_norm_f):
    args = (x, norm1_g, w_in, b_gate, lru_conv_w, lru_conv_b, lru_w_a, lru_b_a, lru_w_x, lru_b_x, lru_lambda, ssd_conv_w, ssd_conv_b, ssd_dt_bias, ssd_A_log, ssd_D, ssd_norm_g, w_branch, w_out, norm2_g, w_ffn_in, w_ffn_out, norm_f, loss_target, m_norm1_g, m_w_in, m_b_gate, m_lru_conv_w, m_lru_conv_b, m_lru_w_a, m_lru_b_a, m_lru_w_x, m_lru_b_x, m_lru_lambda, m_ssd_conv_w, m_ssd_conv_b, m_ssd_dt_bias, m_ssd_A_log, m_ssd_D, m_ssd_norm_g, m_w_branch, m_w_out, m_norm2_g, m_w_ffn_in, m_w_ffn_out, m_norm_f, v_norm1_g, v_w_in, v_b_gate, v_lru_conv_w, v_lru_conv_b, v_lru_w_a, v_lru_b_a, v_lru_w_x, v_lru_b_x, v_lru_lambda, v_ssd_conv_w, v_ssd_conv_b, v_ssd_dt_bias, v_ssd_A_log, v_ssd_D, v_ssd_norm_g, v_w_branch, v_w_out, v_norm2_g, v_w_ffn_in, v_w_ffn_out, v_norm_f)
    assert len(args) == len(INPUTS)
    loss, grad_x, g, d, m, v = _sharded_step(dict(zip(INPUTS, args)))
    return (loss, grad_x, *[g[n] for n in WEIGHTS], *[d[n] for n in WEIGHTS], *[m[n] for n in WEIGHTS],
            *[v[n] for n in WEIGHTS])
```

```python
import functools
import math

import numpy as np
import jax
import jax.numpy as jnp
from jax import lax
from jax.experimental import pallas as pl
from jax.experimental.pallas import tpu as pltpu

F32 = jnp.float32
BF16 = jnp.bfloat16
_MXU = jnp.bfloat16
_HI = lax.Precision.HIGHEST

D = 1024
EPS = 1e-6
N_LAYERS = 2
LRU_C = 8.0
N_HEADS = 32
HEAD_P = 64
N_GROUPS = 4
N_STATE = 128
SSD_INNER = 2048
XBC = 3072
D_FF = 2816
CHUNK = 64
NORM_ROWS = 32
IN_DIM = 9248

NP = 9216
ZX_W = 5120
G0 = 6144
LBLK = 512
DT_PAD = 128

VMEM_LIMIT_BYTES_V7X = 56 * 1024 * 1024

ADAM_LR, ADAM_B1, ADAM_B2, ADAM_EPS, ADAM_WD, ADAM_STEP = 0.001, 0.9, 0.999, 1e-08, 0.01, 10
MESH = pl.DeviceIdType.MESH


def _cp(sem):
    return pltpu.CompilerParams(dimension_semantics=sem, vmem_limit_bytes=VMEM_LIMIT_BYTES_V7X)


def _lblk_col(j):
    return 10 + j + 4 * (j // 2)


def _sigmoid(x):
    return 0.5 * jnp.tanh(0.5 * x) + 0.5


def _softplus(x):
    return jnp.maximum(x, 0.0) + jnp.log(1.0 + jnp.exp(-jnp.abs(x)))


def _silu(x):
    return x * _sigmoid(x)


def _dsilu(x):
    s = _sigmoid(x)
    return s * (1.0 + x * (1.0 - s))


_GELU_C0 = math.sqrt(2.0 / math.pi)
_GELU_C1 = 0.044715


def _gelu_and_grad(x):
    t = jnp.tanh(_GELU_C0 * (x + _GELU_C1 * x * x * x))
    g = 0.5 * x * (1.0 + t)
    dg = 0.5 * (1.0 + t) + 0.5 * x * (1.0 - t * t) * _GELU_C0 * (1.0 + 3.0 * _GELU_C1 * x * x)
    return g, dg


def _one_minus_exp(x):
    p = 1.0 + x * (1.0 / 7.0)
    p = 1.0 + x * (1.0 / 6.0) * p
    p = 1.0 + x * (1.0 / 5.0) * p
    p = 1.0 + x * (1.0 / 4.0) * p
    p = 1.0 + x * (1.0 / 3.0) * p
    p = 1.0 + x * (1.0 / 2.0) * p
    return jnp.where(x > -0.3, -x * p, 1.0 - jnp.exp(x))


def _dot(a, b):
    return jnp.dot(a.astype(_MXU), b.astype(_MXU), preferred_element_type=F32)


def _dot_nt(a, b):
    return lax.dot_general(a.astype(_MXU), b.astype(_MXU), (((1,), (1,)), ((), ())), preferred_element_type=F32)


def _dot_tn(a, b):
    return lax.dot_general(a.astype(_MXU), b.astype(_MXU), (((0,), (0,)), ((), ())), preferred_element_type=F32)


def _shift_down(x, prev8, k):
    xr = pltpu.roll(x, k, 0)
    pr = pltpu.roll(prev8, k, 0)
    row = lax.broadcasted_iota(jnp.int32, prev8.shape, 0)
    head = jnp.where(row < k, pr, xr[0:8])
    return jnp.concatenate([head, xr[8:]], axis=0)


def _shift_up(x, next8, k):
    r = x.shape[0]
    xr = pltpu.roll(x, r - k, 0)
    nr = pltpu.roll(next8, 8 - k, 0)
    row = lax.broadcasted_iota(jnp.int32, next8.shape, 0)
    tail = jnp.where(row >= 8 - k, nr, xr[r - 8:r])
    return jnp.concatenate([xr[:r - 8], tail], axis=0)


def _conv4(x, prev8, w_ref, b_ref, cols=slice(None)):
    acc = x * w_ref[3:4, cols] + b_ref[0:1, cols]
    for k in (1, 2, 3):
        acc = acc + _shift_down(x, prev8, k) * w_ref[3 - k:4 - k, cols]
    return acc


def _conv4_bwd_x(dy, next8, w_ref, cols=slice(None)):
    acc = dy * w_ref[3:4, cols]
    for k in (1, 2, 3):
        acc = acc + _shift_up(dy, next8, k) * w_ref[3 - k:4 - k, cols]
    return acc


def _lin_scan(a, b, reverse):
    r = a.shape[0]
    row = lax.broadcasted_iota(jnp.int32, a.shape, 0)
    d = 1
    while d < r:
        sh = (r - d) if reverse else d
        a_s = pltpu.roll(a, sh, 0)
        b_s = pltpu.roll(b, sh, 0)
        m = (row < r - d) if reverse else (row >= d)
        b = jnp.where(m, a * b_s + b, b)
        a = jnp.where(m, a * a_s, a)
        d *= 2
    return a, b


def _rsum(x):
    return jnp.sum(x, axis=0, keepdims=True)


def _comm_specs(comm):
    if comm is None:
        return [], [], [], [], []
    n = len(comm["inputs"])
    return list(comm["inputs"]), [HBM_SPEC] * n, [HBM_SPEC] * len(comm["out_shapes"]), list(comm["out_shapes"]), comm["sems"]


def _comm_steps(comm, refs, n_in, n_out, first, mid, last):
    ni, no = len(comm["inputs"]), len(comm["out_shapes"])
    parts = (refs[n_in:n_in + ni], refs[n_out:n_out + no]) + tuple(refs[len(refs) - len(comm["sems"]):])
    for when, what in ((first, "start"), (mid, "mid"), (last, "end")):
        @pl.when(when)
        def _():
            comm[what](*parts)


def _norm_mm(h, gamma, w, *, tm, tn, name, out_dtype=F32, comm=None):
    m, k = h.shape
    if w.ndim == 3:
        assert w.shape[2] == tn
        n = w.shape[0] * tn
        w_spec = pl.BlockSpec((None, k, tn), lambda i, j: (j, 0, 0))
    else:
        n = w.shape[1]
        w_spec = pl.BlockSpec((k, tn), lambda i, j: (0, j))

    c_args, c_in_specs, c_out_specs, c_out_shapes, c_sems = _comm_specs(comm)
    ni, nj = m // tm, n // tn

    def body(*refs):
        h_ref, g_ref, w_ref = refs[:3]
        xn_ref, o_ref = refs[3 + len(c_args):5 + len(c_args)]
        i, j = pl.program_id(0), pl.program_id(1)
        if comm is not None:
            _comm_steps(comm, refs, 3, 5 + len(c_args), (i == 0) & (j == 0), (i == (3 * ni) // 4) & (j == 0),
                        (i == ni - 1) & (j == nj - 1))

        @pl.when(j == 0)
        def _():
            x = h_ref[...]
            r = lax.rsqrt(jnp.mean(x * x, axis=-1, keepdims=True) + EPS)
            xn_ref[...] = ((x * r) * g_ref[...]).astype(xn_ref.dtype)
        o_ref[...] = jnp.dot(xn_ref[...], w_ref[...], preferred_element_type=F32).astype(o_ref.dtype)

    return pl.pallas_call(
        body, name=name, grid=(ni, nj),
        in_specs=[pl.BlockSpec((tm, k), lambda i, j: (i, 0)), pl.BlockSpec((1, k), lambda i, j: (0, 0)), w_spec] + c_in_specs,
        out_specs=[pl.BlockSpec((tm, k), lambda i, j: (i, 0)), pl.BlockSpec((tm, tn), lambda i, j: (i, j))] + c_out_specs,
        out_shape=[jax.ShapeDtypeStruct((m, k), _MXU), jax.ShapeDtypeStruct((m, n), out_dtype)] + c_out_shapes,
        scratch_shapes=c_sems,
        compiler_params=_cp(("arbitrary", "arbitrary") if comm is not None else ("parallel", "arbitrary")),
    )(h, gamma, w, *c_args)


def _mm_nn(a, w, *, tm, tn, name, residual=None):
    m, k = a.shape
    n = w.shape[1]

    def body(*refs):
        if residual is None:
            a_ref, w_ref, o_ref = refs
            o_ref[...] = _dot(a_ref[...], w_ref[...])
        else:
            a_ref, w_ref, r_ref, o_ref = refs
            o_ref[...] = _dot(a_ref[...], w_ref[...]) + r_ref[...]

    in_specs = [pl.BlockSpec((tm, k), lambda i, j: (i, 0)), pl.BlockSpec((k, tn), lambda i, j: (0, j))]
    args = [a, w]
    if residual is not None:
        in_specs.append(pl.BlockSpec((tm, tn), lambda i, j: (i, j)))
        args.append(residual)
    return pl.pallas_call(
        body, name=name, grid=(m // tm, n // tn), in_specs=in_specs,
        out_specs=pl.BlockSpec((tm, tn), lambda i, j: (i, j)),
        out_shape=jax.ShapeDtypeStruct((m, n), F32),
        compiler_params=_cp(("parallel", "parallel")),
    )(*args)


def _wgrad(a, b, *, tt, ta, tn, name, out_shape, out_block, out_index, a_tab=None, o_tab=None, into=None):
    t = a.shape[0]
    a_tab = list(range(a.shape[1] // ta)) if a_tab is None else a_tab
    o_tab = a_tab if o_tab is None else o_tab
    nb = b.shape[1] // tn

    def body(at_ref, ot_ref, a_ref, b_ref, *rest):
        del at_ref, ot_ref
        o_ref = rest[-1]

        @pl.when(pl.program_id(2) == 0)
        def _():
            o_ref[...] = jnp.zeros_like(o_ref)
        o_ref[...] += _dot_tn(a_ref[...], b_ref[...])

    in_specs = [pl.BlockSpec((tt, ta), lambda r, j, i, at, ot: (i, at[r])),
                pl.BlockSpec((tt, tn), lambda r, j, i, at, ot: (i, j))]
    args = [jnp.asarray(a_tab, jnp.int32), jnp.asarray(o_tab, jnp.int32), a, b]
    aliases = {}
    if into is not None:
        in_specs.append(pl.BlockSpec(memory_space=pl.ANY))
        args.append(into)
        aliases = {4: 0}
    return pl.pallas_call(
        body, name=name,
        grid_spec=pltpu.PrefetchScalarGridSpec(
            num_scalar_prefetch=2, grid=(len(a_tab), nb, t // tt), in_specs=in_specs,
            out_specs=pl.BlockSpec(out_block, lambda r, j, i, at, ot: out_index(ot[r], j))),
        out_shape=jax.ShapeDtypeStruct(out_shape, F32), input_output_aliases=aliases,
        compiler_params=_cp(("parallel", "parallel", "arbitrary")),
    )(*args)


def _mm_nt(a, w, *, tm, name):
    m, kc = a.shape
    n = w.shape[0]

    def body(a_ref, w_ref, o_ref):
        o_ref[...] = _dot_nt(a_ref[...], w_ref[...])

    return pl.pallas_call(
        body, name=name, grid=(m // tm,),
        in_specs=[pl.BlockSpec((tm, kc), lambda i: (i, 0)), pl.BlockSpec((n, kc), lambda i: (0, 0))],
        out_specs=pl.BlockSpec((tm, n), lambda i: (i, 0)),
        out_shape=jax.ShapeDtypeStruct((m, n), F32),
        compiler_params=_cp(("parallel",)),
    )(a, w)


def _mm_nt_rmsbwd(dy, w, x, gamma, dres, *, tm, tk, name, extra=None, comm=None):
    m, kc = dy.shape
    nk = kc // tk
    ni = m // tm
    n_x = 5 if extra is None else 7
    c_args, c_in_specs, c_out_specs, c_out_shapes, c_sems = _comm_specs(comm)
    if w.ndim == 3:
        assert w.shape[0] == nk and w.shape[2] == tk
        d = w.shape[1]
        w_spec = pl.BlockSpec((None, d, tk), lambda i, k: (k, 0, 0))
    else:
        d = w.shape[0]
        w_spec = pl.BlockSpec((d, tk), lambda i, k: (0, k))

    def body(*refs):
        dy_ref, w_ref, x_ref, g_ref, r_ref = refs[:5]
        if extra is not None:
            dy2_ref, w2_ref = refs[5:7]
        n_out = n_x + len(c_args)
        dx_ref, dg_ref = refs[n_out:n_out + 2]
        acc_ref = refs[n_out + 2 + len(c_out_shapes)]
        i, kk = pl.program_id(0), pl.program_id(1)
        if comm is not None:
            _comm_steps(comm, refs, n_x, n_out + 2, (i == 0) & (kk == 0), (i == (3 * ni) // 4) & (kk == 0),
                        (i == ni - 1) & (kk == nk - 1))

        @pl.when(kk == 0)
        def _():
            acc_ref[...] = jnp.zeros_like(acc_ref)

        @pl.when((i == 0) & (kk == 0))
        def _():
            dg_ref[...] = jnp.zeros_like(dg_ref)

        acc_ref[...] += _dot_nt(dy_ref[...], w_ref[...])

        @pl.when(kk == nk - 1)
        def _():
            dxn = acc_ref[...]
            if extra is not None:
                dxn = dxn + _dot_nt(dy2_ref[...], w2_ref[...])
            xv = x_ref[...]
            r = lax.rsqrt(jnp.mean(xv * xv, axis=-1, keepdims=True) + EPS)
            xh = xv * r
            dg_ref[0:1, :] += _rsum(dxn * xh)
            dxh = dxn * g_ref[...]
            dx_ref[...] = r_ref[...] + r * (dxh - xh * jnp.mean(dxh * xh, axis=-1, keepdims=True))

    in_specs = [pl.BlockSpec((tm, tk), lambda i, k: (i, k)), w_spec,
                pl.BlockSpec((tm, d), lambda i, k: (i, 0)), pl.BlockSpec((1, d), lambda i, k: (0, 0)),
                pl.BlockSpec((tm, d), lambda i, k: (i, 0))]
    args = [dy, w, x, gamma, dres]
    if extra is not None:
        k2 = extra[0].shape[1]
        in_specs += [pl.BlockSpec((tm, k2), lambda i, k: (i, 0)), pl.BlockSpec((d, k2), lambda i, k: (0, 0))]
        args += list(extra)
    return pl.pallas_call(
        body, name=name, grid=(ni, nk), in_specs=in_specs + c_in_specs,
        out_specs=[pl.BlockSpec((tm, d), lambda i, k: (i, 0)), pl.BlockSpec((8, d), lambda i, k: (0, 0))] + c_out_specs,
        out_shape=[jax.ShapeDtypeStruct((m, d), F32), jax.ShapeDtypeStruct((8, d), F32)] + c_out_shapes,
        scratch_shapes=[pltpu.VMEM((tm, d), F32)] + c_sems,
        compiler_params=_cp(("arbitrary", "arbitrary")),
    )(*args, *c_args)


def _rsum8(x):
    acc = x[0:8]
    for g in range(1, x.shape[0] // 8):
        acc = acc + x[8 * g:8 * (g + 1)]
    return acc


def _lru_gates(x, prev8, cw_ref, cb_ref, wa_ref, wx_ref, ba_ref, bx_ref, lam_ref):
    u = _conv4(x, prev8, cw_ref, cb_ref)
    ra =_sigmoid(_dot(u, wa_ref[0]) + ba_ref[...])
    ia = _sigmoid(_dot(u, wx_ref[0]) + bx_ref[...])
    sp = _softplus(-lam_ref[...])
    log_a = -LRU_C * ra * sp
    a = jnp.exp(log_a)
    m2 = _one_minus_exp(2.0 * log_a)
    mult = jnp.sqrt(m2)
    return u, ra, ia, sp, a, m2, mult


def _lru_fwd(proj, lw, *, r, name, comm=None):
    t = proj.shape[0]
    nt = t // r
    c_args, c_in_specs, c_out_specs, c_out_shapes, c_sems = _comm_specs(comm)

    def body(*refs):
        xg_ref, xp_ref, cw_ref, cb_ref, wa_ref, wx_ref, ba_ref, bx_ref, lam_ref = refs[:9]
        hl_ref, ya_ref = refs[9 + len(c_args):11 + len(c_args)]
        carry_ref = refs[11 + len(c_args) + len(c_out_shapes)]
        i = pl.program_id(1)
        if comm is not None:
            j = pl.program_id(0)
            _comm_steps(comm, refs, 9, 11 + len(c_args), (j == 0) & (i == 0), (j == 3) & (i == 0), (j == 3) & (i == nt - 1))

        @pl.when(i == 0)
        def _():
            carry_ref[...] = jnp.zeros_like(carry_ref)

        x = xg_ref[:, 0:256]
        lg = xg_ref[:, 256:512]
        prev8 = jnp.where(i == 0, 0.0, xp_ref[:, 0:256])
        u, ra, ia, sp, a, m2, mult = _lru_gates(x, prev8, cw_ref, cb_ref, wa_ref, wx_ref, ba_ref, bx_ref, lam_ref)
        ac, hc = _lin_scan(a, mult * ia * u, False)
        h = hc + ac * carry_ref[0:1, :]
        hl_ref[...] = h
        carry_ref[0:1, :] = hl_ref[r - 1:r, :]
        g, _ = _gelu_and_grad(lg)
        ya_ref[...] = (g * h).astype(ya_ref.dtype)

    small = lambda rows: pl.BlockSpec((rows, 256), lambda j, i: (0, j))
    return pl.pallas_call(
        body, name=name, grid=(4, nt),
        in_specs=[pl.BlockSpec((r, LBLK), lambda j, i: (i, _lblk_col(j))),
                  pl.BlockSpec((8, LBLK), lambda j, i: (jnp.maximum(i * (r // 8) - 1, 0), _lblk_col(j))),
                  small(4), small(1),
                  pl.BlockSpec((1, 256, 256), lambda j, i: (j, 0, 0)), pl.BlockSpec((1, 256, 256), lambda j, i: (j, 0, 0)),
                  small(1), small(1), small(1)] + c_in_specs,
        out_specs=[pl.BlockSpec((r, 256), lambda j, i: (i, j)), pl.BlockSpec((r, 256), lambda j, i: (i, j))] + c_out_specs,
        out_shape=[jax.ShapeDtypeStruct((t, D), F32), jax.ShapeDtypeStruct((t, D), _MXU)] + c_out_shapes,
        scratch_shapes=[pltpu.VMEM((8, 256), F32)] + c_sems,
        compiler_params=_cp(("arbitrary", "arbitrary") if comm is not None else ("parallel", "arbitrary")),
    )(proj, proj, lw["cw"], lw["cb"], lw["wa"], lw["wx"], lw["ba"], lw["bx"], lw["lam"], *c_args)


def _lru_bwd(proj, hl, dya, dproj, lw, *, r, name, comm=None):
    t = proj.shape[0]
    nt = t // r
    c_args, c_in_specs, c_out_specs, c_out_shapes, c_sems = _comm_specs(comm)
    n_in = 13

    def body(*refs):
        xg_ref, xp_ref, hl_ref, hp_ref, dya_ref, cw_ref, cb_ref, wa_ref, wx_ref, ba_ref, bx_ref, lam_ref = refs[:12]
        n_out = n_in + len(c_args)
        dproj_ref, sm_ref, dwa_ref, dwx_ref = refs[n_out:n_out + 4]
        n_scr = n_out + 4 + len(c_out_shapes)
        carry_ref, du8_ref, row_scr = refs[n_scr:n_scr + 3]
        i = pl.program_id(1)
        if comm is not None:
            j = pl.program_id(0)
            _comm_steps(comm, refs, n_in, n_out + 4, (j == 0) & (i == 0), (j == 3) & (i == 0), (j == 3) & (i == nt - 1))

        @pl.when(i == 0)
        def _():
            carry_ref[...] = jnp.zeros_like(carry_ref)
            du8_ref[...] = jnp.zeros_like(du8_ref)
            sm_ref[...] = jnp.zeros_like(sm_ref)
            dwa_ref[...] = jnp.zeros_like(dwa_ref)
            dwx_ref[...] = jnp.zeros_like(dwx_ref)

        tile0 = i == nt - 1
        xp = xg_ref[:, 0:256]
        lg = xg_ref[:, 256:512]
        prev8 = jnp.where(tile0, 0.0, xp_ref[:, 0:256])
        u, ra, ia, sp, a, m2, mult = _lru_gates(xp, prev8, cw_ref, cb_ref, wa_ref, wx_ref, ba_ref, bx_ref, lam_ref)
        h = hl_ref[...]
        hprev = _shift_down(h, jnp.where(tile0, 0.0, hp_ref[...]), 1)
        dya_v = dya_ref[...]
        g, dg = _gelu_and_grad(lg)
        ac, lc = _lin_scan(_shift_up(a, carry_ref[...], 1), dya_v * g, True)
        lam_v = lc + ac * carry_ref[1:2, :]
        row_scr[0:8, :] = lam_v[0:8]
        row_scr[8:16, :] = a[0:8]
        carry_ref[1:2, :] = row_scr[0:1, :]
        carry_ref[0:1, :] = row_scr[8:9, :]
        da = lam_v * hprev
        dmult = lam_v * ia * u
        dia = lam_v * mult * u
        dlog = da * a - dmult * (1.0 - m2) / mult
        dra = -LRU_C * sp * dlog
        dpa = dra * ra * (1.0 - ra)
        dpx = dia * ia * (1.0 - ia)
        du = lam_v * mult * ia + _dot_nt(dpa, wa_ref[0]) + _dot_nt(dpx, wx_ref[0])
        dwa_ref[0] += _dot_tn(u, dpa)
        dwx_ref[0] += _dot_tn(u, dpx)
        dlx = du * cw_ref[3:4, :]
        sm_ref[24:32, :] += _rsum8(du * xp)
        for k in (1, 2, 3):
            du_k = _shift_up(du, du8_ref[...], k)
            dlx = dlx + du_k * cw_ref[3 - k:4 - k, :]
            sm_ref[8 * (3 - k):8 * (4 - k), :] += _rsum8(du_k * xp)
        du8_ref[...] = du[0:8]
        dproj_ref[:, 0:256] = dlx.astype(dproj_ref.dtype)
        dproj_ref[:, 256:512] = (dya_v * h * dg).astype(dproj_ref.dtype)
        sm_ref[32:40, :] += _rsum8(du)
        sm_ref[40:48, :] += _rsum8(dpa)
        sm_ref[48:56, :] += _rsum8(dpx)
        sm_ref[56:64, :] += _rsum8(-LRU_C * ra * dlog) * (-_sigmoid(-lam_ref[...]))

    rev = lambda i: nt - 1 - i
    small = lambda rows: pl.BlockSpec((rows, 256), lambda j, i: (0, j))
    wblk = pl.BlockSpec((1, 256, 256), lambda j, i: (j, 0, 0))
    return pl.pallas_call(
        body, name=name, grid=(4, nt),
        in_specs=[pl.BlockSpec((r, LBLK), lambda j, i: (rev(i), _lblk_col(j))),
                  pl.BlockSpec((8, LBLK), lambda j, i: (jnp.maximum(rev(i) * (r // 8) - 1, 0), _lblk_col(j))),
                  pl.BlockSpec((r, 256), lambda j, i: (rev(i), j)),
                  pl.BlockSpec((8, 256), lambda j, i: (jnp.maximum(rev(i) * (r // 8) - 1, 0), j)),
                  pl.BlockSpec((r, 256), lambda j, i: (rev(i), j)),
                  small(4), small(1), wblk, wblk, small(1), small(1), small(1),
                  pl.BlockSpec(memory_space=pl.ANY)] + c_in_specs,
        out_specs=[pl.BlockSpec((r, LBLK), lambda j, i: (rev(i), _lblk_col(j))),
                   pl.BlockSpec((64, 256), lambda j, i: (0, j)), wblk, wblk] + c_out_specs,
        out_shape=[jax.ShapeDtypeStruct(dproj.shape, dproj.dtype), jax.ShapeDtypeStruct((64, D), F32),
                   jax.ShapeDtypeStruct((4, 256, 256), F32), jax.ShapeDtypeStruct((4, 256, 256), F32)] + c_out_shapes,
        scratch_shapes=[pltpu.VMEM((8, 256), F32), pltpu.VMEM((8, 256), F32), pltpu.VMEM((16, 256), F32)] + c_sems,
        input_output_aliases={n_in - 1: 0},
        compiler_params=_cp(("arbitrary", "arbitrary") if comm is not None else ("parallel", "arbitrary")),
    )(proj, proj, hl, hl, dya, lw["cw"], lw["cb"], lw["wa"], lw["wx"], lw["ba"], lw["bx"], lw["lam"], dproj, *c_args)


def _head_cols(x):
    lane = lax.broadcasted_iota(jnp.int32, x.shape, 1)
    return [jnp.sum(jnp.where(lane == h, x, 0.0), axis=1, keepdims=True) for h in range(N_HEADS)]


def _compact_heads(blocks):
    lane = lax.broadcasted_iota(jnp.int32, blocks[0].shape, 1)
    lo = lane < HEAD_P
    out = jnp.zeros_like(blocks[0])
    for j, blk in enumerate(blocks):
        s_lo = jnp.sum(jnp.where(lo, blk, 0.0), axis=1, keepdims=True)
        s_hi = jnp.sum(jnp.where(lo, 0.0, blk), axis=1, keepdims=True)
        out = jnp.where(lane == 2 * j, s_lo, out)
        out = jnp.where(lane == 2 * j + 1, s_hi, out)
    return out


def _ssd_prelude(dtraw_ref, dtb_ref, alog_ref, dt_scr, a_scr):
    lane = lax.broadcasted_iota(jnp.int32, dt_scr.shape, 1)
    dt = jnp.where(lane < N_HEADS, _softplus(dtraw_ref[...] + dtb_ref[0:1, :]), 0.0)
    dt_scr[...] = dt
    a_scr[...] = dt * (-jnp.exp(alog_ref[0:1, :]))


def _ssd_chunk_scalars(dt_scr, a_scr, r_scr, r0):
    a_c = a_scr[pl.ds(r0, CHUNK), :]
    dt_c = dt_scr[pl.ds(r0, CHUNK), :]
    i0 = lax.broadcasted_iota(jnp.int32, (CHUNK, CHUNK), 0)
    i1 = lax.broadcasted_iota(jnp.int32, (CHUNK, CHUNK), 1)
    tri = jnp.where(i0 >= i1, 1.0, 0.0).astype(F32)
    cs = jnp.dot(tri, a_c, precision=_HI, preferred_element_type=F32)
    lane = lax.broadcasted_iota(jnp.int32, (CHUNK, 128), 1)
    srow = lax.broadcasted_iota(jnp.int32, (CHUNK, 128), 0)
    t_lo = jnp.where((lane < HEAD_P) & (srow <= lane), 1.0, 0.0).astype(F32)
    t_hi = jnp.where((lane >= HEAD_P) & (srow <= lane - HEAD_P), 1.0, 0.0).astype(F32)
    even = (lane % 2) == 0
    tn = (((0,), (0,)), ((), ()))
    r_scr[...] = (lax.dot_general(jnp.where(even, a_c, 0.0), t_lo, tn, precision=_HI, preferred_element_type=F32)
                  + lax.dot_general(jnp.where(even, 0.0, a_c), t_hi, tn, precision=_HI, preferred_element_type=F32))
    return cs, dt_c, _head_cols(cs), _head_cols(dt_c)


def _block_diag2(v):
    lo = lax.broadcasted_iota(jnp.int32, v.shape, 1) < HEAD_P
    return jnp.concatenate([jnp.where(lo, v, 0.0), jnp.where(lo, 0.0, v)], axis=0).astype(_MXU)


def _ssd_pair(xc_scr, r_scr, cs_cols, dt_cols, s2, r0, j, s2t=None):
    lane = lax.broadcasted_iota(jnp.int32, (CHUNK, 128), 1)
    srow = lax.broadcasted_iota(jnp.int32, (CHUNK, 128), 0)
    lo = lane < HEAD_P
    csc = jnp.where(lo, cs_cols[2 * j], cs_cols[2 * j + 1])
    dtc = jnp.where(lo, dt_cols[2 * j], dt_cols[2 * j + 1])
    csr = r_scr[2 * j:2 * j + 1, :] + r_scr[2 * j + 1:2 * j + 2, :]
    dm = jnp.where((lane & (HEAD_P - 1)) <= srow, jnp.exp(jnp.minimum(csc - csr, 0.0)), 0.0)
    xs = xc_scr[pl.ds(r0, CHUNK), j * 128:(j + 1) * 128]
    xd = xs * dtc
    csl = jnp.sum(jnp.where(srow == CHUNK - 1, csc, 0.0), axis=0, keepdims=True)
    out = dict(csc=csc, dtc=dtc, dm=dm, m2=s2 * dm, xs=xs, xd=xd, rhs=_block_diag2(xd), e=jnp.exp(csc),
               w=jnp.exp(csl - csc), dec=jnp.exp(csl))
    if s2t is not None:
        out["mt2"] = s2t * jnp.where((lane & (HEAD_P - 1)) >= srow, jnp.exp(jnp.minimum(csr - csc, 0.0)), 0.0)
    return out


def _cat(parts):
    return jnp.concatenate(parts, axis=1)


def _ssd_fwd(proj, dtraw, sw, *, rb, name, comm=None):
    t = proj.shape[0]
    ns, cb = t // rb, rb // CHUNK
    c_args, c_in_specs, c_out_specs, c_out_shapes, c_sems = _comm_specs(comm)

    def body(*refs):
        zx_ref, zp_ref, dtraw_ref, cw_ref, cbias_ref, dtb_ref, alog_ref, dsk_ref, ng_ref = refs[:9]
        yssd_ref, yb_ref, st_ref = refs[9 + len(c_args):12 + len(c_args)]
        n_scr = 12 + len(c_args) + len(c_out_shapes)
        h_scr, xc_scr, dt_scr, a_scr, r_scr = refs[n_scr:n_scr + 5]
        i = pl.program_id(0)
        if comm is not None:
            _comm_steps(comm, refs, 9, 12 + len(c_args), i == 0, i == (3 * ns) // 4, i == ns - 1)

        @pl.when(i == 0)
        def _():
            h_scr[...] = jnp.zeros_like(h_scr)

        for j in range(XBC // 128):
            cs_, zc = slice(128 * j, 128 * (j + 1)), slice(2048 + 128 * j, 2048 + 128 * (j + 1))
            pre = _conv4(zx_ref[:, zc], jnp.where(i == 0, 0.0, zp_ref[:, zc]), cw_ref, cbias_ref, cs_)
            xc_scr[:, cs_] = pre * _sigmoid(pre)
        _ssd_prelude(dtraw_ref, dtb_ref, alog_ref, dt_scr, a_scr)

        def chunk(c, carry):
            r0 = pl.multiple_of(c * CHUNK, CHUNK)
            _, _, cs_cols, dt_cols = _ssd_chunk_scalars(dt_scr, a_scr, r_scr, r0)
            st_ref[c] = h_scr[...]
            for g in range(N_GROUPS):
                bg = xc_scr[pl.ds(r0, CHUNK), 2048 + 128 * g:2048 + 128 * (g + 1)]
                cg = xc_scr[pl.ds(r0, CHUNK), 2560 + 128 * g:2560 + 128 * (g + 1)]
                s2 = _dot_nt(cg, jnp.concatenate([bg, bg], axis=0))
                hp = h_scr[:, 512 * g:512 * (g + 1)]
                yoff = _dot(cg, hp)
                xdw, dec = [], []
                for jj in range(4):
                    j = 4 * g + jj
                    p = _ssd_pair(xc_scr, r_scr, cs_cols, dt_cols, s2, r0, j)
                    y = _dot(p["m2"], p["rhs"]) + yoff[:, 128 * jj:128 * (jj + 1)] * p["e"]
                    yssd_ref[pl.ds(r0, CHUNK), 128 * j:128 * (j + 1)] = y + dsk_ref[0:1, 128 * j:128 * (j + 1)] * p["xs"]
                    xdw.append(p["xd"] * p["w"])
                    dec.append(p["dec"])
                h_scr[:, 512 * g:512 * (g + 1)] = hp * _cat(dec) + _dot_tn(bg, _cat(xdw))
            return carry

        lax.fori_loop(0, cb, chunk, 0)
        for g in range(N_GROUPS):
            sl = slice(512 * g, 512 * (g + 1))
            for q in range(rb // NORM_ROWS):
                rw = slice(NORM_ROWS * q, NORM_ROWS * (q + 1))
                yz = yssd_ref[rw, sl] * _silu(zx_ref[rw, sl])
                rg = lax.rsqrt(jnp.mean(yz * yz, axis=-1, keepdims=True) + EPS)
                yb_ref[rw, sl] = (yz * rg * ng_ref[0:1, sl]).astype(yb_ref.dtype)

    full = lambda rows, cols: pl.BlockSpec((rows, cols), lambda i: (0, 0))
    return pl.pallas_call(
        body, name=name, grid=(ns,),
        in_specs=[pl.BlockSpec((rb, ZX_W), lambda i: (i, 0)),
                  pl.BlockSpec((8, ZX_W), lambda i: (jnp.maximum(i * (rb // 8) - 1, 0), 0)),
                  pl.BlockSpec((rb, DT_PAD), lambda i: (i, 0)),
                  full(4, XBC), full(1, XBC), full(1, DT_PAD), full(1, DT_PAD), full(1, SSD_INNER), full(1, SSD_INNER)]
        + c_in_specs,
        out_specs=[pl.BlockSpec((rb, SSD_INNER), lambda i: (i, 0)), pl.BlockSpec((rb, SSD_INNER), lambda i: (i, 0)),
                   pl.BlockSpec((cb, N_STATE, SSD_INNER), lambda i: (i, 0, 0))] + c_out_specs,
        out_shape=[jax.ShapeDtypeStruct((t, SSD_INNER), F32), jax.ShapeDtypeStruct((t, SSD_INNER), _MXU),
                   jax.ShapeDtypeStruct((t // CHUNK, N_STATE, SSD_INNER), F32)] + c_out_shapes,
        scratch_shapes=[pltpu.VMEM((N_STATE, SSD_INNER), F32), pltpu.VMEM((rb, XBC), F32), pltpu.VMEM((rb, DT_PAD), F32),
                        pltpu.VMEM((rb, DT_PAD), F32), pltpu.VMEM((128, 128), F32)] + c_sems,
        compiler_params=_cp(("arbitrary",)),
    )(proj, proj, dtraw, sw["cw"], sw["cb"], sw["dtb"], sw["alog"], sw["dsk"], sw["ng"], *c_args)


def _ssd_bwd(proj, dtraw, yssd, states, dyb, dproj, sw, *, rb, name, comm=None):
    t = proj.shape[0]
    ns, cb = t // rb, rb // CHUNK
    c_args, c_in_specs, c_out_specs, c_out_shapes, c_sems = _comm_specs(comm)
    n_in = 13

    def body(*refs):
        zx_ref, zp_ref, dtraw_ref, yssd_ref, st_ref, dyb_ref, cw_ref, cbias_ref, dtb_ref, alog_ref, dsk_ref, ng_ref = refs[:12]
        n_out = n_in + len(c_args)
        dzx_ref, ddt_ref, gconv_ref, gch_ref, ghd_ref = refs[n_out:n_out + 5]
        n_scr = n_out + 5 + len(c_out_shapes)
        dht_scr, xc_scr, dsl_scr, dy_scr, dxc_scr, dt_scr, a_scr, r_scr, dp8_scr = refs[n_scr:n_scr + 9]
        i = pl.program_id(0)
        if comm is not None:
            _comm_steps(comm, refs, n_in, n_out + 5, i == 0, i == (3 * ns) // 4, i == ns - 1)

        @pl.when(i == 0)
        def _():
            dht_scr[...] = jnp.zeros_like(dht_scr)
            dp8_scr[...] = jnp.zeros_like(dp8_scr)
            gconv_ref[...] = jnp.zeros_like(gconv_ref)
            gch_ref[...] = jnp.zeros_like(gch_ref)
            ghd_ref[...] = jnp.zeros_like(ghd_ref)

        tile0 = i == ns - 1
        for j in range(XBC // 128):
            cs_, zc = slice(128 * j, 128 * (j + 1)), slice(2048 + 128 * j, 2048 + 128 * (j + 1))
            pre = _conv4(zx_ref[:, zc], jnp.where(tile0, 0.0, zp_ref[:, zc]), cw_ref, cbias_ref, cs_)
            sg = _sigmoid(pre)
            xc_scr[:, cs_] = pre * sg
            dsl_scr[:, cs_] = sg * (1.0 + pre * (1.0 - sg))
        _ssd_prelude(dtraw_ref, dtb_ref, alog_ref, dt_scr, a_scr)

        for g in range(N_GROUPS):
            sl = slice(512 * g, 512 * (g + 1))
            for q in range(rb // NORM_ROWS):
                rw = slice(NORM_ROWS * q, NORM_ROWS * (q + 1))
                zv = zx_ref[rw, sl]
                ys = yssd_ref[rw, sl]
                sg = _sigmoid(zv)
                sz = zv * sg
                yz = ys * sz
                rg = lax.rsqrt(jnp.mean(yz * yz, axis=-1, keepdims=True) + EPS)
                yn = yz * rg
                dyb_v = dyb_ref[rw, sl]
                gch_ref[0:8, sl] += _rsum8(dyb_v * yn)
                dyn = dyb_v * ng_ref[0:1, sl]
                dyz = rg * (dyn - yn * jnp.mean(dyn * yn, axis=-1, keepdims=True))
                dy_scr[rw, sl] = dyz * sz
                dzx_ref[rw, sl] = (dyz * ys * (sg * (1.0 + zv * (1.0 - sg)))).astype(dzx_ref.dtype)

        a_row = -jnp.exp(alog_ref[0:1, :])

        def chunk(cc, carry):
            c = cb - 1 - cc
            r0 = pl.multiple_of(c * CHUNK, CHUNK)
            rows = pl.ds(r0, CHUNK)
            _, dt_c, cs_cols, dt_cols = _ssd_chunk_scalars(dt_scr, a_scr, r_scr, r0)
            lane = lax.broadcasted_iota(jnp.int32, (CHUNK, 128), 1)
            srow = lax.broadcasted_iota(jnp.int32, (CHUNK, 128), 0)
            lo = lane < HEAD_P
            last = srow == CHUNK - 1
            p1_blocks, p3_blocks = [], []
            for g in range(N_GROUPS):
                gs = slice(512 * g, 512 * (g + 1))
                bg = xc_scr[rows, 2048 + 128 * g:2048 + 128 * (g + 1)]
                cg = xc_scr[rows, 2560 + 128 * g:2560 + 128 * (g + 1)]
                b2 = jnp.concatenate([bg, bg], axis=0)
                s2 = _dot_nt(cg, b2)
                s2t = _dot_nt(bg, jnp.concatenate([cg, cg], axis=0))
                hp = st_ref[c, :, gs]
                dht = dht_scr[:, gs]
                yoff = _dot(cg, hp)
                ps = [_ssd_pair(xc_scr, r_scr, cs_cols, dt_cols, s2, r0, 4 * g + jj, s2t) for jj in range(4)]
                dys = [dy_scr[rows, 128 * (4 * g + jj):128 * (4 * g + jj + 1)] for jj in range(4)]
                dye = _cat([dys[jj] * ps[jj]["e"] for jj in range(4)])
                w_g = _cat([p["w"] for p in ps])
                dcg = _dot_nt(dye, hp)
                dht_scr[:, gs] = _dot_tn(cg, dye) + _cat([p["dec"] for p in ps]) * dht
                dxd_state = w_g * _dot(bg, dht)
                dbg = _dot_nt(_cat([p["xd"] for p in ps]) * w_g, dht)
                tsum = _rsum(dht * hp)
                ds2 = jnp.zeros((CHUNK, 128), F32)
                for jj in range(4):
                    j = 4 * g + jj
                    ls = slice(128 * j, 128 * (j + 1))
                    p, dy2 = ps[jj], dys[jj]
                    dy_bd = _block_diag2(dy2)
                    dm2 = _dot_nt(dy2, p["rhs"])
                    ds2 = ds2 + dm2 * p["dm"]
                    gdiff = dm2 * p["m2"] - _dot_nt(p["xd"], dy_bd) * p["mt2"]
                    dxs = dxd_state[:, 128 * jj:128 * (jj + 1)]
                    dxd = _dot(p["mt2"], dy_bd) + dxs
                    end_row = _rsum(p["xd"] * dxs) + p["dec"] * tsum[:, 128 * jj:128 * (jj + 1)]
                    p1_blocks.append(gdiff + dy2 * yoff[:, 128 * jj:128 * (jj + 1)] * p["e"] - p["xd"] * dxs
                                     + jnp.where(last, end_row, 0.0))
                    p3_blocks.append(dxd * p["xs"])
                    dxc_scr[rows, ls] = dxd * p["dtc"] + dy2 * dsk_ref[0:1, ls]
                    gch_ref[8:16, ls] += _rsum8(dy2 * p["xs"])
                dcg = dcg + _dot(ds2, b2)
                rb2 = _dot_tn(ds2, cg)
                dxc_scr[rows, 2048 + 128 * g:2048 + 128 * (g + 1)] = dbg + rb2[0:CHUNK] + rb2[CHUNK:2 * CHUNK]
                dxc_scr[rows, 2560 + 128 * g:2560 + 128 * (g + 1)] = dcg
            dcs = _compact_heads(p1_blocks)
            i0 = lax.broadcasted_iota(jnp.int32, (CHUNK, CHUNK), 0)
            i1 = lax.broadcasted_iota(jnp.int32, (CHUNK, CHUNK), 1)
            triu = jnp.where(i1 >= i0, 1.0, 0.0).astype(F32)
            da = jnp.dot(triu, dcs, precision=_HI, preferred_element_type=F32)
            ddt = _compact_heads(p3_blocks) + da * a_row
            ddtraw = jnp.where(lane < N_HEADS, ddt * _sigmoid(dtraw_ref[rows, :] + dtb_ref[0:1, :]), 0.0)
            ddt_ref[rows, :] = ddtraw.astype(ddt_ref.dtype)
            ghd_ref[0:1, :] += _rsum(ddtraw)
            ghd_ref[1:2, :] += _rsum(da * dt_c) * a_row
            return carry

        lax.fori_loop(0, cb, chunk, 0)
        for j in range(XBC // 128):
            cs_, zc = slice(128 * j, 128 * (j + 1)), slice(2048 + 128 * j, 2048 + 128 * (j + 1))
            dpre = dxc_scr[:, cs_] * dsl_scr[:, cs_]
            xraw = zx_ref[:, zc]
            dx = dpre * cw_ref[3:4, cs_]
            gconv_ref[24:32, cs_] += _rsum8(dpre * xraw)
            for k in (1, 2, 3):
                dpre_k = _shift_up(dpre, dp8_scr[:, cs_], k)
                dx = dx + dpre_k * cw_ref[3 - k:4 - k, cs_]
                gconv_ref[8 * (3 - k):8 * (4 - k), cs_] += _rsum8(dpre_k * xraw)
            dzx_ref[:, zc] = dx.astype(dzx_ref.dtype)
            dp8_scr[:, cs_] = dpre[0:8]
            gconv_ref[32:40, cs_] += _rsum8(dpre)

    rev = lambda i: ns - 1 - i
    full = lambda rows, cols: pl.BlockSpec((rows, cols), lambda i: (0, 0))
    return pl.pallas_call(
        body, name=name, grid=(ns,),
        in_specs=[pl.BlockSpec((rb, ZX_W), lambda i: (rev(i), 0)),
                  pl.BlockSpec((8, ZX_W), lambda i: (jnp.maximum(rev(i) * (rb // 8) - 1, 0), 0)),
                  pl.BlockSpec((rb, DT_PAD), lambda i: (rev(i), 0)),
                  pl.BlockSpec((rb, SSD_INNER), lambda i: (rev(i), 0)),
                  pl.BlockSpec((cb, N_STATE, SSD_INNER), lambda i: (rev(i), 0, 0)),
                  pl.BlockSpec((rb, SSD_INNER), lambda i: (rev(i), 0)),
                  full(4, XBC), full(1, XBC), full(1, DT_PAD), full(1, DT_PAD), full(1, SSD_INNER), full(1, SSD_INNER),
                  pl.BlockSpec(memory_space=pl.ANY)] + c_in_specs,
        out_specs=[pl.BlockSpec((rb, ZX_W), lambda i: (rev(i), 0)), pl.BlockSpec((rb, DT_PAD), lambda i: (rev(i), 0)),
                   full(40, XBC), full(16, SSD_INNER), full(8, DT_PAD)] + c_out_specs,
        out_shape=[jax.ShapeDtypeStruct(dproj.shape, dproj.dtype), jax.ShapeDtypeStruct((t, DT_PAD), _MXU),
                   jax.ShapeDtypeStruct((40, XBC), F32), jax.ShapeDtypeStruct((16, SSD_INNER), F32),
                   jax.ShapeDtypeStruct((8, DT_PAD), F32)] + c_out_shapes,
        scratch_shapes=[pltpu.VMEM((N_STATE, SSD_INNER), F32), pltpu.VMEM((rb, XBC), F32), pltpu.VMEM((rb, XBC), F32),
                        pltpu.VMEM((rb, SSD_INNER), F32), pltpu.VMEM((rb, XBC), F32), pltpu.VMEM((rb, DT_PAD), F32),
                        pltpu.VMEM((rb, DT_PAD), F32), pltpu.VMEM((128, 128), F32), pltpu.VMEM((8, XBC), F32)] + c_sems,
        input_output_aliases={n_in - 1: 0},
        compiler_params=_cp(("arbitrary",)),
    )(proj, proj, dtraw, yssd, states, dyb, sw["cw"], sw["cb"], sw["dtb"], sw["alog"], sw["dsk"], sw["ng"], dproj, *c_args)


def _branch_merge(ya, yb, proj, wba, wbb, bgate, *, tm, tn, name):
    t = ya.shape[0]
    nj = D // tn

    def body(ya_ref, yb_ref, ga_ref, gb_ref, wba_ref, wbb_ref, ba_ref, bb_ref, ta_ref, tb_ref, mg_ref):
        ta = _dot(ya_ref[...], wba_ref[...])
        tb = _dot(yb_ref[...], wbb_ref[...])
        ta_ref[...] = ta.astype(ta_ref.dtype)
        tb_ref[...] = tb.astype(tb_ref.dtype)
        ga = _sigmoid(ga_ref[...] + ba_ref[...])
        gb = _sigmoid(gb_ref[...] + bb_ref[...])
        mg_ref[...] = (ga * ta + gb * tb).astype(mg_ref.dtype)

    tile = pl.BlockSpec((tm, tn), lambda i, j: (i, j))
    return pl.pallas_call(
        body, name=name, grid=(t // tm, nj),
        in_specs=[pl.BlockSpec((tm, D), lambda i, j: (i, 0)), pl.BlockSpec((tm, SSD_INNER), lambda i, j: (i, 0)),
                  pl.BlockSpec((tm, tn), lambda i, j: (i, G0 // tn + j)),
                  pl.BlockSpec((tm, tn), lambda i, j: (i, (G0 + D) // tn + j)),
                  pl.BlockSpec((D, tn), lambda i, j: (0, j)), pl.BlockSpec((SSD_INNER, tn), lambda i, j: (0, j)),
                  pl.BlockSpec((1, tn), lambda i, j: (0, j)), pl.BlockSpec((1, tn), lambda i, j: (0, nj + j))],
        out_specs=[tile, tile, tile],
        out_shape=[jax.ShapeDtypeStruct((t, D), _MXU)] * 3,
        compiler_params=_cp(("parallel", "parallel")),
    )(ya, yb, proj, proj, wba, wbb, bgate, bgate)


def _swiglu_mm(gu, wfo, residual, *, tm, tn, name):
    t = gu.shape[0]

    def body(gu_ref, w_ref, r_ref, act_ref, o_ref):
        @pl.when(pl.program_id(1) == 0)
        def _():
            gate = gu_ref[:, 0:D_FF].astype(F32)
            act_ref[...] = (_silu(gate) * gu_ref[:, D_FF:2 * D_FF].astype(F32)).astype(act_ref.dtype)
        o_ref[...] = jnp.dot(act_ref[...], w_ref[...], preferred_element_type=F32) + r_ref[...]

    return pl.pallas_call(
        body, name=name, grid=(t // tm, D // tn),
        in_specs=[pl.BlockSpec((tm, 2 * D_FF), lambda i, j: (i, 0)), pl.BlockSpec((D_FF, tn), lambda i, j: (0, j)),
                  pl.BlockSpec((tm, tn), lambda i, j: (i, j))],
        out_specs=[pl.BlockSpec((tm, D_FF), lambda i, j: (i, 0)), pl.BlockSpec((tm, tn), lambda i, j: (i, j))],
        out_shape=[jax.ShapeDtypeStruct((t, D_FF), _MXU), jax.ShapeDtypeStruct((t, D), F32)],
        compiler_params=_cp(("parallel", "arbitrary")),
    )(gu, wfo, residual)


def _ffn_bwd_act(dh, wfo, gu, *, tm, name, comm=None):
    t = dh.shape[0]
    ni = t // tm
    c_args, c_in_specs, c_out_specs, c_out_shapes, c_sems = _comm_specs(comm)

    def body(*refs):
        dh_ref, w_ref, gu_ref = refs[:3]
        o_ref = refs[3 + len(c_args)]
        if comm is not None:
            i = pl.program_id(0)
            _comm_steps(comm, refs, 3, 4 + len(c_args), i == 0, i == (3 * ni) // 4, i == ni - 1)
        dact = _dot_nt(dh_ref[...], w_ref[...])
        g = gu_ref[:, 0:D_FF].astype(F32)
        u = gu_ref[:, D_FF:2 * D_FF].astype(F32)
        sg = _sigmoid(g)
        o_ref[:, 0:D_FF] = (dact * u * (sg * (1.0 + g * (1.0 - sg)))).astype(o_ref.dtype)
        o_ref[:, D_FF:2 * D_FF] = (dact * (g * sg)).astype(o_ref.dtype)

    return pl.pallas_call(
        body, name=name, grid=(ni,),
        in_specs=[pl.BlockSpec((tm, D), lambda i: (i, 0)), pl.BlockSpec((D_FF, D), lambda i: (0, 0)),
                  pl.BlockSpec((tm, 2 * D_FF), lambda i: (i, 0))] + c_in_specs,
        out_specs=[pl.BlockSpec((tm, 2 * D_FF), lambda i: (i, 0))] + c_out_specs,
        out_shape=[jax.ShapeDtypeStruct((t, 2 * D_FF), _MXU)] + c_out_shapes,
        scratch_shapes=c_sems,
        compiler_params=_cp(("arbitrary",) if comm is not None else ("parallel",)),
    )(dh, wfo, gu, *c_args)


def _outproj_bwd(dh, wout, ta, tb, proj, bgate, dproj, *, tm, name):
    t = dh.shape[0]

    def body(dh_ref, w_ref, ta_ref, tb_ref, g_ref, b_ref, dta_ref, dtb_ref, dg_ref, db_ref):
        @pl.when(pl.program_id(0) == 0)
        def _():
            db_ref[...] = jnp.zeros_like(db_ref)
        dm = _dot_nt(dh_ref[...], w_ref[...])
        ga = _sigmoid(g_ref[:, 0:D] + b_ref[:, 0:D])
        gb = _sigmoid(g_ref[:, D:2 * D] + b_ref[:, D:2 * D])
        dta_ref[...] = (dm * ga).astype(dta_ref.dtype)
        dtb_ref[...] = (dm * gb).astype(dtb_ref.dtype)
        dga = dm * ta_ref[...].astype(F32) * ga * (1.0 - ga)
        dgb = dm * tb_ref[...].astype(F32) * gb * (1.0 - gb)
        dg_ref[:, 0:D] = dga.astype(dg_ref.dtype)
        dg_ref[:, D:2 * D] = dgb.astype(dg_ref.dtype)
        db_ref[0:1, 0:D] += _rsum(dga)
        db_ref[0:1, D:2 * D] += _rsum(dgb)

    row = lambda cols: pl.BlockSpec((tm, cols), lambda i: (i, 0))
    return pl.pallas_call(
        body, name=name, grid=(t // tm,),
        in_specs=[row(D), pl.BlockSpec((D, D), lambda i: (0, 0)), row(D), row(D),
                  pl.BlockSpec((tm, 2 * D), lambda i: (i, G0 // (2 * D))), pl.BlockSpec((1, 2 * D), lambda i: (0, 0))],
        out_specs=[row(D), row(D), pl.BlockSpec((tm, 2 * D), lambda i: (i, G0 // (2 * D))),
                   pl.BlockSpec((8, 2 * D), lambda i: (0, 0))],
        out_shape=[jax.ShapeDtypeStruct((t, D), _MXU), jax.ShapeDtypeStruct((t, D), _MXU),
                   jax.ShapeDtypeStruct(dproj, _MXU), jax.ShapeDtypeStruct((8, 2 * D), F32)],
        compiler_params=_cp(("arbitrary",)),
    )(dh, wout, ta, tb, proj, bgate)


def _loss_head(h, gf, target, *, tm, name):
    t = h.shape[0]

    def body(h_ref, g_ref, t_ref, loss_ref, dg_ref, dh_ref):
        @pl.when(pl.program_id(0) == 0)
        def _():
            loss_ref[...] = jnp.zeros_like(loss_ref)
            dg_ref[...] = jnp.zeros_like(dg_ref)
        x = h_ref[...]
        r = lax.rsqrt(jnp.mean(x * x, axis=-1, keepdims=True) + EPS)
        xh = x * r
        err = xh * g_ref[...] - t_ref[...]
        loss_ref[...] += 0.5 * jnp.sum(jnp.mean(err * err, axis=-1, keepdims=True), axis=0, keepdims=True)
        dy = err * (1.0 / D)
        dg_ref[0:1, :] += _rsum(dy * xh)
        dxh = dy * g_ref[...]
        dh_ref[...] = r * (dxh - xh * jnp.mean(dxh * xh, axis=-1, keepdims=True))

    row = pl.BlockSpec((tm, D), lambda i: (i, 0))
    return pl.pallas_call(
        body, name=name, grid=(t // tm,),
        in_specs=[row, pl.BlockSpec((1, D), lambda i: (0, 0)), row],
        out_specs=[pl.BlockSpec((8, 128), lambda i: (0, 0)), pl.BlockSpec((8, D), lambda i: (0, 0)), row],
        out_shape=[jax.ShapeDtypeStruct((8, 128), F32), jax.ShapeDtypeStruct((8, D), F32), jax.ShapeDtypeStruct((t, D), F32)],
        compiler_params=_cp(("arbitrary",)),
    )(h, gf, target)


def _row_tile(rows, cols, limit_bytes=1 << 20):
    best = None
    for tr in range(8, rows + 1, 8):
        if rows % tr == 0 and tr * cols * 4 <= limit_bytes:
            best = tr
    return best if best is not None else rows


def _adamw(w, g, m, v, *, name):
    rows, cols = w.shape
    tr = _row_tile(rows, cols)

    def body(w_ref, g_ref, m_ref, v_ref, d_ref, nm_ref, nv_ref):
        gv = g_ref[...]
        nm = ADAM_B1 * m_ref[...] + (1.0 - ADAM_B1) * gv
        nv = ADAM_B2 * v_ref[...] + (1.0 - ADAM_B2) * (gv * gv)
        m_hat = nm / (1.0 - ADAM_B1 ** ADAM_STEP)
        v_hat = nv / (1.0 - ADAM_B2 ** ADAM_STEP)
        d_ref[...] = -ADAM_LR * (m_hat / (jnp.sqrt(v_hat) + ADAM_EPS) + ADAM_WD * w_ref[...])
        nm_ref[...] = nm
        nv_ref[...] = nv

    blk = pl.BlockSpec((tr, cols), lambda i: (i, 0))
    shp = jax.ShapeDtypeStruct((rows, cols), F32)
    return pl.pallas_call(
        body, name=name, grid=(rows // tr,), in_specs=[blk] * 4, out_specs=[blk] * 3, out_shape=[shp] * 3,
        compiler_params=_cp(("parallel",)),
    )(w, g, m, v)


def _bd256(w):
    w4 = w.reshape(4, 4, 64, 64)
    eye = jnp.eye(4, dtype=w.dtype)
    return (w4[:, :, :, None, :] * eye[None, :, None, :, None]).reshape(4, 256, 256)


def _bd256_diag(g):
    g5 = g.reshape(4, 4, 64, 4, 64)
    return jnp.stack([g5[:, a, :, a, :] for a in range(4)], axis=1).reshape(16, 64, 64)


FFN_SHARD = 2 * D_FF // 4
W_IN_SHARD = IN_DIM // 4
W_IN_ROWS = 9344


def _w_in_cols(shards, c0, c1):
    out = []
    for p in range(4):
        lo, hi = max(c0, W_IN_SHARD * p), min(c1, W_IN_SHARD * (p + 1))
        if lo < hi:
            out.append(shards[p][:, lo - W_IN_SHARD * p:hi - W_IN_SHARD * p])
    return out


def _in_proj_weights(win):
    lblk = [_w_in_cols(win, 256 * j, 256 * (j + 1)) + _w_in_cols(win, D + 256 * j, D + 256 * (j + 1)) for j in range(4)]
    wp = jnp.concatenate(_w_in_cols(win, 2048, 4096) + _w_in_cols(win, 4096, 7168) + lblk[0] + lblk[1]
                         + _w_in_cols(win, 7200, 9248) + lblk[2] + lblk[3], axis=1)
    wdt = jnp.pad(jnp.concatenate(_w_in_cols(win, 7168, 7200), axis=1), ((0, 0), (0, DT_PAD - N_HEADS)))
    return wp, wdt


def _layer_weights(w, conv, small, l, wp, wdt):
    row = lambda v: v.reshape(1, -1)
    pad_h = lambda v: jnp.pad(v.reshape(1, -1), ((0, 0), (0, DT_PAD - N_HEADS)))
    lw = dict(cw=conv["lru_conv_w"][l], cb=row(small["lru_conv_b"][l]),
              wa=_bd256(small["lru_w_a"][l]).astype(_MXU), wx=_bd256(small["lru_w_x"][l]).astype(_MXU),
              ba=row(small["lru_b_a"][l]), bx=row(small["lru_b_x"][l]), lam=row(small["lru_lambda"][l]))
    sw = dict(cw=conv["ssd_conv_w"][l], cb=row(small["ssd_conv_b"][l]), dtb=pad_h(small["ssd_dt_bias"][l]),
              alog=pad_h(small["ssd_A_log"][l]), dsk=row(jnp.repeat(small["ssd_D"][l], HEAD_P)),
              ng=row(small["ssd_norm_g"][l]))
    return dict(wp=wp, wdt=wdt, lw=lw, sw=sw, wba=w["w_branch"][0:D], wbb=w["w_branch"][D:3 * D],
                wout=w["w_out"], wfi=w["w_ffn_in"], wfo=w["w_ffn_out"],
                g1=row(small["norm1_g"][l]), g2=row(small["norm2_g"][l]), bgate=row(small["b_gate"][l]))


def _tiles(t):
    return dict(tmn=min(1024, t), tm=min(512, t), tm2=min(256, t), r=min(256, t), rb=min(128, t))


def _layer_fwd(h, w, conv, small, l, carried=None):
    tl = _tiles(h.shape[0])
    n = f"l{l}_"
    carried = carried or {}
    arrived = []

    def carry(kernel, key, n_main, *args, **kw):
        comm, finish = carried.get(key, (None, None))
        outs = list(kernel(*args, comm=comm, **kw))
        if comm is not None:
            arrived.append(finish(outs[n_main:]))
        return outs[:n_main]

    wp, wdt = _in_proj_weights(w["w_in"])
    xn, proj = carry(_norm_mm, "in_proj", 2, h, small["norm1_g"][l].reshape(1, -1), wp, tm=tl["tmn"], tn=1024,
                     name=n + "in_proj")
    w = dict(w)
    for layer, ws in arrived:
        if layer == l:
            w.update(ws)
    lwt = _layer_weights(w, conv, small, l, wp, wdt)
    dtraw = _mm_nn(xn, lwt["wdt"], tm=tl["tm"], tn=DT_PAD, name=n + "dt_proj")
    hl, ya = carry(_lru_fwd, "lru", 2, proj, lwt["lw"], r=tl["r"], name=n + "lru_fwd")
    yssd, yb, states = carry(_ssd_fwd, "ssd", 3, proj, dtraw, lwt["sw"], rb=tl["rb"], name=n + "ssd_fwd")
    ta, tb, merged = _branch_merge(ya, yb, proj, lwt["wba"], lwt["wbb"], lwt["bgate"], tm=tl["tm"], tn=512, name=n + "merge")
    hmid = _mm_nn(merged, lwt["wout"], tm=tl["tm"], tn=512, name=n + "out_proj", residual=h)
    xn2, gu = _norm_mm(hmid, lwt["g2"], lwt["wfi"], tm=tl["tmn"], tn=FFN_SHARD, name=n + "ffn_in", out_dtype=_MXU)
    act, hout = _swiglu_mm(gu, lwt["wfo"], hmid, tm=tl["tm"], tn=512, name=n + "ffn_out")
    saved = dict(h=h, xn=xn, proj=proj, dtraw=dtraw, hl=hl, ya=ya, yssd=yssd, yb=yb, states=states, ta=ta, tb=tb,
                 merged=merged, hmid=hmid, xn2=xn2, gu=gu, act=act)
    return hout, saved, lwt, [x for x in arrived if x[0] != l]


def _layer_bwd(dh, s, lwt, l, hooks=None):
    t = dh.shape[0]
    tl = _tiles(t)
    n = f"l{l}_"
    tt = tl["tm"]
    big = {}
    hooks = hooks or {}

    def wgrad(key, a, b, name, **kw):
        big[key] = _wgrad(a, b, tt=tt, name=n + name, into=big.get(key), **kw)

    dgu, = _ffn_bwd_act(dh, lwt["wfo"], s["gu"], tm=tl["tm2"], name=n + "ffn_act_bwd")
    wgrad("w_ffn_out", s["act"], dh, "ffn_out_wgrad", ta=D_FF, tn=1024, out_shape=(D_FF, D),
          out_block=(D_FF, 1024), out_index=lambda o, j: (o, j))
    wgrad("w_ffn_in", s["xn2"], dgu, "ffn_in_wgrad", ta=D, tn=FFN_SHARD, out_shape=(4, D, FFN_SHARD),
          out_block=(None, D, FFN_SHARD), out_index=lambda o, j: (j, o, 0))
    dh1, dg2 = _mm_nt_rmsbwd(dgu, lwt["wfi"], s["hmid"], lwt["g2"], dh, tm=tl["tm"], tk=FFN_SHARD, name=n + "ffn_in_dgrad")
    dta, dtb, dproj, dbg = _outproj_bwd(dh1, lwt["wout"], s["ta"], s["tb"], s["proj"], lwt["bgate"], (t, NP),
                                        tm=tl["tm2"], name=n + "out_proj_bwd")
    rows_d = dict(ta=D, tn=512, out_block=(D, 512), out_index=lambda o, j: (o, j))
    wgrad("w_out", s["merged"], dh1, "out_proj_wgrad", out_shape=(D, D), **rows_d)
    dya = _mm_nt(dta, lwt["wba"], tm=tl["tm"], name=n + "branch_a_dgrad")
    dyb = _mm_nt(dtb, lwt["wbb"], tm=tl["tm"], name=n + "branch_b_dgrad")
    wgrad("w_branch", s["ya"], dta, "branch_a_wgrad", out_shape=(3 * D, D), a_tab=[0], o_tab=[0], **rows_d)
    wgrad("w_branch", s["yb"], dtb, "branch_b_wgrad", out_shape=(3 * D, D), a_tab=[0, 1], o_tab=[1, 2], **rows_d)
    dproj, lsm, dwa, dwx, *got_1 = _lru_bwd(s["proj"], s["hl"], dya, dproj, lwt["lw"], r=tl["r"], name=n + "lru_bwd",
                                            comm=hooks.get("lru"))
    comm_2 = hooks["ssd"](got_1) if "ssd" in hooks else None
    dproj, ddt, gconv, gch, ghd, *got_2 = _ssd_bwd(s["proj"], s["dtraw"], s["yssd"], s["states"], dyb, dproj, lwt["sw"],
                                                   rb=tl["rb"], name=n + "ssd_bwd", comm=comm_2)
    lsm = lsm.reshape(8, 8, D).sum(axis=1)
    gconv = gconv.reshape(5, 8, XBC).sum(axis=1)
    gch = gch.reshape(2, 8, SSD_INNER).sum(axis=1)
    w_in = dict(tn=D, out_shape=(W_IN_ROWS, D), out_index=lambda o, j: (o, j))
    wgrad("w_in", dproj, s["xn"], "in_proj_wgrad", ta=1024, out_block=(1024, D),
          a_tab=list(range(9)), o_tab=[2, 3, 4, 5, 6, 0, 7, 8, 1], **w_in)
    wgrad("w_in", ddt, s["xn"], "dt_proj_wgrad", ta=DT_PAD, out_block=(DT_PAD, D), a_tab=[0],
          o_tab=[NP // DT_PAD], **w_in)
    comm_3 = hooks["in_dgrad"](big) if "in_dgrad" in hooks else None
    dh0, dg1, *got_3 = _mm_nt_rmsbwd(dproj, lwt["wp"], s["h"], lwt["g1"], dh1, tm=tl["tm"], tk=2304,
                                     name=n + "in_proj_dgrad", extra=(ddt, lwt["wdt"]), comm=comm_3)
    grads = dict(
        lru_conv_w=lsm[0:4], lru_conv_b=lsm[4], lru_b_a=lsm[5], lru_b_x=lsm[6], lru_lambda=lsm[7],
        lru_w_a=_bd256_diag(dwa), lru_w_x=_bd256_diag(dwx),
        ssd_conv_w=gconv[0:4], ssd_conv_b=gconv[4], ssd_norm_g=gch[0], ssd_D=gch[1].reshape(N_HEADS, HEAD_P).sum(axis=-1),
        ssd_dt_bias=ghd[0, 0:N_HEADS], ssd_A_log=ghd[1, 0:N_HEADS],
        b_gate=dbg[0], norm1_g=dg1[0], norm2_g=dg2[0])
    return dh0, grads, big, (got_2, got_3)


def _local_step(x, target, w, conv, small, prefetch=None, early_reduce=None):
    h = x
    w = [dict(wl) for wl in w]
    lwts, saved = [], []
    for l in range(N_LAYERS):
        h, s, lwt, arrived = _layer_fwd(h, w[l], conv, small, l, prefetch if l == 0 else None)
        for layer, ws in arrived:
            w[layer].update(ws)
        lwts.append(lwt)
        saved.append(s)
    loss_blk, dgf, dh = _loss_head(h, small["norm_f"].reshape(1, D), target, tm=_tiles(x.shape[0])["tm"], name="loss_head")
    per_layer, big, carried = [None] * N_LAYERS, [None] * N_LAYERS, None
    for l in reversed(range(N_LAYERS)):
        hooks = early_reduce(big[1]) if (early_reduce is not None and l == 0) else None
        dh, per_layer[l], big[l], carried = _layer_bwd(dh, saved[l], lwts[l], l, hooks)
    grads = {k: jnp.stack([per_layer[l][k] for l in range(N_LAYERS)], axis=0) for k in per_layer[0]}
    grads["norm_f"] = dgf[0]
    return loss_blk, dh, grads, big, carried


PACK_W = 1024
BIG = (("w_in", W_IN_SHARD, D, W_IN_SHARD, 256), ("w_branch", 768, D, 256, D), ("w_out", 256, D, 256, D),
       ("w_ffn_in", D, FFN_SHARD, 256, FFN_SHARD), ("w_ffn_out", 704, D, 352, D))
CONV = ("lru_conv_w", "ssd_conv_w")
SMALL = ("norm1_g", "b_gate", "lru_conv_b", "lru_w_a", "lru_b_a", "lru_w_x", "lru_b_x", "lru_lambda", "ssd_conv_b",
         "ssd_dt_bias", "ssd_A_log", "ssd_D", "ssd_norm_g", "norm2_g", "norm_f")
_WIRE = jnp.bfloat16
N_CHIPS = 4
N_DEV = 8


def _mesh_pos():
    return lax.axis_index("x"), lax.axis_index("y"), lax.axis_index("c")


HBM_SPEC = pl.BlockSpec(memory_space=pltpu.HBM)


def _remote(src, dst, send_sems, recv_sems, k, to):
    return pltpu.make_async_remote_copy(src_ref=src, dst_ref=dst, send_sem=send_sems.at[k], recv_sem=recv_sems.at[k],
                                        device_id=to, device_id_type=MESH)


def _other_chips(x, y):
    return [(1 - x, y), (x, 1 - y), (1 - x, 1 - y)]


def _weight_fetch(loc, layer, owner):
    names = list(owner)
    rows = {n: loc[n].shape[1] for n in names}
    by_chip = ("w_in", "w_ffn_in")
    shapes = [((N_CHIPS,) + loc[n].shape[1:]) if n in by_chip else (N_CHIPS * rows[n], D) for n in names]

    def place(o_ref, n, chip):
        if n in by_chip:
            return o_ref.at[chip]
        return o_ref.at[pl.ds(pl.multiple_of(chip * rows[n], 16), rows[n]), :]

    def step(which, in_refs, o_refs, send_sems, recv_sems):
        x, y, c = _mesh_pos()
        s = 2 * x + y
        sib = (x, y, 1 - c)
        chips = _other_chips(x, y)
        for core in (0, 1):
            @pl.when(c == core)
            def _():
                for k, n in enumerate(names):
                    for j, (px, py) in enumerate(chips):
                        landed = place(o_refs[k], n, 2 * px + py)
                        sent = _remote(in_refs[k].at[layer], place(o_refs[k], n, s), send_sems, recv_sems, 3 * k + j,
                                       (px, py, c))
                        arrives = _remote(in_refs[k].at[layer], landed, send_sems, recv_sems, 3 * k + j, (px, py, c))
                        passed = _remote(landed, landed, send_sems, recv_sems, 3 * (len(names) + k) + j, sib)
                        if owner[n] == core:
                            if which == "start":
                                sent.start()
                            elif which == "mid":
                                arrives.wait_recv()
                                passed.start()
                            else:
                                sent.wait_send()
                                passed.wait_send()
                        elif which == "end":
                            passed.wait_recv()

    return dict(inputs=[loc[n] for n in names], names=names,
                out_shapes=[jax.ShapeDtypeStruct(shp, loc[n].dtype) for shp, n in zip(shapes, names)],
                sems=[pltpu.SemaphoreType.DMA((6 * len(names),)), pltpu.SemaphoreType.DMA((6 * len(names),))],
                start=functools.partial(step, "start"), mid=functools.partial(step, "mid"),
                end=functools.partial(step, "end"))


def _comm_now(comm, name):
    n, no = len(comm["inputs"]), len(comm["out_shapes"])

    def body(*refs):
        parts = (refs[:n], refs[n:n + no]) + tuple(refs[n + no:])
        comm["start"](*parts)
        comm["mid"](*parts)
        comm["end"](*parts)

    return pl.pallas_call(
        body, name=name, in_specs=[HBM_SPEC] * n, out_specs=[HBM_SPEC] * no, out_shape=comm["out_shapes"],
        scratch_shapes=comm["sems"],
    )(*comm["inputs"])


def _sibling_send(bufs, layer):
    n = len(bufs)

    def step(which, in_refs, o_refs, send_sems, recv_sems):
        x, y, c = _mesh_pos()
        copies = [_remote(in_refs[k], o_refs[k], send_sems, recv_sems, k, (x, y, 1 - c)) for k in range(n)]

        @pl.when(c != layer)
        def _():
            for cp in copies:
                if which == "start":
                    cp.start()
                elif which == "end":
                    cp.wait_send()

        @pl.when(c == layer)
        def _():
            for cp in copies:
                if which == "end":
                    cp.wait_recv()

    return dict(inputs=list(bufs), out_shapes=[jax.ShapeDtypeStruct(b.shape, b.dtype) for b in bufs],
                sems=[pltpu.SemaphoreType.DMA((n,)), pltpu.SemaphoreType.DMA((n,))],
                start=functools.partial(step, "start"), mid=functools.partial(step, "mid"),
                end=functools.partial(step, "end"))


def _add_cast(g, recv, own, *, a, tr, tc, name):
    wd = g.shape[1]
    nr = a // tr

    def body(own_ref, g_ref, r_ref, o_ref):
        @pl.when(own_ref[0] == 1)
        def _():
            o_ref[...] = (g_ref[...] + r_ref[...]).astype(o_ref.dtype)

    blk = pl.BlockSpec((tr, tc), lambda p, i, j, own_ref: ((p * nr + i) * own_ref[0], j * own_ref[0]))
    return pl.pallas_call(
        body, name=name,
        grid_spec=pltpu.PrefetchScalarGridSpec(
            num_scalar_prefetch=1, grid=(N_CHIPS, nr, wd // tc), in_specs=[blk, blk],
            out_specs=pl.BlockSpec((None, tr, tc), lambda p, i, j, own_ref: (p * own_ref[0], i * own_ref[0], j * own_ref[0]))),
        out_shape=jax.ShapeDtypeStruct((N_CHIPS, a, wd), _WIRE),
        compiler_params=_cp(("arbitrary", "arbitrary", "arbitrary")),
    )(own, g, recv)


def _chip_exchange(parts, layer):
    n = len(parts)

    def step(which, s_refs, o_refs, send_sems, recv_sems):
        x, y, c = _mesh_pos()
        s = 2 * x + y

        @pl.when(c == layer)
        def _():
            for j, (px, py) in enumerate(_other_chips(x, y)):
                for k in range(n):
                    p = 2 * px + py
                    sent = _remote(s_refs[k].at[p], o_refs[k].at[s], send_sems, recv_sems, n * j + k, (px, py, c))
                    if which == "start":
                        sent.start()
                    elif which == "end":
                        _remote(s_refs[k].at[p], o_refs[k].at[p], send_sems, recv_sems, n * j + k, (px, py, c)).wait_recv()
                        sent.wait_send()

    return dict(inputs=list(parts), out_shapes=[jax.ShapeDtypeStruct(p.shape, p.dtype) for p in parts],
                sems=[pltpu.SemaphoreType.DMA((3 * n,)), pltpu.SemaphoreType.DMA((3 * n,))],
                start=functools.partial(step, "start"), mid=functools.partial(step, "mid"),
                end=functools.partial(step, "end"))


def _sum_slots(slots, own, sel, *, tr, tc, name, layer=None, into=None):
    n, rows, wd = slots.shape
    k = own.shape[0]

    def body(sel_ref, s_ref, own_ref, *rest):
        o_ref = rest[-1]

        @pl.when(sel_ref[1] == 1)
        def _():
            mine = sel_ref[0]
            acc = jnp.zeros((tr, tc), F32)
            for p in range(n):
                acc = acc + jnp.where(mine == p, own_ref[...].astype(F32), s_ref[p].astype(F32))
            o_ref[...] = acc

    if layer is not None:
        out_spec = pl.BlockSpec((None, tr, tc), lambda i, j, sel_ref: (layer, i * sel_ref[1], j * sel_ref[1]))
        out_shape = jax.ShapeDtypeStruct((N_LAYERS, rows, wd), F32)
    else:
        out_spec = pl.BlockSpec((tr, tc), lambda i, j, sel_ref: (i * sel_ref[1], j * sel_ref[1]))
        out_shape = jax.ShapeDtypeStruct((rows, wd), F32)
    in_specs = [pl.BlockSpec((n, tr, tc), lambda i, j, sel_ref: (0, i * sel_ref[1], j * sel_ref[1])),
                pl.BlockSpec((None, tr, tc), lambda i, j, sel_ref: (sel_ref[0] if k > 1 else 0, i * sel_ref[1],
                                                                    j * sel_ref[1]))]
    args = [sel, slots, own]
    if into is not None:
        in_specs.append(pl.BlockSpec(memory_space=pl.ANY))
        args.append(into)
    return pl.pallas_call(
        body, name=name,
        grid_spec=pltpu.PrefetchScalarGridSpec(num_scalar_prefetch=1, grid=(rows // tr, wd // tc), in_specs=in_specs,
                                               out_specs=out_spec),
        out_shape=out_shape, input_output_aliases={3: 0} if into is not None else {},
        compiler_params=_cp(("arbitrary", "arbitrary")),
    )(*args)


def _sibling_share(both):
    n = len(both)

    def body(*refs):
        o_refs, (send_sems, recv_sems) = refs[n:2 * n], refs[2 * n:]
        x, y, c = _mesh_pos()
        sends = [_remote(o_refs[k].at[c], o_refs[k].at[c], send_sems, recv_sems, k, (x, y, 1 - c)) for k in range(n)]
        for cp in sends:
            cp.start()
        for k in range(n):
            _remote(o_refs[k].at[1 - c], o_refs[k].at[1 - c], send_sems, recv_sems, k, (x, y, 1 - c)).wait_recv()
        for cp in sends:
            cp.wait_send()

    return pl.pallas_call(
        body, name="grad_sibling_share", in_specs=[HBM_SPEC] * n, out_specs=[HBM_SPEC] * n,
        out_shape=[jax.ShapeDtypeStruct(b.shape, b.dtype) for b in both], input_output_aliases={k: k for k in range(n)},
        scratch_shapes=[pltpu.SemaphoreType.DMA((n,)), pltpu.SemaphoreType.DMA((n,))],
    )(*both)


def _allgather_devices(part):
    rows, wd = part.shape

    def step(which, in_refs, o_refs, send_sems, recv_sems):
        (p_ref,), (o_ref,) = in_refs, o_refs
        x, y, c = _mesh_pos()
        sib = (x, y, 1 - c)
        chips = _other_chips(x, y)
        slot = lambda px, py, pc: o_ref.at[4 * px + 2 * py + pc]
        first = [_remote(p_ref, slot(x, y, c), send_sems, recv_sems, 0, sib)]
        first += [_remote(p_ref, slot(x, y, c), send_sems, recv_sems, 1 + j, (px, py, c)) for j, (px, py) in enumerate(chips)]
        passed = [_remote(slot(px, py, c), slot(px, py, c), send_sems, recv_sems, 4 + j, sib)
                  for j, (px, py) in enumerate(chips)]
        if which == "start":
            for cp in first:
                cp.start()
        elif which == "mid":
            for j, (px, py) in enumerate(chips):
                _remote(p_ref, slot(px, py, c), send_sems, recv_sems, 1 + j, (px, py, c)).wait_recv()
                passed[j].start()
        else:
            _remote(p_ref, slot(x, y, 1 - c), send_sems, recv_sems, 0, sib).wait_recv()
            for j, (px, py) in enumerate(chips):
                _remote(slot(px, py, 1 - c), slot(px, py, 1 - c), send_sems, recv_sems, 4 + j, sib).wait_recv()
            for cp in first + passed:
                cp.wait_send()

    return dict(inputs=[part], out_shapes=[jax.ShapeDtypeStruct((N_DEV, rows, wd), part.dtype)],
                sems=[pltpu.SemaphoreType.DMA((N_DEV - 1,)), pltpu.SemaphoreType.DMA((N_DEV - 1,))],
                start=functools.partial(step, "start"), mid=functools.partial(step, "mid"),
                end=functools.partial(step, "end"))


def _comm_both(a, b):
    na, nao = len(a["inputs"]), len(a["out_shapes"])

    def step(which, in_refs, o_refs, sa, ra, sb, rb_):
        a[which](in_refs[:na], o_refs[:nao], sa, ra)
        b[which](in_refs[na:], o_refs[nao:], sb, rb_)

    return dict(inputs=a["inputs"] + b["inputs"], out_shapes=a["out_shapes"] + b["out_shapes"], sems=a["sems"] + b["sems"],
                start=functools.partial(step, "start"), mid=functools.partial(step, "mid"),
                end=functools.partial(step, "end"))


def _by_chip_to_full(stack):
    _, nl, r, b = stack.shape
    return stack.transpose(1, 2, 0, 3).reshape(nl, r, N_CHIPS * b)


def _sharded_step(a):
    x = a["x"][0]
    target = a["loss_target"][0]
    cx, cy, cc = _mesh_pos()
    chip = (2 * cx + cy).astype(jnp.int32)
    core = cc.astype(jnp.int32)
    me = (4 * cx + 2 * cy + cc).astype(jnp.int32)
    zero = jnp.zeros((), jnp.int32)
    dus = lax.dynamic_update_slice

    loc = {n: a[n].astype(_MXU) for n, *_ in BIG}

    def with_own(got, names, layer):
        out = {}
        for g, n in zip(got, names):
            mine = loc[n][layer]
            out[n] = (dus(g, mine[None], (chip, zero, zero)) if g.ndim == 3 else dus(g, mine, (chip * mine.shape[0], zero)))
        return out

    rest = {"w_ffn_in": 0, "w_branch": 1, "w_out": 1, "w_ffn_out": 1}
    conv_loc = jnp.concatenate([a[n].reshape(-1, PACK_W) for n in CONV], axis=0)
    now = _weight_fetch(loc, 0, {"w_in": 0})
    *got_now, conv_all = _comm_now(_comm_both(now, _allgather_devices(conv_loc)), "allgather_weights")
    w0 = with_own(got_now, now["names"], 0)
    later = {"in_proj": (0, _weight_fetch(loc, 0, rest)), "lru": (1, _weight_fetch(loc, 1, {"w_in": 0})),
             "ssd": (1, _weight_fetch(loc, 1, rest))}
    prefetch = {k: (f, functools.partial(lambda got, layer, f: (layer, with_own(got, f["names"], layer)), layer=layer, f=f))
                for k, (layer, f) in later.items()}
    conv_all = dus(conv_all, conv_loc[None], (me, zero, zero))[0::2]
    conv, off = {}, 0
    for n in CONV:
        rows = a[n].size // PACK_W
        conv[n] = _by_chip_to_full(conv_all[:, off:off + rows].reshape((N_CHIPS,) + a[n].shape))
        off += rows
    small = {n: a[n] for n in SMALL}

    views = lambda big_l: [big_l[n].reshape(-1, wd) for n, _, wd, _, _ in BIG]
    owns = lambda layer: (core == layer).astype(jnp.int32)

    def partial_sums(big_l, recv, layer):
        return [_add_cast(v, r, owns(layer).reshape(1), a=rows, tr=tr, tc=tc, name=f"grad_add_sibling_l{layer}_{n}")
                for v, r, (n, rows, _, tr, tc) in zip(views(big_l), recv, BIG)]

    def reduced(slots, parts, layer, into):
        sel = jnp.stack([chip, owns(layer)])
        return [_sum_slots(s, p, sel, tr=tr, tc=tc, name=f"grad_sum_chips_l{layer}_{n}", layer=layer, into=buf)
                for s, p, buf, (n, _, _, tr, tc) in zip(slots, parts, into, BIG)]

    kept = {}

    def early_reduce(big_1):
        def exchange_second(recv_1):
            kept["parts_1"] = partial_sums(big_1, recv_1, 1)
            return _chip_exchange(kept["parts_1"], 1)
        return dict(lru=_sibling_send(views(big_1), 1), ssd=exchange_second,
                    in_dgrad=lambda big_0: _sibling_send(views(big_0), 0))

    loss_blk, grad_x, grads, big, (slots_1, recv_0) = _local_step(x, target, [w0, {}], conv, small, prefetch, early_reduce)
    loss = lax.psum(loss_blk[0, 0], ("x", "y", "c"))
    both = reduced(slots_1, kept["parts_1"], 1, [None] * len(BIG))
    parts_0 = partial_sums(big[0], recv_0, 0)
    names = SMALL + CONV
    srows = -(-sum(grads[n].size for n in names) // (8 * PACK_W)) * 8
    flat = lambda d, ns: jnp.concatenate([d[n].reshape(-1) for n in ns])
    padto = lambda v: jnp.pad(v, (0, srows * PACK_W - v.shape[0])).reshape(srows, PACK_W)
    g_own = padto(flat(grads, names))
    *slots_0, g_all = _comm_now(_comm_both(_chip_exchange(parts_0, 0), _allgather_devices(g_own)), "grad_chip_exchange")
    both = reduced(slots_0, parts_0, 0, both)
    done = dict(zip([n for n, *_ in BIG], _sibling_share(both)))
    g_big = {n: done[n].reshape(a[n].shape) for n in ("w_branch", "w_out", "w_ffn_in", "w_ffn_out")}
    gt = done["w_in"].transpose(0, 2, 1)
    first = jnp.concatenate([gt[..., 512 * j + 256 * part:512 * j + 256 * (part + 1)] for part in range(2) for j in range(4)]
                            + [gt[..., 2 * D:]], axis=-1)
    tail = W_IN_SHARD - (IN_DIM - 7168)
    last = jnp.concatenate([gt[..., :tail], gt[..., W_IN_SHARD - N_HEADS:], gt[..., tail:W_IN_SHARD - N_HEADS]], axis=-1)
    g_big["w_in"] = jnp.where(chip == 0, first, jnp.where(chip == N_CHIPS - 1, last, gt))

    g_sum = _sum_slots(g_all, g_own[None], jnp.stack([me, zero + 1]), tr=srows, tc=PACK_W, name="small_grad_sum")
    off, g_small = 0, {}
    for n in names:
        g_small[n] = g_sum.reshape(-1)[off:off + grads[n].size].reshape(grads[n].shape)
        off += grads[n].size
    for n in CONV:
        width = a[n].shape[2]
        g_big[n] = lax.dynamic_slice(g_small.pop(n), (zero, zero, chip * width), a[n].shape)

    out_g, out_d, out_m, out_v = {}, {}, {}, {}
    for n in g_big:
        shp = a[n].shape
        two_d = (shp[0] * shp[1], shp[2])
        d_, m_, v_ = _adamw(a[n].reshape(two_d), g_big[n].reshape(two_d), a["m_" + n].reshape(two_d),
                            a["v_" + n].reshape(two_d), name="adamw_" + n)
        out_g[n], out_d[n], out_m[n], out_v[n] = g_big[n], d_.reshape(shp), m_.reshape(shp), v_.reshape(shp)
    d_, m_, v_ = _adamw(padto(flat(a, SMALL)), padto(flat(g_small, SMALL)), padto(flat({n: a["m_" + n] for n in SMALL}, SMALL)),
                        padto(flat({n: a["v_" + n] for n in SMALL}, SMALL)), name="adamw_small")
    off = 0
    for n in SMALL:
        cut = lambda v: v.reshape(-1)[off:off + a[n].size].reshape(a[n].shape)
        out_g[n], out_d[n], out_m[n], out_v[n] = g_small[n], cut(d_), cut(m_), cut(v_)
        off += a[n].size
    return loss, grad_x[None], out_g, out_d, out_m, out_v


WEIGHTS = ("norm1_g", "w_in", "b_gate", "lru_conv_w", "lru_conv_b", "lru_w_a", "lru_b_a", "lru_w_x", "lru_b_x", "lru_lambda",
           "ssd_conv_w", "ssd_conv_b", "ssd_dt_bias", "ssd_A_log", "ssd_D", "ssd_norm_g", "w_branch", "w_out", "norm2_g",
           "w_ffn_in", "w_ffn_out", "norm_f")
INPUTS = ("x",) + WEIGHTS + ("loss_target",) + tuple("m_" + n for n in WEIGHTS) + tuple("v_" + n for n in WEIGHTS)


def kernel(x, norm1_g, w_in, b_gate, lru_conv_w, lru_conv_b, lru_w_a, lru_b_a, lru_w_x, lru_b_x, lru_lambda, ssd_conv_w, ssd_conv_b, ssd_dt_bias, ssd_A_log, ssd_D, ssd_norm_g, w_branch, w_out, norm2_g, w_ffn_in, w_ffn_out, norm_f, loss_target, m_norm1_g, m_w_in, m_b_gate, m_lru_conv_w, m_lru_conv_b, m_lru_w_a, m_lru_b_a, m_lru_w_x, m_lru_b_x, m_lru_lambda, m_ssd_conv_w, m_ssd_conv_b, m_ssd_dt_bias, m_ssd_A_log, m_ssd_D, m_ssd_norm_g, m_w_branch, m_w_out, m_norm2_g, m_w_ffn_in, m_w_ffn_out, m_norm_f, v_norm1_g, v_w_in, v_b_gate, v_lru_conv_w, v_lru_conv_b, v_lru_w_a, v_lru_b_a, v_lru_w_x, v_lru_b_x, v_lru_lambda, v_ssd_conv_w, v_ssd_conv_b, v_ssd_dt_bias, v_ssd_A_log, v_ssd_D, v_ssd_norm_g, v_w_branch, v_w_out, v_norm2_g, v_w_ffn_in, v_w_ffn_out, v_norm_f):
    args = (x, norm1_g, w_in, b_gate, lru_conv_w, lru_conv_b, lru_w_a, lru_b_a, lru_w_x, lru_b_x, lru_lambda, ssd_conv_w, ssd_conv_b, ssd_dt_bias, ssd_A_log, ssd_D, ssd_norm_g, w_branch, w_out, norm2_g, w_ffn_in, w_ffn_out, norm_f, loss_target, m_norm1_g, m_w_in, m_b_gate, m_lru_conv_w, m_lru_conv_b, m_lru_w_a, m_lru_b_a, m_lru_w_x, m_lru_b_x, m_lru_lambda, m_ssd_conv_w, m_ssd_conv_b, m_ssd_dt_bias, m_ssd_A_log, m_ssd_D, m_ssd_norm_g, m_w_branch, m_w_out, m_norm2_g, m_w_ffn_in, m_w_ffn_out, m_norm_f, v_norm1_g, v_w_in, v_b_gate, v_lru_conv_w, v_lru_conv_b, v_lru_w_a, v_lru_b_a, v_lru_w_x, v_lru_b_x, v_lru_lambda, v_ssd_conv_w, v_ssd_conv_b, v_ssd_dt_bias, v_ssd_A_log, v_ssd_D, v_ssd_norm_g, v_w_branch, v_w_out, v_norm2_g, v_w_ffn_in, v_w_ffn_out, v_norm_f)
    assert len(args) == len(INPUTS)
    loss, grad_x, g, d, m, v = _sharded_step(dict(zip(INPUTS, args)))
    return (loss, grad_x, *[g[n] for n in WEIGHTS], *[d[n] for n in WEIGHTS], *[m[n] for n in WEIGHTS],
            *[v[n] for n in WEIGHTS])
```

```python
import functools
import math

import numpy as np
import jax
import jax.numpy as jnp
from jax import lax
from jax.experimental import pallas as pl
from jax.experimental.pallas import tpu as pltpu

F32 = jnp.float32
BF16 = jnp.bfloat16
_MXU = jnp.bfloat16
_HI = lax.Precision.HIGHEST

D = 1024
EPS = 1e-6
N_LAYERS = 2
LRU_C = 8.0
N_HEADS = 32
HEAD_P = 64
N_GROUPS = 4
N_STATE = 128
SSD_INNER = 2048
XBC = 3072
D_FF = 2816
CHUNK = 64
NORM_ROWS = 32
IN_DIM = 9248

NP = 9216
ZX_W = 5120
G0 = 6144
LBLK = 512
DT_PAD = 128

VMEM_LIMIT_BYTES_V7X = 56 * 1024 * 1024

ADAM_LR, ADAM_B1, ADAM_B2, ADAM_EPS, ADAM_WD, ADAM_STEP = 0.001, 0.9, 0.999, 1e-08, 0.01, 10
MESH = pl.DeviceIdType.MESH


def _cp(sem):
    return pltpu.CompilerParams(dimension_semantics=sem, vmem_limit_bytes=VMEM_LIMIT_BYTES_V7X)


def _lblk_col(j):
    return 10 + j + 4 * (j // 2)


def _sigmoid(x):
    return 0.5 * jnp.tanh(0.5 * x) + 0.5


def _softplus(x):
    return jnp.maximum(x, 0.0) + jnp.log(1.0 + jnp.exp(-jnp.abs(x)))


def _silu(x):
    return x * _sigmoid(x)


def _dsilu(x):
    s = _sigmoid(x)
    return s * (1.0 + x * (1.0 - s))


_GELU_C0 = math.sqrt(2.0 / math.pi)
_GELU_C1 = 0.044715


def _gelu_and_grad(x):
    t = jnp.tanh(_GELU_C0 * (x + _GELU_C1 * x * x * x))
    g = 0.5 * x * (1.0 + t)
    dg = 0.5 * (1.0 + t) + 0.5 * x * (1.0 - t * t) * _GELU_C0 * (1.0 + 3.0 * _GELU_C1 * x * x)
    return g, dg


def _one_minus_exp(x):
    p = 1.0 + x * (1.0 / 7.0)
    p = 1.0 + x * (1.0 / 6.0) * p
    p = 1.0 + x * (1.0 / 5.0) * p
    p = 1.0 + x * (1.0 / 4.0) * p
    p = 1.0 + x * (1.0 / 3.0) * p
    p = 1.0 + x * (1.0 / 2.0) * p
    return jnp.where(x > -0.3, -x * p, 1.0 - jnp.exp(x))


def _dot(a, b):
    return jnp.dot(a.astype(_MXU), b.astype(_MXU), preferred_element_type=F32)


def _dot_nt(a, b):
    return lax.dot_general(a.astype(_MXU), b.astype(_MXU), (((1,), (1,)), ((), ())), preferred_element_type=F32)


def _dot_tn(a, b):
    return lax.dot_general(a.astype(_MXU), b.astype(_MXU), (((0,), (0,)), ((), ())), preferred_element_type=F32)


def _shift_down(x, prev8, k):
    xr = pltpu.roll(x, k, 0)
    pr = pltpu.roll(prev8, k, 0)
    row = lax.broadcasted_iota(jnp.int32, prev8.shape, 0)
    head = jnp.where(row < k, pr, xr[0:8])
    return jnp.concatenate([head, xr[8:]], axis=0)


def _shift_up(x, next8, k):
    r = x.shape[0]
    xr = pltpu.roll(x, r - k, 0)
    nr = pltpu.roll(next8, 8 - k, 0)
    row = lax.broadcasted_iota(jnp.int32, next8.shape, 0)
    tail = jnp.where(row >= 8 - k, nr, xr[r - 8:r])
    return jnp.concatenate([xr[:r - 8], tail], axis=0)


def _conv4(x, prev8, w_ref, b_ref, cols=slice(None)):
    acc = x * w_ref[3:4, cols] + b_ref[0:1, cols]
    for k in (1, 2, 3):
        acc = acc + _shift_down(x, prev8, k) * w_ref[3 - k:4 - k, cols]
    return acc


def _conv4_bwd_x(dy, next8, w_ref, cols=slice(None)):
    acc = dy * w_ref[3:4, cols]
    for k in (1, 2, 3):
        acc = acc + _shift_up(dy, next8, k) * w_ref[3 - k:4 - k, cols]
    return acc


def _lin_scan(a, b, reverse):
    r = a.shape[0]
    row = lax.broadcasted_iota(jnp.int32, a.shape, 0)
    d = 1
    while d < r:
        sh = (r - d) if reverse else d
        a_s = pltpu.roll(a, sh, 0)
        b_s = pltpu.roll(b, sh, 0)
        m = (row < r - d) if reverse else (row >= d)
        b = jnp.where(m, a * b_s + b, b)
        a = jnp.where(m, a * a_s, a)
        d *= 2
    return a, b


def _rsum(x):
    return jnp.sum(x, axis=0, keepdims=True)


def _comm_specs(comm):
    if comm is None:
        return [], [], [], [], []
    n = len(comm["inputs"])
    return list(comm["inputs"]), [HBM_SPEC] * n, [HBM_SPEC] * len(comm["out_shapes"]), list(comm["out_shapes"]), comm["sems"]


def _comm_steps(comm, refs, n_in, n_out, first, mid, last):
    ni, no = len(comm["inputs"]), len(comm["out_shapes"])
    parts = (refs[n_in:n_in + ni], refs[n_out:n_out + no]) + tuple(refs[len(refs) - len(comm["sems"]):])
    for when, what in ((first, "start"), (mid, "mid"), (last, "end")):
        @pl.when(when)
        def _():
            comm[what](*parts)


def _norm_mm(h, gamma, w, *, tm, tn, name, out_dtype=F32, comm=None):
    m, k = h.shape
    if w.ndim == 3:
        assert w.shape[2] == tn
        n = w.shape[0] * tn
        w_spec = pl.BlockSpec((None, k, tn), lambda i, j: (j, 0, 0))
    else:
        n = w.shape[1]
        w_spec = pl.BlockSpec((k, tn), lambda i, j: (0, j))

    c_args, c_in_specs, c_out_specs, c_out_shapes, c_sems = _comm_specs(comm)
    ni, nj = m // tm, n // tn

    def body(*refs):
        h_ref, g_ref, w_ref = refs[:3]
        xn_ref, o_ref = refs[3 + len(c_args):5 + len(c_args)]
        i, j = pl.program_id(0), pl.program_id(1)
        if comm is not None:
            _comm_steps(comm, refs, 3, 5 + len(c_args), (i == 0) & (j == 0), (i == (3 * ni) // 4) & (j == 0),
                        (i == ni - 1) & (j == nj - 1))

        @pl.when(j == 0)
        def _():
            x = h_ref[...]
            r = lax.rsqrt(jnp.mean(x * x, axis=-1, keepdims=True) + EPS)
            xn_ref[...] = ((x * r) * g_ref[...]).astype(xn_ref.dtype)
        o_ref[...] = jnp.dot(xn_ref[...], w_ref[...], preferred_element_type=F32).astype(o_ref.dtype)

    return pl.pallas_call(
        body, name=name, grid=(ni, nj),
        in_specs=[pl.BlockSpec((tm, k), lambda i, j: (i, 0)), pl.BlockSpec((1, k), lambda i, j: (0, 0)), w_spec] + c_in_specs,
        out_specs=[pl.BlockSpec((tm, k), lambda i, j: (i, 0)), pl.BlockSpec((tm, tn), lambda i, j: (i, j))] + c_out_specs,
        out_shape=[jax.ShapeDtypeStruct((m, k), _MXU), jax.ShapeDtypeStruct((m, n), out_dtype)] + c_out_shapes,
        scratch_shapes=c_sems,
        compiler_params=_cp(("arbitrary", "arbitrary") if comm is not None else ("parallel", "arbitrary")),
    )(h, gamma, w, *c_args)


def _mm_nn(a, w, *, tm, tn, name, residual=None):
    m, k = a.shape
    n = w.shape[1]

    def body(*refs):
        if residual is None:
            a_ref, w_ref, o_ref = refs
            o_ref[...] = _dot(a_ref[...], w_ref[...])
        else:
            a_ref, w_ref, r_ref, o_ref = refs
            o_ref[...] = _dot(a_ref[...], w_ref[...]) + r_ref[...]

    in_specs = [pl.BlockSpec((tm, k), lambda i, j: (i, 0)), pl.BlockSpec((k, tn), lambda i, j: (0, j))]
    args = [a, w]
    if residual is not None:
        in_specs.append(pl.BlockSpec((tm, tn), lambda i, j: (i, j)))
        args.append(residual)
    return pl.pallas_call(
        body, name=name, grid=(m // tm, n // tn), in_specs=in_specs,
        out_specs=pl.BlockSpec((tm, tn), lambda i, j: (i, j)),
        out_shape=jax.ShapeDtypeStruct((m, n), F32),
        compiler_params=_cp(("parallel", "parallel")),
    )(*args)


def _wgrad(a, b, *, tt, ta, tn, name, out_shape, out_block, out_index, a_tab=None, o_tab=None, into=None):
    t = a.shape[0]
    a_tab = list(range(a.shape[1] // ta)) if a_tab is None else a_tab
    o_tab = a_tab if o_tab is None else o_tab
    nb = b.shape[1] // tn

    def body(at_ref, ot_ref, a_ref, b_ref, *rest):
        del at_ref, ot_ref
        o_ref = rest[-1]

        @pl.when(pl.program_id(2) == 0)
        def _():
            o_ref[...] = jnp.zeros_like(o_ref)
        o_ref[...] += _dot_tn(a_ref[...], b_ref[...])

    in_specs = [pl.BlockSpec((tt, ta), lambda r, j, i, at, ot: (i, at[r])),
                pl.BlockSpec((tt, tn), lambda r, j, i, at, ot: (i, j))]
    args = [jnp.asarray(a_tab, jnp.int32), jnp.asarray(o_tab, jnp.int32), a, b]
    aliases = {}
    if into is not None:
        in_specs.append(pl.BlockSpec(memory_space=pl.ANY))
        args.append(into)
        aliases = {4: 0}
    return pl.pallas_call(
        body, name=name,
        grid_spec=pltpu.PrefetchScalarGridSpec(
            num_scalar_prefetch=2, grid=(len(a_tab), nb, t // tt), in_specs=in_specs,
            out_specs=pl.BlockSpec(out_block, lambda r, j, i, at, ot: out_index(ot[r], j))),
        out_shape=jax.ShapeDtypeStruct(out_shape, F32), input_output_aliases=aliases,
        compiler_params=_cp(("parallel", "parallel", "arbitrary")),
    )(*args)


def _mm_nt(a, w, *, tm, name):
    m, kc = a.shape
    n = w.shape[0]

    def body(a_ref, w_ref, o_ref):
        o_ref[...] = _dot_nt(a_ref[...], w_ref[...])

    return pl.pallas_call(
        body, name=name, grid=(m // tm,),
        in_specs=[pl.BlockSpec((tm, kc), lambda i: (i, 0)), pl.BlockSpec((n, kc), lambda i: (0, 0))],
        out_specs=pl.BlockSpec((tm, n), lambda i: (i, 0)),
        out_shape=jax.ShapeDtypeStruct((m, n), F32),
        compiler_params=_cp(("parallel",)),
    )(a, w)


def _mm_nt_rmsbwd(dy, w, x, gamma, dres, *, tm, tk, name, extra=None, comm=None):
    m, kc = dy.shape
    nk = kc // tk
    ni = m // tm
    n_x = 5 if extra is None else 7
    c_args, c_in_specs, c_out_specs, c_out_shapes, c_sems = _comm_specs(comm)
    if w.ndim == 3:
        assert w.shape[0] == nk and w.shape[2] == tk
        d = w.shape[1]
        w_spec = pl.BlockSpec((None, d, tk), lambda i, k: (k, 0, 0))
    else:
        d = w.shape[0]
        w_spec = pl.BlockSpec((d, tk), lambda i, k: (0, k))

    def body(*refs):
        dy_ref, w_ref, x_ref, g_ref, r_ref = refs[:5]
        if extra is not None:
            dy2_ref, w2_ref = refs[5:7]
        n_out = n_x + len(c_args)
        dx_ref, dg_ref = refs[n_out:n_out + 2]
        acc_ref = refs[n_out + 2 + len(c_out_shapes)]
        i, kk = pl.program_id(0), pl.program_id(1)
        if comm is not None:
            _comm_steps(comm, refs, n_x, n_out + 2, (i == 0) & (kk == 0), (i == (3 * ni) // 4) & (kk == 0),
                        (i == ni - 1) & (kk == nk - 1))

        @pl.when(kk == 0)
        def _():
            acc_ref[...] = jnp.zeros_like(acc_ref)

        @pl.when((i == 0) & (kk == 0))
        def _():
            dg_ref[...] = jnp.zeros_like(dg_ref)

        acc_ref[...] += _dot_nt(dy_ref[...], w_ref[...])

        @pl.when(kk == nk - 1)
        def _():
            dxn = acc_ref[...]
            if extra is not None:
                dxn = dxn + _dot_nt(dy2_ref[...], w2_ref[...])
            xv = x_ref[...]
            r = lax.rsqrt(jnp.mean(xv * xv, axis=-1, keepdims=True) + EPS)
            xh = xv * r
            dg_ref[0:1, :] += _rsum(dxn * xh)
            dxh = dxn * g_ref[...]
            dx_ref[...] = r_ref[...] + r * (dxh - xh * jnp.mean(dxh * xh, axis=-1, keepdims=True))

    in_specs = [pl.BlockSpec((tm, tk), lambda i, k: (i, k)), w_spec,
                pl.BlockSpec((tm, d), lambda i, k: (i, 0)), pl.BlockSpec((1, d), lambda i, k: (0, 0)),
                pl.BlockSpec((tm, d), lambda i, k: (i, 0))]
    args = [dy, w, x, gamma, dres]
    if extra is not None:
        k2 = extra[0].shape[1]
        in_specs += [pl.BlockSpec((tm, k2), lambda i, k: (i, 0)), pl.BlockSpec((d, k2), lambda i, k: (0, 0))]
        args += list(extra)
    return pl.pallas_call(
        body, name=name, grid=(ni, nk), in_specs=in_specs + c_in_specs,
        out_specs=[pl.BlockSpec((tm, d), lambda i, k: (i, 0)), pl.BlockSpec((8, d), lambda i, k: (0, 0))] + c_out_specs,
        out_shape=[jax.ShapeDtypeStruct((m, d), F32), jax.ShapeDtypeStruct((8, d), F32)] + c_out_shapes,
        scratch_shapes=[pltpu.VMEM((tm, d), F32)] + c_sems,
        compiler_params=_cp(("arbitrary", "arbitrary")),
    )(*args, *c_args)


def _rsum8(x):
    acc = x[0:8]
    for g in range(1, x.shape[0] // 8):
        acc = acc + x[8 * g:8 * (g + 1)]
    return acc


def _lru_gates(x, prev8, cw_ref, cb_ref, wa_ref, wx_ref, ba_ref, bx_ref, lam_ref):
    u = _conv4(x, prev8, cw_ref, cb_ref)
    ra =_sigmoid(_dot(u, wa_ref[0]) + ba_ref[...])
    ia = _sigmoid(_dot(u, wx_ref[0]) + bx_ref[...])
    sp = _softplus(-lam_ref[...])
    log_a = -LRU_C * ra * sp
    a = jnp.exp(log_a)
    m2 = _one_minus_exp(2.0 * log_a)
    mult = jnp.sqrt(m2)
    return u, ra, ia, sp, a, m2, mult


def _lru_fwd(proj, lw, *, r, name, comm=None):
    t = proj.shape[0]
    nt = t // r
    c_args, c_in_specs, c_out_specs, c_out_shapes, c_sems = _comm_specs(comm)

    def body(*refs):
        xg_ref, xp_ref, cw_ref, cb_ref, wa_ref, wx_ref, ba_ref, bx_ref, lam_ref = refs[:9]
        hl_ref, ya_ref = refs[9 + len(c_args):11 + len(c_args)]
        carry_ref = refs[11 + len(c_args) + len(c_out_shapes)]
        i = pl.program_id(1)
        if comm is not None:
            j = pl.program_id(0)
            _comm_steps(comm, refs, 9, 11 + len(c_args), (j == 0) & (i == 0), (j == 3) & (i == 0), (j == 3) & (i == nt - 1))

        @pl.when(i == 0)
        def _():
            carry_ref[...] = jnp.zeros_like(carry_ref)

        x = xg_ref[:, 0:256]
        lg = xg_ref[:, 256:512]
        prev8 = jnp.where(i == 0, 0.0, xp_ref[:, 0:256])
        u, ra, ia, sp, a, m2, mult = _lru_gates(x, prev8, cw_ref, cb_ref, wa_ref, wx_ref, ba_ref, bx_ref, lam_ref)
        ac, hc = _lin_scan(a, mult * ia * u, False)
        h = hc + ac * carry_ref[0:1, :]
        hl_ref[...] = h
        carry_ref[0:1, :] = hl_ref[r - 1:r, :]
        g, _ = _gelu_and_grad(lg)
        ya_ref[...] = (g * h).astype(ya_ref.dtype)

    small = lambda rows: pl.BlockSpec((rows, 256), lambda j, i: (0, j))
    return pl.pallas_call(
        body, name=name, grid=(4, nt),
        in_specs=[pl.BlockSpec((r, LBLK), lambda j, i: (i, _lblk_col(j))),
                  pl.BlockSpec((8, LBLK), lambda j, i: (jnp.maximum(i * (r // 8) - 1, 0), _lblk_col(j))),
                  small(4), small(1),
                  pl.BlockSpec((1, 256, 256), lambda j, i: (j, 0, 0)), pl.BlockSpec((1, 256, 256), lambda j, i: (j, 0, 0)),
                  small(1), small(1), small(1)] + c_in_specs,
        out_specs=[pl.BlockSpec((r, 256), lambda j, i: (i, j)), pl.BlockSpec((r, 256), lambda j, i: (i, j))] + c_out_specs,
        out_shape=[jax.ShapeDtypeStruct((t, D), F32), jax.ShapeDtypeStruct((t, D), _MXU)] + c_out_shapes,
        scratch_shapes=[pltpu.VMEM((8, 256), F32)] + c_sems,
        compiler_params=_cp(("arbitrary", "arbitrary") if comm is not None else ("parallel", "arbitrary")),
    )(proj, proj, lw["cw"], lw["cb"], lw["wa"], lw["wx"], lw["ba"], lw["bx"], lw["lam"], *c_args)


def _lru_bwd(proj, hl, dya, dproj, lw, *, r, name, comm=None):
    t = proj.shape[0]
    nt = t // r
    c_args, c_in_specs, c_out_specs, c_out_shapes, c_sems = _comm_specs(comm)
    n_in = 13

    def body(*refs):
        xg_ref, xp_ref, hl_ref, hp_ref, dya_ref, cw_ref, cb_ref, wa_ref, wx_ref, ba_ref, bx_ref, lam_ref = refs[:12]
        n_out = n_in + len(c_args)
        dproj_ref, sm_ref, dwa_ref, dwx_ref = refs[n_out:n_out + 4]
        n_scr = n_out + 4 + len(c_out_shapes)
        carry_ref, du8_ref, row_scr = refs[n_scr:n_scr + 3]
        i = pl.program_id(1)
        if comm is not None:
            j = pl.program_id(0)
            _comm_steps(comm, refs, n_in, n_out + 4, (j == 0) & (i == 0), (j == 3) & (i == 0), (j == 3) & (i == nt - 1))

        @pl.when(i == 0)
        def _():
            carry_ref[...] = jnp.zeros_like(carry_ref)
            du8_ref[...] = jnp.zeros_like(du8_ref)
            sm_ref[...] = jnp.zeros_like(sm_ref)
            dwa_ref[...] = jnp.zeros_like(dwa_ref)
            dwx_ref[...] = jnp.zeros_like(dwx_ref)

        tile0 = i == nt - 1
        xp = xg_ref[:, 0:256]
        lg = xg_ref[:, 256:512]
        prev8 = jnp.where(tile0, 0.0, xp_ref[:, 0:256])
        u, ra, ia, sp, a, m2, mult = _lru_gates(xp, prev8, cw_ref, cb_ref, wa_ref, wx_ref, ba_ref, bx_ref, lam_ref)
        h = hl_ref[...]
        hprev = _shift_down(h, jnp.where(tile0, 0.0, hp_ref[...]), 1)
        dya_v = dya_ref[...]
        g, dg = _gelu_and_grad(lg)
        ac, lc = _lin_scan(_shift_up(a, carry_ref[...], 1), dya_v * g, True)
        lam_v = lc + ac * carry_ref[1:2, :]
        row_scr[0:8, :] = lam_v[0:8]
        row_scr[8:16, :] = a[0:8]
        carry_ref[1:2, :] = row_scr[0:1, :]
        carry_ref[0:1, :] = row_scr[8:9, :]
        da = lam_v * hprev
        dmult = lam_v * ia * u
        dia = lam_v * mult * u
        dlog = da * a - dmult * (1.0 - m2) / mult
        dra = -LRU_C * sp * dlog
        dpa = dra * ra * (1.0 - ra)
        dpx = dia * ia * (1.0 - ia)
        du = lam_v * mult * ia + _dot_nt(dpa, wa_ref[0]) + _dot_nt(dpx, wx_ref[0])
        dwa_ref[0] += _dot_tn(u, dpa)
        dwx_ref[0] += _dot_tn(u, dpx)
        dlx = du * cw_ref[3:4, :]
        sm_ref[24:32, :] += _rsum8(du * xp)
        for k in (1, 2, 3):
            du_k = _shift_up(du, du8_ref[...], k)
            dlx = dlx + du_k * cw_ref[3 - k:4 - k, :]
            sm_ref[8 * (3 - k):8 * (4 - k), :] += _rsum8(du_k * xp)
        du8_ref[...] = du[0:8]
        dproj_ref[:, 0:256] = dlx.astype(dproj_ref.dtype)
        dproj_ref[:, 256:512] = (dya_v * h * dg).astype(dproj_ref.dtype)
        sm_ref[32:40, :] += _rsum8(du)
        sm_ref[40:48, :] += _rsum8(dpa)
        sm_ref[48:56, :] += _rsum8(dpx)
        sm_ref[56:64, :] += _rsum8(-LRU_C * ra * dlog) * (-_sigmoid(-lam_ref[...]))

    rev = lambda i: nt - 1 - i
    small = lambda rows: pl.BlockSpec((rows, 256), lambda j, i: (0, j))
    wblk = pl.BlockSpec((1, 256, 256), lambda j, i: (j, 0, 0))
    return pl.pallas_call(
        body, name=name, grid=(4, nt),
        in_specs=[pl.BlockSpec((r, LBLK), lambda j, i: (rev(i), _lblk_col(j))),
                  pl.BlockSpec((8, LBLK), lambda j, i: (jnp.maximum(rev(i) * (r // 8) - 1, 0), _lblk_col(j))),
                  pl.BlockSpec((r, 256), lambda j, i: (rev(i), j)),
                  pl.BlockSpec((8, 256), lambda j, i: (jnp.maximum(rev(i) * (r // 8) - 1, 0), j)),
                  pl.BlockSpec((r, 256), lambda j, i: (rev(i), j)),
                  small(4), small(1), wblk, wblk, small(1), small(1), small(1),
                  pl.BlockSpec(memory_space=pl.ANY)] + c_in_specs,
        out_specs=[pl.BlockSpec((r, LBLK), lambda j, i: (rev(i), _lblk_col(j))),
                   pl.BlockSpec((64, 256), lambda j, i: (0, j)), wblk, wblk] + c_out_specs,
        out_shape=[jax.ShapeDtypeStruct(dproj.shape, dproj.dtype), jax.ShapeDtypeStruct((64, D), F32),
                   jax.ShapeDtypeStruct((4, 256, 256), F32), jax.ShapeDtypeStruct((4, 256, 256), F32)] + c_out_shapes,
        scratch_shapes=[pltpu.VMEM((8, 256), F32), pltpu.VMEM((8, 256), F32), pltpu.VMEM((16, 256), F32)] + c_sems,
        input_output_aliases={n_in - 1: 0},
        compiler_params=_cp(("arbitrary", "arbitrary") if comm is not None else ("parallel", "arbitrary")),
    )(proj, proj, hl, hl, dya, lw["cw"], lw["cb"], lw["wa"], lw["wx"], lw["ba"], lw["bx"], lw["lam"], dproj, *c_args)


def _head_cols(x):
    lane = lax.broadcasted_iota(jnp.int32, x.shape, 1)
    return [jnp.sum(jnp.where(lane == h, x, 0.0), axis=1, keepdims=True) for h in range(N_HEADS)]


def _compact_heads(blocks):
    lane = lax.broadcasted_iota(jnp.int32, blocks[0].shape, 1)
    lo = lane < HEAD_P
    out = jnp.zeros_like(blocks[0])
    for j, blk in enumerate(blocks):
        s_lo = jnp.sum(jnp.where(lo, blk, 0.0), axis=1, keepdims=True)
        s_hi = jnp.sum(jnp.where(lo, 0.0, blk), axis=1, keepdims=True)
        out = jnp.where(lane == 2 * j, s_lo, out)
        out = jnp.where(lane == 2 * j + 1, s_hi, out)
    return out


def _ssd_prelude(dtraw_ref, dtb_ref, alog_ref, dt_scr, a_scr):
    lane = lax.broadcasted_iota(jnp.int32, dt_scr.shape, 1)
    dt = jnp.where(lane < N_HEADS, _softplus(dtraw_ref[...] + dtb_ref[0:1, :]), 0.0)
    dt_scr[...] = dt
    a_scr[...] = dt * (-jnp.exp(alog_ref[0:1, :]))


def _ssd_chunk_scalars(dt_scr, a_scr, r_scr, r0):
    a_c = a_scr[pl.ds(r0, CHUNK), :]
    dt_c = dt_scr[pl.ds(r0, CHUNK), :]
    i0 = lax.broadcasted_iota(jnp.int32, (CHUNK, CHUNK), 0)
    i1 = lax.broadcasted_iota(jnp.int32, (CHUNK, CHUNK), 1)
    tri = jnp.where(i0 >= i1, 1.0, 0.0).astype(F32)
    cs = jnp.dot(tri, a_c, precision=_HI, preferred_element_type=F32)
    lane = lax.broadcasted_iota(jnp.int32, (CHUNK, 128), 1)
    srow = lax.broadcasted_iota(jnp.int32, (CHUNK, 128), 0)
    t_lo = jnp.where((lane < HEAD_P) & (srow <= lane), 1.0, 0.0).astype(F32)
    t_hi = jnp.where((lane >= HEAD_P) & (srow <= lane - HEAD_P), 1.0, 0.0).astype(F32)
    even = (lane % 2) == 0
    tn = (((0,), (0,)), ((), ()))
    r_scr[...] = (lax.dot_general(jnp.where(even, a_c, 0.0), t_lo, tn, precision=_HI, preferred_element_type=F32)
                  + lax.dot_general(jnp.where(even, 0.0, a_c), t_hi, tn, precision=_HI, preferred_element_type=F32))
    return cs, dt_c, _head_cols(cs), _head_cols(dt_c)


def _block_diag2(v):
    lo = lax.broadcasted_iota(jnp.int32, v.shape, 1) < HEAD_P
    return jnp.concatenate([jnp.where(lo, v, 0.0), jnp.where(lo, 0.0, v)], axis=0).astype(_MXU)


def _ssd_pair(xc_scr, r_scr, cs_cols, dt_cols, s2, r0, j, s2t=None):
    lane = lax.broadcasted_iota(jnp.int32, (CHUNK, 128), 1)
    srow = lax.broadcasted_iota(jnp.int32, (CHUNK, 128), 0)
    lo = lane < HEAD_P
    csc = jnp.where(lo, cs_cols[2 * j], cs_cols[2 * j + 1])
    dtc = jnp.where(lo, dt_cols[2 * j], dt_cols[2 * j + 1])
    csr = r_scr[2 * j:2 * j + 1, :] + r_scr[2 * j + 1:2 * j + 2, :]
    dm = jnp.where((lane & (HEAD_P - 1)) <= srow, jnp.exp(jnp.minimum(csc - csr, 0.0)), 0.0)
    xs = xc_scr[pl.ds(r0, CHUNK), j * 128:(j + 1) * 128]
    xd = xs * dtc
    csl = jnp.sum(jnp.where(srow == CHUNK - 1, csc, 0.0), axis=0, keepdims=True)
    out = dict(csc=csc, dtc=dtc, dm=dm, m2=s2 * dm, xs=xs, xd=xd, rhs=_block_diag2(xd), e=jnp.exp(csc),
               w=jnp.exp(csl - csc), dec=jnp.exp(csl))
    if s2t is not None:
        out["mt2"] = s2t * jnp.where((lane & (HEAD_P - 1)) >= srow, jnp.exp(jnp.minimum(csr - csc, 0.0)), 0.0)
    return out


def _cat(parts):
    return jnp.concatenate(parts, axis=1)


def _ssd_fwd(proj, dtraw, sw, *, rb, name, comm=None):
    t = proj.shape[0]
    ns, cb = t // rb, rb // CHUNK
    c_args, c_in_specs, c_out_specs, c_out_shapes, c_sems = _comm_specs(comm)

    def body(*refs):
        zx_ref, zp_ref, dtraw_ref, cw_ref, cbias_ref, dtb_ref, alog_ref, dsk_ref, ng_ref = refs[:9]
        yssd_ref, yb_ref, st_ref = refs[9 + len(c_args):12 + len(c_args)]
        n_scr = 12 + len(c_args) + len(c_out_shapes)
        h_scr, xc_scr, dt_scr, a_scr, r_scr = refs[n_scr:n_scr + 5]
        i = pl.program_id(0)
        if comm is not None:
            _comm_steps(comm, refs, 9, 12 + len(c_args), i == 0, i == (3 * ns) // 4, i == ns - 1)

        @pl.when(i == 0)
        def _():
            h_scr[...] = jnp.zeros_like(h_scr)

        for j in range(XBC // 128):
            cs_, zc = slice(128 * j, 128 * (j + 1)), slice(2048 + 128 * j, 2048 + 128 * (j + 1))
            pre = _conv4(zx_ref[:, zc], jnp.where(i == 0, 0.0, zp_ref[:, zc]), cw_ref, cbias_ref, cs_)
            xc_scr[:, cs_] = pre * _sigmoid(pre)
        _ssd_prelude(dtraw_ref, dtb_ref, alog_ref, dt_scr, a_scr)

        def chunk(c, carry):
            r0 = pl.multiple_of(c * CHUNK, CHUNK)
            _, _, cs_cols, dt_cols = _ssd_chunk_scalars(dt_scr, a_scr, r_scr, r0)
            st_ref[c] = h_scr[...]
            for g in range(N_GROUPS):
                bg = xc_scr[pl.ds(r0, CHUNK), 2048 + 128 * g:2048 + 128 * (g + 1)]
                cg = xc_scr[pl.ds(r0, CHUNK), 2560 + 128 * g:2560 + 128 * (g + 1)]
                s2 = _dot_nt(cg, jnp.concatenate([bg, bg], axis=0))
                hp = h_scr[:, 512 * g:512 * (g + 1)]
                yoff = _dot(cg, hp)
                xdw, dec = [], []
                for jj in range(4):
                    j = 4 * g + jj
                    p = _ssd_pair(xc_scr, r_scr, cs_cols, dt_cols, s2, r0, j)
                    y = _dot(p["m2"], p["rhs"]) + yoff[:, 128 * jj:128 * (jj + 1)] * p["e"]
                    yssd_ref[pl.ds(r0, CHUNK), 128 * j:128 * (j + 1)] = y + dsk_ref[0:1, 128 * j:128 * (j + 1)] * p["xs"]
                    xdw.append(p["xd"] * p["w"])
                    dec.append(p["dec"])
                h_scr[:, 512 * g:512 * (g + 1)] = hp * _cat(dec) + _dot_tn(bg, _cat(xdw))
            return carry

        lax.fori_loop(0, cb, chunk, 0)
        for g in range(N_GROUPS):
            sl = slice(512 * g, 512 * (g + 1))
            for q in range(rb // NORM_ROWS):
                rw = slice(NORM_ROWS * q, NORM_ROWS * (q + 1))
                yz = yssd_ref[rw, sl] * _silu(zx_ref[rw, sl])
                rg = lax.rsqrt(jnp.mean(yz * yz, axis=-1, keepdims=True) + EPS)
                yb_ref[rw, sl] = (yz * rg * ng_ref[0:1, sl]).astype(yb_ref.dtype)

    full = lambda rows, cols: pl.BlockSpec((rows, cols), lambda i: (0, 0))
    return pl.pallas_call(
        body, name=name, grid=(ns,),
        in_specs=[pl.BlockSpec((rb, ZX_W), lambda i: (i, 0)),
                  pl.BlockSpec((8, ZX_W), lambda i: (jnp.maximum(i * (rb // 8) - 1, 0), 0)),
                  pl.BlockSpec((rb, DT_PAD), lambda i: (i, 0)),
                  full(4, XBC), full(1, XBC), full(1, DT_PAD), full(1, DT_PAD), full(1, SSD_INNER), full(1, SSD_INNER)]
        + c_in_specs,
        out_specs=[pl.BlockSpec((rb, SSD_INNER), lambda i: (i, 0)), pl.BlockSpec((rb, SSD_INNER), lambda i: (i, 0)),
                   pl.BlockSpec((cb, N_STATE, SSD_INNER), lambda i: (i, 0, 0))] + c_out_specs,
        out_shape=[jax.ShapeDtypeStruct((t, SSD_INNER), F32), jax.ShapeDtypeStruct((t, SSD_INNER), _MXU),
                   jax.ShapeDtypeStruct((t // CHUNK, N_STATE, SSD_INNER), F32)] + c_out_shapes,
        scratch_shapes=[pltpu.VMEM((N_STATE, SSD_INNER), F32), pltpu.VMEM((rb, XBC), F32), pltpu.VMEM((rb, DT_PAD), F32),
                        pltpu.VMEM((rb, DT_PAD), F32), pltpu.VMEM((128, 128), F32)] + c_sems,
        compiler_params=_cp(("arbitrary",)),
    )(proj, proj, dtraw, sw["cw"], sw["cb"], sw["dtb"], sw["alog"], sw["dsk"], sw["ng"], *c_args)


def _ssd_bwd(proj, dtraw, yssd, states, dyb, dproj, sw, *, rb, name, comm=None):
    t = proj.shape[0]
    ns, cb = t // rb, rb // CHUNK
    c_args, c_in_specs, c_out_specs, c_out_shapes, c_sems = _comm_specs(comm)
    n_in = 13

    def body(*refs):
        zx_ref, zp_ref, dtraw_ref, yssd_ref, st_ref, dyb_ref, cw_ref, cbias_ref, dtb_ref, alog_ref, dsk_ref, ng_ref = refs[:12]
        n_out = n_in + len(c_args)
        dzx_ref, ddt_ref, gconv_ref, gch_ref, ghd_ref = refs[n_out:n_out + 5]
        n_scr = n_out + 5 + len(c_out_shapes)
        dht_scr, xc_scr, dsl_scr, dy_scr, dxc_scr, dt_scr, a_scr, r_scr, dp8_scr = refs[n_scr:n_scr + 9]
        i = pl.program_id(0)
        if comm is not None:
            _comm_steps(comm, refs, n_in, n_out + 5, i == 0, i == (3 * ns) // 4, i == ns - 1)

        @pl.when(i == 0)
        def _():
            dht_scr[...] = jnp.zeros_like(dht_scr)
            dp8_scr[...] = jnp.zeros_like(dp8_scr)
            gconv_ref[...] = jnp.zeros_like(gconv_ref)
            gch_ref[...] = jnp.zeros_like(gch_ref)
            ghd_ref[...] = jnp.zeros_like(ghd_ref)

        tile0 = i == ns - 1
        for j in range(XBC // 128):
            cs_, zc = slice(128 * j, 128 * (j + 1)), slice(2048 + 128 * j, 2048 + 128 * (j + 1))
            pre = _conv4(zx_ref[:, zc], jnp.where(tile0, 0.0, zp_ref[:, zc]), cw_ref, cbias_ref, cs_)
            sg = _sigmoid(pre)
            xc_scr[:, cs_] = pre * sg
            dsl_scr[:, cs_] = sg * (1.0 + pre * (1.0 - sg))
        _ssd_prelude(dtraw_ref, dtb_ref, alog_ref, dt_scr, a_scr)

        for g in range(N_GROUPS):
            sl = slice(512 * g, 512 * (g + 1))
            for q in range(rb // NORM_ROWS):
                rw = slice(NORM_ROWS * q, NORM_ROWS * (q + 1))
                zv = zx_ref[rw, sl]
                ys = yssd_ref[rw, sl]
                sg = _sigmoid(zv)
                sz = zv * sg
                yz = ys * sz
                rg = lax.rsqrt(jnp.mean(yz * yz, axis=-1, keepdims=True) + EPS)
                yn = yz * rg
                dyb_v = dyb_ref[rw, sl]
                gch_ref[0:8, sl] += _rsum8(dyb_v * yn)
                dyn = dyb_v * ng_ref[0:1, sl]
                dyz = rg * (dyn - yn * jnp.mean(dyn * yn, axis=-1, keepdims=True))
                dy_scr[rw, sl] = dyz * sz
                dzx_ref[rw, sl] = (dyz * ys * (sg * (1.0 + zv * (1.0 - sg)))).astype(dzx_ref.dtype)

        a_row = -jnp.exp(alog_ref[0:1, :])

        def chunk(cc, carry):
            c = cb - 1 - cc
            r0 = pl.multiple_of(c * CHUNK, CHUNK)
            rows = pl.ds(r0, CHUNK)
            _, dt_c, cs_cols, dt_cols = _ssd_chunk_scalars(dt_scr, a_scr, r_scr, r0)
            lane = lax.broadcasted_iota(jnp.int32, (CHUNK, 128), 1)
            srow = lax.broadcasted_iota(jnp.int32, (CHUNK, 128), 0)
            lo = lane < HEAD_P
            last = srow == CHUNK - 1
            p1_blocks, p3_blocks = [], []
            for g in range(N_GROUPS):
                gs = slice(512 * g, 512 * (g + 1))
                bg = xc_scr[rows, 2048 + 128 * g:2048 + 128 * (g + 1)]
                cg = xc_scr[rows, 2560 + 128 * g:2560 + 128 * (g + 1)]
                b2 = jnp.concatenate([bg, bg], axis=0)
                s2 = _dot_nt(cg, b2)
                s2t = _dot_nt(bg, jnp.concatenate([cg, cg], axis=0))
                hp = st_ref[c, :, gs]
                dht = dht_scr[:, gs]
                yoff = _dot(cg, hp)
                ps = [_ssd_pair(xc_scr, r_scr, cs_cols, dt_cols, s2, r0, 4 * g + jj, s2t) for jj in range(4)]
                dys = [dy_scr[rows, 128 * (4 * g + jj):128 * (4 * g + jj + 1)] for jj in range(4)]
                dye = _cat([dys[jj] * ps[jj]["e"] for jj in range(4)])
                w_g = _cat([p["w"] for p in ps])
                dcg = _dot_nt(dye, hp)
                dht_scr[:, gs] = _dot_tn(cg, dye) + _cat([p["dec"] for p in ps]) * dht
                dxd_state = w_g * _dot(bg, dht)
                dbg = _dot_nt(_cat([p["xd"] for p in ps]) * w_g, dht)
                tsum = _rsum(dht * hp)
                ds2 = jnp.zeros((CHUNK, 128), F32)
                for jj in range(4):
                    j = 4 * g + jj
                    ls = slice(128 * j, 128 * (j + 1))
                    p, dy2 = ps[jj], dys[jj]
                    dy_bd = _block_diag2(dy2)
                    dm2 = _dot_nt(dy2, p["rhs"])
                    ds2 = ds2 + dm2 * p["dm"]
                    gdiff = dm2 * p["m2"] - _dot_nt(p["xd"], dy_bd) * p["mt2"]
                    dxs = dxd_state[:, 128 * jj:128 * (jj + 1)]
                    dxd = _dot(p["mt2"], dy_bd) + dxs
                    end_row = _rsum(p["xd"] * dxs) + p["dec"] * tsum[:, 128 * jj:128 * (jj + 1)]
                    p1_blocks.append(gdiff + dy2 * yoff[:, 128 * jj:128 * (jj + 1)] * p["e"] - p["xd"] * dxs
                                     + jnp.where(last, end_row, 0.0))
                    p3_blocks.append(dxd * p["xs"])
                    dxc_scr[rows, ls] = dxd * p["dtc"] + dy2 * dsk_ref[0:1, ls]
                    gch_ref[8:16, ls] += _rsum8(dy2 * p["xs"])
                dcg = dcg + _dot(ds2, b2)
                rb2 = _dot_tn(ds2, cg)
                dxc_scr[rows, 2048 + 128 * g:2048 + 128 * (g + 1)] = dbg + rb2[0:CHUNK] + rb2[CHUNK:2 * CHUNK]
                dxc_scr[rows, 2560 + 128 * g:2560 + 128 * (g + 1)] = dcg
            dcs = _compact_heads(p1_blocks)
            i0 = lax.broadcasted_iota(jnp.int32, (CHUNK, CHUNK), 0)
            i1 = lax.broadcasted_iota(jnp.int32, (CHUNK, CHUNK), 1)
            triu = jnp.where(i1 >= i0, 1.0, 0.0).astype(F32)
            da = jnp.dot(triu, dcs, precision=_HI, preferred_element_type=F32)
            ddt = _compact_heads(p3_blocks) + da * a_row
            ddtraw = jnp.where(lane < N_HEADS, ddt * _sigmoid(dtraw_ref[rows, :] + dtb_ref[0:1, :]), 0.0)
            ddt_ref[rows, :] = ddtraw.astype(ddt_ref.dtype)
            ghd_ref[0:1, :] += _rsum(ddtraw)
            ghd_ref[1:2, :] += _rsum(da * dt_c) * a_row
            return carry

        lax.fori_loop(0, cb, chunk, 0)
        for j in range(XBC // 128):
            cs_, zc = slice(128 * j, 128 * (j + 1)), slice(2048 + 128 * j, 2048 + 128 * (j + 1))
            dpre = dxc_scr[:, cs_] * dsl_scr[:, cs_]
            xraw = zx_ref[:, zc]
            dx = dpre * cw_ref[3:4, cs_]
            gconv_ref[24:32, cs_] += _rsum8(dpre * xraw)
            for k in (1, 2, 3):
                dpre_k = _shift_up(dpre, dp8_scr[:, cs_], k)
                dx = dx + dpre_k * cw_ref[3 - k:4 - k, cs_]
                gconv_ref[8 * (3 - k):8 * (4 - k), cs_] += _rsum8(dpre_k * xraw)
            dzx_ref[:, zc] = dx.astype(dzx_ref.dtype)
            dp8_scr[:, cs_] = dpre[0:8]
            gconv_ref[32:40, cs_] += _rsum8(dpre)

    rev = lambda i: ns - 1 - i
    full = lambda rows, cols: pl.BlockSpec((rows, cols), lambda i: (0, 0))
    return pl.pallas_call(
        body, name=name, grid=(ns,),
        in_specs=[pl.BlockSpec((rb, ZX_W), lambda i: (rev(i), 0)),
                  pl.BlockSpec((8, ZX_W), lambda i: (jnp.maximum(rev(i) * (rb // 8) - 1, 0), 0)),
                  pl.BlockSpec((rb, DT_PAD), lambda i: (rev(i), 0)),
                  pl.BlockSpec((rb, SSD_INNER), lambda i: (rev(i), 0)),
                  pl.BlockSpec((cb, N_STATE, SSD_INNER), lambda i: (rev(i), 0, 0)),
                  pl.BlockSpec((rb, SSD_INNER), lambda i: (rev(i), 0)),
                  full(4, XBC), full(1, XBC), full(1, DT_PAD), full(1, DT_PAD), full(1, SSD_INNER), full(1, SSD_INNER),
                  pl.BlockSpec(memory_space=pl.ANY)] + c_in_specs,
        out_specs=[pl.BlockSpec((rb, ZX_W), lambda i: (rev(i), 0)), pl.BlockSpec((rb, DT_PAD), lambda i: (rev(i), 0)),
                   full(40, XBC), full(16, SSD_INNER), full(8, DT_PAD)] + c_out_specs,
        out_shape=[jax.ShapeDtypeStruct(dproj.shape, dproj.dtype), jax.ShapeDtypeStruct((t, DT_PAD), _MXU),
                   jax.ShapeDtypeStruct((40, XBC), F32), jax.ShapeDtypeStruct((16, SSD_INNER), F32),
                   jax.ShapeDtypeStruct((8, DT_PAD), F32)] + c_out_shapes,
        scratch_shapes=[pltpu.VMEM((N_STATE, SSD_INNER), F32), pltpu.VMEM((rb, XBC), F32), pltpu.VMEM((rb, XBC), F32),
                        pltpu.VMEM((rb, SSD_INNER), F32), pltpu.VMEM((rb, XBC), F32), pltpu.VMEM((rb, DT_PAD), F32),
                        pltpu.VMEM((rb, DT_PAD), F32), pltpu.VMEM((128, 128), F32), pltpu.VMEM((8, XBC), F32)] + c_sems,
        input_output_aliases={n_in - 1: 0},
        compiler_params=_cp(("arbitrary",)),
    )(proj, proj, dtraw, yssd, states, dyb, sw["cw"], sw["cb"], sw["dtb"], sw["alog"], sw["dsk"], sw["ng"], dproj, *c_args)


def _branch_merge(ya, yb, proj, wba, wbb, bgate, *, tm, tn, name):
    t = ya.shape[0]
    nj = D // tn

    def body(ya_ref, yb_ref, ga_ref, gb_ref, wba_ref, wbb_ref, ba_ref, bb_ref, ta_ref, tb_ref, mg_ref):
        ta = _dot(ya_ref[...], wba_ref[...])
        tb = _dot(yb_ref[...], wbb_ref[...])
        ta_ref[...] = ta.astype(ta_ref.dtype)
        tb_ref[...] = tb.astype(tb_ref.dtype)
        ga = _sigmoid(ga_ref[...] + ba_ref[...])
        gb = _sigmoid(gb_ref[...] + bb_ref[...])
        mg_ref[...] = (ga * ta + gb * tb).astype(mg_ref.dtype)

    tile = pl.BlockSpec((tm, tn), lambda i, j: (i, j))
    return pl.pallas_call(
        body, name=name, grid=(t // tm, nj),
        in_specs=[pl.BlockSpec((tm, D), lambda i, j: (i, 0)), pl.BlockSpec((tm, SSD_INNER), lambda i, j: (i, 0)),
                  pl.BlockSpec((tm, tn), lambda i, j: (i, G0 // tn + j)),
                  pl.BlockSpec((tm, tn), lambda i, j: (i, (G0 + D) // tn + j)),
                  pl.BlockSpec((D, tn), lambda i, j: (0, j)), pl.BlockSpec((SSD_INNER, tn), lambda i, j: (0, j)),
                  pl.BlockSpec((1, tn), lambda i, j: (0, j)), pl.BlockSpec((1, tn), lambda i, j: (0, nj + j))],
        out_specs=[tile, tile, tile],
        out_shape=[jax.ShapeDtypeStruct((t, D), _MXU)] * 3,
        compiler_params=_cp(("parallel", "parallel")),
    )(ya, yb, proj, proj, wba, wbb, bgate, bgate)


def _swiglu_mm(gu, wfo, residual, *, tm, tn, name):
    t = gu.shape[0]

    def body(gu_ref, w_ref, r_ref, act_ref, o_ref):
        @pl.when(pl.program_id(1) == 0)
        def _():
            gate = gu_ref[:, 0:D_FF].astype(F32)
            act_ref[...] = (_silu(gate) * gu_ref[:, D_FF:2 * D_FF].astype(F32)).astype(act_ref.dtype)
        o_ref[...] = jnp.dot(act_ref[...], w_ref[...], preferred_element_type=F32) + r_ref[...]

    return pl.pallas_call(
        body, name=name, grid=(t // tm, D // tn),
        in_specs=[pl.BlockSpec((tm, 2 * D_FF), lambda i, j: (i, 0)), pl.BlockSpec((D_FF, tn), lambda i, j: (0, j)),
                  pl.BlockSpec((tm, tn), lambda i, j: (i, j))],
        out_specs=[pl.BlockSpec((tm, D_FF), lambda i, j: (i, 0)), pl.BlockSpec((tm, tn), lambda i, j: (i, j))],
        out_shape=[jax.ShapeDtypeStruct((t, D_FF), _MXU), jax.ShapeDtypeStruct((t, D), F32)],
        compiler_params=_cp(("parallel", "arbitrary")),
    )(gu, wfo, residual)


def _ffn_bwd_act(dh, wfo, gu, *, tm, name, comm=None):
    t = dh.shape[0]
    ni = t // tm
    c_args, c_in_specs, c_out_specs, c_out_shapes, c_sems = _comm_specs(comm)

    def body(*refs):
        dh_ref, w_ref, gu_ref = refs[:3]
        o_ref = refs[3 + len(c_args)]
        if comm is not None:
            i = pl.program_id(0)
            _comm_steps(comm, refs, 3, 4 + len(c_args), i == 0, i == (3 * ni) // 4, i == ni - 1)
        dact = _dot_nt(dh_ref[...], w_ref[...])
        g = gu_ref[:, 0:D_FF].astype(F32)
        u = gu_ref[:, D_FF:2 * D_FF].astype(F32)
        sg = _sigmoid(g)
        o_ref[:, 0:D_FF] = (dact * u * (sg * (1.0 + g * (1.0 - sg)))).astype(o_ref.dtype)
        o_ref[:, D_FF:2 * D_FF] = (dact * (g * sg)).astype(o_ref.dtype)

    return pl.pallas_call(
        body, name=name, grid=(ni,),
        in_specs=[pl.BlockSpec((tm, D), lambda i: (i, 0)), pl.BlockSpec((D_FF, D), lambda i: (0, 0)),
                  pl.BlockSpec((tm, 2 * D_FF), lambda i: (i, 0))] + c_in_specs,
        out_specs=[pl.BlockSpec((tm, 2 * D_FF), lambda i: (i, 0))] + c_out_specs,
        out_shape=[jax.ShapeDtypeStruct((t, 2 * D_FF), _MXU)] + c_out_shapes,
        scratch_shapes=c_sems,
        compiler_params=_cp(("arbitrary",) if comm is not None else ("parallel",)),
    )(dh, wfo, gu, *c_args)


def _outproj_bwd(dh, wout, ta, tb, proj, bgate, dproj, *, tm, name):
    t = dh.shape[0]

    def body(dh_ref, w_ref, ta_ref, tb_ref, g_ref, b_ref, dta_ref, dtb_ref, dg_ref, db_ref):
        @pl.when(pl.program_id(0) == 0)
        def _():
            db_ref[...] = jnp.zeros_like(db_ref)
        dm = _dot_nt(dh_ref[...], w_ref[...])
        ga = _sigmoid(g_ref[:, 0:D] + b_ref[:, 0:D])
        gb = _sigmoid(g_ref[:, D:2 * D] + b_ref[:, D:2 * D])
        dta_ref[...] = (dm * ga).astype(dta_ref.dtype)
        dtb_ref[...] = (dm * gb).astype(dtb_ref.dtype)
        dga = dm * ta_ref[...].astype(F32) * ga * (1.0 - ga)
        dgb = dm * tb_ref[...].astype(F32) * gb * (1.0 - gb)
        dg_ref[:, 0:D] = dga.astype(dg_ref.dtype)
        dg_ref[:, D:2 * D] = dgb.astype(dg_ref.dtype)
        db_ref[0:1, 0:D] += _rsum(dga)
        db_ref[0:1, D:2 * D] += _rsum(dgb)

    row = lambda cols: pl.BlockSpec((tm, cols), lambda i: (i, 0))
    return pl.pallas_call(
        body, name=name, grid=(t // tm,),
        in_specs=[row(D), pl.BlockSpec((D, D), lambda i: (0, 0)), row(D), row(D),
                  pl.BlockSpec((tm, 2 * D), lambda i: (i, G0 // (2 * D))), pl.BlockSpec((1, 2 * D), lambda i: (0, 0))],
        out_specs=[row(D), row(D), pl.BlockSpec((tm, 2 * D), lambda i: (i, G0 // (2 * D))),
                   pl.BlockSpec((8, 2 * D), lambda i: (0, 0))],
        out_shape=[jax.ShapeDtypeStruct((t, D), _MXU), jax.ShapeDtypeStruct((t, D), _MXU),
                   jax.ShapeDtypeStruct(dproj, _MXU), jax.ShapeDtypeStruct((8, 2 * D), F32)],
        compiler_params=_cp(("arbitrary",)),
    )(dh, wout, ta, tb, proj, bgate)


def _loss_head(h, gf, target, *, tm, name):
    t = h.shape[0]

    def body(h_ref, g_ref, t_ref, loss_ref, dg_ref, dh_ref):
        @pl.when(pl.program_id(0) == 0)
        def _():
            loss_ref[...] = jnp.zeros_like(loss_ref)
            dg_ref[...] = jnp.zeros_like(dg_ref)
        x = h_ref[...]
        r = lax.rsqrt(jnp.mean(x * x, axis=-1, keepdims=True) + EPS)
        xh = x * r
        err = xh * g_ref[...] - t_ref[...]
        loss_ref[...] += 0.5 * jnp.sum(jnp.mean(err * err, axis=-1, keepdims=True), axis=0, keepdims=True)
        dy = err * (1.0 / D)
        dg_ref[0:1, :] += _rsum(dy * xh)
        dxh = dy * g_ref[...]
        dh_ref[...] = r * (dxh - xh * jnp.mean(dxh * xh, axis=-1, keepdims=True))

    row = pl.BlockSpec((tm, D), lambda i: (i, 0))
    return pl.pallas_call(
        body, name=name, grid=(t // tm,),
        in_specs=[row, pl.BlockSpec((1, D), lambda i: (0, 0)), row],
        out_specs=[pl.BlockSpec((8, 128), lambda i: (0, 0)), pl.BlockSpec((8, D), lambda i: (0, 0)), row],
        out_shape=[jax.ShapeDtypeStruct((8, 128), F32), jax.ShapeDtypeStruct((8, D), F32), jax.ShapeDtypeStruct((t, D), F32)],
        compiler_params=_cp(("arbitrary",)),
    )(h, gf, target)


def _row_tile(rows, cols, limit_bytes=1 << 20):
    best = None
    for tr in range(8, rows + 1, 8):
        if rows % tr == 0 and tr * cols * 4 <= limit_bytes:
            best = tr
    return best if best is not None else rows


def _adamw(w, g, m, v, *, name):
    rows, cols = w.shape
    tr = _row_tile(rows, cols)

    def body(w_ref, g_ref, m_ref, v_ref, d_ref, nm_ref, nv_ref):
        gv = g_ref[...]
        nm = ADAM_B1 * m_ref[...] + (1.0 - ADAM_B1) * gv
        nv = ADAM_B2 * v_ref[...] + (1.0 - ADAM_B2) * (gv * gv)
        m_hat = nm / (1.0 - ADAM_B1 ** ADAM_STEP)
        v_hat = nv / (1.0 - ADAM_B2 ** ADAM_STEP)
        d_ref[...] = -ADAM_LR * (m_hat / (jnp.sqrt(v_hat) + ADAM_EPS) + ADAM_WD * w_ref[...])
        nm_ref[...] = nm
        nv_ref[...] = nv

    blk = pl.BlockSpec((tr, cols), lambda i: (i, 0))
    shp = jax.ShapeDtypeStruct((rows, cols), F32)
    return pl.pallas_call(
        body, name=name, grid=(rows // tr,), in_specs=[blk] * 4, out_specs=[blk] * 3, out_shape=[shp] * 3,
        compiler_params=_cp(("parallel",)),
    )(w, g, m, v)


def _bd256(w):
    w4 = w.reshape(4, 4, 64, 64)
    eye = jnp.eye(4, dtype=w.dtype)
    return (w4[:, :, :, None, :] * eye[None, :, None, :, None]).reshape(4, 256, 256)


def _bd256_diag(g):
    g5 = g.reshape(4, 4, 64, 4, 64)
    return jnp.stack([g5[:, a, :, a, :] for a in range(4)], axis=1).reshape(16, 64, 64)


FFN_SHARD = 2 * D_FF // 4
W_IN_SHARD = IN_DIM // 4
W_IN_ROWS = 9344


def _w_in_cols(shards, c0, c1):
    out = []
    for p in range(4):
        lo, hi = max(c0, W_IN_SHARD * p), min(c1, W_IN_SHARD * (p + 1))
        if lo < hi:
            out.append(shards[p][:, lo - W_IN_SHARD * p:hi - W_IN_SHARD * p])
    return out


def _in_proj_weights(win):
    lblk = [_w_in_cols(win, 256 * j, 256 * (j + 1)) + _w_in_cols(win, D + 256 * j, D + 256 * (j + 1)) for j in range(4)]
    wp = jnp.concatenate(_w_in_cols(win, 2048, 4096) + _w_in_cols(win, 4096, 7168) + lblk[0] + lblk[1]
                         + _w_in_cols(win, 7200, 9248) + lblk[2] + lblk[3], axis=1)
    wdt = jnp.pad(jnp.concatenate(_w_in_cols(win, 7168, 7200), axis=1), ((0, 0), (0, DT_PAD - N_HEADS)))
    return wp, wdt


def _layer_weights(w, conv, small, l, wp, wdt):
    row = lambda v: v.reshape(1, -1)
    pad_h = lambda v: jnp.pad(v.reshape(1, -1), ((0, 0), (0, DT_PAD - N_HEADS)))
    lw = dict(cw=conv["lru_conv_w"][l], cb=row(small["lru_conv_b"][l]),
              wa=_bd256(small["lru_w_a"][l]).astype(_MXU), wx=_bd256(small["lru_w_x"][l]).astype(_MXU),
              ba=row(small["lru_b_a"][l]), bx=row(small["lru_b_x"][l]), lam=row(small["lru_lambda"][l]))
    sw = dict(cw=conv["ssd_conv_w"][l], cb=row(small["ssd_conv_b"][l]), dtb=pad_h(small["ssd_dt_bias"][l]),
              alog=pad_h(small["ssd_A_log"][l]), dsk=row(jnp.repeat(small["ssd_D"][l], HEAD_P)),
              ng=row(small["ssd_norm_g"][l]))
    return dict(wp=wp, wdt=wdt, lw=lw, sw=sw, wba=w["w_branch"][0:D], wbb=w["w_branch"][D:3 * D],
                wout=w["w_out"], wfi=w["w_ffn_in"], wfo=w["w_ffn_out"],
                g1=row(small["norm1_g"][l]), g2=row(small["norm2_g"][l]), bgate=row(small["b_gate"][l]))


def _tiles(t):
    return dict(tmn=min(1024, t), tm=min(512, t), tm2=min(256, t), r=min(256, t), rb=min(128, t))


def _layer_fwd(h, w, conv, small, l, carried=None):
    tl = _tiles(h.shape[0])
    n = f"l{l}_"
    carried = carried or {}
    arrived = []

    def carry(kernel, key, n_main, *args, **kw):
        comm, finish = carried.get(key, (None, None))
        outs = list(kernel(*args, comm=comm, **kw))
        if comm is not None:
            arrived.append(finish(outs[n_main:]))
        return outs[:n_main]

    wp, wdt = _in_proj_weights(w["w_in"])
    xn, proj = carry(_norm_mm, "in_proj", 2, h, small["norm1_g"][l].reshape(1, -1), wp, tm=tl["tmn"], tn=1024,
                     name=n + "in_proj")
    w = dict(w)
    for layer, ws in arrived:
        if layer == l:
            w.update(ws)
    lwt = _layer_weights(w, conv, small, l, wp, wdt)
    dtraw = _mm_nn(xn, lwt["wdt"], tm=tl["tm"], tn=DT_PAD, name=n + "dt_proj")
    hl, ya = carry(_lru_fwd, "lru", 2, proj, lwt["lw"], r=tl["r"], name=n + "lru_fwd")
    yssd, yb, states = carry(_ssd_fwd, "ssd", 3, proj, dtraw, lwt["sw"], rb=tl["rb"], name=n + "ssd_fwd")
    ta, tb, merged = _branch_merge(ya, yb, proj, lwt["wba"], lwt["wbb"], lwt["bgate"], tm=tl["tm"], tn=512, name=n + "merge")
    hmid = _mm_nn(merged, lwt["wout"], tm=tl["tm"], tn=512, name=n + "out_proj", residual=h)
    xn2, gu = _norm_mm(hmid, lwt["g2"], lwt["wfi"], tm=tl["tmn"], tn=FFN_SHARD, name=n + "ffn_in", out_dtype=_MXU)
    act, hout = _swiglu_mm(gu, lwt["wfo"], hmid, tm=tl["tm"], tn=512, name=n + "ffn_out")
    saved = dict(h=h, xn=xn, proj=proj, dtraw=dtraw, hl=hl, ya=ya, yssd=yssd, yb=yb, states=states, ta=ta, tb=tb,
                 merged=merged, hmid=hmid, xn2=xn2, gu=gu, act=act)
    return hout, saved, lwt, [x for x in arrived if x[0] != l]


def _layer_bwd(dh, s, lwt, l, hooks=None):
    t = dh.shape[0]
    tl = _tiles(t)
    n = f"l{l}_"
    tt = tl["tm"]
    big = {}
    hooks = hooks or {}

    def wgrad(key, a, b, name, **kw):
        big[key] = _wgrad(a, b, tt=tt, name=n + name, into=big.get(key), **kw)

    dgu, = _ffn_bwd_act(dh, lwt["wfo"], s["gu"], tm=tl["tm2"], name=n + "ffn_act_bwd")
    wgrad("w_ffn_out", s["act"], dh, "ffn_out_wgrad", ta=D_FF, tn=1024, out_shape=(D_FF, D),
          out_block=(D_FF, 1024), out_index=lambda o, j: (o, j))
    wgrad("w_ffn_in", s["xn2"], dgu, "ffn_in_wgrad", ta=D, tn=FFN_SHARD, out_shape=(4, D, FFN_SHARD),
          out_block=(None, D, FFN_SHARD), out_index=lambda o, j: (j, o, 0))
    dh1, dg2 = _mm_nt_rmsbwd(dgu, lwt["wfi"], s["hmid"], lwt["g2"], dh, tm=tl["tm"], tk=FFN_SHARD, name=n + "ffn_in_dgrad")
    dta, dtb, dproj, dbg = _outproj_bwd(dh1, lwt["wout"], s["ta"], s["tb"], s["proj"], lwt["bgate"], (t, NP),
                                        tm=tl["tm2"], name=n + "out_proj_bwd")
    rows_d = dict(ta=D, tn=512, out_block=(D, 512), out_index=lambda o, j: (o, j))
    wgrad("w_out", s["merged"], dh1, "out_proj_wgrad", out_shape=(D, D), **rows_d)
    dya = _mm_nt(dta, lwt["wba"], tm=tl["tm"], name=n + "branch_a_dgrad")
    dyb = _mm_nt(dtb, lwt["wbb"], tm=tl["tm"], name=n + "branch_b_dgrad")
    wgrad("w_branch", s["ya"], dta, "branch_a_wgrad", out_shape=(3 * D, D), a_tab=[0], o_tab=[0], **rows_d)
    wgrad("w_branch", s["yb"], dtb, "branch_b_wgrad", out_shape=(3 * D, D), a_tab=[0, 1], o_tab=[1, 2], **rows_d)
    comm_1 = hooks["lru"](big) if "lru" in hooks else None
    dproj, lsm, dwa, dwx, *got_1 = _lru_bwd(s["proj"], s["hl"], dya, dproj, lwt["lw"], r=tl["r"], name=n + "lru_bwd",
                                            comm=comm_1)
    comm_2 = hooks["ssd"](got_1) if "ssd" in hooks else None
    dproj, ddt, gconv, gch, ghd, *got_2 = _ssd_bwd(s["proj"], s["dtraw"], s["yssd"], s["states"], dyb, dproj, lwt["sw"],
                                                   rb=tl["rb"], name=n + "ssd_bwd", comm=comm_2)
    lsm = lsm.reshape(8, 8, D).sum(axis=1)
    gconv = gconv.reshape(5, 8, XBC).sum(axis=1)
    gch = gch.reshape(2, 8, SSD_INNER).sum(axis=1)
    w_in = dict(tn=D, out_shape=(W_IN_ROWS, D), out_index=lambda o, j: (o, j))
    wgrad("w_in", dproj, s["xn"], "in_proj_wgrad", ta=1024, out_block=(1024, D),
          a_tab=list(range(9)), o_tab=[2, 3, 4, 5, 6, 0, 7, 8, 1], **w_in)
    wgrad("w_in", ddt, s["xn"], "dt_proj_wgrad", ta=DT_PAD, out_block=(DT_PAD, D), a_tab=[0],
          o_tab=[NP // DT_PAD], **w_in)
    comm_3 = hooks["in_dgrad"](big) if "in_dgrad" in hooks else None
    dh0, dg1, *got_3 = _mm_nt_rmsbwd(dproj, lwt["wp"], s["h"], lwt["g1"], dh1, tm=tl["tm"], tk=2304,
                                     name=n + "in_proj_dgrad", extra=(ddt, lwt["wdt"]), comm=comm_3)
    grads = dict(
        lru_conv_w=lsm[0:4], lru_conv_b=lsm[4], lru_b_a=lsm[5], lru_b_x=lsm[6], lru_lambda=lsm[7],
        lru_w_a=_bd256_diag(dwa), lru_w_x=_bd256_diag(dwx),
        ssd_conv_w=gconv[0:4], ssd_conv_b=gconv[4], ssd_norm_g=gch[0], ssd_D=gch[1].reshape(N_HEADS, HEAD_P).sum(axis=-1),
        ssd_dt_bias=ghd[0, 0:N_HEADS], ssd_A_log=ghd[1, 0:N_HEADS],
        b_gate=dbg[0], norm1_g=dg1[0], norm2_g=dg2[0])
    return dh0, grads, big, (got_2, got_3)


def _local_step(x, target, w, conv, small, prefetch=None, early_reduce=None):
    h = x
    w = [dict(wl) for wl in w]
    lwts, saved = [], []
    for l in range(N_LAYERS):
        h, s, lwt, arrived = _layer_fwd(h, w[l], conv, small, l, prefetch if l == 0 else None)
        for layer, ws in arrived:
            w[layer].update(ws)
        lwts.append(lwt)
        saved.append(s)
    loss_blk, dgf, dh = _loss_head(h, small["norm_f"].reshape(1, D), target, tm=_tiles(x.shape[0])["tm"], name="loss_head")
    per_layer, big, carried = [None] * N_LAYERS, [None] * N_LAYERS, None
    for l in reversed(range(N_LAYERS)):
        hooks = early_reduce(big[1]) if (early_reduce is not None and l == 0) else None
        dh, per_layer[l], big[l], carried = _layer_bwd(dh, saved[l], lwts[l], l, hooks)
    grads = {k: jnp.stack([per_layer[l][k] for l in range(N_LAYERS)], axis=0) for k in per_layer[0]}
    grads["norm_f"] = dgf[0]
    return loss_blk, dh, grads, big, carried


PACK_W = 1024
BIG = (("w_in", W_IN_SHARD, D, W_IN_SHARD, 256), ("w_branch", 768, D, 256, D), ("w_out", 256, D, 256, D),
       ("w_ffn_in", D, FFN_SHARD, 256, FFN_SHARD), ("w_ffn_out", 704, D, 352, D))
CONV = ("lru_conv_w", "ssd_conv_w")
SMALL = ("norm1_g", "b_gate", "lru_conv_b", "lru_w_a", "lru_b_a", "lru_w_x", "lru_b_x", "lru_lambda", "ssd_conv_b",
         "ssd_dt_bias", "ssd_A_log", "ssd_D", "ssd_norm_g", "norm2_g", "norm_f")
_WIRE = jnp.bfloat16
N_CHIPS = 4
N_DEV = 8


def _mesh_pos():
    return lax.axis_index("x"), lax.axis_index("y"), lax.axis_index("c")


HBM_SPEC = pl.BlockSpec(memory_space=pltpu.HBM)


def _remote(src, dst, send_sems, recv_sems, k, to):
    return pltpu.make_async_remote_copy(src_ref=src, dst_ref=dst, send_sem=send_sems.at[k], recv_sem=recv_sems.at[k],
                                        device_id=to, device_id_type=MESH)


def _other_chips(x, y):
    return [(1 - x, y), (x, 1 - y), (1 - x, 1 - y)]


def _weight_fetch(loc, layer, owner):
    names = list(owner)
    rows = {n: loc[n].shape[1] for n in names}
    by_chip = ("w_in", "w_ffn_in")
    shapes = [((N_CHIPS,) + loc[n].shape[1:]) if n in by_chip else (N_CHIPS * rows[n], D) for n in names]

    def place(o_ref, n, chip):
        if n in by_chip:
            return o_ref.at[chip]
        return o_ref.at[pl.ds(pl.multiple_of(chip * rows[n], 16), rows[n]), :]

    def step(which, in_refs, o_refs, send_sems, recv_sems):
        x, y, c = _mesh_pos()
        s = 2 * x + y
        sib = (x, y, 1 - c)
        chips = _other_chips(x, y)
        for core in (0, 1):
            @pl.when(c == core)
            def _():
                for k, n in enumerate(names):
                    for j, (px, py) in enumerate(chips):
                        landed = place(o_refs[k], n, 2 * px + py)
                        sent = _remote(in_refs[k].at[layer], place(o_refs[k], n, s), send_sems, recv_sems, 3 * k + j,
                                       (px, py, c))
                        arrives = _remote(in_refs[k].at[layer], landed, send_sems, recv_sems, 3 * k + j, (px, py, c))
                        passed = _remote(landed, landed, send_sems, recv_sems, 3 * (len(names) + k) + j, sib)
                        if owner[n] == core:
                            if which == "start":
                                sent.start()
                            elif which == "mid":
                                arrives.wait_recv()
                                passed.start()
                            else:
                                sent.wait_send()
                                passed.wait_send()
                        elif which == "end":
                            passed.wait_recv()

    return dict(inputs=[loc[n] for n in names], names=names,
                out_shapes=[jax.ShapeDtypeStruct(shp, loc[n].dtype) for shp, n in zip(shapes, names)],
                sems=[pltpu.SemaphoreType.DMA((6 * len(names),)), pltpu.SemaphoreType.DMA((6 * len(names),))],
                start=functools.partial(step, "start"), mid=functools.partial(step, "mid"),
                end=functools.partial(step, "end"))


def _comm_now(comm, name):
    n, no = len(comm["inputs"]), len(comm["out_shapes"])

    def body(*refs):
        parts = (refs[:n], refs[n:n + no]) + tuple(refs[n + no:])
        comm["start"](*parts)
        comm["mid"](*parts)
        comm["end"](*parts)

    return pl.pallas_call(
        body, name=name, in_specs=[HBM_SPEC] * n, out_specs=[HBM_SPEC] * no, out_shape=comm["out_shapes"],
        scratch_shapes=comm["sems"],
    )(*comm["inputs"])


def _sibling_send(bufs, layer):
    n = len(bufs)

    def step(which, in_refs, o_refs, send_sems, recv_sems):
        x, y, c = _mesh_pos()
        copies = [_remote(in_refs[k], o_refs[k], send_sems, recv_sems, k, (x, y, 1 - c)) for k in range(n)]

        @pl.when(c != layer)
        def _():
            for cp in copies:
                if which == "start":
                    cp.start()
                elif which == "end":
                    cp.wait_send()

        @pl.when(c == layer)
        def _():
            for cp in copies:
                if which == "end":
                    cp.wait_recv()

    return dict(inputs=list(bufs), out_shapes=[jax.ShapeDtypeStruct(b.shape, b.dtype) for b in bufs],
                sems=[pltpu.SemaphoreType.DMA((n,)), pltpu.SemaphoreType.DMA((n,))],
                start=functools.partial(step, "start"), mid=functools.partial(step, "mid"),
                end=functools.partial(step, "end"))


def _add_cast(g, recv, own, *, a, tr, tc, name):
    wd = g.shape[1]
    nr = a // tr

    def body(own_ref, g_ref, r_ref, o_ref):
        @pl.when(own_ref[0] == 1)
        def _():
            o_ref[...] = (g_ref[...] + r_ref[...]).astype(o_ref.dtype)

    blk = pl.BlockSpec((tr, tc), lambda p, i, j, own_ref: ((p * nr + i) * own_ref[0], j * own_ref[0]))
    return pl.pallas_call(
        body, name=name,
        grid_spec=pltpu.PrefetchScalarGridSpec(
            num_scalar_prefetch=1, grid=(N_CHIPS, nr, wd // tc), in_specs=[blk, blk],
            out_specs=pl.BlockSpec((None, tr, tc), lambda p, i, j, own_ref: (p * own_ref[0], i * own_ref[0], j * own_ref[0]))),
        out_shape=jax.ShapeDtypeStruct((N_CHIPS, a, wd), _WIRE),
        compiler_params=_cp(("arbitrary", "arbitrary", "arbitrary")),
    )(own, g, recv)


def _chip_exchange(parts, layer):
    n = len(parts)

    def step(which, s_refs, o_refs, send_sems, recv_sems):
        x, y, c = _mesh_pos()
        s = 2 * x + y

        @pl.when(c == layer)
        def _():
            for j, (px, py) in enumerate(_other_chips(x, y)):
                for k in range(n):
                    p = 2 * px + py
                    sent = _remote(s_refs[k].at[p], o_refs[k].at[s], send_sems, recv_sems, n * j + k, (px, py, c))
                    if which == "start":
                        sent.start()
                    elif which == "end":
                        _remote(s_refs[k].at[p], o_refs[k].at[p], send_sems, recv_sems, n * j + k, (px, py, c)).wait_recv()
                        sent.wait_send()

    return dict(inputs=list(parts), out_shapes=[jax.ShapeDtypeStruct(p.shape, p.dtype) for p in parts],
                sems=[pltpu.SemaphoreType.DMA((3 * n,)), pltpu.SemaphoreType.DMA((3 * n,))],
                start=functools.partial(step, "start"), mid=functools.partial(step, "mid"),
                end=functools.partial(step, "end"))


def _sum_slots(slots, own, sel, *, tr, tc, name, layer=None, into=None):
    n, rows, wd = slots.shape
    k = own.shape[0]

    def body(sel_ref, s_ref, own_ref, *rest):
        o_ref = rest[-1]

        @pl.when(sel_ref[1] == 1)
        def _():
            mine = sel_ref[0]
            acc = jnp.zeros((tr, tc), F32)
            for p in range(n):
                acc = acc + jnp.where(mine == p, own_ref[...].astype(F32), s_ref[p].astype(F32))
            o_ref[...] = acc

    if layer is not None:
        out_spec = pl.BlockSpec((None, tr, tc), lambda i, j, sel_ref: (layer, i * sel_ref[1], j * sel_ref[1]))
        out_shape = jax.ShapeDtypeStruct((N_LAYERS, rows, wd), F32)
    else:
        out_spec = pl.BlockSpec((tr, tc), lambda i, j, sel_ref: (i * sel_ref[1], j * sel_ref[1]))
        out_shape = jax.ShapeDtypeStruct((rows, wd), F32)
    in_specs = [pl.BlockSpec((n, tr, tc), lambda i, j, sel_ref: (0, i * sel_ref[1], j * sel_ref[1])),
                pl.BlockSpec((None, tr, tc), lambda i, j, sel_ref: (sel_ref[0] if k > 1 else 0, i * sel_ref[1],
                                                                    j * sel_ref[1]))]
    args = [sel, slots, own]
    if into is not None:
        in_specs.append(pl.BlockSpec(memory_space=pl.ANY))
        args.append(into)
    return pl.pallas_call(
        body, name=name,
        grid_spec=pltpu.PrefetchScalarGridSpec(num_scalar_prefetch=1, grid=(rows // tr, wd // tc), in_specs=in_specs,
                                               out_specs=out_spec),
        out_shape=out_shape, input_output_aliases={3: 0} if into is not None else {},
        compiler_params=_cp(("arbitrary", "arbitrary")),
    )(*args)


def _sibling_share(both):
    n = len(both)

    def body(*refs):
        o_refs, (send_sems, recv_sems) = refs[n:2 * n], refs[2 * n:]
        x, y, c = _mesh_pos()
        sends = [_remote(o_refs[k].at[c], o_refs[k].at[c], send_sems, recv_sems, k, (x, y, 1 - c)) for k in range(n)]
        for cp in sends:
            cp.start()
        for k in range(n):
            _remote(o_refs[k].at[1 - c], o_refs[k].at[1 - c], send_sems, recv_sems, k, (x, y, 1 - c)).wait_recv()
        for cp in sends:
            cp.wait_send()

    return pl.pallas_call(
        body, name="grad_sibling_share", in_specs=[HBM_SPEC] * n, out_specs=[HBM_SPEC] * n,
        out_shape=[jax.ShapeDtypeStruct(b.shape, b.dtype) for b in both], input_output_aliases={k: k for k in range(n)},
        scratch_shapes=[pltpu.SemaphoreType.DMA((n,)), pltpu.SemaphoreType.DMA((n,))],
    )(*both)


def _allgather_devices(part):
    rows, wd = part.shape

    def step(which, in_refs, o_refs, send_sems, recv_sems):
        (p_ref,), (o_ref,) = in_refs, o_refs
        x, y, c = _mesh_pos()
        sib = (x, y, 1 - c)
        chips = _other_chips(x, y)
        slot = lambda px, py, pc: o_ref.at[4 * px + 2 * py + pc]
        first = [_remote(p_ref, slot(x, y, c), send_sems, recv_sems, 0, sib)]
        first += [_remote(p_ref, slot(x, y, c), send_sems, recv_sems, 1 + j, (px, py, c)) for j, (px, py) in enumerate(chips)]
        passed = [_remote(slot(px, py, c), slot(px, py, c), send_sems, recv_sems, 4 + j, sib)
                  for j, (px, py) in enumerate(chips)]
        if which == "start":
            for cp in first:
                cp.start()
        elif which == "mid":
            for j, (px, py) in enumerate(chips):
                _remote(p_ref, slot(px, py, c), send_sems, recv_sems, 1 + j, (px, py, c)).wait_recv()
                passed[j].start()
        else:
            _remote(p_ref, slot(x, y, 1 - c), send_sems, recv_sems, 0, sib).wait_recv()
            for j, (px, py) in enumerate(chips):
                _remote(slot(px, py, 1 - c), slot(px, py, 1 - c), send_sems, recv_sems, 4 + j, sib).wait_recv()
            for cp in first + passed:
                cp.wait_send()

    return dict(inputs=[part], out_shapes=[jax.ShapeDtypeStruct((N_DEV, rows, wd), part.dtype)],
                sems=[pltpu.SemaphoreType.DMA((N_DEV - 1,)), pltpu.SemaphoreType.DMA((N_DEV - 1,))],
                start=functools.partial(step, "start"), mid=functools.partial(step, "mid"),
                end=functools.partial(step, "end"))


def _comm_both(a, b):
    na, nao = len(a["inputs"]), len(a["out_shapes"])

    def step(which, in_refs, o_refs, sa, ra, sb, rb_):
        a[which](in_refs[:na], o_refs[:nao], sa, ra)
        b[which](in_refs[na:], o_refs[nao:], sb, rb_)

    return dict(inputs=a["inputs"] + b["inputs"], out_shapes=a["out_shapes"] + b["out_shapes"], sems=a["sems"] + b["sems"],
                start=functools.partial(step, "start"), mid=functools.partial(step, "mid"),
                end=functools.partial(step, "end"))


def _by_chip_to_full(stack):
    _, nl, r, b = stack.shape
    return stack.transpose(1, 2, 0, 3).reshape(nl, r, N_CHIPS * b)


def _sharded_step(a):
    x = a["x"][0]
    target = a["loss_target"][0]
    cx, cy, cc = _mesh_pos()
    chip = (2 * cx + cy).astype(jnp.int32)
    core = cc.astype(jnp.int32)
    me = (4 * cx + 2 * cy + cc).astype(jnp.int32)
    zero = jnp.zeros((), jnp.int32)
    dus = lax.dynamic_update_slice

    loc = {n: a[n].astype(_MXU) for n, *_ in BIG}

    def with_own(got, names, layer):
        out = {}
        for g, n in zip(got, names):
            mine = loc[n][layer]
            out[n] = (dus(g, mine[None], (chip, zero, zero)) if g.ndim == 3 else dus(g, mine, (chip * mine.shape[0], zero)))
        return out

    rest = {"w_ffn_in": 0, "w_branch": 1, "w_out": 1, "w_ffn_out": 1}
    conv_loc = jnp.concatenate([a[n].reshape(-1, PACK_W) for n in CONV], axis=0)
    now = _weight_fetch(loc, 0, {"w_in": 0})
    conv_all, *got_now = _comm_now(_comm_both(_allgather_devices(conv_loc), now), "allgather_weights")
    w0 = with_own(got_now, now["names"], 0)
    later = {"in_proj": (0, _weight_fetch(loc, 0, rest)), "lru": (1, _weight_fetch(loc, 1, {"w_in": 0})),
             "ssd": (1, _weight_fetch(loc, 1, rest))}
    prefetch = {k: (f, functools.partial(lambda got, layer, f: (layer, with_own(got, f["names"], layer)), layer=layer, f=f))
                for k, (layer, f) in later.items()}
    conv_all = dus(conv_all, conv_loc[None], (me, zero, zero))[0::2]
    conv, off = {}, 0
    for n in CONV:
        rows = a[n].size // PACK_W
        conv[n] = _by_chip_to_full(conv_all[:, off:off + rows].reshape((N_CHIPS,) + a[n].shape))
        off += rows
    small = {n: a[n] for n in SMALL}

    views = lambda big_l, specs: [big_l[n].reshape(-1, wd) for n, _, wd, _, _ in specs]
    owns = lambda layer: (core == layer).astype(jnp.int32)
    w_in_only, others = BIG[:1], BIG[1:]

    def partial_sums(big_l, recv, layer, specs):
        return [_add_cast(v, r, owns(layer).reshape(1), a=rows, tr=tr, tc=tc, name=f"grad_add_sibling_l{layer}_{n}")
                for v, r, (n, rows, _, tr, tc) in zip(views(big_l, specs), recv, specs)]

    def reduced(slots, parts, layer, into, specs):
        sel = jnp.stack([chip, owns(layer)])
        return [_sum_slots(s, p, sel, tr=tr, tc=tc, name=f"grad_sum_chips_l{layer}_{n}", layer=layer, into=buf)
                for s, p, buf, (n, _, _, tr, tc) in zip(slots, parts, into, specs)]

    kept = {}

    def early_reduce(big_1):
        def during_lru(big_0):
            kept["big_0"] = dict(big_0)
            return _comm_both(_sibling_send(views(big_1, BIG), 1), _sibling_send(views(big_0, others), 0))

        def during_ssd(recv):
            kept["parts_1"] = partial_sums(big_1, recv[:len(BIG)], 1, BIG)
            kept["parts_0"] = partial_sums(kept["big_0"], recv[len(BIG):], 0, others)
            return _comm_both(_chip_exchange(kept["parts_1"], 1), _chip_exchange(kept["parts_0"], 0))

        return dict(lru=during_lru, ssd=during_ssd, in_dgrad=lambda big_0: _sibling_send(views(big_0, w_in_only), 0))

    loss_blk, grad_x, grads, big, (slots, recv_in) = _local_step(x, target, [w0, {}], conv, small, prefetch, early_reduce)
    loss = lax.psum(loss_blk[0, 0], ("x", "y", "c"))
    both = reduced(slots[:len(BIG)], kept["parts_1"], 1, [None] * len(BIG), BIG)
    both[1:] = reduced(slots[len(BIG):], kept["parts_0"], 0, both[1:], others)
    parts_in = partial_sums(big[0], recv_in, 0, w_in_only)
    names = SMALL + CONV
    srows = -(-sum(grads[n].size for n in names) // (8 * PACK_W)) * 8
    flat = lambda d, ns: jnp.concatenate([d[n].reshape(-1) for n in ns])
    padto = lambda v: jnp.pad(v, (0, srows * PACK_W - v.shape[0])).reshape(srows, PACK_W)
    g_own = padto(flat(grads, names))
    g_all, *slots_in = _comm_now(_comm_both(_allgather_devices(g_own), _chip_exchange(parts_in, 0)), "grad_chip_exchange")
    both[:1] = reduced(slots_in, parts_in, 0, both[:1], w_in_only)
    done = dict(zip([n for n, *_ in BIG], _sibling_share(both)))
    g_big = {n: done[n].reshape(a[n].shape) for n in ("w_branch", "w_out", "w_ffn_in", "w_ffn_out")}
    gt = done["w_in"].transpose(0, 2, 1)
    first = jnp.concatenate([gt[..., 512 * j + 256 * part:512 * j + 256 * (part + 1)] for part in range(2) for j in range(4)]
                            + [gt[..., 2 * D:]], axis=-1)
    tail = W_IN_SHARD - (IN_DIM - 7168)
    last = jnp.concatenate([gt[..., :tail], gt[..., W_IN_SHARD - N_HEADS:], gt[..., tail:W_IN_SHARD - N_HEADS]], axis=-1)
    g_big["w_in"] = jnp.where(chip == 0, first, jnp.where(chip == N_CHIPS - 1, last, gt))

    g_sum = _sum_slots(g_all, g_own[None], jnp.stack([me, zero + 1]), tr=srows, tc=PACK_W, name="small_grad_sum")
    off, g_small = 0, {}
    for n in names:
        g_small[n] = g_sum.reshape(-1)[off:off + grads[n].size].reshape(grads[n].shape)
        off += grads[n].size
    for n in CONV:
        width = a[n].shape[2]
        g_big[n] = lax.dynamic_slice(g_small.pop(n), (zero, zero, chip * width), a[n].shape)

    out_g, out_d, out_m, out_v = {}, {}, {}, {}
    for n in g_big:
        shp = a[n].shape
        two_d = (shp[0] * shp[1], shp[2])
        d_, m_, v_ = _adamw(a[n].reshape(two_d), g_big[n].reshape(two_d), a["m_" + n].reshape(two_d),
                            a["v_" + n].reshape(two_d), name="adamw_" + n)
        out_g[n], out_d[n], out_m[n], out_v[n] = g_big[n], d_.reshape(shp), m_.reshape(shp), v_.reshape(shp)
    d_, m_, v_ = _adamw(padto(flat(a, SMALL)), padto(flat(g_small, SMALL)), padto(flat({n: a["m_" + n] for n in SMALL}, SMALL)),
                        padto(flat({n: a["v_" + n] for n in SMALL}, SMALL)), name="adamw_small")
    off = 0
    for n in SMALL:
        cut = lambda v: v.reshape(-1)[off:off + a[n].size].reshape(a[n].shape)
        out_g[n], out_d[n], out_m[n], out_v[n] = g_small[n], cut(d_), cut(m_), cut(v_)
        off += a[n].size
    return loss, grad_x[None], out_g, out_d, out_m, out_v


WEIGHTS = ("norm1_g", "w_in", "b_gate", "lru_conv_w", "lru_conv_b", "lru_w_a", "lru_b_a", "lru_w_x", "lru_b_x", "lru_lambda",
           "ssd_conv_w", "ssd_conv_b", "ssd_dt_bias", "ssd_A_log", "ssd_D", "ssd_norm_g", "w_branch", "w_out", "norm2_g",
           "w_ffn_in", "w_ffn_out", "norm_f")
INPUTS = ("x",) + WEIGHTS + ("loss_target",) + tuple("m_" + n for n in WEIGHTS) + tuple("v_" + n for n in WEIGHTS)


def kernel(x, norm1_g, w_in, b_gate, lru_conv_w, lru_conv_b, lru_w_a, lru_b_a, lru_w_x, lru_b_x, lru_lambda, ssd_conv_w, ssd_conv_b, ssd_dt_bias, ssd_A_log, ssd_D, ssd_norm_g, w_branch, w_out, norm2_g, w_ffn_in, w_ffn_out, norm_f, loss_target, m_norm1_g, m_w_in, m_b_gate, m_lru_conv_w, m_lru_conv_b, m_lru_w_a, m_lru_b_a, m_lru_w_x, m_lru_b_x, m_lru_lambda, m_ssd_conv_w, m_ssd_conv_b, m_ssd_dt_bias, m_ssd_A_log, m_ssd_D, m_ssd_norm_g, m_w_branch, m_w_out, m_norm2_g, m_w_ffn_in, m_w_ffn_out, m_norm_f, v_norm1_g, v_w_in, v_b_gate, v_lru_conv_w, v_lru_conv_b, v_lru_w_a, v_lru_b_a, v_lru_w_x, v_lru_b_x, v_lru_lambda, v_ssd_conv_w, v_ssd_conv_b, v_ssd_dt_bias, v_ssd_A_log, v_ssd_D, v_ssd_norm_g, v_w_branch, v_w_out, v_norm2_g, v_w_ffn_in, v_w_ffn_out, v_norm_f):
    args = (x, norm1_g, w_in, b_gate, lru_conv_w, lru_conv_b, lru_w_a, lru_b_a, lru_w_x, lru_b_x, lru_lambda, ssd_conv_w, ssd_conv_b, ssd_dt_bias, ssd_A_log, ssd_D, ssd_norm_g, w_branch, w_out, norm2_g, w_ffn_in, w_ffn_out, norm_f, loss_target, m_norm1_g, m_w_in, m_b_gate, m_lru_conv_w, m_lru_conv_b, m_lru_w_a, m_lru_b_a, m_lru_w_x, m_lru_b_x, m_lru_lambda, m_ssd_conv_w, m_ssd_conv_b, m_ssd_dt_bias, m_ssd_A_log, m_ssd_D, m_ssd_norm_g, m_w_branch, m_w_out, m_norm2_g, m_w_ffn_in, m_w_ffn_out, m_norm_f, v_norm1_g, v_w_in, v_b_gate, v_lru_conv_w, v_lru_conv_b, v_lru_w_a, v_lru_b_a, v_lru_w_x, v_lru_b_x, v_lru_lambda, v_ssd_conv_w, v_ssd_conv_b, v_ssd_dt_bias, v_ssd_A_log, v_ssd_D, v_ssd_norm_g, v_w_branch, v_w_out, v_norm2_g, v_w_ffn_in, v_w_ffn_out, v_norm_f)
    assert len(args) == len(INPUTS)
    loss, grad_x, g, d, m, v = _sharded_step(dict(zip(INPUTS, args)))
    return (loss, grad_x, *[g[n] for n in WEIGHTS], *[d[n] for n in WEIGHTS], *[m[n] for n in WEIGHTS],
            *[v[n] for n in WEIGHTS])
```

```python
import functools
import math

import numpy as np
import jax
import jax.numpy as jnp
from jax import lax
from jax.experimental import pallas as pl
from jax.experimental.pallas import tpu as pltpu

F32 = jnp.float32
BF16 = jnp.bfloat16
_MXU = jnp.bfloat16
_HI = lax.Precision.HIGHEST

D = 1024
EPS = 1e-6
N_LAYERS = 2
LRU_C = 8.0
N_HEADS = 32
HEAD_P = 64
N_GROUPS = 4
N_STATE = 128
SSD_INNER = 2048
XBC = 3072
D_FF = 2816
CHUNK = 64
NORM_ROWS = 32
IN_DIM = 9248

NP = 9216
ZX_W = 5120
G0 = 6144
LBLK = 512
DT_PAD = 128

VMEM_LIMIT_BYTES_V7X = 56 * 1024 * 1024

ADAM_LR, ADAM_B1, ADAM_B2, ADAM_EPS, ADAM_WD, ADAM_STEP = 0.001, 0.9, 0.999, 1e-08, 0.01, 10
MESH = pl.DeviceIdType.MESH


def _cp(sem):
    return pltpu.CompilerParams(dimension_semantics=sem, vmem_limit_bytes=VMEM_LIMIT_BYTES_V7X)


def _lblk_col(j):
    return 10 + j + 4 * (j // 2)


def _sigmoid(x):
    return 0.5 * jnp.tanh(0.5 * x) + 0.5


def _softplus(x):
    return jnp.maximum(x, 0.0) + jnp.log(1.0 + jnp.exp(-jnp.abs(x)))


def _silu(x):
    return x * _sigmoid(x)


def _dsilu(x):
    s = _sigmoid(x)
    return s * (1.0 + x * (1.0 - s))


_GELU_C0 = math.sqrt(2.0 / math.pi)
_GELU_C1 = 0.044715


def _gelu_and_grad(x):
    t = jnp.tanh(_GELU_C0 * (x + _GELU_C1 * x * x * x))
    g = 0.5 * x * (1.0 + t)
    dg = 0.5 * (1.0 + t) + 0.5 * x * (1.0 - t * t) * _GELU_C0 * (1.0 + 3.0 * _GELU_C1 * x * x)
    return g, dg


def _one_minus_exp(x):
    p = 1.0 + x * (1.0 / 7.0)
    p = 1.0 + x * (1.0 / 6.0) * p
    p = 1.0 + x * (1.0 / 5.0) * p
    p = 1.0 + x * (1.0 / 4.0) * p
    p = 1.0 + x * (1.0 / 3.0) * p
    p = 1.0 + x * (1.0 / 2.0) * p
    return jnp.where(x > -0.3, -x * p, 1.0 - jnp.exp(x))


def _dot(a, b):
    return jnp.dot(a.astype(_MXU), b.astype(_MXU), preferred_element_type=F32)


def _dot_nt(a, b):
    return lax.dot_general(a.astype(_MXU), b.astype(_MXU), (((1,), (1,)), ((), ())), preferred_element_type=F32)


def _dot_tn(a, b):
    return lax.dot_general(a.astype(_MXU), b.astype(_MXU), (((0,), (0,)), ((), ())), preferred_element_type=F32)


def _shift_down(x, prev8, k):
    xr = pltpu.roll(x, k, 0)
    pr = pltpu.roll(prev8, k, 0)
    row = lax.broadcasted_iota(jnp.int32, prev8.shape, 0)
    head = jnp.where(row < k, pr, xr[0:8])
    return jnp.concatenate([head, xr[8:]], axis=0)


def _shift_up(x, next8, k):
    r = x.shape[0]
    xr = pltpu.roll(x, r - k, 0)
    nr = pltpu.roll(next8, 8 - k, 0)
    row = lax.broadcasted_iota(jnp.int32, next8.shape, 0)
    tail = jnp.where(row >= 8 - k, nr, xr[r - 8:r])
    return jnp.concatenate([xr[:r - 8], tail], axis=0)


def _conv4(x, prev8, w_ref, b_ref, cols=slice(None)):
    acc = x * w_ref[3:4, cols] + b_ref[0:1, cols]
    for k in (1, 2, 3):
        acc = acc + _shift_down(x, prev8, k) * w_ref[3 - k:4 - k, cols]
    return acc


def _conv4_bwd_x(dy, next8, w_ref, cols=slice(None)):
    acc = dy * w_ref[3:4, cols]
    for k in (1, 2, 3):
        acc = acc + _shift_up(dy, next8, k) * w_ref[3 - k:4 - k, cols]
    return acc


def _lin_scan(a, b, reverse):
    r = a.shape[0]
    row = lax.broadcasted_iota(jnp.int32, a.shape, 0)
    d = 1
    while d < r:
        sh = (r - d) if reverse else d
        a_s = pltpu.roll(a, sh, 0)
        b_s = pltpu.roll(b, sh, 0)
        m = (row < r - d) if reverse else (row >= d)
        b = jnp.where(m, a * b_s + b, b)
        a = jnp.where(m, a * a_s, a)
        d *= 2
    return a, b


def _rsum(x):
    return jnp.sum(x, axis=0, keepdims=True)


def _comm_specs(comm):
    if comm is None:
        return [], [], [], [], []
    n = len(comm["inputs"])
    return list(comm["inputs"]), [HBM_SPEC] * n, [HBM_SPEC] * len(comm["out_shapes"]), list(comm["out_shapes"]), comm["sems"]


def _comm_steps(comm, refs, n_in, n_out, first, mid, last):
    ni, no = len(comm["inputs"]), len(comm["out_shapes"])
    parts = (refs[n_in:n_in + ni], refs[n_out:n_out + no]) + tuple(refs[len(refs) - len(comm["sems"]):])
    for when, what in ((first, "start"), (mid, "mid"), (last, "end")):
        @pl.when(when)
        def _():
            comm[what](*parts)


def _norm_mm(h, gamma, w, *, tm, tn, name, out_dtype=F32, comm=None):
    m, k = h.shape
    if w.ndim == 3:
        assert w.shape[2] == tn
        n = w.shape[0] * tn
        w_spec = pl.BlockSpec((None, k, tn), lambda i, j: (j, 0, 0))
    else:
        n = w.shape[1]
        w_spec = pl.BlockSpec((k, tn), lambda i, j: (0, j))

    c_args, c_in_specs, c_out_specs, c_out_shapes, c_sems = _comm_specs(comm)
    ni, nj = m // tm, n // tn

    def body(*refs):
        h_ref, g_ref, w_ref = refs[:3]
        xn_ref, o_ref = refs[3 + len(c_args):5 + len(c_args)]
        i, j = pl.program_id(0), pl.program_id(1)
        if comm is not None:
            _comm_steps(comm, refs, 3, 5 + len(c_args), (i == 0) & (j == 0), (i == (3 * ni) // 4) & (j == 0),
                        (i == ni - 1) & (j == nj - 1))

        @pl.when(j == 0)
        def _():
            x = h_ref[...]
            r = lax.rsqrt(jnp.mean(x * x, axis=-1, keepdims=True) + EPS)
            xn_ref[...] = ((x * r) * g_ref[...]).astype(xn_ref.dtype)
        o_ref[...] = jnp.dot(xn_ref[...], w_ref[...], preferred_element_type=F32).astype(o_ref.dtype)

    return pl.pallas_call(
        body, name=name, grid=(ni, nj),
        in_specs=[pl.BlockSpec((tm, k), lambda i, j: (i, 0)), pl.BlockSpec((1, k), lambda i, j: (0, 0)), w_spec] + c_in_specs,
        out_specs=[pl.BlockSpec((tm, k), lambda i, j: (i, 0)), pl.BlockSpec((tm, tn), lambda i, j: (i, j))] + c_out_specs,
        out_shape=[jax.ShapeDtypeStruct((m, k), _MXU), jax.ShapeDtypeStruct((m, n), out_dtype)] + c_out_shapes,
        scratch_shapes=c_sems,
        compiler_params=_cp(("arbitrary", "arbitrary") if comm is not None else ("parallel", "arbitrary")),
    )(h, gamma, w, *c_args)


def _mm_nn(a, w, *, tm, tn, name, residual=None):
    m, k = a.shape
    n = w.shape[1]

    def body(*refs):
        if residual is None:
            a_ref, w_ref, o_ref = refs
            o_ref[...] = _dot(a_ref[...], w_ref[...])
        else:
            a_ref, w_ref, r_ref, o_ref = refs
            o_ref[...] = _dot(a_ref[...], w_ref[...]) + r_ref[...]

    in_specs = [pl.BlockSpec((tm, k), lambda i, j: (i, 0)), pl.BlockSpec((k, tn), lambda i, j: (0, j))]
    args = [a, w]
    if residual is not None:
        in_specs.append(pl.BlockSpec((tm, tn), lambda i, j: (i, j)))
        args.append(residual)
    return pl.pallas_call(
        body, name=name, grid=(m // tm, n // tn), in_specs=in_specs,
        out_specs=pl.BlockSpec((tm, tn), lambda i, j: (i, j)),
        out_shape=jax.ShapeDtypeStruct((m, n), F32),
        compiler_params=_cp(("parallel", "parallel")),
    )(*args)


def _wgrad(a, b, *, tt, ta, tn, name, out_shape, out_block, out_index, a_tab=None, o_tab=None, into=None):
    t = a.shape[0]
    a_tab = list(range(a.shape[1] // ta)) if a_tab is None else a_tab
    o_tab = a_tab if o_tab is None else o_tab
    nb = b.shape[1] // tn

    def body(at_ref, ot_ref, a_ref, b_ref, *rest):
        del at_ref, ot_ref
        o_ref = rest[-1]

        @pl.when(pl.program_id(2) == 0)
        def _():
            o_ref[...] = jnp.zeros_like(o_ref)
        o_ref[...] += _dot_tn(a_ref[...], b_ref[...])

    in_specs = [pl.BlockSpec((tt, ta), lambda r, j, i, at, ot: (i, at[r])),
                pl.BlockSpec((tt, tn), lambda r, j, i, at, ot: (i, j))]
    args = [jnp.asarray(a_tab, jnp.int32), jnp.asarray(o_tab, jnp.int32), a, b]
    aliases = {}
    if into is not None:
        in_specs.append(pl.BlockSpec(memory_space=pl.ANY))
        args.append(into)
        aliases = {4: 0}
    return pl.pallas_call(
        body, name=name,
        grid_spec=pltpu.PrefetchScalarGridSpec(
            num_scalar_prefetch=2, grid=(len(a_tab), nb, t // tt), in_specs=in_specs,
            out_specs=pl.BlockSpec(out_block, lambda r, j, i, at, ot: out_index(ot[r], j))),
        out_shape=jax.ShapeDtypeStruct(out_shape, F32), input_output_aliases=aliases,
        compiler_params=_cp(("parallel", "parallel", "arbitrary")),
    )(*args)


def _mm_nt(a, w, *, tm, name):
    m, kc = a.shape
    n = w.shape[0]

    def body(a_ref, w_ref, o_ref):
        o_ref[...] = _dot_nt(a_ref[...], w_ref[...])

    return pl.pallas_call(
        body, name=name, grid=(m // tm,),
        in_specs=[pl.BlockSpec((tm, kc), lambda i: (i, 0)), pl.BlockSpec((n, kc), lambda i: (0, 0))],
        out_specs=pl.BlockSpec((tm, n), lambda i: (i, 0)),
        out_shape=jax.ShapeDtypeStruct((m, n), F32),
        compiler_params=_cp(("parallel",)),
    )(a, w)


def _mm_nt_rmsbwd(dy, w, x, gamma, dres, *, tm, tk, name, extra=None, comm=None):
    m, kc = dy.shape
    nk = kc // tk
    ni = m // tm
    n_x = 5 if extra is None else 7
    c_args, c_in_specs, c_out_specs, c_out_shapes, c_sems = _comm_specs(comm)
    if w.ndim == 3:
        assert w.shape[0] == nk and w.shape[2] == tk
        d = w.shape[1]
        w_spec = pl.BlockSpec((None, d, tk), lambda i, k: (k, 0, 0))
    else:
        d = w.shape[0]
        w_spec = pl.BlockSpec((d, tk), lambda i, k: (0, k))

    def body(*refs):
        dy_ref, w_ref, x_ref, g_ref, r_ref = refs[:5]
        if extra is not None:
            dy2_ref, w2_ref = refs[5:7]
        n_out = n_x + len(c_args)
        dx_ref, dg_ref = refs[n_out:n_out + 2]
        acc_ref = refs[n_out + 2 + len(c_out_shapes)]
        i, kk = pl.program_id(0), pl.program_id(1)
        if comm is not None:
            _comm_steps(comm, refs, n_x, n_out + 2, (i == 0) & (kk == 0), (i == (3 * ni) // 4) & (kk == 0),
                        (i == ni - 1) & (kk == nk - 1))

        @pl.when(kk == 0)
        def _():
            acc_ref[...] = jnp.zeros_like(acc_ref)

        @pl.when((i == 0) & (kk == 0))
        def _():
            dg_ref[...] = jnp.zeros_like(dg_ref)

        acc_ref[...] += _dot_nt(dy_ref[...], w_ref[...])

        @pl.when(kk == nk - 1)
        def _():
            dxn = acc_ref[...]
            if extra is not None:
                dxn = dxn + _dot_nt(dy2_ref[...], w2_ref[...])
            xv = x_ref[...]
            r = lax.rsqrt(jnp.mean(xv * xv, axis=-1, keepdims=True) + EPS)
            xh = xv * r
            dg_ref[0:1, :] += _rsum(dxn * xh)
            dxh = dxn * g_ref[...]
            dx_ref[...] = r_ref[...] + r * (dxh - xh * jnp.mean(dxh * xh, axis=-1, keepdims=True))

    in_specs = [pl.BlockSpec((tm, tk), lambda i, k: (i, k)), w_spec,
                pl.BlockSpec((tm, d), lambda i, k: (i, 0)), pl.BlockSpec((1, d), lambda i, k: (0, 0)),
                pl.BlockSpec((tm, d), lambda i, k: (i, 0))]
    args = [dy, w, x, gamma, dres]
    if extra is not None:
        k2 = extra[0].shape[1]
        in_specs += [pl.BlockSpec((tm, k2), lambda i, k: (i, 0)), pl.BlockSpec((d, k2), lambda i, k: (0, 0))]
        args += list(extra)
    return pl.pallas_call(
        body, name=name, grid=(ni, nk), in_specs=in_specs + c_in_specs,
        out_specs=[pl.BlockSpec((tm, d), lambda i, k: (i, 0)), pl.BlockSpec((8, d), lambda i, k: (0, 0))] + c_out_specs,
        out_shape=[jax.ShapeDtypeStruct((m, d), F32), jax.ShapeDtypeStruct((8, d), F32)] + c_out_shapes,
        scratch_shapes=[pltpu.VMEM((tm, d), F32)] + c_sems,
        compiler_params=_cp(("arbitrary", "arbitrary")),
    )(*args, *c_args)


def _rsum8(x):
    acc = x[0:8]
    for g in range(1, x.shape[0] // 8):
        acc = acc + x[8 * g:8 * (g + 1)]
    return acc


def _lru_gates(x, prev8, cw_ref, cb_ref, wa_ref, wx_ref, ba_ref, bx_ref, lam_ref):
    u = _conv4(x, prev8, cw_ref, cb_ref)
    ra =_sigmoid(_dot(u, wa_ref[0]) + ba_ref[...])
    ia = _sigmoid(_dot(u, wx_ref[0]) + bx_ref[...])
    sp = _softplus(-lam_ref[...])
    log_a = -LRU_C * ra * sp
    a = jnp.exp(log_a)
    m2 = _one_minus_exp(2.0 * log_a)
    mult = jnp.sqrt(m2)
    return u, ra, ia, sp, a, m2, mult


def _lru_fwd(proj, lw, *, r, name, comm=None):
    t = proj.shape[0]
    nt = t // r
    c_args, c_in_specs, c_out_specs, c_out_shapes, c_sems = _comm_specs(comm)

    def body(*refs):
        xg_ref, xp_ref, cw_ref, cb_ref, wa_ref, wx_ref, ba_ref, bx_ref, lam_ref = refs[:9]
        hl_ref, ya_ref = refs[9 + len(c_args):11 + len(c_args)]
        carry_ref = refs[11 + len(c_args) + len(c_out_shapes)]
        i = pl.program_id(1)
        if comm is not None:
            j = pl.program_id(0)
            _comm_steps(comm, refs, 9, 11 + len(c_args), (j == 0) & (i == 0), (j == 3) & (i == 0), (j == 3) & (i == nt - 1))

        @pl.when(i == 0)
        def _():
            carry_ref[...] = jnp.zeros_like(carry_ref)

        x = xg_ref[:, 0:256]
        lg = xg_ref[:, 256:512]
        prev8 = jnp.where(i == 0, 0.0, xp_ref[:, 0:256])
        u, ra, ia, sp, a, m2, mult = _lru_gates(x, prev8, cw_ref, cb_ref, wa_ref, wx_ref, ba_ref, bx_ref, lam_ref)
        ac, hc = _lin_scan(a, mult * ia * u, False)
        h = hc + ac * carry_ref[0:1, :]
        hl_ref[...] = h
        carry_ref[0:1, :] = hl_ref[r - 1:r, :]
        g, _ = _gelu_and_grad(lg)
        ya_ref[...] = (g * h).astype(ya_ref.dtype)

    small = lambda rows: pl.BlockSpec((rows, 256), lambda j, i: (0, j))
    return pl.pallas_call(
        body, name=name, grid=(4, nt),
        in_specs=[pl.BlockSpec((r, LBLK), lambda j, i: (i, _lblk_col(j))),
                  pl.BlockSpec((8, LBLK), lambda j, i: (jnp.maximum(i * (r // 8) - 1, 0), _lblk_col(j))),
                  small(4), small(1),
                  pl.BlockSpec((1, 256, 256), lambda j, i: (j, 0, 0)), pl.BlockSpec((1, 256, 256), lambda j, i: (j, 0, 0)),
                  small(1), small(1), small(1)] + c_in_specs,
        out_specs=[pl.BlockSpec((r, 256), lambda j, i: (i, j)), pl.BlockSpec((r, 256), lambda j, i: (i, j))] + c_out_specs,
        out_shape=[jax.ShapeDtypeStruct((t, D), F32), jax.ShapeDtypeStruct((t, D), _MXU)] + c_out_shapes,
        scratch_shapes=[pltpu.VMEM((8, 256), F32)] + c_sems,
        compiler_params=_cp(("arbitrary", "arbitrary") if comm is not None else ("parallel", "arbitrary")),
    )(proj, proj, lw["cw"], lw["cb"], lw["wa"], lw["wx"], lw["ba"], lw["bx"], lw["lam"], *c_args)


def _lru_bwd(proj, hl, dya, dproj, lw, *, r, name, comm=None):
    t = proj.shape[0]
    nt = t // r
    c_args, c_in_specs, c_out_specs, c_out_shapes, c_sems = _comm_specs(comm)
    n_in = 13

    def body(*refs):
        xg_ref, xp_ref, hl_ref, hp_ref, dya_ref, cw_ref, cb_ref, wa_ref, wx_ref, ba_ref, bx_ref, lam_ref = refs[:12]
        n_out = n_in + len(c_args)
        dproj_ref, sm_ref, dwa_ref, dwx_ref = refs[n_out:n_out + 4]
        n_scr = n_out + 4 + len(c_out_shapes)
        carry_ref, du8_ref, row_scr = refs[n_scr:n_scr + 3]
        i = pl.program_id(1)
        if comm is not None:
            j = pl.program_id(0)
            _comm_steps(comm, refs, n_in, n_out + 4, (j == 0) & (i == 0), (j == 3) & (i == 0), (j == 3) & (i == nt - 1))

        @pl.when(i == 0)
        def _():
            carry_ref[...] = jnp.zeros_like(carry_ref)
            du8_ref[...] = jnp.zeros_like(du8_ref)
            sm_ref[...] = jnp.zeros_like(sm_ref)
            dwa_ref[...] = jnp.zeros_like(dwa_ref)
            dwx_ref[...] = jnp.zeros_like(dwx_ref)

        tile0 = i == nt - 1
        xp = xg_ref[:, 0:256]
        lg = xg_ref[:, 256:512]
        prev8 = jnp.where(tile0, 0.0, xp_ref[:, 0:256])
        u, ra, ia, sp, a, m2, mult = _lru_gates(xp, prev8, cw_ref, cb_ref, wa_ref, wx_ref, ba_ref, bx_ref, lam_ref)
        h = hl_ref[...]
        hprev = _shift_down(h, jnp.where(tile0, 0.0, hp_ref[...]), 1)
        dya_v = dya_ref[...]
        g, dg = _gelu_and_grad(lg)
        ac, lc = _lin_scan(_shift_up(a, carry_ref[...], 1), dya_v * g, True)
        lam_v = lc + ac * carry_ref[1:2, :]
        row_scr[0:8, :] = lam_v[0:8]
        row_scr[8:16, :] = a[0:8]
        carry_ref[1:2, :] = row_scr[0:1, :]
        carry_ref[0:1, :] = row_scr[8:9, :]
        da = lam_v * hprev
        dmult = lam_v * ia * u
        dia = lam_v * mult * u
        dlog = da * a - dmult * (1.0 - m2) / mult
        dra = -LRU_C * sp * dlog
        dpa = dra * ra * (1.0 - ra)
        dpx = dia * ia * (1.0 - ia)
        du = lam_v * mult * ia + _dot_nt(dpa, wa_ref[0]) + _dot_nt(dpx, wx_ref[0])
        dwa_ref[0] += _dot_tn(u, dpa)
        dwx_ref[0] += _dot_tn(u, dpx)
        dlx = du * cw_ref[3:4, :]
        sm_ref[24:32, :] += _rsum8(du * xp)
        for k in (1, 2, 3):
            du_k = _shift_up(du, du8_ref[...], k)
            dlx = dlx + du_k * cw_ref[3 - k:4 - k, :]
            sm_ref[8 * (3 - k):8 * (4 - k), :] += _rsum8(du_k * xp)
        du8_ref[...] = du[0:8]
        dproj_ref[:, 0:256] = dlx.astype(dproj_ref.dtype)
        dproj_ref[:, 256:512] = (dya_v * h * dg).astype(dproj_ref.dtype)
        sm_ref[32:40, :] += _rsum8(du)
        sm_ref[40:48, :] += _rsum8(dpa)
        sm_ref[48:56, :] += _rsum8(dpx)
        sm_ref[56:64, :] += _rsum8(-LRU_C * ra * dlog) * (-_sigmoid(-lam_ref[...]))

    rev = lambda i: nt - 1 - i
    small = lambda rows: pl.BlockSpec((rows, 256), lambda j, i: (0, j))
    wblk = pl.BlockSpec((1, 256, 256), lambda j, i: (j, 0, 0))
    return pl.pallas_call(
        body, name=name, grid=(4, nt),
        in_specs=[pl.BlockSpec((r, LBLK), lambda j, i: (rev(i), _lblk_col(j))),
                  pl.BlockSpec((8, LBLK), lambda j, i: (jnp.maximum(rev(i) * (r // 8) - 1, 0), _lblk_col(j))),
                  pl.BlockSpec((r, 256), lambda j, i: (rev(i), j)),
                  pl.BlockSpec((8, 256), lambda j, i: (jnp.maximum(rev(i) * (r // 8) - 1, 0), j)),
                  pl.BlockSpec((r, 256), lambda j, i: (rev(i), j)),
                  small(4), small(1), wblk, wblk, small(1), small(1), small(1),
                  pl.BlockSpec(memory_space=pl.ANY)] + c_in_specs,
        out_specs=[pl.BlockSpec((r, LBLK), lambda j, i: (rev(i), _lblk_col(j))),
                   pl.BlockSpec((64, 256), lambda j, i: (0, j)), wblk, wblk] + c_out_specs,
        out_shape=[jax.ShapeDtypeStruct(dproj.shape, dproj.dtype), jax.ShapeDtypeStruct((64, D), F32),
                   jax.ShapeDtypeStruct((4, 256, 256), F32), jax.ShapeDtypeStruct((4, 256, 256), F32)] + c_out_shapes,
        scratch_shapes=[pltpu.VMEM((8, 256), F32), pltpu.VMEM((8, 256), F32), pltpu.VMEM((16, 256), F32)] + c_sems,
        input_output_aliases={n_in - 1: 0},
        compiler_params=_cp(("arbitrary", "arbitrary") if comm is not None else ("parallel", "arbitrary")),
    )(proj, proj, hl, hl, dya, lw["cw"], lw["cb"], lw["wa"], lw["wx"], lw["ba"], lw["bx"], lw["lam"], dproj, *c_args)


def _head_cols(x):
    lane = lax.broadcasted_iota(jnp.int32, x.shape, 1)
    return [jnp.sum(jnp.where(lane == h, x, 0.0), axis=1, keepdims=True) for h in range(N_HEADS)]


def _compact_heads(blocks):
    lane = lax.broadcasted_iota(jnp.int32, blocks[0].shape, 1)
    lo = lane < HEAD_P
    out = jnp.zeros_like(blocks[0])
    for j, blk in enumerate(blocks):
        s_lo = jnp.sum(jnp.where(lo, blk, 0.0), axis=1, keepdims=True)
        s_hi = jnp.sum(jnp.where(lo, 0.0, blk), axis=1, keepdims=True)
        out = jnp.where(lane == 2 * j, s_lo, out)
        out = jnp.where(lane == 2 * j + 1, s_hi, out)
    return out


def _ssd_prelude(dtraw_ref, dtb_ref, alog_ref, dt_scr, a_scr):
    lane = lax.broadcasted_iota(jnp.int32, dt_scr.shape, 1)
    dt = jnp.where(lane < N_HEADS, _softplus(dtraw_ref[...] + dtb_ref[0:1, :]), 0.0)
    dt_scr[...] = dt
    a_scr[...] = dt * (-jnp.exp(alog_ref[0:1, :]))


def _ssd_chunk_scalars(dt_scr, a_scr, r_scr, r0):
    a_c = a_scr[pl.ds(r0, CHUNK), :]
    dt_c = dt_scr[pl.ds(r0, CHUNK), :]
    i0 = lax.broadcasted_iota(jnp.int32, (CHUNK, CHUNK), 0)
    i1 = lax.broadcasted_iota(jnp.int32, (CHUNK, CHUNK), 1)
    tri = jnp.where(i0 >= i1, 1.0, 0.0).astype(F32)
    cs = jnp.dot(tri, a_c, precision=_HI, preferred_element_type=F32)
    lane = lax.broadcasted_iota(jnp.int32, (CHUNK, 128), 1)
    srow = lax.broadcasted_iota(jnp.int32, (CHUNK, 128), 0)
    t_lo = jnp.where((lane < HEAD_P) & (srow <= lane), 1.0, 0.0).astype(F32)
    t_hi = jnp.where((lane >= HEAD_P) & (srow <= lane - HEAD_P), 1.0, 0.0).astype(F32)
    even = (lane % 2) == 0
    tn = (((0,), (0,)), ((), ()))
    r_scr[...] = (lax.dot_general(jnp.where(even, a_c, 0.0), t_lo, tn, precision=_HI, preferred_element_type=F32)
                  + lax.dot_general(jnp.where(even, 0.0, a_c), t_hi, tn, precision=_HI, preferred_element_type=F32))
    return cs, dt_c, _head_cols(cs), _head_cols(dt_c)


def _block_diag2(v):
    lo = lax.broadcasted_iota(jnp.int32, v.shape, 1) < HEAD_P
    return jnp.concatenate([jnp.where(lo, v, 0.0), jnp.where(lo, 0.0, v)], axis=0).astype(_MXU)


def _ssd_pair(xc_scr, r_scr, cs_cols, dt_cols, s2, r0, j, s2t=None):
    lane = lax.broadcasted_iota(jnp.int32, (CHUNK, 128), 1)
    srow = lax.broadcasted_iota(jnp.int32, (CHUNK, 128), 0)
    lo = lane < HEAD_P
    csc = jnp.where(lo, cs_cols[2 * j], cs_cols[2 * j + 1])
    dtc = jnp.where(lo, dt_cols[2 * j], dt_cols[2 * j + 1])
    csr = r_scr[2 * j:2 * j + 1, :] + r_scr[2 * j + 1:2 * j + 2, :]
    dm = jnp.where((lane & (HEAD_P - 1)) <= srow, jnp.exp(jnp.minimum(csc - csr, 0.0)), 0.0)
    xs = xc_scr[pl.ds(r0, CHUNK), j * 128:(j + 1) * 128]
    xd = xs * dtc
    csl = jnp.sum(jnp.where(srow == CHUNK - 1, csc, 0.0), axis=0, keepdims=True)
    out = dict(csc=csc, dtc=dtc, dm=dm, m2=s2 * dm, xs=xs, xd=xd, rhs=_block_diag2(xd), e=jnp.exp(csc),
               w=jnp.exp(csl - csc), dec=jnp.exp(csl))
    if s2t is not None:
        out["mt2"] = s2t * jnp.where((lane & (HEAD_P - 1)) >= srow, jnp.exp(jnp.minimum(csr - csc, 0.0)), 0.0)
    return out


def _cat(parts):
    return jnp.concatenate(parts, axis=1)


def _ssd_fwd(proj, dtraw, sw, *, rb, name, comm=None):
    t = proj.shape[0]
    ns, cb = t // rb, rb // CHUNK
    c_args, c_in_specs, c_out_specs, c_out_shapes, c_sems = _comm_specs(comm)

    def body(*refs):
        zx_ref, zp_ref, dtraw_ref, cw_ref, cbias_ref, dtb_ref, alog_ref, dsk_ref, ng_ref = refs[:9]
        yssd_ref, yb_ref, st_ref, xc_scr, dsl_ref = refs[9 + len(c_args):14 + len(c_args)]
        n_scr = 14 + len(c_args) + len(c_out_shapes)
        h_scr, dt_scr, a_scr, r_scr = refs[n_scr:n_scr + 4]
        i = pl.program_id(0)
        if comm is not None:
            _comm_steps(comm, refs, 9, 14 + len(c_args), i == 0, i == (3 * ns) // 4, i == ns - 1)

        @pl.when(i == 0)
        def _():
            h_scr[...] = jnp.zeros_like(h_scr)

        for j in range(XBC // 128):
            cs_, zc = slice(128 * j, 128 * (j + 1)), slice(2048 + 128 * j, 2048 + 128 * (j + 1))
            pre = _conv4(zx_ref[:, zc], jnp.where(i == 0, 0.0, zp_ref[:, zc]), cw_ref, cbias_ref, cs_)
            sg = _sigmoid(pre)
            xc_scr[:, cs_] = pre * sg
            dsl_ref[:, cs_] = sg * (1.0 + pre * (1.0 - sg))
        _ssd_prelude(dtraw_ref, dtb_ref, alog_ref, dt_scr, a_scr)

        def chunk(c, carry):
            r0 = pl.multiple_of(c * CHUNK, CHUNK)
            _, _, cs_cols, dt_cols = _ssd_chunk_scalars(dt_scr, a_scr, r_scr, r0)
            st_ref[c] = h_scr[...]
            for g in range(N_GROUPS):
                bg = xc_scr[pl.ds(r0, CHUNK), 2048 + 128 * g:2048 + 128 * (g + 1)]
                cg = xc_scr[pl.ds(r0, CHUNK), 2560 + 128 * g:2560 + 128 * (g + 1)]
                s2 = _dot_nt(cg, jnp.concatenate([bg, bg], axis=0))
                hp = h_scr[:, 512 * g:512 * (g + 1)]
                yoff = _dot(cg, hp)
                xdw, dec = [], []
                for jj in range(4):
                    j = 4 * g + jj
                    p = _ssd_pair(xc_scr, r_scr, cs_cols, dt_cols, s2, r0, j)
                    y = _dot(p["m2"], p["rhs"]) + yoff[:, 128 * jj:128 * (jj + 1)] * p["e"]
                    yssd_ref[pl.ds(r0, CHUNK), 128 * j:128 * (j + 1)] = y + dsk_ref[0:1, 128 * j:128 * (j + 1)] * p["xs"]
                    xdw.append(p["xd"] * p["w"])
                    dec.append(p["dec"])
                h_scr[:, 512 * g:512 * (g + 1)] = hp * _cat(dec) + _dot_tn(bg, _cat(xdw))
            return carry

        lax.fori_loop(0, cb, chunk, 0)
        for g in range(N_GROUPS):
            sl = slice(512 * g, 512 * (g + 1))
            for q in range(rb // NORM_ROWS):
                rw = slice(NORM_ROWS * q, NORM_ROWS * (q + 1))
                yz = yssd_ref[rw, sl] * _silu(zx_ref[rw, sl])
                rg = lax.rsqrt(jnp.mean(yz * yz, axis=-1, keepdims=True) + EPS)
                yb_ref[rw, sl] = (yz * rg * ng_ref[0:1, sl]).astype(yb_ref.dtype)

    full = lambda rows, cols: pl.BlockSpec((rows, cols), lambda i: (0, 0))
    return pl.pallas_call(
        body, name=name, grid=(ns,),
        in_specs=[pl.BlockSpec((rb, ZX_W), lambda i: (i, 0)),
                  pl.BlockSpec((8, ZX_W), lambda i: (jnp.maximum(i * (rb // 8) - 1, 0), 0)),
                  pl.BlockSpec((rb, DT_PAD), lambda i: (i, 0)),
                  full(4, XBC), full(1, XBC), full(1, DT_PAD), full(1, DT_PAD), full(1, SSD_INNER), full(1, SSD_INNER)]
        + c_in_specs,
        out_specs=[pl.BlockSpec((rb, SSD_INNER), lambda i: (i, 0)), pl.BlockSpec((rb, SSD_INNER), lambda i: (i, 0)),
                   pl.BlockSpec((cb, N_STATE, SSD_INNER), lambda i: (i, 0, 0)),
                   pl.BlockSpec((rb, XBC), lambda i: (i, 0)), pl.BlockSpec((rb, XBC), lambda i: (i, 0))] + c_out_specs,
        out_shape=[jax.ShapeDtypeStruct((t, SSD_INNER), F32), jax.ShapeDtypeStruct((t, SSD_INNER), _MXU),
                   jax.ShapeDtypeStruct((t // CHUNK, N_STATE, SSD_INNER), F32),
                   jax.ShapeDtypeStruct((t, XBC), F32), jax.ShapeDtypeStruct((t, XBC), F32)] + c_out_shapes,
        scratch_shapes=[pltpu.VMEM((N_STATE, SSD_INNER), F32), pltpu.VMEM((rb, DT_PAD), F32),
                        pltpu.VMEM((rb, DT_PAD), F32), pltpu.VMEM((128, 128), F32)] + c_sems,
        compiler_params=_cp(("arbitrary",)),
    )(proj, proj, dtraw, sw["cw"], sw["cb"], sw["dtb"], sw["alog"], sw["dsk"], sw["ng"], *c_args)


def _ssd_bwd(proj, dtraw, yssd, states, xc, dsl, dyb, dproj, sw, *, rb, name, comm=None):
    t = proj.shape[0]
    ns, cb = t // rb, rb // CHUNK
    c_args, c_in_specs, c_out_specs, c_out_shapes, c_sems = _comm_specs(comm)
    n_in = 13

    def body(*refs):
        zx_ref, dtraw_ref, yssd_ref, st_ref, xc_scr, dsl_scr, dyb_ref, cw_ref, dtb_ref, alog_ref, dsk_ref, ng_ref = refs[:12]
        n_out = n_in + len(c_args)
        dzx_ref, ddt_ref, gconv_ref, gch_ref, ghd_ref = refs[n_out:n_out + 5]
        n_scr = n_out + 5 + len(c_out_shapes)
        dht_scr, dy_scr, dxc_scr, dt_scr, a_scr, r_scr, dp8_scr = refs[n_scr:n_scr + 7]
        i = pl.program_id(0)
        if comm is not None:
            _comm_steps(comm, refs, n_in, n_out + 5, i == 0, i == (3 * ns) // 4, i == ns - 1)

        @pl.when(i == 0)
        def _():
            dht_scr[...] = jnp.zeros_like(dht_scr)
            dp8_scr[...] = jnp.zeros_like(dp8_scr)
            gconv_ref[...] = jnp.zeros_like(gconv_ref)
            gch_ref[...] = jnp.zeros_like(gch_ref)
            ghd_ref[...] = jnp.zeros_like(ghd_ref)

        _ssd_prelude(dtraw_ref, dtb_ref, alog_ref, dt_scr, a_scr)

        for g in range(N_GROUPS):
            sl = slice(512 * g, 512 * (g + 1))
            for q in range(rb // NORM_ROWS):
                rw = slice(NORM_ROWS * q, NORM_ROWS * (q + 1))
                zv = zx_ref[rw, sl]
                ys = yssd_ref[rw, sl]
                sg = _sigmoid(zv)
                sz = zv * sg
                yz = ys * sz
                rg = lax.rsqrt(jnp.mean(yz * yz, axis=-1, keepdims=True) + EPS)
                yn = yz * rg
                dyb_v = dyb_ref[rw, sl]
                gch_ref[0:8, sl] += _rsum8(dyb_v * yn)
                dyn = dyb_v * ng_ref[0:1, sl]
                dyz = rg * (dyn - yn * jnp.mean(dyn * yn, axis=-1, keepdims=True))
                dy_scr[rw, sl] = dyz * sz
                dzx_ref[rw, sl] = (dyz * ys * (sg * (1.0 + zv * (1.0 - sg)))).astype(dzx_ref.dtype)

        a_row = -jnp.exp(alog_ref[0:1, :])

        def chunk(cc, carry):
            c = cb - 1 - cc
            r0 = pl.multiple_of(c * CHUNK, CHUNK)
            rows = pl.ds(r0, CHUNK)
            _, dt_c, cs_cols, dt_cols = _ssd_chunk_scalars(dt_scr, a_scr, r_scr, r0)
            lane = lax.broadcasted_iota(jnp.int32, (CHUNK, 128), 1)
            srow = lax.broadcasted_iota(jnp.int32, (CHUNK, 128), 0)
            lo = lane < HEAD_P
            last = srow == CHUNK - 1
            p1_blocks, p3_blocks = [], []
            for g in range(N_GROUPS):
                gs = slice(512 * g, 512 * (g + 1))
                bg = xc_scr[rows, 2048 + 128 * g:2048 + 128 * (g + 1)]
                cg = xc_scr[rows, 2560 + 128 * g:2560 + 128 * (g + 1)]
                b2 = jnp.concatenate([bg, bg], axis=0)
                s2 = _dot_nt(cg, b2)
                s2t = _dot_nt(bg, jnp.concatenate([cg, cg], axis=0))
                hp = st_ref[c, :, gs]
                dht = dht_scr[:, gs]
                yoff = _dot(cg, hp)
                ps = [_ssd_pair(xc_scr, r_scr, cs_cols, dt_cols, s2, r0, 4 * g + jj, s2t) for jj in range(4)]
                dys = [dy_scr[rows, 128 * (4 * g + jj):128 * (4 * g + jj + 1)] for jj in range(4)]
                dye = _cat([dys[jj] * ps[jj]["e"] for jj in range(4)])
                w_g = _cat([p["w"] for p in ps])
                dcg = _dot_nt(dye, hp)
                dht_scr[:, gs] = _dot_tn(cg, dye) + _cat([p["dec"] for p in ps]) * dht
                dxd_state = w_g * _dot(bg, dht)
                dbg = _dot_nt(_cat([p["xd"] for p in ps]) * w_g, dht)
                tsum = _rsum(dht * hp)
                ds2 = jnp.zeros((CHUNK, 128), F32)
                for jj in range(4):
                    j = 4 * g + jj
                    ls = slice(128 * j, 128 * (j + 1))
                    p, dy2 = ps[jj], dys[jj]
                    dy_bd = _block_diag2(dy2)
                    dm2 = _dot_nt(dy2, p["rhs"])
                    ds2 = ds2 + dm2 * p["dm"]
                    gdiff = dm2 * p["m2"] - _dot_nt(p["xd"], dy_bd) * p["mt2"]
                    dxs = dxd_state[:, 128 * jj:128 * (jj + 1)]
                    dxd = _dot(p["mt2"], dy_bd) + dxs
                    end_row = _rsum(p["xd"] * dxs) + p["dec"] * tsum[:, 128 * jj:128 * (jj + 1)]
                    p1_blocks.append(gdiff + dy2 * yoff[:, 128 * jj:128 * (jj + 1)] * p["e"] - p["xd"] * dxs
                                     + jnp.where(last, end_row, 0.0))
                    p3_blocks.append(dxd * p["xs"])
                    dxc_scr[rows, ls] = dxd * p["dtc"] + dy2 * dsk_ref[0:1, ls]
                    gch_ref[8:16, ls] += _rsum8(dy2 * p["xs"])
                dcg = dcg + _dot(ds2, b2)
                rb2 = _dot_tn(ds2, cg)
                dxc_scr[rows, 2048 + 128 * g:2048 + 128 * (g + 1)] = dbg + rb2[0:CHUNK] + rb2[CHUNK:2 * CHUNK]
                dxc_scr[rows, 2560 + 128 * g:2560 + 128 * (g + 1)] = dcg
            dcs = _compact_heads(p1_blocks)
            i0 = lax.broadcasted_iota(jnp.int32, (CHUNK, CHUNK), 0)
            i1 = lax.broadcasted_iota(jnp.int32, (CHUNK, CHUNK), 1)
            triu = jnp.where(i1 >= i0, 1.0, 0.0).astype(F32)
            da = jnp.dot(triu, dcs, precision=_HI, preferred_element_type=F32)
            ddt = _compact_heads(p3_blocks) + da * a_row
            ddtraw = jnp.where(lane < N_HEADS, ddt * _sigmoid(dtraw_ref[rows, :] + dtb_ref[0:1, :]), 0.0)
            ddt_ref[rows, :] = ddtraw.astype(ddt_ref.dtype)
            ghd_ref[0:1, :] += _rsum(ddtraw)
            ghd_ref[1:2, :] += _rsum(da * dt_c) * a_row
            return carry

        lax.fori_loop(0, cb, chunk, 0)
        for j in range(XBC // 128):
            cs_, zc = slice(128 * j, 128 * (j + 1)), slice(2048 + 128 * j, 2048 + 128 * (j + 1))
            dpre = dxc_scr[:, cs_] * dsl_scr[:, cs_]
            xraw = zx_ref[:, zc]
            dx = dpre * cw_ref[3:4, cs_]
            gconv_ref[24:32, cs_] += _rsum8(dpre * xraw)
            for k in (1, 2, 3):
                dpre_k = _shift_up(dpre, dp8_scr[:, cs_], k)
                dx = dx + dpre_k * cw_ref[3 - k:4 - k, cs_]
                gconv_ref[8 * (3 - k):8 * (4 - k), cs_] += _rsum8(dpre_k * xraw)
            dzx_ref[:, zc] = dx.astype(dzx_ref.dtype)
            dp8_scr[:, cs_] = dpre[0:8]
            gconv_ref[32:40, cs_] += _rsum8(dpre)

    rev = lambda i: ns - 1 - i
    full = lambda rows, cols: pl.BlockSpec((rows, cols), lambda i: (0, 0))
    return pl.pallas_call(
        body, name=name, grid=(ns,),
        in_specs=[pl.BlockSpec((rb, ZX_W), lambda i: (rev(i), 0)),
                  pl.BlockSpec((rb, DT_PAD), lambda i: (rev(i), 0)),
                  pl.BlockSpec((rb, SSD_INNER), lambda i: (rev(i), 0)),
                  pl.BlockSpec((cb, N_STATE, SSD_INNER), lambda i: (rev(i), 0, 0)),
                  pl.BlockSpec((rb, XBC), lambda i: (rev(i), 0)), pl.BlockSpec((rb, XBC), lambda i: (rev(i), 0)),
                  pl.BlockSpec((rb, SSD_INNER), lambda i: (rev(i), 0)),
                  full(4, XBC), full(1, DT_PAD), full(1, DT_PAD), full(1, SSD_INNER), full(1, SSD_INNER),
                  pl.BlockSpec(memory_space=pl.ANY)] + c_in_specs,
        out_specs=[pl.BlockSpec((rb, ZX_W), lambda i: (rev(i), 0)), pl.BlockSpec((rb, DT_PAD), lambda i: (rev(i), 0)),
                   full(40, XBC), full(16, SSD_INNER), full(8, DT_PAD)] + c_out_specs,
        out_shape=[jax.ShapeDtypeStruct(dproj.shape, dproj.dtype), jax.ShapeDtypeStruct((t, DT_PAD), _MXU),
                   jax.ShapeDtypeStruct((40, XBC), F32), jax.ShapeDtypeStruct((16, SSD_INNER), F32),
                   jax.ShapeDtypeStruct((8, DT_PAD), F32)] + c_out_shapes,
        scratch_shapes=[pltpu.VMEM((N_STATE, SSD_INNER), F32),
                        pltpu.VMEM((rb, SSD_INNER), F32), pltpu.VMEM((rb, XBC), F32), pltpu.VMEM((rb, DT_PAD), F32),
                        pltpu.VMEM((rb, DT_PAD), F32), pltpu.VMEM((128, 128), F32), pltpu.VMEM((8, XBC), F32)] + c_sems,
        input_output_aliases={n_in - 1: 0},
        compiler_params=_cp(("arbitrary",)),
    )(proj, dtraw, yssd, states, xc, dsl, dyb, sw["cw"], sw["dtb"], sw["alog"], sw["dsk"], sw["ng"], dproj, *c_args)


def _branch_merge(ya, yb, proj, wba, wbb, bgate, *, tm, tn, name):
    t = ya.shape[0]
    nj = D // tn

    def body(ya_ref, yb_ref, ga_ref, gb_ref, wba_ref, wbb_ref, ba_ref, bb_ref, ta_ref, tb_ref, mg_ref):
        ta = _dot(ya_ref[...], wba_ref[...])
        tb = _dot(yb_ref[...], wbb_ref[...])
        ta_ref[...] = ta.astype(ta_ref.dtype)
        tb_ref[...] = tb.astype(tb_ref.dtype)
        ga = _sigmoid(ga_ref[...] + ba_ref[...])
        gb = _sigmoid(gb_ref[...] + bb_ref[...])
        mg_ref[...] = (ga * ta + gb * tb).astype(mg_ref.dtype)

    tile = pl.BlockSpec((tm, tn), lambda i, j: (i, j))
    return pl.pallas_call(
        body, name=name, grid=(t // tm, nj),
        in_specs=[pl.BlockSpec((tm, D), lambda i, j: (i, 0)), pl.BlockSpec((tm, SSD_INNER), lambda i, j: (i, 0)),
                  pl.BlockSpec((tm, tn), lambda i, j: (i, G0 // tn + j)),
                  pl.BlockSpec((tm, tn), lambda i, j: (i, (G0 + D) // tn + j)),
                  pl.BlockSpec((D, tn), lambda i, j: (0, j)), pl.BlockSpec((SSD_INNER, tn), lambda i, j: (0, j)),
                  pl.BlockSpec((1, tn), lambda i, j: (0, j)), pl.BlockSpec((1, tn), lambda i, j: (0, nj + j))],
        out_specs=[tile, tile, tile],
        out_shape=[jax.ShapeDtypeStruct((t, D), _MXU)] * 3,
        compiler_params=_cp(("parallel", "parallel")),
    )(ya, yb, proj, proj, wba, wbb, bgate, bgate)


def _swiglu_mm(gu, wfo, residual, *, tm, tn, name):
    t = gu.shape[0]

    def body(gu_ref, w_ref, r_ref, act_ref, o_ref):
        @pl.when(pl.program_id(1) == 0)
        def _():
            gate = gu_ref[:, 0:D_FF].astype(F32)
            act_ref[...] = (_silu(gate) * gu_ref[:, D_FF:2 * D_FF].astype(F32)).astype(act_ref.dtype)
        o_ref[...] = jnp.dot(act_ref[...], w_ref[...], preferred_element_type=F32) + r_ref[...]

    return pl.pallas_call(
        body, name=name, grid=(t // tm, D // tn),
        in_specs=[pl.BlockSpec((tm, 2 * D_FF), lambda i, j: (i, 0)), pl.BlockSpec((D_FF, tn), lambda i, j: (0, j)),
                  pl.BlockSpec((tm, tn), lambda i, j: (i, j))],
        out_specs=[pl.BlockSpec((tm, D_FF), lambda i, j: (i, 0)), pl.BlockSpec((tm, tn), lambda i, j: (i, j))],
        out_shape=[jax.ShapeDtypeStruct((t, D_FF), _MXU), jax.ShapeDtypeStruct((t, D), F32)],
        compiler_params=_cp(("parallel", "arbitrary")),
    )(gu, wfo, residual)


def _ffn_bwd_act(dh, wfo, gu, *, tm, name, comm=None):
    t = dh.shape[0]
    ni = t // tm
    c_args, c_in_specs, c_out_specs, c_out_shapes, c_sems = _comm_specs(comm)

    def body(*refs):
        dh_ref, w_ref, gu_ref = refs[:3]
        o_ref = refs[3 + len(c_args)]
        if comm is not None:
            i = pl.program_id(0)
            _comm_steps(comm, refs, 3, 4 + len(c_args), i == 0, i == (3 * ni) // 4, i == ni - 1)
        dact = _dot_nt(dh_ref[...], w_ref[...])
        g = gu_ref[:, 0:D_FF].astype(F32)
        u = gu_ref[:, D_FF:2 * D_FF].astype(F32)
        sg = _sigmoid(g)
        o_ref[:, 0:D_FF] = (dact * u * (sg * (1.0 + g * (1.0 - sg)))).astype(o_ref.dtype)
        o_ref[:, D_FF:2 * D_FF] = (dact * (g * sg)).astype(o_ref.dtype)

    return pl.pallas_call(
        body, name=name, grid=(ni,),
        in_specs=[pl.BlockSpec((tm, D), lambda i: (i, 0)), pl.BlockSpec((D_FF, D), lambda i: (0, 0)),
                  pl.BlockSpec((tm, 2 * D_FF), lambda i: (i, 0))] + c_in_specs,
        out_specs=[pl.BlockSpec((tm, 2 * D_FF), lambda i: (i, 0))] + c_out_specs,
        out_shape=[jax.ShapeDtypeStruct((t, 2 * D_FF), _MXU)] + c_out_shapes,
        scratch_shapes=c_sems,
        compiler_params=_cp(("arbitrary",) if comm is not None else ("parallel",)),
    )(dh, wfo, gu, *c_args)


def _outproj_bwd(dh, wout, ta, tb, proj, bgate, dproj, *, tm, name):
    t = dh.shape[0]

    def body(dh_ref, w_ref, ta_ref, tb_ref, g_ref, b_ref, dta_ref, dtb_ref, dg_ref, db_ref):
        @pl.when(pl.program_id(0) == 0)
        def _():
            db_ref[...] = jnp.zeros_like(db_ref)
        dm = _dot_nt(dh_ref[...], w_ref[...])
        ga = _sigmoid(g_ref[:, 0:D] + b_ref[:, 0:D])
        gb = _sigmoid(g_ref[:, D:2 * D] + b_ref[:, D:2 * D])
        dta_ref[...] = (dm * ga).astype(dta_ref.dtype)
        dtb_ref[...] = (dm * gb).astype(dtb_ref.dtype)
        dga = dm * ta_ref[...].astype(F32) * ga * (1.0 - ga)
        dgb = dm * tb_ref[...].astype(F32) * gb * (1.0 - gb)
        dg_ref[:, 0:D] = dga.astype(dg_ref.dtype)
        dg_ref[:, D:2 * D] = dgb.astype(dg_ref.dtype)
        db_ref[0:1, 0:D] += _rsum(dga)
        db_ref[0:1, D:2 * D] += _rsum(dgb)

    row = lambda cols: pl.BlockSpec((tm, cols), lambda i: (i, 0))
    return pl.pallas_call(
        body, name=name, grid=(t // tm,),
        in_specs=[row(D), pl.BlockSpec((D, D), lambda i: (0, 0)), row(D), row(D),
                  pl.BlockSpec((tm, 2 * D), lambda i: (i, G0 // (2 * D))), pl.BlockSpec((1, 2 * D), lambda i: (0, 0))],
        out_specs=[row(D), row(D), pl.BlockSpec((tm, 2 * D), lambda i: (i, G0 // (2 * D))),
                   pl.BlockSpec((8, 2 * D), lambda i: (0, 0))],
        out_shape=[jax.ShapeDtypeStruct((t, D), _MXU), jax.ShapeDtypeStruct((t, D), _MXU),
                   jax.ShapeDtypeStruct(dproj, _MXU), jax.ShapeDtypeStruct((8, 2 * D), F32)],
        compiler_params=_cp(("arbitrary",)),
    )(dh, wout, ta, tb, proj, bgate)


def _loss_head(h, gf, target, *, tm, name):
    t = h.shape[0]

    def body(h_ref, g_ref, t_ref, loss_ref, dg_ref, dh_ref):
        @pl.when(pl.program_id(0) == 0)
        def _():
            loss_ref[...] = jnp.zeros_like(loss_ref)
            dg_ref[...] = jnp.zeros_like(dg_ref)
        x = h_ref[...]
        r = lax.rsqrt(jnp.mean(x * x, axis=-1, keepdims=True) + EPS)
        xh = x * r
        err = xh * g_ref[...] - t_ref[...]
        loss_ref[...] += 0.5 * jnp.sum(jnp.mean(err * err, axis=-1, keepdims=True), axis=0, keepdims=True)
        dy = err * (1.0 / D)
        dg_ref[0:1, :] += _rsum(dy * xh)
        dxh = dy * g_ref[...]
        dh_ref[...] = r * (dxh - xh * jnp.mean(dxh * xh, axis=-1, keepdims=True))

    row = pl.BlockSpec((tm, D), lambda i: (i, 0))
    return pl.pallas_call(
        body, name=name, grid=(t // tm,),
        in_specs=[row, pl.BlockSpec((1, D), lambda i: (0, 0)), row],
        out_specs=[pl.BlockSpec((8, 128), lambda i: (0, 0)), pl.BlockSpec((8, D), lambda i: (0, 0)), row],
        out_shape=[jax.ShapeDtypeStruct((8, 128), F32), jax.ShapeDtypeStruct((8, D), F32), jax.ShapeDtypeStruct((t, D), F32)],
        compiler_params=_cp(("arbitrary",)),
    )(h, gf, target)


def _row_tile(rows, cols, limit_bytes=1 << 20):
    best = None
    for tr in range(8, rows + 1, 8):
        if rows % tr == 0 and tr * cols * 4 <= limit_bytes:
            best = tr
    return best if best is not None else rows


def _adamw(w, g, m, v, *, name):
    rows, cols = w.shape
    tr = _row_tile(rows, cols)

    def body(w_ref, g_ref, m_ref, v_ref, d_ref, nm_ref, nv_ref):
        gv = g_ref[...]
        nm = ADAM_B1 * m_ref[...] + (1.0 - ADAM_B1) * gv
        nv = ADAM_B2 * v_ref[...] + (1.0 - ADAM_B2) * (gv * gv)
        m_hat = nm / (1.0 - ADAM_B1 ** ADAM_STEP)
        v_hat = nv / (1.0 - ADAM_B2 ** ADAM_STEP)
        d_ref[...] = -ADAM_LR * (m_hat / (jnp.sqrt(v_hat) + ADAM_EPS) + ADAM_WD * w_ref[...])
        nm_ref[...] = nm
        nv_ref[...] = nv

    blk = pl.BlockSpec((tr, cols), lambda i: (i, 0))
    shp = jax.ShapeDtypeStruct((rows, cols), F32)
    return pl.pallas_call(
        body, name=name, grid=(rows // tr,), in_specs=[blk] * 4, out_specs=[blk] * 3, out_shape=[shp] * 3,
        compiler_params=_cp(("parallel",)),
    )(w, g, m, v)


def _bd256(w):
    w4 = w.reshape(4, 4, 64, 64)
    eye = jnp.eye(4, dtype=w.dtype)
    return (w4[:, :, :, None, :] * eye[None, :, None, :, None]).reshape(4, 256, 256)


def _bd256_diag(g):
    g5 = g.reshape(4, 4, 64, 4, 64)
    return jnp.stack([g5[:, a, :, a, :] for a in range(4)], axis=1).reshape(16, 64, 64)


FFN_SHARD = 2 * D_FF // 4
W_IN_SHARD = IN_DIM // 4
W_IN_ROWS = 9344


def _w_in_cols(shards, c0, c1):
    out = []
    for p in range(4):
        lo, hi = max(c0, W_IN_SHARD * p), min(c1, W_IN_SHARD * (p + 1))
        if lo < hi:
            out.append(shards[p][:, lo - W_IN_SHARD * p:hi - W_IN_SHARD * p])
    return out


def _in_proj_weights(win):
    lblk = [_w_in_cols(win, 256 * j, 256 * (j + 1)) + _w_in_cols(win, D + 256 * j, D + 256 * (j + 1)) for j in range(4)]
    wp = jnp.concatenate(_w_in_cols(win, 2048, 4096) + _w_in_cols(win, 4096, 7168) + lblk[0] + lblk[1]
                         + _w_in_cols(win, 7200, 9248) + lblk[2] + lblk[3], axis=1)
    wdt = jnp.pad(jnp.concatenate(_w_in_cols(win, 7168, 7200), axis=1), ((0, 0), (0, DT_PAD - N_HEADS)))
    return wp, wdt


def _layer_weights(w, conv, small, l, wp, wdt):
    row = lambda v: v.reshape(1, -1)
    pad_h = lambda v: jnp.pad(v.reshape(1, -1), ((0, 0), (0, DT_PAD - N_HEADS)))
    lw = dict(cw=conv["lru_conv_w"][l], cb=row(small["lru_conv_b"][l]),
              wa=_bd256(small["lru_w_a"][l]).astype(_MXU), wx=_bd256(small["lru_w_x"][l]).astype(_MXU),
              ba=row(small["lru_b_a"][l]), bx=row(small["lru_b_x"][l]), lam=row(small["lru_lambda"][l]))
    sw = dict(cw=conv["ssd_conv_w"][l], cb=row(small["ssd_conv_b"][l]), dtb=pad_h(small["ssd_dt_bias"][l]),
              alog=pad_h(small["ssd_A_log"][l]), dsk=row(jnp.repeat(small["ssd_D"][l], HEAD_P)),
              ng=row(small["ssd_norm_g"][l]))
    return dict(wp=wp, wdt=wdt, lw=lw, sw=sw, wba=w["w_branch"][0:D], wbb=w["w_branch"][D:3 * D],
                wout=w["w_out"], wfi=w["w_ffn_in"], wfo=w["w_ffn_out"],
                g1=row(small["norm1_g"][l]), g2=row(small["norm2_g"][l]), bgate=row(small["b_gate"][l]))


def _tiles(t):
    return dict(tmn=min(1024, t), tm=min(512, t), tm2=min(256, t), r=min(256, t), rb=min(128, t))


def _layer_fwd(h, w, conv, small, l, carried=None):
    tl = _tiles(h.shape[0])
    n = f"l{l}_"
    carried = carried or {}
    arrived = []

    def carry(kernel, key, n_main, *args, **kw):
        comm, finish = carried.get(key, (None, None))
        outs = list(kernel(*args, comm=comm, **kw))
        if comm is not None:
            arrived.append(finish(outs[n_main:]))
        return outs[:n_main]

    wp, wdt = _in_proj_weights(w["w_in"])
    xn, proj = carry(_norm_mm, "in_proj", 2, h, small["norm1_g"][l].reshape(1, -1), wp, tm=tl["tmn"], tn=1024,
                     name=n + "in_proj")
    w = dict(w)
    for layer, ws in arrived:
        if layer == l:
            w.update(ws)
    lwt = _layer_weights(w, conv, small, l, wp, wdt)
    dtraw = _mm_nn(xn, lwt["wdt"], tm=tl["tm"], tn=DT_PAD, name=n + "dt_proj")
    hl, ya = carry(_lru_fwd, "lru", 2, proj, lwt["lw"], r=tl["r"], name=n + "lru_fwd")
    yssd, yb, states, xc, dsl = carry(_ssd_fwd, "ssd", 5, proj, dtraw, lwt["sw"], rb=tl["rb"], name=n + "ssd_fwd")
    ta, tb, merged = _branch_merge(ya, yb, proj, lwt["wba"], lwt["wbb"], lwt["bgate"], tm=tl["tm"], tn=512, name=n + "merge")
    hmid = _mm_nn(merged, lwt["wout"], tm=tl["tm"], tn=512, name=n + "out_proj", residual=h)
    xn2, gu = _norm_mm(hmid, lwt["g2"], lwt["wfi"], tm=tl["tmn"], tn=FFN_SHARD, name=n + "ffn_in", out_dtype=_MXU)
    act, hout = _swiglu_mm(gu, lwt["wfo"], hmid, tm=tl["tm"], tn=512, name=n + "ffn_out")
    saved = dict(h=h, xn=xn, proj=proj, dtraw=dtraw, hl=hl, ya=ya, yssd=yssd, yb=yb, states=states, xc=xc, dsl=dsl, ta=ta, tb=tb,
                 merged=merged, hmid=hmid, xn2=xn2, gu=gu, act=act)
    return hout, saved, lwt, [x for x in arrived if x[0] != l]


def _layer_bwd(dh, s, lwt, l, hooks=None):
    t = dh.shape[0]
    tl = _tiles(t)
    n = f"l{l}_"
    tt = tl["tmn"]
    big = {}
    hooks = hooks or {}

    def wgrad(key, a, b, name, **kw):
        big[key] = _wgrad(a, b, tt=tt, name=n + name, into=big.get(key), **kw)

    dgu, = _ffn_bwd_act(dh, lwt["wfo"], s["gu"], tm=tl["tm2"], name=n + "ffn_act_bwd")
    wgrad("w_ffn_out", s["act"], dh, "ffn_out_wgrad", ta=D_FF, tn=1024, out_shape=(D_FF, D),
          out_block=(D_FF, 1024), out_index=lambda o, j: (o, j))
    wgrad("w_ffn_in", s["xn2"], dgu, "ffn_in_wgrad", ta=D, tn=FFN_SHARD, out_shape=(4, D, FFN_SHARD),
          out_block=(None, D, FFN_SHARD), out_index=lambda o, j: (j, o, 0))
    dh1, dg2 = _mm_nt_rmsbwd(dgu, lwt["wfi"], s["hmid"], lwt["g2"], dh, tm=tl["tm"], tk=FFN_SHARD, name=n + "ffn_in_dgrad")
    dta, dtb, dproj, dbg = _outproj_bwd(dh1, lwt["wout"], s["ta"], s["tb"], s["proj"], lwt["bgate"], (t, NP),
                                        tm=tl["tm2"], name=n + "out_proj_bwd")
    rows_d = dict(ta=D, tn=512, out_block=(D, 512), out_index=lambda o, j: (o, j))
    wgrad("w_out", s["merged"], dh1, "out_proj_wgrad", out_shape=(D, D), **rows_d)
    dya = _mm_nt(dta, lwt["wba"], tm=tl["tm"], name=n + "branch_a_dgrad")
    dyb = _mm_nt(dtb, lwt["wbb"], tm=tl["tm"], name=n + "branch_b_dgrad")
    wgrad("w_branch", s["ya"], dta, "branch_a_wgrad", out_shape=(3 * D, D), a_tab=[0], o_tab=[0], **rows_d)
    wgrad("w_branch", s["yb"], dtb, "branch_b_wgrad", out_shape=(3 * D, D), a_tab=[0, 1], o_tab=[1, 2], **rows_d)
    comm_1 = hooks["lru"](big) if "lru" in hooks else None
    dproj, lsm, dwa, dwx, *got_1 = _lru_bwd(s["proj"], s["hl"], dya, dproj, lwt["lw"], r=tl["r"], name=n + "lru_bwd",
                                            comm=comm_1)
    comm_2 = hooks["ssd"](got_1) if "ssd" in hooks else None
    dproj, ddt, gconv, gch, ghd, *got_2 = _ssd_bwd(s["proj"], s["dtraw"], s["yssd"], s["states"], s["xc"], s["dsl"], dyb, dproj, lwt["sw"],
                                                   rb=tl["rb"], name=n + "ssd_bwd", comm=comm_2)
    lsm = lsm.reshape(8, 8, D).sum(axis=1)
    gconv = gconv.reshape(5, 8, XBC).sum(axis=1)
    gch = gch.reshape(2, 8, SSD_INNER).sum(axis=1)
    w_in = dict(tn=D, out_shape=(W_IN_ROWS, D), out_index=lambda o, j: (o, j))
    wgrad("w_in", dproj, s["xn"], "in_proj_wgrad", ta=1024, out_block=(1024, D),
          a_tab=list(range(9)), o_tab=[2, 3, 4, 5, 6, 0, 7, 8, 1], **w_in)
    wgrad("w_in", ddt, s["xn"], "dt_proj_wgrad", ta=DT_PAD, out_block=(DT_PAD, D), a_tab=[0],
          o_tab=[NP // DT_PAD], **w_in)
    comm_3 = hooks["in_dgrad"](big) if "in_dgrad" in hooks else None
    dh0, dg1, *got_3 = _mm_nt_rmsbwd(dproj, lwt["wp"], s["h"], lwt["g1"], dh1, tm=tl["tm"], tk=4608,
                                     name=n + "in_proj_dgrad", extra=(ddt, lwt["wdt"]), comm=comm_3)
    grads = dict(
        lru_conv_w=lsm[0:4], lru_conv_b=lsm[4], lru_b_a=lsm[5], lru_b_x=lsm[6], lru_lambda=lsm[7],
        lru_w_a=_bd256_diag(dwa), lru_w_x=_bd256_diag(dwx),
        ssd_conv_w=gconv[0:4], ssd_conv_b=gconv[4], ssd_norm_g=gch[0], ssd_D=gch[1].reshape(N_HEADS, HEAD_P).sum(axis=-1),
        ssd_dt_bias=ghd[0, 0:N_HEADS], ssd_A_log=ghd[1, 0:N_HEADS],
        b_gate=dbg[0], norm1_g=dg1[0], norm2_g=dg2[0])
    return dh0, grads, big, (got_2, got_3)


def _local_step(x, target, w, conv, small, prefetch=None, early_reduce=None):
    h = x
    w = [dict(wl) for wl in w]
    lwts, saved = [], []
    for l in range(N_LAYERS):
        h, s, lwt, arrived = _layer_fwd(h, w[l], conv, small, l, prefetch if l == 0 else None)
        for layer, ws in arrived:
            w[layer].update(ws)
        lwts.append(lwt)
        saved.append(s)
    loss_blk, dgf, dh = _loss_head(h, small["norm_f"].reshape(1, D), target, tm=_tiles(x.shape[0])["tm"], name="loss_head")
    per_layer, big, carried = [None] * N_LAYERS, [None] * N_LAYERS, None
    for l in reversed(range(N_LAYERS)):
        hooks = early_reduce(big[1]) if (early_reduce is not None and l == 0) else None
        dh, per_layer[l], big[l], carried = _layer_bwd(dh, saved[l], lwts[l], l, hooks)
    grads = {k: jnp.stack([per_layer[l][k] for l in range(N_LAYERS)], axis=0) for k in per_layer[0]}
    grads["norm_f"] = dgf[0]
    return loss_blk, dh, grads, big, carried


PACK_W = 1024
BIG = (("w_in", W_IN_SHARD, D, W_IN_SHARD, 256), ("w_branch", 768, D, 256, D), ("w_out", 256, D, 256, D),
       ("w_ffn_in", D, FFN_SHARD, 256, FFN_SHARD), ("w_ffn_out", 704, D, 352, D))
CONV = ("lru_conv_w", "ssd_conv_w")
SMALL = ("norm1_g", "b_gate", "lru_conv_b", "lru_w_a", "lru_b_a", "lru_w_x", "lru_b_x", "lru_lambda", "ssd_conv_b",
         "ssd_dt_bias", "ssd_A_log", "ssd_D", "ssd_norm_g", "norm2_g", "norm_f")
_WIRE = jnp.bfloat16
N_CHIPS = 4
N_DEV = 8


def _mesh_pos():
    return lax.axis_index("x"), lax.axis_index("y"), lax.axis_index("c")


HBM_SPEC = pl.BlockSpec(memory_space=pltpu.HBM)


def _remote(src, dst, send_sems, recv_sems, k, to):
    return pltpu.make_async_remote_copy(src_ref=src, dst_ref=dst, send_sem=send_sems.at[k], recv_sem=recv_sems.at[k],
                                        device_id=to, device_id_type=MESH)


def _other_chips(x, y):
    return [(1 - x, y), (x, 1 - y), (1 - x, 1 - y)]


def _weight_fetch(loc, layer, owner):
    names = list(owner)
    rows = {n: loc[n].shape[1] for n in names}
    by_chip = ("w_in", "w_ffn_in")
    shapes = [((N_CHIPS,) + loc[n].shape[1:]) if n in by_chip else (N_CHIPS * rows[n], D) for n in names]

    def place(o_ref, n, chip):
        if n in by_chip:
            return o_ref.at[chip]
        return o_ref.at[pl.ds(pl.multiple_of(chip * rows[n], 16), rows[n]), :]

    def step(which, in_refs, o_refs, send_sems, recv_sems):
        x, y, c = _mesh_pos()
        s = 2 * x + y
        sib = (x, y, 1 - c)
        chips = _other_chips(x, y)
        for core in (0, 1):
            @pl.when(c == core)
            def _():
                for k, n in enumerate(names):
                    for j, (px, py) in enumerate(chips):
                        landed = place(o_refs[k], n, 2 * px + py)
                        sent = _remote(in_refs[k].at[layer], place(o_refs[k], n, s), send_sems, recv_sems, 3 * k + j,
                                       (px, py, c))
                        arrives = _remote(in_refs[k].at[layer], landed, send_sems, recv_sems, 3 * k + j, (px, py, c))
                        passed = _remote(landed, landed, send_sems, recv_sems, 3 * (len(names) + k) + j, sib)
                        if owner[n] == core:
                            if which == "start":
                                sent.start()
                            elif which == "mid":
                                arrives.wait_recv()
                                passed.start()
                            else:
                                sent.wait_send()
                                passed.wait_send()
                        elif which == "end":
                            passed.wait_recv()

    return dict(inputs=[loc[n] for n in names], names=names,
                out_shapes=[jax.ShapeDtypeStruct(shp, loc[n].dtype) for shp, n in zip(shapes, names)],
                sems=[pltpu.SemaphoreType.DMA((6 * len(names),)), pltpu.SemaphoreType.DMA((6 * len(names),))],
                start=functools.partial(step, "start"), mid=functools.partial(step, "mid"),
                end=functools.partial(step, "end"))


def _comm_now(comm, name):
    n, no = len(comm["inputs"]), len(comm["out_shapes"])

    def body(*refs):
        parts = (refs[:n], refs[n:n + no]) + tuple(refs[n + no:])
        comm["start"](*parts)
        comm["mid"](*parts)
        comm["end"](*parts)

    return pl.pallas_call(
        body, name=name, in_specs=[HBM_SPEC] * n, out_specs=[HBM_SPEC] * no, out_shape=comm["out_shapes"],
        scratch_shapes=comm["sems"],
    )(*comm["inputs"])


def _sibling_send(bufs, layer):
    n = len(bufs)

    def step(which, in_refs, o_refs, send_sems, recv_sems):
        x, y, c = _mesh_pos()
        copies = [_remote(in_refs[k], o_refs[k], send_sems, recv_sems, k, (x, y, 1 - c)) for k in range(n)]

        @pl.when(c != layer)
        def _():
            for cp in copies:
                if which == "start":
                    cp.start()
                elif which == "end":
                    cp.wait_send()

        @pl.when(c == layer)
        def _():
            for cp in copies:
                if which == "end":
                    cp.wait_recv()

    return dict(inputs=list(bufs), out_shapes=[jax.ShapeDtypeStruct(b.shape, b.dtype) for b in bufs],
                sems=[pltpu.SemaphoreType.DMA((n,)), pltpu.SemaphoreType.DMA((n,))],
                start=functools.partial(step, "start"), mid=functools.partial(step, "mid"),
                end=functools.partial(step, "end"))


def _add_cast(g, recv, own, *, a, tr, tc, name):
    wd = g.shape[1]
    nr = a // tr

    def body(own_ref, g_ref, r_ref, o_ref):
        @pl.when(own_ref[0] == 1)
        def _():
            o_ref[...] = (g_ref[...] + r_ref[...]).astype(o_ref.dtype)

    blk = pl.BlockSpec((tr, tc), lambda p, i, j, own_ref: ((p * nr + i) * own_ref[0], j * own_ref[0]))
    return pl.pallas_call(
        body, name=name,
        grid_spec=pltpu.PrefetchScalarGridSpec(
            num_scalar_prefetch=1, grid=(N_CHIPS, nr, wd // tc), in_specs=[blk, blk],
            out_specs=pl.BlockSpec((None, tr, tc), lambda p, i, j, own_ref: (p * own_ref[0], i * own_ref[0], j * own_ref[0]))),
        out_shape=jax.ShapeDtypeStruct((N_CHIPS, a, wd), _WIRE),
        compiler_params=_cp(("arbitrary", "arbitrary", "arbitrary")),
    )(own, g, recv)


def _chip_exchange(parts, layer):
    n = len(parts)

    def step(which, s_refs, o_refs, send_sems, recv_sems):
        x, y, c = _mesh_pos()
        s = 2 * x + y

        @pl.when(c == layer)
        def _():
            for j, (px, py) in enumerate(_other_chips(x, y)):
                for k in range(n):
                    p = 2 * px + py
                    sent = _remote(s_refs[k].at[p], o_refs[k].at[s], send_sems, recv_sems, n * j + k, (px, py, c))
                    if which == "start":
                        sent.start()
                    elif which == "end":
                        _remote(s_refs[k].at[p], o_refs[k].at[p], send_sems, recv_sems, n * j + k, (px, py, c)).wait_recv()
                        sent.wait_send()

    return dict(inputs=list(parts), out_shapes=[jax.ShapeDtypeStruct(p.shape, p.dtype) for p in parts],
                sems=[pltpu.SemaphoreType.DMA((3 * n,)), pltpu.SemaphoreType.DMA((3 * n,))],
                start=functools.partial(step, "start"), mid=functools.partial(step, "mid"),
                end=functools.partial(step, "end"))


def _sum_slots(slots, own, sel, *, tr, tc, name, layer=None, into=None):
    n, rows, wd = slots.shape
    k = own.shape[0]

    def body(sel_ref, s_ref, own_ref, *rest):
        o_ref = rest[-1]

        @pl.when(sel_ref[1] == 1)
        def _():
            mine = sel_ref[0]
            acc = jnp.zeros((tr, tc), F32)
            for p in range(n):
                acc = acc + jnp.where(mine == p, own_ref[...].astype(F32), s_ref[p].astype(F32))
            o_ref[...] = acc

    if layer is not None:
        out_spec = pl.BlockSpec((None, tr, tc), lambda i, j, sel_ref: (layer, i * sel_ref[1], j * sel_ref[1]))
        out_shape = jax.ShapeDtypeStruct((N_LAYERS, rows, wd), F32)
    else:
        out_spec = pl.BlockSpec((tr, tc), lambda i, j, sel_ref: (i * sel_ref[1], j * sel_ref[1]))
        out_shape = jax.ShapeDtypeStruct((rows, wd), F32)
    in_specs = [pl.BlockSpec((n, tr, tc), lambda i, j, sel_ref: (0, i * sel_ref[1], j * sel_ref[1])),
                pl.BlockSpec((None, tr, tc), lambda i, j, sel_ref: (sel_ref[0] if k > 1 else 0, i * sel_ref[1],
                                                                    j * sel_ref[1]))]
    args = [sel, slots, own]
    if into is not None:
        in_specs.append(pl.BlockSpec(memory_space=pl.ANY))
        args.append(into)
    return pl.pallas_call(
        body, name=name,
        grid_spec=pltpu.PrefetchScalarGridSpec(num_scalar_prefetch=1, grid=(rows // tr, wd // tc), in_specs=in_specs,
                                               out_specs=out_spec),
        out_shape=out_shape, input_output_aliases={3: 0} if into is not None else {},
        compiler_params=_cp(("arbitrary", "arbitrary")),
    )(*args)


def _sibling_share(both):
    n = len(both)

    def body(*refs):
        o_refs, (send_sems, recv_sems) = refs[n:2 * n], refs[2 * n:]
        x, y, c = _mesh_pos()
        sends = [_remote(o_refs[k].at[c], o_refs[k].at[c], send_sems, recv_sems, k, (x, y, 1 - c)) for k in range(n)]
        for cp in sends:
            cp.start()
        for k in range(n):
            _remote(o_refs[k].at[1 - c], o_refs[k].at[1 - c], send_sems, recv_sems, k, (x, y, 1 - c)).wait_recv()
        for cp in sends:
            cp.wait_send()

    return pl.pallas_call(
        body, name="grad_sibling_share", in_specs=[HBM_SPEC] * n, out_specs=[HBM_SPEC] * n,
        out_shape=[jax.ShapeDtypeStruct(b.shape, b.dtype) for b in both], input_output_aliases={k: k for k in range(n)},
        scratch_shapes=[pltpu.SemaphoreType.DMA((n,)), pltpu.SemaphoreType.DMA((n,))],
    )(*both)


def _allgather_devices(part):
    rows, wd = part.shape

    def step(which, in_refs, o_refs, send_sems, recv_sems):
        (p_ref,), (o_ref,) = in_refs, o_refs
        x, y, c = _mesh_pos()
        sib = (x, y, 1 - c)
        chips = _other_chips(x, y)
        slot = lambda px, py, pc: o_ref.at[4 * px + 2 * py + pc]
        first = [_remote(p_ref, slot(x, y, c), send_sems, recv_sems, 0, sib)]
        first += [_remote(p_ref, slot(x, y, c), send_sems, recv_sems, 1 + j, (px, py, c)) for j, (px, py) in enumerate(chips)]
        passed = [_remote(slot(px, py, c), slot(px, py, c), send_sems, recv_sems, 4 + j, sib)
                  for j, (px, py) in enumerate(chips)]
        if which == "start":
            for cp in first:
                cp.start()
        elif which == "mid":
            for j, (px, py) in enumerate(chips):
                _remote(p_ref, slot(px, py, c), send_sems, recv_sems, 1 + j, (px, py, c)).wait_recv()
                passed[j].start()
        else:
            _remote(p_ref, slot(x, y, 1 - c), send_sems, recv_sems, 0, sib).wait_recv()
            for j, (px, py) in enumerate(chips):
                _remote(slot(px, py, 1 - c), slot(px, py, 1 - c), send_sems, recv_sems, 4 + j, sib).wait_recv()
            for cp in first + passed:
                cp.wait_send()

    return dict(inputs=[part], out_shapes=[jax.ShapeDtypeStruct((N_DEV, rows, wd), part.dtype)],
                sems=[pltpu.SemaphoreType.DMA((N_DEV - 1,)), pltpu.SemaphoreType.DMA((N_DEV - 1,))],
                start=functools.partial(step, "start"), mid=functools.partial(step, "mid"),
                end=functools.partial(step, "end"))


def _comm_both(a, b):
    na, nao = len(a["inputs"]), len(a["out_shapes"])

    def step(which, in_refs, o_refs, sa, ra, sb, rb_):
        a[which](in_refs[:na], o_refs[:nao], sa, ra)
        b[which](in_refs[na:], o_refs[nao:], sb, rb_)

    return dict(inputs=a["inputs"] + b["inputs"], out_shapes=a["out_shapes"] + b["out_shapes"], sems=a["sems"] + b["sems"],
                start=functools.partial(step, "start"), mid=functools.partial(step, "mid"),
                end=functools.partial(step, "end"))


def _by_chip_to_full(stack):
    _, nl, r, b = stack.shape
    return stack.transpose(1, 2, 0, 3).reshape(nl, r, N_CHIPS * b)


def _sharded_step(a):
    x = a["x"][0]
    target = a["loss_target"][0]
    cx, cy, cc = _mesh_pos()
    chip = (2 * cx + cy).astype(jnp.int32)
    core = cc.astype(jnp.int32)
    me = (4 * cx + 2 * cy + cc).astype(jnp.int32)
    zero = jnp.zeros((), jnp.int32)
    dus = lax.dynamic_update_slice

    loc = {n: a[n].astype(_MXU) for n, *_ in BIG}

    def with_own(got, names, layer):
        out = {}
        for g, n in zip(got, names):
            mine = loc[n][layer]
            out[n] = (dus(g, mine[None], (chip, zero, zero)) if g.ndim == 3 else dus(g, mine, (chip * mine.shape[0], zero)))
        return out

    rest = {"w_ffn_in": 0, "w_branch": 1, "w_out": 1, "w_ffn_out": 1}
    conv_loc = jnp.concatenate([a[n].reshape(-1, PACK_W) for n in CONV], axis=0)
    now = _weight_fetch(loc, 0, {"w_in": 0})
    conv_all, *got_now = _comm_now(_comm_both(_allgather_devices(conv_loc), now), "allgather_weights")
    w0 = with_own(got_now, now["names"], 0)
    later = {"in_proj": (0, _weight_fetch(loc, 0, rest)), "lru": (1, _weight_fetch(loc, 1, {"w_in": 0})),
             "ssd": (1, _weight_fetch(loc, 1, rest))}
    prefetch = {k: (f, functools.partial(lambda got, layer, f: (layer, with_own(got, f["names"], layer)), layer=layer, f=f))
                for k, (layer, f) in later.items()}
    conv_all = dus(conv_all, conv_loc[None], (me, zero, zero))[0::2]
    conv, off = {}, 0
    for n in CONV:
        rows = a[n].size // PACK_W
        conv[n] = _by_chip_to_full(conv_all[:, off:off + rows].reshape((N_CHIPS,) + a[n].shape))
        off += rows
    small = {n: a[n] for n in SMALL}

    views = lambda big_l, specs: [big_l[n].reshape(-1, wd) for n, _, wd, _, _ in specs]
    owns = lambda layer: (core == layer).astype(jnp.int32)
    w_in_only, others = BIG[:1], BIG[1:]

    def partial_sums(big_l, recv, layer, specs):
        return [_add_cast(v, r, owns(layer).reshape(1), a=rows, tr=tr, tc=tc, name=f"grad_add_sibling_l{layer}_{n}")
                for v, r, (n, rows, _, tr, tc) in zip(views(big_l, specs), recv, specs)]

    def reduced(slots, parts, layer, into, specs):
        sel = jnp.stack([chip, owns(layer)])
        return [_sum_slots(s, p, sel, tr=tr, tc=tc, name=f"grad_sum_chips_l{layer}_{n}", layer=layer, into=buf)
                for s, p, buf, (n, _, _, tr, tc) in zip(slots, parts, into, specs)]

    kept = {}

    def early_reduce(big_1):
        def during_lru(big_0):
            kept["big_0"] = dict(big_0)
            return _comm_both(_sibling_send(views(big_1, BIG), 1), _sibling_send(views(big_0, others), 0))

        def during_ssd(recv):
            kept["parts_1"] = partial_sums(big_1, recv[:len(BIG)], 1, BIG)
            kept["parts_0"] = partial_sums(kept["big_0"], recv[len(BIG):], 0, others)
            return _comm_both(_chip_exchange(kept["parts_1"], 1), _chip_exchange(kept["parts_0"], 0))

        return dict(lru=during_lru, ssd=during_ssd, in_dgrad=lambda big_0: _sibling_send(views(big_0, w_in_only), 0))

    loss_blk, grad_x, grads, big, (slots, recv_in) = _local_step(x, target, [w0, {}], conv, small, prefetch, early_reduce)
    loss = lax.psum(loss_blk[0, 0], ("x", "y", "c"))
    both = reduced(slots[:len(BIG)], kept["parts_1"], 1, [None] * len(BIG), BIG)
    both[1:] = reduced(slots[len(BIG):], kept["parts_0"], 0, both[1:], others)
    parts_in = partial_sums(big[0], recv_in, 0, w_in_only)
    names = SMALL + CONV
    srows = -(-sum(grads[n].size for n in names) // (8 * PACK_W)) * 8
    flat = lambda d, ns: jnp.concatenate([d[n].reshape(-1) for n in ns])
    padto = lambda v: jnp.pad(v, (0, srows * PACK_W - v.shape[0])).reshape(srows, PACK_W)
    g_own = padto(flat(grads, names))
    g_all, *slots_in = _comm_now(_comm_both(_allgather_devices(g_own), _chip_exchange(parts_in, 0)), "grad_chip_exchange")
    both[:1] = reduced(slots_in, parts_in, 0, both[:1], w_in_only)
    done = dict(zip([n for n, *_ in BIG], _sibling_share(both)))
    g_big = {n: done[n].reshape(a[n].shape) for n in ("w_branch", "w_out", "w_ffn_in", "w_ffn_out")}
    gt = done["w_in"].transpose(0, 2, 1)
    first = jnp.concatenate([gt[..., 512 * j + 256 * part:512 * j + 256 * (part + 1)] for part in range(2) for j in range(4)]
                            + [gt[..., 2 * D:]], axis=-1)
    tail = W_IN_SHARD - (IN_DIM - 7168)
    last = jnp.concatenate([gt[..., :tail], gt[..., W_IN_SHARD - N_HEADS:], gt[..., tail:W_IN_SHARD - N_HEADS]], axis=-1)
    g_big["w_in"] = jnp.where(chip == 0, first, jnp.where(chip == N_CHIPS - 1, last, gt))

    g_sum = _sum_slots(g_all, g_own[None], jnp.stack([me, zero + 1]), tr=srows, tc=PACK_W, name="small_grad_sum")
    off, g_small = 0, {}
    for n in names:
        g_small[n] = g_sum.reshape(-1)[off:off + grads[n].size].reshape(grads[n].shape)
        off += grads[n].size
    for n in CONV:
        width = a[n].shape[2]
        g_big[n] = lax.dynamic_slice(g_small.pop(n), (zero, zero, chip * width), a[n].shape)

    out_g, out_d, out_m, out_v = {}, {}, {}, {}
    for n in g_big:
        shp = a[n].shape
        two_d = (shp[0] * shp[1], shp[2])
        d_, m_, v_ = _adamw(a[n].reshape(two_d), g_big[n].reshape(two_d), a["m_" + n].reshape(two_d),
                            a["v_" + n].reshape(two_d), name="adamw_" + n)
        out_g[n], out_d[n], out_m[n], out_v[n] = g_big[n], d_.reshape(shp), m_.reshape(shp), v_.reshape(shp)
    d_, m_, v_ = _adamw(padto(flat(a, SMALL)), padto(flat(g_small, SMALL)), padto(flat({n: a["m_" + n] for n in SMALL}, SMALL)),
                        padto(flat({n: a["v_" + n] for n in SMALL}, SMALL)), name="adamw_small")
    off = 0
    for n in SMALL:
        cut = lambda v: v.reshape(-1)[off:off + a[n].size].reshape(a[n].shape)
        out_g[n], out_d[n], out_m[n], out_v[n] = g_small[n], cut(d_), cut(m_), cut(v_)
        off += a[n].size
    return loss, grad_x[None], out_g, out_d, out_m, out_v


WEIGHTS = ("norm1_g", "w_in", "b_gate", "lru_conv_w", "lru_conv_b", "lru_w_a", "lru_b_a", "lru_w_x", "lru_b_x", "lru_lambda",
           "ssd_conv_w", "ssd_conv_b", "ssd_dt_bias", "ssd_A_log", "ssd_D", "ssd_norm_g", "w_branch", "w_out", "norm2_g",
           "w_ffn_in", "w_ffn_out", "norm_f")
INPUTS = ("x",) + WEIGHTS + ("loss_target",) + tuple("m_" + n for n in WEIGHTS) + tuple("v_" + n for n in WEIGHTS)


def kernel(x, norm1_g, w_in, b_gate, lru_conv_w, lru_conv_b, lru_w_a, lru_b_a, lru_w_x, lru_b_x, lru_lambda, ssd_conv_w, ssd_conv_b, ssd_dt_bias, ssd_A_log, ssd_D, ssd_norm_g, w_branch, w_out, norm2_g, w_ffn_in, w_ffn_out, norm_f, loss_target, m_norm1_g, m_w_in, m_b_gate, m_lru_conv_w, m_lru_conv_b, m_lru_w_a, m_lru_b_a, m_lru_w_x, m_lru_b_x, m_lru_lambda, m_ssd_conv_w, m_ssd_conv_b, m_ssd_dt_bias, m_ssd_A_log, m_ssd_D, m_ssd_norm_g, m_w_branch, m_w_out, m_norm2_g, m_w_ffn_in, m_w_ffn_out, m_norm_f, v_norm1_g, v_w_in, v_b_gate, v_lru_conv_w, v_lru_conv_b, v_lru_w_a, v_lru_b_a, v_lru_w_x, v_lru_b_x, v_lru_lambda, v_ssd_conv_w, v_ssd_conv_b, v_ssd_dt_bias, v_ssd_A_log, v_ssd_D, v_ssd_norm_g, v_w_branch, v_w_out, v_norm2_g, v_w_ffn_in, v_w_ffn_out, v_norm_f):
    args = (x, norm1_g, w_in, b_gate, lru_conv_w, lru_conv_b, lru_w_a, lru_b_a, lru_w_x, lru_b_x, lru_lambda, ssd_conv_w, ssd_conv_b, ssd_dt_bias, ssd_A_log, ssd_D, ssd_norm_g, w_branch, w_out, norm2_g, w_ffn_in, w_ffn_out, norm_f, loss_target, m_norm1_g, m_w_in, m_b_gate, m_lru_conv_w, m_lru_conv_b, m_lru_w_a, m_lru_b_a, m_lru_w_x, m_lru_b_x, m_lru_lambda, m_ssd_conv_w, m_ssd_conv_b, m_ssd_dt_bias, m_ssd_A_log, m_ssd_D, m_ssd_norm_g, m_w_branch, m_w_out, m_norm2_g, m_w_ffn_in, m_w_ffn_out, m_norm_f, v_norm1_g, v_w_in, v_b_gate, v_lru_conv_w, v_lru_conv_b, v_lru_w_a, v_lru_b_a, v_lru_w_x, v_lru_b_x, v_lru_lambda, v_ssd_conv_w, v_ssd_conv_b, v_ssd_dt_bias, v_ssd_A_log, v_ssd_D, v_ssd_norm_g, v_w_branch, v_w_out, v_norm2_g, v_w_ffn_in, v_w_ffn_out, v_norm_f)
    assert len(args) == len(INPUTS)
    loss, grad_x, g, d, m, v = _sharded_step(dict(zip(INPUTS, args)))
    return (loss, grad_x, *[g[n] for n in WEIGHTS], *[d[n] for n in WEIGHTS], *[m[n] for n in WEIGHTS],
            *[v[n] for n in WEIGHTS])
```

```python
import functools
import math

import numpy as np
import jax
import jax.numpy as jnp
from jax import lax
from jax.experimental import pallas as pl
from jax.experimental.pallas import tpu as pltpu

F32 = jnp.float32
BF16 = jnp.bfloat16
_MXU = jnp.bfloat16
_HI = lax.Precision.HIGHEST

D = 1024
EPS = 1e-6
N_LAYERS = 2
LRU_C = 8.0
N_HEADS = 32
HEAD_P = 64
N_GROUPS = 4
N_STATE = 128
SSD_INNER = 2048
XBC = 3072
D_FF = 2816
CHUNK = 64
NORM_ROWS = 32
IN_DIM = 9248

NP = 9216
ZX_W = 5120
G0 = 6144
LBLK = 512
DT_PAD = 128

VMEM_LIMIT_BYTES_V7X = 56 * 1024 * 1024

ADAM_LR, ADAM_B1, ADAM_B2, ADAM_EPS, ADAM_WD, ADAM_STEP = 0.001, 0.9, 0.999, 1e-08, 0.01, 10
MESH = pl.DeviceIdType.MESH


def _cp(sem):
    return pltpu.CompilerParams(dimension_semantics=sem, vmem_limit_bytes=VMEM_LIMIT_BYTES_V7X)


def _lblk_col(j):
    return 10 + j + 4 * (j // 2)


def _sigmoid(x):
    return 0.5 * jnp.tanh(0.5 * x) + 0.5


def _softplus(x):
    return jnp.maximum(x, 0.0) + jnp.log(1.0 + jnp.exp(-jnp.abs(x)))


def _silu(x):
    return x * _sigmoid(x)


def _dsilu(x):
    s = _sigmoid(x)
    return s * (1.0 + x * (1.0 - s))


_GELU_C0 = math.sqrt(2.0 / math.pi)
_GELU_C1 = 0.044715


def _gelu_and_grad(x):
    t = jnp.tanh(_GELU_C0 * (x + _GELU_C1 * x * x * x))
    g = 0.5 * x * (1.0 + t)
    dg = 0.5 * (1.0 + t) + 0.5 * x * (1.0 - t * t) * _GELU_C0 * (1.0 + 3.0 * _GELU_C1 * x * x)
    return g, dg


def _one_minus_exp(x):
    p = 1.0 + x * (1.0 / 7.0)
    p = 1.0 + x * (1.0 / 6.0) * p
    p = 1.0 + x * (1.0 / 5.0) * p
    p = 1.0 + x * (1.0 / 4.0) * p
    p = 1.0 + x * (1.0 / 3.0) * p
    p = 1.0 + x * (1.0 / 2.0) * p
    return jnp.where(x > -0.3, -x * p, 1.0 - jnp.exp(x))


def _dot(a, b):
    return jnp.dot(a.astype(_MXU), b.astype(_MXU), preferred_element_type=F32)


def _dot_nt(a, b):
    return lax.dot_general(a.astype(_MXU), b.astype(_MXU), (((1,), (1,)), ((), ())), preferred_element_type=F32)


def _dot_tn(a, b):
    return lax.dot_general(a.astype(_MXU), b.astype(_MXU), (((0,), (0,)), ((), ())), preferred_element_type=F32)


def _shift_down(x, prev8, k):
    xr = pltpu.roll(x, k, 0)
    pr = pltpu.roll(prev8, k, 0)
    row = lax.broadcasted_iota(jnp.int32, prev8.shape, 0)
    head = jnp.where(row < k, pr, xr[0:8])
    return jnp.concatenate([head, xr[8:]], axis=0)


def _shift_up(x, next8, k):
    r = x.shape[0]
    xr = pltpu.roll(x, r - k, 0)
    nr = pltpu.roll(next8, 8 - k, 0)
    row = lax.broadcasted_iota(jnp.int32, next8.shape, 0)
    tail = jnp.where(row >= 8 - k, nr, xr[r - 8:r])
    return jnp.concatenate([xr[:r - 8], tail], axis=0)


def _conv4(x, prev8, w_ref, b_ref, cols=slice(None)):
    acc = x * w_ref[3:4, cols] + b_ref[0:1, cols]
    for k in (1, 2, 3):
        acc = acc + _shift_down(x, prev8, k) * w_ref[3 - k:4 - k, cols]
    return acc


def _conv4_bwd_x(dy, next8, w_ref, cols=slice(None)):
    acc = dy * w_ref[3:4, cols]
    for k in (1, 2, 3):
        acc = acc + _shift_up(dy, next8, k) * w_ref[3 - k:4 - k, cols]
    return acc


def _lin_scan(a, b, reverse):
    r = a.shape[0]
    row = lax.broadcasted_iota(jnp.int32, a.shape, 0)
    d = 1
    while d < r:
        sh = (r - d) if reverse else d
        a_s = pltpu.roll(a, sh, 0)
        b_s = pltpu.roll(b, sh, 0)
        m = (row < r - d) if reverse else (row >= d)
        b = jnp.where(m, a * b_s + b, b)
        a = jnp.where(m, a * a_s, a)
        d *= 2
    return a, b


def _rsum(x):
    return jnp.sum(x, axis=0, keepdims=True)


def _comm_specs(comm):
    if comm is None:
        return [], [], [], [], []
    n = len(comm["inputs"])
    return list(comm["inputs"]), [HBM_SPEC] * n, [HBM_SPEC] * len(comm["out_shapes"]), list(comm["out_shapes"]), comm["sems"]


def _comm_steps(comm, refs, n_in, n_out, first, mid, last):
    ni, no = len(comm["inputs"]), len(comm["out_shapes"])
    parts = (refs[n_in:n_in + ni], refs[n_out:n_out + no]) + tuple(refs[len(refs) - len(comm["sems"]):])
    for when, what in ((first, "start"), (mid, "mid"), (last, "end")):
        @pl.when(when)
        def _():
            comm[what](*parts)


def _norm_mm(h, gamma, w, *, tm, tn, name, out_dtype=F32, comm=None):
    m, k = h.shape
    if w.ndim == 3:
        assert w.shape[2] == tn
        n = w.shape[0] * tn
        w_spec = pl.BlockSpec((None, k, tn), lambda i, j: (j, 0, 0))
    else:
        n = w.shape[1]
        w_spec = pl.BlockSpec((k, tn), lambda i, j: (0, j))

    c_args, c_in_specs, c_out_specs, c_out_shapes, c_sems = _comm_specs(comm)
    ni, nj = m // tm, n // tn

    def body(*refs):
        h_ref, g_ref, w_ref = refs[:3]
        xn_ref, o_ref = refs[3 + len(c_args):5 + len(c_args)]
        i, j = pl.program_id(0), pl.program_id(1)
        if comm is not None:
            _comm_steps(comm, refs, 3, 5 + len(c_args), (i == 0) & (j == 0), (i == (3 * ni) // 4) & (j == 0),
                        (i == ni - 1) & (j == nj - 1))

        @pl.when(j == 0)
        def _():
            x = h_ref[...]
            r = lax.rsqrt(jnp.mean(x * x, axis=-1, keepdims=True) + EPS)
            xn_ref[...] = ((x * r) * g_ref[...]).astype(xn_ref.dtype)
        o_ref[...] = jnp.dot(xn_ref[...], w_ref[...], preferred_element_type=F32).astype(o_ref.dtype)

    return pl.pallas_call(
        body, name=name, grid=(ni, nj),
        in_specs=[pl.BlockSpec((tm, k), lambda i, j: (i, 0)), pl.BlockSpec((1, k), lambda i, j: (0, 0)), w_spec] + c_in_specs,
        out_specs=[pl.BlockSpec((tm, k), lambda i, j: (i, 0)), pl.BlockSpec((tm, tn), lambda i, j: (i, j))] + c_out_specs,
        out_shape=[jax.ShapeDtypeStruct((m, k), _MXU), jax.ShapeDtypeStruct((m, n), out_dtype)] + c_out_shapes,
        scratch_shapes=c_sems,
        compiler_params=_cp(("arbitrary", "arbitrary") if comm is not None else ("parallel", "arbitrary")),
    )(h, gamma, w, *c_args)


def _mm_nn(a, w, *, tm, tn, name, residual=None):
    m, k = a.shape
    n = w.shape[1]

    def body(*refs):
        if residual is None:
            a_ref, w_ref, o_ref = refs
            o_ref[...] = _dot(a_ref[...], w_ref[...])
        else:
            a_ref, w_ref, r_ref, o_ref = refs
            o_ref[...] = _dot(a_ref[...], w_ref[...]) + r_ref[...]

    in_specs = [pl.BlockSpec((tm, k), lambda i, j: (i, 0)), pl.BlockSpec((k, tn), lambda i, j: (0, j))]
    args = [a, w]
    if residual is not None:
        in_specs.append(pl.BlockSpec((tm, tn), lambda i, j: (i, j)))
        args.append(residual)
    return pl.pallas_call(
        body, name=name, grid=(m // tm, n // tn), in_specs=in_specs,
        out_specs=pl.BlockSpec((tm, tn), lambda i, j: (i, j)),
        out_shape=jax.ShapeDtypeStruct((m, n), F32),
        compiler_params=_cp(("parallel", "parallel")),
    )(*args)


def _wgrad(a, b, *, tt, ta, tn, name, out_shape, out_block, out_index, a_tab=None, o_tab=None, into=None):
    t = a.shape[0]
    a_tab = list(range(a.shape[1] // ta)) if a_tab is None else a_tab
    o_tab = a_tab if o_tab is None else o_tab
    nb = b.shape[1] // tn

    def body(at_ref, ot_ref, a_ref, b_ref, *rest):
        del at_ref, ot_ref
        o_ref = rest[-1]

        @pl.when(pl.program_id(2) == 0)
        def _():
            o_ref[...] = jnp.zeros_like(o_ref)
        o_ref[...] += _dot_tn(a_ref[...], b_ref[...])

    in_specs = [pl.BlockSpec((tt, ta), lambda r, j, i, at, ot: (i, at[r])),
                pl.BlockSpec((tt, tn), lambda r, j, i, at, ot: (i, j))]
    args = [jnp.asarray(a_tab, jnp.int32), jnp.asarray(o_tab, jnp.int32), a, b]
    aliases = {}
    if into is not None:
        in_specs.append(pl.BlockSpec(memory_space=pl.ANY))
        args.append(into)
        aliases = {4: 0}
    return pl.pallas_call(
        body, name=name,
        grid_spec=pltpu.PrefetchScalarGridSpec(
            num_scalar_prefetch=2, grid=(len(a_tab), nb, t // tt), in_specs=in_specs,
            out_specs=pl.BlockSpec(out_block, lambda r, j, i, at, ot: out_index(ot[r], j))),
        out_shape=jax.ShapeDtypeStruct(out_shape, F32), input_output_aliases=aliases,
        compiler_params=_cp(("parallel", "parallel", "arbitrary")),
    )(*args)


def _mm_nt(a, w, *, tm, name):
    m, kc = a.shape
    n = w.shape[0]

    def body(a_ref, w_ref, o_ref):
        o_ref[...] = _dot_nt(a_ref[...], w_ref[...])

    return pl.pallas_call(
        body, name=name, grid=(m // tm,),
        in_specs=[pl.BlockSpec((tm, kc), lambda i: (i, 0)), pl.BlockSpec((n, kc), lambda i: (0, 0))],
        out_specs=pl.BlockSpec((tm, n), lambda i: (i, 0)),
        out_shape=jax.ShapeDtypeStruct((m, n), F32),
        compiler_params=_cp(("parallel",)),
    )(a, w)


def _mm_nt_rmsbwd(dy, w, x, gamma, dres, *, tm, tk, name, extra=None, comm=None):
    m, kc = dy.shape
    nk = kc // tk
    ni = m // tm
    n_x = 5 if extra is None else 7
    c_args, c_in_specs, c_out_specs, c_out_shapes, c_sems = _comm_specs(comm)
    if w.ndim == 3:
        assert w.shape[0] == nk and w.shape[2] == tk
        d = w.shape[1]
        w_spec = pl.BlockSpec((None, d, tk), lambda i, k: (k, 0, 0))
    else:
        d = w.shape[0]
        w_spec = pl.BlockSpec((d, tk), lambda i, k: (0, k))

    def body(*refs):
        dy_ref, w_ref, x_ref, g_ref, r_ref = refs[:5]
        if extra is not None:
            dy2_ref, w2_ref = refs[5:7]
        n_out = n_x + len(c_args)
        dx_ref, dg_ref = refs[n_out:n_out + 2]
        acc_ref = refs[n_out + 2 + len(c_out_shapes)]
        i, kk = pl.program_id(0), pl.program_id(1)
        if comm is not None:
            _comm_steps(comm, refs, n_x, n_out + 2, (i == 0) & (kk == 0), (i == (3 * ni) // 4) & (kk == 0),
                        (i == ni - 1) & (kk == nk - 1))

        @pl.when(kk == 0)
        def _():
            acc_ref[...] = jnp.zeros_like(acc_ref)

        @pl.when((i == 0) & (kk == 0))
        def _():
            dg_ref[...] = jnp.zeros_like(dg_ref)

        acc_ref[...] += _dot_nt(dy_ref[...], w_ref[...])

        @pl.when(kk == nk - 1)
        def _():
            dxn = acc_ref[...]
            if extra is not None:
                dxn = dxn + _dot_nt(dy2_ref[...], w2_ref[...])
            xv = x_ref[...]
            r = lax.rsqrt(jnp.mean(xv * xv, axis=-1, keepdims=True) + EPS)
            xh = xv * r
            dg_ref[0:1, :] += _rsum(dxn * xh)
            dxh = dxn * g_ref[...]
            dx_ref[...] = r_ref[...] + r * (dxh - xh * jnp.mean(dxh * xh, axis=-1, keepdims=True))

    in_specs = [pl.BlockSpec((tm, tk), lambda i, k: (i, k)), w_spec,
                pl.BlockSpec((tm, d), lambda i, k: (i, 0)), pl.BlockSpec((1, d), lambda i, k: (0, 0)),
                pl.BlockSpec((tm, d), lambda i, k: (i, 0))]
    args = [dy, w, x, gamma, dres]
    if extra is not None:
        k2 = extra[0].shape[1]
        in_specs += [pl.BlockSpec((tm, k2), lambda i, k: (i, 0)), pl.BlockSpec((d, k2), lambda i, k: (0, 0))]
        args += list(extra)
    return pl.pallas_call(
        body, name=name, grid=(ni, nk), in_specs=in_specs + c_in_specs,
        out_specs=[pl.BlockSpec((tm, d), lambda i, k: (i, 0)), pl.BlockSpec((8, d), lambda i, k: (0, 0))] + c_out_specs,
        out_shape=[jax.ShapeDtypeStruct((m, d), F32), jax.ShapeDtypeStruct((8, d), F32)] + c_out_shapes,
        scratch_shapes=[pltpu.VMEM((tm, d), F32)] + c_sems,
        compiler_params=_cp(("arbitrary", "arbitrary")),
    )(*args, *c_args)


def _rsum8(x):
    acc = x[0:8]
    for g in range(1, x.shape[0] // 8):
        acc = acc + x[8 * g:8 * (g + 1)]
    return acc


def _lru_gates(x, prev8, cw_ref, cb_ref, wa_ref, wx_ref, ba_ref, bx_ref, lam_ref):
    u = _conv4(x, prev8, cw_ref, cb_ref)
    ra =_sigmoid(_dot(u, wa_ref[0]) + ba_ref[...])
    ia = _sigmoid(_dot(u, wx_ref[0]) + bx_ref[...])
    sp = _softplus(-lam_ref[...])
    log_a = -LRU_C * ra * sp
    a = jnp.exp(log_a)
    m2 = _one_minus_exp(2.0 * log_a)
    mult = jnp.sqrt(m2)
    return u, ra, ia, sp, a, m2, mult


def _lru_fwd(proj, lw, *, r, name, comm=None):
    t = proj.shape[0]
    nt = t // r
    c_args, c_in_specs, c_out_specs, c_out_shapes, c_sems = _comm_specs(comm)

    def body(*refs):
        xg_ref, xp_ref, cw_ref, cb_ref, wa_ref, wx_ref, ba_ref, bx_ref, lam_ref = refs[:9]
        hl_ref, ya_ref, sv_ref = refs[9 + len(c_args):12 + len(c_args)]
        carry_ref = refs[12 + len(c_args) + len(c_out_shapes)]
        i = pl.program_id(1)
        if comm is not None:
            j = pl.program_id(0)
            _comm_steps(comm, refs, 9, 12 + len(c_args), (j == 0) & (i == 0), (j == 3) & (i == 0), (j == 3) & (i == nt - 1))

        @pl.when(i == 0)
        def _():
            carry_ref[...] = jnp.zeros_like(carry_ref)

        x = xg_ref[:, 0:256]
        lg = xg_ref[:, 256:512]
        prev8 = jnp.where(i == 0, 0.0, xp_ref[:, 0:256])
        u, ra, ia, sp, a, m2, mult = _lru_gates(x, prev8, cw_ref, cb_ref, wa_ref, wx_ref, ba_ref, bx_ref, lam_ref)
        for k, v in enumerate((u, ra, ia, a, mult)):
            sv_ref[k] = v
        ac, hc = _lin_scan(a, mult * ia * u, False)
        h = hc + ac * carry_ref[0:1, :]
        hl_ref[...] = h
        carry_ref[0:1, :] = hl_ref[r - 1:r, :]
        g, _ = _gelu_and_grad(lg)
        ya_ref[...] = (g * h).astype(ya_ref.dtype)

    small = lambda rows: pl.BlockSpec((rows, 256), lambda j, i: (0, j))
    return pl.pallas_call(
        body, name=name, grid=(4, nt),
        in_specs=[pl.BlockSpec((r, LBLK), lambda j, i: (i, _lblk_col(j))),
                  pl.BlockSpec((8, LBLK), lambda j, i: (jnp.maximum(i * (r // 8) - 1, 0), _lblk_col(j))),
                  small(4), small(1),
                  pl.BlockSpec((1, 256, 256), lambda j, i: (j, 0, 0)), pl.BlockSpec((1, 256, 256), lambda j, i: (j, 0, 0)),
                  small(1), small(1), small(1)] + c_in_specs,
        out_specs=[pl.BlockSpec((r, 256), lambda j, i: (i, j)), pl.BlockSpec((r, 256), lambda j, i: (i, j)),
                   pl.BlockSpec((5, r, 256), lambda j, i: (0, i, j))] + c_out_specs,
        out_shape=[jax.ShapeDtypeStruct((t, D), F32), jax.ShapeDtypeStruct((t, D), _MXU),
                   jax.ShapeDtypeStruct((5, t, D), F32)] + c_out_shapes,
        scratch_shapes=[pltpu.VMEM((8, 256), F32)] + c_sems,
        compiler_params=_cp(("arbitrary", "arbitrary") if comm is not None else ("parallel", "arbitrary")),
    )(proj, proj, lw["cw"], lw["cb"], lw["wa"], lw["wx"], lw["ba"], lw["bx"], lw["lam"], *c_args)


def _lru_bwd(proj, hl, gates, dya, dproj, lw, *, r, name, comm=None):
    t = proj.shape[0]
    nt = t // r
    c_args, c_in_specs, c_out_specs, c_out_shapes, c_sems = _comm_specs(comm)
    n_in = 10

    def body(*refs):
        xg_ref, hl_ref, hp_ref, sv_ref, dya_ref, cw_ref, wa_ref, wx_ref, lam_ref = refs[:9]
        n_out = n_in + len(c_args)
        dproj_ref, sm_ref, dwa_ref, dwx_ref = refs[n_out:n_out + 4]
        n_scr = n_out + 4 + len(c_out_shapes)
        carry_ref, du8_ref, row_scr = refs[n_scr:n_scr + 3]
        i = pl.program_id(1)
        if comm is not None:
            j = pl.program_id(0)
            _comm_steps(comm, refs, n_in, n_out + 4, (j == 0) & (i == 0), (j == 3) & (i == 0), (j == 3) & (i == nt - 1))

        @pl.when(i == 0)
        def _():
            carry_ref[...] = jnp.zeros_like(carry_ref)
            du8_ref[...] = jnp.zeros_like(du8_ref)
            sm_ref[...] = jnp.zeros_like(sm_ref)
            dwa_ref[...] = jnp.zeros_like(dwa_ref)
            dwx_ref[...] = jnp.zeros_like(dwx_ref)

        tile0 = i == nt - 1
        xp = xg_ref[:, 0:256]
        lg = xg_ref[:, 256:512]
        u, ra, ia, a, mult = (sv_ref[k] for k in range(5))
        sp = _softplus(-lam_ref[...])
        h = hl_ref[...]
        hprev = _shift_down(h, jnp.where(tile0, 0.0, hp_ref[...]), 1)
        dya_v = dya_ref[...]
        g, dg = _gelu_and_grad(lg)
        ac, lc = _lin_scan(_shift_up(a, carry_ref[...], 1), dya_v * g, True)
        lam_v = lc + ac * carry_ref[1:2, :]
        row_scr[0:8, :] = lam_v[0:8]
        row_scr[8:16, :] = a[0:8]
        carry_ref[1:2, :] = row_scr[0:1, :]
        carry_ref[0:1, :] = row_scr[8:9, :]
        da = lam_v * hprev
        dmult = lam_v * ia * u
        dia = lam_v * mult * u
        dlog = da * a - dmult * (a * a) / mult
        dra = -LRU_C * sp * dlog
        dpa = dra * ra * (1.0 - ra)
        dpx = dia * ia * (1.0 - ia)
        du = lam_v * mult * ia + _dot_nt(dpa, wa_ref[0]) + _dot_nt(dpx, wx_ref[0])
        dwa_ref[0] += _dot_tn(u, dpa)
        dwx_ref[0] += _dot_tn(u, dpx)
        dlx = du * cw_ref[3:4, :]
        sm_ref[24:32, :] += _rsum8(du * xp)
        for k in (1, 2, 3):
            du_k = _shift_up(du, du8_ref[...], k)
            dlx = dlx + du_k * cw_ref[3 - k:4 - k, :]
            sm_ref[8 * (3 - k):8 * (4 - k), :] += _rsum8(du_k * xp)
        du8_ref[...] = du[0:8]
        dproj_ref[:, 0:256] = dlx.astype(dproj_ref.dtype)
        dproj_ref[:, 256:512] = (dya_v * h * dg).astype(dproj_ref.dtype)
        sm_ref[32:40, :] += _rsum8(du)
        sm_ref[40:48, :] += _rsum8(dpa)
        sm_ref[48:56, :] += _rsum8(dpx)
        sm_ref[56:64, :] += _rsum8(-LRU_C * ra * dlog) * (-_sigmoid(-lam_ref[...]))

    rev = lambda i: nt - 1 - i
    small = lambda rows: pl.BlockSpec((rows, 256), lambda j, i: (0, j))
    wblk = pl.BlockSpec((1, 256, 256), lambda j, i: (j, 0, 0))
    return pl.pallas_call(
        body, name=name, grid=(4, nt),
        in_specs=[pl.BlockSpec((r, LBLK), lambda j, i: (rev(i), _lblk_col(j))),
                  pl.BlockSpec((r, 256), lambda j, i: (rev(i), j)),
                  pl.BlockSpec((8, 256), lambda j, i: (jnp.maximum(rev(i) * (r // 8) - 1, 0), j)),
                  pl.BlockSpec((5, r, 256), lambda j, i: (0, rev(i), j)),
                  pl.BlockSpec((r, 256), lambda j, i: (rev(i), j)),
                  small(4), wblk, wblk, small(1),
                  pl.BlockSpec(memory_space=pl.ANY)] + c_in_specs,
        out_specs=[pl.BlockSpec((r, LBLK), lambda j, i: (rev(i), _lblk_col(j))),
                   pl.BlockSpec((64, 256), lambda j, i: (0, j)), wblk, wblk] + c_out_specs,
        out_shape=[jax.ShapeDtypeStruct(dproj.shape, dproj.dtype), jax.ShapeDtypeStruct((64, D), F32),
                   jax.ShapeDtypeStruct((4, 256, 256), F32), jax.ShapeDtypeStruct((4, 256, 256), F32)] + c_out_shapes,
        scratch_shapes=[pltpu.VMEM((8, 256), F32), pltpu.VMEM((8, 256), F32), pltpu.VMEM((16, 256), F32)] + c_sems,
        input_output_aliases={n_in - 1: 0},
        compiler_params=_cp(("arbitrary", "arbitrary") if comm is not None else ("parallel", "arbitrary")),
    )(proj, hl, hl, gates, dya, lw["cw"], lw["wa"], lw["wx"], lw["lam"], dproj, *c_args)


def _head_cols(x):
    lane = lax.broadcasted_iota(jnp.int32, x.shape, 1)
    return [jnp.sum(jnp.where(lane == h, x, 0.0), axis=1, keepdims=True) for h in range(N_HEADS)]


def _compact_heads(blocks):
    lane = lax.broadcasted_iota(jnp.int32, blocks[0].shape, 1)
    lo = lane < HEAD_P
    out = jnp.zeros_like(blocks[0])
    for j, blk in enumerate(blocks):
        s_lo = jnp.sum(jnp.where(lo, blk, 0.0), axis=1, keepdims=True)
        s_hi = jnp.sum(jnp.where(lo, 0.0, blk), axis=1, keepdims=True)
        out = jnp.where(lane == 2 * j, s_lo, out)
        out = jnp.where(lane == 2 * j + 1, s_hi, out)
    return out


def _ssd_prelude(dtraw_ref, dtb_ref, alog_ref, dt_scr, a_scr):
    lane = lax.broadcasted_iota(jnp.int32, dt_scr.shape, 1)
    dt = jnp.where(lane < N_HEADS, _softplus(dtraw_ref[...] + dtb_ref[0:1, :]), 0.0)
    dt_scr[...] = dt
    a_scr[...] = dt * (-jnp.exp(alog_ref[0:1, :]))


def _ssd_chunk_scalars(dt_scr, a_scr, r_scr, r0):
    a_c = a_scr[pl.ds(r0, CHUNK), :]
    dt_c = dt_scr[pl.ds(r0, CHUNK), :]
    i0 = lax.broadcasted_iota(jnp.int32, (CHUNK, CHUNK), 0)
    i1 = lax.broadcasted_iota(jnp.int32, (CHUNK, CHUNK), 1)
    tri = jnp.where(i0 >= i1, 1.0, 0.0).astype(F32)
    cs = jnp.dot(tri, a_c, precision=_HI, preferred_element_type=F32)
    lane = lax.broadcasted_iota(jnp.int32, (CHUNK, 128), 1)
    srow = lax.broadcasted_iota(jnp.int32, (CHUNK, 128), 0)
    t_lo = jnp.where((lane < HEAD_P) & (srow <= lane), 1.0, 0.0).astype(F32)
    t_hi = jnp.where((lane >= HEAD_P) & (srow <= lane - HEAD_P), 1.0, 0.0).astype(F32)
    even = (lane % 2) == 0
    tn = (((0,), (0,)), ((), ()))
    r_scr[...] = (lax.dot_general(jnp.where(even, a_c, 0.0), t_lo, tn, precision=_HI, preferred_element_type=F32)
                  + lax.dot_general(jnp.where(even, 0.0, a_c), t_hi, tn, precision=_HI, preferred_element_type=F32))
    return cs, dt_c, _head_cols(cs), _head_cols(dt_c)


def _block_diag2(v):
    lo = lax.broadcasted_iota(jnp.int32, v.shape, 1) < HEAD_P
    return jnp.concatenate([jnp.where(lo, v, 0.0), jnp.where(lo, 0.0, v)], axis=0).astype(_MXU)


def _ssd_pair(xc_scr, r_scr, cs_cols, dt_cols, s2, r0, j, s2t=None):
    lane = lax.broadcasted_iota(jnp.int32, (CHUNK, 128), 1)
    srow = lax.broadcasted_iota(jnp.int32, (CHUNK, 128), 0)
    lo = lane < HEAD_P
    csc = jnp.where(lo, cs_cols[2 * j], cs_cols[2 * j + 1])
    dtc = jnp.where(lo, dt_cols[2 * j], dt_cols[2 * j + 1])
    csr = r_scr[2 * j:2 * j + 1, :] + r_scr[2 * j + 1:2 * j + 2, :]
    dm = jnp.where((lane & (HEAD_P - 1)) <= srow, jnp.exp(jnp.minimum(csc - csr, 0.0)), 0.0)
    xs = xc_scr[pl.ds(r0, CHUNK), j * 128:(j + 1) * 128]
    xd = xs * dtc
    csl = jnp.sum(jnp.where(srow == CHUNK - 1, csc, 0.0), axis=0, keepdims=True)
    out = dict(csc=csc, dtc=dtc, dm=dm, m2=s2 * dm, xs=xs, xd=xd, rhs=_block_diag2(xd), e=jnp.exp(csc),
               w=jnp.exp(csl - csc), dec=jnp.exp(csl))
    if s2t is not None:
        out["mt2"] = s2t * jnp.where((lane & (HEAD_P - 1)) >= srow, jnp.exp(jnp.minimum(csr - csc, 0.0)), 0.0)
    return out


def _cat(parts):
    return jnp.concatenate(parts, axis=1)


def _ssd_fwd(proj, dtraw, sw, *, rb, name, comm=None):
    t = proj.shape[0]
    ns, cb = t // rb, rb // CHUNK
    c_args, c_in_specs, c_out_specs, c_out_shapes, c_sems = _comm_specs(comm)

    def body(*refs):
        zx_ref, zp_ref, dtraw_ref, cw_ref, cbias_ref, dtb_ref, alog_ref, dsk_ref, ng_ref = refs[:9]
        yssd_ref, yb_ref, st_ref, xc_scr, dsl_ref = refs[9 + len(c_args):14 + len(c_args)]
        n_scr = 14 + len(c_args) + len(c_out_shapes)
        h_scr, dt_scr, a_scr, r_scr = refs[n_scr:n_scr + 4]
        i = pl.program_id(0)
        if comm is not None:
            _comm_steps(comm, refs, 9, 14 + len(c_args), i == 0, i == (3 * ns) // 4, i == ns - 1)

        @pl.when(i == 0)
        def _():
            h_scr[...] = jnp.zeros_like(h_scr)

        for j in range(XBC // 128):
            cs_, zc = slice(128 * j, 128 * (j + 1)), slice(2048 + 128 * j, 2048 + 128 * (j + 1))
            pre = _conv4(zx_ref[:, zc], jnp.where(i == 0, 0.0, zp_ref[:, zc]), cw_ref, cbias_ref, cs_)
            sg = _sigmoid(pre)
            xc_scr[:, cs_] = pre * sg
            dsl_ref[:, cs_] = sg * (1.0 + pre * (1.0 - sg))
        _ssd_prelude(dtraw_ref, dtb_ref, alog_ref, dt_scr, a_scr)

        def chunk(c, carry):
            r0 = pl.multiple_of(c * CHUNK, CHUNK)
            _, _, cs_cols, dt_cols = _ssd_chunk_scalars(dt_scr, a_scr, r_scr, r0)
            st_ref[c] = h_scr[...]
            for g in range(N_GROUPS):
                bg = xc_scr[pl.ds(r0, CHUNK), 2048 + 128 * g:2048 + 128 * (g + 1)]
                cg = xc_scr[pl.ds(r0, CHUNK), 2560 + 128 * g:2560 + 128 * (g + 1)]
                s2 = _dot_nt(cg, jnp.concatenate([bg, bg], axis=0))
                hp = h_scr[:, 512 * g:512 * (g + 1)]
                yoff = _dot(cg, hp)
                xdw, dec = [], []
                for jj in range(4):
                    j = 4 * g + jj
                    p = _ssd_pair(xc_scr, r_scr, cs_cols, dt_cols, s2, r0, j)
                    y = _dot(p["m2"], p["rhs"]) + yoff[:, 128 * jj:128 * (jj + 1)] * p["e"]
                    yssd_ref[pl.ds(r0, CHUNK), 128 * j:128 * (j + 1)] = y + dsk_ref[0:1, 128 * j:128 * (j + 1)] * p["xs"]
                    xdw.append(p["xd"] * p["w"])
                    dec.append(p["dec"])
                h_scr[:, 512 * g:512 * (g + 1)] = hp * _cat(dec) + _dot_tn(bg, _cat(xdw))
            return carry

        lax.fori_loop(0, cb, chunk, 0)
        for g in range(N_GROUPS):
            sl = slice(512 * g, 512 * (g + 1))
            for q in range(rb // NORM_ROWS):
                rw = slice(NORM_ROWS * q, NORM_ROWS * (q + 1))
                yz = yssd_ref[rw, sl] * _silu(zx_ref[rw, sl])
                rg = lax.rsqrt(jnp.mean(yz * yz, axis=-1, keepdims=True) + EPS)
                yb_ref[rw, sl] = (yz * rg * ng_ref[0:1, sl]).astype(yb_ref.dtype)

    full = lambda rows, cols: pl.BlockSpec((rows, cols), lambda i: (0, 0))
    return pl.pallas_call(
        body, name=name, grid=(ns,),
        in_specs=[pl.BlockSpec((rb, ZX_W), lambda i: (i, 0)),
                  pl.BlockSpec((8, ZX_W), lambda i: (jnp.maximum(i * (rb // 8) - 1, 0), 0)),
                  pl.BlockSpec((rb, DT_PAD), lambda i: (i, 0)),
                  full(4, XBC), full(1, XBC), full(1, DT_PAD), full(1, DT_PAD), full(1, SSD_INNER), full(1, SSD_INNER)]
        + c_in_specs,
        out_specs=[pl.BlockSpec((rb, SSD_INNER), lambda i: (i, 0)), pl.BlockSpec((rb, SSD_INNER), lambda i: (i, 0)),
                   pl.BlockSpec((cb, N_STATE, SSD_INNER), lambda i: (i, 0, 0)),
                   pl.BlockSpec((rb, XBC), lambda i: (i, 0)), pl.BlockSpec((rb, XBC), lambda i: (i, 0))] + c_out_specs,
        out_shape=[jax.ShapeDtypeStruct((t, SSD_INNER), F32), jax.ShapeDtypeStruct((t, SSD_INNER), _MXU),
                   jax.ShapeDtypeStruct((t // CHUNK, N_STATE, SSD_INNER), F32),
                   jax.ShapeDtypeStruct((t, XBC), F32), jax.ShapeDtypeStruct((t, XBC), F32)] + c_out_shapes,
        scratch_shapes=[pltpu.VMEM((N_STATE, SSD_INNER), F32), pltpu.VMEM((rb, DT_PAD), F32),
                        pltpu.VMEM((rb, DT_PAD), F32), pltpu.VMEM((128, 128), F32)] + c_sems,
        compiler_params=_cp(("arbitrary",)),
    )(proj, proj, dtraw, sw["cw"], sw["cb"], sw["dtb"], sw["alog"], sw["dsk"], sw["ng"], *c_args)


def _ssd_bwd(proj, dtraw, yssd, states, xc, dsl, dyb, dproj, sw, *, rb, name, comm=None):
    t = proj.shape[0]
    ns, cb = t // rb, rb // CHUNK
    c_args, c_in_specs, c_out_specs, c_out_shapes, c_sems = _comm_specs(comm)
    n_in = 13

    def body(*refs):
        zx_ref, dtraw_ref, yssd_ref, st_ref, xc_scr, dsl_scr, dyb_ref, cw_ref, dtb_ref, alog_ref, dsk_ref, ng_ref = refs[:12]
        n_out = n_in + len(c_args)
        dzx_ref, ddt_ref, gconv_ref, gch_ref, ghd_ref = refs[n_out:n_out + 5]
        n_scr = n_out + 5 + len(c_out_shapes)
        dht_scr, dy_scr, dxc_scr, dt_scr, a_scr, r_scr, dp8_scr = refs[n_scr:n_scr + 7]
        i = pl.program_id(0)
        if comm is not None:
            _comm_steps(comm, refs, n_in, n_out + 5, i == 0, i == (3 * ns) // 4, i == ns - 1)

        @pl.when(i == 0)
        def _():
            dht_scr[...] = jnp.zeros_like(dht_scr)
            dp8_scr[...] = jnp.zeros_like(dp8_scr)
            gconv_ref[...] = jnp.zeros_like(gconv_ref)
            gch_ref[...] = jnp.zeros_like(gch_ref)
            ghd_ref[...] = jnp.zeros_like(ghd_ref)

        _ssd_prelude(dtraw_ref, dtb_ref, alog_ref, dt_scr, a_scr)

        for g in range(N_GROUPS):
            sl = slice(512 * g, 512 * (g + 1))
            for q in range(rb // NORM_ROWS):
                rw = slice(NORM_ROWS * q, NORM_ROWS * (q + 1))
                zv = zx_ref[rw, sl]
                ys = yssd_ref[rw, sl]
                sg = _sigmoid(zv)
                sz = zv * sg
                yz = ys * sz
                rg = lax.rsqrt(jnp.mean(yz * yz, axis=-1, keepdims=True) + EPS)
                yn = yz * rg
                dyb_v = dyb_ref[rw, sl]
                gch_ref[0:8, sl] += _rsum8(dyb_v * yn)
                dyn = dyb_v * ng_ref[0:1, sl]
                dyz = rg * (dyn - yn * jnp.mean(dyn * yn, axis=-1, keepdims=True))
                dy_scr[rw, sl] = dyz * sz
                dzx_ref[rw, sl] = (dyz * ys * (sg * (1.0 + zv * (1.0 - sg)))).astype(dzx_ref.dtype)

        a_row = -jnp.exp(alog_ref[0:1, :])

        def chunk(cc, carry):
            c = cb - 1 - cc
            r0 = pl.multiple_of(c * CHUNK, CHUNK)
            rows = pl.ds(r0, CHUNK)
            _, dt_c, cs_cols, dt_cols = _ssd_chunk_scalars(dt_scr, a_scr, r_scr, r0)
            lane = lax.broadcasted_iota(jnp.int32, (CHUNK, 128), 1)
            srow = lax.broadcasted_iota(jnp.int32, (CHUNK, 128), 0)
            lo = lane < HEAD_P
            last = srow == CHUNK - 1
            p1_blocks, p3_blocks = [], []
            for g in range(N_GROUPS):
                gs = slice(512 * g, 512 * (g + 1))
                bg = xc_scr[rows, 2048 + 128 * g:2048 + 128 * (g + 1)]
                cg = xc_scr[rows, 2560 + 128 * g:2560 + 128 * (g + 1)]
                b2 = jnp.concatenate([bg, bg], axis=0)
                s2 = _dot_nt(cg, b2)
                s2t = _dot_nt(bg, jnp.concatenate([cg, cg], axis=0))
                hp = st_ref[c, :, gs]
                dht = dht_scr[:, gs]
                yoff = _dot(cg, hp)
                ps = [_ssd_pair(xc_scr, r_scr, cs_cols, dt_cols, s2, r0, 4 * g + jj, s2t) for jj in range(4)]
                dys = [dy_scr[rows, 128 * (4 * g + jj):128 * (4 * g + jj + 1)] for jj in range(4)]
                dye = _cat([dys[jj] * ps[jj]["e"] for jj in range(4)])
                w_g = _cat([p["w"] for p in ps])
                dcg = _dot_nt(dye, hp)
                dht_scr[:, gs] = _dot_tn(cg, dye) + _cat([p["dec"] for p in ps]) * dht
                dxd_state = w_g * _dot(bg, dht)
                dbg = _dot_nt(_cat([p["xd"] for p in ps]) * w_g, dht)
                tsum = _rsum(dht * hp)
                ds2 = jnp.zeros((CHUNK, 128), F32)
                for jj in range(4):
                    j = 4 * g + jj
                    ls = slice(128 * j, 128 * (j + 1))
                    p, dy2 = ps[jj], dys[jj]
                    dy_bd = _block_diag2(dy2)
                    dm2 = _dot_nt(dy2, p["rhs"])
                    ds2 = ds2 + dm2 * p["dm"]
                    gdiff = dm2 * p["m2"] - _dot_nt(p["xd"], dy_bd) * p["mt2"]
                    dxs = dxd_state[:, 128 * jj:128 * (jj + 1)]
                    dxd = _dot(p["mt2"], dy_bd) + dxs
                    end_row = _rsum(p["xd"] * dxs) + p["dec"] * tsum[:, 128 * jj:128 * (jj + 1)]
                    p1_blocks.append(gdiff + dy2 * yoff[:, 128 * jj:128 * (jj + 1)] * p["e"] - p["xd"] * dxs
                                     + jnp.where(last, end_row, 0.0))
                    p3_blocks.append(dxd * p["xs"])
                    dxc_scr[rows, ls] = dxd * p["dtc"] + dy2 * dsk_ref[0:1, ls]
                    gch_ref[8:16, ls] += _rsum8(dy2 * p["xs"])
                dcg = dcg + _dot(ds2, b2)
                rb2 = _dot_tn(ds2, cg)
                dxc_scr[rows, 2048 + 128 * g:2048 + 128 * (g + 1)] = dbg + rb2[0:CHUNK] + rb2[CHUNK:2 * CHUNK]
                dxc_scr[rows, 2560 + 128 * g:2560 + 128 * (g + 1)] = dcg
            dcs = _compact_heads(p1_blocks)
            i0 = lax.broadcasted_iota(jnp.int32, (CHUNK, CHUNK), 0)
            i1 = lax.broadcasted_iota(jnp.int32, (CHUNK, CHUNK), 1)
            triu = jnp.where(i1 >= i0, 1.0, 0.0).astype(F32)
            da = jnp.dot(triu, dcs, precision=_HI, preferred_element_type=F32)
            ddt = _compact_heads(p3_blocks) + da * a_row
            ddtraw = jnp.where(lane < N_HEADS, ddt * _sigmoid(dtraw_ref[rows, :] + dtb_ref[0:1, :]), 0.0)
            ddt_ref[rows, :] = ddtraw.astype(ddt_ref.dtype)
            ghd_ref[0:1, :] += _rsum(ddtraw)
            ghd_ref[1:2, :] += _rsum(da * dt_c) * a_row
            return carry

        lax.fori_loop(0, cb, chunk, 0)
        for j in range(XBC // 128):
            cs_, zc = slice(128 * j, 128 * (j + 1)), slice(2048 + 128 * j, 2048 + 128 * (j + 1))
            dpre = dxc_scr[:, cs_] * dsl_scr[:, cs_]
            xraw = zx_ref[:, zc]
            dx = dpre * cw_ref[3:4, cs_]
            gconv_ref[24:32, cs_] += _rsum8(dpre * xraw)
            for k in (1, 2, 3):
                dpre_k = _shift_up(dpre, dp8_scr[:, cs_], k)
                dx = dx + dpre_k * cw_ref[3 - k:4 - k, cs_]
                gconv_ref[8 * (3 - k):8 * (4 - k), cs_] += _rsum8(dpre_k * xraw)
            dzx_ref[:, zc] = dx.astype(dzx_ref.dtype)
            dp8_scr[:, cs_] = dpre[0:8]
            gconv_ref[32:40, cs_] += _rsum8(dpre)

    rev = lambda i: ns - 1 - i
    full = lambda rows, cols: pl.BlockSpec((rows, cols), lambda i: (0, 0))
    return pl.pallas_call(
        body, name=name, grid=(ns,),
        in_specs=[pl.BlockSpec((rb, ZX_W), lambda i: (rev(i), 0)),
                  pl.BlockSpec((rb, DT_PAD), lambda i: (rev(i), 0)),
                  pl.BlockSpec((rb, SSD_INNER), lambda i: (rev(i), 0)),
                  pl.BlockSpec((cb, N_STATE, SSD_INNER), lambda i: (rev(i), 0, 0)),
                  pl.BlockSpec((rb, XBC), lambda i: (rev(i), 0)), pl.BlockSpec((rb, XBC), lambda i: (rev(i), 0)),
                  pl.BlockSpec((rb, SSD_INNER), lambda i: (rev(i), 0)),
                  full(4, XBC), full(1, DT_PAD), full(1, DT_PAD), full(1, SSD_INNER), full(1, SSD_INNER),
                  pl.BlockSpec(memory_space=pl.ANY)] + c_in_specs,
        out_specs=[pl.BlockSpec((rb, ZX_W), lambda i: (rev(i), 0)), pl.BlockSpec((rb, DT_PAD), lambda i: (rev(i), 0)),
                   full(40, XBC), full(16, SSD_INNER), full(8, DT_PAD)] + c_out_specs,
        out_shape=[jax.ShapeDtypeStruct(dproj.shape, dproj.dtype), jax.ShapeDtypeStruct((t, DT_PAD), _MXU),
                   jax.ShapeDtypeStruct((40, XBC), F32), jax.ShapeDtypeStruct((16, SSD_INNER), F32),
                   jax.ShapeDtypeStruct((8, DT_PAD), F32)] + c_out_shapes,
        scratch_shapes=[pltpu.VMEM((N_STATE, SSD_INNER), F32),
                        pltpu.VMEM((rb, SSD_INNER), F32), pltpu.VMEM((rb, XBC), F32), pltpu.VMEM((rb, DT_PAD), F32),
                        pltpu.VMEM((rb, DT_PAD), F32), pltpu.VMEM((128, 128), F32), pltpu.VMEM((8, XBC), F32)] + c_sems,
        input_output_aliases={n_in - 1: 0},
        compiler_params=_cp(("arbitrary",)),
    )(proj, dtraw, yssd, states, xc, dsl, dyb, sw["cw"], sw["dtb"], sw["alog"], sw["dsk"], sw["ng"], dproj, *c_args)


def _branch_merge(ya, yb, proj, wba, wbb, bgate, *, tm, tn, name):
    t = ya.shape[0]
    nj = D // tn

    def body(ya_ref, yb_ref, ga_ref, gb_ref, wba_ref, wbb_ref, ba_ref, bb_ref, ta_ref, tb_ref, mg_ref):
        ta = _dot(ya_ref[...], wba_ref[...])
        tb = _dot(yb_ref[...], wbb_ref[...])
        ta_ref[...] = ta.astype(ta_ref.dtype)
        tb_ref[...] = tb.astype(tb_ref.dtype)
        ga = _sigmoid(ga_ref[...] + ba_ref[...])
        gb = _sigmoid(gb_ref[...] + bb_ref[...])
        mg_ref[...] = (ga * ta + gb * tb).astype(mg_ref.dtype)

    tile = pl.BlockSpec((tm, tn), lambda i, j: (i, j))
    return pl.pallas_call(
        body, name=name, grid=(t // tm, nj),
        in_specs=[pl.BlockSpec((tm, D), lambda i, j: (i, 0)), pl.BlockSpec((tm, SSD_INNER), lambda i, j: (i, 0)),
                  pl.BlockSpec((tm, tn), lambda i, j: (i, G0 // tn + j)),
                  pl.BlockSpec((tm, tn), lambda i, j: (i, (G0 + D) // tn + j)),
                  pl.BlockSpec((D, tn), lambda i, j: (0, j)), pl.BlockSpec((SSD_INNER, tn), lambda i, j: (0, j)),
                  pl.BlockSpec((1, tn), lambda i, j: (0, j)), pl.BlockSpec((1, tn), lambda i, j: (0, nj + j))],
        out_specs=[tile, tile, tile],
        out_shape=[jax.ShapeDtypeStruct((t, D), _MXU)] * 3,
        compiler_params=_cp(("parallel", "parallel")),
    )(ya, yb, proj, proj, wba, wbb, bgate, bgate)


def _swiglu_mm(gu, wfo, residual, *, tm, tn, name):
    t = gu.shape[0]

    def body(gu_ref, w_ref, r_ref, act_ref, o_ref):
        @pl.when(pl.program_id(1) == 0)
        def _():
            gate = gu_ref[:, 0:D_FF].astype(F32)
            act_ref[...] = (_silu(gate) * gu_ref[:, D_FF:2 * D_FF].astype(F32)).astype(act_ref.dtype)
        o_ref[...] = jnp.dot(act_ref[...], w_ref[...], preferred_element_type=F32) + r_ref[...]

    return pl.pallas_call(
        body, name=name, grid=(t // tm, D // tn),
        in_specs=[pl.BlockSpec((tm, 2 * D_FF), lambda i, j: (i, 0)), pl.BlockSpec((D_FF, tn), lambda i, j: (0, j)),
                  pl.BlockSpec((tm, tn), lambda i, j: (i, j))],
        out_specs=[pl.BlockSpec((tm, D_FF), lambda i, j: (i, 0)), pl.BlockSpec((tm, tn), lambda i, j: (i, j))],
        out_shape=[jax.ShapeDtypeStruct((t, D_FF), _MXU), jax.ShapeDtypeStruct((t, D), F32)],
        compiler_params=_cp(("parallel", "arbitrary")),
    )(gu, wfo, residual)


def _ffn_bwd_act(dh, wfo, gu, *, tm, name, comm=None):
    t = dh.shape[0]
    ni = t // tm
    c_args, c_in_specs, c_out_specs, c_out_shapes, c_sems = _comm_specs(comm)

    def body(*refs):
        dh_ref, w_ref, gu_ref = refs[:3]
        o_ref = refs[3 + len(c_args)]
        if comm is not None:
            i = pl.program_id(0)
            _comm_steps(comm, refs, 3, 4 + len(c_args), i == 0, i == (3 * ni) // 4, i == ni - 1)
        dact = _dot_nt(dh_ref[...], w_ref[...])
        g = gu_ref[:, 0:D_FF].astype(F32)
        u = gu_ref[:, D_FF:2 * D_FF].astype(F32)
        sg = _sigmoid(g)
        o_ref[:, 0:D_FF] = (dact * u * (sg * (1.0 + g * (1.0 - sg)))).astype(o_ref.dtype)
        o_ref[:, D_FF:2 * D_FF] = (dact * (g * sg)).astype(o_ref.dtype)

    return pl.pallas_call(
        body, name=name, grid=(ni,),
        in_specs=[pl.BlockSpec((tm, D), lambda i: (i, 0)), pl.BlockSpec((D_FF, D), lambda i: (0, 0)),
                  pl.BlockSpec((tm, 2 * D_FF), lambda i: (i, 0))] + c_in_specs,
        out_specs=[pl.BlockSpec((tm, 2 * D_FF), lambda i: (i, 0))] + c_out_specs,
        out_shape=[jax.ShapeDtypeStruct((t, 2 * D_FF), _MXU)] + c_out_shapes,
        scratch_shapes=c_sems,
        compiler_params=_cp(("arbitrary",) if comm is not None else ("parallel",)),
    )(dh, wfo, gu, *c_args)


def _outproj_bwd(dh, wout, ta, tb, proj, bgate, dproj, *, tm, name):
    t = dh.shape[0]

    def body(dh_ref, w_ref, ta_ref, tb_ref, g_ref, b_ref, dta_ref, dtb_ref, dg_ref, db_ref):
        @pl.when(pl.program_id(0) == 0)
        def _():
            db_ref[...] = jnp.zeros_like(db_ref)
        dm = _dot_nt(dh_ref[...], w_ref[...])
        ga = _sigmoid(g_ref[:, 0:D] + b_ref[:, 0:D])
        gb = _sigmoid(g_ref[:, D:2 * D] + b_ref[:, D:2 * D])
        dta_ref[...] = (dm * ga).astype(dta_ref.dtype)
        dtb_ref[...] = (dm * gb).astype(dtb_ref.dtype)
        dga = dm * ta_ref[...].astype(F32) * ga * (1.0 - ga)
        dgb = dm * tb_ref[...].astype(F32) * gb * (1.0 - gb)
        dg_ref[:, 0:D] = dga.astype(dg_ref.dtype)
        dg_ref[:, D:2 * D] = dgb.astype(dg_ref.dtype)
        db_ref[0:1, 0:D] += _rsum(dga)
        db_ref[0:1, D:2 * D] += _rsum(dgb)

    row = lambda cols: pl.BlockSpec((tm, cols), lambda i: (i, 0))
    return pl.pallas_call(
        body, name=name, grid=(t // tm,),
        in_specs=[row(D), pl.BlockSpec((D, D), lambda i: (0, 0)), row(D), row(D),
                  pl.BlockSpec((tm, 2 * D), lambda i: (i, G0 // (2 * D))), pl.BlockSpec((1, 2 * D), lambda i: (0, 0))],
        out_specs=[row(D), row(D), pl.BlockSpec((tm, 2 * D), lambda i: (i, G0 // (2 * D))),
                   pl.BlockSpec((8, 2 * D), lambda i: (0, 0))],
        out_shape=[jax.ShapeDtypeStruct((t, D), _MXU), jax.ShapeDtypeStruct((t, D), _MXU),
                   jax.ShapeDtypeStruct(dproj, _MXU), jax.ShapeDtypeStruct((8, 2 * D), F32)],
        compiler_params=_cp(("arbitrary",)),
    )(dh, wout, ta, tb, proj, bgate)


def _loss_head(h, gf, target, *, tm, name):
    t = h.shape[0]

    def body(h_ref, g_ref, t_ref, loss_ref, dg_ref, dh_ref):
        @pl.when(pl.program_id(0) == 0)
        def _():
            loss_ref[...] = jnp.zeros_like(loss_ref)
            dg_ref[...] = jnp.zeros_like(dg_ref)
        x = h_ref[...]
        r = lax.rsqrt(jnp.mean(x * x, axis=-1, keepdims=True) + EPS)
        xh = x * r
        err = xh * g_ref[...] - t_ref[...]
        loss_ref[...] += 0.5 * jnp.sum(jnp.mean(err * err, axis=-1, keepdims=True), axis=0, keepdims=True)
        dy = err * (1.0 / D)
        dg_ref[0:1, :] += _rsum(dy * xh)
        dxh = dy * g_ref[...]
        dh_ref[...] = r * (dxh - xh * jnp.mean(dxh * xh, axis=-1, keepdims=True))

    row = pl.BlockSpec((tm, D), lambda i: (i, 0))
    return pl.pallas_call(
        body, name=name, grid=(t // tm,),
        in_specs=[row, pl.BlockSpec((1, D), lambda i: (0, 0)), row],
        out_specs=[pl.BlockSpec((8, 128), lambda i: (0, 0)), pl.BlockSpec((8, D), lambda i: (0, 0)), row],
        out_shape=[jax.ShapeDtypeStruct((8, 128), F32), jax.ShapeDtypeStruct((8, D), F32), jax.ShapeDtypeStruct((t, D), F32)],
        compiler_params=_cp(("arbitrary",)),
    )(h, gf, target)


def _row_tile(rows, cols, limit_bytes=1 << 20):
    best = None
    for tr in range(8, rows + 1, 8):
        if rows % tr == 0 and tr * cols * 4 <= limit_bytes:
            best = tr
    return best if best is not None else rows


def _adamw(w, g, m, v, *, name):
    rows, cols = w.shape
    tr = _row_tile(rows, cols)

    def body(w_ref, g_ref, m_ref, v_ref, d_ref, nm_ref, nv_ref):
        gv = g_ref[...]
        nm = ADAM_B1 * m_ref[...] + (1.0 - ADAM_B1) * gv
        nv = ADAM_B2 * v_ref[...] + (1.0 - ADAM_B2) * (gv * gv)
        m_hat = nm / (1.0 - ADAM_B1 ** ADAM_STEP)
        v_hat = nv / (1.0 - ADAM_B2 ** ADAM_STEP)
        d_ref[...] = -ADAM_LR * (m_hat / (jnp.sqrt(v_hat) + ADAM_EPS) + ADAM_WD * w_ref[...])
        nm_ref[...] = nm
        nv_ref[...] = nv

    blk = pl.BlockSpec((tr, cols), lambda i: (i, 0))
    shp = jax.ShapeDtypeStruct((rows, cols), F32)
    return pl.pallas_call(
        body, name=name, grid=(rows // tr,), in_specs=[blk] * 4, out_specs=[blk] * 3, out_shape=[shp] * 3,
        compiler_params=_cp(("parallel",)),
    )(w, g, m, v)


def _bd256(w):
    w4 = w.reshape(4, 4, 64, 64)
    eye = jnp.eye(4, dtype=w.dtype)
    return (w4[:, :, :, None, :] * eye[None, :, None, :, None]).reshape(4, 256, 256)


def _bd256_diag(g):
    g5 = g.reshape(4, 4, 64, 4, 64)
    return jnp.stack([g5[:, a, :, a, :] for a in range(4)], axis=1).reshape(16, 64, 64)


FFN_SHARD = 2 * D_FF // 4
W_IN_SHARD = IN_DIM // 4
W_IN_ROWS = 9344


def _w_in_cols(shards, c0, c1):
    out = []
    for p in range(4):
        lo, hi = max(c0, W_IN_SHARD * p), min(c1, W_IN_SHARD * (p + 1))
        if lo < hi:
            out.append(shards[p][:, lo - W_IN_SHARD * p:hi - W_IN_SHARD * p])
    return out


def _in_proj_weights(win):
    lblk = [_w_in_cols(win, 256 * j, 256 * (j + 1)) + _w_in_cols(win, D + 256 * j, D + 256 * (j + 1)) for j in range(4)]
    wp = jnp.concatenate(_w_in_cols(win, 2048, 4096) + _w_in_cols(win, 4096, 7168) + lblk[0] + lblk[1]
                         + _w_in_cols(win, 7200, 9248) + lblk[2] + lblk[3], axis=1)
    wdt = jnp.pad(jnp.concatenate(_w_in_cols(win, 7168, 7200), axis=1), ((0, 0), (0, DT_PAD - N_HEADS)))
    return wp, wdt


def _layer_weights(w, conv, small, l, wp, wdt):
    row = lambda v: v.reshape(1, -1)
    pad_h = lambda v: jnp.pad(v.reshape(1, -1), ((0, 0), (0, DT_PAD - N_HEADS)))
    lw = dict(cw=conv["lru_conv_w"][l], cb=row(small["lru_conv_b"][l]),
              wa=_bd256(small["lru_w_a"][l]).astype(_MXU), wx=_bd256(small["lru_w_x"][l]).astype(_MXU),
              ba=row(small["lru_b_a"][l]), bx=row(small["lru_b_x"][l]), lam=row(small["lru_lambda"][l]))
    sw = dict(cw=conv["ssd_conv_w"][l], cb=row(small["ssd_conv_b"][l]), dtb=pad_h(small["ssd_dt_bias"][l]),
              alog=pad_h(small["ssd_A_log"][l]), dsk=row(jnp.repeat(small["ssd_D"][l], HEAD_P)),
              ng=row(small["ssd_norm_g"][l]))
    return dict(wp=wp, wdt=wdt, lw=lw, sw=sw, wba=w["w_branch"][0:D], wbb=w["w_branch"][D:3 * D],
                wout=w["w_out"], wfi=w["w_ffn_in"], wfo=w["w_ffn_out"],
                g1=row(small["norm1_g"][l]), g2=row(small["norm2_g"][l]), bgate=row(small["b_gate"][l]))


def _tiles(t):
    return dict(tmn=min(1024, t), tm=min(512, t), tm2=min(256, t), r=min(256, t), rb=min(128, t))


def _layer_fwd(h, w, conv, small, l, carried=None):
    tl = _tiles(h.shape[0])
    n = f"l{l}_"
    carried = carried or {}
    arrived = []

    def carry(kernel, key, n_main, *args, **kw):
        comm, finish = carried.get(key, (None, None))
        outs = list(kernel(*args, comm=comm, **kw))
        if comm is not None:
            arrived.append(finish(outs[n_main:]))
        return outs[:n_main]

    wp, wdt = _in_proj_weights(w["w_in"])
    xn, proj = carry(_norm_mm, "in_proj", 2, h, small["norm1_g"][l].reshape(1, -1), wp, tm=tl["tmn"], tn=1024,
                     name=n + "in_proj")
    w = dict(w)
    for layer, ws in arrived:
        if layer == l:
            w.update(ws)
    lwt = _layer_weights(w, conv, small, l, wp, wdt)
    dtraw = _mm_nn(xn, lwt["wdt"], tm=tl["tm"], tn=DT_PAD, name=n + "dt_proj")
    hl, ya, gates = carry(_lru_fwd, "lru", 3, proj, lwt["lw"], r=tl["r"], name=n + "lru_fwd")
    yssd, yb, states, xc, dsl = carry(_ssd_fwd, "ssd", 5, proj, dtraw, lwt["sw"], rb=tl["rb"], name=n + "ssd_fwd")
    ta, tb, merged = _branch_merge(ya, yb, proj, lwt["wba"], lwt["wbb"], lwt["bgate"], tm=tl["tm"], tn=512, name=n + "merge")
    hmid = _mm_nn(merged, lwt["wout"], tm=tl["tm"], tn=512, name=n + "out_proj", residual=h)
    xn2, gu = _norm_mm(hmid, lwt["g2"], lwt["wfi"], tm=tl["tmn"], tn=FFN_SHARD, name=n + "ffn_in", out_dtype=_MXU)
    act, hout = _swiglu_mm(gu, lwt["wfo"], hmid, tm=tl["tm"], tn=512, name=n + "ffn_out")
    saved = dict(h=h, xn=xn, proj=proj, dtraw=dtraw, hl=hl, ya=ya, gates=gates, yssd=yssd, yb=yb, states=states, xc=xc, dsl=dsl, ta=ta, tb=tb,
                 merged=merged, hmid=hmid, xn2=xn2, gu=gu, act=act)
    return hout, saved, lwt, [x for x in arrived if x[0] != l]


def _layer_bwd(dh, s, lwt, l, hooks=None):
    t = dh.shape[0]
    tl = _tiles(t)
    n = f"l{l}_"
    tt = tl["tmn"]
    big = {}
    hooks = hooks or {}

    def wgrad(key, a, b, name, **kw):
        big[key] = _wgrad(a, b, tt=tt, name=n + name, into=big.get(key), **kw)

    dgu, = _ffn_bwd_act(dh, lwt["wfo"], s["gu"], tm=tl["tm2"], name=n + "ffn_act_bwd")
    wgrad("w_ffn_out", s["act"], dh, "ffn_out_wgrad", ta=D_FF, tn=1024, out_shape=(D_FF, D),
          out_block=(D_FF, 1024), out_index=lambda o, j: (o, j))
    wgrad("w_ffn_in", s["xn2"], dgu, "ffn_in_wgrad", ta=D, tn=FFN_SHARD, out_shape=(4, D, FFN_SHARD),
          out_block=(None, D, FFN_SHARD), out_index=lambda o, j: (j, o, 0))
    dh1, dg2 = _mm_nt_rmsbwd(dgu, lwt["wfi"], s["hmid"], lwt["g2"], dh, tm=tl["tm"], tk=FFN_SHARD, name=n + "ffn_in_dgrad")
    dta, dtb, dproj, dbg = _outproj_bwd(dh1, lwt["wout"], s["ta"], s["tb"], s["proj"], lwt["bgate"], (t, NP),
                                        tm=tl["tm2"], name=n + "out_proj_bwd")
    rows_d = dict(ta=D, tn=512, out_block=(D, 512), out_index=lambda o, j: (o, j))
    wgrad("w_out", s["merged"], dh1, "out_proj_wgrad", out_shape=(D, D), **rows_d)
    dya = _mm_nt(dta, lwt["wba"], tm=tl["tm"], name=n + "branch_a_dgrad")
    dyb = _mm_nt(dtb, lwt["wbb"], tm=tl["tm"], name=n + "branch_b_dgrad")
    wgrad("w_branch", s["ya"], dta, "branch_a_wgrad", out_shape=(3 * D, D), a_tab=[0], o_tab=[0], **rows_d)
    wgrad("w_branch", s["yb"], dtb, "branch_b_wgrad", out_shape=(3 * D, D), a_tab=[0, 1], o_tab=[1, 2], **rows_d)
    comm_1 = hooks["lru"](big) if "lru" in hooks else None
    dproj, lsm, dwa, dwx, *got_1 = _lru_bwd(s["proj"], s["hl"], s["gates"], dya, dproj, lwt["lw"], r=tl["r"], name=n + "lru_bwd",
                                            comm=comm_1)
    comm_2 = hooks["ssd"](got_1) if "ssd" in hooks else None
    dproj, ddt, gconv, gch, ghd, *got_2 = _ssd_bwd(s["proj"], s["dtraw"], s["yssd"], s["states"], s["xc"], s["dsl"], dyb, dproj, lwt["sw"],
                                                   rb=tl["rb"], name=n + "ssd_bwd", comm=comm_2)
    lsm = lsm.reshape(8, 8, D).sum(axis=1)
    gconv = gconv.reshape(5, 8, XBC).sum(axis=1)
    gch = gch.reshape(2, 8, SSD_INNER).sum(axis=1)
    w_in = dict(tn=D, out_shape=(W_IN_ROWS, D), out_index=lambda o, j: (o, j))
    wgrad("w_in", dproj, s["xn"], "in_proj_wgrad", ta=1024, out_block=(1024, D),
          a_tab=list(range(9)), o_tab=[2, 3, 4, 5, 6, 0, 7, 8, 1], **w_in)
    wgrad("w_in", ddt, s["xn"], "dt_proj_wgrad", ta=DT_PAD, out_block=(DT_PAD, D), a_tab=[0],
          o_tab=[NP // DT_PAD], **w_in)
    comm_3 = hooks["in_dgrad"](big) if "in_dgrad" in hooks else None
    dh0, dg1, *got_3 = _mm_nt_rmsbwd(dproj, lwt["wp"], s["h"], lwt["g1"], dh1, tm=tl["tm"], tk=4608,
                                     name=n + "in_proj_dgrad", extra=(ddt, lwt["wdt"]), comm=comm_3)
    grads = dict(
        lru_conv_w=lsm[0:4], lru_conv_b=lsm[4], lru_b_a=lsm[5], lru_b_x=lsm[6], lru_lambda=lsm[7],
        lru_w_a=_bd256_diag(dwa), lru_w_x=_bd256_diag(dwx),
        ssd_conv_w=gconv[0:4], ssd_conv_b=gconv[4], ssd_norm_g=gch[0], ssd_D=gch[1].reshape(N_HEADS, HEAD_P).sum(axis=-1),
        ssd_dt_bias=ghd[0, 0:N_HEADS], ssd_A_log=ghd[1, 0:N_HEADS],
        b_gate=dbg[0], norm1_g=dg1[0], norm2_g=dg2[0])
    return dh0, grads, big, (got_2, got_3)


def _local_step(x, target, w, conv, small, prefetch=None, early_reduce=None):
    h = x
    w = [dict(wl) for wl in w]
    lwts, saved = [], []
    for l in range(N_LAYERS):
        h, s, lwt, arrived = _layer_fwd(h, w[l], conv, small, l, prefetch if l == 0 else None)
        for layer, ws in arrived:
            w[layer].update(ws)
        lwts.append(lwt)
        saved.append(s)
    loss_blk, dgf, dh = _loss_head(h, small["norm_f"].reshape(1, D), target, tm=_tiles(x.shape[0])["tm"], name="loss_head")
    per_layer, big, carried = [None] * N_LAYERS, [None] * N_LAYERS, None
    for l in reversed(range(N_LAYERS)):
        hooks = early_reduce(big[1]) if (early_reduce is not None and l == 0) else None
        dh, per_layer[l], big[l], carried = _layer_bwd(dh, saved[l], lwts[l], l, hooks)
    grads = {k: jnp.stack([per_layer[l][k] for l in range(N_LAYERS)], axis=0) for k in per_layer[0]}
    grads["norm_f"] = dgf[0]
    return loss_blk, dh, grads, big, carried


PACK_W = 1024
BIG = (("w_in", W_IN_SHARD, D, W_IN_SHARD, 256), ("w_branch", 768, D, 256, D), ("w_out", 256, D, 256, D),
       ("w_ffn_in", D, FFN_SHARD, 256, FFN_SHARD), ("w_ffn_out", 704, D, 352, D))
CONV = ("lru_conv_w", "ssd_conv_w")
SMALL = ("norm1_g", "b_gate", "lru_conv_b", "lru_w_a", "lru_b_a", "lru_w_x", "lru_b_x", "lru_lambda", "ssd_conv_b",
         "ssd_dt_bias", "ssd_A_log", "ssd_D", "ssd_norm_g", "norm2_g", "norm_f")
_WIRE = jnp.bfloat16
N_CHIPS = 4
N_DEV = 8


def _mesh_pos():
    return lax.axis_index("x"), lax.axis_index("y"), lax.axis_index("c")


HBM_SPEC = pl.BlockSpec(memory_space=pltpu.HBM)


def _remote(src, dst, send_sems, recv_sems, k, to):
    return pltpu.make_async_remote_copy(src_ref=src, dst_ref=dst, send_sem=send_sems.at[k], recv_sem=recv_sems.at[k],
                                        device_id=to, device_id_type=MESH)


def _other_chips(x, y):
    return [(1 - x, y), (x, 1 - y), (1 - x, 1 - y)]


def _weight_fetch(loc, layer, owner):
    names = list(owner)
    rows = {n: loc[n].shape[1] for n in names}
    by_chip = ("w_in", "w_ffn_in")
    shapes = [((N_CHIPS,) + loc[n].shape[1:]) if n in by_chip else (N_CHIPS * rows[n], D) for n in names]

    def place(o_ref, n, chip):
        if n in by_chip:
            return o_ref.at[chip]
        return o_ref.at[pl.ds(pl.multiple_of(chip * rows[n], 16), rows[n]), :]

    def step(which, in_refs, o_refs, send_sems, recv_sems):
        x, y, c = _mesh_pos()
        s = 2 * x + y
        sib = (x, y, 1 - c)
        chips = _other_chips(x, y)
        for core in (0, 1):
            @pl.when(c == core)
            def _():
                for k, n in enumerate(names):
                    for j, (px, py) in enumerate(chips):
                        landed = place(o_refs[k], n, 2 * px + py)
                        sent = _remote(in_refs[k].at[layer], place(o_refs[k], n, s), send_sems, recv_sems, 3 * k + j,
                                       (px, py, c))
                        arrives = _remote(in_refs[k].at[layer], landed, send_sems, recv_sems, 3 * k + j, (px, py, c))
                        passed = _remote(landed, landed, send_sems, recv_sems, 3 * (len(names) + k) + j, sib)
                        if owner[n] == core:
                            if which == "start":
                                sent.start()
                            elif which == "mid":
                                arrives.wait_recv()
                                passed.start()
                            else:
                                sent.wait_send()
                                passed.wait_send()
                        elif which == "end":
                            passed.wait_recv()

    return dict(inputs=[loc[n] for n in names], names=names,
                out_shapes=[jax.ShapeDtypeStruct(shp, loc[n].dtype) for shp, n in zip(shapes, names)],
                sems=[pltpu.SemaphoreType.DMA((6 * len(names),)), pltpu.SemaphoreType.DMA((6 * len(names),))],
                start=functools.partial(step, "start"), mid=functools.partial(step, "mid"),
                end=functools.partial(step, "end"))


def _comm_now(comm, name):
    n, no = len(comm["inputs"]), len(comm["out_shapes"])

    def body(*refs):
        parts = (refs[:n], refs[n:n + no]) + tuple(refs[n + no:])
        comm["start"](*parts)
        comm["mid"](*parts)
        comm["end"](*parts)

    return pl.pallas_call(
        body, name=name, in_specs=[HBM_SPEC] * n, out_specs=[HBM_SPEC] * no, out_shape=comm["out_shapes"],
        scratch_shapes=comm["sems"],
    )(*comm["inputs"])


def _sibling_send(bufs, layer):
    n = len(bufs)

    def step(which, in_refs, o_refs, send_sems, recv_sems):
        x, y, c = _mesh_pos()
        copies = [_remote(in_refs[k], o_refs[k], send_sems, recv_sems, k, (x, y, 1 - c)) for k in range(n)]

        @pl.when(c != layer)
        def _():
            for cp in copies:
                if which == "start":
                    cp.start()
                elif which == "end":
                    cp.wait_send()

        @pl.when(c == layer)
        def _():
            for cp in copies:
                if which == "end":
                    cp.wait_recv()

    return dict(inputs=list(bufs), out_shapes=[jax.ShapeDtypeStruct(b.shape, b.dtype) for b in bufs],
                sems=[pltpu.SemaphoreType.DMA((n,)), pltpu.SemaphoreType.DMA((n,))],
                start=functools.partial(step, "start"), mid=functools.partial(step, "mid"),
                end=functools.partial(step, "end"))


def _add_cast(g, recv, own, *, a, tr, tc, name):
    wd = g.shape[1]
    nr = a // tr

    def body(own_ref, g_ref, r_ref, o_ref):
        @pl.when(own_ref[0] == 1)
        def _():
            o_ref[...] = (g_ref[...] + r_ref[...]).astype(o_ref.dtype)

    blk = pl.BlockSpec((tr, tc), lambda p, i, j, own_ref: ((p * nr + i) * own_ref[0], j * own_ref[0]))
    return pl.pallas_call(
        body, name=name,
        grid_spec=pltpu.PrefetchScalarGridSpec(
            num_scalar_prefetch=1, grid=(N_CHIPS, nr, wd // tc), in_specs=[blk, blk],
            out_specs=pl.BlockSpec((None, tr, tc), lambda p, i, j, own_ref: (p * own_ref[0], i * own_ref[0], j * own_ref[0]))),
        out_shape=jax.ShapeDtypeStruct((N_CHIPS, a, wd), _WIRE),
        compiler_params=_cp(("arbitrary", "arbitrary", "arbitrary")),
    )(own, g, recv)


def _chip_exchange(parts, layer):
    n = len(parts)

    def step(which, s_refs, o_refs, send_sems, recv_sems):
        x, y, c = _mesh_pos()
        s = 2 * x + y

        @pl.when(c == layer)
        def _():
            for j, (px, py) in enumerate(_other_chips(x, y)):
                for k in range(n):
                    p = 2 * px + py
                    sent = _remote(s_refs[k].at[p], o_refs[k].at[s], send_sems, recv_sems, n * j + k, (px, py, c))
                    if which == "start":
                        sent.start()
                    elif which == "end":
                        _remote(s_refs[k].at[p], o_refs[k].at[p], send_sems, recv_sems, n * j + k, (px, py, c)).wait_recv()
                        sent.wait_send()

    return dict(inputs=list(parts), out_shapes=[jax.ShapeDtypeStruct(p.shape, p.dtype) for p in parts],
                sems=[pltpu.SemaphoreType.DMA((3 * n,)), pltpu.SemaphoreType.DMA((3 * n,))],
                start=functools.partial(step, "start"), mid=functools.partial(step, "mid"),
                end=functools.partial(step, "end"))


def _sum_slots(slots, own, sel, *, tr, tc, name, layer=None, into=None):
    n, rows, wd = slots.shape
    k = own.shape[0]

    def body(sel_ref, s_ref, own_ref, *rest):
        o_ref = rest[-1]

        @pl.when(sel_ref[1] == 1)
        def _():
            mine = sel_ref[0]
            acc = jnp.zeros((tr, tc), F32)
            for p in range(n):
                acc = acc + jnp.where(mine == p, own_ref[...].astype(F32), s_ref[p].astype(F32))
            o_ref[...] = acc

    if layer is not None:
        out_spec = pl.BlockSpec((None, tr, tc), lambda i, j, sel_ref: (layer, i * sel_ref[1], j * sel_ref[1]))
        out_shape = jax.ShapeDtypeStruct((N_LAYERS, rows, wd), F32)
    else:
        out_spec = pl.BlockSpec((tr, tc), lambda i, j, sel_ref: (i * sel_ref[1], j * sel_ref[1]))
        out_shape = jax.ShapeDtypeStruct((rows, wd), F32)
    in_specs = [pl.BlockSpec((n, tr, tc), lambda i, j, sel_ref: (0, i * sel_ref[1], j * sel_ref[1])),
                pl.BlockSpec((None, tr, tc), lambda i, j, sel_ref: (sel_ref[0] if k > 1 else 0, i * sel_ref[1],
                                                                    j * sel_ref[1]))]
    args = [sel, slots, own]
    if into is not None:
        in_specs.append(pl.BlockSpec(memory_space=pl.ANY))
        args.append(into)
    return pl.pallas_call(
        body, name=name,
        grid_spec=pltpu.PrefetchScalarGridSpec(num_scalar_prefetch=1, grid=(rows // tr, wd // tc), in_specs=in_specs,
                                               out_specs=out_spec),
        out_shape=out_shape, input_output_aliases={3: 0} if into is not None else {},
        compiler_params=_cp(("arbitrary", "arbitrary")),
    )(*args)


def _sibling_share(both):
    n = len(both)

    def body(*refs):
        o_refs, (send_sems, recv_sems) = refs[n:2 * n], refs[2 * n:]
        x, y, c = _mesh_pos()
        sends = [_remote(o_refs[k].at[c], o_refs[k].at[c], send_sems, recv_sems, k, (x, y, 1 - c)) for k in range(n)]
        for cp in sends:
            cp.start()
        for k in range(n):
            _remote(o_refs[k].at[1 - c], o_refs[k].at[1 - c], send_sems, recv_sems, k, (x, y, 1 - c)).wait_recv()
        for cp in sends:
            cp.wait_send()

    return pl.pallas_call(
        body, name="grad_sibling_share", in_specs=[HBM_SPEC] * n, out_specs=[HBM_SPEC] * n,
        out_shape=[jax.ShapeDtypeStruct(b.shape, b.dtype) for b in both], input_output_aliases={k: k for k in range(n)},
        scratch_shapes=[pltpu.SemaphoreType.DMA((n,)), pltpu.SemaphoreType.DMA((n,))],
    )(*both)


def _allgather_devices(part):
    rows, wd = part.shape

    def step(which, in_refs, o_refs, send_sems, recv_sems):
        (p_ref,), (o_ref,) = in_refs, o_refs
        x, y, c = _mesh_pos()
        sib = (x, y, 1 - c)
        chips = _other_chips(x, y)
        slot = lambda px, py, pc: o_ref.at[4 * px + 2 * py + pc]
        first = [_remote(p_ref, slot(x, y, c), send_sems, recv_sems, 0, sib)]
        first += [_remote(p_ref, slot(x, y, c), send_sems, recv_sems, 1 + j, (px, py, c)) for j, (px, py) in enumerate(chips)]
        passed = [_remote(slot(px, py, c), slot(px, py, c), send_sems, recv_sems, 4 + j, sib)
                  for j, (px, py) in enumerate(chips)]
        if which == "start":
            for cp in first:
                cp.start()
        elif which == "mid":
            for j, (px, py) in enumerate(chips):
                _remote(p_ref, slot(px, py, c), send_sems, recv_sems, 1 + j, (px, py, c)).wait_recv()
                passed[j].start()
        else:
            _remote(p_ref, slot(x, y, 1 - c), send_sems, recv_sems, 0, sib).wait_recv()
            for j, (px, py) in enumerate(chips):
                _remote(slot(px, py, 1 - c), slot(px, py, 1 - c), send_sems, recv_sems, 4 + j, sib).wait_recv()
            for cp in first + passed:
                cp.wait_send()

    return dict(inputs=[part], out_shapes=[jax.ShapeDtypeStruct((N_DEV, rows, wd), part.dtype)],
                sems=[pltpu.SemaphoreType.DMA((N_DEV - 1,)), pltpu.SemaphoreType.DMA((N_DEV - 1,))],
                start=functools.partial(step, "start"), mid=functools.partial(step, "mid"),
                end=functools.partial(step, "end"))


def _comm_both(a, b):
    na, nao = len(a["inputs"]), len(a["out_shapes"])

    def step(which, in_refs, o_refs, sa, ra, sb, rb_):
        a[which](in_refs[:na], o_refs[:nao], sa, ra)
        b[which](in_refs[na:], o_refs[nao:], sb, rb_)

    return dict(inputs=a["inputs"] + b["inputs"], out_shapes=a["out_shapes"] + b["out_shapes"], sems=a["sems"] + b["sems"],
                start=functools.partial(step, "start"), mid=functools.partial(step, "mid"),
                end=functools.partial(step, "end"))


def _by_chip_to_full(stack):
    _, nl, r, b = stack.shape
    return stack.transpose(1, 2, 0, 3).reshape(nl, r, N_CHIPS * b)


def _sharded_step(a):
    x = a["x"][0]
    target = a["loss_target"][0]
    cx, cy, cc = _mesh_pos()
    chip = (2 * cx + cy).astype(jnp.int32)
    core = cc.astype(jnp.int32)
    me = (4 * cx + 2 * cy + cc).astype(jnp.int32)
    zero = jnp.zeros((), jnp.int32)
    dus = lax.dynamic_update_slice

    loc = {n: a[n].astype(_MXU) for n, *_ in BIG}

    def with_own(got, names, layer):
        out = {}
        for g, n in zip(got, names):
            mine = loc[n][layer]
            out[n] = (dus(g, mine[None], (chip, zero, zero)) if g.ndim == 3 else dus(g, mine, (chip * mine.shape[0], zero)))
        return out

    rest = {"w_ffn_in": 0, "w_branch": 1, "w_out": 1, "w_ffn_out": 1}
    conv_loc = jnp.concatenate([a[n].reshape(-1, PACK_W) for n in CONV], axis=0)
    now = _weight_fetch(loc, 0, {"w_in": 0})
    conv_all, *got_now = _comm_now(_comm_both(_allgather_devices(conv_loc), now), "allgather_weights")
    w0 = with_own(got_now, now["names"], 0)
    later = {"in_proj": (0, _weight_fetch(loc, 0, rest)), "lru": (1, _weight_fetch(loc, 1, {"w_in": 0})),
             "ssd": (1, _weight_fetch(loc, 1, rest))}
    prefetch = {k: (f, functools.partial(lambda got, layer, f: (layer, with_own(got, f["names"], layer)), layer=layer, f=f))
                for k, (layer, f) in later.items()}
    conv_all = dus(conv_all, conv_loc[None], (me, zero, zero))[0::2]
    conv, off = {}, 0
    for n in CONV:
        rows = a[n].size // PACK_W
        conv[n] = _by_chip_to_full(conv_all[:, off:off + rows].reshape((N_CHIPS,) + a[n].shape))
        off += rows
    small = {n: a[n] for n in SMALL}

    views = lambda big_l, specs: [big_l[n].reshape(-1, wd) for n, _, wd, _, _ in specs]
    owns = lambda layer: (core == layer).astype(jnp.int32)
    w_in_only, others = BIG[:1], BIG[1:]

    def partial_sums(big_l, recv, layer, specs):
        return [_add_cast(v, r, owns(layer).reshape(1), a=rows, tr=tr, tc=tc, name=f"grad_add_sibling_l{layer}_{n}")
                for v, r, (n, rows, _, tr, tc) in zip(views(big_l, specs), recv, specs)]

    def reduced(slots, parts, layer, into, specs):
        sel = jnp.stack([chip, owns(layer)])
        return [_sum_slots(s, p, sel, tr=tr, tc=tc, name=f"grad_sum_chips_l{layer}_{n}", layer=layer, into=buf)
                for s, p, buf, (n, _, _, tr, tc) in zip(slots, parts, into, specs)]

    kept = {}

    def early_reduce(big_1):
        def during_lru(big_0):
            kept["big_0"] = dict(big_0)
            return _comm_both(_sibling_send(views(big_1, BIG), 1), _sibling_send(views(big_0, others), 0))

        def during_ssd(recv):
            kept["parts_1"] = partial_sums(big_1, recv[:len(BIG)], 1, BIG)
            kept["parts_0"] = partial_sums(kept["big_0"], recv[len(BIG):], 0, others)
            return _comm_both(_chip_exchange(kept["parts_1"], 1), _chip_exchange(kept["parts_0"], 0))

        return dict(lru=during_lru, ssd=during_ssd, in_dgrad=lambda big_0: _sibling_send(views(big_0, w_in_only), 0))

    loss_blk, grad_x, grads, big, (slots, recv_in) = _local_step(x, target, [w0, {}], conv, small, prefetch, early_reduce)
    loss = lax.psum(loss_blk[0, 0], ("x", "y", "c"))
    both = reduced(slots[:len(BIG)], kept["parts_1"], 1, [None] * len(BIG), BIG)
    both[1:] = reduced(slots[len(BIG):], kept["parts_0"], 0, both[1:], others)
    parts_in = partial_sums(big[0], recv_in, 0, w_in_only)
    names = SMALL + CONV
    srows = -(-sum(grads[n].size for n in names) // (8 * PACK_W)) * 8
    flat = lambda d, ns: jnp.concatenate([d[n].reshape(-1) for n in ns])
    padto = lambda v: jnp.pad(v, (0, srows * PACK_W - v.shape[0])).reshape(srows, PACK_W)
    g_own = padto(flat(grads, names))
    g_all, *slots_in = _comm_now(_comm_both(_allgather_devices(g_own), _chip_exchange(parts_in, 0)), "grad_chip_exchange")
    both[:1] = reduced(slots_in, parts_in, 0, both[:1], w_in_only)
    done = dict(zip([n for n, *_ in BIG], _sibling_share(both)))
    g_big = {n: done[n].reshape(a[n].shape) for n in ("w_branch", "w_out", "w_ffn_in", "w_ffn_out")}
    gt = done["w_in"].transpose(0, 2, 1)
    first = jnp.concatenate([gt[..., 512 * j + 256 * part:512 * j + 256 * (part + 1)] for part in range(2) for j in range(4)]
                            + [gt[..., 2 * D:]], axis=-1)
    tail = W_IN_SHARD - (IN_DIM - 7168)
    last = jnp.concatenate([gt[..., :tail], gt[..., W_IN_SHARD - N_HEADS:], gt[..., tail:W_IN_SHARD - N_HEADS]], axis=-1)
    g_big["w_in"] = jnp.where(chip == 0, first, jnp.where(chip == N_CHIPS - 1, last, gt))

    g_sum = _sum_slots(g_all, g_own[None], jnp.stack([me, zero + 1]), tr=srows, tc=PACK_W, name="small_grad_sum")
    off, g_small = 0, {}
    for n in names:
        g_small[n] = g_sum.reshape(-1)[off:off + grads[n].size].reshape(grads[n].shape)
        off += grads[n].size
    for n in CONV:
        width = a[n].shape[2]
        g_big[n] = lax.dynamic_slice(g_small.pop(n), (zero, zero, chip * width), a[n].shape)

    out_g, out_d, out_m, out_v = {}, {}, {}, {}
    for n in g_big:
        shp = a[n].shape
        two_d = (shp[0] * shp[1], shp[2])
        d_, m_, v_ = _adamw(a[n].reshape(two_d), g_big[n].reshape(two_d), a["m_" + n].reshape(two_d),
                            a["v_" + n].reshape(two_d), name="adamw_" + n)
        out_g[n], out_d[n], out_m[n], out_v[n] = g_big[n], d_.reshape(shp), m_.reshape(shp), v_.reshape(shp)
    d_, m_, v_ = _adamw(padto(flat(a, SMALL)), padto(flat(g_small, SMALL)), padto(flat({n: a["m_" + n] for n in SMALL}, SMALL)),
                        padto(flat({n: a["v_" + n] for n in SMALL}, SMALL)), name="adamw_small")
    off = 0
    for n in SMALL:
        cut = lambda v: v.reshape(-1)[off:off + a[n].size].reshape(a[n].shape)
        out_g[n], out_d[n], out_m[n], out_v[n] = g_small[n], cut(d_), cut(m_), cut(v_)
        off += a[n].size
    return loss, grad_x[None], out_g, out_d, out_m, out_v


WEIGHTS = ("norm1_g", "w_in", "b_gate", "lru_conv_w", "lru_conv_b", "lru_w_a", "lru_b_a", "lru_w_x", "lru_b_x", "lru_lambda",
           "ssd_conv_w", "ssd_conv_b", "ssd_dt_bias", "ssd_A_log", "ssd_D", "ssd_norm_g", "w_branch", "w_out", "norm2_g",
           "w_ffn_in", "w_ffn_out", "norm_f")
INPUTS = ("x",) + WEIGHTS + ("loss_target",) + tuple("m_" + n for n in WEIGHTS) + tuple("v_" + n for n in WEIGHTS)


def kernel(x, norm1_g, w_in, b_gate, lru_conv_w, lru_conv_b, lru_w_a, lru_b_a, lru_w_x, lru_b_x, lru_lambda, ssd_conv_w, ssd_conv_b, ssd_dt_bias, ssd_A_log, ssd_D, ssd_norm_g, w_branch, w_out, norm2_g, w_ffn_in, w_ffn_out, norm_f, loss_target, m_norm1_g, m_w_in, m_b_gate, m_lru_conv_w, m_lru_conv_b, m_lru_w_a, m_lru_b_a, m_lru_w_x, m_lru_b_x, m_lru_lambda, m_ssd_conv_w, m_ssd_conv_b, m_ssd_dt_bias, m_ssd_A_log, m_ssd_D, m_ssd_norm_g, m_w_branch, m_w_out, m_norm2_g, m_w_ffn_in, m_w_ffn_out, m_norm_f, v_norm1_g, v_w_in, v_b_gate, v_lru_conv_w, v_lru_conv_b, v_lru_w_a, v_lru_b_a, v_lru_w_x, v_lru_b_x, v_lru_lambda, v_ssd_conv_w, v_ssd_conv_b, v_ssd_dt_bias, v_ssd_A_log, v_ssd_D, v_ssd_norm_g, v_w_branch, v_w_out, v_norm2_g, v_w_ffn_in, v_w_ffn_out, v_norm_f):
    args = (x, norm1_g, w_in, b_gate, lru_conv_w, lru_conv_b, lru_w_a, lru_b_a, lru_w_x, lru_b_x, lru_lambda, ssd_conv_w, ssd_conv_b, ssd_dt_bias, ssd_A_log, ssd_D, ssd_norm_g, w_branch, w_out, norm2_g, w_ffn_in, w_ffn_out, norm_f, loss_target, m_norm1_g, m_w_in, m_b_gate, m_lru_conv_w, m_lru_conv_b, m_lru_w_a, m_lru_b_a, m_lru_w_x, m_lru_b_x, m_lru_lambda, m_ssd_conv_w, m_ssd_conv_b, m_ssd_dt_bias, m_ssd_A_log, m_ssd_D, m_ssd_norm_g, m_w_branch, m_w_out, m_norm2_g, m_w_ffn_in, m_w_ffn_out, m_norm_f, v_norm1_g, v_w_in, v_b_gate, v_lru_conv_w, v_lru_conv_b, v_lru_w_a, v_lru_b_a, v_lru_w_x, v_lru_b_x, v_lru_lambda, v_ssd_conv_w, v_ssd_conv_b, v_ssd_dt_bias, v_ssd_A_log, v_ssd_D, v_ssd_norm_g, v_w_branch, v_w_out, v_norm2_g, v_w_ffn_in, v_w_ffn_out, v_norm_f)
    assert len(args) == len(INPUTS)
    loss, grad_x, g, d, m, v = _sharded_step(dict(zip(INPUTS, args)))
    return (loss, grad_x, *[g[n] for n in WEIGHTS], *[d[n] for n in WEIGHTS], *[m[n] for n in WEIGHTS],
            *[v[n] for n in WEIGHTS])
```

```python
import functools
import math

import numpy as np
import jax
import jax.numpy as jnp
from jax import lax
from jax.experimental import pallas as pl
from jax.experimental.pallas import tpu as pltpu

F32 = jnp.float32
_MXU = jnp.bfloat16
_HI = lax.Precision.HIGHEST

D = 1024
EPS = 1e-6
N_LAYERS = 2
LRU_C = 8.0
N_HEADS = 32
HEAD_P = 64
N_GROUPS = 4
N_STATE = 128
SSD_INNER = 2048
XBC = 3072
D_FF = 2816
CHUNK = 64
NORM_ROWS = 32
IN_DIM = 9248

NP = 9216
ZX_W = 5120
G0 = 6144
LBLK = 512
DT_PAD = 128

VMEM_LIMIT_BYTES_V7X = 56 * 1024 * 1024

ADAM_LR, ADAM_B1, ADAM_B2, ADAM_EPS, ADAM_WD, ADAM_STEP = 0.001, 0.9, 0.999, 1e-08, 0.01, 10
MESH = pl.DeviceIdType.MESH


def _cp(sem):
    return pltpu.CompilerParams(dimension_semantics=sem, vmem_limit_bytes=VMEM_LIMIT_BYTES_V7X)


def _lblk_col(j):
    return 10 + j + 4 * (j // 2)


def _sigmoid(x):
    return 0.5 * jnp.tanh(0.5 * x) + 0.5


def _softplus(x):
    return jnp.maximum(x, 0.0) + jnp.log(1.0 + jnp.exp(-jnp.abs(x)))


def _silu(x):
    return x * _sigmoid(x)


_GELU_C0 = math.sqrt(2.0 / math.pi)
_GELU_C1 = 0.044715


def _gelu_and_grad(x):
    t = jnp.tanh(_GELU_C0 * (x + _GELU_C1 * x * x * x))
    g = 0.5 * x * (1.0 + t)
    dg = 0.5 * (1.0 + t) + 0.5 * x * (1.0 - t * t) * _GELU_C0 * (1.0 + 3.0 * _GELU_C1 * x * x)
    return g, dg


def _one_minus_exp(x):
    p = 1.0 + x * (1.0 / 7.0)
    p = 1.0 + x * (1.0 / 6.0) * p
    p = 1.0 + x * (1.0 / 5.0) * p
    p = 1.0 + x * (1.0 / 4.0) * p
    p = 1.0 + x * (1.0 / 3.0) * p
    p = 1.0 + x * (1.0 / 2.0) * p
    return jnp.where(x > -0.3, -x * p, 1.0 - jnp.exp(x))


def _dot(a, b):
    return jnp.dot(a.astype(_MXU), b.astype(_MXU), preferred_element_type=F32)


def _dot_nt(a, b):
    return lax.dot_general(a.astype(_MXU), b.astype(_MXU), (((1,), (1,)), ((), ())), preferred_element_type=F32)


def _dot_tn(a, b):
    return lax.dot_general(a.astype(_MXU), b.astype(_MXU), (((0,), (0,)), ((), ())), preferred_element_type=F32)


def _shift_down(x, prev8, k):
    xr = pltpu.roll(x, k, 0)
    pr = pltpu.roll(prev8, k, 0)
    row = lax.broadcasted_iota(jnp.int32, prev8.shape, 0)
    head = jnp.where(row < k, pr, xr[0:8])
    return jnp.concatenate([head, xr[8:]], axis=0)


def _shift_up(x, next8, k):
    r = x.shape[0]
    xr = pltpu.roll(x, r - k, 0)
    nr = pltpu.roll(next8, 8 - k, 0)
    row = lax.broadcasted_iota(jnp.int32, next8.shape, 0)
    tail = jnp.where(row >= 8 - k, nr, xr[r - 8:r])
    return jnp.concatenate([xr[:r - 8], tail], axis=0)


def _conv4(x, prev8, w_ref, b_ref, cols=slice(None)):
    acc = x * w_ref[3:4, cols] + b_ref[0:1, cols]
    for k in (1, 2, 3):
        acc = acc + _shift_down(x, prev8, k) * w_ref[3 - k:4 - k, cols]
    return acc


def _lin_scan(a, b, reverse):
    r = a.shape[0]
    row = lax.broadcasted_iota(jnp.int32, a.shape, 0)
    d = 1
    while d < r:
        sh = (r - d) if reverse else d
        a_s = pltpu.roll(a, sh, 0)
        b_s = pltpu.roll(b, sh, 0)
        m = (row < r - d) if reverse else (row >= d)
        b = jnp.where(m, a * b_s + b, b)
        a = jnp.where(m, a * a_s, a)
        d *= 2
    return a, b


def _rsum(x):
    return jnp.sum(x, axis=0, keepdims=True)


def _comm_specs(comm):
    if comm is None:
        return [], [], [], [], []
    n = len(comm["inputs"])
    return list(comm["inputs"]), [HBM_SPEC] * n, [HBM_SPEC] * len(comm["out_shapes"]), list(comm["out_shapes"]), comm["sems"]


def _comm_steps(comm, refs, n_in, n_out, first, mid, last):
    ni, no = len(comm["inputs"]), len(comm["out_shapes"])
    parts = (refs[n_in:n_in + ni], refs[n_out:n_out + no]) + tuple(refs[len(refs) - len(comm["sems"]):])
    for when, what in ((first, "start"), (mid, "mid"), (last, "end")):
        @pl.when(when)
        def _():
            comm[what](*parts)


def _norm_mm(h, gamma, w, *, tm, tn, name, out_dtype=F32, comm=None):
    m, k = h.shape
    if w.ndim == 3:
        assert w.shape[2] == tn
        n = w.shape[0] * tn
        w_spec = pl.BlockSpec((None, k, tn), lambda i, j: (j, 0, 0))
    else:
        n = w.shape[1]
        w_spec = pl.BlockSpec((k, tn), lambda i, j: (0, j))

    c_args, c_in_specs, c_out_specs, c_out_shapes, c_sems = _comm_specs(comm)
    ni, nj = m // tm, n // tn

    def body(*refs):
        h_ref, g_ref, w_ref = refs[:3]
        xn_ref, o_ref = refs[3 + len(c_args):5 + len(c_args)]
        i, j = pl.program_id(0), pl.program_id(1)
        if comm is not None:
            _comm_steps(comm, refs, 3, 5 + len(c_args), (i == 0) & (j == 0), (i == (3 * ni) // 4) & (j == 0),
                        (i == ni - 1) & (j == nj - 1))

        @pl.when(j == 0)
        def _():
            x = h_ref[...]
            r = lax.rsqrt(jnp.mean(x * x, axis=-1, keepdims=True) + EPS)
            xn_ref[...] = ((x * r) * g_ref[...]).astype(xn_ref.dtype)
        o_ref[...] = jnp.dot(xn_ref[...], w_ref[...], preferred_element_type=F32).astype(o_ref.dtype)

    return pl.pallas_call(
        body, name=name, grid=(ni, nj),
        in_specs=[pl.BlockSpec((tm, k), lambda i, j: (i, 0)), pl.BlockSpec((1, k), lambda i, j: (0, 0)), w_spec] + c_in_specs,
        out_specs=[pl.BlockSpec((tm, k), lambda i, j: (i, 0)), pl.BlockSpec((tm, tn), lambda i, j: (i, j))] + c_out_specs,
        out_shape=[jax.ShapeDtypeStruct((m, k), _MXU), jax.ShapeDtypeStruct((m, n), out_dtype)] + c_out_shapes,
        scratch_shapes=c_sems,
        compiler_params=_cp(("arbitrary", "arbitrary") if comm is not None else ("parallel", "arbitrary")),
    )(h, gamma, w, *c_args)


def _mm_nn(a, w, *, tm, tn, name, residual=None):
    m, k = a.shape
    n = w.shape[1]

    def body(*refs):
        if residual is None:
            a_ref, w_ref, o_ref = refs
            o_ref[...] = _dot(a_ref[...], w_ref[...])
        else:
            a_ref, w_ref, r_ref, o_ref = refs
            o_ref[...] = _dot(a_ref[...], w_ref[...]) + r_ref[...]

    in_specs = [pl.BlockSpec((tm, k), lambda i, j: (i, 0)), pl.BlockSpec((k, tn), lambda i, j: (0, j))]
    args = [a, w]
    if residual is not None:
        in_specs.append(pl.BlockSpec((tm, tn), lambda i, j: (i, j)))
        args.append(residual)
    return pl.pallas_call(
        body, name=name, grid=(m // tm, n // tn), in_specs=in_specs,
        out_specs=pl.BlockSpec((tm, tn), lambda i, j: (i, j)),
        out_shape=jax.ShapeDtypeStruct((m, n), F32),
        compiler_params=_cp(("parallel", "parallel")),
    )(*args)


def _wgrad(a, b, *, tt, ta, tn, name, out_shape, out_block, out_index, a_tab=None, o_tab=None, into=None):
    t = a.shape[0]
    a_tab = list(range(a.shape[1] // ta)) if a_tab is None else a_tab
    o_tab = a_tab if o_tab is None else o_tab
    nb = b.shape[1] // tn

    def body(at_ref, ot_ref, a_ref, b_ref, *rest):
        del at_ref, ot_ref
        o_ref = rest[-1]

        @pl.when(pl.program_id(2) == 0)
        def _():
            o_ref[...] = jnp.zeros_like(o_ref)
        o_ref[...] += _dot_tn(a_ref[...], b_ref[...])

    in_specs = [pl.BlockSpec((tt, ta), lambda r, j, i, at, ot: (i, at[r])),
                pl.BlockSpec((tt, tn), lambda r, j, i, at, ot: (i, j))]
    args = [jnp.asarray(a_tab, jnp.int32), jnp.asarray(o_tab, jnp.int32), a, b]
    aliases = {}
    if into is not None:
        in_specs.append(pl.BlockSpec(memory_space=pl.ANY))
        args.append(into)
        aliases = {4: 0}
    return pl.pallas_call(
        body, name=name,
        grid_spec=pltpu.PrefetchScalarGridSpec(
            num_scalar_prefetch=2, grid=(len(a_tab), nb, t // tt), in_specs=in_specs,
            out_specs=pl.BlockSpec(out_block, lambda r, j, i, at, ot: out_index(ot[r], j))),
        out_shape=jax.ShapeDtypeStruct(out_shape, F32), input_output_aliases=aliases,
        compiler_params=_cp(("parallel", "parallel", "arbitrary")),
    )(*args)


def _mm_nt(a, w, *, tm, name):
    m, kc = a.shape
    n = w.shape[0]

    def body(a_ref, w_ref, o_ref):
        o_ref[...] = _dot_nt(a_ref[...], w_ref[...])

    return pl.pallas_call(
        body, name=name, grid=(m // tm,),
        in_specs=[pl.BlockSpec((tm, kc), lambda i: (i, 0)), pl.BlockSpec((n, kc), lambda i: (0, 0))],
        out_specs=pl.BlockSpec((tm, n), lambda i: (i, 0)),
        out_shape=jax.ShapeDtypeStruct((m, n), F32),
        compiler_params=_cp(("parallel",)),
    )(a, w)


def _mm_nt_rmsbwd(dy, w, x, gamma, dres, *, tm, tk, name, extra=None, comm=None):
    m, kc = dy.shape
    nk = kc // tk
    ni = m // tm
    n_x = 5 if extra is None else 7
    c_args, c_in_specs, c_out_specs, c_out_shapes, c_sems = _comm_specs(comm)
    if w.ndim == 3:
        assert w.shape[0] == nk and w.shape[2] == tk
        d = w.shape[1]
        w_spec = pl.BlockSpec((None, d, tk), lambda i, k: (k, 0, 0))
    else:
        d = w.shape[0]
        w_spec = pl.BlockSpec((d, tk), lambda i, k: (0, k))

    def body(*refs):
        dy_ref, w_ref, x_ref, g_ref, r_ref = refs[:5]
        if extra is not None:
            dy2_ref, w2_ref = refs[5:7]
        n_out = n_x + len(c_args)
        dx_ref, dg_ref = refs[n_out:n_out + 2]
        acc_ref = refs[n_out + 2 + len(c_out_shapes)]
        i, kk = pl.program_id(0), pl.program_id(1)
        if comm is not None:
            _comm_steps(comm, refs, n_x, n_out + 2, (i == 0) & (kk == 0), (i == (3 * ni) // 4) & (kk == 0),
                        (i == ni - 1) & (kk == nk - 1))

        @pl.when(kk == 0)
        def _():
            acc_ref[...] = jnp.zeros_like(acc_ref)

        @pl.when((i == 0) & (kk == 0))
        def _():
            dg_ref[...] = jnp.zeros_like(dg_ref)

        acc_ref[...] += _dot_nt(dy_ref[...], w_ref[...])

        @pl.when(kk == nk - 1)
        def _():
            dxn = acc_ref[...]
            if extra is not None:
                dxn = dxn + _dot_nt(dy2_ref[...], w2_ref[...])
            xv = x_ref[...]
            r = lax.rsqrt(jnp.mean(xv * xv, axis=-1, keepdims=True) + EPS)
            xh = xv * r
            dg_ref[0:1, :] += _rsum(dxn * xh)
            dxh = dxn * g_ref[...]
            dx_ref[...] = r_ref[...] + r * (dxh - xh * jnp.mean(dxh * xh, axis=-1, keepdims=True))

    in_specs = [pl.BlockSpec((tm, tk), lambda i, k: (i, k)), w_spec,
                pl.BlockSpec((tm, d), lambda i, k: (i, 0)), pl.BlockSpec((1, d), lambda i, k: (0, 0)),
                pl.BlockSpec((tm, d), lambda i, k: (i, 0))]
    args = [dy, w, x, gamma, dres]
    if extra is not None:
        k2 = extra[0].shape[1]
        in_specs += [pl.BlockSpec((tm, k2), lambda i, k: (i, 0)), pl.BlockSpec((d, k2), lambda i, k: (0, 0))]
        args += list(extra)
    return pl.pallas_call(
        body, name=name, grid=(ni, nk), in_specs=in_specs + c_in_specs,
        out_specs=[pl.BlockSpec((tm, d), lambda i, k: (i, 0)), pl.BlockSpec((8, d), lambda i, k: (0, 0))] + c_out_specs,
        out_shape=[jax.ShapeDtypeStruct((m, d), F32), jax.ShapeDtypeStruct((8, d), F32)] + c_out_shapes,
        scratch_shapes=[pltpu.VMEM((tm, d), F32)] + c_sems,
        compiler_params=_cp(("arbitrary", "arbitrary")),
    )(*args, *c_args)


def _rsum8(x):
    acc = x[0:8]
    for g in range(1, x.shape[0] // 8):
        acc = acc + x[8 * g:8 * (g + 1)]
    return acc


def _lru_gates(x, prev8, cw_ref, cb_ref, wa_ref, wx_ref, ba_ref, bx_ref, lam_ref):
    u = _conv4(x, prev8, cw_ref, cb_ref)
    ra =_sigmoid(_dot(u, wa_ref[0]) + ba_ref[...])
    ia = _sigmoid(_dot(u, wx_ref[0]) + bx_ref[...])
    sp = _softplus(-lam_ref[...])
    log_a = -LRU_C * ra * sp
    a = jnp.exp(log_a)
    m2 = _one_minus_exp(2.0 * log_a)
    mult = jnp.sqrt(m2)
    return u, ra, ia, sp, a, m2, mult


def _lru_fwd(proj, lw, *, r, name, comm=None):
    t = proj.shape[0]
    nt = t // r
    c_args, c_in_specs, c_out_specs, c_out_shapes, c_sems = _comm_specs(comm)

    def body(*refs):
        xg_ref, xp_ref, cw_ref, cb_ref, wa_ref, wx_ref, ba_ref, bx_ref, lam_ref = refs[:9]
        hl_ref, ya_ref, sv_ref = refs[9 + len(c_args):12 + len(c_args)]
        carry_ref = refs[12 + len(c_args) + len(c_out_shapes)]
        i = pl.program_id(1)
        if comm is not None:
            j = pl.program_id(0)
            _comm_steps(comm, refs, 9, 12 + len(c_args), (j == 0) & (i == 0), (j == 3) & (i == 0), (j == 3) & (i == nt - 1))

        @pl.when(i == 0)
        def _():
            carry_ref[...] = jnp.zeros_like(carry_ref)

        x = xg_ref[:, 0:256]
        lg = xg_ref[:, 256:512]
        prev8 = jnp.where(i == 0, 0.0, xp_ref[:, 0:256])
        u, ra, ia, sp, a, m2, mult = _lru_gates(x, prev8, cw_ref, cb_ref, wa_ref, wx_ref, ba_ref, bx_ref, lam_ref)
        for k, v in enumerate((u, ra, ia, a, mult)):
            sv_ref[k] = v
        ac, hc = _lin_scan(a, mult * ia * u, False)
        h = hc + ac * carry_ref[0:1, :]
        hl_ref[...] = h
        carry_ref[0:1, :] = hl_ref[r - 1:r, :]
        g, _ = _gelu_and_grad(lg)
        ya_ref[...] = (g * h).astype(ya_ref.dtype)

    small = lambda rows: pl.BlockSpec((rows, 256), lambda j, i: (0, j))
    return pl.pallas_call(
        body, name=name, grid=(4, nt),
        in_specs=[pl.BlockSpec((r, LBLK), lambda j, i: (i, _lblk_col(j))),
                  pl.BlockSpec((8, LBLK), lambda j, i: (jnp.maximum(i * (r // 8) - 1, 0), _lblk_col(j))),
                  small(4), small(1),
                  pl.BlockSpec((1, 256, 256), lambda j, i: (j, 0, 0)), pl.BlockSpec((1, 256, 256), lambda j, i: (j, 0, 0)),
                  small(1), small(1), small(1)] + c_in_specs,
        out_specs=[pl.BlockSpec((r, 256), lambda j, i: (i, j)), pl.BlockSpec((r, 256), lambda j, i: (i, j)),
                   pl.BlockSpec((5, r, 256), lambda j, i: (0, i, j))] + c_out_specs,
        out_shape=[jax.ShapeDtypeStruct((t, D), F32), jax.ShapeDtypeStruct((t, D), _MXU),
                   jax.ShapeDtypeStruct((5, t, D), F32)] + c_out_shapes,
        scratch_shapes=[pltpu.VMEM((8, 256), F32)] + c_sems,
        compiler_params=_cp(("arbitrary", "arbitrary") if comm is not None else ("parallel", "arbitrary")),
    )(proj, proj, lw["cw"], lw["cb"], lw["wa"], lw["wx"], lw["ba"], lw["bx"], lw["lam"], *c_args)


def _lru_bwd(proj, hl, gates, dya, dproj, lw, *, r, name, comm=None):
    t = proj.shape[0]
    nt = t // r
    c_args, c_in_specs, c_out_specs, c_out_shapes, c_sems = _comm_specs(comm)
    n_in = 10

    def body(*refs):
        xg_ref, hl_ref, hp_ref, sv_ref, dya_ref, cw_ref, wa_ref, wx_ref, lam_ref = refs[:9]
        n_out = n_in + len(c_args)
        dproj_ref, sm_ref, dwa_ref, dwx_ref = refs[n_out:n_out + 4]
        n_scr = n_out + 4 + len(c_out_shapes)
        carry_ref, du8_ref, row_scr = refs[n_scr:n_scr + 3]
        i = pl.program_id(1)
        if comm is not None:
            j = pl.program_id(0)
            _comm_steps(comm, refs, n_in, n_out + 4, (j == 0) & (i == 0), (j == 3) & (i == 0), (j == 3) & (i == nt - 1))

        @pl.when(i == 0)
        def _():
            carry_ref[...] = jnp.zeros_like(carry_ref)
            du8_ref[...] = jnp.zeros_like(du8_ref)
            sm_ref[...] = jnp.zeros_like(sm_ref)
            dwa_ref[...] = jnp.zeros_like(dwa_ref)
            dwx_ref[...] = jnp.zeros_like(dwx_ref)

        tile0 = i == nt - 1
        xp = xg_ref[:, 0:256]
        lg = xg_ref[:, 256:512]
        u, ra, ia, a, mult = (sv_ref[k] for k in range(5))
        sp = _softplus(-lam_ref[...])
        h = hl_ref[...]
        hprev = _shift_down(h, jnp.where(tile0, 0.0, hp_ref[...]), 1)
        dya_v = dya_ref[...]
        g, dg = _gelu_and_grad(lg)
        ac, lc = _lin_scan(_shift_up(a, carry_ref[...], 1), dya_v * g, True)
        lam_v = lc + ac * carry_ref[1:2, :]
        row_scr[0:8, :] = lam_v[0:8]
        row_scr[8:16, :] = a[0:8]
        carry_ref[1:2, :] = row_scr[0:1, :]
        carry_ref[0:1, :] = row_scr[8:9, :]
        da = lam_v * hprev
        dmult = lam_v * ia * u
        dia = lam_v * mult * u
        dlog = da * a - dmult * (a * a) / mult
        dra = -LRU_C * sp * dlog
        dpa = dra * ra * (1.0 - ra)
        dpx = dia * ia * (1.0 - ia)
        du = lam_v * mult * ia + _dot_nt(dpa, wa_ref[0]) + _dot_nt(dpx, wx_ref[0])
        dwa_ref[0] += _dot_tn(u, dpa)
        dwx_ref[0] += _dot_tn(u, dpx)
        dlx = du * cw_ref[3:4, :]
        sm_ref[24:32, :] += _rsum8(du * xp)
        for k in (1, 2, 3):
            du_k = _shift_up(du, du8_ref[...], k)
            dlx = dlx + du_k * cw_ref[3 - k:4 - k, :]
            sm_ref[8 * (3 - k):8 * (4 - k), :] += _rsum8(du_k * xp)
        du8_ref[...] = du[0:8]
        dproj_ref[:, 0:256] = dlx.astype(dproj_ref.dtype)
        dproj_ref[:, 256:512] = (dya_v * h * dg).astype(dproj_ref.dtype)
        sm_ref[32:40, :] += _rsum8(du)
        sm_ref[40:48, :] += _rsum8(dpa)
        sm_ref[48:56, :] += _rsum8(dpx)
        sm_ref[56:64, :] += _rsum8(-LRU_C * ra * dlog) * (-_sigmoid(-lam_ref[...]))

    rev = lambda i: nt - 1 - i
    small = lambda rows: pl.BlockSpec((rows, 256), lambda j, i: (0, j))
    wblk = pl.BlockSpec((1, 256, 256), lambda j, i: (j, 0, 0))
    return pl.pallas_call(
        body, name=name, grid=(4, nt),
        in_specs=[pl.BlockSpec((r, LBLK), lambda j, i: (rev(i), _lblk_col(j))),
                  pl.BlockSpec((r, 256), lambda j, i: (rev(i), j)),
                  pl.BlockSpec((8, 256), lambda j, i: (jnp.maximum(rev(i) * (r // 8) - 1, 0), j)),
                  pl.BlockSpec((5, r, 256), lambda j, i: (0, rev(i), j)),
                  pl.BlockSpec((r, 256), lambda j, i: (rev(i), j)),
                  small(4), wblk, wblk, small(1),
                  pl.BlockSpec(memory_space=pl.ANY)] + c_in_specs,
        out_specs=[pl.BlockSpec((r, LBLK), lambda j, i: (rev(i), _lblk_col(j))),
                   pl.BlockSpec((64, 256), lambda j, i: (0, j)), wblk, wblk] + c_out_specs,
        out_shape=[jax.ShapeDtypeStruct(dproj.shape, dproj.dtype), jax.ShapeDtypeStruct((64, D), F32),
                   jax.ShapeDtypeStruct((4, 256, 256), F32), jax.ShapeDtypeStruct((4, 256, 256), F32)] + c_out_shapes,
        scratch_shapes=[pltpu.VMEM((8, 256), F32), pltpu.VMEM((8, 256), F32), pltpu.VMEM((16, 256), F32)] + c_sems,
        input_output_aliases={n_in - 1: 0},
        compiler_params=_cp(("arbitrary", "arbitrary") if comm is not None else ("parallel", "arbitrary")),
    )(proj, hl, hl, gates, dya, lw["cw"], lw["wa"], lw["wx"], lw["lam"], dproj, *c_args)


def _head_cols(x):
    lane = lax.broadcasted_iota(jnp.int32, x.shape, 1)
    return [jnp.sum(jnp.where(lane == h, x, 0.0), axis=1, keepdims=True) for h in range(N_HEADS)]


def _compact_heads(blocks):
    lane = lax.broadcasted_iota(jnp.int32, blocks[0].shape, 1)
    lo = lane < HEAD_P
    out = jnp.zeros_like(blocks[0])
    for j, blk in enumerate(blocks):
        s_lo = jnp.sum(jnp.where(lo, blk, 0.0), axis=1, keepdims=True)
        s_hi = jnp.sum(jnp.where(lo, 0.0, blk), axis=1, keepdims=True)
        out = jnp.where(lane == 2 * j, s_lo, out)
        out = jnp.where(lane == 2 * j + 1, s_hi, out)
    return out


def _ssd_prelude(dtraw_ref, dtb_ref, alog_ref, dt_scr, a_scr):
    lane = lax.broadcasted_iota(jnp.int32, dt_scr.shape, 1)
    dt = jnp.where(lane < N_HEADS, _softplus(dtraw_ref[...] + dtb_ref[0:1, :]), 0.0)
    dt_scr[...] = dt
    a_scr[...] = dt * (-jnp.exp(alog_ref[0:1, :]))


def _ssd_chunk_scalars(dt_scr, a_scr, r_scr, r0):
    a_c = a_scr[pl.ds(r0, CHUNK), :]
    dt_c = dt_scr[pl.ds(r0, CHUNK), :]
    i0 = lax.broadcasted_iota(jnp.int32, (CHUNK, CHUNK), 0)
    i1 = lax.broadcasted_iota(jnp.int32, (CHUNK, CHUNK), 1)
    tri = jnp.where(i0 >= i1, 1.0, 0.0).astype(F32)
    cs = jnp.dot(tri, a_c, precision=_HI, preferred_element_type=F32)
    lane = lax.broadcasted_iota(jnp.int32, (CHUNK, 128), 1)
    srow = lax.broadcasted_iota(jnp.int32, (CHUNK, 128), 0)
    t_lo = jnp.where((lane < HEAD_P) & (srow <= lane), 1.0, 0.0).astype(F32)
    t_hi = jnp.where((lane >= HEAD_P) & (srow <= lane - HEAD_P), 1.0, 0.0).astype(F32)
    even = (lane % 2) == 0
    tn = (((0,), (0,)), ((), ()))
    r_scr[...] = (lax.dot_general(jnp.where(even, a_c, 0.0), t_lo, tn, precision=_HI, preferred_element_type=F32)
                  + lax.dot_general(jnp.where(even, 0.0, a_c), t_hi, tn, precision=_HI, preferred_element_type=F32))
    return cs, dt_c, _head_cols(cs), _head_cols(dt_c)


def _block_diag2(v):
    lo = lax.broadcasted_iota(jnp.int32, v.shape, 1) < HEAD_P
    return jnp.concatenate([jnp.where(lo, v, 0.0), jnp.where(lo, 0.0, v)], axis=0).astype(_MXU)


def _ssd_pair(xc_scr, r_scr, cs_cols, dt_cols, s2, r0, j, s2t=None):
    lane = lax.broadcasted_iota(jnp.int32, (CHUNK, 128), 1)
    srow = lax.broadcasted_iota(jnp.int32, (CHUNK, 128), 0)
    lo = lane < HEAD_P
    csc = jnp.where(lo, cs_cols[2 * j], cs_cols[2 * j + 1])
    dtc = jnp.where(lo, dt_cols[2 * j], dt_cols[2 * j + 1])
    csr = r_scr[2 * j:2 * j + 1, :] + r_scr[2 * j + 1:2 * j + 2, :]
    dm = jnp.where((lane & (HEAD_P - 1)) <= srow, jnp.exp(jnp.minimum(csc - csr, 0.0)), 0.0)
    xs = xc_scr[pl.ds(r0, CHUNK), j * 128:(j + 1) * 128]
    xd = xs * dtc
    csl = jnp.sum(jnp.where(srow == CHUNK - 1, csc, 0.0), axis=0, keepdims=True)
    out = dict(csc=csc, dtc=dtc, dm=dm, m2=s2 * dm, xs=xs, xd=xd, rhs=_block_diag2(xd), e=jnp.exp(csc),
               w=jnp.exp(csl - csc), dec=jnp.exp(csl))
    if s2t is not None:
        out["mt2"] = s2t * jnp.where((lane & (HEAD_P - 1)) >= srow, jnp.exp(jnp.minimum(csr - csc, 0.0)), 0.0)
    return out


def _cat(parts):
    return jnp.concatenate(parts, axis=1)


def _ssd_fwd(proj, dtraw, sw, *, rb, name, comm=None):
    t = proj.shape[0]
    ns, cb = t // rb, rb // CHUNK
    c_args, c_in_specs, c_out_specs, c_out_shapes, c_sems = _comm_specs(comm)

    def body(*refs):
        zx_ref, zp_ref, dtraw_ref, cw_ref, cbias_ref, dtb_ref, alog_ref, dsk_ref, ng_ref = refs[:9]
        yssd_ref, yb_ref, st_ref, xc_scr, dsl_ref = refs[9 + len(c_args):14 + len(c_args)]
        n_scr = 14 + len(c_args) + len(c_out_shapes)
        h_scr, dt_scr, a_scr, r_scr = refs[n_scr:n_scr + 4]
        i = pl.program_id(0)
        if comm is not None:
            _comm_steps(comm, refs, 9, 14 + len(c_args), i == 0, i == (3 * ns) // 4, i == ns - 1)

        @pl.when(i == 0)
        def _():
            h_scr[...] = jnp.zeros_like(h_scr)

        for j in range(XBC // 128):
            cs_, zc = slice(128 * j, 128 * (j + 1)), slice(2048 + 128 * j, 2048 + 128 * (j + 1))
            pre = _conv4(zx_ref[:, zc], jnp.where(i == 0, 0.0, zp_ref[:, zc]), cw_ref, cbias_ref, cs_)
            sg = _sigmoid(pre)
            xc_scr[:, cs_] = pre * sg
            dsl_ref[:, cs_] = sg * (1.0 + pre * (1.0 - sg))
        _ssd_prelude(dtraw_ref, dtb_ref, alog_ref, dt_scr, a_scr)

        def chunk(c, carry):
            r0 = pl.multiple_of(c * CHUNK, CHUNK)
            _, _, cs_cols, dt_cols = _ssd_chunk_scalars(dt_scr, a_scr, r_scr, r0)
            st_ref[c] = h_scr[...]
            for g in range(N_GROUPS):
                bg = xc_scr[pl.ds(r0, CHUNK), 2048 + 128 * g:2048 + 128 * (g + 1)]
                cg = xc_scr[pl.ds(r0, CHUNK), 2560 + 128 * g:2560 + 128 * (g + 1)]
                s2 = _dot_nt(cg, jnp.concatenate([bg, bg], axis=0))
                hp = h_scr[:, 512 * g:512 * (g + 1)]
                yoff = _dot(cg, hp)
                xdw, dec = [], []
                for jj in range(4):
                    j = 4 * g + jj
                    p = _ssd_pair(xc_scr, r_scr, cs_cols, dt_cols, s2, r0, j)
                    y = _dot(p["m2"], p["rhs"]) + yoff[:, 128 * jj:128 * (jj + 1)] * p["e"]
                    yssd_ref[pl.ds(r0, CHUNK), 128 * j:128 * (j + 1)] = y + dsk_ref[0:1, 128 * j:128 * (j + 1)] * p["xs"]
                    xdw.append(p["xd"] * p["w"])
                    dec.append(p["dec"])
                h_scr[:, 512 * g:512 * (g + 1)] = hp * _cat(dec) + _dot_tn(bg, _cat(xdw))
            return carry

        lax.fori_loop(0, cb, chunk, 0)
        for g in range(N_GROUPS):
            sl = slice(512 * g, 512 * (g + 1))
            for q in range(rb // NORM_ROWS):
                rw = slice(NORM_ROWS * q, NORM_ROWS * (q + 1))
                yz = yssd_ref[rw, sl] * _silu(zx_ref[rw, sl])
                rg = lax.rsqrt(jnp.mean(yz * yz, axis=-1, keepdims=True) + EPS)
                yb_ref[rw, sl] = (yz * rg * ng_ref[0:1, sl]).astype(yb_ref.dtype)

    full = lambda rows, cols: pl.BlockSpec((rows, cols), lambda i: (0, 0))
    return pl.pallas_call(
        body, name=name, grid=(ns,),
        in_specs=[pl.BlockSpec((rb, ZX_W), lambda i: (i, 0)),
                  pl.BlockSpec((8, ZX_W), lambda i: (jnp.maximum(i * (rb // 8) - 1, 0), 0)),
                  pl.BlockSpec((rb, DT_PAD), lambda i: (i, 0)),
                  full(4, XBC), full(1, XBC), full(1, DT_PAD), full(1, DT_PAD), full(1, SSD_INNER), full(1, SSD_INNER)]
        + c_in_specs,
        out_specs=[pl.BlockSpec((rb, SSD_INNER), lambda i: (i, 0)), pl.BlockSpec((rb, SSD_INNER), lambda i: (i, 0)),
                   pl.BlockSpec((cb, N_STATE, SSD_INNER), lambda i: (i, 0, 0)),
                   pl.BlockSpec((rb, XBC), lambda i: (i, 0)), pl.BlockSpec((rb, XBC), lambda i: (i, 0))] + c_out_specs,
        out_shape=[jax.ShapeDtypeStruct((t, SSD_INNER), F32), jax.ShapeDtypeStruct((t, SSD_INNER), _MXU),
                   jax.ShapeDtypeStruct((t // CHUNK, N_STATE, SSD_INNER), F32),
                   jax.ShapeDtypeStruct((t, XBC), F32), jax.ShapeDtypeStruct((t, XBC), F32)] + c_out_shapes,
        scratch_shapes=[pltpu.VMEM((N_STATE, SSD_INNER), F32), pltpu.VMEM((rb, DT_PAD), F32),
                        pltpu.VMEM((rb, DT_PAD), F32), pltpu.VMEM((128, 128), F32)] + c_sems,
        compiler_params=_cp(("arbitrary",)),
    )(proj, proj, dtraw, sw["cw"], sw["cb"], sw["dtb"], sw["alog"], sw["dsk"], sw["ng"], *c_args)


def _ssd_bwd(proj, dtraw, yssd, states, xc, dsl, dyb, dproj, sw, *, rb, name, comm=None):
    t = proj.shape[0]
    ns, cb = t // rb, rb // CHUNK
    c_args, c_in_specs, c_out_specs, c_out_shapes, c_sems = _comm_specs(comm)
    n_in = 13

    def body(*refs):
        zx_ref, dtraw_ref, yssd_ref, st_ref, xc_scr, dsl_scr, dyb_ref, cw_ref, dtb_ref, alog_ref, dsk_ref, ng_ref = refs[:12]
        n_out = n_in + len(c_args)
        dzx_ref, ddt_ref, gconv_ref, gch_ref, ghd_ref = refs[n_out:n_out + 5]
        n_scr = n_out + 5 + len(c_out_shapes)
        dht_scr, dy_scr, dxc_scr, dt_scr, a_scr, r_scr, dp8_scr = refs[n_scr:n_scr + 7]
        i = pl.program_id(0)
        if comm is not None:
            _comm_steps(comm, refs, n_in, n_out + 5, i == 0, i == (3 * ns) // 4, i == ns - 1)

        @pl.when(i == 0)
        def _():
            dht_scr[...] = jnp.zeros_like(dht_scr)
            dp8_scr[...] = jnp.zeros_like(dp8_scr)
            gconv_ref[...] = jnp.zeros_like(gconv_ref)
            gch_ref[...] = jnp.zeros_like(gch_ref)
            ghd_ref[...] = jnp.zeros_like(ghd_ref)

        _ssd_prelude(dtraw_ref, dtb_ref, alog_ref, dt_scr, a_scr)

        for g in range(N_GROUPS):
            sl = slice(512 * g, 512 * (g + 1))
            for q in range(rb // NORM_ROWS):
                rw = slice(NORM_ROWS * q, NORM_ROWS * (q + 1))
                zv = zx_ref[rw, sl]
                ys = yssd_ref[rw, sl]
                sg = _sigmoid(zv)
                sz = zv * sg
                yz = ys * sz
                rg = lax.rsqrt(jnp.mean(yz * yz, axis=-1, keepdims=True) + EPS)
                yn = yz * rg
                dyb_v = dyb_ref[rw, sl]
                gch_ref[0:8, sl] += _rsum8(dyb_v * yn)
                dyn = dyb_v * ng_ref[0:1, sl]
                dyz = rg * (dyn - yn * jnp.mean(dyn * yn, axis=-1, keepdims=True))
                dy_scr[rw, sl] = dyz * sz
                dzx_ref[rw, sl] = (dyz * ys * (sg * (1.0 + zv * (1.0 - sg)))).astype(dzx_ref.dtype)

        a_row = -jnp.exp(alog_ref[0:1, :])

        def chunk(cc, carry):
            c = cb - 1 - cc
            r0 = pl.multiple_of(c * CHUNK, CHUNK)
            rows = pl.ds(r0, CHUNK)
            _, dt_c, cs_cols, dt_cols = _ssd_chunk_scalars(dt_scr, a_scr, r_scr, r0)
            lane = lax.broadcasted_iota(jnp.int32, (CHUNK, 128), 1)
            srow = lax.broadcasted_iota(jnp.int32, (CHUNK, 128), 0)
            lo = lane < HEAD_P
            last = srow == CHUNK - 1
            p1_blocks, p3_blocks = [], []
            for g in range(N_GROUPS):
                gs = slice(512 * g, 512 * (g + 1))
                bg = xc_scr[rows, 2048 + 128 * g:2048 + 128 * (g + 1)]
                cg = xc_scr[rows, 2560 + 128 * g:2560 + 128 * (g + 1)]
                b2 = jnp.concatenate([bg, bg], axis=0)
                s2 = _dot_nt(cg, b2)
                s2t = _dot_nt(bg, jnp.concatenate([cg, cg], axis=0))
                hp = st_ref[c, :, gs]
                dht = dht_scr[:, gs]
                yoff = _dot(cg, hp)
                ps = [_ssd_pair(xc_scr, r_scr, cs_cols, dt_cols, s2, r0, 4 * g + jj, s2t) for jj in range(4)]
                dys = [dy_scr[rows, 128 * (4 * g + jj):128 * (4 * g + jj + 1)] for jj in range(4)]
                dye = _cat([dys[jj] * ps[jj]["e"] for jj in range(4)])
                w_g = _cat([p["w"] for p in ps])
                dcg = _dot_nt(dye, hp)
                dht_scr[:, gs] = _dot_tn(cg, dye) + _cat([p["dec"] for p in ps]) * dht
                dxd_state = w_g * _dot(bg, dht)
                dbg = _dot_nt(_cat([p["xd"] for p in ps]) * w_g, dht)
                tsum = _rsum(dht * hp)
                ds2 = jnp.zeros((CHUNK, 128), F32)
                for jj in range(4):
                    j = 4 * g + jj
                    ls = slice(128 * j, 128 * (j + 1))
                    p, dy2 = ps[jj], dys[jj]
                    dy_bd = _block_diag2(dy2)
                    dm2 = _dot_nt(dy2, p["rhs"])
                    ds2 = ds2 + dm2 * p["dm"]
                    gdiff = dm2 * p["m2"] - _dot_nt(p["xd"], dy_bd) * p["mt2"]
                    dxs = dxd_state[:, 128 * jj:128 * (jj + 1)]
                    dxd = _dot(p["mt2"], dy_bd) + dxs
                    end_row = _rsum(p["xd"] * dxs) + p["dec"] * tsum[:, 128 * jj:128 * (jj + 1)]
                    p1_blocks.append(gdiff + dy2 * yoff[:, 128 * jj:128 * (jj + 1)] * p["e"] - p["xd"] * dxs
                                     + jnp.where(last, end_row, 0.0))
                    p3_blocks.append(dxd * p["xs"])
                    dxc_scr[rows, ls] = dxd * p["dtc"] + dy2 * dsk_ref[0:1, ls]
                    gch_ref[8:16, ls] += _rsum8(dy2 * p["xs"])
                dcg = dcg + _dot(ds2, b2)
                rb2 = _dot_tn(ds2, cg)
                dxc_scr[rows, 2048 + 128 * g:2048 + 128 * (g + 1)] = dbg + rb2[0:CHUNK] + rb2[CHUNK:2 * CHUNK]
                dxc_scr[rows, 2560 + 128 * g:2560 + 128 * (g + 1)] = dcg
            dcs = _compact_heads(p1_blocks)
            i0 = lax.broadcasted_iota(jnp.int32, (CHUNK, CHUNK), 0)
            i1 = lax.broadcasted_iota(jnp.int32, (CHUNK, CHUNK), 1)
            triu = jnp.where(i1 >= i0, 1.0, 0.0).astype(F32)
            da = jnp.dot(triu, dcs, precision=_HI, preferred_element_type=F32)
            ddt = _compact_heads(p3_blocks) + da * a_row
            ddtraw = jnp.where(lane < N_HEADS, ddt * _sigmoid(dtraw_ref[rows, :] + dtb_ref[0:1, :]), 0.0)
            ddt_ref[rows, :] = ddtraw.astype(ddt_ref.dtype)
            ghd_ref[0:1, :] += _rsum(ddtraw)
            ghd_ref[1:2, :] += _rsum(da * dt_c) * a_row
            return carry

        lax.fori_loop(0, cb, chunk, 0)
        for j in range(XBC // 128):
            cs_, zc = slice(128 * j, 128 * (j + 1)), slice(2048 + 128 * j, 2048 + 128 * (j + 1))
            dpre = dxc_scr[:, cs_] * dsl_scr[:, cs_]
            xraw = zx_ref[:, zc]
            dx = dpre * cw_ref[3:4, cs_]
            gconv_ref[24:32, cs_] += _rsum8(dpre * xraw)
            for k in (1, 2, 3):
                dpre_k = _shift_up(dpre, dp8_scr[:, cs_], k)
                dx = dx + dpre_k * cw_ref[3 - k:4 - k, cs_]
                gconv_ref[8 * (3 - k):8 * (4 - k), cs_] += _rsum8(dpre_k * xraw)
            dzx_ref[:, zc] = dx.astype(dzx_ref.dtype)
            dp8_scr[:, cs_] = dpre[0:8]
            gconv_ref[32:40, cs_] += _rsum8(dpre)

    rev = lambda i: ns - 1 - i
    full = lambda rows, cols: pl.BlockSpec((rows, cols), lambda i: (0, 0))
    return pl.pallas_call(
        body, name=name, grid=(ns,),
        in_specs=[pl.BlockSpec((rb, ZX_W), lambda i: (rev(i), 0)),
                  pl.BlockSpec((rb, DT_PAD), lambda i: (rev(i), 0)),
                  pl.BlockSpec((rb, SSD_INNER), lambda i: (rev(i), 0)),
                  pl.BlockSpec((cb, N_STATE, SSD_INNER), lambda i: (rev(i), 0, 0)),
                  pl.BlockSpec((rb, XBC), lambda i: (rev(i), 0)), pl.BlockSpec((rb, XBC), lambda i: (rev(i), 0)),
                  pl.BlockSpec((rb, SSD_INNER), lambda i: (rev(i), 0)),
                  full(4, XBC), full(1, DT_PAD), full(1, DT_PAD), full(1, SSD_INNER), full(1, SSD_INNER),
                  pl.BlockSpec(memory_space=pl.ANY)] + c_in_specs,
        out_specs=[pl.BlockSpec((rb, ZX_W), lambda i: (rev(i), 0)), pl.BlockSpec((rb, DT_PAD), lambda i: (rev(i), 0)),
                   full(40, XBC), full(16, SSD_INNER), full(8, DT_PAD)] + c_out_specs,
        out_shape=[jax.ShapeDtypeStruct(dproj.shape, dproj.dtype), jax.ShapeDtypeStruct((t, DT_PAD), _MXU),
                   jax.ShapeDtypeStruct((40, XBC), F32), jax.ShapeDtypeStruct((16, SSD_INNER), F32),
                   jax.ShapeDtypeStruct((8, DT_PAD), F32)] + c_out_shapes,
        scratch_shapes=[pltpu.VMEM((N_STATE, SSD_INNER), F32),
                        pltpu.VMEM((rb, SSD_INNER), F32), pltpu.VMEM((rb, XBC), F32), pltpu.VMEM((rb, DT_PAD), F32),
                        pltpu.VMEM((rb, DT_PAD), F32), pltpu.VMEM((128, 128), F32), pltpu.VMEM((8, XBC), F32)] + c_sems,
        input_output_aliases={n_in - 1: 0},
        compiler_params=_cp(("arbitrary",)),
    )(proj, dtraw, yssd, states, xc, dsl, dyb, sw["cw"], sw["dtb"], sw["alog"], sw["dsk"], sw["ng"], dproj, *c_args)


def _branch_merge(ya, yb, proj, wba, wbb, bgate, *, tm, tn, name):
    t = ya.shape[0]
    nj = D // tn

    def body(ya_ref, yb_ref, ga_ref, gb_ref, wba_ref, wbb_ref, ba_ref, bb_ref, ta_ref, tb_ref, mg_ref):
        ta = _dot(ya_ref[...], wba_ref[...])
        tb = _dot(yb_ref[...], wbb_ref[...])
        ta_ref[...] = ta.astype(ta_ref.dtype)
        tb_ref[...] = tb.astype(tb_ref.dtype)
        ga = _sigmoid(ga_ref[...] + ba_ref[...])
        gb = _sigmoid(gb_ref[...] + bb_ref[...])
        mg_ref[...] = (ga * ta + gb * tb).astype(mg_ref.dtype)

    tile = pl.BlockSpec((tm, tn), lambda i, j: (i, j))
    return pl.pallas_call(
        body, name=name, grid=(t // tm, nj),
        in_specs=[pl.BlockSpec((tm, D), lambda i, j: (i, 0)), pl.BlockSpec((tm, SSD_INNER), lambda i, j: (i, 0)),
                  pl.BlockSpec((tm, tn), lambda i, j: (i, G0 // tn + j)),
                  pl.BlockSpec((tm, tn), lambda i, j: (i, (G0 + D) // tn + j)),
                  pl.BlockSpec((D, tn), lambda i, j: (0, j)), pl.BlockSpec((SSD_INNER, tn), lambda i, j: (0, j)),
                  pl.BlockSpec((1, tn), lambda i, j: (0, j)), pl.BlockSpec((1, tn), lambda i, j: (0, nj + j))],
        out_specs=[tile, tile, tile],
        out_shape=[jax.ShapeDtypeStruct((t, D), _MXU)] * 3,
        compiler_params=_cp(("parallel", "parallel")),
    )(ya, yb, proj, proj, wba, wbb, bgate, bgate)


def _swiglu_mm(gu, wfo, residual, *, tm, tn, name):
    t = gu.shape[0]

    def body(gu_ref, w_ref, r_ref, act_ref, o_ref):
        @pl.when(pl.program_id(1) == 0)
        def _():
            gate = gu_ref[:, 0:D_FF].astype(F32)
            act_ref[...] = (_silu(gate) * gu_ref[:, D_FF:2 * D_FF].astype(F32)).astype(act_ref.dtype)
        o_ref[...] = jnp.dot(act_ref[...], w_ref[...], preferred_element_type=F32) + r_ref[...]

    return pl.pallas_call(
        body, name=name, grid=(t // tm, D // tn),
        in_specs=[pl.BlockSpec((tm, 2 * D_FF), lambda i, j: (i, 0)), pl.BlockSpec((D_FF, tn), lambda i, j: (0, j)),
                  pl.BlockSpec((tm, tn), lambda i, j: (i, j))],
        out_specs=[pl.BlockSpec((tm, D_FF), lambda i, j: (i, 0)), pl.BlockSpec((tm, tn), lambda i, j: (i, j))],
        out_shape=[jax.ShapeDtypeStruct((t, D_FF), _MXU), jax.ShapeDtypeStruct((t, D), F32)],
        compiler_params=_cp(("parallel", "arbitrary")),
    )(gu, wfo, residual)


def _ffn_bwd_act(dh, wfo, gu, *, tm, name):
    t = dh.shape[0]

    def body(dh_ref, w_ref, gu_ref, o_ref):
        dact = _dot_nt(dh_ref[...], w_ref[...])
        g = gu_ref[:, 0:D_FF].astype(F32)
        u = gu_ref[:, D_FF:2 * D_FF].astype(F32)
        sg = _sigmoid(g)
        o_ref[:, 0:D_FF] = (dact * u * (sg * (1.0 + g * (1.0 - sg)))).astype(o_ref.dtype)
        o_ref[:, D_FF:2 * D_FF] = (dact * (g * sg)).astype(o_ref.dtype)

    return pl.pallas_call(
        body, name=name, grid=(t // tm,),
        in_specs=[pl.BlockSpec((tm, D), lambda i: (i, 0)), pl.BlockSpec((D_FF, D), lambda i: (0, 0)),
                  pl.BlockSpec((tm, 2 * D_FF), lambda i: (i, 0))],
        out_specs=pl.BlockSpec((tm, 2 * D_FF), lambda i: (i, 0)),
        out_shape=jax.ShapeDtypeStruct((t, 2 * D_FF), _MXU),
        compiler_params=_cp(("parallel",)),
    )(dh, wfo, gu)


def _outproj_bwd(dh, wout, ta, tb, proj, bgate, dproj, *, tm, name):
    t = dh.shape[0]

    def body(dh_ref, w_ref, ta_ref, tb_ref, g_ref, b_ref, dta_ref, dtb_ref, dg_ref, db_ref):
        @pl.when(pl.program_id(0) == 0)
        def _():
            db_ref[...] = jnp.zeros_like(db_ref)
        dm = _dot_nt(dh_ref[...], w_ref[...])
        ga = _sigmoid(g_ref[:, 0:D] + b_ref[:, 0:D])
        gb = _sigmoid(g_ref[:, D:2 * D] + b_ref[:, D:2 * D])
        dta_ref[...] = (dm * ga).astype(dta_ref.dtype)
        dtb_ref[...] = (dm * gb).astype(dtb_ref.dtype)
        dga = dm * ta_ref[...].astype(F32) * ga * (1.0 - ga)
        dgb = dm * tb_ref[...].astype(F32) * gb * (1.0 - gb)
        dg_ref[:, 0:D] = dga.astype(dg_ref.dtype)
        dg_ref[:, D:2 * D] = dgb.astype(dg_ref.dtype)
        db_ref[0:1, 0:D] += _rsum(dga)
        db_ref[0:1, D:2 * D] += _rsum(dgb)

    row = lambda cols: pl.BlockSpec((tm, cols), lambda i: (i, 0))
    return pl.pallas_call(
        body, name=name, grid=(t // tm,),
        in_specs=[row(D), pl.BlockSpec((D, D), lambda i: (0, 0)), row(D), row(D),
                  pl.BlockSpec((tm, 2 * D), lambda i: (i, G0 // (2 * D))), pl.BlockSpec((1, 2 * D), lambda i: (0, 0))],
        out_specs=[row(D), row(D), pl.BlockSpec((tm, 2 * D), lambda i: (i, G0 // (2 * D))),
                   pl.BlockSpec((8, 2 * D), lambda i: (0, 0))],
        out_shape=[jax.ShapeDtypeStruct((t, D), _MXU), jax.ShapeDtypeStruct((t, D), _MXU),
                   jax.ShapeDtypeStruct(dproj, _MXU), jax.ShapeDtypeStruct((8, 2 * D), F32)],
        compiler_params=_cp(("arbitrary",)),
    )(dh, wout, ta, tb, proj, bgate)


def _loss_head(h, gf, target, *, tm, name):
    t = h.shape[0]

    def body(h_ref, g_ref, t_ref, loss_ref, dg_ref, dh_ref):
        @pl.when(pl.program_id(0) == 0)
        def _():
            loss_ref[...] = jnp.zeros_like(loss_ref)
            dg_ref[...] = jnp.zeros_like(dg_ref)
        x = h_ref[...]
        r = lax.rsqrt(jnp.mean(x * x, axis=-1, keepdims=True) + EPS)
        xh = x * r
        err = xh * g_ref[...] - t_ref[...]
        loss_ref[...] += 0.5 * jnp.sum(jnp.mean(err * err, axis=-1, keepdims=True), axis=0, keepdims=True)
        dy = err * (1.0 / D)
        dg_ref[0:1, :] += _rsum(dy * xh)
        dxh = dy * g_ref[...]
        dh_ref[...] = r * (dxh - xh * jnp.mean(dxh * xh, axis=-1, keepdims=True))

    row = pl.BlockSpec((tm, D), lambda i: (i, 0))
    return pl.pallas_call(
        body, name=name, grid=(t // tm,),
        in_specs=[row, pl.BlockSpec((1, D), lambda i: (0, 0)), row],
        out_specs=[pl.BlockSpec((8, 128), lambda i: (0, 0)), pl.BlockSpec((8, D), lambda i: (0, 0)), row],
        out_shape=[jax.ShapeDtypeStruct((8, 128), F32), jax.ShapeDtypeStruct((8, D), F32), jax.ShapeDtypeStruct((t, D), F32)],
        compiler_params=_cp(("arbitrary",)),
    )(h, gf, target)


def _row_tile(rows, cols, limit_bytes=1 << 20):
    best = None
    for tr in range(8, rows + 1, 8):
        if rows % tr == 0 and tr * cols * 4 <= limit_bytes:
            best = tr
    return best if best is not None else rows


def _adamw(w, g, m, v, *, name):
    rows, cols = w.shape
    tr = _row_tile(rows, cols)

    def body(w_ref, g_ref, m_ref, v_ref, d_ref, nm_ref, nv_ref):
        gv = g_ref[...]
        nm = ADAM_B1 * m_ref[...] + (1.0 - ADAM_B1) * gv
        nv = ADAM_B2 * v_ref[...] + (1.0 - ADAM_B2) * (gv * gv)
        m_hat = nm / (1.0 - ADAM_B1 ** ADAM_STEP)
        v_hat = nv / (1.0 - ADAM_B2 ** ADAM_STEP)
        d_ref[...] = -ADAM_LR * (m_hat / (jnp.sqrt(v_hat) + ADAM_EPS) + ADAM_WD * w_ref[...])
        nm_ref[...] = nm
        nv_ref[...] = nv

    blk = pl.BlockSpec((tr, cols), lambda i: (i, 0))
    shp = jax.ShapeDtypeStruct((rows, cols), F32)
    return pl.pallas_call(
        body, name=name, grid=(rows // tr,), in_specs=[blk] * 4, out_specs=[blk] * 3, out_shape=[shp] * 3,
        compiler_params=_cp(("parallel",)),
    )(w, g, m, v)


def _bd256(w):
    w4 = w.reshape(4, 4, 64, 64)
    eye = jnp.eye(4, dtype=w.dtype)
    return (w4[:, :, :, None, :] * eye[None, :, None, :, None]).reshape(4, 256, 256)


def _bd256_diag(g):
    g5 = g.reshape(4, 4, 64, 4, 64)
    return jnp.stack([g5[:, a, :, a, :] for a in range(4)], axis=1).reshape(16, 64, 64)


FFN_SHARD = 2 * D_FF // 4
W_IN_SHARD = IN_DIM // 4
W_IN_ROWS = 9344


def _w_in_cols(shards, c0, c1):
    out = []
    for p in range(4):
        lo, hi = max(c0, W_IN_SHARD * p), min(c1, W_IN_SHARD * (p + 1))
        if lo < hi:
            out.append(shards[p][:, lo - W_IN_SHARD * p:hi - W_IN_SHARD * p])
    return out


def _in_proj_weights(win):
    lblk = [_w_in_cols(win, 256 * j, 256 * (j + 1)) + _w_in_cols(win, D + 256 * j, D + 256 * (j + 1)) for j in range(4)]
    wp = jnp.concatenate(_w_in_cols(win, 2048, 4096) + _w_in_cols(win, 4096, 7168) + lblk[0] + lblk[1]
                         + _w_in_cols(win, 7200, 9248) + lblk[2] + lblk[3], axis=1)
    wdt = jnp.pad(jnp.concatenate(_w_in_cols(win, 7168, 7200), axis=1), ((0, 0), (0, DT_PAD - N_HEADS)))
    return wp, wdt


def _layer_weights(w, conv, small, l, wp, wdt):
    row = lambda v: v.reshape(1, -1)
    pad_h = lambda v: jnp.pad(v.reshape(1, -1), ((0, 0), (0, DT_PAD - N_HEADS)))
    lw = dict(cw=conv["lru_conv_w"][l], cb=row(small["lru_conv_b"][l]),
              wa=_bd256(small["lru_w_a"][l]).astype(_MXU), wx=_bd256(small["lru_w_x"][l]).astype(_MXU),
              ba=row(small["lru_b_a"][l]), bx=row(small["lru_b_x"][l]), lam=row(small["lru_lambda"][l]))
    sw = dict(cw=conv["ssd_conv_w"][l], cb=row(small["ssd_conv_b"][l]), dtb=pad_h(small["ssd_dt_bias"][l]),
              alog=pad_h(small["ssd_A_log"][l]), dsk=row(jnp.repeat(small["ssd_D"][l], HEAD_P)),
              ng=row(small["ssd_norm_g"][l]))
    return dict(wp=wp, wdt=wdt, lw=lw, sw=sw, wba=w["w_branch"][0:D], wbb=w["w_branch"][D:3 * D],
                wout=w["w_out"], wfi=w["w_ffn_in"], wfo=w["w_ffn_out"],
                g1=row(small["norm1_g"][l]), g2=row(small["norm2_g"][l]), bgate=row(small["b_gate"][l]))


def _tiles(t):
    return dict(tmi=min(2048, t), tmn=min(1024, t), tm=min(512, t), tm2=min(256, t), r=min(256, t), rb=min(128, t))


def _layer_fwd(h, w, conv, small, l, carried=None):
    tl = _tiles(h.shape[0])
    n = f"l{l}_"
    carried = carried or {}
    arrived = []

    def carry(kernel, key, n_main, *args, **kw):
        comm, finish = carried.get(key, (None, None))
        outs = list(kernel(*args, comm=comm, **kw))
        if comm is not None:
            arrived.append(finish(outs[n_main:]))
        return outs[:n_main]

    wp, wdt = _in_proj_weights(w["w_in"])
    xn, proj = carry(_norm_mm, "in_proj", 2, h, small["norm1_g"][l].reshape(1, -1), wp, tm=tl["tmi"], tn=1024,
                     name=n + "in_proj")
    w = dict(w)
    for layer, ws in arrived:
        if layer == l:
            w.update(ws)
    lwt = _layer_weights(w, conv, small, l, wp, wdt)
    dtraw = _mm_nn(xn, lwt["wdt"], tm=tl["tm"], tn=DT_PAD, name=n + "dt_proj")
    hl, ya, gates = carry(_lru_fwd, "lru", 3, proj, lwt["lw"], r=tl["r"], name=n + "lru_fwd")
    yssd, yb, states, xc, dsl = carry(_ssd_fwd, "ssd", 5, proj, dtraw, lwt["sw"], rb=tl["rb"], name=n + "ssd_fwd")
    ta, tb, merged = _branch_merge(ya, yb, proj, lwt["wba"], lwt["wbb"], lwt["bgate"], tm=tl["tm"], tn=512, name=n + "merge")
    hmid = _mm_nn(merged, lwt["wout"], tm=tl["tm"], tn=512, name=n + "out_proj", residual=h)
    xn2, gu = _norm_mm(hmid, lwt["g2"], lwt["wfi"], tm=tl["tmn"], tn=FFN_SHARD, name=n + "ffn_in", out_dtype=_MXU)
    act, hout = _swiglu_mm(gu, lwt["wfo"], hmid, tm=tl["tm"], tn=512, name=n + "ffn_out")
    saved = dict(h=h, xn=xn, proj=proj, dtraw=dtraw, hl=hl, ya=ya, gates=gates, yssd=yssd, yb=yb, states=states, xc=xc, dsl=dsl, ta=ta, tb=tb,
                 merged=merged, hmid=hmid, xn2=xn2, gu=gu, act=act)
    return hout, saved, lwt, [x for x in arrived if x[0] != l]


def _layer_bwd(dh, s, lwt, l, hooks=None):
    t = dh.shape[0]
    tl = _tiles(t)
    n = f"l{l}_"
    tt = tl["tmn"]
    big = {}
    hooks = hooks or {}

    def wgrad(key, a, b, name, **kw):
        big[key] = _wgrad(a, b, tt=tt, name=n + name, into=big.get(key), **kw)

    dgu = _ffn_bwd_act(dh, lwt["wfo"], s["gu"], tm=tl["tm2"], name=n + "ffn_act_bwd")
    wgrad("w_ffn_out", s["act"], dh, "ffn_out_wgrad", ta=D_FF, tn=1024, out_shape=(D_FF, D),
          out_block=(D_FF, 1024), out_index=lambda o, j: (o, j))
    wgrad("w_ffn_in", s["xn2"], dgu, "ffn_in_wgrad", ta=D, tn=FFN_SHARD, out_shape=(4, D, FFN_SHARD),
          out_block=(None, D, FFN_SHARD), out_index=lambda o, j: (j, o, 0))
    dh1, dg2 = _mm_nt_rmsbwd(dgu, lwt["wfi"], s["hmid"], lwt["g2"], dh, tm=tl["tm"], tk=FFN_SHARD, name=n + "ffn_in_dgrad")
    dta, dtb, dproj, dbg = _outproj_bwd(dh1, lwt["wout"], s["ta"], s["tb"], s["proj"], lwt["bgate"], (t, NP),
                                        tm=tl["tm2"], name=n + "out_proj_bwd")
    rows_d = dict(ta=D, tn=512, out_block=(D, 512), out_index=lambda o, j: (o, j))
    wgrad("w_out", s["merged"], dh1, "out_proj_wgrad", out_shape=(D, D), **rows_d)
    dya = _mm_nt(dta, lwt["wba"], tm=tl["tm"], name=n + "branch_a_dgrad")
    dyb = _mm_nt(dtb, lwt["wbb"], tm=tl["tm"], name=n + "branch_b_dgrad")
    wgrad("w_branch", s["ya"], dta, "branch_a_wgrad", out_shape=(3 * D, D), a_tab=[0], o_tab=[0], **rows_d)
    wgrad("w_branch", s["yb"], dtb, "branch_b_wgrad", out_shape=(3 * D, D), a_tab=[0, 1], o_tab=[1, 2], **rows_d)
    comm_1 = hooks["lru"](big) if "lru" in hooks else None
    dproj, lsm, dwa, dwx, *got_1 = _lru_bwd(s["proj"], s["hl"], s["gates"], dya, dproj, lwt["lw"], r=tl["r"], name=n + "lru_bwd",
                                            comm=comm_1)
    comm_2 = hooks["ssd"](got_1) if "ssd" in hooks else None
    dproj, ddt, gconv, gch, ghd, *got_2 = _ssd_bwd(s["proj"], s["dtraw"], s["yssd"], s["states"], s["xc"], s["dsl"], dyb, dproj, lwt["sw"],
                                                   rb=tl["rb"], name=n + "ssd_bwd", comm=comm_2)
    lsm = lsm.reshape(8, 8, D).sum(axis=1)
    gconv = gconv.reshape(5, 8, XBC).sum(axis=1)
    gch = gch.reshape(2, 8, SSD_INNER).sum(axis=1)
    w_in = dict(tn=D, out_shape=(W_IN_ROWS, D), out_index=lambda o, j: (o, j))
    wgrad("w_in", dproj, s["xn"], "in_proj_wgrad", ta=1024, out_block=(1024, D),
          a_tab=list(range(9)), o_tab=[2, 3, 4, 5, 6, 0, 7, 8, 1], **w_in)
    wgrad("w_in", ddt, s["xn"], "dt_proj_wgrad", ta=DT_PAD, out_block=(DT_PAD, D), a_tab=[0],
          o_tab=[NP // DT_PAD], **w_in)
    comm_3 = hooks["in_dgrad"](big) if "in_dgrad" in hooks else None
    dh0, dg1, *got_3 = _mm_nt_rmsbwd(dproj, lwt["wp"], s["h"], lwt["g1"], dh1, tm=tl["tm"], tk=4608,
                                     name=n + "in_proj_dgrad", extra=(ddt, lwt["wdt"]), comm=comm_3)
    grads = dict(
        lru_conv_w=lsm[0:4], lru_conv_b=lsm[4], lru_b_a=lsm[5], lru_b_x=lsm[6], lru_lambda=lsm[7],
        lru_w_a=_bd256_diag(dwa), lru_w_x=_bd256_diag(dwx),
        ssd_conv_w=gconv[0:4], ssd_conv_b=gconv[4], ssd_norm_g=gch[0], ssd_D=gch[1].reshape(N_HEADS, HEAD_P).sum(axis=-1),
        ssd_dt_bias=ghd[0, 0:N_HEADS], ssd_A_log=ghd[1, 0:N_HEADS],
        b_gate=dbg[0], norm1_g=dg1[0], norm2_g=dg2[0])
    return dh0, grads, big, (got_2, got_3)


def _local_step(x, target, w, conv, small, prefetch=None, early_reduce=None):
    h = x
    w = [dict(wl) for wl in w]
    lwts, saved = [], []
    for l in range(N_LAYERS):
        h, s, lwt, arrived = _layer_fwd(h, w[l], conv, small, l, prefetch if l == 0 else None)
        for layer, ws in arrived:
            w[layer].update(ws)
        lwts.append(lwt)
        saved.append(s)
    loss_blk, dgf, dh = _loss_head(h, small["norm_f"].reshape(1, D), target, tm=_tiles(x.shape[0])["tm"], name="loss_head")
    per_layer, big, carried = [None] * N_LAYERS, [None] * N_LAYERS, None
    for l in reversed(range(N_LAYERS)):
        hooks = early_reduce(big[1]) if (early_reduce is not None and l == 0) else None
        dh, per_layer[l], big[l], carried = _layer_bwd(dh, saved[l], lwts[l], l, hooks)
    grads = {k: jnp.stack([per_layer[l][k] for l in range(N_LAYERS)], axis=0) for k in per_layer[0]}
    grads["norm_f"] = dgf[0]
    return loss_blk, dh, grads, big, carried


PACK_W = 1024
BIG = (("w_in", W_IN_SHARD, D, W_IN_SHARD, 256), ("w_branch", 768, D, 256, D), ("w_out", 256, D, 256, D),
       ("w_ffn_in", D, FFN_SHARD, 256, FFN_SHARD), ("w_ffn_out", 704, D, 352, D))
CONV = ("lru_conv_w", "ssd_conv_w")
SMALL = ("norm1_g", "b_gate", "lru_conv_b", "lru_w_a", "lru_b_a", "lru_w_x", "lru_b_x", "lru_lambda", "ssd_conv_b",
         "ssd_dt_bias", "ssd_A_log", "ssd_D", "ssd_norm_g", "norm2_g", "norm_f")
_WIRE = jnp.bfloat16
N_CHIPS = 4
N_DEV = 8


def _mesh_pos():
    return lax.axis_index("x"), lax.axis_index("y"), lax.axis_index("c")


HBM_SPEC = pl.BlockSpec(memory_space=pltpu.HBM)


def _remote(src, dst, send_sems, recv_sems, k, to):
    return pltpu.make_async_remote_copy(src_ref=src, dst_ref=dst, send_sem=send_sems.at[k], recv_sem=recv_sems.at[k],
                                        device_id=to, device_id_type=MESH)


def _other_chips(x, y):
    return [(1 - x, y), (x, 1 - y), (1 - x, 1 - y)]


def _weight_fetch(loc, layer, owner):
    names = list(owner)
    rows = {n: loc[n].shape[1] for n in names}
    by_chip = ("w_in", "w_ffn_in")
    shapes = [((N_CHIPS,) + loc[n].shape[1:]) if n in by_chip else (N_CHIPS * rows[n], D) for n in names]

    def place(o_ref, n, chip):
        if n in by_chip:
            return o_ref.at[chip]
        return o_ref.at[pl.ds(pl.multiple_of(chip * rows[n], 16), rows[n]), :]

    def step(which, in_refs, o_refs, send_sems, recv_sems):
        x, y, c = _mesh_pos()
        s = 2 * x + y
        sib = (x, y, 1 - c)
        chips = _other_chips(x, y)
        for core in (0, 1):
            @pl.when(c == core)
            def _():
                for k, n in enumerate(names):
                    for j, (px, py) in enumerate(chips):
                        landed = place(o_refs[k], n, 2 * px + py)
                        sent = _remote(in_refs[k].at[layer], place(o_refs[k], n, s), send_sems, recv_sems, 3 * k + j,
                                       (px, py, c))
                        arrives = _remote(in_refs[k].at[layer], landed, send_sems, recv_sems, 3 * k + j, (px, py, c))
                        passed = _remote(landed, landed, send_sems, recv_sems, 3 * (len(names) + k) + j, sib)
                        if owner[n] == core:
                            if which == "start":
                                sent.start()
                            elif which == "mid":
                                arrives.wait_recv()
                                passed.start()
                            else:
                                sent.wait_send()
                                passed.wait_send()
                        elif which == "end":
                            passed.wait_recv()

    return dict(inputs=[loc[n] for n in names], names=names,
                out_shapes=[jax.ShapeDtypeStruct(shp, loc[n].dtype) for shp, n in zip(shapes, names)],
                sems=[pltpu.SemaphoreType.DMA((6 * len(names),)), pltpu.SemaphoreType.DMA((6 * len(names),))],
                start=functools.partial(step, "start"), mid=functools.partial(step, "mid"),
                end=functools.partial(step, "end"))


def _comm_now(comm, name):
    n, no = len(comm["inputs"]), len(comm["out_shapes"])

    def body(*refs):
        parts = (refs[:n], refs[n:n + no]) + tuple(refs[n + no:])
        comm["start"](*parts)
        comm["mid"](*parts)
        comm["end"](*parts)

    return pl.pallas_call(
        body, name=name, in_specs=[HBM_SPEC] * n, out_specs=[HBM_SPEC] * no, out_shape=comm["out_shapes"],
        scratch_shapes=comm["sems"],
    )(*comm["inputs"])


def _sibling_send(bufs, layer):
    n = len(bufs)

    def step(which, in_refs, o_refs, send_sems, recv_sems):
        x, y, c = _mesh_pos()
        copies = [_remote(in_refs[k], o_refs[k], send_sems, recv_sems, k, (x, y, 1 - c)) for k in range(n)]

        @pl.when(c != layer)
        def _():
            for cp in copies:
                if which == "start":
                    cp.start()
                elif which == "end":
                    cp.wait_send()

        @pl.when(c == layer)
        def _():
            for cp in copies:
                if which == "end":
                    cp.wait_recv()

    return dict(inputs=list(bufs), out_shapes=[jax.ShapeDtypeStruct(b.shape, b.dtype) for b in bufs],
                sems=[pltpu.SemaphoreType.DMA((n,)), pltpu.SemaphoreType.DMA((n,))],
                start=functools.partial(step, "start"), mid=functools.partial(step, "mid"),
                end=functools.partial(step, "end"))


def _add_cast(g, recv, own, *, a, tr, tc, name):
    wd = g.shape[1]
    nr = a // tr

    def body(own_ref, g_ref, r_ref, o_ref):
        @pl.when(own_ref[0] == 1)
        def _():
            o_ref[...] = (g_ref[...] + r_ref[...]).astype(o_ref.dtype)

    blk = pl.BlockSpec((tr, tc), lambda p, i, j, own_ref: ((p * nr + i) * own_ref[0], j * own_ref[0]))
    return pl.pallas_call(
        body, name=name,
        grid_spec=pltpu.PrefetchScalarGridSpec(
            num_scalar_prefetch=1, grid=(N_CHIPS, nr, wd // tc), in_specs=[blk, blk],
            out_specs=pl.BlockSpec((None, tr, tc), lambda p, i, j, own_ref: (p * own_ref[0], i * own_ref[0], j * own_ref[0]))),
        out_shape=jax.ShapeDtypeStruct((N_CHIPS, a, wd), _WIRE),
        compiler_params=_cp(("arbitrary", "arbitrary", "arbitrary")),
    )(own, g, recv)


def _chip_exchange(parts, layer):
    n = len(parts)

    def step(which, s_refs, o_refs, send_sems, recv_sems):
        x, y, c = _mesh_pos()
        s = 2 * x + y

        @pl.when(c == layer)
        def _():
            for j, (px, py) in enumerate(_other_chips(x, y)):
                for k in range(n):
                    p = 2 * px + py
                    sent = _remote(s_refs[k].at[p], o_refs[k].at[s], send_sems, recv_sems, n * j + k, (px, py, c))
                    if which == "start":
                        sent.start()
                    elif which == "end":
                        _remote(s_refs[k].at[p], o_refs[k].at[p], send_sems, recv_sems, n * j + k, (px, py, c)).wait_recv()
                        sent.wait_send()

    return dict(inputs=list(parts), out_shapes=[jax.ShapeDtypeStruct(p.shape, p.dtype) for p in parts],
                sems=[pltpu.SemaphoreType.DMA((3 * n,)), pltpu.SemaphoreType.DMA((3 * n,))],
                start=functools.partial(step, "start"), mid=functools.partial(step, "mid"),
                end=functools.partial(step, "end"))


def _sum_slots(slots, own, sel, *, tr, tc, name, layer=None, into=None):
    n, rows, wd = slots.shape
    k = own.shape[0]

    def body(sel_ref, s_ref, own_ref, *rest):
        o_ref = rest[-1]

        @pl.when(sel_ref[1] == 1)
        def _():
            mine = sel_ref[0]
            acc = jnp.zeros((tr, tc), F32)
            for p in range(n):
                acc = acc + jnp.where(mine == p, own_ref[...].astype(F32), s_ref[p].astype(F32))
            o_ref[...] = acc

    if layer is not None:
        out_spec = pl.BlockSpec((None, tr, tc), lambda i, j, sel_ref: (layer, i * sel_ref[1], j * sel_ref[1]))
        out_shape = jax.ShapeDtypeStruct((N_LAYERS, rows, wd), F32)
    else:
        out_spec = pl.BlockSpec((tr, tc), lambda i, j, sel_ref: (i * sel_ref[1], j * sel_ref[1]))
        out_shape = jax.ShapeDtypeStruct((rows, wd), F32)
    in_specs = [pl.BlockSpec((n, tr, tc), lambda i, j, sel_ref: (0, i * sel_ref[1], j * sel_ref[1])),
                pl.BlockSpec((None, tr, tc), lambda i, j, sel_ref: (sel_ref[0] if k > 1 else 0, i * sel_ref[1],
                                                                    j * sel_ref[1]))]
    args = [sel, slots, own]
    if into is not None:
        in_specs.append(pl.BlockSpec(memory_space=pl.ANY))
        args.append(into)
    return pl.pallas_call(
        body, name=name,
        grid_spec=pltpu.PrefetchScalarGridSpec(num_scalar_prefetch=1, grid=(rows // tr, wd // tc), in_specs=in_specs,
                                               out_specs=out_spec),
        out_shape=out_shape, input_output_aliases={3: 0} if into is not None else {},
        compiler_params=_cp(("arbitrary", "arbitrary")),
    )(*args)


def _sibling_share(both):
    n = len(both)

    def body(*refs):
        o_refs, (send_sems, recv_sems) = refs[n:2 * n], refs[2 * n:]
        x, y, c = _mesh_pos()
        sends = [_remote(o_refs[k].at[c], o_refs[k].at[c], send_sems, recv_sems, k, (x, y, 1 - c)) for k in range(n)]
        for cp in sends:
            cp.start()
        for k in range(n):
            _remote(o_refs[k].at[1 - c], o_refs[k].at[1 - c], send_sems, recv_sems, k, (x, y, 1 - c)).wait_recv()
        for cp in sends:
            cp.wait_send()

    return pl.pallas_call(
        body, name="grad_sibling_share", in_specs=[HBM_SPEC] * n, out_specs=[HBM_SPEC] * n,
        out_shape=[jax.ShapeDtypeStruct(b.shape, b.dtype) for b in both], input_output_aliases={k: k for k in range(n)},
        scratch_shapes=[pltpu.SemaphoreType.DMA((n,)), pltpu.SemaphoreType.DMA((n,))],
    )(*both)


def _allgather_devices(part):
    rows, wd = part.shape

    def step(which, in_refs, o_refs, send_sems, recv_sems):
        (p_ref,), (o_ref,) = in_refs, o_refs
        x, y, c = _mesh_pos()
        sib = (x, y, 1 - c)
        chips = _other_chips(x, y)
        slot = lambda px, py, pc: o_ref.at[4 * px + 2 * py + pc]
        first = [_remote(p_ref, slot(x, y, c), send_sems, recv_sems, 0, sib)]
        first += [_remote(p_ref, slot(x, y, c), send_sems, recv_sems, 1 + j, (px, py, c)) for j, (px, py) in enumerate(chips)]
        passed = [_remote(slot(px, py, c), slot(px, py, c), send_sems, recv_sems, 4 + j, sib)
                  for j, (px, py) in enumerate(chips)]
        if which == "start":
            for cp in first:
                cp.start()
        elif which == "mid":
            for j, (px, py) in enumerate(chips):
                _remote(p_ref, slot(px, py, c), send_sems, recv_sems, 1 + j, (px, py, c)).wait_recv()
                passed[j].start()
        else:
            _remote(p_ref, slot(x, y, 1 - c), send_sems, recv_sems, 0, sib).wait_recv()
            for j, (px, py) in enumerate(chips):
                _remote(slot(px, py, 1 - c), slot(px, py, 1 - c), send_sems, recv_sems, 4 + j, sib).wait_recv()
            for cp in first + passed:
                cp.wait_send()

    return dict(inputs=[part], out_shapes=[jax.ShapeDtypeStruct((N_DEV, rows, wd), part.dtype)],
                sems=[pltpu.SemaphoreType.DMA((N_DEV - 1,)), pltpu.SemaphoreType.DMA((N_DEV - 1,))],
                start=functools.partial(step, "start"), mid=functools.partial(step, "mid"),
                end=functools.partial(step, "end"))


def _comm_both(a, b):
    na, nao = len(a["inputs"]), len(a["out_shapes"])

    def step(which, in_refs, o_refs, sa, ra, sb, rb_):
        a[which](in_refs[:na], o_refs[:nao], sa, ra)
        b[which](in_refs[na:], o_refs[nao:], sb, rb_)

    return dict(inputs=a["inputs"] + b["inputs"], out_shapes=a["out_shapes"] + b["out_shapes"], sems=a["sems"] + b["sems"],
                start=functools.partial(step, "start"), mid=functools.partial(step, "mid"),
                end=functools.partial(step, "end"))


def _by_chip_to_full(stack):
    _, nl, r, b = stack.shape
    return stack.transpose(1, 2, 0, 3).reshape(nl, r, N_CHIPS * b)


def _sharded_step(a):
    x = a["x"][0]
    target = a["loss_target"][0]
    cx, cy, cc = _mesh_pos()
    chip = (2 * cx + cy).astype(jnp.int32)
    core = cc.astype(jnp.int32)
    me = (4 * cx + 2 * cy + cc).astype(jnp.int32)
    zero = jnp.zeros((), jnp.int32)
    dus = lax.dynamic_update_slice

    loc = {n: a[n].astype(_MXU) for n, *_ in BIG}

    def with_own(got, names, layer):
        out = {}
        for g, n in zip(got, names):
            mine = loc[n][layer]
            out[n] = (dus(g, mine[None], (chip, zero, zero)) if g.ndim == 3 else dus(g, mine, (chip * mine.shape[0], zero)))
        return out

    rest = {"w_ffn_in": 0, "w_branch": 1, "w_out": 1, "w_ffn_out": 1}
    conv_loc = jnp.concatenate([a[n].reshape(-1, PACK_W) for n in CONV], axis=0)
    now = _weight_fetch(loc, 0, {"w_in": 0})
    conv_all, *got_now = _comm_now(_comm_both(_allgather_devices(conv_loc), now), "allgather_weights")
    w0 = with_own(got_now, now["names"], 0)
    later = {"in_proj": (0, _weight_fetch(loc, 0, rest)), "lru": (1, _weight_fetch(loc, 1, {"w_in": 0})),
             "ssd": (1, _weight_fetch(loc, 1, rest))}
    prefetch = {k: (f, functools.partial(lambda got, layer, f: (layer, with_own(got, f["names"], layer)), layer=layer, f=f))
                for k, (layer, f) in later.items()}
    conv_all = dus(conv_all, conv_loc[None], (me, zero, zero))[0::2]
    conv, off = {}, 0
    for n in CONV:
        rows = a[n].size // PACK_W
        conv[n] = _by_chip_to_full(conv_all[:, off:off + rows].reshape((N_CHIPS,) + a[n].shape))
        off += rows
    small = {n: a[n] for n in SMALL}

    views = lambda big_l, specs: [big_l[n].reshape(-1, wd) for n, _, wd, _, _ in specs]
    owns = lambda layer: (core == layer).astype(jnp.int32)
    w_in_only, others = BIG[:1], BIG[1:]

    def partial_sums(big_l, recv, layer, specs):
        return [_add_cast(v, r, owns(layer).reshape(1), a=rows, tr=tr, tc=tc, name=f"grad_add_sibling_l{layer}_{n}")
                for v, r, (n, rows, _, tr, tc) in zip(views(big_l, specs), recv, specs)]

    def reduced(slots, parts, layer, into, specs):
        sel = jnp.stack([chip, owns(layer)])
        return [_sum_slots(s, p, sel, tr=tr, tc=tc, name=f"grad_sum_chips_l{layer}_{n}", layer=layer, into=buf)
                for s, p, buf, (n, _, _, tr, tc) in zip(slots, parts, into, specs)]

    kept = {}

    def early_reduce(big_1):
        def during_lru(big_0):
            kept["big_0"] = dict(big_0)
            return _comm_both(_sibling_send(views(big_1, BIG), 1), _sibling_send(views(big_0, others), 0))

        def during_ssd(recv):
            kept["parts_1"] = partial_sums(big_1, recv[:len(BIG)], 1, BIG)
            kept["parts_0"] = partial_sums(kept["big_0"], recv[len(BIG):], 0, others)
            return _comm_both(_chip_exchange(kept["parts_1"], 1), _chip_exchange(kept["parts_0"], 0))

        return dict(lru=during_lru, ssd=during_ssd, in_dgrad=lambda big_0: _sibling_send(views(big_0, w_in_only), 0))

    loss_blk, grad_x, grads, big, (slots, recv_in) = _local_step(x, target, [w0, {}], conv, small, prefetch, early_reduce)
    loss = lax.psum(loss_blk[0, 0], ("x", "y", "c"))
    both = reduced(slots[:len(BIG)], kept["parts_1"], 1, [None] * len(BIG), BIG)
    both[1:] = reduced(slots[len(BIG):], kept["parts_0"], 0, both[1:], others)
    parts_in = partial_sums(big[0], recv_in, 0, w_in_only)
    names = SMALL + CONV
    srows = -(-sum(grads[n].size for n in names) // (8 * PACK_W)) * 8
    flat = lambda d, ns: jnp.concatenate([d[n].reshape(-1) for n in ns])
    padto = lambda v: jnp.pad(v, (0, srows * PACK_W - v.shape[0])).reshape(srows, PACK_W)
    g_own = padto(flat(grads, names))
    g_all, *slots_in = _comm_now(_comm_both(_allgather_devices(g_own), _chip_exchange(parts_in, 0)), "grad_chip_exchange")
    both[:1] = reduced(slots_in, parts_in, 0, both[:1], w_in_only)
    done = dict(zip([n for n, *_ in BIG], _sibling_share(both)))
    g_big = {n: done[n].reshape(a[n].shape) for n in ("w_branch", "w_out", "w_ffn_in", "w_ffn_out")}
    gt = done["w_in"].transpose(0, 2, 1)
    first = jnp.concatenate([gt[..., 512 * j + 256 * part:512 * j + 256 * (part + 1)] for part in range(2) for j in range(4)]
                            + [gt[..., 2 * D:]], axis=-1)
    tail = W_IN_SHARD - (IN_DIM - 7168)
    last = jnp.concatenate([gt[..., :tail], gt[..., W_IN_SHARD - N_HEADS:], gt[..., tail:W_IN_SHARD - N_HEADS]], axis=-1)
    g_big["w_in"] = jnp.where(chip == 0, first, jnp.where(chip == N_CHIPS - 1, last, gt))

    g_sum = _sum_slots(g_all, g_own[None], jnp.stack([me, zero + 1]), tr=srows, tc=PACK_W, name="small_grad_sum")
    off, g_small = 0, {}
    for n in names:
        g_small[n] = g_sum.reshape(-1)[off:off + grads[n].size].reshape(grads[n].shape)
        off += grads[n].size
    for n in CONV:
        width = a[n].shape[2]
        g_big[n] = lax.dynamic_slice(g_small.pop(n), (zero, zero, chip * width), a[n].shape)

    out_g, out_d, out_m, out_v = {}, {}, {}, {}
    for n in g_big:
        shp = a[n].shape
        two_d = (shp[0] * shp[1], shp[2])
        d_, m_, v_ = _adamw(a[n].reshape(two_d), g_big[n].reshape(two_d), a["m_" + n].reshape(two_d),
                            a["v_" + n].reshape(two_d), name="adamw_" + n)
        out_g[n], out_d[n], out_m[n], out_v[n] = g_big[n], d_.reshape(shp), m_.reshape(shp), v_.reshape(shp)
    d_, m_, v_ = _adamw(padto(flat(a, SMALL)), padto(flat(g_small, SMALL)), padto(flat({n: a["m_" + n] for n in SMALL}, SMALL)),
                        padto(flat({n: a["v_" + n] for n in SMALL}, SMALL)), name="adamw_small")
    off = 0
    for n in SMALL:
        cut = lambda v: v.reshape(-1)[off:off + a[n].size].reshape(a[n].shape)
        out_g[n], out_d[n], out_m[n], out_v[n] = g_small[n], cut(d_), cut(m_), cut(v_)
        off += a[n].size
    return loss, grad_x[None], out_g, out_d, out_m, out_v


WEIGHTS = ("norm1_g", "w_in", "b_gate", "lru_conv_w", "lru_conv_b", "lru_w_a", "lru_b_a", "lru_w_x", "lru_b_x", "lru_lambda",
           "ssd_conv_w", "ssd_conv_b", "ssd_dt_bias", "ssd_A_log", "ssd_D", "ssd_norm_g", "w_branch", "w_out", "norm2_g",
           "w_ffn_in", "w_ffn_out", "norm_f")
INPUTS = ("x",) + WEIGHTS + ("loss_target",) + tuple("m_" + n for n in WEIGHTS) + tuple("v_" + n for n in WEIGHTS)


def kernel(x, norm1_g, w_in, b_gate, lru_conv_w, lru_conv_b, lru_w_a, lru_b_a, lru_w_x, lru_b_x, lru_lambda, ssd_conv_w, ssd_conv_b, ssd_dt_bias, ssd_A_log, ssd_D, ssd_norm_g, w_branch, w_out, norm2_g, w_ffn_in, w_ffn_out, norm_f, loss_target, m_norm1_g, m_w_in, m_b_gate, m_lru_conv_w, m_lru_conv_b, m_lru_w_a, m_lru_b_a, m_lru_w_x, m_lru_b_x, m_lru_lambda, m_ssd_conv_w, m_ssd_conv_b, m_ssd_dt_bias, m_ssd_A_log, m_ssd_D, m_ssd_norm_g, m_w_branch, m_w_out, m_norm2_g, m_w_ffn_in, m_w_ffn_out, m_norm_f, v_norm1_g, v_w_in, v_b_gate, v_lru_conv_w, v_lru_conv_b, v_lru_w_a, v_lru_b_a, v_lru_w_x, v_lru_b_x, v_lru_lambda, v_ssd_conv_w, v_ssd_conv_b, v_ssd_dt_bias, v_ssd_A_log, v_ssd_D, v_ssd_norm_g, v_w_branch, v_w_out, v_norm2_g, v_w_ffn_in, v_w_ffn_out, v_norm_f):
    args = (x, norm1_g, w_in, b_gate, lru_conv_w, lru_conv_b, lru_w_a, lru_b_a, lru_w_x, lru_b_x, lru_lambda, ssd_conv_w, ssd_conv_b, ssd_dt_bias, ssd_A_log, ssd_D, ssd_norm_g, w_branch, w_out, norm2_g, w_ffn_in, w_ffn_out, norm_f, loss_target, m_norm1_g, m_w_in, m_b_gate, m_lru_conv_w, m_lru_conv_b, m_lru_w_a, m_lru_b_a, m_lru_w_x, m_lru_b_x, m_lru_lambda, m_ssd_conv_w, m_ssd_conv_b, m_ssd_dt_bias, m_ssd_A_log, m_ssd_D, m_ssd_norm_g, m_w_branch, m_w_out, m_norm2_g, m_w_ffn_in, m_w_ffn_out, m_norm_f, v_norm1_g, v_w_in, v_b_gate, v_lru_conv_w, v_lru_conv_b, v_lru_w_a, v_lru_b_a, v_lru_w_x, v_lru_b_x, v_lru_lambda, v_ssd_conv_w, v_ssd_conv_b, v_ssd_dt_bias, v_ssd_A_log, v_ssd_D, v_ssd_norm_g, v_w_branch, v_w_out, v_norm2_g, v_w_ffn_in, v_w_ffn_out, v_norm_f)
    assert len(args) == len(INPUTS)
    loss, grad_x, g, d, m, v = _sharded_step(dict(zip(INPUTS, args)))
    return (loss, grad_x, *[g[n] for n in WEIGHTS], *[d[n] for n in WEIGHTS], *[m[n] for n in WEIGHTS],
            *[v[n] for n in WEIGHTS])
```

```python
import functools
import math

import numpy as np
import jax
import jax.numpy as jnp
from jax import lax
from jax.experimental import pallas as pl
from jax.experimental.pallas import tpu as pltpu

F32 = jnp.float32
_MXU = jnp.bfloat16
_HI = lax.Precision.HIGHEST

D = 1024
EPS = 1e-6
N_LAYERS = 2
LRU_C = 8.0
N_HEADS = 32
HEAD_P = 64
N_GROUPS = 4
N_STATE = 128
SSD_INNER = 2048
XBC = 3072
D_FF = 2816
CHUNK = 64
NORM_ROWS = 32
IN_DIM = 9248

NP = 9216
ZX_W = 5120
G0 = 6144
LBLK = 512
DT_PAD = 128

VMEM_LIMIT_BYTES_V7X = 56 * 1024 * 1024

ADAM_LR, ADAM_B1, ADAM_B2, ADAM_EPS, ADAM_WD, ADAM_STEP = 0.001, 0.9, 0.999, 1e-08, 0.01, 10
MESH = pl.DeviceIdType.MESH


def _cp(sem):
    return pltpu.CompilerParams(dimension_semantics=sem, vmem_limit_bytes=VMEM_LIMIT_BYTES_V7X)


def _lblk_col(j):
    return 10 + j + 4 * (j // 2)


def _sigmoid(x):
    return 0.5 * jnp.tanh(0.5 * x) + 0.5


def _softplus(x):
    return jnp.maximum(x, 0.0) + jnp.log(1.0 + jnp.exp(-jnp.abs(x)))


def _silu(x):
    return x * _sigmoid(x)


_GELU_C0 = math.sqrt(2.0 / math.pi)
_GELU_C1 = 0.044715


def _gelu_and_grad(x):
    t = jnp.tanh(_GELU_C0 * (x + _GELU_C1 * x * x * x))
    g = 0.5 * x * (1.0 + t)
    dg = 0.5 * (1.0 + t) + 0.5 * x * (1.0 - t * t) * _GELU_C0 * (1.0 + 3.0 * _GELU_C1 * x * x)
    return g, dg


def _one_minus_exp(x):
    p = 1.0 + x * (1.0 / 7.0)
    p = 1.0 + x * (1.0 / 6.0) * p
    p = 1.0 + x * (1.0 / 5.0) * p
    p = 1.0 + x * (1.0 / 4.0) * p
    p = 1.0 + x * (1.0 / 3.0) * p
    p = 1.0 + x * (1.0 / 2.0) * p
    return jnp.where(x > -0.3, -x * p, 1.0 - jnp.exp(x))


def _dot(a, b):
    return jnp.dot(a.astype(_MXU), b.astype(_MXU), preferred_element_type=F32)


def _dot_nt(a, b):
    return lax.dot_general(a.astype(_MXU), b.astype(_MXU), (((1,), (1,)), ((), ())), preferred_element_type=F32)


def _dot_tn(a, b):
    return lax.dot_general(a.astype(_MXU), b.astype(_MXU), (((0,), (0,)), ((), ())), preferred_element_type=F32)


def _shift_down(x, prev8, k):
    xr = pltpu.roll(x, k, 0)
    pr = pltpu.roll(prev8, k, 0)
    row = lax.broadcasted_iota(jnp.int32, prev8.shape, 0)
    head = jnp.where(row < k, pr, xr[0:8])
    return jnp.concatenate([head, xr[8:]], axis=0)


def _shift_up(x, next8, k):
    r = x.shape[0]
    xr = pltpu.roll(x, r - k, 0)
    nr = pltpu.roll(next8, 8 - k, 0)
    row = lax.broadcasted_iota(jnp.int32, next8.shape, 0)
    tail = jnp.where(row >= 8 - k, nr, xr[r - 8:r])
    return jnp.concatenate([xr[:r - 8], tail], axis=0)


def _conv4(x, prev8, w_ref, b_ref, cols=slice(None)):
    acc = x * w_ref[3:4, cols] + b_ref[0:1, cols]
    for k in (1, 2, 3):
        acc = acc + _shift_down(x, prev8, k) * w_ref[3 - k:4 - k, cols]
    return acc


def _lin_scan(a, b, reverse):
    r = a.shape[0]
    row = lax.broadcasted_iota(jnp.int32, a.shape, 0)
    d = 1
    while d < r:
        sh = (r - d) if reverse else d
        a_s = pltpu.roll(a, sh, 0)
        b_s = pltpu.roll(b, sh, 0)
        m = (row < r - d) if reverse else (row >= d)
        b = jnp.where(m, a * b_s + b, b)
        a = jnp.where(m, a * a_s, a)
        d *= 2
    return a, b


def _rsum(x):
    return jnp.sum(x, axis=0, keepdims=True)


def _comm_specs(comm):
    if comm is None:
        return [], [], [], [], []
    n = len(comm["inputs"])
    return list(comm["inputs"]), [HBM_SPEC] * n, [HBM_SPEC] * len(comm["out_shapes"]), list(comm["out_shapes"]), comm["sems"]


def _comm_steps(comm, refs, n_in, n_out, first, mid, last):
    ni, no = len(comm["inputs"]), len(comm["out_shapes"])
    parts = (refs[n_in:n_in + ni], refs[n_out:n_out + no]) + tuple(refs[len(refs) - len(comm["sems"]):])
    for when, what in ((first, "start"), (mid, "mid"), (last, "end")):
        @pl.when(when)
        def _():
            comm[what](*parts)


def _norm_mm(h, gamma, w, *, tm, tn, name, out_dtype=F32, comm=None):
    m, k = h.shape
    if w.ndim == 3:
        assert w.shape[2] == tn
        n = w.shape[0] * tn
        w_spec = pl.BlockSpec((None, k, tn), lambda i, j: (j, 0, 0))
    else:
        n = w.shape[1]
        w_spec = pl.BlockSpec((k, tn), lambda i, j: (0, j))

    c_args, c_in_specs, c_out_specs, c_out_shapes, c_sems = _comm_specs(comm)
    ni, nj = m // tm, n // tn

    def body(*refs):
        h_ref, g_ref, w_ref = refs[:3]
        xn_ref, o_ref = refs[3 + len(c_args):5 + len(c_args)]
        i, j = pl.program_id(0), pl.program_id(1)
        if comm is not None:
            _comm_steps(comm, refs, 3, 5 + len(c_args), (i == 0) & (j == 0), (i == (3 * ni) // 4) & (j == 0),
                        (i == ni - 1) & (j == nj - 1))

        @pl.when(j == 0)
        def _():
            x = h_ref[...]
            r = lax.rsqrt(jnp.mean(x * x, axis=-1, keepdims=True) + EPS)
            xn_ref[...] = ((x * r) * g_ref[...]).astype(xn_ref.dtype)
        o_ref[...] = jnp.dot(xn_ref[...], w_ref[...], preferred_element_type=F32).astype(o_ref.dtype)

    return pl.pallas_call(
        body, name=name, grid=(ni, nj),
        in_specs=[pl.BlockSpec((tm, k), lambda i, j: (i, 0)), pl.BlockSpec((1, k), lambda i, j: (0, 0)), w_spec] + c_in_specs,
        out_specs=[pl.BlockSpec((tm, k), lambda i, j: (i, 0)), pl.BlockSpec((tm, tn), lambda i, j: (i, j))] + c_out_specs,
        out_shape=[jax.ShapeDtypeStruct((m, k), _MXU), jax.ShapeDtypeStruct((m, n), out_dtype)] + c_out_shapes,
        scratch_shapes=c_sems,
        compiler_params=_cp(("arbitrary", "arbitrary") if comm is not None else ("parallel", "arbitrary")),
    )(h, gamma, w, *c_args)


def _mm_nn(a, w, *, tm, tn, name, residual=None):
    m, k = a.shape
    n = w.shape[1]

    def body(*refs):
        if residual is None:
            a_ref, w_ref, o_ref = refs
            o_ref[...] = _dot(a_ref[...], w_ref[...])
        else:
            a_ref, w_ref, r_ref, o_ref = refs
            o_ref[...] = _dot(a_ref[...], w_ref[...]) + r_ref[...]

    in_specs = [pl.BlockSpec((tm, k), lambda i, j: (i, 0)), pl.BlockSpec((k, tn), lambda i, j: (0, j))]
    args = [a, w]
    if residual is not None:
        in_specs.append(pl.BlockSpec((tm, tn), lambda i, j: (i, j)))
        args.append(residual)
    return pl.pallas_call(
        body, name=name, grid=(m // tm, n // tn), in_specs=in_specs,
        out_specs=pl.BlockSpec((tm, tn), lambda i, j: (i, j)),
        out_shape=jax.ShapeDtypeStruct((m, n), F32),
        compiler_params=_cp(("parallel", "parallel")),
    )(*args)


def _wgrad(a, b, *, tt, ta, tn, name, out_shape, out_block, out_index, a_tab=None, o_tab=None, into=None):
    t = a.shape[0]
    a_tab = list(range(a.shape[1] // ta)) if a_tab is None else a_tab
    o_tab = a_tab if o_tab is None else o_tab
    nb = b.shape[1] // tn

    def body(at_ref, ot_ref, a_ref, b_ref, *rest):
        del at_ref, ot_ref
        o_ref = rest[-1]

        @pl.when(pl.program_id(2) == 0)
        def _():
            o_ref[...] = jnp.zeros_like(o_ref)
        o_ref[...] += _dot_tn(a_ref[...], b_ref[...])

    in_specs = [pl.BlockSpec((tt, ta), lambda r, j, i, at, ot: (i, at[r])),
                pl.BlockSpec((tt, tn), lambda r, j, i, at, ot: (i, j))]
    args = [jnp.asarray(a_tab, jnp.int32), jnp.asarray(o_tab, jnp.int32), a, b]
    aliases = {}
    if into is not None:
        in_specs.append(pl.BlockSpec(memory_space=pl.ANY))
        args.append(into)
        aliases = {4: 0}
    return pl.pallas_call(
        body, name=name,
        grid_spec=pltpu.PrefetchScalarGridSpec(
            num_scalar_prefetch=2, grid=(len(a_tab), nb, t // tt), in_specs=in_specs,
            out_specs=pl.BlockSpec(out_block, lambda r, j, i, at, ot: out_index(ot[r], j))),
        out_shape=jax.ShapeDtypeStruct(out_shape, F32), input_output_aliases=aliases,
        compiler_params=_cp(("parallel", "parallel", "arbitrary")),
    )(*args)


def _mm_nt(a, w, *, tm, name):
    m, kc = a.shape
    n = w.shape[0]

    def body(a_ref, w_ref, o_ref):
        o_ref[...] = _dot_nt(a_ref[...], w_ref[...])

    return pl.pallas_call(
        body, name=name, grid=(m // tm,),
        in_specs=[pl.BlockSpec((tm, kc), lambda i: (i, 0)), pl.BlockSpec((n, kc), lambda i: (0, 0))],
        out_specs=pl.BlockSpec((tm, n), lambda i: (i, 0)),
        out_shape=jax.ShapeDtypeStruct((m, n), F32),
        compiler_params=_cp(("parallel",)),
    )(a, w)


def _mm_nt_rmsbwd(dy, w, x, gamma, dres, *, tm, tk, name, extra=None, comm=None):
    m, kc = dy.shape
    nk = kc // tk
    ni = m // tm
    n_x = 5 if extra is None else 7
    c_args, c_in_specs, c_out_specs, c_out_shapes, c_sems = _comm_specs(comm)
    if w.ndim == 3:
        assert w.shape[0] == nk and w.shape[2] == tk
        d = w.shape[1]
        w_spec = pl.BlockSpec((None, d, tk), lambda i, k: (k, 0, 0))
    else:
        d = w.shape[0]
        w_spec = pl.BlockSpec((d, tk), lambda i, k: (0, k))

    def body(*refs):
        dy_ref, w_ref, x_ref, g_ref, r_ref = refs[:5]
        if extra is not None:
            dy2_ref, w2_ref = refs[5:7]
        n_out = n_x + len(c_args)
        dx_ref, dg_ref = refs[n_out:n_out + 2]
        acc_ref = refs[n_out + 2 + len(c_out_shapes)]
        i, kk = pl.program_id(0), pl.program_id(1)
        if comm is not None:
            _comm_steps(comm, refs, n_x, n_out + 2, (i == 0) & (kk == 0), (i == (3 * ni) // 4) & (kk == 0),
                        (i == ni - 1) & (kk == nk - 1))

        @pl.when(kk == 0)
        def _():
            acc_ref[...] = jnp.zeros_like(acc_ref)

        @pl.when((i == 0) & (kk == 0))
        def _():
            dg_ref[...] = jnp.zeros_like(dg_ref)

        acc_ref[...] += _dot_nt(dy_ref[...], w_ref[...])

        @pl.when(kk == nk - 1)
        def _():
            dxn = acc_ref[...]
            if extra is not None:
                dxn = dxn + _dot_nt(dy2_ref[...], w2_ref[...])
            xv = x_ref[...]
            r = lax.rsqrt(jnp.mean(xv * xv, axis=-1, keepdims=True) + EPS)
            xh = xv * r
            dg_ref[0:1, :] += _rsum(dxn * xh)
            dxh = dxn * g_ref[...]
            dx_ref[...] = r_ref[...] + r * (dxh - xh * jnp.mean(dxh * xh, axis=-1, keepdims=True))

    in_specs = [pl.BlockSpec((tm, tk), lambda i, k: (i, k)), w_spec,
                pl.BlockSpec((tm, d), lambda i, k: (i, 0)), pl.BlockSpec((1, d), lambda i, k: (0, 0)),
                pl.BlockSpec((tm, d), lambda i, k: (i, 0))]
    args = [dy, w, x, gamma, dres]
    if extra is not None:
        k2 = extra[0].shape[1]
        in_specs += [pl.BlockSpec((tm, k2), lambda i, k: (i, 0)), pl.BlockSpec((d, k2), lambda i, k: (0, 0))]
        args += list(extra)
    return pl.pallas_call(
        body, name=name, grid=(ni, nk), in_specs=in_specs + c_in_specs,
        out_specs=[pl.BlockSpec((tm, d), lambda i, k: (i, 0)), pl.BlockSpec((8, d), lambda i, k: (0, 0))] + c_out_specs,
        out_shape=[jax.ShapeDtypeStruct((m, d), F32), jax.ShapeDtypeStruct((8, d), F32)] + c_out_shapes,
        scratch_shapes=[pltpu.VMEM((tm, d), F32)] + c_sems,
        compiler_params=_cp(("arbitrary", "arbitrary")),
    )(*args, *c_args)


def _rsum8(x):
    acc = x[0:8]
    for g in range(1, x.shape[0] // 8):
        acc = acc + x[8 * g:8 * (g + 1)]
    return acc


def _lru_gates(x, prev8, cw_ref, cb_ref, wa_ref, wx_ref, ba_ref, bx_ref, lam_ref):
    u = _conv4(x, prev8, cw_ref, cb_ref)
    ra =_sigmoid(_dot(u, wa_ref[0]) + ba_ref[...])
    ia = _sigmoid(_dot(u, wx_ref[0]) + bx_ref[...])
    sp = _softplus(-lam_ref[...])
    log_a = -LRU_C * ra * sp
    a = jnp.exp(log_a)
    m2 = _one_minus_exp(2.0 * log_a)
    mult = jnp.sqrt(m2)
    return u, ra, ia, sp, a, m2, mult


def _lru_fwd(proj, lw, *, r, name, comm=None):
    t = proj.shape[0]
    nt = t // r
    c_args, c_in_specs, c_out_specs, c_out_shapes, c_sems = _comm_specs(comm)

    def body(*refs):
        xg_ref, xp_ref, cw_ref, cb_ref, wa_ref, wx_ref, ba_ref, bx_ref, lam_ref = refs[:9]
        hl_ref, ya_ref, sv_ref = refs[9 + len(c_args):12 + len(c_args)]
        carry_ref = refs[12 + len(c_args) + len(c_out_shapes)]
        i = pl.program_id(1)
        if comm is not None:
            j = pl.program_id(0)
            _comm_steps(comm, refs, 9, 12 + len(c_args), (j == 0) & (i == 0), (j == 3) & (i == 0), (j == 3) & (i == nt - 1))

        @pl.when(i == 0)
        def _():
            carry_ref[...] = jnp.zeros_like(carry_ref)

        x = xg_ref[:, 0:256]
        lg = xg_ref[:, 256:512]
        prev8 = jnp.where(i == 0, 0.0, xp_ref[:, 0:256])
        u, ra, ia, sp, a, m2, mult = _lru_gates(x, prev8, cw_ref, cb_ref, wa_ref, wx_ref, ba_ref, bx_ref, lam_ref)
        for k, v in enumerate((u, ra, ia, a, mult)):
            sv_ref[k] = v
        ac, hc = _lin_scan(a, mult * ia * u, False)
        h = hc + ac * carry_ref[0:1, :]
        hl_ref[...] = h
        carry_ref[0:1, :] = hl_ref[r - 1:r, :]
        g, _ = _gelu_and_grad(lg)
        ya_ref[...] = (g * h).astype(ya_ref.dtype)

    small = lambda rows: pl.BlockSpec((rows, 256), lambda j, i: (0, j))
    return pl.pallas_call(
        body, name=name, grid=(4, nt),
        in_specs=[pl.BlockSpec((r, LBLK), lambda j, i: (i, _lblk_col(j))),
                  pl.BlockSpec((8, LBLK), lambda j, i: (jnp.maximum(i * (r // 8) - 1, 0), _lblk_col(j))),
                  small(4), small(1),
                  pl.BlockSpec((1, 256, 256), lambda j, i: (j, 0, 0)), pl.BlockSpec((1, 256, 256), lambda j, i: (j, 0, 0)),
                  small(1), small(1), small(1)] + c_in_specs,
        out_specs=[pl.BlockSpec((r, 256), lambda j, i: (i, j)), pl.BlockSpec((r, 256), lambda j, i: (i, j)),
                   pl.BlockSpec((5, r, 256), lambda j, i: (0, i, j))] + c_out_specs,
        out_shape=[jax.ShapeDtypeStruct((t, D), F32), jax.ShapeDtypeStruct((t, D), _MXU),
                   jax.ShapeDtypeStruct((5, t, D), F32)] + c_out_shapes,
        scratch_shapes=[pltpu.VMEM((8, 256), F32)] + c_sems,
        compiler_params=_cp(("arbitrary", "arbitrary") if comm is not None else ("parallel", "arbitrary")),
    )(proj, proj, lw["cw"], lw["cb"], lw["wa"], lw["wx"], lw["ba"], lw["bx"], lw["lam"], *c_args)


def _lru_bwd(proj, hl, gates, dya, dproj, lw, *, r, name, comm=None):
    t = proj.shape[0]
    nt = t // r
    c_args, c_in_specs, c_out_specs, c_out_shapes, c_sems = _comm_specs(comm)
    n_in = 10

    def body(*refs):
        xg_ref, hl_ref, hp_ref, sv_ref, dya_ref, cw_ref, wa_ref, wx_ref, lam_ref = refs[:9]
        n_out = n_in + len(c_args)
        dproj_ref, sm_ref, dwa_ref, dwx_ref = refs[n_out:n_out + 4]
        n_scr = n_out + 4 + len(c_out_shapes)
        carry_ref, du8_ref, row_scr = refs[n_scr:n_scr + 3]
        i = pl.program_id(1)
        if comm is not None:
            j = pl.program_id(0)
            _comm_steps(comm, refs, n_in, n_out + 4, (j == 0) & (i == 0), (j == 3) & (i == 0), (j == 3) & (i == nt - 1))

        @pl.when(i == 0)
        def _():
            carry_ref[...] = jnp.zeros_like(carry_ref)
            du8_ref[...] = jnp.zeros_like(du8_ref)
            sm_ref[...] = jnp.zeros_like(sm_ref)
            dwa_ref[...] = jnp.zeros_like(dwa_ref)
            dwx_ref[...] = jnp.zeros_like(dwx_ref)

        tile0 = i == nt - 1
        xp = xg_ref[:, 0:256]
        lg = xg_ref[:, 256:512]
        u, ra, ia, a, mult = (sv_ref[k] for k in range(5))
        sp = _softplus(-lam_ref[...])
        h = hl_ref[...]
        hprev = _shift_down(h, jnp.where(tile0, 0.0, hp_ref[...]), 1)
        dya_v = dya_ref[...]
        g, dg = _gelu_and_grad(lg)
        ac, lc = _lin_scan(_shift_up(a, carry_ref[...], 1), dya_v * g, True)
        lam_v = lc + ac * carry_ref[1:2, :]
        row_scr[0:8, :] = lam_v[0:8]
        row_scr[8:16, :] = a[0:8]
        carry_ref[1:2, :] = row_scr[0:1, :]
        carry_ref[0:1, :] = row_scr[8:9, :]
        da = lam_v * hprev
        dmult = lam_v * ia * u
        dia = lam_v * mult * u
        dlog = da * a - dmult * (a * a) / mult
        dra = -LRU_C * sp * dlog
        dpa = dra * ra * (1.0 - ra)
        dpx = dia * ia * (1.0 - ia)
        du = lam_v * mult * ia + _dot_nt(dpa, wa_ref[0]) + _dot_nt(dpx, wx_ref[0])
        dwa_ref[0] += _dot_tn(u, dpa)
        dwx_ref[0] += _dot_tn(u, dpx)
        dlx = du * cw_ref[3:4, :]
        sm_ref[24:32, :] += _rsum8(du * xp)
        for k in (1, 2, 3):
            du_k = _shift_up(du, du8_ref[...], k)
            dlx = dlx + du_k * cw_ref[3 - k:4 - k, :]
            sm_ref[8 * (3 - k):8 * (4 - k), :] += _rsum8(du_k * xp)
        du8_ref[...] = du[0:8]
        dproj_ref[:, 0:256] = dlx.astype(dproj_ref.dtype)
        dproj_ref[:, 256:512] = (dya_v * h * dg).astype(dproj_ref.dtype)
        sm_ref[32:40, :] += _rsum8(du)
        sm_ref[40:48, :] += _rsum8(dpa)
        sm_ref[48:56, :] += _rsum8(dpx)
        sm_ref[56:64, :] += _rsum8(-LRU_C * ra * dlog) * (-_sigmoid(-lam_ref[...]))

    rev = lambda i: nt - 1 - i
    small = lambda rows: pl.BlockSpec((rows, 256), lambda j, i: (0, j))
    wblk = pl.BlockSpec((1, 256, 256), lambda j, i: (j, 0, 0))
    return pl.pallas_call(
        body, name=name, grid=(4, nt),
        in_specs=[pl.BlockSpec((r, LBLK), lambda j, i: (rev(i), _lblk_col(j))),
                  pl.BlockSpec((r, 256), lambda j, i: (rev(i), j)),
                  pl.BlockSpec((8, 256), lambda j, i: (jnp.maximum(rev(i) * (r // 8) - 1, 0), j)),
                  pl.BlockSpec((5, r, 256), lambda j, i: (0, rev(i), j)),
                  pl.BlockSpec((r, 256), lambda j, i: (rev(i), j)),
                  small(4), wblk, wblk, small(1),
                  pl.BlockSpec(memory_space=pl.ANY)] + c_in_specs,
        out_specs=[pl.BlockSpec((r, LBLK), lambda j, i: (rev(i), _lblk_col(j))),
                   pl.BlockSpec((64, 256), lambda j, i: (0, j)), wblk, wblk] + c_out_specs,
        out_shape=[jax.ShapeDtypeStruct(dproj.shape, dproj.dtype), jax.ShapeDtypeStruct((64, D), F32),
                   jax.ShapeDtypeStruct((4, 256, 256), F32), jax.ShapeDtypeStruct((4, 256, 256), F32)] + c_out_shapes,
        scratch_shapes=[pltpu.VMEM((8, 256), F32), pltpu.VMEM((8, 256), F32), pltpu.VMEM((16, 256), F32)] + c_sems,
        input_output_aliases={n_in - 1: 0},
        compiler_params=_cp(("arbitrary", "arbitrary") if comm is not None else ("parallel", "arbitrary")),
    )(proj, hl, hl, gates, dya, lw["cw"], lw["wa"], lw["wx"], lw["lam"], dproj, *c_args)


def _head_cols(x):
    lane = lax.broadcasted_iota(jnp.int32, x.shape, 1)
    return [jnp.sum(jnp.where(lane == h, x, 0.0), axis=1, keepdims=True) for h in range(N_HEADS)]


def _compact_heads(blocks):
    lane = lax.broadcasted_iota(jnp.int32, blocks[0].shape, 1)
    lo = lane < HEAD_P
    out = jnp.zeros_like(blocks[0])
    for j, blk in enumerate(blocks):
        s_lo = jnp.sum(jnp.where(lo, blk, 0.0), axis=1, keepdims=True)
        s_hi = jnp.sum(jnp.where(lo, 0.0, blk), axis=1, keepdims=True)
        out = jnp.where(lane == 2 * j, s_lo, out)
        out = jnp.where(lane == 2 * j + 1, s_hi, out)
    return out


def _ssd_prelude(dtraw_ref, dtb_ref, alog_ref, dt_scr, a_scr):
    lane = lax.broadcasted_iota(jnp.int32, dt_scr.shape, 1)
    dt = jnp.where(lane < N_HEADS, _softplus(dtraw_ref[...] + dtb_ref[0:1, :]), 0.0)
    dt_scr[...] = dt
    a_scr[...] = dt * (-jnp.exp(alog_ref[0:1, :]))


def _ssd_chunk_scalars(dt_scr, a_scr, r_scr, r0):
    a_c = a_scr[pl.ds(r0, CHUNK), :]
    dt_c = dt_scr[pl.ds(r0, CHUNK), :]
    i0 = lax.broadcasted_iota(jnp.int32, (CHUNK, CHUNK), 0)
    i1 = lax.broadcasted_iota(jnp.int32, (CHUNK, CHUNK), 1)
    tri = jnp.where(i0 >= i1, 1.0, 0.0).astype(F32)
    cs = jnp.dot(tri, a_c, precision=_HI, preferred_element_type=F32)
    lane = lax.broadcasted_iota(jnp.int32, (CHUNK, 128), 1)
    srow = lax.broadcasted_iota(jnp.int32, (CHUNK, 128), 0)
    t_lo = jnp.where((lane < HEAD_P) & (srow <= lane), 1.0, 0.0).astype(F32)
    t_hi = jnp.where((lane >= HEAD_P) & (srow <= lane - HEAD_P), 1.0, 0.0).astype(F32)
    even = (lane % 2) == 0
    tn = (((0,), (0,)), ((), ()))
    r_scr[...] = (lax.dot_general(jnp.where(even, a_c, 0.0), t_lo, tn, precision=_HI, preferred_element_type=F32)
                  + lax.dot_general(jnp.where(even, 0.0, a_c), t_hi, tn, precision=_HI, preferred_element_type=F32))
    return cs, dt_c, _head_cols(cs), _head_cols(dt_c)


def _block_diag2(v):
    lo = lax.broadcasted_iota(jnp.int32, v.shape, 1) < HEAD_P
    return jnp.concatenate([jnp.where(lo, v, 0.0), jnp.where(lo, 0.0, v)], axis=0).astype(_MXU)


def _ssd_pair(xc_scr, r_scr, cs_cols, dt_cols, s2, r0, j, s2t=None):
    lane = lax.broadcasted_iota(jnp.int32, (CHUNK, 128), 1)
    srow = lax.broadcasted_iota(jnp.int32, (CHUNK, 128), 0)
    lo = lane < HEAD_P
    csc = jnp.where(lo, cs_cols[2 * j], cs_cols[2 * j + 1])
    dtc = jnp.where(lo, dt_cols[2 * j], dt_cols[2 * j + 1])
    csr = r_scr[2 * j:2 * j + 1, :] + r_scr[2 * j + 1:2 * j + 2, :]
    dm = jnp.where((lane & (HEAD_P - 1)) <= srow, jnp.exp(jnp.minimum(csc - csr, 0.0)), 0.0)
    xs = xc_scr[pl.ds(r0, CHUNK), j * 128:(j + 1) * 128]
    xd = xs * dtc
    csl = jnp.sum(jnp.where(srow == CHUNK - 1, csc, 0.0), axis=0, keepdims=True)
    out = dict(csc=csc, dtc=dtc, dm=dm, m2=s2 * dm, xs=xs, xd=xd, rhs=_block_diag2(xd), e=jnp.exp(csc),
               w=jnp.exp(csl - csc), dec=jnp.exp(csl))
    if s2t is not None:
        out["mt2"] = s2t * jnp.where((lane & (HEAD_P - 1)) >= srow, jnp.exp(jnp.minimum(csr - csc, 0.0)), 0.0)
    return out


def _cat(parts):
    return jnp.concatenate(parts, axis=1)


def _ssd_fwd(proj, dtraw, sw, *, rb, name, comm=None):
    t = proj.shape[0]
    ns, cb = t // rb, rb // CHUNK
    c_args, c_in_specs, c_out_specs, c_out_shapes, c_sems = _comm_specs(comm)

    def body(*refs):
        zx_ref, zp_ref, dtraw_ref, cw_ref, cbias_ref, dtb_ref, alog_ref, dsk_ref, ng_ref = refs[:9]
        yssd_ref, yb_ref, st_ref, xc_scr, dsl_ref = refs[9 + len(c_args):14 + len(c_args)]
        n_scr = 14 + len(c_args) + len(c_out_shapes)
        h_scr, dt_scr, a_scr, r_scr = refs[n_scr:n_scr + 4]
        i = pl.program_id(0)
        if comm is not None:
            _comm_steps(comm, refs, 9, 14 + len(c_args), i == 0, i == (3 * ns) // 4, i == ns - 1)

        @pl.when(i == 0)
        def _():
            h_scr[...] = jnp.zeros_like(h_scr)

        for j in range(XBC // 128):
            cs_, zc = slice(128 * j, 128 * (j + 1)), slice(2048 + 128 * j, 2048 + 128 * (j + 1))
            pre = _conv4(zx_ref[:, zc], jnp.where(i == 0, 0.0, zp_ref[:, zc]), cw_ref, cbias_ref, cs_)
            sg = _sigmoid(pre)
            xc_scr[:, cs_] = pre * sg
            dsl_ref[:, cs_] = sg * (1.0 + pre * (1.0 - sg))
        _ssd_prelude(dtraw_ref, dtb_ref, alog_ref, dt_scr, a_scr)

        def chunk(c, carry):
            r0 = pl.multiple_of(c * CHUNK, CHUNK)
            _, _, cs_cols, dt_cols = _ssd_chunk_scalars(dt_scr, a_scr, r_scr, r0)
            st_ref[c] = h_scr[...]
            for g in range(N_GROUPS):
                bg = xc_scr[pl.ds(r0, CHUNK), 2048 + 128 * g:2048 + 128 * (g + 1)]
                cg = xc_scr[pl.ds(r0, CHUNK), 2560 + 128 * g:2560 + 128 * (g + 1)]
                s2 = _dot_nt(cg, jnp.concatenate([bg, bg], axis=0))
                hp = h_scr[:, 512 * g:512 * (g + 1)]
                yoff = _dot(cg, hp)
                xdw, dec = [], []
                for jj in range(4):
                    j = 4 * g + jj
                    p = _ssd_pair(xc_scr, r_scr, cs_cols, dt_cols, s2, r0, j)
                    y = _dot(p["m2"], p["rhs"]) + yoff[:, 128 * jj:128 * (jj + 1)] * p["e"]
                    yssd_ref[pl.ds(r0, CHUNK), 128 * j:128 * (j + 1)] = y + dsk_ref[0:1, 128 * j:128 * (j + 1)] * p["xs"]
                    xdw.append(p["xd"] * p["w"])
                    dec.append(p["dec"])
                h_scr[:, 512 * g:512 * (g + 1)] = hp * _cat(dec) + _dot_tn(bg, _cat(xdw))
            return carry

        lax.fori_loop(0, cb, chunk, 0)
        for g in range(N_GROUPS):
            sl = slice(512 * g, 512 * (g + 1))
            for q in range(rb // NORM_ROWS):
                rw = slice(NORM_ROWS * q, NORM_ROWS * (q + 1))
                yz = yssd_ref[rw, sl] * _silu(zx_ref[rw, sl])
                rg = lax.rsqrt(jnp.mean(yz * yz, axis=-1, keepdims=True) + EPS)
                yb_ref[rw, sl] = (yz * rg * ng_ref[0:1, sl]).astype(yb_ref.dtype)

    full = lambda rows, cols: pl.BlockSpec((rows, cols), lambda i: (0, 0))
    return pl.pallas_call(
        body, name=name, grid=(ns,),
        in_specs=[pl.BlockSpec((rb, ZX_W), lambda i: (i, 0)),
                  pl.BlockSpec((8, ZX_W), lambda i: (jnp.maximum(i * (rb // 8) - 1, 0), 0)),
                  pl.BlockSpec((rb, DT_PAD), lambda i: (i, 0)),
                  full(4, XBC), full(1, XBC), full(1, DT_PAD), full(1, DT_PAD), full(1, SSD_INNER), full(1, SSD_INNER)]
        + c_in_specs,
        out_specs=[pl.BlockSpec((rb, SSD_INNER), lambda i: (i, 0)), pl.BlockSpec((rb, SSD_INNER), lambda i: (i, 0)),
                   pl.BlockSpec((cb, N_STATE, SSD_INNER), lambda i: (i, 0, 0)),
                   pl.BlockSpec((rb, XBC), lambda i: (i, 0)), pl.BlockSpec((rb, XBC), lambda i: (i, 0))] + c_out_specs,
        out_shape=[jax.ShapeDtypeStruct((t, SSD_INNER), F32), jax.ShapeDtypeStruct((t, SSD_INNER), _MXU),
                   jax.ShapeDtypeStruct((t // CHUNK, N_STATE, SSD_INNER), F32),
                   jax.ShapeDtypeStruct((t, XBC), F32), jax.ShapeDtypeStruct((t, XBC), F32)] + c_out_shapes,
        scratch_shapes=[pltpu.VMEM((N_STATE, SSD_INNER), F32), pltpu.VMEM((rb, DT_PAD), F32),
                        pltpu.VMEM((rb, DT_PAD), F32), pltpu.VMEM((128, 128), F32)] + c_sems,
        compiler_params=_cp(("arbitrary",)),
    )(proj, proj, dtraw, sw["cw"], sw["cb"], sw["dtb"], sw["alog"], sw["dsk"], sw["ng"], *c_args)


def _ssd_bwd(proj, dtraw, yssd, states, xc, dsl, dyb, dproj, sw, *, rb, name, comm=None):
    t = proj.shape[0]
    ns, cb = t // rb, rb // CHUNK
    c_args, c_in_specs, c_out_specs, c_out_shapes, c_sems = _comm_specs(comm)
    n_in = 13

    def body(*refs):
        zx_ref, dtraw_ref, yssd_ref, st_ref, xc_scr, dsl_scr, dyb_ref, cw_ref, dtb_ref, alog_ref, dsk_ref, ng_ref = refs[:12]
        n_out = n_in + len(c_args)
        dzx_ref, ddt_ref, gconv_ref, gch_ref, ghd_ref = refs[n_out:n_out + 5]
        n_scr = n_out + 5 + len(c_out_shapes)
        dht_scr, dy_scr, dxc_scr, dt_scr, a_scr, r_scr, dp8_scr = refs[n_scr:n_scr + 7]
        i = pl.program_id(0)
        if comm is not None:
            _comm_steps(comm, refs, n_in, n_out + 5, i == 0, i == (3 * ns) // 4, i == ns - 1)

        @pl.when(i == 0)
        def _():
            dht_scr[...] = jnp.zeros_like(dht_scr)
            dp8_scr[...] = jnp.zeros_like(dp8_scr)
            gconv_ref[...] = jnp.zeros_like(gconv_ref)
            gch_ref[...] = jnp.zeros_like(gch_ref)
            ghd_ref[...] = jnp.zeros_like(ghd_ref)

        _ssd_prelude(dtraw_ref, dtb_ref, alog_ref, dt_scr, a_scr)

        for g in range(N_GROUPS):
            sl = slice(512 * g, 512 * (g + 1))
            for q in range(rb // NORM_ROWS):
                rw = slice(NORM_ROWS * q, NORM_ROWS * (q + 1))
                zv = zx_ref[rw, sl]
                ys = yssd_ref[rw, sl]
                sg = _sigmoid(zv)
                sz = zv * sg
                yz = ys * sz
                rg = lax.rsqrt(jnp.mean(yz * yz, axis=-1, keepdims=True) + EPS)
                yn = yz * rg
                dyb_v = dyb_ref[rw, sl]
                gch_ref[0:8, sl] += _rsum8(dyb_v * yn)
                dyn = dyb_v * ng_ref[0:1, sl]
                dyz = rg * (dyn - yn * jnp.mean(dyn * yn, axis=-1, keepdims=True))
                dy_scr[rw, sl] = dyz * sz
                dzx_ref[rw, sl] = (dyz * ys * (sg * (1.0 + zv * (1.0 - sg)))).astype(dzx_ref.dtype)

        a_row = -jnp.exp(alog_ref[0:1, :])

        def chunk(cc, carry):
            c = cb - 1 - cc
            r0 = pl.multiple_of(c * CHUNK, CHUNK)
            rows = pl.ds(r0, CHUNK)
            _, dt_c, cs_cols, dt_cols = _ssd_chunk_scalars(dt_scr, a_scr, r_scr, r0)
            lane = lax.broadcasted_iota(jnp.int32, (CHUNK, 128), 1)
            srow = lax.broadcasted_iota(jnp.int32, (CHUNK, 128), 0)
            lo = lane < HEAD_P
            last = srow == CHUNK - 1
            p1_blocks, p3_blocks = [], []
            for g in range(N_GROUPS):
                gs = slice(512 * g, 512 * (g + 1))
                bg = xc_scr[rows, 2048 + 128 * g:2048 + 128 * (g + 1)]
                cg = xc_scr[rows, 2560 + 128 * g:2560 + 128 * (g + 1)]
                b2 = jnp.concatenate([bg, bg], axis=0)
                s2 = _dot_nt(cg, b2)
                s2t = _dot_nt(bg, jnp.concatenate([cg, cg], axis=0))
                hp = st_ref[c, :, gs]
                dht = dht_scr[:, gs]
                yoff = _dot(cg, hp)
                ps = [_ssd_pair(xc_scr, r_scr, cs_cols, dt_cols, s2, r0, 4 * g + jj, s2t) for jj in range(4)]
                dys = [dy_scr[rows, 128 * (4 * g + jj):128 * (4 * g + jj + 1)] for jj in range(4)]
                dye = _cat([dys[jj] * ps[jj]["e"] for jj in range(4)])
                w_g = _cat([p["w"] for p in ps])
                dcg = _dot_nt(dye, hp)
                dht_scr[:, gs] = _dot_tn(cg, dye) + _cat([p["dec"] for p in ps]) * dht
                dxd_state = w_g * _dot(bg, dht)
                dbg = _dot_nt(_cat([p["xd"] for p in ps]) * w_g, dht)
                tsum = _rsum(dht * hp)
                ds2 = jnp.zeros((CHUNK, 128), F32)
                for jj in range(4):
                    j = 4 * g + jj
                    ls = slice(128 * j, 128 * (j + 1))
                    p, dy2 = ps[jj], dys[jj]
                    dy_bd = _block_diag2(dy2)
                    dm2 = _dot_nt(dy2, p["rhs"])
                    ds2 = ds2 + dm2 * p["dm"]
                    gdiff = dm2 * p["m2"] - _dot_nt(p["xd"], dy_bd) * p["mt2"]
                    dxs = dxd_state[:, 128 * jj:128 * (jj + 1)]
                    dxd = _dot(p["mt2"], dy_bd) + dxs
                    end_row = _rsum(p["xd"] * dxs) + p["dec"] * tsum[:, 128 * jj:128 * (jj + 1)]
                    p1_blocks.append(gdiff + dy2 * yoff[:, 128 * jj:128 * (jj + 1)] * p["e"] - p["xd"] * dxs
                                     + jnp.where(last, end_row, 0.0))
                    p3_blocks.append(dxd * p["xs"])
                    dxc_scr[rows, ls] = dxd * p["dtc"] + dy2 * dsk_ref[0:1, ls]
                    gch_ref[8:16, ls] += _rsum8(dy2 * p["xs"])
                dcg = dcg + _dot(ds2, b2)
                rb2 = _dot_tn(ds2, cg)
                dxc_scr[rows, 2048 + 128 * g:2048 + 128 * (g + 1)] = dbg + rb2[0:CHUNK] + rb2[CHUNK:2 * CHUNK]
                dxc_scr[rows, 2560 + 128 * g:2560 + 128 * (g + 1)] = dcg
            dcs = _compact_heads(p1_blocks)
            i0 = lax.broadcasted_iota(jnp.int32, (CHUNK, CHUNK), 0)
            i1 = lax.broadcasted_iota(jnp.int32, (CHUNK, CHUNK), 1)
            triu = jnp.where(i1 >= i0, 1.0, 0.0).astype(F32)
            da = jnp.dot(triu, dcs, precision=_HI, preferred_element_type=F32)
            ddt = _compact_heads(p3_blocks) + da * a_row
            ddtraw = jnp.where(lane < N_HEADS, ddt * _sigmoid(dtraw_ref[rows, :] + dtb_ref[0:1, :]), 0.0)
            ddt_ref[rows, :] = ddtraw.astype(ddt_ref.dtype)
            ghd_ref[0:1, :] += _rsum(ddtraw)
            ghd_ref[1:2, :] += _rsum(da * dt_c) * a_row
            return carry

        lax.fori_loop(0, cb, chunk, 0)
        for j in range(XBC // 128):
            cs_, zc = slice(128 * j, 128 * (j + 1)), slice(2048 + 128 * j, 2048 + 128 * (j + 1))
            dpre = dxc_scr[:, cs_] * dsl_scr[:, cs_]
            xraw = zx_ref[:, zc]
            dx = dpre * cw_ref[3:4, cs_]
            gconv_ref[24:32, cs_] += _rsum8(dpre * xraw)
            for k in (1, 2, 3):
                dpre_k = _shift_up(dpre, dp8_scr[:, cs_], k)
                dx = dx + dpre_k * cw_ref[3 - k:4 - k, cs_]
                gconv_ref[8 * (3 - k):8 * (4 - k), cs_] += _rsum8(dpre_k * xraw)
            dzx_ref[:, zc] = dx.astype(dzx_ref.dtype)
            dp8_scr[:, cs_] = dpre[0:8]
            gconv_ref[32:40, cs_] += _rsum8(dpre)

    rev = lambda i: ns - 1 - i
    full = lambda rows, cols: pl.BlockSpec((rows, cols), lambda i: (0, 0))
    return pl.pallas_call(
        body, name=name, grid=(ns,),
        in_specs=[pl.BlockSpec((rb, ZX_W), lambda i: (rev(i), 0)),
                  pl.BlockSpec((rb, DT_PAD), lambda i: (rev(i), 0)),
                  pl.BlockSpec((rb, SSD_INNER), lambda i: (rev(i), 0)),
                  pl.BlockSpec((cb, N_STATE, SSD_INNER), lambda i: (rev(i), 0, 0)),
                  pl.BlockSpec((rb, XBC), lambda i: (rev(i), 0)), pl.BlockSpec((rb, XBC), lambda i: (rev(i), 0)),
                  pl.BlockSpec((rb, SSD_INNER), lambda i: (rev(i), 0)),
                  full(4, XBC), full(1, DT_PAD), full(1, DT_PAD), full(1, SSD_INNER), full(1, SSD_INNER),
                  pl.BlockSpec(memory_space=pl.ANY)] + c_in_specs,
        out_specs=[pl.BlockSpec((rb, ZX_W), lambda i: (rev(i), 0)), pl.BlockSpec((rb, DT_PAD), lambda i: (rev(i), 0)),
                   full(40, XBC), full(16, SSD_INNER), full(8, DT_PAD)] + c_out_specs,
        out_shape=[jax.ShapeDtypeStruct(dproj.shape, dproj.dtype), jax.ShapeDtypeStruct((t, DT_PAD), _MXU),
                   jax.ShapeDtypeStruct((40, XBC), F32), jax.ShapeDtypeStruct((16, SSD_INNER), F32),
                   jax.ShapeDtypeStruct((8, DT_PAD), F32)] + c_out_shapes,
        scratch_shapes=[pltpu.VMEM((N_STATE, SSD_INNER), F32),
                        pltpu.VMEM((rb, SSD_INNER), F32), pltpu.VMEM((rb, XBC), F32), pltpu.VMEM((rb, DT_PAD), F32),
                        pltpu.VMEM((rb, DT_PAD), F32), pltpu.VMEM((128, 128), F32), pltpu.VMEM((8, XBC), F32)] + c_sems,
        input_output_aliases={n_in - 1: 0},
        compiler_params=_cp(("arbitrary",)),
    )(proj, dtraw, yssd, states, xc, dsl, dyb, sw["cw"], sw["dtb"], sw["alog"], sw["dsk"], sw["ng"], dproj, *c_args)


def _branch_merge(ya, yb, proj, wba, wbb, bgate, *, tm, tn, name):
    t = ya.shape[0]
    nj = D // tn

    def body(ya_ref, yb_ref, ga_ref, gb_ref, wba_ref, wbb_ref, ba_ref, bb_ref, ta_ref, tb_ref, mg_ref):
        ta = _dot(ya_ref[...], wba_ref[...])
        tb = _dot(yb_ref[...], wbb_ref[...])
        ta_ref[...] = ta.astype(ta_ref.dtype)
        tb_ref[...] = tb.astype(tb_ref.dtype)
        ga = _sigmoid(ga_ref[...] + ba_ref[...])
        gb = _sigmoid(gb_ref[...] + bb_ref[...])
        mg_ref[...] = (ga * ta + gb * tb).astype(mg_ref.dtype)

    tile = pl.BlockSpec((tm, tn), lambda i, j: (i, j))
    return pl.pallas_call(
        body, name=name, grid=(t // tm, nj),
        in_specs=[pl.BlockSpec((tm, D), lambda i, j: (i, 0)), pl.BlockSpec((tm, SSD_INNER), lambda i, j: (i, 0)),
                  pl.BlockSpec((tm, tn), lambda i, j: (i, G0 // tn + j)),
                  pl.BlockSpec((tm, tn), lambda i, j: (i, (G0 + D) // tn + j)),
                  pl.BlockSpec((D, tn), lambda i, j: (0, j)), pl.BlockSpec((SSD_INNER, tn), lambda i, j: (0, j)),
                  pl.BlockSpec((1, tn), lambda i, j: (0, j)), pl.BlockSpec((1, tn), lambda i, j: (0, nj + j))],
        out_specs=[tile, tile, tile],
        out_shape=[jax.ShapeDtypeStruct((t, D), _MXU)] * 3,
        compiler_params=_cp(("parallel", "parallel")),
    )(ya, yb, proj, proj, wba, wbb, bgate, bgate)


def _swiglu_mm(gu, wfo, residual, *, tm, tn, name):
    t = gu.shape[0]

    def body(gu_ref, w_ref, r_ref, act_ref, o_ref):
        @pl.when(pl.program_id(1) == 0)
        def _():
            gate = gu_ref[:, 0:D_FF].astype(F32)
            act_ref[...] = (_silu(gate) * gu_ref[:, D_FF:2 * D_FF].astype(F32)).astype(act_ref.dtype)
        o_ref[...] = jnp.dot(act_ref[...], w_ref[...], preferred_element_type=F32) + r_ref[...]

    return pl.pallas_call(
        body, name=name, grid=(t // tm, D // tn),
        in_specs=[pl.BlockSpec((tm, 2 * D_FF), lambda i, j: (i, 0)), pl.BlockSpec((D_FF, tn), lambda i, j: (0, j)),
                  pl.BlockSpec((tm, tn), lambda i, j: (i, j))],
        out_specs=[pl.BlockSpec((tm, D_FF), lambda i, j: (i, 0)), pl.BlockSpec((tm, tn), lambda i, j: (i, j))],
        out_shape=[jax.ShapeDtypeStruct((t, D_FF), _MXU), jax.ShapeDtypeStruct((t, D), F32)],
        compiler_params=_cp(("parallel", "arbitrary")),
    )(gu, wfo, residual)


def _ffn_bwd_act(dh, wfo, gu, *, tm, name):
    t = dh.shape[0]

    def body(dh_ref, w_ref, gu_ref, o_ref):
        dact = _dot_nt(dh_ref[...], w_ref[...])
        g = gu_ref[:, 0:D_FF].astype(F32)
        u = gu_ref[:, D_FF:2 * D_FF].astype(F32)
        sg = _sigmoid(g)
        o_ref[:, 0:D_FF] = (dact * u * (sg * (1.0 + g * (1.0 - sg)))).astype(o_ref.dtype)
        o_ref[:, D_FF:2 * D_FF] = (dact * (g * sg)).astype(o_ref.dtype)

    return pl.pallas_call(
        body, name=name, grid=(t // tm,),
        in_specs=[pl.BlockSpec((tm, D), lambda i: (i, 0)), pl.BlockSpec((D_FF, D), lambda i: (0, 0)),
                  pl.BlockSpec((tm, 2 * D_FF), lambda i: (i, 0))],
        out_specs=pl.BlockSpec((tm, 2 * D_FF), lambda i: (i, 0)),
        out_shape=jax.ShapeDtypeStruct((t, 2 * D_FF), _MXU),
        compiler_params=_cp(("parallel",)),
    )(dh, wfo, gu)


def _outproj_bwd(dh, wout, ta, tb, proj, bgate, dproj, *, tm, name):
    t = dh.shape[0]

    def body(dh_ref, w_ref, ta_ref, tb_ref, g_ref, b_ref, dta_ref, dtb_ref, dg_ref, db_ref):
        @pl.when(pl.program_id(0) == 0)
        def _():
            db_ref[...] = jnp.zeros_like(db_ref)
        dm = _dot_nt(dh_ref[...], w_ref[...])
        ga = _sigmoid(g_ref[:, 0:D] + b_ref[:, 0:D])
        gb = _sigmoid(g_ref[:, D:2 * D] + b_ref[:, D:2 * D])
        dta_ref[...] = (dm * ga).astype(dta_ref.dtype)
        dtb_ref[...] = (dm * gb).astype(dtb_ref.dtype)
        dga = dm * ta_ref[...].astype(F32) * ga * (1.0 - ga)
        dgb = dm * tb_ref[...].astype(F32) * gb * (1.0 - gb)
        dg_ref[:, 0:D] = dga.astype(dg_ref.dtype)
        dg_ref[:, D:2 * D] = dgb.astype(dg_ref.dtype)
        db_ref[0:1, 0:D] += _rsum(dga)
        db_ref[0:1, D:2 * D] += _rsum(dgb)

    row = lambda cols: pl.BlockSpec((tm, cols), lambda i: (i, 0))
    return pl.pallas_call(
        body, name=name, grid=(t // tm,),
        in_specs=[row(D), pl.BlockSpec((D, D), lambda i: (0, 0)), row(D), row(D),
                  pl.BlockSpec((tm, 2 * D), lambda i: (i, G0 // (2 * D))), pl.BlockSpec((1, 2 * D), lambda i: (0, 0))],
        out_specs=[row(D), row(D), pl.BlockSpec((tm, 2 * D), lambda i: (i, G0 // (2 * D))),
                   pl.BlockSpec((8, 2 * D), lambda i: (0, 0))],
        out_shape=[jax.ShapeDtypeStruct((t, D), _MXU), jax.ShapeDtypeStruct((t, D), _MXU),
                   jax.ShapeDtypeStruct(dproj, _MXU), jax.ShapeDtypeStruct((8, 2 * D), F32)],
        compiler_params=_cp(("arbitrary",)),
    )(dh, wout, ta, tb, proj, bgate)


def _loss_head(h, gf, target, *, tm, name):
    t = h.shape[0]

    def body(h_ref, g_ref, t_ref, loss_ref, dg_ref, dh_ref):
        @pl.when(pl.program_id(0) == 0)
        def _():
            loss_ref[...] = jnp.zeros_like(loss_ref)
            dg_ref[...] = jnp.zeros_like(dg_ref)
        x = h_ref[...]
        r = lax.rsqrt(jnp.mean(x * x, axis=-1, keepdims=True) + EPS)
        xh = x * r
        err = xh * g_ref[...] - t_ref[...]
        loss_ref[...] += 0.5 * jnp.sum(jnp.mean(err * err, axis=-1, keepdims=True), axis=0, keepdims=True)
        dy = err * (1.0 / D)
        dg_ref[0:1, :] += _rsum(dy * xh)
        dxh = dy * g_ref[...]
        dh_ref[...] = r * (dxh - xh * jnp.mean(dxh * xh, axis=-1, keepdims=True))

    row = pl.BlockSpec((tm, D), lambda i: (i, 0))
    return pl.pallas_call(
        body, name=name, grid=(t // tm,),
        in_specs=[row, pl.BlockSpec((1, D), lambda i: (0, 0)), row],
        out_specs=[pl.BlockSpec((8, 128), lambda i: (0, 0)), pl.BlockSpec((8, D), lambda i: (0, 0)), row],
        out_shape=[jax.ShapeDtypeStruct((8, 128), F32), jax.ShapeDtypeStruct((8, D), F32), jax.ShapeDtypeStruct((t, D), F32)],
        compiler_params=_cp(("arbitrary",)),
    )(h, gf, target)


def _row_tile(rows, cols, limit_bytes=1 << 20):
    best = None
    for tr in range(8, rows + 1, 8):
        if rows % tr == 0 and tr * cols * 4 <= limit_bytes:
            best = tr
    return best if best is not None else rows


def _adamw(w, g, m, v, *, name):
    rows, cols = w.shape
    tr = _row_tile(rows, cols)

    def body(w_ref, g_ref, m_ref, v_ref, d_ref, nm_ref, nv_ref):
        gv = g_ref[...]
        nm = ADAM_B1 * m_ref[...] + (1.0 - ADAM_B1) * gv
        nv = ADAM_B2 * v_ref[...] + (1.0 - ADAM_B2) * (gv * gv)
        m_hat = nm / (1.0 - ADAM_B1 ** ADAM_STEP)
        v_hat = nv / (1.0 - ADAM_B2 ** ADAM_STEP)
        d_ref[...] = -ADAM_LR * (m_hat / (jnp.sqrt(v_hat) + ADAM_EPS) + ADAM_WD * w_ref[...])
        nm_ref[...] = nm
        nv_ref[...] = nv

    blk = pl.BlockSpec((tr, cols), lambda i: (i, 0))
    shp = jax.ShapeDtypeStruct((rows, cols), F32)
    return pl.pallas_call(
        body, name=name, grid=(rows // tr,), in_specs=[blk] * 4, out_specs=[blk] * 3, out_shape=[shp] * 3,
        compiler_params=_cp(("parallel",)),
    )(w, g, m, v)


def _bd256(w):
    w4 = w.reshape(4, 4, 64, 64)
    eye = jnp.eye(4, dtype=w.dtype)
    return (w4[:, :, :, None, :] * eye[None, :, None, :, None]).reshape(4, 256, 256)


def _bd256_diag(g):
    g5 = g.reshape(4, 4, 64, 4, 64)
    return jnp.stack([g5[:, a, :, a, :] for a in range(4)], axis=1).reshape(16, 64, 64)


FFN_SHARD = 2 * D_FF // 4
W_IN_SHARD = IN_DIM // 4
W_IN_ROWS = 9344


def _w_in_cols(shards, c0, c1):
    out = []
    for p in range(4):
        lo, hi = max(c0, W_IN_SHARD * p), min(c1, W_IN_SHARD * (p + 1))
        if lo < hi:
            out.append(shards[p][:, lo - W_IN_SHARD * p:hi - W_IN_SHARD * p])
    return out


def _in_proj_weights(win):
    lblk = [_w_in_cols(win, 256 * j, 256 * (j + 1)) + _w_in_cols(win, D + 256 * j, D + 256 * (j + 1)) for j in range(4)]
    wp = jnp.concatenate(_w_in_cols(win, 2048, 4096) + _w_in_cols(win, 4096, 7168) + lblk[0] + lblk[1]
                         + _w_in_cols(win, 7200, 9248) + lblk[2] + lblk[3], axis=1)
    wdt = jnp.pad(jnp.concatenate(_w_in_cols(win, 7168, 7200), axis=1), ((0, 0), (0, DT_PAD - N_HEADS)))
    return wp, wdt


def _layer_weights(w, conv, small, l, wp, wdt):
    row = lambda v: v.reshape(1, -1)
    pad_h = lambda v: jnp.pad(v.reshape(1, -1), ((0, 0), (0, DT_PAD - N_HEADS)))
    lw = dict(cw=conv["lru_conv_w"][l], cb=row(small["lru_conv_b"][l]),
              wa=_bd256(small["lru_w_a"][l]).astype(_MXU), wx=_bd256(small["lru_w_x"][l]).astype(_MXU),
              ba=row(small["lru_b_a"][l]), bx=row(small["lru_b_x"][l]), lam=row(small["lru_lambda"][l]))
    sw = dict(cw=conv["ssd_conv_w"][l], cb=row(small["ssd_conv_b"][l]), dtb=pad_h(small["ssd_dt_bias"][l]),
              alog=pad_h(small["ssd_A_log"][l]), dsk=row(jnp.repeat(small["ssd_D"][l], HEAD_P)),
              ng=row(small["ssd_norm_g"][l]))
    return dict(wp=wp, wdt=wdt, lw=lw, sw=sw, wba=w["w_branch"][0:D], wbb=w["w_branch"][D:3 * D],
                wout=w["w_out"], wfi=w["w_ffn_in"], wfo=w["w_ffn_out"],
                g1=row(small["norm1_g"][l]), g2=row(small["norm2_g"][l]), bgate=row(small["b_gate"][l]))


def _tiles(t):
    return dict(tmi=min(2048, t), tmn=min(1024, t), tm=min(512, t), r=min(256, t), rb=min(128, t))


def _layer_fwd(h, w, conv, small, l, carried=None):
    tl = _tiles(h.shape[0])
    n = f"l{l}_"
    carried = carried or {}
    arrived = []

    def carry(kernel, key, n_main, *args, **kw):
        comm, finish = carried.get(key, (None, None))
        outs = list(kernel(*args, comm=comm, **kw))
        if comm is not None:
            arrived.append(finish(outs[n_main:]))
        return outs[:n_main]

    wp, wdt = _in_proj_weights(w["w_in"])
    xn, proj = carry(_norm_mm, "in_proj", 2, h, small["norm1_g"][l].reshape(1, -1), wp, tm=tl["tmi"], tn=1024,
                     name=n + "in_proj")
    w = dict(w)
    for layer, ws in arrived:
        if layer == l:
            w.update(ws)
    lwt = _layer_weights(w, conv, small, l, wp, wdt)
    dtraw = _mm_nn(xn, lwt["wdt"], tm=tl["tm"], tn=DT_PAD, name=n + "dt_proj")
    hl, ya, gates = carry(_lru_fwd, "lru", 3, proj, lwt["lw"], r=tl["r"], name=n + "lru_fwd")
    yssd, yb, states, xc, dsl = carry(_ssd_fwd, "ssd", 5, proj, dtraw, lwt["sw"], rb=tl["rb"], name=n + "ssd_fwd")
    ta, tb, merged = _branch_merge(ya, yb, proj, lwt["wba"], lwt["wbb"], lwt["bgate"], tm=tl["tmn"], tn=512, name=n + "merge")
    hmid = _mm_nn(merged, lwt["wout"], tm=tl["tmn"], tn=512, name=n + "out_proj", residual=h)
    xn2, gu = _norm_mm(hmid, lwt["g2"], lwt["wfi"], tm=tl["tmi"], tn=FFN_SHARD, name=n + "ffn_in", out_dtype=_MXU)
    act, hout = _swiglu_mm(gu, lwt["wfo"], hmid, tm=tl["tm"], tn=512, name=n + "ffn_out")
    saved = dict(h=h, xn=xn, proj=proj, dtraw=dtraw, hl=hl, ya=ya, gates=gates, yssd=yssd, yb=yb, states=states, xc=xc, dsl=dsl, ta=ta, tb=tb,
                 merged=merged, hmid=hmid, xn2=xn2, gu=gu, act=act)
    return hout, saved, lwt, [x for x in arrived if x[0] != l]


def _layer_bwd(dh, s, lwt, l, hooks=None):
    t = dh.shape[0]
    tl = _tiles(t)
    n = f"l{l}_"
    tt = tl["tmn"]
    big = {}
    hooks = hooks or {}

    def wgrad(key, a, b, name, **kw):
        big[key] = _wgrad(a, b, tt=tt, name=n + name, into=big.get(key), **kw)

    dgu = _ffn_bwd_act(dh, lwt["wfo"], s["gu"], tm=tl["tm"], name=n + "ffn_act_bwd")
    wgrad("w_ffn_out", s["act"], dh, "ffn_out_wgrad", ta=D_FF, tn=1024, out_shape=(D_FF, D),
          out_block=(D_FF, 1024), out_index=lambda o, j: (o, j))
    wgrad("w_ffn_in", s["xn2"], dgu, "ffn_in_wgrad", ta=D, tn=FFN_SHARD, out_shape=(4, D, FFN_SHARD),
          out_block=(None, D, FFN_SHARD), out_index=lambda o, j: (j, o, 0))
    dh1, dg2 = _mm_nt_rmsbwd(dgu, lwt["wfi"], s["hmid"], lwt["g2"], dh, tm=tl["tm"], tk=FFN_SHARD, name=n + "ffn_in_dgrad")
    dta, dtb, dproj, dbg = _outproj_bwd(dh1, lwt["wout"], s["ta"], s["tb"], s["proj"], lwt["bgate"], (t, NP),
                                        tm=tl["tm"], name=n + "out_proj_bwd")
    rows_d = dict(ta=D, tn=512, out_block=(D, 512), out_index=lambda o, j: (o, j))
    wgrad("w_out", s["merged"], dh1, "out_proj_wgrad", out_shape=(D, D), **rows_d)
    dya = _mm_nt(dta, lwt["wba"], tm=tl["tm"], name=n + "branch_a_dgrad")
    dyb = _mm_nt(dtb, lwt["wbb"], tm=tl["tm"], name=n + "branch_b_dgrad")
    wgrad("w_branch", s["ya"], dta, "branch_a_wgrad", out_shape=(3 * D, D), a_tab=[0], o_tab=[0], **rows_d)
    wgrad("w_branch", s["yb"], dtb, "branch_b_wgrad", out_shape=(3 * D, D), a_tab=[0, 1], o_tab=[1, 2], **rows_d)
    comm_1 = hooks["lru"](big) if "lru" in hooks else None
    dproj, lsm, dwa, dwx, *got_1 = _lru_bwd(s["proj"], s["hl"], s["gates"], dya, dproj, lwt["lw"], r=tl["r"], name=n + "lru_bwd",
                                            comm=comm_1)
    comm_2 = hooks["ssd"](got_1) if "ssd" in hooks else None
    dproj, ddt, gconv, gch, ghd, *got_2 = _ssd_bwd(s["proj"], s["dtraw"], s["yssd"], s["states"], s["xc"], s["dsl"], dyb, dproj, lwt["sw"],
                                                   rb=tl["rb"], name=n + "ssd_bwd", comm=comm_2)
    lsm = lsm.reshape(8, 8, D).sum(axis=1)
    gconv = gconv.reshape(5, 8, XBC).sum(axis=1)
    gch = gch.reshape(2, 8, SSD_INNER).sum(axis=1)
    w_in = dict(tn=D, out_shape=(W_IN_ROWS, D), out_index=lambda o, j: (o, j))
    wgrad("w_in", dproj, s["xn"], "in_proj_wgrad", ta=1024, out_block=(1024, D),
          a_tab=list(range(9)), o_tab=[2, 3, 4, 5, 6, 0, 7, 8, 1], **w_in)
    wgrad("w_in", ddt, s["xn"], "dt_proj_wgrad", ta=DT_PAD, out_block=(DT_PAD, D), a_tab=[0],
          o_tab=[NP // DT_PAD], **w_in)
    comm_3 = hooks["in_dgrad"](big) if "in_dgrad" in hooks else None
    dh0, dg1, *got_3 = _mm_nt_rmsbwd(dproj, lwt["wp"], s["h"], lwt["g1"], dh1, tm=tl["tm"], tk=4608,
                                     name=n + "in_proj_dgrad", extra=(ddt, lwt["wdt"]), comm=comm_3)
    grads = dict(
        lru_conv_w=lsm[0:4], lru_conv_b=lsm[4], lru_b_a=lsm[5], lru_b_x=lsm[6], lru_lambda=lsm[7],
        lru_w_a=_bd256_diag(dwa), lru_w_x=_bd256_diag(dwx),
        ssd_conv_w=gconv[0:4], ssd_conv_b=gconv[4], ssd_norm_g=gch[0], ssd_D=gch[1].reshape(N_HEADS, HEAD_P).sum(axis=-1),
        ssd_dt_bias=ghd[0, 0:N_HEADS], ssd_A_log=ghd[1, 0:N_HEADS],
        b_gate=dbg[0], norm1_g=dg1[0], norm2_g=dg2[0])
    return dh0, grads, big, (got_2, got_3)


def _local_step(x, target, w, conv, small, prefetch=None, early_reduce=None):
    h = x
    w = [dict(wl) for wl in w]
    lwts, saved = [], []
    for l in range(N_LAYERS):
        h, s, lwt, arrived = _layer_fwd(h, w[l], conv, small, l, prefetch if l == 0 else None)
        for layer, ws in arrived:
            w[layer].update(ws)
        lwts.append(lwt)
        saved.append(s)
    loss_blk, dgf, dh = _loss_head(h, small["norm_f"].reshape(1, D), target, tm=_tiles(x.shape[0])["tm"], name="loss_head")
    per_layer, big, carried = [None] * N_LAYERS, [None] * N_LAYERS, None
    for l in reversed(range(N_LAYERS)):
        hooks = early_reduce(big[1]) if (early_reduce is not None and l == 0) else None
        dh, per_layer[l], big[l], carried = _layer_bwd(dh, saved[l], lwts[l], l, hooks)
    grads = {k: jnp.stack([per_layer[l][k] for l in range(N_LAYERS)], axis=0) for k in per_layer[0]}
    grads["norm_f"] = dgf[0]
    return loss_blk, dh, grads, big, carried


PACK_W = 1024
BIG = (("w_in", W_IN_SHARD, D, W_IN_SHARD, 256), ("w_branch", 768, D, 256, D), ("w_out", 256, D, 256, D),
       ("w_ffn_in", D, FFN_SHARD, 256, FFN_SHARD), ("w_ffn_out", 704, D, 352, D))
CONV = ("lru_conv_w", "ssd_conv_w")
SMALL = ("norm1_g", "b_gate", "lru_conv_b", "lru_w_a", "lru_b_a", "lru_w_x", "lru_b_x", "lru_lambda", "ssd_conv_b",
         "ssd_dt_bias", "ssd_A_log", "ssd_D", "ssd_norm_g", "norm2_g", "norm_f")
_WIRE = jnp.bfloat16
N_CHIPS = 4
N_DEV = 8


def _mesh_pos():
    return lax.axis_index("x"), lax.axis_index("y"), lax.axis_index("c")


HBM_SPEC = pl.BlockSpec(memory_space=pltpu.HBM)


def _remote(src, dst, send_sems, recv_sems, k, to):
    return pltpu.make_async_remote_copy(src_ref=src, dst_ref=dst, send_sem=send_sems.at[k], recv_sem=recv_sems.at[k],
                                        device_id=to, device_id_type=MESH)


def _other_chips(x, y):
    return [(1 - x, y), (x, 1 - y), (1 - x, 1 - y)]


def _weight_fetch(loc, layer, owner):
    names = list(owner)
    rows = {n: loc[n].shape[1] for n in names}
    by_chip = ("w_in", "w_ffn_in")
    shapes = [((N_CHIPS,) + loc[n].shape[1:]) if n in by_chip else (N_CHIPS * rows[n], D) for n in names]

    def place(o_ref, n, chip):
        if n in by_chip:
            return o_ref.at[chip]
        return o_ref.at[pl.ds(pl.multiple_of(chip * rows[n], 16), rows[n]), :]

    def step(which, in_refs, o_refs, send_sems, recv_sems):
        x, y, c = _mesh_pos()
        s = 2 * x + y
        sib = (x, y, 1 - c)
        chips = _other_chips(x, y)
        for core in (0, 1):
            @pl.when(c == core)
            def _():
                for k, n in enumerate(names):
                    for j, (px, py) in enumerate(chips):
                        landed = place(o_refs[k], n, 2 * px + py)
                        sent = _remote(in_refs[k].at[layer], place(o_refs[k], n, s), send_sems, recv_sems, 3 * k + j,
                                       (px, py, c))
                        arrives = _remote(in_refs[k].at[layer], landed, send_sems, recv_sems, 3 * k + j, (px, py, c))
                        passed = _remote(landed, landed, send_sems, recv_sems, 3 * (len(names) + k) + j, sib)
                        if owner[n] == core:
                            if which == "start":
                                sent.start()
                            elif which == "mid":
                                arrives.wait_recv()
                                passed.start()
                            else:
                                sent.wait_send()
                                passed.wait_send()
                        elif which == "end":
                            passed.wait_recv()

    return dict(inputs=[loc[n] for n in names], names=names,
                out_shapes=[jax.ShapeDtypeStruct(shp, loc[n].dtype) for shp, n in zip(shapes, names)],
                sems=[pltpu.SemaphoreType.DMA((6 * len(names),)), pltpu.SemaphoreType.DMA((6 * len(names),))],
                start=functools.partial(step, "start"), mid=functools.partial(step, "mid"),
                end=functools.partial(step, "end"))


def _comm_now(comm, name):
    n, no = len(comm["inputs"]), len(comm["out_shapes"])

    def body(*refs):
        parts = (refs[:n], refs[n:n + no]) + tuple(refs[n + no:])
        comm["start"](*parts)
        comm["mid"](*parts)
        comm["end"](*parts)

    return pl.pallas_call(
        body, name=name, in_specs=[HBM_SPEC] * n, out_specs=[HBM_SPEC] * no, out_shape=comm["out_shapes"],
        scratch_shapes=comm["sems"],
    )(*comm["inputs"])


def _sibling_send(bufs, layer):
    n = len(bufs)

    def step(which, in_refs, o_refs, send_sems, recv_sems):
        x, y, c = _mesh_pos()
        copies = [_remote(in_refs[k], o_refs[k], send_sems, recv_sems, k, (x, y, 1 - c)) for k in range(n)]

        @pl.when(c != layer)
        def _():
            for cp in copies:
                if which == "start":
                    cp.start()
                elif which == "end":
                    cp.wait_send()

        @pl.when(c == layer)
        def _():
            for cp in copies:
                if which == "end":
                    cp.wait_recv()

    return dict(inputs=list(bufs), out_shapes=[jax.ShapeDtypeStruct(b.shape, b.dtype) for b in bufs],
                sems=[pltpu.SemaphoreType.DMA((n,)), pltpu.SemaphoreType.DMA((n,))],
                start=functools.partial(step, "start"), mid=functools.partial(step, "mid"),
                end=functools.partial(step, "end"))


def _add_cast(g, recv, own, *, a, tr, tc, name):
    wd = g.shape[1]
    nr = a // tr

    def body(own_ref, g_ref, r_ref, o_ref):
        @pl.when(own_ref[0] == 1)
        def _():
            o_ref[...] = (g_ref[...] + r_ref[...]).astype(o_ref.dtype)

    blk = pl.BlockSpec((tr, tc), lambda p, i, j, own_ref: ((p * nr + i) * own_ref[0], j * own_ref[0]))
    return pl.pallas_call(
        body, name=name,
        grid_spec=pltpu.PrefetchScalarGridSpec(
            num_scalar_prefetch=1, grid=(N_CHIPS, nr, wd // tc), in_specs=[blk, blk],
            out_specs=pl.BlockSpec((None, tr, tc), lambda p, i, j, own_ref: (p * own_ref[0], i * own_ref[0], j * own_ref[0]))),
        out_shape=jax.ShapeDtypeStruct((N_CHIPS, a, wd), _WIRE),
        compiler_params=_cp(("arbitrary", "arbitrary", "arbitrary")),
    )(own, g, recv)


def _chip_exchange(parts, layer):
    n = len(parts)

    def step(which, s_refs, o_refs, send_sems, recv_sems):
        x, y, c = _mesh_pos()
        s = 2 * x + y

        @pl.when(c == layer)
        def _():
            for j, (px, py) in enumerate(_other_chips(x, y)):
                for k in range(n):
                    p = 2 * px + py
                    sent = _remote(s_refs[k].at[p], o_refs[k].at[s], send_sems, recv_sems, n * j + k, (px, py, c))
                    if which == "start":
                        sent.start()
                    elif which == "end":
                        _remote(s_refs[k].at[p], o_refs[k].at[p], send_sems, recv_sems, n * j + k, (px, py, c)).wait_recv()
                        sent.wait_send()

    return dict(inputs=list(parts), out_shapes=[jax.ShapeDtypeStruct(p.shape, p.dtype) for p in parts],
                sems=[pltpu.SemaphoreType.DMA((3 * n,)), pltpu.SemaphoreType.DMA((3 * n,))],
                start=functools.partial(step, "start"), mid=functools.partial(step, "mid"),
                end=functools.partial(step, "end"))


def _sum_slots(slots, own, sel, *, tr, tc, name, layer=None, into=None):
    n, rows, wd = slots.shape
    k = own.shape[0]

    def body(sel_ref, s_ref, own_ref, *rest):
        o_ref = rest[-1]

        @pl.when(sel_ref[1] == 1)
        def _():
            mine = sel_ref[0]
            acc = jnp.zeros((tr, tc), F32)
            for p in range(n):
                acc = acc + jnp.where(mine == p, own_ref[...].astype(F32), s_ref[p].astype(F32))
            o_ref[...] = acc

    if layer is not None:
        out_spec = pl.BlockSpec((None, tr, tc), lambda i, j, sel_ref: (layer, i * sel_ref[1], j * sel_ref[1]))
        out_shape = jax.ShapeDtypeStruct((N_LAYERS, rows, wd), F32)
    else:
        out_spec = pl.BlockSpec((tr, tc), lambda i, j, sel_ref: (i * sel_ref[1], j * sel_ref[1]))
        out_shape = jax.ShapeDtypeStruct((rows, wd), F32)
    in_specs = [pl.BlockSpec((n, tr, tc), lambda i, j, sel_ref: (0, i * sel_ref[1], j * sel_ref[1])),
                pl.BlockSpec((None, tr, tc), lambda i, j, sel_ref: (sel_ref[0] if k > 1 else 0, i * sel_ref[1],
                                                                    j * sel_ref[1]))]
    args = [sel, slots, own]
    if into is not None:
        in_specs.append(pl.BlockSpec(memory_space=pl.ANY))
        args.append(into)
    return pl.pallas_call(
        body, name=name,
        grid_spec=pltpu.PrefetchScalarGridSpec(num_scalar_prefetch=1, grid=(rows // tr, wd // tc), in_specs=in_specs,
                                               out_specs=out_spec),
        out_shape=out_shape, input_output_aliases={3: 0} if into is not None else {},
        compiler_params=_cp(("arbitrary", "arbitrary")),
    )(*args)


def _sibling_share(both):
    n = len(both)

    def body(*refs):
        o_refs, (send_sems, recv_sems) = refs[n:2 * n], refs[2 * n:]
        x, y, c = _mesh_pos()
        sends = [_remote(o_refs[k].at[c], o_refs[k].at[c], send_sems, recv_sems, k, (x, y, 1 - c)) for k in range(n)]
        for cp in sends:
            cp.start()
        for k in range(n):
            _remote(o_refs[k].at[1 - c], o_refs[k].at[1 - c], send_sems, recv_sems, k, (x, y, 1 - c)).wait_recv()
        for cp in sends:
            cp.wait_send()

    return pl.pallas_call(
        body, name="grad_sibling_share", in_specs=[HBM_SPEC] * n, out_specs=[HBM_SPEC] * n,
        out_shape=[jax.ShapeDtypeStruct(b.shape, b.dtype) for b in both], input_output_aliases={k: k for k in range(n)},
        scratch_shapes=[pltpu.SemaphoreType.DMA((n,)), pltpu.SemaphoreType.DMA((n,))],
    )(*both)


def _allgather_devices(part):
    rows, wd = part.shape

    def step(which, in_refs, o_refs, send_sems, recv_sems):
        (p_ref,), (o_ref,) = in_refs, o_refs
        x, y, c = _mesh_pos()
        sib = (x, y, 1 - c)
        chips = _other_chips(x, y)
        slot = lambda px, py, pc: o_ref.at[4 * px + 2 * py + pc]
        first = [_remote(p_ref, slot(x, y, c), send_sems, recv_sems, 0, sib)]
        first += [_remote(p_ref, slot(x, y, c), send_sems, recv_sems, 1 + j, (px, py, c)) for j, (px, py) in enumerate(chips)]
        passed = [_remote(slot(px, py, c), slot(px, py, c), send_sems, recv_sems, 4 + j, sib)
                  for j, (px, py) in enumerate(chips)]
        if which == "start":
            for cp in first:
                cp.start()
        elif which == "mid":
            for j, (px, py) in enumerate(chips):
                _remote(p_ref, slot(px, py, c), send_sems, recv_sems, 1 + j, (px, py, c)).wait_recv()
                passed[j].start()
        else:
            _remote(p_ref, slot(x, y, 1 - c), send_sems, recv_sems, 0, sib).wait_recv()
            for j, (px, py) in enumerate(chips):
                _remote(slot(px, py, 1 - c), slot(px, py, 1 - c), send_sems, recv_sems, 4 + j, sib).wait_recv()
            for cp in first + passed:
                cp.wait_send()

    return dict(inputs=[part], out_shapes=[jax.ShapeDtypeStruct((N_DEV, rows, wd), part.dtype)],
                sems=[pltpu.SemaphoreType.DMA((N_DEV - 1,)), pltpu.SemaphoreType.DMA((N_DEV - 1,))],
                start=functools.partial(step, "start"), mid=functools.partial(step, "mid"),
                end=functools.partial(step, "end"))


def _comm_both(a, b):
    na, nao = len(a["inputs"]), len(a["out_shapes"])

    def step(which, in_refs, o_refs, sa, ra, sb, rb_):
        a[which](in_refs[:na], o_refs[:nao], sa, ra)
        b[which](in_refs[na:], o_refs[nao:], sb, rb_)

    return dict(inputs=a["inputs"] + b["inputs"], out_shapes=a["out_shapes"] + b["out_shapes"], sems=a["sems"] + b["sems"],
                start=functools.partial(step, "start"), mid=functools.partial(step, "mid"),
                end=functools.partial(step, "end"))


def _by_chip_to_full(stack):
    _, nl, r, b = stack.shape
    return stack.transpose(1, 2, 0, 3).reshape(nl, r, N_CHIPS * b)


def _sharded_step(a):
    x = a["x"][0]
    target = a["loss_target"][0]
    cx, cy, cc = _mesh_pos()
    chip = (2 * cx + cy).astype(jnp.int32)
    core = cc.astype(jnp.int32)
    me = (4 * cx + 2 * cy + cc).astype(jnp.int32)
    zero = jnp.zeros((), jnp.int32)
    dus = lax.dynamic_update_slice

    loc = {n: a[n].astype(_MXU) for n, *_ in BIG}

    def with_own(got, names, layer):
        out = {}
        for g, n in zip(got, names):
            mine = loc[n][layer]
            out[n] = (dus(g, mine[None], (chip, zero, zero)) if g.ndim == 3 else dus(g, mine, (chip * mine.shape[0], zero)))
        return out

    rest = {"w_ffn_in": 0, "w_branch": 1, "w_out": 1, "w_ffn_out": 1}
    conv_loc = jnp.concatenate([a[n].reshape(-1, PACK_W) for n in CONV], axis=0)
    now = _weight_fetch(loc, 0, {"w_in": 0})
    conv_all, *got_now = _comm_now(_comm_both(_allgather_devices(conv_loc), now), "allgather_weights")
    w0 = with_own(got_now, now["names"], 0)
    later = {"in_proj": (0, _weight_fetch(loc, 0, rest)), "lru": (1, _weight_fetch(loc, 1, {"w_in": 0})),
             "ssd": (1, _weight_fetch(loc, 1, rest))}
    prefetch = {k: (f, functools.partial(lambda got, layer, f: (layer, with_own(got, f["names"], layer)), layer=layer, f=f))
                for k, (layer, f) in later.items()}
    conv_all = dus(conv_all, conv_loc[None], (me, zero, zero))[0::2]
    conv, off = {}, 0
    for n in CONV:
        rows = a[n].size // PACK_W
        conv[n] = _by_chip_to_full(conv_all[:, off:off + rows].reshape((N_CHIPS,) + a[n].shape))
        off += rows
    small = {n: a[n] for n in SMALL}

    views = lambda big_l, specs: [big_l[n].reshape(-1, wd) for n, _, wd, _, _ in specs]
    owns = lambda layer: (core == layer).astype(jnp.int32)
    w_in_only, others = BIG[:1], BIG[1:]

    def partial_sums(big_l, recv, layer, specs):
        return [_add_cast(v, r, owns(layer).reshape(1), a=rows, tr=tr, tc=tc, name=f"grad_add_sibling_l{layer}_{n}")
                for v, r, (n, rows, _, tr, tc) in zip(views(big_l, specs), recv, specs)]

    def reduced(slots, parts, layer, into, specs):
        sel = jnp.stack([chip, owns(layer)])
        return [_sum_slots(s, p, sel, tr=tr, tc=tc, name=f"grad_sum_chips_l{layer}_{n}", layer=layer, into=buf)
                for s, p, buf, (n, _, _, tr, tc) in zip(slots, parts, into, specs)]

    kept = {}

    def early_reduce(big_1):
        def during_lru(big_0):
            kept["big_0"] = dict(big_0)
            return _comm_both(_sibling_send(views(big_1, BIG), 1), _sibling_send(views(big_0, others), 0))

        def during_ssd(recv):
            kept["parts_1"] = partial_sums(big_1, recv[:len(BIG)], 1, BIG)
            kept["parts_0"] = partial_sums(kept["big_0"], recv[len(BIG):], 0, others)
            return _comm_both(_chip_exchange(kept["parts_1"], 1), _chip_exchange(kept["parts_0"], 0))

        return dict(lru=during_lru, ssd=during_ssd, in_dgrad=lambda big_0: _sibling_send(views(big_0, w_in_only), 0))

    loss_blk, grad_x, grads, big, (slots, recv_in) = _local_step(x, target, [w0, {}], conv, small, prefetch, early_reduce)
    loss = lax.psum(loss_blk[0, 0], ("x", "y", "c"))
    both = reduced(slots[:len(BIG)], kept["parts_1"], 1, [None] * len(BIG), BIG)
    both[1:] = reduced(slots[len(BIG):], kept["parts_0"], 0, both[1:], others)
    parts_in = partial_sums(big[0], recv_in, 0, w_in_only)
    names = SMALL + CONV
    srows = -(-sum(grads[n].size for n in names) // (8 * PACK_W)) * 8
    flat = lambda d, ns: jnp.concatenate([d[n].reshape(-1) for n in ns])
    padto = lambda v: jnp.pad(v, (0, srows * PACK_W - v.shape[0])).reshape(srows, PACK_W)
    g_own = padto(flat(grads, names))
    g_all, *slots_in = _comm_now(_comm_both(_allgather_devices(g_own), _chip_exchange(parts_in, 0)), "grad_chip_exchange")
    both[:1] = reduced(slots_in, parts_in, 0, both[:1], w_in_only)
    done = dict(zip([n for n, *_ in BIG], _sibling_share(both)))
    g_big = {n: done[n].reshape(a[n].shape) for n in ("w_branch", "w_out", "w_ffn_in", "w_ffn_out")}
    gt = done["w_in"].transpose(0, 2, 1)
    first = jnp.concatenate([gt[..., 512 * j + 256 * part:512 * j + 256 * (part + 1)] for part in range(2) for j in range(4)]
                            + [gt[..., 2 * D:]], axis=-1)
    tail = W_IN_SHARD - (IN_DIM - 7168)
    last = jnp.concatenate([gt[..., :tail], gt[..., W_IN_SHARD - N_HEADS:], gt[..., tail:W_IN_SHARD - N_HEADS]], axis=-1)
    g_big["w_in"] = jnp.where(chip == 0, first, jnp.where(chip == N_CHIPS - 1, last, gt))

    g_sum = _sum_slots(g_all, g_own[None], jnp.stack([me, zero + 1]), tr=srows, tc=PACK_W, name="small_grad_sum")
    off, g_small = 0, {}
    for n in names:
        g_small[n] = g_sum.reshape(-1)[off:off + grads[n].size].reshape(grads[n].shape)
        off += grads[n].size
    for n in CONV:
        width = a[n].shape[2]
        g_big[n] = lax.dynamic_slice(g_small.pop(n), (zero, zero, chip * width), a[n].shape)

    out_g, out_d, out_m, out_v = {}, {}, {}, {}
    for n in g_big:
        shp = a[n].shape
        two_d = (shp[0] * shp[1], shp[2])
        d_, m_, v_ = _adamw(a[n].reshape(two_d), g_big[n].reshape(two_d), a["m_" + n].reshape(two_d),
                            a["v_" + n].reshape(two_d), name="adamw_" + n)
        out_g[n], out_d[n], out_m[n], out_v[n] = g_big[n], d_.reshape(shp), m_.reshape(shp), v_.reshape(shp)
    d_, m_, v_ = _adamw(padto(flat(a, SMALL)), padto(flat(g_small, SMALL)), padto(flat({n: a["m_" + n] for n in SMALL}, SMALL)),
                        padto(flat({n: a["v_" + n] for n in SMALL}, SMALL)), name="adamw_small")
    off = 0
    for n in SMALL:
        cut = lambda v: v.reshape(-1)[off:off + a[n].size].reshape(a[n].shape)
        out_g[n], out_d[n], out_m[n], out_v[n] = g_small[n], cut(d_), cut(m_), cut(v_)
        off += a[n].size
    return loss, grad_x[None], out_g, out_d, out_m, out_v


WEIGHTS = ("norm1_g", "w_in", "b_gate", "lru_conv_w", "lru_conv_b", "lru_w_a", "lru_b_a", "lru_w_x", "lru_b_x", "lru_lambda",
           "ssd_conv_w", "ssd_conv_b", "ssd_dt_bias", "ssd_A_log", "ssd_D", "ssd_norm_g", "w_branch", "w_out", "norm2_g",
           "w_ffn_in", "w_ffn_out", "norm_f")
INPUTS = ("x",) + WEIGHTS + ("loss_target",) + tuple("m_" + n for n in WEIGHTS) + tuple("v_" + n for n in WEIGHTS)


def kernel(x, norm1_g, w_in, b_gate, lru_conv_w, lru_conv_b, lru_w_a, lru_b_a, lru_w_x, lru_b_x, lru_lambda, ssd_conv_w, ssd_conv_b, ssd_dt_bias, ssd_A_log, ssd_D, ssd_norm_g, w_branch, w_out, norm2_g, w_ffn_in, w_ffn_out, norm_f, loss_target, m_norm1_g, m_w_in, m_b_gate, m_lru_conv_w, m_lru_conv_b, m_lru_w_a, m_lru_b_a, m_lru_w_x, m_lru_b_x, m_lru_lambda, m_ssd_conv_w, m_ssd_conv_b, m_ssd_dt_bias, m_ssd_A_log, m_ssd_D, m_ssd_norm_g, m_w_branch, m_w_out, m_norm2_g, m_w_ffn_in, m_w_ffn_out, m_norm_f, v_norm1_g, v_w_in, v_b_gate, v_lru_conv_w, v_lru_conv_b, v_lru_w_a, v_lru_b_a, v_lru_w_x, v_lru_b_x, v_lru_lambda, v_ssd_conv_w, v_ssd_conv_b, v_ssd_dt_bias, v_ssd_A_log, v_ssd_D, v_ssd_norm_g, v_w_branch, v_w_out, v_norm2_g, v_w_ffn_in, v_w_ffn_out, v_norm_f):
    args = (x, norm1_g, w_in, b_gate, lru_conv_w, lru_conv_b, lru_w_a, lru_b_a, lru_w_x, lru_b_x, lru_lambda, ssd_conv_w, ssd_conv_b, ssd_dt_bias, ssd_A_log, ssd_D, ssd_norm_g, w_branch, w_out, norm2_g, w_ffn_in, w_ffn_out, norm_f, loss_target, m_norm1_g, m_w_in, m_b_gate, m_lru_conv_w, m_lru_conv_b, m_lru_w_a, m_lru_b_a, m_lru_w_x, m_lru_b_x, m_lru_lambda, m_ssd_conv_w, m_ssd_conv_b, m_ssd_dt_bias, m_ssd_A_log, m_ssd_D, m_ssd_norm_g, m_w_branch, m_w_out, m_norm2_g, m_w_ffn_in, m_w_ffn_out, m_norm_f, v_norm1_g, v_w_in, v_b_gate, v_lru_conv_w, v_lru_conv_b, v_lru_w_a, v_lru_b_a, v_lru_w_x, v_lru_b_x, v_lru_lambda, v_ssd_conv_w, v_ssd_conv_b, v_ssd_dt_bias, v_ssd_A_log, v_ssd_D, v_ssd_norm_g, v_w_branch, v_w_out, v_norm2_g, v_w_ffn_in, v_w_ffn_out, v_norm_f)
    assert len(args) == len(INPUTS)
    loss, grad_x, g, d, m, v = _sharded_step(dict(zip(INPUTS, args)))
    return (loss, grad_x, *[g[n] for n in WEIGHTS], *[d[n] for n in WEIGHTS], *[m[n] for n in WEIGHTS],
            *[v[n] for n in WEIGHTS])
```

```python
import functools
import math

import numpy as np
import jax
import jax.numpy as jnp
from jax import lax
from jax.experimental import pallas as pl
from jax.experimental.pallas import tpu as pltpu

F32 = jnp.float32
_MXU = jnp.bfloat16
_HI = lax.Precision.HIGHEST

D = 1024
EPS = 1e-6
N_LAYERS = 2
LRU_C = 8.0
N_HEADS = 32
HEAD_P = 64
N_GROUPS = 4
N_STATE = 128
SSD_INNER = 2048
XBC = 3072
D_FF = 2816
CHUNK = 64
NORM_ROWS = 32
IN_DIM = 9248

NP = 9216
ZX_W = 5120
G0 = 6144
LBLK = 512
DT_PAD = 128

VMEM_LIMIT_BYTES_V7X = 56 * 1024 * 1024

ADAM_LR, ADAM_B1, ADAM_B2, ADAM_EPS, ADAM_WD, ADAM_STEP = 0.001, 0.9, 0.999, 1e-08, 0.01, 10
MESH = pl.DeviceIdType.MESH


def _cp(sem):
    return pltpu.CompilerParams(dimension_semantics=sem, vmem_limit_bytes=VMEM_LIMIT_BYTES_V7X)


def _lblk_col(j):
    return 10 + j + 4 * (j // 2)


def _sigmoid(x):
    return 0.5 * jnp.tanh(0.5 * x) + 0.5


def _softplus(x):
    return jnp.maximum(x, 0.0) + jnp.log(1.0 + jnp.exp(-jnp.abs(x)))


def _silu(x):
    return x * _sigmoid(x)


_GELU_C0 = math.sqrt(2.0 / math.pi)
_GELU_C1 = 0.044715


def _gelu_and_grad(x):
    t = jnp.tanh(_GELU_C0 * (x + _GELU_C1 * x * x * x))
    g = 0.5 * x * (1.0 + t)
    dg = 0.5 * (1.0 + t) + 0.5 * x * (1.0 - t * t) * _GELU_C0 * (1.0 + 3.0 * _GELU_C1 * x * x)
    return g, dg


def _one_minus_exp(x):
    p = 1.0 + x * (1.0 / 7.0)
    p = 1.0 + x * (1.0 / 6.0) * p
    p = 1.0 + x * (1.0 / 5.0) * p
    p = 1.0 + x * (1.0 / 4.0) * p
    p = 1.0 + x * (1.0 / 3.0) * p
    p = 1.0 + x * (1.0 / 2.0) * p
    return jnp.where(x > -0.3, -x * p, 1.0 - jnp.exp(x))


def _dot(a, b):
    return jnp.dot(a.astype(_MXU), b.astype(_MXU), preferred_element_type=F32)


def _dot_nt(a, b):
    return lax.dot_general(a.astype(_MXU), b.astype(_MXU), (((1,), (1,)), ((), ())), preferred_element_type=F32)


def _dot_tn(a, b):
    return lax.dot_general(a.astype(_MXU), b.astype(_MXU), (((0,), (0,)), ((), ())), preferred_element_type=F32)


def _shift_down(x, prev8, k):
    xr = pltpu.roll(x, k, 0)
    pr = pltpu.roll(prev8, k, 0)
    row = lax.broadcasted_iota(jnp.int32, prev8.shape, 0)
    head = jnp.where(row < k, pr, xr[0:8])
    return jnp.concatenate([head, xr[8:]], axis=0)


def _shift_up(x, next8, k):
    r = x.shape[0]
    xr = pltpu.roll(x, r - k, 0)
    nr = pltpu.roll(next8, 8 - k, 0)
    row = lax.broadcasted_iota(jnp.int32, next8.shape, 0)
    tail = jnp.where(row >= 8 - k, nr, xr[r - 8:r])
    return jnp.concatenate([xr[:r - 8], tail], axis=0)


def _conv4(x, prev8, w_ref, b_ref, cols=slice(None)):
    acc = x * w_ref[3:4, cols] + b_ref[0:1, cols]
    for k in (1, 2, 3):
        acc = acc + _shift_down(x, prev8, k) * w_ref[3 - k:4 - k, cols]
    return acc


def _lin_scan(a, b, reverse):
    r = a.shape[0]
    row = lax.broadcasted_iota(jnp.int32, a.shape, 0)
    d = 1
    while d < r:
        sh = (r - d) if reverse else d
        a_s = pltpu.roll(a, sh, 0)
        b_s = pltpu.roll(b, sh, 0)
        m = (row < r - d) if reverse else (row >= d)
        b = jnp.where(m, a * b_s + b, b)
        a = jnp.where(m, a * a_s, a)
        d *= 2
    return a, b


def _rsum(x):
    return jnp.sum(x, axis=0, keepdims=True)


def _comm_specs(comm):
    if comm is None:
        return [], [], [], [], []
    n = len(comm["inputs"])
    return list(comm["inputs"]), [HBM_SPEC] * n, [HBM_SPEC] * len(comm["out_shapes"]), list(comm["out_shapes"]), comm["sems"]


def _comm_steps(comm, refs, n_in, n_out, first, mid, last):
    ni, no = len(comm["inputs"]), len(comm["out_shapes"])
    parts = (refs[n_in:n_in + ni], refs[n_out:n_out + no]) + tuple(refs[len(refs) - len(comm["sems"]):])
    for when, what in ((first, "start"), (mid, "mid"), (last, "end")):
        @pl.when(when)
        def _():
            comm[what](*parts)


def _norm_mm(h, gamma, w, *, tm, tn, name, out_dtype=F32, comm=None):
    m, k = h.shape
    if w.ndim == 3:
        assert w.shape[2] == tn
        n = w.shape[0] * tn
        w_spec = pl.BlockSpec((None, k, tn), lambda i, j: (j, 0, 0))
    else:
        n = w.shape[1]
        w_spec = pl.BlockSpec((k, tn), lambda i, j: (0, j))

    c_args, c_in_specs, c_out_specs, c_out_shapes, c_sems = _comm_specs(comm)
    ni, nj = m // tm, n // tn

    def body(*refs):
        h_ref, g_ref, w_ref = refs[:3]
        xn_ref, o_ref = refs[3 + len(c_args):5 + len(c_args)]
        i, j = pl.program_id(0), pl.program_id(1)
        if comm is not None:
            _comm_steps(comm, refs, 3, 5 + len(c_args), (i == 0) & (j == 0), (i == (3 * ni) // 4) & (j == 0),
                        (i == ni - 1) & (j == nj - 1))

        @pl.when(j == 0)
        def _():
            x = h_ref[...]
            r = lax.rsqrt(jnp.mean(x * x, axis=-1, keepdims=True) + EPS)
            xn_ref[...] = ((x * r) * g_ref[...]).astype(xn_ref.dtype)
        o_ref[...] = jnp.dot(xn_ref[...], w_ref[...], preferred_element_type=F32).astype(o_ref.dtype)

    return pl.pallas_call(
        body, name=name, grid=(ni, nj),
        in_specs=[pl.BlockSpec((tm, k), lambda i, j: (i, 0)), pl.BlockSpec((1, k), lambda i, j: (0, 0)), w_spec] + c_in_specs,
        out_specs=[pl.BlockSpec((tm, k), lambda i, j: (i, 0)), pl.BlockSpec((tm, tn), lambda i, j: (i, j))] + c_out_specs,
        out_shape=[jax.ShapeDtypeStruct((m, k), _MXU), jax.ShapeDtypeStruct((m, n), out_dtype)] + c_out_shapes,
        scratch_shapes=c_sems,
        compiler_params=_cp(("arbitrary", "arbitrary") if comm is not None else ("parallel", "arbitrary")),
    )(h, gamma, w, *c_args)


def _mm_nn(a, w, *, tm, tn, name, residual=None):
    m, k = a.shape
    n = w.shape[1]

    def body(*refs):
        if residual is None:
            a_ref, w_ref, o_ref = refs
            o_ref[...] = _dot(a_ref[...], w_ref[...])
        else:
            a_ref, w_ref, r_ref, o_ref = refs
            o_ref[...] = _dot(a_ref[...], w_ref[...]) + r_ref[...]

    in_specs = [pl.BlockSpec((tm, k), lambda i, j: (i, 0)), pl.BlockSpec((k, tn), lambda i, j: (0, j))]
    args = [a, w]
    if residual is not None:
        in_specs.append(pl.BlockSpec((tm, tn), lambda i, j: (i, j)))
        args.append(residual)
    return pl.pallas_call(
        body, name=name, grid=(m // tm, n // tn), in_specs=in_specs,
        out_specs=pl.BlockSpec((tm, tn), lambda i, j: (i, j)),
        out_shape=jax.ShapeDtypeStruct((m, n), F32),
        compiler_params=_cp(("parallel", "parallel")),
    )(*args)


def _wgrad(a, b, *, tt, ta, tn, name, out_shape, out_block, out_index, a_tab=None, o_tab=None, into=None):
    t = a.shape[0]
    a_tab = list(range(a.shape[1] // ta)) if a_tab is None else a_tab
    o_tab = a_tab if o_tab is None else o_tab
    nb = b.shape[1] // tn

    def body(at_ref, ot_ref, a_ref, b_ref, *rest):
        del at_ref, ot_ref
        o_ref = rest[-1]

        @pl.when(pl.program_id(2) == 0)
        def _():
            o_ref[...] = jnp.zeros_like(o_ref)
        o_ref[...] += _dot_tn(a_ref[...], b_ref[...])

    in_specs = [pl.BlockSpec((tt, ta), lambda r, j, i, at, ot: (i, at[r])),
                pl.BlockSpec((tt, tn), lambda r, j, i, at, ot: (i, j))]
    args = [jnp.asarray(a_tab, jnp.int32), jnp.asarray(o_tab, jnp.int32), a, b]
    aliases = {}
    if into is not None:
        in_specs.append(pl.BlockSpec(memory_space=pl.ANY))
        args.append(into)
        aliases = {4: 0}
    return pl.pallas_call(
        body, name=name,
        grid_spec=pltpu.PrefetchScalarGridSpec(
            num_scalar_prefetch=2, grid=(len(a_tab), nb, t // tt), in_specs=in_specs,
            out_specs=pl.BlockSpec(out_block, lambda r, j, i, at, ot: out_index(ot[r], j))),
        out_shape=jax.ShapeDtypeStruct(out_shape, F32), input_output_aliases=aliases,
        compiler_params=_cp(("parallel", "parallel", "arbitrary")),
    )(*args)


def _mm_nt(a, w, *, tm, name):
    m, kc = a.shape
    n = w.shape[0]

    def body(a_ref, w_ref, o_ref):
        o_ref[...] = _dot_nt(a_ref[...], w_ref[...])

    return pl.pallas_call(
        body, name=name, grid=(m // tm,),
        in_specs=[pl.BlockSpec((tm, kc), lambda i: (i, 0)), pl.BlockSpec((n, kc), lambda i: (0, 0))],
        out_specs=pl.BlockSpec((tm, n), lambda i: (i, 0)),
        out_shape=jax.ShapeDtypeStruct((m, n), F32),
        compiler_params=_cp(("parallel",)),
    )(a, w)


def _mm_nt_rmsbwd(dy, w, x, gamma, dres, *, tm, tk, name, extra=None, comm=None):
    m, kc = dy.shape
    nk = kc // tk
    ni = m // tm
    n_x = 5 if extra is None else 7
    c_args, c_in_specs, c_out_specs, c_out_shapes, c_sems = _comm_specs(comm)
    if w.ndim == 3:
        assert w.shape[0] == nk and w.shape[2] == tk
        d = w.shape[1]
        w_spec = pl.BlockSpec((None, d, tk), lambda i, k: (k, 0, 0))
    else:
        d = w.shape[0]
        w_spec = pl.BlockSpec((d, tk), lambda i, k: (0, k))

    def body(*refs):
        dy_ref, w_ref, x_ref, g_ref, r_ref = refs[:5]
        if extra is not None:
            dy2_ref, w2_ref = refs[5:7]
        n_out = n_x + len(c_args)
        dx_ref, dg_ref = refs[n_out:n_out + 2]
        acc_ref = refs[n_out + 2 + len(c_out_shapes)]
        i, kk = pl.program_id(0), pl.program_id(1)
        if comm is not None:
            _comm_steps(comm, refs, n_x, n_out + 2, (i == 0) & (kk == 0), (i == (3 * ni) // 4) & (kk == 0),
                        (i == ni - 1) & (kk == nk - 1))

        @pl.when(kk == 0)
        def _():
            acc_ref[...] = jnp.zeros_like(acc_ref)

        @pl.when((i == 0) & (kk == 0))
        def _():
            dg_ref[...] = jnp.zeros_like(dg_ref)

        acc_ref[...] += _dot_nt(dy_ref[...], w_ref[...])

        @pl.when(kk == nk - 1)
        def _():
            dxn = acc_ref[...]
            if extra is not None:
                dxn = dxn + _dot_nt(dy2_ref[...], w2_ref[...])
            xv = x_ref[...]
            r = lax.rsqrt(jnp.mean(xv * xv, axis=-1, keepdims=True) + EPS)
            xh = xv * r
            dg_ref[0:1, :] += _rsum(dxn * xh)
            dxh = dxn * g_ref[...]
            dx_ref[...] = r_ref[...] + r * (dxh - xh * jnp.mean(dxh * xh, axis=-1, keepdims=True))

    in_specs = [pl.BlockSpec((tm, tk), lambda i, k: (i, k)), w_spec,
                pl.BlockSpec((tm, d), lambda i, k: (i, 0)), pl.BlockSpec((1, d), lambda i, k: (0, 0)),
                pl.BlockSpec((tm, d), lambda i, k: (i, 0))]
    args = [dy, w, x, gamma, dres]
    if extra is not None:
        k2 = extra[0].shape[1]
        in_specs += [pl.BlockSpec((tm, k2), lambda i, k: (i, 0)), pl.BlockSpec((d, k2), lambda i, k: (0, 0))]
        args += list(extra)
    return pl.pallas_call(
        body, name=name, grid=(ni, nk), in_specs=in_specs + c_in_specs,
        out_specs=[pl.BlockSpec((tm, d), lambda i, k: (i, 0)), pl.BlockSpec((8, d), lambda i, k: (0, 0))] + c_out_specs,
        out_shape=[jax.ShapeDtypeStruct((m, d), F32), jax.ShapeDtypeStruct((8, d), F32)] + c_out_shapes,
        scratch_shapes=[pltpu.VMEM((tm, d), F32)] + c_sems,
        compiler_params=_cp(("arbitrary", "arbitrary")),
    )(*args, *c_args)


def _rsum8(x):
    acc = x[0:8]
    for g in range(1, x.shape[0] // 8):
        acc = acc + x[8 * g:8 * (g + 1)]
    return acc


def _lru_gates(x, prev8, cw_ref, cb_ref, wa_ref, wx_ref, ba_ref, bx_ref, lam_ref):
    u = _conv4(x, prev8, cw_ref, cb_ref)
    ra =_sigmoid(_dot(u, wa_ref[0]) + ba_ref[...])
    ia = _sigmoid(_dot(u, wx_ref[0]) + bx_ref[...])
    sp = _softplus(-lam_ref[...])
    log_a = -LRU_C * ra * sp
    a = jnp.exp(log_a)
    m2 = _one_minus_exp(2.0 * log_a)
    mult = jnp.sqrt(m2)
    return u, ra, ia, sp, a, m2, mult


def _lru_fwd(proj, lw, *, r, name, comm=None):
    t = proj.shape[0]
    nt = t // r
    c_args, c_in_specs, c_out_specs, c_out_shapes, c_sems = _comm_specs(comm)

    def body(*refs):
        xg_ref, xp_ref, cw_ref, cb_ref, wa_ref, wx_ref, ba_ref, bx_ref, lam_ref = refs[:9]
        hl_ref, ya_ref, sv_ref = refs[9 + len(c_args):12 + len(c_args)]
        carry_ref = refs[12 + len(c_args) + len(c_out_shapes)]
        i = pl.program_id(1)
        if comm is not None:
            j = pl.program_id(0)
            _comm_steps(comm, refs, 9, 12 + len(c_args), (j == 0) & (i == 0), (j == 3) & (i == 0), (j == 3) & (i == nt - 1))

        @pl.when(i == 0)
        def _():
            carry_ref[...] = jnp.zeros_like(carry_ref)

        x = xg_ref[:, 0:256]
        lg = xg_ref[:, 256:512]
        prev8 = jnp.where(i == 0, 0.0, xp_ref[:, 0:256])
        u, ra, ia, sp, a, m2, mult = _lru_gates(x, prev8, cw_ref, cb_ref, wa_ref, wx_ref, ba_ref, bx_ref, lam_ref)
        for k, v in enumerate((u, ra, ia, a, mult)):
            sv_ref[k] = v
        ac, hc = _lin_scan(a, mult * ia * u, False)
        h = hc + ac * carry_ref[0:1, :]
        hl_ref[...] = h
        carry_ref[0:1, :] = hl_ref[r - 1:r, :]
        g, _ = _gelu_and_grad(lg)
        ya_ref[...] = (g * h).astype(ya_ref.dtype)

    small = lambda rows: pl.BlockSpec((rows, 256), lambda j, i: (0, j))
    return pl.pallas_call(
        body, name=name, grid=(4, nt),
        in_specs=[pl.BlockSpec((r, LBLK), lambda j, i: (i, _lblk_col(j))),
                  pl.BlockSpec((8, LBLK), lambda j, i: (jnp.maximum(i * (r // 8) - 1, 0), _lblk_col(j))),
                  small(4), small(1),
                  pl.BlockSpec((1, 256, 256), lambda j, i: (j, 0, 0)), pl.BlockSpec((1, 256, 256), lambda j, i: (j, 0, 0)),
                  small(1), small(1), small(1)] + c_in_specs,
        out_specs=[pl.BlockSpec((r, 256), lambda j, i: (i, j)), pl.BlockSpec((r, 256), lambda j, i: (i, j)),
                   pl.BlockSpec((5, r, 256), lambda j, i: (0, i, j))] + c_out_specs,
        out_shape=[jax.ShapeDtypeStruct((t, D), F32), jax.ShapeDtypeStruct((t, D), _MXU),
                   jax.ShapeDtypeStruct((5, t, D), F32)] + c_out_shapes,
        scratch_shapes=[pltpu.VMEM((8, 256), F32)] + c_sems,
        compiler_params=_cp(("arbitrary", "arbitrary") if comm is not None else ("parallel", "arbitrary")),
    )(proj, proj, lw["cw"], lw["cb"], lw["wa"], lw["wx"], lw["ba"], lw["bx"], lw["lam"], *c_args)


def _lru_bwd(proj, hl, gates, dya, dproj, lw, *, r, name, comm=None):
    t = proj.shape[0]
    nt = t // r
    c_args, c_in_specs, c_out_specs, c_out_shapes, c_sems = _comm_specs(comm)
    n_in = 10

    def body(*refs):
        xg_ref, hl_ref, hp_ref, sv_ref, dya_ref, cw_ref, wa_ref, wx_ref, lam_ref = refs[:9]
        n_out = n_in + len(c_args)
        dproj_ref, sm_ref, dwa_ref, dwx_ref = refs[n_out:n_out + 4]
        n_scr = n_out + 4 + len(c_out_shapes)
        carry_ref, du8_ref, row_scr = refs[n_scr:n_scr + 3]
        i = pl.program_id(1)
        if comm is not None:
            j = pl.program_id(0)
            _comm_steps(comm, refs, n_in, n_out + 4, (j == 0) & (i == 0), (j == 3) & (i == 0), (j == 3) & (i == nt - 1))

        @pl.when(i == 0)
        def _():
            carry_ref[...] = jnp.zeros_like(carry_ref)
            du8_ref[...] = jnp.zeros_like(du8_ref)
            sm_ref[...] = jnp.zeros_like(sm_ref)
            dwa_ref[...] = jnp.zeros_like(dwa_ref)
            dwx_ref[...] = jnp.zeros_like(dwx_ref)

        tile0 = i == nt - 1
        xp = xg_ref[:, 0:256]
        lg = xg_ref[:, 256:512]
        u, ra, ia, a, mult = (sv_ref[k] for k in range(5))
        sp = _softplus(-lam_ref[...])
        h = hl_ref[...]
        hprev = _shift_down(h, jnp.where(tile0, 0.0, hp_ref[...]), 1)
        dya_v = dya_ref[...]
        g, dg = _gelu_and_grad(lg)
        ac, lc = _lin_scan(_shift_up(a, carry_ref[...], 1), dya_v * g, True)
        lam_v = lc + ac * carry_ref[1:2, :]
        row_scr[0:8, :] = lam_v[0:8]
        row_scr[8:16, :] = a[0:8]
        carry_ref[1:2, :] = row_scr[0:1, :]
        carry_ref[0:1, :] = row_scr[8:9, :]
        da = lam_v * hprev
        dmult = lam_v * ia * u
        dia = lam_v * mult * u
        dlog = da * a - dmult * (a * a) / mult
        dra = -LRU_C * sp * dlog
        dpa = dra * ra * (1.0 - ra)
        dpx = dia * ia * (1.0 - ia)
        du = lam_v * mult * ia + _dot_nt(dpa, wa_ref[0]) + _dot_nt(dpx, wx_ref[0])
        dwa_ref[0] += _dot_tn(u, dpa)
        dwx_ref[0] += _dot_tn(u, dpx)
        dlx = du * cw_ref[3:4, :]
        sm_ref[24:32, :] += _rsum8(du * xp)
        for k in (1, 2, 3):
            du_k = _shift_up(du, du8_ref[...], k)
            dlx = dlx + du_k * cw_ref[3 - k:4 - k, :]
            sm_ref[8 * (3 - k):8 * (4 - k), :] += _rsum8(du_k * xp)
        du8_ref[...] = du[0:8]
        dproj_ref[:, 0:256] = dlx.astype(dproj_ref.dtype)
        dproj_ref[:, 256:512] = (dya_v * h * dg).astype(dproj_ref.dtype)
        sm_ref[32:40, :] += _rsum8(du)
        sm_ref[40:48, :] += _rsum8(dpa)
        sm_ref[48:56, :] += _rsum8(dpx)
        sm_ref[56:64, :] += _rsum8(-LRU_C * ra * dlog) * (-_sigmoid(-lam_ref[...]))

    rev = lambda i: nt - 1 - i
    small = lambda rows: pl.BlockSpec((rows, 256), lambda j, i: (0, j))
    wblk = pl.BlockSpec((1, 256, 256), lambda j, i: (j, 0, 0))
    return pl.pallas_call(
        body, name=name, grid=(4, nt),
        in_specs=[pl.BlockSpec((r, LBLK), lambda j, i: (rev(i), _lblk_col(j))),
                  pl.BlockSpec((r, 256), lambda j, i: (rev(i), j)),
                  pl.BlockSpec((8, 256), lambda j, i: (jnp.maximum(rev(i) * (r // 8) - 1, 0), j)),
                  pl.BlockSpec((5, r, 256), lambda j, i: (0, rev(i), j)),
                  pl.BlockSpec((r, 256), lambda j, i: (rev(i), j)),
                  small(4), wblk, wblk, small(1),
                  pl.BlockSpec(memory_space=pl.ANY)] + c_in_specs,
        out_specs=[pl.BlockSpec((r, LBLK), lambda j, i: (rev(i), _lblk_col(j))),
                   pl.BlockSpec((64, 256), lambda j, i: (0, j)), wblk, wblk] + c_out_specs,
        out_shape=[jax.ShapeDtypeStruct(dproj.shape, dproj.dtype), jax.ShapeDtypeStruct((64, D), F32),
                   jax.ShapeDtypeStruct((4, 256, 256), F32), jax.ShapeDtypeStruct((4, 256, 256), F32)] + c_out_shapes,
        scratch_shapes=[pltpu.VMEM((8, 256), F32), pltpu.VMEM((8, 256), F32), pltpu.VMEM((16, 256), F32)] + c_sems,
        input_output_aliases={n_in - 1: 0},
        compiler_params=_cp(("arbitrary", "arbitrary") if comm is not None else ("parallel", "arbitrary")),
    )(proj, hl, hl, gates, dya, lw["cw"], lw["wa"], lw["wx"], lw["lam"], dproj, *c_args)


def _head_cols(x):
    lane = lax.broadcasted_iota(jnp.int32, x.shape, 1)
    return [jnp.sum(jnp.where(lane == h, x, 0.0), axis=1, keepdims=True) for h in range(N_HEADS)]


def _compact_heads(blocks):
    lane = lax.broadcasted_iota(jnp.int32, blocks[0].shape, 1)
    lo = lane < HEAD_P
    out = jnp.zeros_like(blocks[0])
    for j, blk in enumerate(blocks):
        s_lo = jnp.sum(jnp.where(lo, blk, 0.0), axis=1, keepdims=True)
        s_hi = jnp.sum(jnp.where(lo, 0.0, blk), axis=1, keepdims=True)
        out = jnp.where(lane == 2 * j, s_lo, out)
        out = jnp.where(lane == 2 * j + 1, s_hi, out)
    return out


def _ssd_prelude(dtraw_ref, dtb_ref, alog_ref, dt_scr, a_scr):
    lane = lax.broadcasted_iota(jnp.int32, dt_scr.shape, 1)
    dt = jnp.where(lane < N_HEADS, _softplus(dtraw_ref[...] + dtb_ref[0:1, :]), 0.0)
    dt_scr[...] = dt
    a_scr[...] = dt * (-jnp.exp(alog_ref[0:1, :]))


def _ssd_chunk_scalars(dt_scr, a_scr, r_scr, r0):
    a_c = a_scr[pl.ds(r0, CHUNK), :]
    dt_c = dt_scr[pl.ds(r0, CHUNK), :]
    i0 = lax.broadcasted_iota(jnp.int32, (CHUNK, CHUNK), 0)
    i1 = lax.broadcasted_iota(jnp.int32, (CHUNK, CHUNK), 1)
    tri = jnp.where(i0 >= i1, 1.0, 0.0).astype(F32)
    cs = jnp.dot(tri, a_c, precision=_HI, preferred_element_type=F32)
    lane = lax.broadcasted_iota(jnp.int32, (CHUNK, 128), 1)
    srow = lax.broadcasted_iota(jnp.int32, (CHUNK, 128), 0)
    t_lo = jnp.where((lane < HEAD_P) & (srow <= lane), 1.0, 0.0).astype(F32)
    t_hi = jnp.where((lane >= HEAD_P) & (srow <= lane - HEAD_P), 1.0, 0.0).astype(F32)
    even = (lane % 2) == 0
    tn = (((0,), (0,)), ((), ()))
    r_scr[...] = (lax.dot_general(jnp.where(even, a_c, 0.0), t_lo, tn, precision=_HI, preferred_element_type=F32)
                  + lax.dot_general(jnp.where(even, 0.0, a_c), t_hi, tn, precision=_HI, preferred_element_type=F32))
    return cs, dt_c, _head_cols(cs), _head_cols(dt_c)


def _block_diag2(v):
    lo = lax.broadcasted_iota(jnp.int32, v.shape, 1) < HEAD_P
    return jnp.concatenate([jnp.where(lo, v, 0.0), jnp.where(lo, 0.0, v)], axis=0).astype(_MXU)


def _ssd_pair(xc_scr, r_scr, cs_cols, dt_cols, s2, r0, j, s2t=None):
    lane = lax.broadcasted_iota(jnp.int32, (CHUNK, 128), 1)
    srow = lax.broadcasted_iota(jnp.int32, (CHUNK, 128), 0)
    lo = lane < HEAD_P
    csc = jnp.where(lo, cs_cols[2 * j], cs_cols[2 * j + 1])
    dtc = jnp.where(lo, dt_cols[2 * j], dt_cols[2 * j + 1])
    csr = r_scr[2 * j:2 * j + 1, :] + r_scr[2 * j + 1:2 * j + 2, :]
    dm = jnp.where((lane & (HEAD_P - 1)) <= srow, jnp.exp(jnp.minimum(csc - csr, 0.0)), 0.0)
    xs = xc_scr[pl.ds(r0, CHUNK), j * 128:(j + 1) * 128]
    xd = xs * dtc
    csl = jnp.sum(jnp.where(srow == CHUNK - 1, csc, 0.0), axis=0, keepdims=True)
    out = dict(csc=csc, dtc=dtc, dm=dm, m2=s2 * dm, xs=xs, xd=xd, rhs=_block_diag2(xd), e=jnp.exp(csc),
               w=jnp.exp(csl - csc), dec=jnp.exp(csl))
    if s2t is not None:
        out["mt2"] = s2t * jnp.where((lane & (HEAD_P - 1)) >= srow, jnp.exp(jnp.minimum(csr - csc, 0.0)), 0.0)
    return out


def _cat(parts):
    return jnp.concatenate(parts, axis=1)


def _ssd_fwd(proj, dtraw, sw, *, rb, name, comm=None):
    t = proj.shape[0]
    ns, cb = t // rb, rb // CHUNK
    c_args, c_in_specs, c_out_specs, c_out_shapes, c_sems = _comm_specs(comm)

    def body(*refs):
        zx_ref, zp_ref, dtraw_ref, cw_ref, cbias_ref, dtb_ref, alog_ref, dsk_ref, ng_ref = refs[:9]
        yssd_ref, yb_ref, st_ref, xc_scr, dsl_ref = refs[9 + len(c_args):14 + len(c_args)]
        n_scr = 14 + len(c_args) + len(c_out_shapes)
        h_scr, dt_scr, a_scr, r_scr = refs[n_scr:n_scr + 4]
        i = pl.program_id(0)
        if comm is not None:
            _comm_steps(comm, refs, 9, 14 + len(c_args), i == 0, i == (3 * ns) // 4, i == ns - 1)

        @pl.when(i == 0)
        def _():
            h_scr[...] = jnp.zeros_like(h_scr)

        for j in range(XBC // 128):
            cs_, zc = slice(128 * j, 128 * (j + 1)), slice(2048 + 128 * j, 2048 + 128 * (j + 1))
            pre = _conv4(zx_ref[:, zc], jnp.where(i == 0, 0.0, zp_ref[:, zc]), cw_ref, cbias_ref, cs_)
            sg = _sigmoid(pre)
            xc_scr[:, cs_] = pre * sg
            dsl_ref[:, cs_] = sg * (1.0 + pre * (1.0 - sg))
        _ssd_prelude(dtraw_ref, dtb_ref, alog_ref, dt_scr, a_scr)

        def chunk(c, carry):
            r0 = pl.multiple_of(c * CHUNK, CHUNK)
            _, _, cs_cols, dt_cols = _ssd_chunk_scalars(dt_scr, a_scr, r_scr, r0)
            st_ref[c] = h_scr[...]
            for g in range(N_GROUPS):
                bg = xc_scr[pl.ds(r0, CHUNK), 2048 + 128 * g:2048 + 128 * (g + 1)]
                cg = xc_scr[pl.ds(r0, CHUNK), 2560 + 128 * g:2560 + 128 * (g + 1)]
                s2 = _dot_nt(cg, jnp.concatenate([bg, bg], axis=0))
                hp = h_scr[:, 512 * g:512 * (g + 1)]
                yoff = _dot(cg, hp)
                xdw, dec = [], []
                for jj in range(4):
                    j = 4 * g + jj
                    p = _ssd_pair(xc_scr, r_scr, cs_cols, dt_cols, s2, r0, j)
                    y = _dot(p["m2"], p["rhs"]) + yoff[:, 128 * jj:128 * (jj + 1)] * p["e"]
                    yssd_ref[pl.ds(r0, CHUNK), 128 * j:128 * (j + 1)] = y + dsk_ref[0:1, 128 * j:128 * (j + 1)] * p["xs"]
                    xdw.append(p["xd"] * p["w"])
                    dec.append(p["dec"])
                h_scr[:, 512 * g:512 * (g + 1)] = hp * _cat(dec) + _dot_tn(bg, _cat(xdw))
            return carry

        lax.fori_loop(0, cb, chunk, 0)
        for g in range(N_GROUPS):
            sl = slice(512 * g, 512 * (g + 1))
            for q in range(rb // NORM_ROWS):
                rw = slice(NORM_ROWS * q, NORM_ROWS * (q + 1))
                yz = yssd_ref[rw, sl] * _silu(zx_ref[rw, sl])
                rg = lax.rsqrt(jnp.mean(yz * yz, axis=-1, keepdims=True) + EPS)
                yb_ref[rw, sl] = (yz * rg * ng_ref[0:1, sl]).astype(yb_ref.dtype)

    full = lambda rows, cols: pl.BlockSpec((rows, cols), lambda i: (0, 0))
    return pl.pallas_call(
        body, name=name, grid=(ns,),
        in_specs=[pl.BlockSpec((rb, ZX_W), lambda i: (i, 0)),
                  pl.BlockSpec((8, ZX_W), lambda i: (jnp.maximum(i * (rb // 8) - 1, 0), 0)),
                  pl.BlockSpec((rb, DT_PAD), lambda i: (i, 0)),
                  full(4, XBC), full(1, XBC), full(1, DT_PAD), full(1, DT_PAD), full(1, SSD_INNER), full(1, SSD_INNER)]
        + c_in_specs,
        out_specs=[pl.BlockSpec((rb, SSD_INNER), lambda i: (i, 0)), pl.BlockSpec((rb, SSD_INNER), lambda i: (i, 0)),
                   pl.BlockSpec((cb, N_STATE, SSD_INNER), lambda i: (i, 0, 0)),
                   pl.BlockSpec((rb, XBC), lambda i: (i, 0)), pl.BlockSpec((rb, XBC), lambda i: (i, 0))] + c_out_specs,
        out_shape=[jax.ShapeDtypeStruct((t, SSD_INNER), F32), jax.ShapeDtypeStruct((t, SSD_INNER), _MXU),
                   jax.ShapeDtypeStruct((t // CHUNK, N_STATE, SSD_INNER), F32),
                   jax.ShapeDtypeStruct((t, XBC), F32), jax.ShapeDtypeStruct((t, XBC), F32)] + c_out_shapes,
        scratch_shapes=[pltpu.VMEM((N_STATE, SSD_INNER), F32), pltpu.VMEM((rb, DT_PAD), F32),
                        pltpu.VMEM((rb, DT_PAD), F32), pltpu.VMEM((128, 128), F32)] + c_sems,
        compiler_params=_cp(("arbitrary",)),
    )(proj, proj, dtraw, sw["cw"], sw["cb"], sw["dtb"], sw["alog"], sw["dsk"], sw["ng"], *c_args)


def _ssd_bwd(proj, dtraw, yssd, states, xc, dsl, dyb, dproj, sw, *, rb, name, comm=None):
    t = proj.shape[0]
    ns, cb = t // rb, rb // CHUNK
    c_args, c_in_specs, c_out_specs, c_out_shapes, c_sems = _comm_specs(comm)
    n_in = 13

    def body(*refs):
        zx_ref, dtraw_ref, yssd_ref, st_ref, xc_scr, dsl_scr, dyb_ref, cw_ref, dtb_ref, alog_ref, dsk_ref, ng_ref = refs[:12]
        n_out = n_in + len(c_args)
        dzx_ref, ddt_ref, gconv_ref, gch_ref, ghd_ref = refs[n_out:n_out + 5]
        n_scr = n_out + 5 + len(c_out_shapes)
        dht_scr, dy_scr, dxc_scr, dt_scr, a_scr, r_scr, dp8_scr = refs[n_scr:n_scr + 7]
        i = pl.program_id(0)
        if comm is not None:
            _comm_steps(comm, refs, n_in, n_out + 5, i == 0, i == (3 * ns) // 4, i == ns - 1)

        @pl.when(i == 0)
        def _():
            dht_scr[...] = jnp.zeros_like(dht_scr)
            dp8_scr[...] = jnp.zeros_like(dp8_scr)
            gconv_ref[...] = jnp.zeros_like(gconv_ref)
            gch_ref[...] = jnp.zeros_like(gch_ref)
            ghd_ref[...] = jnp.zeros_like(ghd_ref)

        _ssd_prelude(dtraw_ref, dtb_ref, alog_ref, dt_scr, a_scr)

        for g in range(N_GROUPS):
            sl = slice(512 * g, 512 * (g + 1))
            for q in range(rb // NORM_ROWS):
                rw = slice(NORM_ROWS * q, NORM_ROWS * (q + 1))
                zv = zx_ref[rw, sl]
                ys = yssd_ref[rw, sl]
                sg = _sigmoid(zv)
                sz = zv * sg
                yz = ys * sz
                rg = lax.rsqrt(jnp.mean(yz * yz, axis=-1, keepdims=True) + EPS)
                yn = yz * rg
                dyb_v = dyb_ref[rw, sl]
                gch_ref[0:8, sl] += _rsum8(dyb_v * yn)
                dyn = dyb_v * ng_ref[0:1, sl]
                dyz = rg * (dyn - yn * jnp.mean(dyn * yn, axis=-1, keepdims=True))
                dy_scr[rw, sl] = dyz * sz
                dzx_ref[rw, sl] = (dyz * ys * (sg * (1.0 + zv * (1.0 - sg)))).astype(dzx_ref.dtype)

        a_row = -jnp.exp(alog_ref[0:1, :])

        def chunk(cc, carry):
            c = cb - 1 - cc
            r0 = pl.multiple_of(c * CHUNK, CHUNK)
            rows = pl.ds(r0, CHUNK)
            _, dt_c, cs_cols, dt_cols = _ssd_chunk_scalars(dt_scr, a_scr, r_scr, r0)
            lane = lax.broadcasted_iota(jnp.int32, (CHUNK, 128), 1)
            srow = lax.broadcasted_iota(jnp.int32, (CHUNK, 128), 0)
            lo = lane < HEAD_P
            last = srow == CHUNK - 1
            p1_blocks, p3_blocks = [], []
            for g in range(N_GROUPS):
                gs = slice(512 * g, 512 * (g + 1))
                bg = xc_scr[rows, 2048 + 128 * g:2048 + 128 * (g + 1)]
                cg = xc_scr[rows, 2560 + 128 * g:2560 + 128 * (g + 1)]
                b2 = jnp.concatenate([bg, bg], axis=0)
                s2 = _dot_nt(cg, b2)
                s2t = _dot_nt(bg, jnp.concatenate([cg, cg], axis=0))
                hp = st_ref[c, :, gs]
                dht = dht_scr[:, gs]
                yoff = _dot(cg, hp)
                ps = [_ssd_pair(xc_scr, r_scr, cs_cols, dt_cols, s2, r0, 4 * g + jj, s2t) for jj in range(4)]
                dys = [dy_scr[rows, 128 * (4 * g + jj):128 * (4 * g + jj + 1)] for jj in range(4)]
                dye = _cat([dys[jj] * ps[jj]["e"] for jj in range(4)])
                w_g = _cat([p["w"] for p in ps])
                dcg = _dot_nt(dye, hp)
                dht_scr[:, gs] = _dot_tn(cg, dye) + _cat([p["dec"] for p in ps]) * dht
                dxd_state = w_g * _dot(bg, dht)
                dbg = _dot_nt(_cat([p["xd"] for p in ps]) * w_g, dht)
                tsum = _rsum(dht * hp)
                ds2 = jnp.zeros((CHUNK, 128), F32)
                for jj in range(4):
                    j = 4 * g + jj
                    ls = slice(128 * j, 128 * (j + 1))
                    p, dy2 = ps[jj], dys[jj]
                    dy_bd = _block_diag2(dy2)
                    dm2 = _dot_nt(dy2, p["rhs"])
                    ds2 = ds2 + dm2 * p["dm"]
                    gdiff = dm2 * p["m2"] - _dot_nt(p["xd"], dy_bd) * p["mt2"]
                    dxs = dxd_state[:, 128 * jj:128 * (jj + 1)]
                    dxd = _dot(p["mt2"], dy_bd) + dxs
                    end_row = _rsum(p["xd"] * dxs) + p["dec"] * tsum[:, 128 * jj:128 * (jj + 1)]
                    p1_blocks.append(gdiff + dy2 * yoff[:, 128 * jj:128 * (jj + 1)] * p["e"] - p["xd"] * dxs
                                     + jnp.where(last, end_row, 0.0))
                    p3_blocks.append(dxd * p["xs"])
                    dxc_scr[rows, ls] = dxd * p["dtc"] + dy2 * dsk_ref[0:1, ls]
                    gch_ref[8:16, ls] += _rsum8(dy2 * p["xs"])
                dcg = dcg + _dot(ds2, b2)
                rb2 = _dot_tn(ds2, cg)
                dxc_scr[rows, 2048 + 128 * g:2048 + 128 * (g + 1)] = dbg + rb2[0:CHUNK] + rb2[CHUNK:2 * CHUNK]
                dxc_scr[rows, 2560 + 128 * g:2560 + 128 * (g + 1)] = dcg
            dcs = _compact_heads(p1_blocks)
            i0 = lax.broadcasted_iota(jnp.int32, (CHUNK, CHUNK), 0)
            i1 = lax.broadcasted_iota(jnp.int32, (CHUNK, CHUNK), 1)
            triu = jnp.where(i1 >= i0, 1.0, 0.0).astype(F32)
            da = jnp.dot(triu, dcs, precision=_HI, preferred_element_type=F32)
            ddt = _compact_heads(p3_blocks) + da * a_row
            ddtraw = jnp.where(lane < N_HEADS, ddt * _sigmoid(dtraw_ref[rows, :] + dtb_ref[0:1, :]), 0.0)
            ddt_ref[rows, :] = ddtraw.astype(ddt_ref.dtype)
            ghd_ref[0:1, :] += _rsum(ddtraw)
            ghd_ref[1:2, :] += _rsum(da * dt_c) * a_row
            return carry

        lax.fori_loop(0, cb, chunk, 0)
        for j in range(XBC // 128):
            cs_, zc = slice(128 * j, 128 * (j + 1)), slice(2048 + 128 * j, 2048 + 128 * (j + 1))
            dpre = dxc_scr[:, cs_] * dsl_scr[:, cs_]
            xraw = zx_ref[:, zc]
            dx = dpre * cw_ref[3:4, cs_]
            gconv_ref[24:32, cs_] += _rsum8(dpre * xraw)
            for k in (1, 2, 3):
                dpre_k = _shift_up(dpre, dp8_scr[:, cs_], k)
                dx = dx + dpre_k * cw_ref[3 - k:4 - k, cs_]
                gconv_ref[8 * (3 - k):8 * (4 - k), cs_] += _rsum8(dpre_k * xraw)
            dzx_ref[:, zc] = dx.astype(dzx_ref.dtype)
            dp8_scr[:, cs_] = dpre[0:8]
            gconv_ref[32:40, cs_] += _rsum8(dpre)

    rev = lambda i: ns - 1 - i
    full = lambda rows, cols: pl.BlockSpec((rows, cols), lambda i: (0, 0))
    return pl.pallas_call(
        body, name=name, grid=(ns,),
        in_specs=[pl.BlockSpec((rb, ZX_W), lambda i: (rev(i), 0)),
                  pl.BlockSpec((rb, DT_PAD), lambda i: (rev(i), 0)),
                  pl.BlockSpec((rb, SSD_INNER), lambda i: (rev(i), 0)),
                  pl.BlockSpec((cb, N_STATE, SSD_INNER), lambda i: (rev(i), 0, 0)),
                  pl.BlockSpec((rb, XBC), lambda i: (rev(i), 0)), pl.BlockSpec((rb, XBC), lambda i: (rev(i), 0)),
                  pl.BlockSpec((rb, SSD_INNER), lambda i: (rev(i), 0)),
                  full(4, XBC), full(1, DT_PAD), full(1, DT_PAD), full(1, SSD_INNER), full(1, SSD_INNER),
                  pl.BlockSpec(memory_space=pl.ANY)] + c_in_specs,
        out_specs=[pl.BlockSpec((rb, ZX_W), lambda i: (rev(i), 0)), pl.BlockSpec((rb, DT_PAD), lambda i: (rev(i), 0)),
                   full(40, XBC), full(16, SSD_INNER), full(8, DT_PAD)] + c_out_specs,
        out_shape=[jax.ShapeDtypeStruct(dproj.shape, dproj.dtype), jax.ShapeDtypeStruct((t, DT_PAD), _MXU),
                   jax.ShapeDtypeStruct((40, XBC), F32), jax.ShapeDtypeStruct((16, SSD_INNER), F32),
                   jax.ShapeDtypeStruct((8, DT_PAD), F32)] + c_out_shapes,
        scratch_shapes=[pltpu.VMEM((N_STATE, SSD_INNER), F32),
                        pltpu.VMEM((rb, SSD_INNER), F32), pltpu.VMEM((rb, XBC), F32), pltpu.VMEM((rb, DT_PAD), F32),
                        pltpu.VMEM((rb, DT_PAD), F32), pltpu.VMEM((128, 128), F32), pltpu.VMEM((8, XBC), F32)] + c_sems,
        input_output_aliases={n_in - 1: 0},
        compiler_params=_cp(("arbitrary",)),
    )(proj, dtraw, yssd, states, xc, dsl, dyb, sw["cw"], sw["dtb"], sw["alog"], sw["dsk"], sw["ng"], dproj, *c_args)


def _branch_merge(ya, yb, proj, wba, wbb, bgate, *, tm, tn, name):
    t = ya.shape[0]
    nj = D // tn

    def body(ya_ref, yb_ref, ga_ref, gb_ref, wba_ref, wbb_ref, ba_ref, bb_ref, ta_ref, tb_ref, mg_ref):
        ta = _dot(ya_ref[...], wba_ref[...])
        tb = _dot(yb_ref[...], wbb_ref[...])
        ta_ref[...] = ta.astype(ta_ref.dtype)
        tb_ref[...] = tb.astype(tb_ref.dtype)
        ga = _sigmoid(ga_ref[...] + ba_ref[...])
        gb = _sigmoid(gb_ref[...] + bb_ref[...])
        mg_ref[...] = (ga * ta + gb * tb).astype(mg_ref.dtype)

    tile = pl.BlockSpec((tm, tn), lambda i, j: (i, j))
    return pl.pallas_call(
        body, name=name, grid=(t // tm, nj),
        in_specs=[pl.BlockSpec((tm, D), lambda i, j: (i, 0)), pl.BlockSpec((tm, SSD_INNER), lambda i, j: (i, 0)),
                  pl.BlockSpec((tm, tn), lambda i, j: (i, G0 // tn + j)),
                  pl.BlockSpec((tm, tn), lambda i, j: (i, (G0 + D) // tn + j)),
                  pl.BlockSpec((D, tn), lambda i, j: (0, j)), pl.BlockSpec((SSD_INNER, tn), lambda i, j: (0, j)),
                  pl.BlockSpec((1, tn), lambda i, j: (0, j)), pl.BlockSpec((1, tn), lambda i, j: (0, nj + j))],
        out_specs=[tile, tile, tile],
        out_shape=[jax.ShapeDtypeStruct((t, D), _MXU)] * 3,
        compiler_params=_cp(("parallel", "parallel")),
    )(ya, yb, proj, proj, wba, wbb, bgate, bgate)


def _swiglu_mm(gu, wfo, residual, *, tm, tn, name):
    t = gu.shape[0]

    def body(gu_ref, w_ref, r_ref, act_ref, o_ref):
        @pl.when(pl.program_id(1) == 0)
        def _():
            gate = gu_ref[:, 0:D_FF].astype(F32)
            act_ref[...] = (_silu(gate) * gu_ref[:, D_FF:2 * D_FF].astype(F32)).astype(act_ref.dtype)
        o_ref[...] = jnp.dot(act_ref[...], w_ref[...], preferred_element_type=F32) + r_ref[...]

    return pl.pallas_call(
        body, name=name, grid=(t // tm, D // tn),
        in_specs=[pl.BlockSpec((tm, 2 * D_FF), lambda i, j: (i, 0)), pl.BlockSpec((D_FF, tn), lambda i, j: (0, j)),
                  pl.BlockSpec((tm, tn), lambda i, j: (i, j))],
        out_specs=[pl.BlockSpec((tm, D_FF), lambda i, j: (i, 0)), pl.BlockSpec((tm, tn), lambda i, j: (i, j))],
        out_shape=[jax.ShapeDtypeStruct((t, D_FF), _MXU), jax.ShapeDtypeStruct((t, D), F32)],
        compiler_params=_cp(("parallel", "arbitrary")),
    )(gu, wfo, residual)


def _ffn_bwd_act(dh, wfo, gu, *, tm, name):
    t = dh.shape[0]

    def body(dh_ref, w_ref, gu_ref, o_ref):
        dact = _dot_nt(dh_ref[...], w_ref[...])
        g = gu_ref[:, 0:D_FF].astype(F32)
        u = gu_ref[:, D_FF:2 * D_FF].astype(F32)
        sg = _sigmoid(g)
        o_ref[:, 0:D_FF] = (dact * u * (sg * (1.0 + g * (1.0 - sg)))).astype(o_ref.dtype)
        o_ref[:, D_FF:2 * D_FF] = (dact * (g * sg)).astype(o_ref.dtype)

    return pl.pallas_call(
        body, name=name, grid=(t // tm,),
        in_specs=[pl.BlockSpec((tm, D), lambda i: (i, 0)), pl.BlockSpec((D_FF, D), lambda i: (0, 0)),
                  pl.BlockSpec((tm, 2 * D_FF), lambda i: (i, 0))],
        out_specs=pl.BlockSpec((tm, 2 * D_FF), lambda i: (i, 0)),
        out_shape=jax.ShapeDtypeStruct((t, 2 * D_FF), _MXU),
        compiler_params=_cp(("parallel",)),
    )(dh, wfo, gu)


def _outproj_bwd(dh, wout, ta, tb, proj, bgate, dproj, *, tm, name):
    t = dh.shape[0]

    def body(dh_ref, w_ref, ta_ref, tb_ref, g_ref, b_ref, dta_ref, dtb_ref, dg_ref, db_ref):
        @pl.when(pl.program_id(0) == 0)
        def _():
            db_ref[...] = jnp.zeros_like(db_ref)
        dm = _dot_nt(dh_ref[...], w_ref[...])
        ga = _sigmoid(g_ref[:, 0:D] + b_ref[:, 0:D])
        gb = _sigmoid(g_ref[:, D:2 * D] + b_ref[:, D:2 * D])
        dta_ref[...] = (dm * ga).astype(dta_ref.dtype)
        dtb_ref[...] = (dm * gb).astype(dtb_ref.dtype)
        dga = dm * ta_ref[...].astype(F32) * ga * (1.0 - ga)
        dgb = dm * tb_ref[...].astype(F32) * gb * (1.0 - gb)
        dg_ref[:, 0:D] = dga.astype(dg_ref.dtype)
        dg_ref[:, D:2 * D] = dgb.astype(dg_ref.dtype)
        db_ref[0:1, 0:D] += _rsum(dga)
        db_ref[0:1, D:2 * D] += _rsum(dgb)

    row = lambda cols: pl.BlockSpec((tm, cols), lambda i: (i, 0))
    return pl.pallas_call(
        body, name=name, grid=(t // tm,),
        in_specs=[row(D), pl.BlockSpec((D, D), lambda i: (0, 0)), row(D), row(D),
                  pl.BlockSpec((tm, 2 * D), lambda i: (i, G0 // (2 * D))), pl.BlockSpec((1, 2 * D), lambda i: (0, 0))],
        out_specs=[row(D), row(D), pl.BlockSpec((tm, 2 * D), lambda i: (i, G0 // (2 * D))),
                   pl.BlockSpec((8, 2 * D), lambda i: (0, 0))],
        out_shape=[jax.ShapeDtypeStruct((t, D), _MXU), jax.ShapeDtypeStruct((t, D), _MXU),
                   jax.ShapeDtypeStruct(dproj, _MXU), jax.ShapeDtypeStruct((8, 2 * D), F32)],
        compiler_params=_cp(("arbitrary",)),
    )(dh, wout, ta, tb, proj, bgate)


def _loss_head(h, gf, target, *, tm, name):
    t = h.shape[0]

    def body(h_ref, g_ref, t_ref, loss_ref, dg_ref, dh_ref):
        @pl.when(pl.program_id(0) == 0)
        def _():
            loss_ref[...] = jnp.zeros_like(loss_ref)
            dg_ref[...] = jnp.zeros_like(dg_ref)
        x = h_ref[...]
        r = lax.rsqrt(jnp.mean(x * x, axis=-1, keepdims=True) + EPS)
        xh = x * r
        err = xh * g_ref[...] - t_ref[...]
        loss_ref[...] += 0.5 * jnp.sum(jnp.mean(err * err, axis=-1, keepdims=True), axis=0, keepdims=True)
        dy = err * (1.0 / D)
        dg_ref[0:1, :] += _rsum(dy * xh)
        dxh = dy * g_ref[...]
        dh_ref[...] = r * (dxh - xh * jnp.mean(dxh * xh, axis=-1, keepdims=True))

    row = pl.BlockSpec((tm, D), lambda i: (i, 0))
    return pl.pallas_call(
        body, name=name, grid=(t // tm,),
        in_specs=[row, pl.BlockSpec((1, D), lambda i: (0, 0)), row],
        out_specs=[pl.BlockSpec((8, 128), lambda i: (0, 0)), pl.BlockSpec((8, D), lambda i: (0, 0)), row],
        out_shape=[jax.ShapeDtypeStruct((8, 128), F32), jax.ShapeDtypeStruct((8, D), F32), jax.ShapeDtypeStruct((t, D), F32)],
        compiler_params=_cp(("arbitrary",)),
    )(h, gf, target)


def _row_tile(rows, cols, limit_bytes=1 << 20):
    best = None
    for tr in range(8, rows + 1, 8):
        if rows % tr == 0 and tr * cols * 4 <= limit_bytes:
            best = tr
    return best if best is not None else rows


def _adamw(w, g, m, v, *, name):
    rows, cols = w.shape
    tr = _row_tile(rows, cols)

    def body(w_ref, g_ref, m_ref, v_ref, d_ref, nm_ref, nv_ref):
        gv = g_ref[...]
        nm = ADAM_B1 * m_ref[...] + (1.0 - ADAM_B1) * gv
        nv = ADAM_B2 * v_ref[...] + (1.0 - ADAM_B2) * (gv * gv)
        m_hat = nm / (1.0 - ADAM_B1 ** ADAM_STEP)
        v_hat = nv / (1.0 - ADAM_B2 ** ADAM_STEP)
        d_ref[...] = -ADAM_LR * (m_hat / (jnp.sqrt(v_hat) + ADAM_EPS) + ADAM_WD * w_ref[...])
        nm_ref[...] = nm
        nv_ref[...] = nv

    blk = pl.BlockSpec((tr, cols), lambda i: (i, 0))
    shp = jax.ShapeDtypeStruct((rows, cols), F32)
    return pl.pallas_call(
        body, name=name, grid=(rows // tr,), in_specs=[blk] * 4, out_specs=[blk] * 3, out_shape=[shp] * 3,
        compiler_params=_cp(("parallel",)),
    )(w, g, m, v)


def _bd256(w):
    w4 = w.reshape(4, 4, 64, 64)
    eye = jnp.eye(4, dtype=w.dtype)
    return (w4[:, :, :, None, :] * eye[None, :, None, :, None]).reshape(4, 256, 256)


def _bd256_diag(g):
    g5 = g.reshape(4, 4, 64, 4, 64)
    return jnp.stack([g5[:, a, :, a, :] for a in range(4)], axis=1).reshape(16, 64, 64)


FFN_SHARD = 2 * D_FF // 4
W_IN_SHARD = IN_DIM // 4
W_IN_ROWS = 9344


def _w_in_cols(shards, c0, c1):
    out = []
    for p in range(4):
        lo, hi = max(c0, W_IN_SHARD * p), min(c1, W_IN_SHARD * (p + 1))
        if lo < hi:
            out.append(shards[p][:, lo - W_IN_SHARD * p:hi - W_IN_SHARD * p])
    return out


def _in_proj_weights(win):
    lblk = [_w_in_cols(win, 256 * j, 256 * (j + 1)) + _w_in_cols(win, D + 256 * j, D + 256 * (j + 1)) for j in range(4)]
    wp = jnp.concatenate(_w_in_cols(win, 2048, 4096) + _w_in_cols(win, 4096, 7168) + lblk[0] + lblk[1]
                         + _w_in_cols(win, 7200, 9248) + lblk[2] + lblk[3], axis=1)
    wdt = jnp.pad(jnp.concatenate(_w_in_cols(win, 7168, 7200), axis=1), ((0, 0), (0, DT_PAD - N_HEADS)))
    return wp, wdt


def _layer_weights(w, conv, small, l, wp, wdt):
    row = lambda v: v.reshape(1, -1)
    pad_h = lambda v: jnp.pad(v.reshape(1, -1), ((0, 0), (0, DT_PAD - N_HEADS)))
    lw = dict(cw=conv["lru_conv_w"][l], cb=row(small["lru_conv_b"][l]),
              wa=_bd256(small["lru_w_a"][l]).astype(_MXU), wx=_bd256(small["lru_w_x"][l]).astype(_MXU),
              ba=row(small["lru_b_a"][l]), bx=row(small["lru_b_x"][l]), lam=row(small["lru_lambda"][l]))
    sw = dict(cw=conv["ssd_conv_w"][l], cb=row(small["ssd_conv_b"][l]), dtb=pad_h(small["ssd_dt_bias"][l]),
              alog=pad_h(small["ssd_A_log"][l]), dsk=row(jnp.repeat(small["ssd_D"][l], HEAD_P)),
              ng=row(small["ssd_norm_g"][l]))
    return dict(wp=wp, wdt=wdt, lw=lw, sw=sw, wba=w["w_branch"][0:D], wbb=w["w_branch"][D:3 * D],
                wout=w["w_out"], wfi=w["w_ffn_in"], wfo=w["w_ffn_out"],
                g1=row(small["norm1_g"][l]), g2=row(small["norm2_g"][l]), bgate=row(small["b_gate"][l]))


def _tiles(t):
    return dict(tmi=min(2048, t), tmn=min(1024, t), tm=min(512, t), r=min(256, t), rb=min(128, t))


def _layer_fwd(h, w, conv, small, l, carried=None):
    tl = _tiles(h.shape[0])
    n = f"l{l}_"
    carried = carried or {}
    arrived = []

    def carry(kernel, key, n_main, *args, **kw):
        comm, finish = carried.get(key, (None, None))
        outs = list(kernel(*args, comm=comm, **kw))
        if comm is not None:
            arrived.append(finish(outs[n_main:]))
        return outs[:n_main]

    wp, wdt = _in_proj_weights(w["w_in"])
    xn, proj = carry(_norm_mm, "in_proj", 2, h, small["norm1_g"][l].reshape(1, -1), wp, tm=tl["tmi"], tn=1024,
                     name=n + "in_proj")
    w = dict(w)
    for layer, ws in arrived:
        if layer == l:
            w.update(ws)
    lwt = _layer_weights(w, conv, small, l, wp, wdt)
    dtraw = _mm_nn(xn, lwt["wdt"], tm=tl["tm"], tn=DT_PAD, name=n + "dt_proj")
    hl, ya, gates = carry(_lru_fwd, "lru", 3, proj, lwt["lw"], r=tl["r"], name=n + "lru_fwd")
    yssd, yb, states, xc, dsl = carry(_ssd_fwd, "ssd", 5, proj, dtraw, lwt["sw"], rb=tl["rb"], name=n + "ssd_fwd")
    ta, tb, merged = _branch_merge(ya, yb, proj, lwt["wba"], lwt["wbb"], lwt["bgate"], tm=tl["tmn"], tn=512, name=n + "merge")
    hmid = _mm_nn(merged, lwt["wout"], tm=tl["tmn"], tn=512, name=n + "out_proj", residual=h)
    xn2, gu = _norm_mm(hmid, lwt["g2"], lwt["wfi"], tm=tl["tmi"], tn=FFN_SHARD, name=n + "ffn_in", out_dtype=_MXU)
    act, hout = _swiglu_mm(gu, lwt["wfo"], hmid, tm=tl["tm"], tn=512, name=n + "ffn_out")
    saved = dict(h=h, xn=xn, proj=proj, dtraw=dtraw, hl=hl, ya=ya, gates=gates, yssd=yssd, yb=yb, states=states, xc=xc, dsl=dsl, ta=ta, tb=tb,
                 merged=merged, hmid=hmid, xn2=xn2, gu=gu, act=act)
    return hout, saved, lwt, [x for x in arrived if x[0] != l]


def _layer_bwd(dh, s, lwt, l, hooks=None):
    t = dh.shape[0]
    tl = _tiles(t)
    n = f"l{l}_"
    tt = tl["tmn"]
    big = {}
    hooks = hooks or {}

    def wgrad(key, a, b, name, **kw):
        big[key] = _wgrad(a, b, tt=tt, name=n + name, into=big.get(key), **kw)

    dgu = _ffn_bwd_act(dh, lwt["wfo"], s["gu"], tm=tl["tm"], name=n + "ffn_act_bwd")
    wgrad("w_ffn_out", s["act"], dh, "ffn_out_wgrad", ta=D_FF, tn=1024, out_shape=(D_FF, D),
          out_block=(D_FF, 1024), out_index=lambda o, j: (o, j))
    wgrad("w_ffn_in", s["xn2"], dgu, "ffn_in_wgrad", ta=D, tn=FFN_SHARD, out_shape=(4, D, FFN_SHARD),
          out_block=(None, D, FFN_SHARD), out_index=lambda o, j: (j, o, 0))
    dh1, dg2 = _mm_nt_rmsbwd(dgu, lwt["wfi"], s["hmid"], lwt["g2"], dh, tm=tl["tmn"], tk=FFN_SHARD, name=n + "ffn_in_dgrad")
    dta, dtb, dproj, dbg = _outproj_bwd(dh1, lwt["wout"], s["ta"], s["tb"], s["proj"], lwt["bgate"], (t, NP),
                                        tm=tl["tm"], name=n + "out_proj_bwd")
    rows_d = dict(ta=D, tn=512, out_block=(D, 512), out_index=lambda o, j: (o, j))
    wgrad("w_out", s["merged"], dh1, "out_proj_wgrad", out_shape=(D, D), **rows_d)
    dya = _mm_nt(dta, lwt["wba"], tm=tl["tm"], name=n + "branch_a_dgrad")
    dyb = _mm_nt(dtb, lwt["wbb"], tm=tl["tm"], name=n + "branch_b_dgrad")
    wgrad("w_branch", s["ya"], dta, "branch_a_wgrad", out_shape=(3 * D, D), a_tab=[0], o_tab=[0], **rows_d)
    wgrad("w_branch", s["yb"], dtb, "branch_b_wgrad", out_shape=(3 * D, D), a_tab=[0, 1], o_tab=[1, 2], **rows_d)
    comm_1 = hooks["lru"](big) if "lru" in hooks else None
    dproj, lsm, dwa, dwx, *got_1 = _lru_bwd(s["proj"], s["hl"], s["gates"], dya, dproj, lwt["lw"], r=tl["r"], name=n + "lru_bwd",
                                            comm=comm_1)
    comm_2 = hooks["ssd"](got_1) if "ssd" in hooks else None
    dproj, ddt, gconv, gch, ghd, *got_2 = _ssd_bwd(s["proj"], s["dtraw"], s["yssd"], s["states"], s["xc"], s["dsl"], dyb, dproj, lwt["sw"],
                                                   rb=tl["rb"], name=n + "ssd_bwd", comm=comm_2)
    lsm = lsm.reshape(8, 8, D).sum(axis=1)
    gconv = gconv.reshape(5, 8, XBC).sum(axis=1)
    gch = gch.reshape(2, 8, SSD_INNER).sum(axis=1)
    w_in = dict(tn=D, out_shape=(W_IN_ROWS, D), out_index=lambda o, j: (o, j))
    wgrad("w_in", dproj, s["xn"], "in_proj_wgrad", ta=1024, out_block=(1024, D),
          a_tab=list(range(9)), o_tab=[2, 3, 4, 5, 6, 0, 7, 8, 1], **w_in)
    wgrad("w_in", ddt, s["xn"], "dt_proj_wgrad", ta=DT_PAD, out_block=(DT_PAD, D), a_tab=[0],
          o_tab=[NP // DT_PAD], **w_in)
    comm_3 = hooks["in_dgrad"](big) if "in_dgrad" in hooks else None
    dh0, dg1, *got_3 = _mm_nt_rmsbwd(dproj, lwt["wp"], s["h"], lwt["g1"], dh1, tm=tl["tmn"], tk=2304,
                                     name=n + "in_proj_dgrad", extra=(ddt, lwt["wdt"]), comm=comm_3)
    grads = dict(
        lru_conv_w=lsm[0:4], lru_conv_b=lsm[4], lru_b_a=lsm[5], lru_b_x=lsm[6], lru_lambda=lsm[7],
        lru_w_a=_bd256_diag(dwa), lru_w_x=_bd256_diag(dwx),
        ssd_conv_w=gconv[0:4], ssd_conv_b=gconv[4], ssd_norm_g=gch[0], ssd_D=gch[1].reshape(N_HEADS, HEAD_P).sum(axis=-1),
        ssd_dt_bias=ghd[0, 0:N_HEADS], ssd_A_log=ghd[1, 0:N_HEADS],
        b_gate=dbg[0], norm1_g=dg1[0], norm2_g=dg2[0])
    return dh0, grads, big, (got_2, got_3)


def _local_step(x, target, w, conv, small, prefetch=None, early_reduce=None):
    h = x
    w = [dict(wl) for wl in w]
    lwts, saved = [], []
    for l in range(N_LAYERS):
        h, s, lwt, arrived = _layer_fwd(h, w[l], conv, small, l, prefetch if l == 0 else None)
        for layer, ws in arrived:
            w[layer].update(ws)
        lwts.append(lwt)
        saved.append(s)
    loss_blk, dgf, dh = _loss_head(h, small["norm_f"].reshape(1, D), target, tm=_tiles(x.shape[0])["tm"], name="loss_head")
    per_layer, big, carried = [None] * N_LAYERS, [None] * N_LAYERS, None
    for l in reversed(range(N_LAYERS)):
        hooks = early_reduce(big[1]) if (early_reduce is not None and l == 0) else None
        dh, per_layer[l], big[l], carried = _layer_bwd(dh, saved[l], lwts[l], l, hooks)
    grads = {k: jnp.stack([per_layer[l][k] for l in range(N_LAYERS)], axis=0) for k in per_layer[0]}
    grads["norm_f"] = dgf[0]
    return loss_blk, dh, grads, big, carried


PACK_W = 1024
BIG = (("w_in", W_IN_SHARD, D, W_IN_SHARD, 256), ("w_branch", 768, D, 256, D), ("w_out", 256, D, 256, D),
       ("w_ffn_in", D, FFN_SHARD, 256, FFN_SHARD), ("w_ffn_out", 704, D, 352, D))
CONV = ("lru_conv_w", "ssd_conv_w")
SMALL = ("norm1_g", "b_gate", "lru_conv_b", "lru_w_a", "lru_b_a", "lru_w_x", "lru_b_x", "lru_lambda", "ssd_conv_b",
         "ssd_dt_bias", "ssd_A_log", "ssd_D", "ssd_norm_g", "norm2_g", "norm_f")
_WIRE = jnp.bfloat16
N_CHIPS = 4
N_DEV = 8


def _mesh_pos():
    return lax.axis_index("x"), lax.axis_index("y"), lax.axis_index("c")


HBM_SPEC = pl.BlockSpec(memory_space=pltpu.HBM)


def _remote(src, dst, send_sems, recv_sems, k, to):
    return pltpu.make_async_remote_copy(src_ref=src, dst_ref=dst, send_sem=send_sems.at[k], recv_sem=recv_sems.at[k],
                                        device_id=to, device_id_type=MESH)


def _other_chips(x, y):
    return [(1 - x, y), (x, 1 - y), (1 - x, 1 - y)]


def _weight_fetch(loc, layer, owner):
    names = list(owner)
    rows = {n: loc[n].shape[1] for n in names}
    by_chip = ("w_in", "w_ffn_in")
    shapes = [((N_CHIPS,) + loc[n].shape[1:]) if n in by_chip else (N_CHIPS * rows[n], D) for n in names]

    def place(o_ref, n, chip):
        if n in by_chip:
            return o_ref.at[chip]
        return o_ref.at[pl.ds(pl.multiple_of(chip * rows[n], 16), rows[n]), :]

    def step(which, in_refs, o_refs, send_sems, recv_sems):
        x, y, c = _mesh_pos()
        s = 2 * x + y
        sib = (x, y, 1 - c)
        chips = _other_chips(x, y)
        for core in (0, 1):
            @pl.when(c == core)
            def _():
                for k, n in enumerate(names):
                    for j, (px, py) in enumerate(chips):
                        landed = place(o_refs[k], n, 2 * px + py)
                        sent = _remote(in_refs[k].at[layer], place(o_refs[k], n, s), send_sems, recv_sems, 3 * k + j,
                                       (px, py, c))
                        arrives = _remote(in_refs[k].at[layer], landed, send_sems, recv_sems, 3 * k + j, (px, py, c))
                        passed = _remote(landed, landed, send_sems, recv_sems, 3 * (len(names) + k) + j, sib)
                        if owner[n] == core:
                            if which == "start":
                                sent.start()
                            elif which == "mid":
                                arrives.wait_recv()
                                passed.start()
                            else:
                                sent.wait_send()
                                passed.wait_send()
                        elif which == "end":
                            passed.wait_recv()

    return dict(inputs=[loc[n] for n in names], names=names,
                out_shapes=[jax.ShapeDtypeStruct(shp, loc[n].dtype) for shp, n in zip(shapes, names)],
                sems=[pltpu.SemaphoreType.DMA((6 * len(names),)), pltpu.SemaphoreType.DMA((6 * len(names),))],
                start=functools.partial(step, "start"), mid=functools.partial(step, "mid"),
                end=functools.partial(step, "end"))


def _comm_now(comm, name):
    n, no = len(comm["inputs"]), len(comm["out_shapes"])

    def body(*refs):
        parts = (refs[:n], refs[n:n + no]) + tuple(refs[n + no:])
        comm["start"](*parts)
        comm["mid"](*parts)
        comm["end"](*parts)

    return pl.pallas_call(
        body, name=name, in_specs=[HBM_SPEC] * n, out_specs=[HBM_SPEC] * no, out_shape=comm["out_shapes"],
        scratch_shapes=comm["sems"],
    )(*comm["inputs"])


def _sibling_send(bufs, layer):
    n = len(bufs)

    def step(which, in_refs, o_refs, send_sems, recv_sems):
        x, y, c = _mesh_pos()
        copies = [_remote(in_refs[k], o_refs[k], send_sems, recv_sems, k, (x, y, 1 - c)) for k in range(n)]

        @pl.when(c != layer)
        def _():
            for cp in copies:
                if which == "start":
                    cp.start()
                elif which == "end":
                    cp.wait_send()

        @pl.when(c == layer)
        def _():
            for cp in copies:
                if which == "end":
                    cp.wait_recv()

    return dict(inputs=list(bufs), out_shapes=[jax.ShapeDtypeStruct(b.shape, b.dtype) for b in bufs],
                sems=[pltpu.SemaphoreType.DMA((n,)), pltpu.SemaphoreType.DMA((n,))],
                start=functools.partial(step, "start"), mid=functools.partial(step, "mid"),
                end=functools.partial(step, "end"))


def _add_cast(g, recv, own, *, a, tr, tc, name):
    wd = g.shape[1]
    nr = a // tr

    def body(own_ref, g_ref, r_ref, o_ref):
        @pl.when(own_ref[0] == 1)
        def _():
            o_ref[...] = (g_ref[...] + r_ref[...]).astype(o_ref.dtype)

    blk = pl.BlockSpec((tr, tc), lambda p, i, j, own_ref: ((p * nr + i) * own_ref[0], j * own_ref[0]))
    return pl.pallas_call(
        body, name=name,
        grid_spec=pltpu.PrefetchScalarGridSpec(
            num_scalar_prefetch=1, grid=(N_CHIPS, nr, wd // tc), in_specs=[blk, blk],
            out_specs=pl.BlockSpec((None, tr, tc), lambda p, i, j, own_ref: (p * own_ref[0], i * own_ref[0], j * own_ref[0]))),
        out_shape=jax.ShapeDtypeStruct((N_CHIPS, a, wd), _WIRE),
        compiler_params=_cp(("arbitrary", "arbitrary", "arbitrary")),
    )(own, g, recv)


def _chip_exchange(parts, layer):
    n = len(parts)

    def step(which, s_refs, o_refs, send_sems, recv_sems):
        x, y, c = _mesh_pos()
        s = 2 * x + y

        @pl.when(c == layer)
        def _():
            for j, (px, py) in enumerate(_other_chips(x, y)):
                for k in range(n):
                    p = 2 * px + py
                    sent = _remote(s_refs[k].at[p], o_refs[k].at[s], send_sems, recv_sems, n * j + k, (px, py, c))
                    if which == "start":
                        sent.start()
                    elif which == "end":
                        _remote(s_refs[k].at[p], o_refs[k].at[p], send_sems, recv_sems, n * j + k, (px, py, c)).wait_recv()
                        sent.wait_send()

    return dict(inputs=list(parts), out_shapes=[jax.ShapeDtypeStruct(p.shape, p.dtype) for p in parts],
                sems=[pltpu.SemaphoreType.DMA((3 * n,)), pltpu.SemaphoreType.DMA((3 * n,))],
                start=functools.partial(step, "start"), mid=functools.partial(step, "mid"),
                end=functools.partial(step, "end"))


def _sum_slots(slots, own, sel, *, tr, tc, name, layer=None, into=None):
    n, rows, wd = slots.shape
    k = own.shape[0]

    def body(sel_ref, s_ref, own_ref, *rest):
        o_ref = rest[-1]

        @pl.when(sel_ref[1] == 1)
        def _():
            mine = sel_ref[0]
            acc = jnp.zeros((tr, tc), F32)
            for p in range(n):
                acc = acc + jnp.where(mine == p, own_ref[...].astype(F32), s_ref[p].astype(F32))
            o_ref[...] = acc

    if layer is not None:
        out_spec = pl.BlockSpec((None, tr, tc), lambda i, j, sel_ref: (layer, i * sel_ref[1], j * sel_ref[1]))
        out_shape = jax.ShapeDtypeStruct((N_LAYERS, rows, wd), F32)
    else:
        out_spec = pl.BlockSpec((tr, tc), lambda i, j, sel_ref: (i * sel_ref[1], j * sel_ref[1]))
        out_shape = jax.ShapeDtypeStruct((rows, wd), F32)
    in_specs = [pl.BlockSpec((n, tr, tc), lambda i, j, sel_ref: (0, i * sel_ref[1], j * sel_ref[1])),
                pl.BlockSpec((None, tr, tc), lambda i, j, sel_ref: (sel_ref[0] if k > 1 else 0, i * sel_ref[1],
                                                                    j * sel_ref[1]))]
    args = [sel, slots, own]
    if into is not None:
        in_specs.append(pl.BlockSpec(memory_space=pl.ANY))
        args.append(into)
    return pl.pallas_call(
        body, name=name,
        grid_spec=pltpu.PrefetchScalarGridSpec(num_scalar_prefetch=1, grid=(rows // tr, wd // tc), in_specs=in_specs,
                                               out_specs=out_spec),
        out_shape=out_shape, input_output_aliases={3: 0} if into is not None else {},
        compiler_params=_cp(("arbitrary", "arbitrary")),
    )(*args)


def _sibling_share(both):
    n = len(both)

    def body(*refs):
        o_refs, (send_sems, recv_sems) = refs[n:2 * n], refs[2 * n:]
        x, y, c = _mesh_pos()
        sends = [_remote(o_refs[k].at[c], o_refs[k].at[c], send_sems, recv_sems, k, (x, y, 1 - c)) for k in range(n)]
        for cp in sends:
            cp.start()
        for k in range(n):
            _remote(o_refs[k].at[1 - c], o_refs[k].at[1 - c], send_sems, recv_sems, k, (x, y, 1 - c)).wait_recv()
        for cp in sends:
            cp.wait_send()

    return pl.pallas_call(
        body, name="grad_sibling_share", in_specs=[HBM_SPEC] * n, out_specs=[HBM_SPEC] * n,
        out_shape=[jax.ShapeDtypeStruct(b.shape, b.dtype) for b in both], input_output_aliases={k: k for k in range(n)},
        scratch_shapes=[pltpu.SemaphoreType.DMA((n,)), pltpu.SemaphoreType.DMA((n,))],
    )(*both)


def _allgather_devices(part):
    rows, wd = part.shape

    def step(which, in_refs, o_refs, send_sems, recv_sems):
        (p_ref,), (o_ref,) = in_refs, o_refs
        x, y, c = _mesh_pos()
        sib = (x, y, 1 - c)
        chips = _other_chips(x, y)
        slot = lambda px, py, pc: o_ref.at[4 * px + 2 * py + pc]
        first = [_remote(p_ref, slot(x, y, c), send_sems, recv_sems, 0, sib)]
        first += [_remote(p_ref, slot(x, y, c), send_sems, recv_sems, 1 + j, (px, py, c)) for j, (px, py) in enumerate(chips)]
        passed = [_remote(slot(px, py, c), slot(px, py, c), send_sems, recv_sems, 4 + j, sib)
                  for j, (px, py) in enumerate(chips)]
        if which == "start":
            for cp in first:
                cp.start()
        elif which == "mid":
            for j, (px, py) in enumerate(chips):
                _remote(p_ref, slot(px, py, c), send_sems, recv_sems, 1 + j, (px, py, c)).wait_recv()
                passed[j].start()
        else:
            _remote(p_ref, slot(x, y, 1 - c), send_sems, recv_sems, 0, sib).wait_recv()
            for j, (px, py) in enumerate(chips):
                _remote(slot(px, py, 1 - c), slot(px, py, 1 - c), send_sems, recv_sems, 4 + j, sib).wait_recv()
            for cp in first + passed:
                cp.wait_send()

    return dict(inputs=[part], out_shapes=[jax.ShapeDtypeStruct((N_DEV, rows, wd), part.dtype)],
                sems=[pltpu.SemaphoreType.DMA((N_DEV - 1,)), pltpu.SemaphoreType.DMA((N_DEV - 1,))],
                start=functools.partial(step, "start"), mid=functools.partial(step, "mid"),
                end=functools.partial(step, "end"))


def _comm_both(a, b):
    na, nao = len(a["inputs"]), len(a["out_shapes"])

    def step(which, in_refs, o_refs, sa, ra, sb, rb_):
        a[which](in_refs[:na], o_refs[:nao], sa, ra)
        b[which](in_refs[na:], o_refs[nao:], sb, rb_)

    return dict(inputs=a["inputs"] + b["inputs"], out_shapes=a["out_shapes"] + b["out_shapes"], sems=a["sems"] + b["sems"],
                start=functools.partial(step, "start"), mid=functools.partial(step, "mid"),
                end=functools.partial(step, "end"))


def _by_chip_to_full(stack):
    _, nl, r, b = stack.shape
    return stack.transpose(1, 2, 0, 3).reshape(nl, r, N_CHIPS * b)


def _sharded_step(a):
    x = a["x"][0]
    target = a["loss_target"][0]
    cx, cy, cc = _mesh_pos()
    chip = (2 * cx + cy).astype(jnp.int32)
    core = cc.astype(jnp.int32)
    me = (4 * cx + 2 * cy + cc).astype(jnp.int32)
    zero = jnp.zeros((), jnp.int32)
    dus = lax.dynamic_update_slice

    loc = {n: a[n].astype(_MXU) for n, *_ in BIG}

    def with_own(got, names, layer):
        out = {}
        for g, n in zip(got, names):
            mine = loc[n][layer]
            out[n] = (dus(g, mine[None], (chip, zero, zero)) if g.ndim == 3 else dus(g, mine, (chip * mine.shape[0], zero)))
        return out

    rest = {"w_ffn_in": 0, "w_branch": 1, "w_out": 1, "w_ffn_out": 1}
    conv_loc = jnp.concatenate([a[n].reshape(-1, PACK_W) for n in CONV], axis=0)
    now = _weight_fetch(loc, 0, {"w_in": 0})
    conv_all, *got_now = _comm_now(_comm_both(_allgather_devices(conv_loc), now), "allgather_weights")
    w0 = with_own(got_now, now["names"], 0)
    later = {"in_proj": (0, _weight_fetch(loc, 0, rest)), "lru": (1, _weight_fetch(loc, 1, {"w_in": 0})),
             "ssd": (1, _weight_fetch(loc, 1, rest))}
    prefetch = {k: (f, functools.partial(lambda got, layer, f: (layer, with_own(got, f["names"], layer)), layer=layer, f=f))
                for k, (layer, f) in later.items()}
    conv_all = dus(conv_all, conv_loc[None], (me, zero, zero))[0::2]
    conv, off = {}, 0
    for n in CONV:
        rows = a[n].size // PACK_W
        conv[n] = _by_chip_to_full(conv_all[:, off:off + rows].reshape((N_CHIPS,) + a[n].shape))
        off += rows
    small = {n: a[n] for n in SMALL}

    views = lambda big_l, specs: [big_l[n].reshape(-1, wd) for n, _, wd, _, _ in specs]
    owns = lambda layer: (core == layer).astype(jnp.int32)
    w_in_only, others = BIG[:1], BIG[1:]

    def partial_sums(big_l, recv, layer, specs):
        return [_add_cast(v, r, owns(layer).reshape(1), a=rows, tr=tr, tc=tc, name=f"grad_add_sibling_l{layer}_{n}")
                for v, r, (n, rows, _, tr, tc) in zip(views(big_l, specs), recv, specs)]

    def reduced(slots, parts, layer, into, specs):
        sel = jnp.stack([chip, owns(layer)])
        return [_sum_slots(s, p, sel, tr=tr, tc=tc, name=f"grad_sum_chips_l{layer}_{n}", layer=layer, into=buf)
                for s, p, buf, (n, _, _, tr, tc) in zip(slots, parts, into, specs)]

    kept = {}

    def early_reduce(big_1):
        def during_lru(big_0):
            kept["big_0"] = dict(big_0)
            return _comm_both(_sibling_send(views(big_1, BIG), 1), _sibling_send(views(big_0, others), 0))

        def during_ssd(recv):
            kept["parts_1"] = partial_sums(big_1, recv[:len(BIG)], 1, BIG)
            kept["parts_0"] = partial_sums(kept["big_0"], recv[len(BIG):], 0, others)
            return _comm_both(_chip_exchange(kept["parts_1"], 1), _chip_exchange(kept["parts_0"], 0))

        return dict(lru=during_lru, ssd=during_ssd, in_dgrad=lambda big_0: _sibling_send(views(big_0, w_in_only), 0))

    loss_blk, grad_x, grads, big, (slots, recv_in) = _local_step(x, target, [w0, {}], conv, small, prefetch, early_reduce)
    loss = lax.psum(loss_blk[0, 0], ("x", "y", "c"))
    both = reduced(slots[:len(BIG)], kept["parts_1"], 1, [None] * len(BIG), BIG)
    both[1:] = reduced(slots[len(BIG):], kept["parts_0"], 0, both[1:], others)
    parts_in = partial_sums(big[0], recv_in, 0, w_in_only)
    names = SMALL + CONV
    srows = -(-sum(grads[n].size for n in names) // (8 * PACK_W)) * 8
    flat = lambda d, ns: jnp.concatenate([d[n].reshape(-1) for n in ns])
    padto = lambda v: jnp.pad(v, (0, srows * PACK_W - v.shape[0])).reshape(srows, PACK_W)
    g_own = padto(flat(grads, names))
    g_all, *slots_in = _comm_now(_comm_both(_allgather_devices(g_own), _chip_exchange(parts_in, 0)), "grad_chip_exchange")
    both[:1] = reduced(slots_in, parts_in, 0, both[:1], w_in_only)
    done = dict(zip([n for n, *_ in BIG], _sibling_share(both)))
    g_big = {n: done[n].reshape(a[n].shape) for n in ("w_branch", "w_out", "w_ffn_in", "w_ffn_out")}
    gt = done["w_in"].transpose(0, 2, 1)
    first = jnp.concatenate([gt[..., 512 * j + 256 * part:512 * j + 256 * (part + 1)] for part in range(2) for j in range(4)]
                            + [gt[..., 2 * D:]], axis=-1)
    tail = W_IN_SHARD - (IN_DIM - 7168)
    last = jnp.concatenate([gt[..., :tail], gt[..., W_IN_SHARD - N_HEADS:], gt[..., tail:W_IN_SHARD - N_HEADS]], axis=-1)
    g_big["w_in"] = jnp.where(chip == 0, first, jnp.where(chip == N_CHIPS - 1, last, gt))

    g_sum = _sum_slots(g_all, g_own[None], jnp.stack([me, zero + 1]), tr=srows, tc=PACK_W, name="small_grad_sum")
    off, g_small = 0, {}
    for n in names:
        g_small[n] = g_sum.reshape(-1)[off:off + grads[n].size].reshape(grads[n].shape)
        off += grads[n].size
    for n in CONV:
        width = a[n].shape[2]
        g_big[n] = lax.dynamic_slice(g_small.pop(n), (zero, zero, chip * width), a[n].shape)

    out_g, out_d, out_m, out_v = {}, {}, {}, {}
    for n in g_big:
        shp = a[n].shape
        two_d = (shp[0] * shp[1], shp[2])
        d_, m_, v_ = _adamw(a[n].reshape(two_d), g_big[n].reshape(two_d), a["m_" + n].reshape(two_d),
                            a["v_" + n].reshape(two_d), name="adamw_" + n)
        out_g[n], out_d[n], out_m[n], out_v[n] = g_big[n], d_.reshape(shp), m_.reshape(shp), v_.reshape(shp)
    d_, m_, v_ = _adamw(padto(flat(a, SMALL)), padto(flat(g_small, SMALL)), padto(flat({n: a["m_" + n] for n in SMALL}, SMALL)),
                        padto(flat({n: a["v_" + n] for n in SMALL}, SMALL)), name="adamw_small")
    off = 0
    for n in SMALL:
        cut = lambda v: v.reshape(-1)[off:off + a[n].size].reshape(a[n].shape)
        out_g[n], out_d[n], out_m[n], out_v[n] = g_small[n], cut(d_), cut(m_), cut(v_)
        off += a[n].size
    return loss, grad_x[None], out_g, out_d, out_m, out_v


WEIGHTS = ("norm1_g", "w_in", "b_gate", "lru_conv_w", "lru_conv_b", "lru_w_a", "lru_b_a", "lru_w_x", "lru_b_x", "lru_lambda",
           "ssd_conv_w", "ssd_conv_b", "ssd_dt_bias", "ssd_A_log", "ssd_D", "ssd_norm_g", "w_branch", "w_out", "norm2_g",
           "w_ffn_in", "w_ffn_out", "norm_f")
INPUTS = ("x",) + WEIGHTS + ("loss_target",) + tuple("m_" + n for n in WEIGHTS) + tuple("v_" + n for n in WEIGHTS)


def kernel(x, norm1_g, w_in, b_gate, lru_conv_w, lru_conv_b, lru_w_a, lru_b_a, lru_w_x, lru_b_x, lru_lambda, ssd_conv_w, ssd_conv_b, ssd_dt_bias, ssd_A_log, ssd_D, ssd_norm_g, w_branch, w_out, norm2_g, w_ffn_in, w_ffn_out, norm_f, loss_target, m_norm1_g, m_w_in, m_b_gate, m_lru_conv_w, m_lru_conv_b, m_lru_w_a, m_lru_b_a, m_lru_w_x, m_lru_b_x, m_lru_lambda, m_ssd_conv_w, m_ssd_conv_b, m_ssd_dt_bias, m_ssd_A_log, m_ssd_D, m_ssd_norm_g, m_w_branch, m_w_out, m_norm2_g, m_w_ffn_in, m_w_ffn_out, m_norm_f, v_norm1_g, v_w_in, v_b_gate, v_lru_conv_w, v_lru_conv_b, v_lru_w_a, v_lru_b_a, v_lru_w_x, v_lru_b_x, v_lru_lambda, v_ssd_conv_w, v_ssd_conv_b, v_ssd_dt_bias, v_ssd_A_log, v_ssd_D, v_ssd_norm_g, v_w_branch, v_w_out, v_norm2_g, v_w_ffn_in, v_w_ffn_out, v_norm_f):
    args = (x, norm1_g, w_in, b_gate, lru_conv_w, lru_conv_b, lru_w_a, lru_b_a, lru_w_x, lru_b_x, lru_lambda, ssd_conv_w, ssd_conv_b, ssd_dt_bias, ssd_A_log, ssd_D, ssd_norm_g, w_branch, w_out, norm2_g, w_ffn_in, w_ffn_out, norm_f, loss_target, m_norm1_g, m_w_in, m_b_gate, m_lru_conv_w, m_lru_conv_b, m_lru_w_a, m_lru_b_a, m_lru_w_x, m_lru_b_x, m_lru_lambda, m_ssd_conv_w, m_ssd_conv_b, m_ssd_dt_bias, m_ssd_A_log, m_ssd_D, m_ssd_norm_g, m_w_branch, m_w_out, m_norm2_g, m_w_ffn_in, m_w_ffn_out, m_norm_f, v_norm1_g, v_w_in, v_b_gate, v_lru_conv_w, v_lru_conv_b, v_lru_w_a, v_lru_b_a, v_lru_w_x, v_lru_b_x, v_lru_lambda, v_ssd_conv_w, v_ssd_conv_b, v_ssd_dt_bias, v_ssd_A_log, v_ssd_D, v_ssd_norm_g, v_w_branch, v_w_out, v_norm2_g, v_w_ffn_in, v_w_ffn_out, v_norm_f)
    assert len(args) == len(INPUTS)
    loss, grad_x, g, d, m, v = _sharded_step(dict(zip(INPUTS, args)))
    return (loss, grad_x, *[g[n] for n in WEIGHTS], *[d[n] for n in WEIGHTS], *[m[n] for n in WEIGHTS],
            *[v[n] for n in WEIGHTS])
```

```python
import functools
import math

import numpy as np
import jax
import jax.numpy as jnp
from jax import lax
from jax.experimental import pallas as pl
from jax.experimental.pallas import tpu as pltpu

F32 = jnp.float32
_MXU = jnp.bfloat16
_HI = lax.Precision.HIGHEST

D = 1024
EPS = 1e-6
N_LAYERS = 2
LRU_C = 8.0
N_HEADS = 32
HEAD_P = 64
N_GROUPS = 4
N_STATE = 128
SSD_INNER = 2048
XBC = 3072
D_FF = 2816
CHUNK = 64
NORM_ROWS = 32
IN_DIM = 9248

NP = 9216
ZX_W = 5120
G0 = 6144
LBLK = 512
DT_PAD = 128

VMEM_LIMIT_BYTES_V7X = 56 * 1024 * 1024

ADAM_LR, ADAM_B1, ADAM_B2, ADAM_EPS, ADAM_WD, ADAM_STEP = 0.001, 0.9, 0.999, 1e-08, 0.01, 10
MESH = pl.DeviceIdType.MESH


def _cp(sem):
    return pltpu.CompilerParams(dimension_semantics=sem, vmem_limit_bytes=VMEM_LIMIT_BYTES_V7X)


def _lblk_col(j):
    return 10 + j + 4 * (j // 2)


def _sigmoid(x):
    return 0.5 * jnp.tanh(0.5 * x) + 0.5


def _softplus(x):
    return jnp.maximum(x, 0.0) + jnp.log(1.0 + jnp.exp(-jnp.abs(x)))


def _silu(x):
    return x * _sigmoid(x)


_GELU_C0 = math.sqrt(2.0 / math.pi)
_GELU_C1 = 0.044715


def _gelu_and_grad(x):
    t = jnp.tanh(_GELU_C0 * (x + _GELU_C1 * x * x * x))
    g = 0.5 * x * (1.0 + t)
    dg = 0.5 * (1.0 + t) + 0.5 * x * (1.0 - t * t) * _GELU_C0 * (1.0 + 3.0 * _GELU_C1 * x * x)
    return g, dg


def _one_minus_exp(x):
    p = 1.0 + x * (1.0 / 7.0)
    p = 1.0 + x * (1.0 / 6.0) * p
    p = 1.0 + x * (1.0 / 5.0) * p
    p = 1.0 + x * (1.0 / 4.0) * p
    p = 1.0 + x * (1.0 / 3.0) * p
    p = 1.0 + x * (1.0 / 2.0) * p
    return jnp.where(x > -0.3, -x * p, 1.0 - jnp.exp(x))


def _dot(a, b):
    return jnp.dot(a.astype(_MXU), b.astype(_MXU), preferred_element_type=F32)


def _dot_nt(a, b):
    return lax.dot_general(a.astype(_MXU), b.astype(_MXU), (((1,), (1,)), ((), ())), preferred_element_type=F32)


def _dot_tn(a, b):
    return lax.dot_general(a.astype(_MXU), b.astype(_MXU), (((0,), (0,)), ((), ())), preferred_element_type=F32)


def _shift_down(x, prev8, k):
    xr = pltpu.roll(x, k, 0)
    pr = pltpu.roll(prev8, k, 0)
    row = lax.broadcasted_iota(jnp.int32, prev8.shape, 0)
    head = jnp.where(row < k, pr, xr[0:8])
    return jnp.concatenate([head, xr[8:]], axis=0)


def _shift_up(x, next8, k):
    r = x.shape[0]
    xr = pltpu.roll(x, r - k, 0)
    nr = pltpu.roll(next8, 8 - k, 0)
    row = lax.broadcasted_iota(jnp.int32, next8.shape, 0)
    tail = jnp.where(row >= 8 - k, nr, xr[r - 8:r])
    return jnp.concatenate([xr[:r - 8], tail], axis=0)


def _conv4(x, prev8, w_ref, b_ref, cols=slice(None)):
    acc = x * w_ref[3:4, cols] + b_ref[0:1, cols]
    for k in (1, 2, 3):
        acc = acc + _shift_down(x, prev8, k) * w_ref[3 - k:4 - k, cols]
    return acc


def _lin_scan(a, b, reverse):
    r = a.shape[0]
    row = lax.broadcasted_iota(jnp.int32, a.shape, 0)
    d = 1
    while d < r:
        sh = (r - d) if reverse else d
        a_s = pltpu.roll(a, sh, 0)
        b_s = pltpu.roll(b, sh, 0)
        m = (row < r - d) if reverse else (row >= d)
        b = jnp.where(m, a * b_s + b, b)
        a = jnp.where(m, a * a_s, a)
        d *= 2
    return a, b


def _rsum(x):
    return jnp.sum(x, axis=0, keepdims=True)


def _comm_specs(comm):
    if comm is None:
        return [], [], [], [], []
    n = len(comm["inputs"])
    return list(comm["inputs"]), [HBM_SPEC] * n, [HBM_SPEC] * len(comm["out_shapes"]), list(comm["out_shapes"]), comm["sems"]


def _comm_steps(comm, refs, n_in, n_out, first, mid, last):
    ni, no = len(comm["inputs"]), len(comm["out_shapes"])
    parts = (refs[n_in:n_in + ni], refs[n_out:n_out + no]) + tuple(refs[len(refs) - len(comm["sems"]):])
    for when, what in ((first, "start"), (mid, "mid"), (last, "end")):
        @pl.when(when)
        def _():
            comm[what](*parts)


def _norm_mm(h, gamma, w, *, tm, tn, name, out_dtype=F32, comm=None):
    m, k = h.shape
    if w.ndim == 3:
        assert w.shape[2] == tn
        n = w.shape[0] * tn
        w_spec = pl.BlockSpec((None, k, tn), lambda i, j: (j, 0, 0))
    else:
        n = w.shape[1]
        w_spec = pl.BlockSpec((k, tn), lambda i, j: (0, j))

    c_args, c_in_specs, c_out_specs, c_out_shapes, c_sems = _comm_specs(comm)
    ni, nj = m // tm, n // tn

    def body(*refs):
        h_ref, g_ref, w_ref = refs[:3]
        xn_ref, o_ref = refs[3 + len(c_args):5 + len(c_args)]
        i, j = pl.program_id(0), pl.program_id(1)
        if comm is not None:
            _comm_steps(comm, refs, 3, 5 + len(c_args), (i == 0) & (j == 0), (i == (3 * ni) // 4) & (j == 0),
                        (i == ni - 1) & (j == nj - 1))

        @pl.when(j == 0)
        def _():
            x = h_ref[...]
            r = lax.rsqrt(jnp.mean(x * x, axis=-1, keepdims=True) + EPS)
            xn_ref[...] = ((x * r) * g_ref[...]).astype(xn_ref.dtype)
        o_ref[...] = jnp.dot(xn_ref[...], w_ref[...], preferred_element_type=F32).astype(o_ref.dtype)

    return pl.pallas_call(
        body, name=name, grid=(ni, nj),
        in_specs=[pl.BlockSpec((tm, k), lambda i, j: (i, 0)), pl.BlockSpec((1, k), lambda i, j: (0, 0)), w_spec] + c_in_specs,
        out_specs=[pl.BlockSpec((tm, k), lambda i, j: (i, 0)), pl.BlockSpec((tm, tn), lambda i, j: (i, j))] + c_out_specs,
        out_shape=[jax.ShapeDtypeStruct((m, k), _MXU), jax.ShapeDtypeStruct((m, n), out_dtype)] + c_out_shapes,
        scratch_shapes=c_sems,
        compiler_params=_cp(("arbitrary", "arbitrary") if comm is not None else ("parallel", "arbitrary")),
    )(h, gamma, w, *c_args)


def _mm_nn(a, w, *, tm, tn, name, residual=None):
    m, k = a.shape
    n = w.shape[1]

    def body(*refs):
        if residual is None:
            a_ref, w_ref, o_ref = refs
            o_ref[...] = _dot(a_ref[...], w_ref[...])
        else:
            a_ref, w_ref, r_ref, o_ref = refs
            o_ref[...] = _dot(a_ref[...], w_ref[...]) + r_ref[...]

    in_specs = [pl.BlockSpec((tm, k), lambda i, j: (i, 0)), pl.BlockSpec((k, tn), lambda i, j: (0, j))]
    args = [a, w]
    if residual is not None:
        in_specs.append(pl.BlockSpec((tm, tn), lambda i, j: (i, j)))
        args.append(residual)
    return pl.pallas_call(
        body, name=name, grid=(m // tm, n // tn), in_specs=in_specs,
        out_specs=pl.BlockSpec((tm, tn), lambda i, j: (i, j)),
        out_shape=jax.ShapeDtypeStruct((m, n), F32),
        compiler_params=_cp(("parallel", "parallel")),
    )(*args)


def _wgrad(a, b, *, tt, ta, tn, name, out_shape, out_block, out_index, a_tab=None, o_tab=None, into=None):
    t = a.shape[0]
    a_tab = list(range(a.shape[1] // ta)) if a_tab is None else a_tab
    o_tab = a_tab if o_tab is None else o_tab
    nb = b.shape[1] // tn

    def body(at_ref, ot_ref, a_ref, b_ref, *rest):
        del at_ref, ot_ref
        o_ref = rest[-1]

        @pl.when(pl.program_id(2) == 0)
        def _():
            o_ref[...] = jnp.zeros_like(o_ref)
        o_ref[...] += _dot_tn(a_ref[...], b_ref[...])

    in_specs = [pl.BlockSpec((tt, ta), lambda r, j, i, at, ot: (i, at[r])),
                pl.BlockSpec((tt, tn), lambda r, j, i, at, ot: (i, j))]
    args = [jnp.asarray(a_tab, jnp.int32), jnp.asarray(o_tab, jnp.int32), a, b]
    aliases = {}
    if into is not None:
        in_specs.append(pl.BlockSpec(memory_space=pl.ANY))
        args.append(into)
        aliases = {4: 0}
    return pl.pallas_call(
        body, name=name,
        grid_spec=pltpu.PrefetchScalarGridSpec(
            num_scalar_prefetch=2, grid=(len(a_tab), nb, t // tt), in_specs=in_specs,
            out_specs=pl.BlockSpec(out_block, lambda r, j, i, at, ot: out_index(ot[r], j))),
        out_shape=jax.ShapeDtypeStruct(out_shape, F32), input_output_aliases=aliases,
        compiler_params=_cp(("parallel", "parallel", "arbitrary")),
    )(*args)


def _mm_nt(a, w, *, tm, name):
    m, kc = a.shape
    n = w.shape[0]

    def body(a_ref, w_ref, o_ref):
        o_ref[...] = _dot_nt(a_ref[...], w_ref[...])

    return pl.pallas_call(
        body, name=name, grid=(m // tm,),
        in_specs=[pl.BlockSpec((tm, kc), lambda i: (i, 0)), pl.BlockSpec((n, kc), lambda i: (0, 0))],
        out_specs=pl.BlockSpec((tm, n), lambda i: (i, 0)),
        out_shape=jax.ShapeDtypeStruct((m, n), F32),
        compiler_params=_cp(("parallel",)),
    )(a, w)


def _mm_nt_rmsbwd(dy, w, x, gamma, dres, *, tm, tk, name, extra=None, comm=None):
    m, kc = dy.shape
    nk = kc // tk
    ni = m // tm
    n_x = 5 if extra is None else 7
    c_args, c_in_specs, c_out_specs, c_out_shapes, c_sems = _comm_specs(comm)
    if w.ndim == 3:
        assert w.shape[0] == nk and w.shape[2] == tk
        d = w.shape[1]
        w_spec = pl.BlockSpec((None, d, tk), lambda i, k: (k, 0, 0))
    else:
        d = w.shape[0]
        w_spec = pl.BlockSpec((d, tk), lambda i, k: (0, k))

    def body(*refs):
        dy_ref, w_ref, x_ref, g_ref, r_ref = refs[:5]
        if extra is not None:
            dy2_ref, w2_ref = refs[5:7]
        n_out = n_x + len(c_args)
        dx_ref, dg_ref = refs[n_out:n_out + 2]
        acc_ref = refs[n_out + 2 + len(c_out_shapes)]
        i, kk = pl.program_id(0), pl.program_id(1)
        if comm is not None:
            _comm_steps(comm, refs, n_x, n_out + 2, (i == 0) & (kk == 0), (i == (3 * ni) // 4) & (kk == 0),
                        (i == ni - 1) & (kk == nk - 1))

        @pl.when(kk == 0)
        def _():
            acc_ref[...] = jnp.zeros_like(acc_ref)

        @pl.when((i == 0) & (kk == 0))
        def _():
            dg_ref[...] = jnp.zeros_like(dg_ref)

        acc_ref[...] += _dot_nt(dy_ref[...], w_ref[...])

        @pl.when(kk == nk - 1)
        def _():
            dxn = acc_ref[...]
            if extra is not None:
                dxn = dxn + _dot_nt(dy2_ref[...], w2_ref[...])
            xv = x_ref[...]
            r = lax.rsqrt(jnp.mean(xv * xv, axis=-1, keepdims=True) + EPS)
            xh = xv * r
            dg_ref[0:1, :] += _rsum(dxn * xh)
            dxh = dxn * g_ref[...]
            dx_ref[...] = r_ref[...] + r * (dxh - xh * jnp.mean(dxh * xh, axis=-1, keepdims=True))

    in_specs = [pl.BlockSpec((tm, tk), lambda i, k: (i, k)), w_spec,
                pl.BlockSpec((tm, d), lambda i, k: (i, 0)), pl.BlockSpec((1, d), lambda i, k: (0, 0)),
                pl.BlockSpec((tm, d), lambda i, k: (i, 0))]
    args = [dy, w, x, gamma, dres]
    if extra is not None:
        k2 = extra[0].shape[1]
        in_specs += [pl.BlockSpec((tm, k2), lambda i, k: (i, 0)), pl.BlockSpec((d, k2), lambda i, k: (0, 0))]
        args += list(extra)
    return pl.pallas_call(
        body, name=name, grid=(ni, nk), in_specs=in_specs + c_in_specs,
        out_specs=[pl.BlockSpec((tm, d), lambda i, k: (i, 0)), pl.BlockSpec((8, d), lambda i, k: (0, 0))] + c_out_specs,
        out_shape=[jax.ShapeDtypeStruct((m, d), F32), jax.ShapeDtypeStruct((8, d), F32)] + c_out_shapes,
        scratch_shapes=[pltpu.VMEM((tm, d), F32)] + c_sems,
        compiler_params=_cp(("arbitrary", "arbitrary")),
    )(*args, *c_args)


def _rsum8(x):
    acc = x[0:8]
    for g in range(1, x.shape[0] // 8):
        acc = acc + x[8 * g:8 * (g + 1)]
    return acc


def _lru_gates(x, prev8, cw_ref, cb_ref, wa_ref, wx_ref, ba_ref, bx_ref, lam_ref):
    u = _conv4(x, prev8, cw_ref, cb_ref)
    ra =_sigmoid(_dot(u, wa_ref[0]) + ba_ref[...])
    ia = _sigmoid(_dot(u, wx_ref[0]) + bx_ref[...])
    sp = _softplus(-lam_ref[...])
    log_a = -LRU_C * ra * sp
    a = jnp.exp(log_a)
    m2 = _one_minus_exp(2.0 * log_a)
    mult = jnp.sqrt(m2)
    return u, ra, ia, sp, a, m2, mult


def _lru_fwd(proj, lw, *, r, name, comm=None):
    t = proj.shape[0]
    nt = t // r
    c_args, c_in_specs, c_out_specs, c_out_shapes, c_sems = _comm_specs(comm)

    def body(*refs):
        xg_ref, xp_ref, cw_ref, cb_ref, wa_ref, wx_ref, ba_ref, bx_ref, lam_ref = refs[:9]
        hl_ref, ya_ref, sv_ref = refs[9 + len(c_args):12 + len(c_args)]
        carry_ref = refs[12 + len(c_args) + len(c_out_shapes)]
        i = pl.program_id(1)
        if comm is not None:
            j = pl.program_id(0)
            _comm_steps(comm, refs, 9, 12 + len(c_args), (j == 0) & (i == 0), (j == 3) & (i == 0), (j == 3) & (i == nt - 1))

        @pl.when(i == 0)
        def _():
            carry_ref[...] = jnp.zeros_like(carry_ref)

        x = xg_ref[:, 0:256]
        lg = xg_ref[:, 256:512]
        prev8 = jnp.where(i == 0, 0.0, xp_ref[:, 0:256])
        u, ra, ia, sp, a, m2, mult = _lru_gates(x, prev8, cw_ref, cb_ref, wa_ref, wx_ref, ba_ref, bx_ref, lam_ref)
        for k, v in enumerate((u, ra, ia, a, mult)):
            sv_ref[k] = v
        ac, hc = _lin_scan(a, mult * ia * u, False)
        h = hc + ac * carry_ref[0:1, :]
        hl_ref[...] = h
        carry_ref[0:1, :] = hl_ref[r - 1:r, :]
        g, _ = _gelu_and_grad(lg)
        ya_ref[...] = (g * h).astype(ya_ref.dtype)

    small = lambda rows: pl.BlockSpec((rows, 256), lambda j, i: (0, j))
    return pl.pallas_call(
        body, name=name, grid=(4, nt),
        in_specs=[pl.BlockSpec((r, LBLK), lambda j, i: (i, _lblk_col(j))),
                  pl.BlockSpec((8, LBLK), lambda j, i: (jnp.maximum(i * (r // 8) - 1, 0), _lblk_col(j))),
                  small(4), small(1),
                  pl.BlockSpec((1, 256, 256), lambda j, i: (j, 0, 0)), pl.BlockSpec((1, 256, 256), lambda j, i: (j, 0, 0)),
                  small(1), small(1), small(1)] + c_in_specs,
        out_specs=[pl.BlockSpec((r, 256), lambda j, i: (i, j)), pl.BlockSpec((r, 256), lambda j, i: (i, j)),
                   pl.BlockSpec((5, r, 256), lambda j, i: (0, i, j))] + c_out_specs,
        out_shape=[jax.ShapeDtypeStruct((t, D), F32), jax.ShapeDtypeStruct((t, D), _MXU),
                   jax.ShapeDtypeStruct((5, t, D), F32)] + c_out_shapes,
        scratch_shapes=[pltpu.VMEM((8, 256), F32)] + c_sems,
        compiler_params=_cp(("arbitrary", "arbitrary") if comm is not None else ("parallel", "arbitrary")),
    )(proj, proj, lw["cw"], lw["cb"], lw["wa"], lw["wx"], lw["ba"], lw["bx"], lw["lam"], *c_args)


def _lru_bwd(proj, hl, gates, dya, dproj, lw, *, r, name, comm=None):
    t = proj.shape[0]
    nt = t // r
    c_args, c_in_specs, c_out_specs, c_out_shapes, c_sems = _comm_specs(comm)
    n_in = 10

    def body(*refs):
        xg_ref, hl_ref, hp_ref, sv_ref, dya_ref, cw_ref, wa_ref, wx_ref, lam_ref = refs[:9]
        n_out = n_in + len(c_args)
        dproj_ref, sm_ref, dwa_ref, dwx_ref = refs[n_out:n_out + 4]
        n_scr = n_out + 4 + len(c_out_shapes)
        carry_ref, du8_ref, row_scr = refs[n_scr:n_scr + 3]
        i = pl.program_id(1)
        if comm is not None:
            j = pl.program_id(0)
            _comm_steps(comm, refs, n_in, n_out + 4, (j == 0) & (i == 0), (j == 3) & (i == 0), (j == 3) & (i == nt - 1))

        @pl.when(i == 0)
        def _():
            carry_ref[...] = jnp.zeros_like(carry_ref)
            du8_ref[...] = jnp.zeros_like(du8_ref)
            sm_ref[...] = jnp.zeros_like(sm_ref)
            dwa_ref[...] = jnp.zeros_like(dwa_ref)
            dwx_ref[...] = jnp.zeros_like(dwx_ref)

        tile0 = i == nt - 1
        xp = xg_ref[:, 0:256]
        lg = xg_ref[:, 256:512]
        u, ra, ia, a, mult = (sv_ref[k] for k in range(5))
        sp = _softplus(-lam_ref[...])
        h = hl_ref[...]
        hprev = _shift_down(h, jnp.where(tile0, 0.0, hp_ref[...]), 1)
        dya_v = dya_ref[...]
        g, dg = _gelu_and_grad(lg)
        ac, lc = _lin_scan(_shift_up(a, carry_ref[...], 1), dya_v * g, True)
        lam_v = lc + ac * carry_ref[1:2, :]
        row_scr[0:8, :] = lam_v[0:8]
        row_scr[8:16, :] = a[0:8]
        carry_ref[1:2, :] = row_scr[0:1, :]
        carry_ref[0:1, :] = row_scr[8:9, :]
        da = lam_v * hprev
        dmult = lam_v * ia * u
        dia = lam_v * mult * u
        dlog = da * a - dmult * (a * a) / mult
        dra = -LRU_C * sp * dlog
        dpa = dra * ra * (1.0 - ra)
        dpx = dia * ia * (1.0 - ia)
        du = lam_v * mult * ia + _dot_nt(dpa, wa_ref[0]) + _dot_nt(dpx, wx_ref[0])
        dwa_ref[0] += _dot_tn(u, dpa)
        dwx_ref[0] += _dot_tn(u, dpx)
        dlx = du * cw_ref[3:4, :]
        sm_ref[24:32, :] += _rsum8(du * xp)
        for k in (1, 2, 3):
            du_k = _shift_up(du, du8_ref[...], k)
            dlx = dlx + du_k * cw_ref[3 - k:4 - k, :]
            sm_ref[8 * (3 - k):8 * (4 - k), :] += _rsum8(du_k * xp)
        du8_ref[...] = du[0:8]
        dproj_ref[:, 0:256] = dlx.astype(dproj_ref.dtype)
        dproj_ref[:, 256:512] = (dya_v * h * dg).astype(dproj_ref.dtype)
        sm_ref[32:40, :] += _rsum8(du)
        sm_ref[40:48, :] += _rsum8(dpa)
        sm_ref[48:56, :] += _rsum8(dpx)
        sm_ref[56:64, :] += _rsum8(-LRU_C * ra * dlog) * (-_sigmoid(-lam_ref[...]))

    rev = lambda i: nt - 1 - i
    small = lambda rows: pl.BlockSpec((rows, 256), lambda j, i: (0, j))
    wblk = pl.BlockSpec((1, 256, 256), lambda j, i: (j, 0, 0))
    return pl.pallas_call(
        body, name=name, grid=(4, nt),
        in_specs=[pl.BlockSpec((r, LBLK), lambda j, i: (rev(i), _lblk_col(j))),
                  pl.BlockSpec((r, 256), lambda j, i: (rev(i), j)),
                  pl.BlockSpec((8, 256), lambda j, i: (jnp.maximum(rev(i) * (r // 8) - 1, 0), j)),
                  pl.BlockSpec((5, r, 256), lambda j, i: (0, rev(i), j)),
                  pl.BlockSpec((r, 256), lambda j, i: (rev(i), j)),
                  small(4), wblk, wblk, small(1),
                  pl.BlockSpec(memory_space=pl.ANY)] + c_in_specs,
        out_specs=[pl.BlockSpec((r, LBLK), lambda j, i: (rev(i), _lblk_col(j))),
                   pl.BlockSpec((64, 256), lambda j, i: (0, j)), wblk, wblk] + c_out_specs,
        out_shape=[jax.ShapeDtypeStruct(dproj.shape, dproj.dtype), jax.ShapeDtypeStruct((64, D), F32),
                   jax.ShapeDtypeStruct((4, 256, 256), F32), jax.ShapeDtypeStruct((4, 256, 256), F32)] + c_out_shapes,
        scratch_shapes=[pltpu.VMEM((8, 256), F32), pltpu.VMEM((8, 256), F32), pltpu.VMEM((16, 256), F32)] + c_sems,
        input_output_aliases={n_in - 1: 0},
        compiler_params=_cp(("arbitrary", "arbitrary") if comm is not None else ("parallel", "arbitrary")),
    )(proj, hl, hl, gates, dya, lw["cw"], lw["wa"], lw["wx"], lw["lam"], dproj, *c_args)


def _head_cols(x):
    lane = lax.broadcasted_iota(jnp.int32, x.shape, 1)
    return [jnp.sum(jnp.where(lane == h, x, 0.0), axis=1, keepdims=True) for h in range(N_HEADS)]


def _compact_heads(blocks):
    lane = lax.broadcasted_iota(jnp.int32, blocks[0].shape, 1)
    lo = lane < HEAD_P
    out = jnp.zeros_like(blocks[0])
    for j, blk in enumerate(blocks):
        s_lo = jnp.sum(jnp.where(lo, blk, 0.0), axis=1, keepdims=True)
        s_hi = jnp.sum(jnp.where(lo, 0.0, blk), axis=1, keepdims=True)
        out = jnp.where(lane == 2 * j, s_lo, out)
        out = jnp.where(lane == 2 * j + 1, s_hi, out)
    return out


def _ssd_prelude(dtraw_ref, dtb_ref, alog_ref, dt_scr, a_scr):
    lane = lax.broadcasted_iota(jnp.int32, dt_scr.shape, 1)
    dt = jnp.where(lane < N_HEADS, _softplus(dtraw_ref[...] + dtb_ref[0:1, :]), 0.0)
    dt_scr[...] = dt
    a_scr[...] = dt * (-jnp.exp(alog_ref[0:1, :]))


def _ssd_chunk_scalars(dt_scr, a_scr, r_scr, r0):
    a_c = a_scr[pl.ds(r0, CHUNK), :]
    dt_c = dt_scr[pl.ds(r0, CHUNK), :]
    i0 = lax.broadcasted_iota(jnp.int32, (CHUNK, CHUNK), 0)
    i1 = lax.broadcasted_iota(jnp.int32, (CHUNK, CHUNK), 1)
    tri = jnp.where(i0 >= i1, 1.0, 0.0).astype(F32)
    cs = jnp.dot(tri, a_c, precision=_HI, preferred_element_type=F32)
    lane = lax.broadcasted_iota(jnp.int32, (CHUNK, 128), 1)
    srow = lax.broadcasted_iota(jnp.int32, (CHUNK, 128), 0)
    t_lo = jnp.where((lane < HEAD_P) & (srow <= lane), 1.0, 0.0).astype(F32)
    t_hi = jnp.where((lane >= HEAD_P) & (srow <= lane - HEAD_P), 1.0, 0.0).astype(F32)
    even = (lane % 2) == 0
    tn = (((0,), (0,)), ((), ()))
    r_scr[...] = (lax.dot_general(jnp.where(even, a_c, 0.0), t_lo, tn, precision=_HI, preferred_element_type=F32)
                  + lax.dot_general(jnp.where(even, 0.0, a_c), t_hi, tn, precision=_HI, preferred_element_type=F32))
    return cs, dt_c, _head_cols(cs), _head_cols(dt_c)


def _block_diag2(v):
    lo = lax.broadcasted_iota(jnp.int32, v.shape, 1) < HEAD_P
    return jnp.concatenate([jnp.where(lo, v, 0.0), jnp.where(lo, 0.0, v)], axis=0).astype(_MXU)


def _ssd_pair(xc_scr, r_scr, cs_cols, dt_cols, s2, r0, j, s2t=None):
    lane = lax.broadcasted_iota(jnp.int32, (CHUNK, 128), 1)
    srow = lax.broadcasted_iota(jnp.int32, (CHUNK, 128), 0)
    lo = lane < HEAD_P
    csc = jnp.where(lo, cs_cols[2 * j], cs_cols[2 * j + 1])
    dtc = jnp.where(lo, dt_cols[2 * j], dt_cols[2 * j + 1])
    csr = r_scr[2 * j:2 * j + 1, :] + r_scr[2 * j + 1:2 * j + 2, :]
    dm = jnp.where((lane & (HEAD_P - 1)) <= srow, jnp.exp(jnp.minimum(csc - csr, 0.0)), 0.0)
    xs = xc_scr[pl.ds(r0, CHUNK), j * 128:(j + 1) * 128]
    xd = xs * dtc
    csl = jnp.sum(jnp.where(srow == CHUNK - 1, csc, 0.0), axis=0, keepdims=True)
    out = dict(csc=csc, dtc=dtc, dm=dm, m2=s2 * dm, xs=xs, xd=xd, rhs=_block_diag2(xd), e=jnp.exp(csc),
               w=jnp.exp(csl - csc), dec=jnp.exp(csl))
    if s2t is not None:
        out["mt2"] = s2t * jnp.where((lane & (HEAD_P - 1)) >= srow, jnp.exp(jnp.minimum(csr - csc, 0.0)), 0.0)
    return out


def _cat(parts):
    return jnp.concatenate(parts, axis=1)


def _ssd_fwd(proj, dtraw, sw, *, rb, name, comm=None):
    t = proj.shape[0]
    ns, cb = t // rb, rb // CHUNK
    c_args, c_in_specs, c_out_specs, c_out_shapes, c_sems = _comm_specs(comm)

    def body(*refs):
        zx_ref, zp_ref, dtraw_ref, cw_ref, cbias_ref, dtb_ref, alog_ref, dsk_ref, ng_ref = refs[:9]
        yssd_ref, yb_ref, st_ref, xc_scr, dsl_ref = refs[9 + len(c_args):14 + len(c_args)]
        n_scr = 14 + len(c_args) + len(c_out_shapes)
        h_scr, dt_scr, a_scr, r_scr = refs[n_scr:n_scr + 4]
        i = pl.program_id(0)
        if comm is not None:
            _comm_steps(comm, refs, 9, 14 + len(c_args), i == 0, i == (3 * ns) // 4, i == ns - 1)

        @pl.when(i == 0)
        def _():
            h_scr[...] = jnp.zeros_like(h_scr)

        for j in range(XBC // 128):
            cs_, zc = slice(128 * j, 128 * (j + 1)), slice(2048 + 128 * j, 2048 + 128 * (j + 1))
            pre = _conv4(zx_ref[:, zc], jnp.where(i == 0, 0.0, zp_ref[:, zc]), cw_ref, cbias_ref, cs_)
            sg = _sigmoid(pre)
            xc_scr[:, cs_] = pre * sg
            dsl_ref[:, cs_] = sg * (1.0 + pre * (1.0 - sg))
        _ssd_prelude(dtraw_ref, dtb_ref, alog_ref, dt_scr, a_scr)

        def chunk(c, carry):
            r0 = pl.multiple_of(c * CHUNK, CHUNK)
            _, _, cs_cols, dt_cols = _ssd_chunk_scalars(dt_scr, a_scr, r_scr, r0)
            st_ref[c] = h_scr[...]
            for g in range(N_GROUPS):
                bg = xc_scr[pl.ds(r0, CHUNK), 2048 + 128 * g:2048 + 128 * (g + 1)]
                cg = xc_scr[pl.ds(r0, CHUNK), 2560 + 128 * g:2560 + 128 * (g + 1)]
                s2 = _dot_nt(cg, jnp.concatenate([bg, bg], axis=0))
                hp = h_scr[:, 512 * g:512 * (g + 1)]
                yoff = _dot(cg, hp)
                xdw, dec = [], []
                for jj in range(4):
                    j = 4 * g + jj
                    p = _ssd_pair(xc_scr, r_scr, cs_cols, dt_cols, s2, r0, j)
                    y = _dot(p["m2"], p["rhs"]) + yoff[:, 128 * jj:128 * (jj + 1)] * p["e"]
                    yssd_ref[pl.ds(r0, CHUNK), 128 * j:128 * (j + 1)] = y + dsk_ref[0:1, 128 * j:128 * (j + 1)] * p["xs"]
                    xdw.append(p["xd"] * p["w"])
                    dec.append(p["dec"])
                h_scr[:, 512 * g:512 * (g + 1)] = hp * _cat(dec) + _dot_tn(bg, _cat(xdw))
            return carry

        lax.fori_loop(0, cb, chunk, 0)
        for g in range(N_GROUPS):
            sl = slice(512 * g, 512 * (g + 1))
            for q in range(rb // NORM_ROWS):
                rw = slice(NORM_ROWS * q, NORM_ROWS * (q + 1))
                yz = yssd_ref[rw, sl] * _silu(zx_ref[rw, sl])
                rg = lax.rsqrt(jnp.mean(yz * yz, axis=-1, keepdims=True) + EPS)
                yb_ref[rw, sl] = (yz * rg * ng_ref[0:1, sl]).astype(yb_ref.dtype)

    full = lambda rows, cols: pl.BlockSpec((rows, cols), lambda i: (0, 0))
    return pl.pallas_call(
        body, name=name, grid=(ns,),
        in_specs=[pl.BlockSpec((rb, ZX_W), lambda i: (i, 0)),
                  pl.BlockSpec((8, ZX_W), lambda i: (jnp.maximum(i * (rb // 8) - 1, 0), 0)),
                  pl.BlockSpec((rb, DT_PAD), lambda i: (i, 0)),
                  full(4, XBC), full(1, XBC), full(1, DT_PAD), full(1, DT_PAD), full(1, SSD_INNER), full(1, SSD_INNER)]
        + c_in_specs,
        out_specs=[pl.BlockSpec((rb, SSD_INNER), lambda i: (i, 0)), pl.BlockSpec((rb, SSD_INNER), lambda i: (i, 0)),
                   pl.BlockSpec((cb, N_STATE, SSD_INNER), lambda i: (i, 0, 0)),
                   pl.BlockSpec((rb, XBC), lambda i: (i, 0)), pl.BlockSpec((rb, XBC), lambda i: (i, 0))] + c_out_specs,
        out_shape=[jax.ShapeDtypeStruct((t, SSD_INNER), F32), jax.ShapeDtypeStruct((t, SSD_INNER), _MXU),
                   jax.ShapeDtypeStruct((t // CHUNK, N_STATE, SSD_INNER), F32),
                   jax.ShapeDtypeStruct((t, XBC), F32), jax.ShapeDtypeStruct((t, XBC), F32)] + c_out_shapes,
        scratch_shapes=[pltpu.VMEM((N_STATE, SSD_INNER), F32), pltpu.VMEM((rb, DT_PAD), F32),
                        pltpu.VMEM((rb, DT_PAD), F32), pltpu.VMEM((128, 128), F32)] + c_sems,
        compiler_params=_cp(("arbitrary",)),
    )(proj, proj, dtraw, sw["cw"], sw["cb"], sw["dtb"], sw["alog"], sw["dsk"], sw["ng"], *c_args)


def _ssd_bwd(proj, dtraw, yssd, states, xc, dsl, dyb, dproj, sw, *, rb, name, comm=None):
    t = proj.shape[0]
    ns, cb = t // rb, rb // CHUNK
    c_args, c_in_specs, c_out_specs, c_out_shapes, c_sems = _comm_specs(comm)
    n_in = 13

    def body(*refs):
        zx_ref, dtraw_ref, yssd_ref, st_ref, xc_scr, dsl_scr, dyb_ref, cw_ref, dtb_ref, alog_ref, dsk_ref, ng_ref = refs[:12]
        n_out = n_in + len(c_args)
        dzx_ref, ddt_ref, gconv_ref, gch_ref, ghd_ref = refs[n_out:n_out + 5]
        n_scr = n_out + 5 + len(c_out_shapes)
        dht_scr, dy_scr, dxc_scr, dt_scr, a_scr, r_scr, dp8_scr = refs[n_scr:n_scr + 7]
        i = pl.program_id(0)
        if comm is not None:
            _comm_steps(comm, refs, n_in, n_out + 5, i == 0, i == (3 * ns) // 4, i == ns - 1)

        @pl.when(i == 0)
        def _():
            dht_scr[...] = jnp.zeros_like(dht_scr)
            dp8_scr[...] = jnp.zeros_like(dp8_scr)
            gconv_ref[...] = jnp.zeros_like(gconv_ref)
            gch_ref[...] = jnp.zeros_like(gch_ref)
            ghd_ref[...] = jnp.zeros_like(ghd_ref)

        _ssd_prelude(dtraw_ref, dtb_ref, alog_ref, dt_scr, a_scr)

        for g in range(N_GROUPS):
            sl = slice(512 * g, 512 * (g + 1))
            for q in range(rb // NORM_ROWS):
                rw = slice(NORM_ROWS * q, NORM_ROWS * (q + 1))
                zv = zx_ref[rw, sl]
                ys = yssd_ref[rw, sl]
                sg = _sigmoid(zv)
                sz = zv * sg
                yz = ys * sz
                rg = lax.rsqrt(jnp.mean(yz * yz, axis=-1, keepdims=True) + EPS)
                yn = yz * rg
                dyb_v = dyb_ref[rw, sl]
                gch_ref[0:8, sl] += _rsum8(dyb_v * yn)
                dyn = dyb_v * ng_ref[0:1, sl]
                dyz = rg * (dyn - yn * jnp.mean(dyn * yn, axis=-1, keepdims=True))
                dy_scr[rw, sl] = dyz * sz
                dzx_ref[rw, sl] = (dyz * ys * (sg * (1.0 + zv * (1.0 - sg)))).astype(dzx_ref.dtype)

        a_row = -jnp.exp(alog_ref[0:1, :])

        def chunk(cc, carry):
            c = cb - 1 - cc
            r0 = pl.multiple_of(c * CHUNK, CHUNK)
            rows = pl.ds(r0, CHUNK)
            _, dt_c, cs_cols, dt_cols = _ssd_chunk_scalars(dt_scr, a_scr, r_scr, r0)
            lane = lax.broadcasted_iota(jnp.int32, (CHUNK, 128), 1)
            srow = lax.broadcasted_iota(jnp.int32, (CHUNK, 128), 0)
            lo = lane < HEAD_P
            last = srow == CHUNK - 1
            p1_blocks, p3_blocks = [], []
            for g in range(N_GROUPS):
                gs = slice(512 * g, 512 * (g + 1))
                bg = xc_scr[rows, 2048 + 128 * g:2048 + 128 * (g + 1)]
                cg = xc_scr[rows, 2560 + 128 * g:2560 + 128 * (g + 1)]
                b2 = jnp.concatenate([bg, bg], axis=0)
                s2 = _dot_nt(cg, b2)
                s2t = _dot_nt(bg, jnp.concatenate([cg, cg], axis=0))
                hp = st_ref[c, :, gs]
                dht = dht_scr[:, gs]
                yoff = _dot(cg, hp)
                ps = [_ssd_pair(xc_scr, r_scr, cs_cols, dt_cols, s2, r0, 4 * g + jj, s2t) for jj in range(4)]
                dys = [dy_scr[rows, 128 * (4 * g + jj):128 * (4 * g + jj + 1)] for jj in range(4)]
                dye = _cat([dys[jj] * ps[jj]["e"] for jj in range(4)])
                w_g = _cat([p["w"] for p in ps])
                dcg = _dot_nt(dye, hp)
                dht_scr[:, gs] = _dot_tn(cg, dye) + _cat([p["dec"] for p in ps]) * dht
                dxd_state = w_g * _dot(bg, dht)
                dbg = _dot_nt(_cat([p["xd"] for p in ps]) * w_g, dht)
                tsum = _rsum(dht * hp)
                ds2 = jnp.zeros((CHUNK, 128), F32)
                for jj in range(4):
                    j = 4 * g + jj
                    ls = slice(128 * j, 128 * (j + 1))
                    p, dy2 = ps[jj], dys[jj]
                    dy_bd = _block_diag2(dy2)
                    dm2 = _dot_nt(dy2, p["rhs"])
                    ds2 = ds2 + dm2 * p["dm"]
                    gdiff = dm2 * p["m2"] - _dot_nt(p["xd"], dy_bd) * p["mt2"]
                    dxs = dxd_state[:, 128 * jj:128 * (jj + 1)]
                    dxd = _dot(p["mt2"], dy_bd) + dxs
                    end_row = _rsum(p["xd"] * dxs) + p["dec"] * tsum[:, 128 * jj:128 * (jj + 1)]
                    p1_blocks.append(gdiff + dy2 * yoff[:, 128 * jj:128 * (jj + 1)] * p["e"] - p["xd"] * dxs
                                     + jnp.where(last, end_row, 0.0))
                    p3_blocks.append(dxd * p["xs"])
                    dxc_scr[rows, ls] = dxd * p["dtc"] + dy2 * dsk_ref[0:1, ls]
                    gch_ref[8:16, ls] += _rsum8(dy2 * p["xs"])
                dcg = dcg + _dot(ds2, b2)
                rb2 = _dot_tn(ds2, cg)
                dxc_scr[rows, 2048 + 128 * g:2048 + 128 * (g + 1)] = dbg + rb2[0:CHUNK] + rb2[CHUNK:2 * CHUNK]
                dxc_scr[rows, 2560 + 128 * g:2560 + 128 * (g + 1)] = dcg
            dcs = _compact_heads(p1_blocks)
            i0 = lax.broadcasted_iota(jnp.int32, (CHUNK, CHUNK), 0)
            i1 = lax.broadcasted_iota(jnp.int32, (CHUNK, CHUNK), 1)
            triu = jnp.where(i1 >= i0, 1.0, 0.0).astype(F32)
            da = jnp.dot(triu, dcs, precision=_HI, preferred_element_type=F32)
            ddt = _compact_heads(p3_blocks) + da * a_row
            ddtraw = jnp.where(lane < N_HEADS, ddt * _sigmoid(dtraw_ref[rows, :] + dtb_ref[0:1, :]), 0.0)
            ddt_ref[rows, :] = ddtraw.astype(ddt_ref.dtype)
            ghd_ref[0:1, :] += _rsum(ddtraw)
            ghd_ref[1:2, :] += _rsum(da * dt_c) * a_row
            return carry

        lax.fori_loop(0, cb, chunk, 0)
        for j in range(XBC // 128):
            cs_, zc = slice(128 * j, 128 * (j + 1)), slice(2048 + 128 * j, 2048 + 128 * (j + 1))
            dpre = dxc_scr[:, cs_] * dsl_scr[:, cs_]
            xraw = zx_ref[:, zc]
            dx = dpre * cw_ref[3:4, cs_]
            gconv_ref[24:32, cs_] += _rsum8(dpre * xraw)
            for k in (1, 2, 3):
                dpre_k = _shift_up(dpre, dp8_scr[:, cs_], k)
                dx = dx + dpre_k * cw_ref[3 - k:4 - k, cs_]
                gconv_ref[8 * (3 - k):8 * (4 - k), cs_] += _rsum8(dpre_k * xraw)
            dzx_ref[:, zc] = dx.astype(dzx_ref.dtype)
            dp8_scr[:, cs_] = dpre[0:8]
            gconv_ref[32:40, cs_] += _rsum8(dpre)

    rev = lambda i: ns - 1 - i
    full = lambda rows, cols: pl.BlockSpec((rows, cols), lambda i: (0, 0))
    return pl.pallas_call(
        body, name=name, grid=(ns,),
        in_specs=[pl.BlockSpec((rb, ZX_W), lambda i: (rev(i), 0)),
                  pl.BlockSpec((rb, DT_PAD), lambda i: (rev(i), 0)),
                  pl.BlockSpec((rb, SSD_INNER), lambda i: (rev(i), 0)),
                  pl.BlockSpec((cb, N_STATE, SSD_INNER), lambda i: (rev(i), 0, 0)),
                  pl.BlockSpec((rb, XBC), lambda i: (rev(i), 0)), pl.BlockSpec((rb, XBC), lambda i: (rev(i), 0)),
                  pl.BlockSpec((rb, SSD_INNER), lambda i: (rev(i), 0)),
                  full(4, XBC), full(1, DT_PAD), full(1, DT_PAD), full(1, SSD_INNER), full(1, SSD_INNER),
                  pl.BlockSpec(memory_space=pl.ANY)] + c_in_specs,
        out_specs=[pl.BlockSpec((rb, ZX_W), lambda i: (rev(i), 0)), pl.BlockSpec((rb, DT_PAD), lambda i: (rev(i), 0)),
                   full(40, XBC), full(16, SSD_INNER), full(8, DT_PAD)] + c_out_specs,
        out_shape=[jax.ShapeDtypeStruct(dproj.shape, dproj.dtype), jax.ShapeDtypeStruct((t, DT_PAD), _MXU),
                   jax.ShapeDtypeStruct((40, XBC), F32), jax.ShapeDtypeStruct((16, SSD_INNER), F32),
                   jax.ShapeDtypeStruct((8, DT_PAD), F32)] + c_out_shapes,
        scratch_shapes=[pltpu.VMEM((N_STATE, SSD_INNER), F32),
                        pltpu.VMEM((rb, SSD_INNER), F32), pltpu.VMEM((rb, XBC), F32), pltpu.VMEM((rb, DT_PAD), F32),
                        pltpu.VMEM((rb, DT_PAD), F32), pltpu.VMEM((128, 128), F32), pltpu.VMEM((8, XBC), F32)] + c_sems,
        input_output_aliases={n_in - 1: 0},
        compiler_params=_cp(("arbitrary",)),
    )(proj, dtraw, yssd, states, xc, dsl, dyb, sw["cw"], sw["dtb"], sw["alog"], sw["dsk"], sw["ng"], dproj, *c_args)


def _branch_merge(ya, yb, proj, wba, wbb, bgate, *, tm, tn, name):
    t = ya.shape[0]
    nj = D // tn

    def body(ya_ref, yb_ref, ga_ref, gb_ref, wba_ref, wbb_ref, ba_ref, bb_ref, ta_ref, tb_ref, mg_ref):
        ta = _dot(ya_ref[...], wba_ref[...])
        tb = _dot(yb_ref[...], wbb_ref[...])
        ta_ref[...] = ta.astype(ta_ref.dtype)
        tb_ref[...] = tb.astype(tb_ref.dtype)
        ga = _sigmoid(ga_ref[...] + ba_ref[...])
        gb = _sigmoid(gb_ref[...] + bb_ref[...])
        mg_ref[...] = (ga * ta + gb * tb).astype(mg_ref.dtype)

    tile = pl.BlockSpec((tm, tn), lambda i, j: (i, j))
    return pl.pallas_call(
        body, name=name, grid=(t // tm, nj),
        in_specs=[pl.BlockSpec((tm, D), lambda i, j: (i, 0)), pl.BlockSpec((tm, SSD_INNER), lambda i, j: (i, 0)),
                  pl.BlockSpec((tm, tn), lambda i, j: (i, G0 // tn + j)),
                  pl.BlockSpec((tm, tn), lambda i, j: (i, (G0 + D) // tn + j)),
                  pl.BlockSpec((D, tn), lambda i, j: (0, j)), pl.BlockSpec((SSD_INNER, tn), lambda i, j: (0, j)),
                  pl.BlockSpec((1, tn), lambda i, j: (0, j)), pl.BlockSpec((1, tn), lambda i, j: (0, nj + j))],
        out_specs=[tile, tile, tile],
        out_shape=[jax.ShapeDtypeStruct((t, D), _MXU)] * 3,
        compiler_params=_cp(("parallel", "parallel")),
    )(ya, yb, proj, proj, wba, wbb, bgate, bgate)


def _swiglu_mm(gu, wfo, residual, *, tm, tn, name):
    t = gu.shape[0]

    def body(gu_ref, w_ref, r_ref, act_ref, o_ref):
        @pl.when(pl.program_id(1) == 0)
        def _():
            gate = gu_ref[:, 0:D_FF].astype(F32)
            act_ref[...] = (_silu(gate) * gu_ref[:, D_FF:2 * D_FF].astype(F32)).astype(act_ref.dtype)
        o_ref[...] = jnp.dot(act_ref[...], w_ref[...], preferred_element_type=F32) + r_ref[...]

    return pl.pallas_call(
        body, name=name, grid=(t // tm, D // tn),
        in_specs=[pl.BlockSpec((tm, 2 * D_FF), lambda i, j: (i, 0)), pl.BlockSpec((D_FF, tn), lambda i, j: (0, j)),
                  pl.BlockSpec((tm, tn), lambda i, j: (i, j))],
        out_specs=[pl.BlockSpec((tm, D_FF), lambda i, j: (i, 0)), pl.BlockSpec((tm, tn), lambda i, j: (i, j))],
        out_shape=[jax.ShapeDtypeStruct((t, D_FF), _MXU), jax.ShapeDtypeStruct((t, D), F32)],
        compiler_params=_cp(("parallel", "arbitrary")),
    )(gu, wfo, residual)


def _ffn_bwd_act(dh, wfo, gu, *, tm, name):
    t = dh.shape[0]

    def body(dh_ref, w_ref, gu_ref, o_ref):
        dact = _dot_nt(dh_ref[...], w_ref[...])
        g = gu_ref[:, 0:D_FF].astype(F32)
        u = gu_ref[:, D_FF:2 * D_FF].astype(F32)
        sg = _sigmoid(g)
        o_ref[:, 0:D_FF] = (dact * u * (sg * (1.0 + g * (1.0 - sg)))).astype(o_ref.dtype)
        o_ref[:, D_FF:2 * D_FF] = (dact * (g * sg)).astype(o_ref.dtype)

    return pl.pallas_call(
        body, name=name, grid=(t // tm,),
        in_specs=[pl.BlockSpec((tm, D), lambda i: (i, 0)), pl.BlockSpec((D_FF, D), lambda i: (0, 0)),
                  pl.BlockSpec((tm, 2 * D_FF), lambda i: (i, 0))],
        out_specs=pl.BlockSpec((tm, 2 * D_FF), lambda i: (i, 0)),
        out_shape=jax.ShapeDtypeStruct((t, 2 * D_FF), _MXU),
        compiler_params=_cp(("parallel",)),
    )(dh, wfo, gu)


def _outproj_bwd(dh, wout, ta, tb, proj, bgate, dproj, *, tm, name):
    t = dh.shape[0]

    def body(dh_ref, w_ref, ta_ref, tb_ref, g_ref, b_ref, dta_ref, dtb_ref, dg_ref, db_ref):
        @pl.when(pl.program_id(0) == 0)
        def _():
            db_ref[...] = jnp.zeros_like(db_ref)
        dm = _dot_nt(dh_ref[...], w_ref[...])
        ga = _sigmoid(g_ref[:, 0:D] + b_ref[:, 0:D])
        gb = _sigmoid(g_ref[:, D:2 * D] + b_ref[:, D:2 * D])
        dta_ref[...] = (dm * ga).astype(dta_ref.dtype)
        dtb_ref[...] = (dm * gb).astype(dtb_ref.dtype)
        dga = dm * ta_ref[...].astype(F32) * ga * (1.0 - ga)
        dgb = dm * tb_ref[...].astype(F32) * gb * (1.0 - gb)
        dg_ref[:, 0:D] = dga.astype(dg_ref.dtype)
        dg_ref[:, D:2 * D] = dgb.astype(dg_ref.dtype)
        db_ref[0:1, 0:D] += _rsum(dga)
        db_ref[0:1, D:2 * D] += _rsum(dgb)

    row = lambda cols: pl.BlockSpec((tm, cols), lambda i: (i, 0))
    return pl.pallas_call(
        body, name=name, grid=(t // tm,),
        in_specs=[row(D), pl.BlockSpec((D, D), lambda i: (0, 0)), row(D), row(D),
                  pl.BlockSpec((tm, 2 * D), lambda i: (i, G0 // (2 * D))), pl.BlockSpec((1, 2 * D), lambda i: (0, 0))],
        out_specs=[row(D), row(D), pl.BlockSpec((tm, 2 * D), lambda i: (i, G0 // (2 * D))),
                   pl.BlockSpec((8, 2 * D), lambda i: (0, 0))],
        out_shape=[jax.ShapeDtypeStruct((t, D), _MXU), jax.ShapeDtypeStruct((t, D), _MXU),
                   jax.ShapeDtypeStruct(dproj, _MXU), jax.ShapeDtypeStruct((8, 2 * D), F32)],
        compiler_params=_cp(("arbitrary",)),
    )(dh, wout, ta, tb, proj, bgate)


def _loss_head(h, gf, target, *, tm, name):
    t = h.shape[0]

    def body(h_ref, g_ref, t_ref, loss_ref, dg_ref, dh_ref):
        @pl.when(pl.program_id(0) == 0)
        def _():
            loss_ref[...] = jnp.zeros_like(loss_ref)
            dg_ref[...] = jnp.zeros_like(dg_ref)
        x = h_ref[...]
        r = lax.rsqrt(jnp.mean(x * x, axis=-1, keepdims=True) + EPS)
        xh = x * r
        err = xh * g_ref[...] - t_ref[...]
        loss_ref[...] += 0.5 * jnp.sum(jnp.mean(err * err, axis=-1, keepdims=True), axis=0, keepdims=True)
        dy = err * (1.0 / D)
        dg_ref[0:1, :] += _rsum(dy * xh)
        dxh = dy * g_ref[...]
        dh_ref[...] = r * (dxh - xh * jnp.mean(dxh * xh, axis=-1, keepdims=True))

    row = pl.BlockSpec((tm, D), lambda i: (i, 0))
    return pl.pallas_call(
        body, name=name, grid=(t // tm,),
        in_specs=[row, pl.BlockSpec((1, D), lambda i: (0, 0)), row],
        out_specs=[pl.BlockSpec((8, 128), lambda i: (0, 0)), pl.BlockSpec((8, D), lambda i: (0, 0)), row],
        out_shape=[jax.ShapeDtypeStruct((8, 128), F32), jax.ShapeDtypeStruct((8, D), F32), jax.ShapeDtypeStruct((t, D), F32)],
        compiler_params=_cp(("arbitrary",)),
    )(h, gf, target)


def _row_tile(rows, cols, limit_bytes=1 << 20):
    best = None
    for tr in range(8, rows + 1, 8):
        if rows % tr == 0 and tr * cols * 4 <= limit_bytes:
            best = tr
    return best if best is not None else rows


def _adamw(w, g, m, v, *, name):
    rows, cols = w.shape
    tr = _row_tile(rows, cols)

    def body(w_ref, g_ref, m_ref, v_ref, d_ref, nm_ref, nv_ref):
        gv = g_ref[...]
        nm = ADAM_B1 * m_ref[...] + (1.0 - ADAM_B1) * gv
        nv = ADAM_B2 * v_ref[...] + (1.0 - ADAM_B2) * (gv * gv)
        m_hat = nm / (1.0 - ADAM_B1 ** ADAM_STEP)
        v_hat = nv / (1.0 - ADAM_B2 ** ADAM_STEP)
        d_ref[...] = -ADAM_LR * (m_hat / (jnp.sqrt(v_hat) + ADAM_EPS) + ADAM_WD * w_ref[...])
        nm_ref[...] = nm
        nv_ref[...] = nv

    blk = pl.BlockSpec((tr, cols), lambda i: (i, 0))
    shp = jax.ShapeDtypeStruct((rows, cols), F32)
    return pl.pallas_call(
        body, name=name, grid=(rows // tr,), in_specs=[blk] * 4, out_specs=[blk] * 3, out_shape=[shp] * 3,
        compiler_params=_cp(("parallel",)),
    )(w, g, m, v)


def _bd256(w):
    w4 = w.reshape(4, 4, 64, 64)
    eye = jnp.eye(4, dtype=w.dtype)
    return (w4[:, :, :, None, :] * eye[None, :, None, :, None]).reshape(4, 256, 256)


def _bd256_diag(g):
    g5 = g.reshape(4, 4, 64, 4, 64)
    return jnp.stack([g5[:, a, :, a, :] for a in range(4)], axis=1).reshape(16, 64, 64)


FFN_SHARD = 2 * D_FF // 4
W_IN_SHARD = IN_DIM // 4
W_IN_ROWS = 9344


def _w_in_cols(shards, c0, c1):
    out = []
    for p in range(4):
        lo, hi = max(c0, W_IN_SHARD * p), min(c1, W_IN_SHARD * (p + 1))
        if lo < hi:
            out.append(shards[p][:, lo - W_IN_SHARD * p:hi - W_IN_SHARD * p])
    return out


def _in_proj_weights(win):
    lblk = [_w_in_cols(win, 256 * j, 256 * (j + 1)) + _w_in_cols(win, D + 256 * j, D + 256 * (j + 1)) for j in range(4)]
    wp = jnp.concatenate(_w_in_cols(win, 2048, 4096) + _w_in_cols(win, 4096, 7168) + lblk[0] + lblk[1]
                         + _w_in_cols(win, 7200, 9248) + lblk[2] + lblk[3], axis=1)
    wdt = jnp.pad(jnp.concatenate(_w_in_cols(win, 7168, 7200), axis=1), ((0, 0), (0, DT_PAD - N_HEADS)))
    return wp, wdt


def _layer_weights(w, conv, small, l, wp, wdt):
    row = lambda v: v.reshape(1, -1)
    pad_h = lambda v: jnp.pad(v.reshape(1, -1), ((0, 0), (0, DT_PAD - N_HEADS)))
    lw = dict(cw=conv["lru_conv_w"][l], cb=row(small["lru_conv_b"][l]),
              wa=_bd256(small["lru_w_a"][l]).astype(_MXU), wx=_bd256(small["lru_w_x"][l]).astype(_MXU),
              ba=row(small["lru_b_a"][l]), bx=row(small["lru_b_x"][l]), lam=row(small["lru_lambda"][l]))
    sw = dict(cw=conv["ssd_conv_w"][l], cb=row(small["ssd_conv_b"][l]), dtb=pad_h(small["ssd_dt_bias"][l]),
              alog=pad_h(small["ssd_A_log"][l]), dsk=row(jnp.repeat(small["ssd_D"][l], HEAD_P)),
              ng=row(small["ssd_norm_g"][l]))
    return dict(wp=wp, wdt=wdt, lw=lw, sw=sw, wba=w["w_branch"][0:D], wbb=w["w_branch"][D:3 * D],
                wout=w["w_out"], wfi=w["w_ffn_in"], wfo=w["w_ffn_out"],
                g1=row(small["norm1_g"][l]), g2=row(small["norm2_g"][l]), bgate=row(small["b_gate"][l]))


def _tiles(t):
    return dict(tmi=min(2048, t), tmn=min(1024, t), tm=min(512, t), r=min(256, t), rb=min(128, t))


def _layer_fwd(h, w, conv, small, l, carried=None):
    tl = _tiles(h.shape[0])
    n = f"l{l}_"
    carried = carried or {}
    arrived = []

    def carry(kernel, key, n_main, *args, **kw):
        comm, finish = carried.get(key, (None, None))
        outs = list(kernel(*args, comm=comm, **kw))
        if comm is not None:
            arrived.append(finish(outs[n_main:]))
        return outs[:n_main]

    wp, wdt = _in_proj_weights(w["w_in"])
    xn, proj = carry(_norm_mm, "in_proj", 2, h, small["norm1_g"][l].reshape(1, -1), wp, tm=tl["tmi"], tn=1024,
                     name=n + "in_proj")
    w = dict(w)
    for layer, ws in arrived:
        if layer == l:
            w.update(ws)
    lwt = _layer_weights(w, conv, small, l, wp, wdt)
    dtraw = _mm_nn(xn, lwt["wdt"], tm=tl["tm"], tn=DT_PAD, name=n + "dt_proj")
    hl, ya, gates = carry(_lru_fwd, "lru", 3, proj, lwt["lw"], r=tl["r"], name=n + "lru_fwd")
    yssd, yb, states, xc, dsl = carry(_ssd_fwd, "ssd", 5, proj, dtraw, lwt["sw"], rb=tl["rb"], name=n + "ssd_fwd")
    ta, tb, merged = _branch_merge(ya, yb, proj, lwt["wba"], lwt["wbb"], lwt["bgate"], tm=tl["tmn"], tn=D, name=n + "merge")
    hmid = _mm_nn(merged, lwt["wout"], tm=tl["tmn"], tn=D, name=n + "out_proj", residual=h)
    xn2, gu = _norm_mm(hmid, lwt["g2"], lwt["wfi"], tm=tl["tmi"], tn=FFN_SHARD, name=n + "ffn_in", out_dtype=_MXU)
    act, hout = _swiglu_mm(gu, lwt["wfo"], hmid, tm=tl["tm"], tn=D, name=n + "ffn_out")
    saved = dict(h=h, xn=xn, proj=proj, dtraw=dtraw, hl=hl, ya=ya, gates=gates, yssd=yssd, yb=yb, states=states, xc=xc, dsl=dsl, ta=ta, tb=tb,
                 merged=merged, hmid=hmid, xn2=xn2, gu=gu, act=act)
    return hout, saved, lwt, [x for x in arrived if x[0] != l]


def _layer_bwd(dh, s, lwt, l, hooks=None):
    t = dh.shape[0]
    tl = _tiles(t)
    n = f"l{l}_"
    tt = tl["tmn"]
    big = {}
    hooks = hooks or {}

    def wgrad(key, a, b, name, **kw):
        big[key] = _wgrad(a, b, tt=tt, name=n + name, into=big.get(key), **kw)

    dgu = _ffn_bwd_act(dh, lwt["wfo"], s["gu"], tm=tl["tm"], name=n + "ffn_act_bwd")
    wgrad("w_ffn_out", s["act"], dh, "ffn_out_wgrad", ta=D_FF, tn=1024, out_shape=(D_FF, D),
          out_block=(D_FF, 1024), out_index=lambda o, j: (o, j))
    wgrad("w_ffn_in", s["xn2"], dgu, "ffn_in_wgrad", ta=D, tn=FFN_SHARD, out_shape=(4, D, FFN_SHARD),
          out_block=(None, D, FFN_SHARD), out_index=lambda o, j: (j, o, 0))
    dh1, dg2 = _mm_nt_rmsbwd(dgu, lwt["wfi"], s["hmid"], lwt["g2"], dh, tm=tl["tmn"], tk=FFN_SHARD, name=n + "ffn_in_dgrad")
    dta, dtb, dproj, dbg = _outproj_bwd(dh1, lwt["wout"], s["ta"], s["tb"], s["proj"], lwt["bgate"], (t, NP),
                                        tm=tl["tm"], name=n + "out_proj_bwd")
    rows_d = dict(ta=D, tn=512, out_block=(D, 512), out_index=lambda o, j: (o, j))
    wgrad("w_out", s["merged"], dh1, "out_proj_wgrad", out_shape=(D, D), **rows_d)
    dya = _mm_nt(dta, lwt["wba"], tm=tl["tm"], name=n + "branch_a_dgrad")
    dyb = _mm_nt(dtb, lwt["wbb"], tm=tl["tm"], name=n + "branch_b_dgrad")
    wgrad("w_branch", s["ya"], dta, "branch_a_wgrad", out_shape=(3 * D, D), a_tab=[0], o_tab=[0], **rows_d)
    wgrad("w_branch", s["yb"], dtb, "branch_b_wgrad", out_shape=(3 * D, D), a_tab=[0, 1], o_tab=[1, 2], **rows_d)
    comm_1 = hooks["lru"](big) if "lru" in hooks else None
    dproj, lsm, dwa, dwx, *got_1 = _lru_bwd(s["proj"], s["hl"], s["gates"], dya, dproj, lwt["lw"], r=tl["r"], name=n + "lru_bwd",
                                            comm=comm_1)
    comm_2 = hooks["ssd"](got_1) if "ssd" in hooks else None
    dproj, ddt, gconv, gch, ghd, *got_2 = _ssd_bwd(s["proj"], s["dtraw"], s["yssd"], s["states"], s["xc"], s["dsl"], dyb, dproj, lwt["sw"],
                                                   rb=tl["rb"], name=n + "ssd_bwd", comm=comm_2)
    lsm = lsm.reshape(8, 8, D).sum(axis=1)
    gconv = gconv.reshape(5, 8, XBC).sum(axis=1)
    gch = gch.reshape(2, 8, SSD_INNER).sum(axis=1)
    w_in = dict(tn=D, out_shape=(W_IN_ROWS, D), out_index=lambda o, j: (o, j))
    wgrad("w_in", dproj, s["xn"], "in_proj_wgrad", ta=1024, out_block=(1024, D),
          a_tab=list(range(9)), o_tab=[2, 3, 4, 5, 6, 0, 7, 8, 1], **w_in)
    wgrad("w_in", ddt, s["xn"], "dt_proj_wgrad", ta=DT_PAD, out_block=(DT_PAD, D), a_tab=[0],
          o_tab=[NP // DT_PAD], **w_in)
    comm_3 = hooks["in_dgrad"](big) if "in_dgrad" in hooks else None
    dh0, dg1, *got_3 = _mm_nt_rmsbwd(dproj, lwt["wp"], s["h"], lwt["g1"], dh1, tm=tl["tmn"], tk=2304,
                                     name=n + "in_proj_dgrad", extra=(ddt, lwt["wdt"]), comm=comm_3)
    grads = dict(
        lru_conv_w=lsm[0:4], lru_conv_b=lsm[4], lru_b_a=lsm[5], lru_b_x=lsm[6], lru_lambda=lsm[7],
        lru_w_a=_bd256_diag(dwa), lru_w_x=_bd256_diag(dwx),
        ssd_conv_w=gconv[0:4], ssd_conv_b=gconv[4], ssd_norm_g=gch[0], ssd_D=gch[1].reshape(N_HEADS, HEAD_P).sum(axis=-1),
        ssd_dt_bias=ghd[0, 0:N_HEADS], ssd_A_log=ghd[1, 0:N_HEADS],
        b_gate=dbg[0], norm1_g=dg1[0], norm2_g=dg2[0])
    return dh0, grads, big, (got_2, got_3)


def _local_step(x, target, w, conv, small, prefetch=None, early_reduce=None):
    h = x
    w = [dict(wl) for wl in w]
    lwts, saved = [], []
    for l in range(N_LAYERS):
        h, s, lwt, arrived = _layer_fwd(h, w[l], conv, small, l, prefetch if l == 0 else None)
        for layer, ws in arrived:
            w[layer].update(ws)
        lwts.append(lwt)
        saved.append(s)
    loss_blk, dgf, dh = _loss_head(h, small["norm_f"].reshape(1, D), target, tm=_tiles(x.shape[0])["tm"], name="loss_head")
    per_layer, big, carried = [None] * N_LAYERS, [None] * N_LAYERS, None
    for l in reversed(range(N_LAYERS)):
        hooks = early_reduce(big[1]) if (early_reduce is not None and l == 0) else None
        dh, per_layer[l], big[l], carried = _layer_bwd(dh, saved[l], lwts[l], l, hooks)
    grads = {k: jnp.stack([per_layer[l][k] for l in range(N_LAYERS)], axis=0) for k in per_layer[0]}
    grads["norm_f"] = dgf[0]
    return loss_blk, dh, grads, big, carried


PACK_W = 1024
BIG = (("w_in", W_IN_SHARD, D, W_IN_SHARD, 256), ("w_branch", 768, D, 256, D), ("w_out", 256, D, 256, D),
       ("w_ffn_in", D, FFN_SHARD, 256, FFN_SHARD), ("w_ffn_out", 704, D, 352, D))
CONV = ("lru_conv_w", "ssd_conv_w")
SMALL = ("norm1_g", "b_gate", "lru_conv_b", "lru_w_a", "lru_b_a", "lru_w_x", "lru_b_x", "lru_lambda", "ssd_conv_b",
         "ssd_dt_bias", "ssd_A_log", "ssd_D", "ssd_norm_g", "norm2_g", "norm_f")
_WIRE = jnp.bfloat16
N_CHIPS = 4
N_DEV = 8


def _mesh_pos():
    return lax.axis_index("x"), lax.axis_index("y"), lax.axis_index("c")


HBM_SPEC = pl.BlockSpec(memory_space=pltpu.HBM)


def _remote(src, dst, send_sems, recv_sems, k, to):
    return pltpu.make_async_remote_copy(src_ref=src, dst_ref=dst, send_sem=send_sems.at[k], recv_sem=recv_sems.at[k],
                                        device_id=to, device_id_type=MESH)


def _other_chips(x, y):
    return [(1 - x, y), (x, 1 - y), (1 - x, 1 - y)]


def _weight_fetch(loc, layer, owner):
    names = list(owner)
    rows = {n: loc[n].shape[1] for n in names}
    by_chip = ("w_in", "w_ffn_in")
    shapes = [((N_CHIPS,) + loc[n].shape[1:]) if n in by_chip else (N_CHIPS * rows[n], D) for n in names]

    def place(o_ref, n, chip):
        if n in by_chip:
            return o_ref.at[chip]
        return o_ref.at[pl.ds(pl.multiple_of(chip * rows[n], 16), rows[n]), :]

    def step(which, in_refs, o_refs, send_sems, recv_sems):
        x, y, c = _mesh_pos()
        s = 2 * x + y
        sib = (x, y, 1 - c)
        chips = _other_chips(x, y)
        for core in (0, 1):
            @pl.when(c == core)
            def _():
                for k, n in enumerate(names):
                    for j, (px, py) in enumerate(chips):
                        landed = place(o_refs[k], n, 2 * px + py)
                        sent = _remote(in_refs[k].at[layer], place(o_refs[k], n, s), send_sems, recv_sems, 3 * k + j,
                                       (px, py, c))
                        arrives = _remote(in_refs[k].at[layer], landed, send_sems, recv_sems, 3 * k + j, (px, py, c))
                        passed = _remote(landed, landed, send_sems, recv_sems, 3 * (len(names) + k) + j, sib)
                        if owner[n] == core:
                            if which == "start":
                                sent.start()
                            elif which == "mid":
                                arrives.wait_recv()
                                passed.start()
                            else:
                                sent.wait_send()
                                passed.wait_send()
                        elif which == "end":
                            passed.wait_recv()

    return dict(inputs=[loc[n] for n in names], names=names,
                out_shapes=[jax.ShapeDtypeStruct(shp, loc[n].dtype) for shp, n in zip(shapes, names)],
                sems=[pltpu.SemaphoreType.DMA((6 * len(names),)), pltpu.SemaphoreType.DMA((6 * len(names),))],
                start=functools.partial(step, "start"), mid=functools.partial(step, "mid"),
                end=functools.partial(step, "end"))


def _comm_now(comm, name):
    n, no = len(comm["inputs"]), len(comm["out_shapes"])

    def body(*refs):
        parts = (refs[:n], refs[n:n + no]) + tuple(refs[n + no:])
        comm["start"](*parts)
        comm["mid"](*parts)
        comm["end"](*parts)

    return pl.pallas_call(
        body, name=name, in_specs=[HBM_SPEC] * n, out_specs=[HBM_SPEC] * no, out_shape=comm["out_shapes"],
        scratch_shapes=comm["sems"],
    )(*comm["inputs"])


def _sibling_send(bufs, layer):
    n = len(bufs)

    def step(which, in_refs, o_refs, send_sems, recv_sems):
        x, y, c = _mesh_pos()
        copies = [_remote(in_refs[k], o_refs[k], send_sems, recv_sems, k, (x, y, 1 - c)) for k in range(n)]

        @pl.when(c != layer)
        def _():
            for cp in copies:
                if which == "start":
                    cp.start()
                elif which == "end":
                    cp.wait_send()

        @pl.when(c == layer)
        def _():
            for cp in copies:
                if which == "end":
                    cp.wait_recv()

    return dict(inputs=list(bufs), out_shapes=[jax.ShapeDtypeStruct(b.shape, b.dtype) for b in bufs],
                sems=[pltpu.SemaphoreType.DMA((n,)), pltpu.SemaphoreType.DMA((n,))],
                start=functools.partial(step, "start"), mid=functools.partial(step, "mid"),
                end=functools.partial(step, "end"))


def _add_cast(g, recv, own, *, a, tr, tc, name):
    wd = g.shape[1]
    nr = a // tr

    def body(own_ref, g_ref, r_ref, o_ref):
        @pl.when(own_ref[0] == 1)
        def _():
            o_ref[...] = (g_ref[...] + r_ref[...]).astype(o_ref.dtype)

    blk = pl.BlockSpec((tr, tc), lambda p, i, j, own_ref: ((p * nr + i) * own_ref[0], j * own_ref[0]))
    return pl.pallas_call(
        body, name=name,
        grid_spec=pltpu.PrefetchScalarGridSpec(
            num_scalar_prefetch=1, grid=(N_CHIPS, nr, wd // tc), in_specs=[blk, blk],
            out_specs=pl.BlockSpec((None, tr, tc), lambda p, i, j, own_ref: (p * own_ref[0], i * own_ref[0], j * own_ref[0]))),
        out_shape=jax.ShapeDtypeStruct((N_CHIPS, a, wd), _WIRE),
        compiler_params=_cp(("arbitrary", "arbitrary", "arbitrary")),
    )(own, g, recv)


def _chip_exchange(parts, layer):
    n = len(parts)

    def step(which, s_refs, o_refs, send_sems, recv_sems):
        x, y, c = _mesh_pos()
        s = 2 * x + y

        @pl.when(c == layer)
        def _():
            for j, (px, py) in enumerate(_other_chips(x, y)):
                for k in range(n):
                    p = 2 * px + py
                    sent = _remote(s_refs[k].at[p], o_refs[k].at[s], send_sems, recv_sems, n * j + k, (px, py, c))
                    if which == "start":
                        sent.start()
                    elif which == "end":
                        _remote(s_refs[k].at[p], o_refs[k].at[p], send_sems, recv_sems, n * j + k, (px, py, c)).wait_recv()
                        sent.wait_send()

    return dict(inputs=list(parts), out_shapes=[jax.ShapeDtypeStruct(p.shape, p.dtype) for p in parts],
                sems=[pltpu.SemaphoreType.DMA((3 * n,)), pltpu.SemaphoreType.DMA((3 * n,))],
                start=functools.partial(step, "start"), mid=functools.partial(step, "mid"),
                end=functools.partial(step, "end"))


def _sum_slots(slots, own, sel, *, tr, tc, name, layer=None, into=None):
    n, rows, wd = slots.shape
    k = own.shape[0]

    def body(sel_ref, s_ref, own_ref, *rest):
        o_ref = rest[-1]

        @pl.when(sel_ref[1] == 1)
        def _():
            mine = sel_ref[0]
            acc = jnp.zeros((tr, tc), F32)
            for p in range(n):
                acc = acc + jnp.where(mine == p, own_ref[...].astype(F32), s_ref[p].astype(F32))
            o_ref[...] = acc

    if layer is not None:
        out_spec = pl.BlockSpec((None, tr, tc), lambda i, j, sel_ref: (layer, i * sel_ref[1], j * sel_ref[1]))
        out_shape = jax.ShapeDtypeStruct((N_LAYERS, rows, wd), F32)
    else:
        out_spec = pl.BlockSpec((tr, tc), lambda i, j, sel_ref: (i * sel_ref[1], j * sel_ref[1]))
        out_shape = jax.ShapeDtypeStruct((rows, wd), F32)
    in_specs = [pl.BlockSpec((n, tr, tc), lambda i, j, sel_ref: (0, i * sel_ref[1], j * sel_ref[1])),
                pl.BlockSpec((None, tr, tc), lambda i, j, sel_ref: (sel_ref[0] if k > 1 else 0, i * sel_ref[1],
                                                                    j * sel_ref[1]))]
    args = [sel, slots, own]
    if into is not None:
        in_specs.append(pl.BlockSpec(memory_space=pl.ANY))
        args.append(into)
    return pl.pallas_call(
        body, name=name,
        grid_spec=pltpu.PrefetchScalarGridSpec(num_scalar_prefetch=1, grid=(rows // tr, wd // tc), in_specs=in_specs,
                                               out_specs=out_spec),
        out_shape=out_shape, input_output_aliases={3: 0} if into is not None else {},
        compiler_params=_cp(("arbitrary", "arbitrary")),
    )(*args)


def _sibling_share(both):
    n = len(both)

    def body(*refs):
        o_refs, (send_sems, recv_sems) = refs[n:2 * n], refs[2 * n:]
        x, y, c = _mesh_pos()
        sends = [_remote(o_refs[k].at[c], o_refs[k].at[c], send_sems, recv_sems, k, (x, y, 1 - c)) for k in range(n)]
        for cp in sends:
            cp.start()
        for k in range(n):
            _remote(o_refs[k].at[1 - c], o_refs[k].at[1 - c], send_sems, recv_sems, k, (x, y, 1 - c)).wait_recv()
        for cp in sends:
            cp.wait_send()

    return pl.pallas_call(
        body, name="grad_sibling_share", in_specs=[HBM_SPEC] * n, out_specs=[HBM_SPEC] * n,
        out_shape=[jax.ShapeDtypeStruct(b.shape, b.dtype) for b in both], input_output_aliases={k: k for k in range(n)},
        scratch_shapes=[pltpu.SemaphoreType.DMA((n,)), pltpu.SemaphoreType.DMA((n,))],
    )(*both)


def _allgather_devices(part):
    rows, wd = part.shape

    def step(which, in_refs, o_refs, send_sems, recv_sems):
        (p_ref,), (o_ref,) = in_refs, o_refs
        x, y, c = _mesh_pos()
        sib = (x, y, 1 - c)
        chips = _other_chips(x, y)
        slot = lambda px, py, pc: o_ref.at[4 * px + 2 * py + pc]
        first = [_remote(p_ref, slot(x, y, c), send_sems, recv_sems, 0, sib)]
        first += [_remote(p_ref, slot(x, y, c), send_sems, recv_sems, 1 + j, (px, py, c)) for j, (px, py) in enumerate(chips)]
        passed = [_remote(slot(px, py, c), slot(px, py, c), send_sems, recv_sems, 4 + j, sib)
                  for j, (px, py) in enumerate(chips)]
        if which == "start":
            for cp in first:
                cp.start()
        elif which == "mid":
            for j, (px, py) in enumerate(chips):
                _remote(p_ref, slot(px, py, c), send_sems, recv_sems, 1 + j, (px, py, c)).wait_recv()
                passed[j].start()
        else:
            _remote(p_ref, slot(x, y, 1 - c), send_sems, recv_sems, 0, sib).wait_recv()
            for j, (px, py) in enumerate(chips):
                _remote(slot(px, py, 1 - c), slot(px, py, 1 - c), send_sems, recv_sems, 4 + j, sib).wait_recv()
            for cp in first + passed:
                cp.wait_send()

    return dict(inputs=[part], out_shapes=[jax.ShapeDtypeStruct((N_DEV, rows, wd), part.dtype)],
                sems=[pltpu.SemaphoreType.DMA((N_DEV - 1,)), pltpu.SemaphoreType.DMA((N_DEV - 1,))],
                start=functools.partial(step, "start"), mid=functools.partial(step, "mid"),
                end=functools.partial(step, "end"))


def _comm_both(a, b):
    na, nao = len(a["inputs"]), len(a["out_shapes"])

    def step(which, in_refs, o_refs, sa, ra, sb, rb_):
        a[which](in_refs[:na], o_refs[:nao], sa, ra)
        b[which](in_refs[na:], o_refs[nao:], sb, rb_)

    return dict(inputs=a["inputs"] + b["inputs"], out_shapes=a["out_shapes"] + b["out_shapes"], sems=a["sems"] + b["sems"],
                start=functools.partial(step, "start"), mid=functools.partial(step, "mid"),
                end=functools.partial(step, "end"))


def _by_chip_to_full(stack):
    _, nl, r, b = stack.shape
    return stack.transpose(1, 2, 0, 3).reshape(nl, r, N_CHIPS * b)


def _sharded_step(a):
    x = a["x"][0]
    target = a["loss_target"][0]
    cx, cy, cc = _mesh_pos()
    chip = (2 * cx + cy).astype(jnp.int32)
    core = cc.astype(jnp.int32)
    me = (4 * cx + 2 * cy + cc).astype(jnp.int32)
    zero = jnp.zeros((), jnp.int32)
    dus = lax.dynamic_update_slice

    loc = {n: a[n].astype(_MXU) for n, *_ in BIG}

    def with_own(got, names, layer):
        out = {}
        for g, n in zip(got, names):
            mine = loc[n][layer]
            out[n] = (dus(g, mine[None], (chip, zero, zero)) if g.ndim == 3 else dus(g, mine, (chip * mine.shape[0], zero)))
        return out

    rest = {"w_ffn_in": 0, "w_branch": 1, "w_out": 1, "w_ffn_out": 1}
    conv_loc = jnp.concatenate([a[n].reshape(-1, PACK_W) for n in CONV], axis=0)
    now = _weight_fetch(loc, 0, {"w_in": 0})
    conv_all, *got_now = _comm_now(_comm_both(_allgather_devices(conv_loc), now), "allgather_weights")
    w0 = with_own(got_now, now["names"], 0)
    later = {"in_proj": (0, _weight_fetch(loc, 0, rest)), "lru": (1, _weight_fetch(loc, 1, {"w_in": 0})),
             "ssd": (1, _weight_fetch(loc, 1, rest))}
    prefetch = {k: (f, functools.partial(lambda got, layer, f: (layer, with_own(got, f["names"], layer)), layer=layer, f=f))
                for k, (layer, f) in later.items()}
    conv_all = dus(conv_all, conv_loc[None], (me, zero, zero))[0::2]
    conv, off = {}, 0
    for n in CONV:
        rows = a[n].size // PACK_W
        conv[n] = _by_chip_to_full(conv_all[:, off:off + rows].reshape((N_CHIPS,) + a[n].shape))
        off += rows
    small = {n: a[n] for n in SMALL}

    views = lambda big_l, specs: [big_l[n].reshape(-1, wd) for n, _, wd, _, _ in specs]
    owns = lambda layer: (core == layer).astype(jnp.int32)
    w_in_only, others = BIG[:1], BIG[1:]

    def partial_sums(big_l, recv, layer, specs):
        return [_add_cast(v, r, owns(layer).reshape(1), a=rows, tr=tr, tc=tc, name=f"grad_add_sibling_l{layer}_{n}")
                for v, r, (n, rows, _, tr, tc) in zip(views(big_l, specs), recv, specs)]

    def reduced(slots, parts, layer, into, specs):
        sel = jnp.stack([chip, owns(layer)])
        return [_sum_slots(s, p, sel, tr=tr, tc=tc, name=f"grad_sum_chips_l{layer}_{n}", layer=layer, into=buf)
                for s, p, buf, (n, _, _, tr, tc) in zip(slots, parts, into, specs)]

    kept = {}

    def early_reduce(big_1):
        def during_lru(big_0):
            kept["big_0"] = dict(big_0)
            return _comm_both(_sibling_send(views(big_1, BIG), 1), _sibling_send(views(big_0, others), 0))

        def during_ssd(recv):
            kept["parts_1"] = partial_sums(big_1, recv[:len(BIG)], 1, BIG)
            kept["parts_0"] = partial_sums(kept["big_0"], recv[len(BIG):], 0, others)
            return _comm_both(_chip_exchange(kept["parts_1"], 1), _chip_exchange(kept["parts_0"], 0))

        return dict(lru=during_lru, ssd=during_ssd, in_dgrad=lambda big_0: _sibling_send(views(big_0, w_in_only), 0))

    loss_blk, grad_x, grads, big, (slots, recv_in) = _local_step(x, target, [w0, {}], conv, small, prefetch, early_reduce)
    loss = lax.psum(loss_blk[0, 0], ("x", "y", "c"))
    both = reduced(slots[:len(BIG)], kept["parts_1"], 1, [None] * len(BIG), BIG)
    both[1:] = reduced(slots[len(BIG):], kept["parts_0"], 0, both[1:], others)
    parts_in = partial_sums(big[0], recv_in, 0, w_in_only)
    names = SMALL + CONV
    srows = -(-sum(grads[n].size for n in names) // (8 * PACK_W)) * 8
    flat = lambda d, ns: jnp.concatenate([d[n].reshape(-1) for n in ns])
    padto = lambda v: jnp.pad(v, (0, srows * PACK_W - v.shape[0])).reshape(srows, PACK_W)
    g_own = padto(flat(grads, names))
    g_all, *slots_in = _comm_now(_comm_both(_allgather_devices(g_own), _chip_exchange(parts_in, 0)), "grad_chip_exchange")
    both[:1] = reduced(slots_in, parts_in, 0, both[:1], w_in_only)
    done = dict(zip([n for n, *_ in BIG], _sibling_share(both)))
    g_big = {n: done[n].reshape(a[n].shape) for n in ("w_branch", "w_out", "w_ffn_in", "w_ffn_out")}
    gt = done["w_in"].transpose(0, 2, 1)
    first = jnp.concatenate([gt[..., 512 * j + 256 * part:512 * j + 256 * (part + 1)] for part in range(2) for j in range(4)]
                            + [gt[..., 2 * D:]], axis=-1)
    tail = W_IN_SHARD - (IN_DIM - 7168)
    last = jnp.concatenate([gt[..., :tail], gt[..., W_IN_SHARD - N_HEADS:], gt[..., tail:W_IN_SHARD - N_HEADS]], axis=-1)
    g_big["w_in"] = jnp.where(chip == 0, first, jnp.where(chip == N_CHIPS - 1, last, gt))

    g_sum = _sum_slots(g_all, g_own[None], jnp.stack([me, zero + 1]), tr=srows, tc=PACK_W, name="small_grad_sum")
    off, g_small = 0, {}
    for n in names:
        g_small[n] = g_sum.reshape(-1)[off:off + grads[n].size].reshape(grads[n].shape)
        off += grads[n].size
    for n in CONV:
        width = a[n].shape[2]
        g_big[n] = lax.dynamic_slice(g_small.pop(n), (zero, zero, chip * width), a[n].shape)

    out_g, out_d, out_m, out_v = {}, {}, {}, {}
    for n in g_big:
        shp = a[n].shape
        two_d = (shp[0] * shp[1], shp[2])
        d_, m_, v_ = _adamw(a[n].reshape(two_d), g_big[n].reshape(two_d), a["m_" + n].reshape(two_d),
                            a["v_" + n].reshape(two_d), name="adamw_" + n)
        out_g[n], out_d[n], out_m[n], out_v[n] = g_big[n], d_.reshape(shp), m_.reshape(shp), v_.reshape(shp)
    d_, m_, v_ = _adamw(padto(flat(a, SMALL)), padto(flat(g_small, SMALL)), padto(flat({n: a["m_" + n] for n in SMALL}, SMALL)),
                        padto(flat({n: a["v_" + n] for n in SMALL}, SMALL)), name="adamw_small")
    off = 0
    for n in SMALL:
        cut = lambda v: v.reshape(-1)[off:off + a[n].size].reshape(a[n].shape)
        out_g[n], out_d[n], out_m[n], out_v[n] = g_small[n], cut(d_), cut(m_), cut(v_)
        off += a[n].size
    return loss, grad_x[None], out_g, out_d, out_m, out_v


WEIGHTS = ("norm1_g", "w_in", "b_gate", "lru_conv_w", "lru_conv_b", "lru_w_a", "lru_b_a", "lru_w_x", "lru_b_x", "lru_lambda",
           "ssd_conv_w", "ssd_conv_b", "ssd_dt_bias", "ssd_A_log", "ssd_D", "ssd_norm_g", "w_branch", "w_out", "norm2_g",
           "w_ffn_in", "w_ffn_out", "norm_f")
INPUTS = ("x",) + WEIGHTS + ("loss_target",) + tuple("m_" + n for n in WEIGHTS) + tuple("v_" + n for n in WEIGHTS)


def kernel(x, norm1_g, w_in, b_gate, lru_conv_w, lru_conv_b, lru_w_a, lru_b_a, lru_w_x, lru_b_x, lru_lambda, ssd_conv_w, ssd_conv_b, ssd_dt_bias, ssd_A_log, ssd_D, ssd_norm_g, w_branch, w_out, norm2_g, w_ffn_in, w_ffn_out, norm_f, loss_target, m_norm1_g, m_w_in, m_b_gate, m_lru_conv_w, m_lru_conv_b, m_lru_w_a, m_lru_b_a, m_lru_w_x, m_lru_b_x, m_lru_lambda, m_ssd_conv_w, m_ssd_conv_b, m_ssd_dt_bias, m_ssd_A_log, m_ssd_D, m_ssd_norm_g, m_w_branch, m_w_out, m_norm2_g, m_w_ffn_in, m_w_ffn_out, m_norm_f, v_norm1_g, v_w_in, v_b_gate, v_lru_conv_w, v_lru_conv_b, v_lru_w_a, v_lru_b_a, v_lru_w_x, v_lru_b_x, v_lru_lambda, v_ssd_conv_w, v_ssd_conv_b, v_ssd_dt_bias, v_ssd_A_log, v_ssd_D, v_ssd_norm_g, v_w_branch, v_w_out, v_norm2_g, v_w_ffn_in, v_w_ffn_out, v_norm_f):
    args = (x, norm1_g, w_in, b_gate, lru_conv_w, lru_conv_b, lru_w_a, lru_b_a, lru_w_x, lru_b_x, lru_lambda, ssd_conv_w, ssd_conv_b, ssd_dt_bias, ssd_A_log, ssd_D, ssd_norm_g, w_branch, w_out, norm2_g, w_ffn_in, w_ffn_out, norm_f, loss_target, m_norm1_g, m_w_in, m_b_gate, m_lru_conv_w, m_lru_conv_b, m_lru_w_a, m_lru_b_a, m_lru_w_x, m_lru_b_x, m_lru_lambda, m_ssd_conv_w, m_ssd_conv_b, m_ssd_dt_bias, m_ssd_A_log, m_ssd_D, m_ssd_norm_g, m_w_branch, m_w_out, m_norm2_g, m_w_ffn_in, m_w_ffn_out, m_norm_f, v_norm1_g, v_w_in, v_b_gate, v_lru_conv_w, v_lru_conv_b, v_lru_w_a, v_lru_b_a, v_lru_w_x, v_lru_b_x, v_lru_lambda, v_ssd_conv_w, v_ssd_conv_b, v_ssd_dt_bias, v_ssd_A_log, v_ssd_D, v_ssd_norm_g, v_w_branch, v_w_out, v_norm2_g, v_w_ffn_in, v_w_ffn_out, v_norm_f)
    assert len(args) == len(INPUTS)
    loss, grad_x, g, d, m, v = _sharded_step(dict(zip(INPUTS, args)))
    return (loss, grad_x, *[g[n] for n in WEIGHTS], *[d[n] for n in WEIGHTS], *[m[n] for n in WEIGHTS],
            *[v[n] for n in WEIGHTS])
```

```python
import functools
import math

import numpy as np
import jax
import jax.numpy as jnp
from jax import lax
from jax.experimental import pallas as pl
from jax.experimental.pallas import tpu as pltpu

F32 = jnp.float32
_MXU = jnp.bfloat16
_HI = lax.Precision.HIGHEST

D = 1024
EPS = 1e-6
N_LAYERS = 2
LRU_C = 8.0
N_HEADS = 32
HEAD_P = 64
N_GROUPS = 4
N_STATE = 128
SSD_INNER = 2048
XBC = 3072
D_FF = 2816
CHUNK = 64
NORM_ROWS = 32
IN_DIM = 9248

NP = 9216
ZX_W = 5120
G0 = 6144
LBLK = 512
DT_PAD = 128

VMEM_LIMIT_BYTES_V7X = 56 * 1024 * 1024

ADAM_LR, ADAM_B1, ADAM_B2, ADAM_EPS, ADAM_WD, ADAM_STEP = 0.001, 0.9, 0.999, 1e-08, 0.01, 10
MESH = pl.DeviceIdType.MESH


def _cp(sem):
    return pltpu.CompilerParams(dimension_semantics=sem, vmem_limit_bytes=VMEM_LIMIT_BYTES_V7X)


def _lblk_col(j):
    return 10 + j + 4 * (j // 2)


def _sigmoid(x):
    return 0.5 * jnp.tanh(0.5 * x) + 0.5


def _softplus(x):
    return jnp.maximum(x, 0.0) + jnp.log(1.0 + jnp.exp(-jnp.abs(x)))


def _silu(x):
    return x * _sigmoid(x)


_GELU_C0 = math.sqrt(2.0 / math.pi)
_GELU_C1 = 0.044715


def _gelu_and_grad(x):
    t = jnp.tanh(_GELU_C0 * (x + _GELU_C1 * x * x * x))
    g = 0.5 * x * (1.0 + t)
    dg = 0.5 * (1.0 + t) + 0.5 * x * (1.0 - t * t) * _GELU_C0 * (1.0 + 3.0 * _GELU_C1 * x * x)
    return g, dg


def _one_minus_exp(x):
    p = 1.0 + x * (1.0 / 7.0)
    p = 1.0 + x * (1.0 / 6.0) * p
    p = 1.0 + x * (1.0 / 5.0) * p
    p = 1.0 + x * (1.0 / 4.0) * p
    p = 1.0 + x * (1.0 / 3.0) * p
    p = 1.0 + x * (1.0 / 2.0) * p
    return jnp.where(x > -0.3, -x * p, 1.0 - jnp.exp(x))


def _dot(a, b):
    return jnp.dot(a.astype(_MXU), b.astype(_MXU), preferred_element_type=F32)


def _dot_nt(a, b):
    return lax.dot_general(a.astype(_MXU), b.astype(_MXU), (((1,), (1,)), ((), ())), preferred_element_type=F32)


def _dot_tn(a, b):
    return lax.dot_general(a.astype(_MXU), b.astype(_MXU), (((0,), (0,)), ((), ())), preferred_element_type=F32)


def _shift_down(x, prev8, k):
    xr = pltpu.roll(x, k, 0)
    pr = pltpu.roll(prev8, k, 0)
    row = lax.broadcasted_iota(jnp.int32, prev8.shape, 0)
    head = jnp.where(row < k, pr, xr[0:8])
    return jnp.concatenate([head, xr[8:]], axis=0)


def _shift_up(x, next8, k):
    r = x.shape[0]
    xr = pltpu.roll(x, r - k, 0)
    nr = pltpu.roll(next8, 8 - k, 0)
    row = lax.broadcasted_iota(jnp.int32, next8.shape, 0)
    tail = jnp.where(row >= 8 - k, nr, xr[r - 8:r])
    return jnp.concatenate([xr[:r - 8], tail], axis=0)


def _conv4(x, prev8, w_ref, b_ref, cols=slice(None)):
    acc = x * w_ref[3:4, cols] + b_ref[0:1, cols]
    for k in (1, 2, 3):
        acc = acc + _shift_down(x, prev8, k) * w_ref[3 - k:4 - k, cols]
    return acc


def _lin_scan(a, b, reverse):
    r = a.shape[0]
    row = lax.broadcasted_iota(jnp.int32, a.shape, 0)
    d = 1
    while d < r:
        sh = (r - d) if reverse else d
        a_s = pltpu.roll(a, sh, 0)
        b_s = pltpu.roll(b, sh, 0)
        m = (row < r - d) if reverse else (row >= d)
        b = jnp.where(m, a * b_s + b, b)
        a = jnp.where(m, a * a_s, a)
        d *= 2
    return a, b


def _rsum(x):
    return jnp.sum(x, axis=0, keepdims=True)


def _comm_specs(comm):
    if comm is None:
        return [], [], [], [], []
    n = len(comm["inputs"])
    return list(comm["inputs"]), [HBM_SPEC] * n, [HBM_SPEC] * len(comm["out_shapes"]), list(comm["out_shapes"]), comm["sems"]


def _comm_steps(comm, refs, n_in, n_out, first, mid, last):
    ni, no = len(comm["inputs"]), len(comm["out_shapes"])
    parts = (refs[n_in:n_in + ni], refs[n_out:n_out + no]) + tuple(refs[len(refs) - len(comm["sems"]):])
    for when, what in ((first, "start"), (mid, "mid"), (last, "end")):
        @pl.when(when)
        def _():
            comm[what](*parts)


def _norm_mm(h, gamma, w, *, tm, tn, name, out_dtype=F32, comm=None):
    m, k = h.shape
    if w.ndim == 3:
        assert w.shape[2] == tn
        n = w.shape[0] * tn
        w_spec = pl.BlockSpec((None, k, tn), lambda i, j: (j, 0, 0))
    else:
        n = w.shape[1]
        w_spec = pl.BlockSpec((k, tn), lambda i, j: (0, j))

    c_args, c_in_specs, c_out_specs, c_out_shapes, c_sems = _comm_specs(comm)
    ni, nj = m // tm, n // tn

    def body(*refs):
        h_ref, g_ref, w_ref = refs[:3]
        xn_ref, o_ref = refs[3 + len(c_args):5 + len(c_args)]
        i, j = pl.program_id(0), pl.program_id(1)
        if comm is not None:
            _comm_steps(comm, refs, 3, 5 + len(c_args), (i == 0) & (j == 0), (i == (3 * ni) // 4) & (j == 0),
                        (i == ni - 1) & (j == nj - 1))

        @pl.when(j == 0)
        def _():
            x = h_ref[...]
            r = lax.rsqrt(jnp.mean(x * x, axis=-1, keepdims=True) + EPS)
            xn_ref[...] = ((x * r) * g_ref[...]).astype(xn_ref.dtype)
        o_ref[...] = jnp.dot(xn_ref[...], w_ref[...], preferred_element_type=F32).astype(o_ref.dtype)

    return pl.pallas_call(
        body, name=name, grid=(ni, nj),
        in_specs=[pl.BlockSpec((tm, k), lambda i, j: (i, 0)), pl.BlockSpec((1, k), lambda i, j: (0, 0)), w_spec] + c_in_specs,
        out_specs=[pl.BlockSpec((tm, k), lambda i, j: (i, 0)), pl.BlockSpec((tm, tn), lambda i, j: (i, j))] + c_out_specs,
        out_shape=[jax.ShapeDtypeStruct((m, k), _MXU), jax.ShapeDtypeStruct((m, n), out_dtype)] + c_out_shapes,
        scratch_shapes=c_sems,
        compiler_params=_cp(("arbitrary", "arbitrary") if comm is not None else ("parallel", "arbitrary")),
    )(h, gamma, w, *c_args)


def _mm_nn(a, w, *, tm, tn, name, residual=None):
    m, k = a.shape
    n = w.shape[1]

    def body(*refs):
        if residual is None:
            a_ref, w_ref, o_ref = refs
            o_ref[...] = _dot(a_ref[...], w_ref[...])
        else:
            a_ref, w_ref, r_ref, o_ref = refs
            o_ref[...] = _dot(a_ref[...], w_ref[...]) + r_ref[...]

    in_specs = [pl.BlockSpec((tm, k), lambda i, j: (i, 0)), pl.BlockSpec((k, tn), lambda i, j: (0, j))]
    args = [a, w]
    if residual is not None:
        in_specs.append(pl.BlockSpec((tm, tn), lambda i, j: (i, j)))
        args.append(residual)
    return pl.pallas_call(
        body, name=name, grid=(m // tm, n // tn), in_specs=in_specs,
        out_specs=pl.BlockSpec((tm, tn), lambda i, j: (i, j)),
        out_shape=jax.ShapeDtypeStruct((m, n), F32),
        compiler_params=_cp(("parallel", "parallel")),
    )(*args)


def _wgrad(a, b, *, tt, ta, tn, name, out_shape, out_block, out_index, a_tab=None, o_tab=None, into=None):
    t = a.shape[0]
    a_tab = list(range(a.shape[1] // ta)) if a_tab is None else a_tab
    o_tab = a_tab if o_tab is None else o_tab
    nb = b.shape[1] // tn

    def body(at_ref, ot_ref, a_ref, b_ref, *rest):
        del at_ref, ot_ref
        o_ref = rest[-1]

        @pl.when(pl.program_id(2) == 0)
        def _():
            o_ref[...] = jnp.zeros_like(o_ref)
        o_ref[...] += _dot_tn(a_ref[...], b_ref[...])

    in_specs = [pl.BlockSpec((tt, ta), lambda r, j, i, at, ot: (i, at[r])),
                pl.BlockSpec((tt, tn), lambda r, j, i, at, ot: (i, j))]
    args = [jnp.asarray(a_tab, jnp.int32), jnp.asarray(o_tab, jnp.int32), a, b]
    aliases = {}
    if into is not None:
        in_specs.append(pl.BlockSpec(memory_space=pl.ANY))
        args.append(into)
        aliases = {4: 0}
    return pl.pallas_call(
        body, name=name,
        grid_spec=pltpu.PrefetchScalarGridSpec(
            num_scalar_prefetch=2, grid=(len(a_tab), nb, t // tt), in_specs=in_specs,
            out_specs=pl.BlockSpec(out_block, lambda r, j, i, at, ot: out_index(ot[r], j))),
        out_shape=jax.ShapeDtypeStruct(out_shape, F32), input_output_aliases=aliases,
        compiler_params=_cp(("parallel", "parallel", "arbitrary")),
    )(*args)


def _mm_nt(a, w, *, tm, name):
    m, kc = a.shape
    n = w.shape[0]

    def body(a_ref, w_ref, o_ref):
        o_ref[...] = _dot_nt(a_ref[...], w_ref[...])

    return pl.pallas_call(
        body, name=name, grid=(m // tm,),
        in_specs=[pl.BlockSpec((tm, kc), lambda i: (i, 0)), pl.BlockSpec((n, kc), lambda i: (0, 0))],
        out_specs=pl.BlockSpec((tm, n), lambda i: (i, 0)),
        out_shape=jax.ShapeDtypeStruct((m, n), F32),
        compiler_params=_cp(("parallel",)),
    )(a, w)


def _mm_nt_rmsbwd(dy, w, x, gamma, dres, *, tm, tk, name, extra=None, comm=None):
    m, kc = dy.shape
    nk = kc // tk
    ni = m // tm
    n_x = 5 if extra is None else 7
    c_args, c_in_specs, c_out_specs, c_out_shapes, c_sems = _comm_specs(comm)
    if w.ndim == 3:
        assert w.shape[0] == nk and w.shape[2] == tk
        d = w.shape[1]
        w_spec = pl.BlockSpec((None, d, tk), lambda i, k: (k, 0, 0))
    else:
        d = w.shape[0]
        w_spec = pl.BlockSpec((d, tk), lambda i, k: (0, k))

    def body(*refs):
        dy_ref, w_ref, x_ref, g_ref, r_ref = refs[:5]
        if extra is not None:
            dy2_ref, w2_ref = refs[5:7]
        n_out = n_x + len(c_args)
        dx_ref, dg_ref = refs[n_out:n_out + 2]
        acc_ref = refs[n_out + 2 + len(c_out_shapes)]
        i, kk = pl.program_id(0), pl.program_id(1)
        if comm is not None:
            _comm_steps(comm, refs, n_x, n_out + 2, (i == 0) & (kk == 0), (i == (3 * ni) // 4) & (kk == 0),
                        (i == ni - 1) & (kk == nk - 1))

        @pl.when(kk == 0)
        def _():
            acc_ref[...] = jnp.zeros_like(acc_ref)

        @pl.when((i == 0) & (kk == 0))
        def _():
            dg_ref[...] = jnp.zeros_like(dg_ref)

        acc_ref[...] += _dot_nt(dy_ref[...], w_ref[...])

        @pl.when(kk == nk - 1)
        def _():
            dxn = acc_ref[...]
            if extra is not None:
                dxn = dxn + _dot_nt(dy2_ref[...], w2_ref[...])
            xv = x_ref[...]
            r = lax.rsqrt(jnp.mean(xv * xv, axis=-1, keepdims=True) + EPS)
            xh = xv * r
            dg_ref[0:1, :] += _rsum(dxn * xh)
            dxh = dxn * g_ref[...]
            dx_ref[...] = r_ref[...] + r * (dxh - xh * jnp.mean(dxh * xh, axis=-1, keepdims=True))

    in_specs = [pl.BlockSpec((tm, tk), lambda i, k: (i, k)), w_spec,
                pl.BlockSpec((tm, d), lambda i, k: (i, 0)), pl.BlockSpec((1, d), lambda i, k: (0, 0)),
                pl.BlockSpec((tm, d), lambda i, k: (i, 0))]
    args = [dy, w, x, gamma, dres]
    if extra is not None:
        k2 = extra[0].shape[1]
        in_specs += [pl.BlockSpec((tm, k2), lambda i, k: (i, 0)), pl.BlockSpec((d, k2), lambda i, k: (0, 0))]
        args += list(extra)
    return pl.pallas_call(
        body, name=name, grid=(ni, nk), in_specs=in_specs + c_in_specs,
        out_specs=[pl.BlockSpec((tm, d), lambda i, k: (i, 0)), pl.BlockSpec((8, d), lambda i, k: (0, 0))] + c_out_specs,
        out_shape=[jax.ShapeDtypeStruct((m, d), F32), jax.ShapeDtypeStruct((8, d), F32)] + c_out_shapes,
        scratch_shapes=[pltpu.VMEM((tm, d), F32)] + c_sems,
        compiler_params=_cp(("arbitrary", "arbitrary")),
    )(*args, *c_args)


def _rsum8(x):
    acc = x[0:8]
    for g in range(1, x.shape[0] // 8):
        acc = acc + x[8 * g:8 * (g + 1)]
    return acc


def _lru_gates(x, prev8, cw_ref, cb_ref, wa_ref, wx_ref, ba_ref, bx_ref, lam_ref):
    u = _conv4(x, prev8, cw_ref, cb_ref)
    ra =_sigmoid(_dot(u, wa_ref[0]) + ba_ref[...])
    ia = _sigmoid(_dot(u, wx_ref[0]) + bx_ref[...])
    sp = _softplus(-lam_ref[...])
    log_a = -LRU_C * ra * sp
    a = jnp.exp(log_a)
    m2 = _one_minus_exp(2.0 * log_a)
    mult = jnp.sqrt(m2)
    return u, ra, ia, sp, a, m2, mult


def _lru_fwd(proj, lw, *, r, name, comm=None):
    t = proj.shape[0]
    nt = t // r
    c_args, c_in_specs, c_out_specs, c_out_shapes, c_sems = _comm_specs(comm)

    def body(*refs):
        xg_ref, xp_ref, cw_ref, cb_ref, wa_ref, wx_ref, ba_ref, bx_ref, lam_ref = refs[:9]
        hl_ref, ya_ref, sv_ref = refs[9 + len(c_args):12 + len(c_args)]
        carry_ref = refs[12 + len(c_args) + len(c_out_shapes)]
        i = pl.program_id(1)
        if comm is not None:
            j = pl.program_id(0)
            _comm_steps(comm, refs, 9, 12 + len(c_args), (j == 0) & (i == 0), (j == 3) & (i == 0), (j == 3) & (i == nt - 1))

        @pl.when(i == 0)
        def _():
            carry_ref[...] = jnp.zeros_like(carry_ref)

        x = xg_ref[:, 0:256]
        lg = xg_ref[:, 256:512]
        prev8 = jnp.where(i == 0, 0.0, xp_ref[:, 0:256])
        u, ra, ia, sp, a, m2, mult = _lru_gates(x, prev8, cw_ref, cb_ref, wa_ref, wx_ref, ba_ref, bx_ref, lam_ref)
        for k, v in enumerate((u, ra, ia, a, mult)):
            sv_ref[k] = v
        ac, hc = _lin_scan(a, mult * ia * u, False)
        h = hc + ac * carry_ref[0:1, :]
        hl_ref[...] = h
        carry_ref[0:1, :] = hl_ref[r - 1:r, :]
        g, _ = _gelu_and_grad(lg)
        ya_ref[...] = (g * h).astype(ya_ref.dtype)

    small = lambda rows: pl.BlockSpec((rows, 256), lambda j, i: (0, j))
    return pl.pallas_call(
        body, name=name, grid=(4, nt),
        in_specs=[pl.BlockSpec((r, LBLK), lambda j, i: (i, _lblk_col(j))),
                  pl.BlockSpec((8, LBLK), lambda j, i: (jnp.maximum(i * (r // 8) - 1, 0), _lblk_col(j))),
                  small(4), small(1),
                  pl.BlockSpec((1, 256, 256), lambda j, i: (j, 0, 0)), pl.BlockSpec((1, 256, 256), lambda j, i: (j, 0, 0)),
                  small(1), small(1), small(1)] + c_in_specs,
        out_specs=[pl.BlockSpec((r, 256), lambda j, i: (i, j)), pl.BlockSpec((r, 256), lambda j, i: (i, j)),
                   pl.BlockSpec((5, r, 256), lambda j, i: (0, i, j))] + c_out_specs,
        out_shape=[jax.ShapeDtypeStruct((t, D), F32), jax.ShapeDtypeStruct((t, D), _MXU),
                   jax.ShapeDtypeStruct((5, t, D), F32)] + c_out_shapes,
        scratch_shapes=[pltpu.VMEM((8, 256), F32)] + c_sems,
        compiler_params=_cp(("arbitrary", "arbitrary") if comm is not None else ("parallel", "arbitrary")),
    )(proj, proj, lw["cw"], lw["cb"], lw["wa"], lw["wx"], lw["ba"], lw["bx"], lw["lam"], *c_args)


def _lru_bwd(proj, hl, gates, dya, dproj, lw, *, r, name, comm=None):
    t = proj.shape[0]
    nt = t // r
    c_args, c_in_specs, c_out_specs, c_out_shapes, c_sems = _comm_specs(comm)
    n_in = 10

    def body(*refs):
        xg_ref, hl_ref, hp_ref, sv_ref, dya_ref, cw_ref, wa_ref, wx_ref, lam_ref = refs[:9]
        n_out = n_in + len(c_args)
        dproj_ref, sm_ref, dwa_ref, dwx_ref = refs[n_out:n_out + 4]
        n_scr = n_out + 4 + len(c_out_shapes)
        carry_ref, du8_ref, row_scr = refs[n_scr:n_scr + 3]
        i = pl.program_id(1)
        if comm is not None:
            j = pl.program_id(0)
            _comm_steps(comm, refs, n_in, n_out + 4, (j == 0) & (i == 0), (j == 3) & (i == 0), (j == 3) & (i == nt - 1))

        @pl.when(i == 0)
        def _():
            carry_ref[...] = jnp.zeros_like(carry_ref)
            du8_ref[...] = jnp.zeros_like(du8_ref)
            sm_ref[...] = jnp.zeros_like(sm_ref)
            dwa_ref[...] = jnp.zeros_like(dwa_ref)
            dwx_ref[...] = jnp.zeros_like(dwx_ref)

        tile0 = i == nt - 1
        xp = xg_ref[:, 0:256]
        lg = xg_ref[:, 256:512]
        u, ra, ia, a, mult = (sv_ref[k] for k in range(5))
        sp = _softplus(-lam_ref[...])
        h = hl_ref[...]
        hprev = _shift_down(h, jnp.where(tile0, 0.0, hp_ref[...]), 1)
        dya_v = dya_ref[...]
        g, dg = _gelu_and_grad(lg)
        ac, lc = _lin_scan(_shift_up(a, carry_ref[...], 1), dya_v * g, True)
        lam_v = lc + ac * carry_ref[1:2, :]
        row_scr[0:8, :] = lam_v[0:8]
        row_scr[8:16, :] = a[0:8]
        carry_ref[1:2, :] = row_scr[0:1, :]
        carry_ref[0:1, :] = row_scr[8:9, :]
        da = lam_v * hprev
        dmult = lam_v * ia * u
        dia = lam_v * mult * u
        dlog = da * a - dmult * (a * a) / mult
        dra = -LRU_C * sp * dlog
        dpa = dra * ra * (1.0 - ra)
        dpx = dia * ia * (1.0 - ia)
        du = lam_v * mult * ia + _dot_nt(dpa, wa_ref[0]) + _dot_nt(dpx, wx_ref[0])
        dwa_ref[0] += _dot_tn(u, dpa)
        dwx_ref[0] += _dot_tn(u, dpx)
        dlx = du * cw_ref[3:4, :]
        sm_ref[24:32, :] += _rsum8(du * xp)
        for k in (1, 2, 3):
            du_k = _shift_up(du, du8_ref[...], k)
            dlx = dlx + du_k * cw_ref[3 - k:4 - k, :]
            sm_ref[8 * (3 - k):8 * (4 - k), :] += _rsum8(du_k * xp)
        du8_ref[...] = du[0:8]
        dproj_ref[:, 0:256] = dlx.astype(dproj_ref.dtype)
        dproj_ref[:, 256:512] = (dya_v * h * dg).astype(dproj_ref.dtype)
        sm_ref[32:40, :] += _rsum8(du)
        sm_ref[40:48, :] += _rsum8(dpa)
        sm_ref[48:56, :] += _rsum8(dpx)
        sm_ref[56:64, :] += _rsum8(-LRU_C * ra * dlog) * (-_sigmoid(-lam_ref[...]))

    rev = lambda i: nt - 1 - i
    small = lambda rows: pl.BlockSpec((rows, 256), lambda j, i: (0, j))
    wblk = pl.BlockSpec((1, 256, 256), lambda j, i: (j, 0, 0))
    return pl.pallas_call(
        body, name=name, grid=(4, nt),
        in_specs=[pl.BlockSpec((r, LBLK), lambda j, i: (rev(i), _lblk_col(j))),
                  pl.BlockSpec((r, 256), lambda j, i: (rev(i), j)),
                  pl.BlockSpec((8, 256), lambda j, i: (jnp.maximum(rev(i) * (r // 8) - 1, 0), j)),
                  pl.BlockSpec((5, r, 256), lambda j, i: (0, rev(i), j)),
                  pl.BlockSpec((r, 256), lambda j, i: (rev(i), j)),
                  small(4), wblk, wblk, small(1),
                  pl.BlockSpec(memory_space=pl.ANY)] + c_in_specs,
        out_specs=[pl.BlockSpec((r, LBLK), lambda j, i: (rev(i), _lblk_col(j))),
                   pl.BlockSpec((64, 256), lambda j, i: (0, j)), wblk, wblk] + c_out_specs,
        out_shape=[jax.ShapeDtypeStruct(dproj.shape, dproj.dtype), jax.ShapeDtypeStruct((64, D), F32),
                   jax.ShapeDtypeStruct((4, 256, 256), F32), jax.ShapeDtypeStruct((4, 256, 256), F32)] + c_out_shapes,
        scratch_shapes=[pltpu.VMEM((8, 256), F32), pltpu.VMEM((8, 256), F32), pltpu.VMEM((16, 256), F32)] + c_sems,
        input_output_aliases={n_in - 1: 0},
        compiler_params=_cp(("arbitrary", "arbitrary") if comm is not None else ("parallel", "arbitrary")),
    )(proj, hl, hl, gates, dya, lw["cw"], lw["wa"], lw["wx"], lw["lam"], dproj, *c_args)


def _head_cols(x):
    lane = lax.broadcasted_iota(jnp.int32, x.shape, 1)
    return [jnp.sum(jnp.where(lane == h, x, 0.0), axis=1, keepdims=True) for h in range(N_HEADS)]


def _compact_heads(blocks):
    lane = lax.broadcasted_iota(jnp.int32, blocks[0].shape, 1)
    lo = lane < HEAD_P
    out = jnp.zeros_like(blocks[0])
    for j, blk in enumerate(blocks):
        s_lo = jnp.sum(jnp.where(lo, blk, 0.0), axis=1, keepdims=True)
        s_hi = jnp.sum(jnp.where(lo, 0.0, blk), axis=1, keepdims=True)
        out = jnp.where(lane == 2 * j, s_lo, out)
        out = jnp.where(lane == 2 * j + 1, s_hi, out)
    return out


def _ssd_prelude(dtraw_ref, dtb_ref, alog_ref, dt_scr, a_scr):
    lane = lax.broadcasted_iota(jnp.int32, dt_scr.shape, 1)
    dt = jnp.where(lane < N_HEADS, _softplus(dtraw_ref[...] + dtb_ref[0:1, :]), 0.0)
    dt_scr[...] = dt
    a_scr[...] = dt * (-jnp.exp(alog_ref[0:1, :]))


def _ssd_chunk_scalars(dt_scr, a_scr, r_scr, r0):
    a_c = a_scr[pl.ds(r0, CHUNK), :]
    dt_c = dt_scr[pl.ds(r0, CHUNK), :]
    i0 = lax.broadcasted_iota(jnp.int32, (CHUNK, CHUNK), 0)
    i1 = lax.broadcasted_iota(jnp.int32, (CHUNK, CHUNK), 1)
    tri = jnp.where(i0 >= i1, 1.0, 0.0).astype(F32)
    cs = jnp.dot(tri, a_c, precision=_HI, preferred_element_type=F32)
    lane = lax.broadcasted_iota(jnp.int32, (CHUNK, 128), 1)
    srow = lax.broadcasted_iota(jnp.int32, (CHUNK, 128), 0)
    t_lo = jnp.where((lane < HEAD_P) & (srow <= lane), 1.0, 0.0).astype(F32)
    t_hi = jnp.where((lane >= HEAD_P) & (srow <= lane - HEAD_P), 1.0, 0.0).astype(F32)
    even = (lane % 2) == 0
    tn = (((0,), (0,)), ((), ()))
    r_scr[...] = (lax.dot_general(jnp.where(even, a_c, 0.0), t_lo, tn, precision=_HI, preferred_element_type=F32)
                  + lax.dot_general(jnp.where(even, 0.0, a_c), t_hi, tn, precision=_HI, preferred_element_type=F32))
    return cs, dt_c, _head_cols(cs), _head_cols(dt_c)


def _block_diag2(v):
    lo = lax.broadcasted_iota(jnp.int32, v.shape, 1) < HEAD_P
    return jnp.concatenate([jnp.where(lo, v, 0.0), jnp.where(lo, 0.0, v)], axis=0).astype(_MXU)


def _ssd_pair(xc_scr, r_scr, cs_cols, dt_cols, s2, r0, j, s2t=None):
    lane = lax.broadcasted_iota(jnp.int32, (CHUNK, 128), 1)
    srow = lax.broadcasted_iota(jnp.int32, (CHUNK, 128), 0)
    lo = lane < HEAD_P
    csc = jnp.where(lo, cs_cols[2 * j], cs_cols[2 * j + 1])
    dtc = jnp.where(lo, dt_cols[2 * j], dt_cols[2 * j + 1])
    csr = r_scr[2 * j:2 * j + 1, :] + r_scr[2 * j + 1:2 * j + 2, :]
    dm = jnp.where((lane & (HEAD_P - 1)) <= srow, jnp.exp(jnp.minimum(csc - csr, 0.0)), 0.0)
    xs = xc_scr[pl.ds(r0, CHUNK), j * 128:(j + 1) * 128]
    xd = xs * dtc
    csl = jnp.sum(jnp.where(srow == CHUNK - 1, csc, 0.0), axis=0, keepdims=True)
    out = dict(csc=csc, dtc=dtc, dm=dm, m2=s2 * dm, xs=xs, xd=xd, rhs=_block_diag2(xd), e=jnp.exp(csc),
               w=jnp.exp(csl - csc), dec=jnp.exp(csl))
    if s2t is not None:
        out["mt2"] = s2t * jnp.where((lane & (HEAD_P - 1)) >= srow, jnp.exp(jnp.minimum(csr - csc, 0.0)), 0.0)
    return out


def _cat(parts):
    return jnp.concatenate(parts, axis=1)


def _ssd_fwd(proj, dtraw, sw, *, rb, name, comm=None):
    t = proj.shape[0]
    ns, cb = t // rb, rb // CHUNK
    c_args, c_in_specs, c_out_specs, c_out_shapes, c_sems = _comm_specs(comm)

    def body(*refs):
        zx_ref, zp_ref, dtraw_ref, cw_ref, cbias_ref, dtb_ref, alog_ref, dsk_ref, ng_ref = refs[:9]
        yssd_ref, yb_ref, st_ref, xc_scr, dsl_ref = refs[9 + len(c_args):14 + len(c_args)]
        n_scr = 14 + len(c_args) + len(c_out_shapes)
        h_scr, dt_scr, a_scr, r_scr = refs[n_scr:n_scr + 4]
        i = pl.program_id(0)
        if comm is not None:
            _comm_steps(comm, refs, 9, 14 + len(c_args), i == 0, i == (3 * ns) // 4, i == ns - 1)

        @pl.when(i == 0)
        def _():
            h_scr[...] = jnp.zeros_like(h_scr)

        for j in range(XBC // 128):
            cs_, zc = slice(128 * j, 128 * (j + 1)), slice(2048 + 128 * j, 2048 + 128 * (j + 1))
            pre = _conv4(zx_ref[:, zc], jnp.where(i == 0, 0.0, zp_ref[:, zc]), cw_ref, cbias_ref, cs_)
            sg = _sigmoid(pre)
            xc_scr[:, cs_] = pre * sg
            dsl_ref[:, cs_] = sg * (1.0 + pre * (1.0 - sg))
        _ssd_prelude(dtraw_ref, dtb_ref, alog_ref, dt_scr, a_scr)

        def chunk(c, carry):
            r0 = pl.multiple_of(c * CHUNK, CHUNK)
            _, _, cs_cols, dt_cols = _ssd_chunk_scalars(dt_scr, a_scr, r_scr, r0)
            st_ref[c] = h_scr[...]
            for g in range(N_GROUPS):
                bg = xc_scr[pl.ds(r0, CHUNK), 2048 + 128 * g:2048 + 128 * (g + 1)]
                cg = xc_scr[pl.ds(r0, CHUNK), 2560 + 128 * g:2560 + 128 * (g + 1)]
                s2 = _dot_nt(cg, jnp.concatenate([bg, bg], axis=0))
                hp = h_scr[:, 512 * g:512 * (g + 1)]
                yoff = _dot(cg, hp)
                xdw, dec = [], []
                for jj in range(4):
                    j = 4 * g + jj
                    p = _ssd_pair(xc_scr, r_scr, cs_cols, dt_cols, s2, r0, j)
                    y = _dot(p["m2"], p["rhs"]) + yoff[:, 128 * jj:128 * (jj + 1)] * p["e"]
                    yssd_ref[pl.ds(r0, CHUNK), 128 * j:128 * (j + 1)] = y + dsk_ref[0:1, 128 * j:128 * (j + 1)] * p["xs"]
                    xdw.append(p["xd"] * p["w"])
                    dec.append(p["dec"])
                h_scr[:, 512 * g:512 * (g + 1)] = hp * _cat(dec) + _dot_tn(bg, _cat(xdw))
            return carry

        lax.fori_loop(0, cb, chunk, 0)
        for g in range(N_GROUPS):
            sl = slice(512 * g, 512 * (g + 1))
            for q in range(rb // NORM_ROWS):
                rw = slice(NORM_ROWS * q, NORM_ROWS * (q + 1))
                yz = yssd_ref[rw, sl] * _silu(zx_ref[rw, sl])
                rg = lax.rsqrt(jnp.mean(yz * yz, axis=-1, keepdims=True) + EPS)
                yb_ref[rw, sl] = (yz * rg * ng_ref[0:1, sl]).astype(yb_ref.dtype)

    full = lambda rows, cols: pl.BlockSpec((rows, cols), lambda i: (0, 0))
    return pl.pallas_call(
        body, name=name, grid=(ns,),
        in_specs=[pl.BlockSpec((rb, ZX_W), lambda i: (i, 0)),
                  pl.BlockSpec((8, ZX_W), lambda i: (jnp.maximum(i * (rb // 8) - 1, 0), 0)),
                  pl.BlockSpec((rb, DT_PAD), lambda i: (i, 0)),
                  full(4, XBC), full(1, XBC), full(1, DT_PAD), full(1, DT_PAD), full(1, SSD_INNER), full(1, SSD_INNER)]
        + c_in_specs,
        out_specs=[pl.BlockSpec((rb, SSD_INNER), lambda i: (i, 0)), pl.BlockSpec((rb, SSD_INNER), lambda i: (i, 0)),
                   pl.BlockSpec((cb, N_STATE, SSD_INNER), lambda i: (i, 0, 0)),
                   pl.BlockSpec((rb, XBC), lambda i: (i, 0)), pl.BlockSpec((rb, XBC), lambda i: (i, 0))] + c_out_specs,
        out_shape=[jax.ShapeDtypeStruct((t, SSD_INNER), F32), jax.ShapeDtypeStruct((t, SSD_INNER), _MXU),
                   jax.ShapeDtypeStruct((t // CHUNK, N_STATE, SSD_INNER), F32),
                   jax.ShapeDtypeStruct((t, XBC), F32), jax.ShapeDtypeStruct((t, XBC), F32)] + c_out_shapes,
        scratch_shapes=[pltpu.VMEM((N_STATE, SSD_INNER), F32), pltpu.VMEM((rb, DT_PAD), F32),
                        pltpu.VMEM((rb, DT_PAD), F32), pltpu.VMEM((128, 128), F32)] + c_sems,
        compiler_params=_cp(("arbitrary",)),
    )(proj, proj, dtraw, sw["cw"], sw["cb"], sw["dtb"], sw["alog"], sw["dsk"], sw["ng"], *c_args)


def _ssd_bwd(proj, dtraw, yssd, states, xc, dsl, dyb, dproj, sw, *, rb, name, comm=None):
    t = proj.shape[0]
    ns, cb = t // rb, rb // CHUNK
    c_args, c_in_specs, c_out_specs, c_out_shapes, c_sems = _comm_specs(comm)
    n_in = 13

    def body(*refs):
        zx_ref, dtraw_ref, yssd_ref, st_ref, xc_scr, dsl_scr, dyb_ref, cw_ref, dtb_ref, alog_ref, dsk_ref, ng_ref = refs[:12]
        n_out = n_in + len(c_args)
        dzx_ref, ddt_ref, gconv_ref, gch_ref, ghd_ref = refs[n_out:n_out + 5]
        n_scr = n_out + 5 + len(c_out_shapes)
        dht_scr, dy_scr, dxc_scr, dt_scr, a_scr, r_scr, dp8_scr = refs[n_scr:n_scr + 7]
        i = pl.program_id(0)
        if comm is not None:
            _comm_steps(comm, refs, n_in, n_out + 5, i == 0, i == (3 * ns) // 4, i == ns - 1)

        @pl.when(i == 0)
        def _():
            dht_scr[...] = jnp.zeros_like(dht_scr)
            dp8_scr[...] = jnp.zeros_like(dp8_scr)
            gconv_ref[...] = jnp.zeros_like(gconv_ref)
            gch_ref[...] = jnp.zeros_like(gch_ref)
            ghd_ref[...] = jnp.zeros_like(ghd_ref)

        _ssd_prelude(dtraw_ref, dtb_ref, alog_ref, dt_scr, a_scr)

        for g in range(N_GROUPS):
            sl = slice(512 * g, 512 * (g + 1))
            for q in range(rb // NORM_ROWS):
                rw = slice(NORM_ROWS * q, NORM_ROWS * (q + 1))
                zv = zx_ref[rw, sl]
                ys = yssd_ref[rw, sl]
                sg = _sigmoid(zv)
                sz = zv * sg
                yz = ys * sz
                rg = lax.rsqrt(jnp.mean(yz * yz, axis=-1, keepdims=True) + EPS)
                yn = yz * rg
                dyb_v = dyb_ref[rw, sl]
                gch_ref[0:8, sl] += _rsum8(dyb_v * yn)
                dyn = dyb_v * ng_ref[0:1, sl]
                dyz = rg * (dyn - yn * jnp.mean(dyn * yn, axis=-1, keepdims=True))
                dy_scr[rw, sl] = dyz * sz
                dzx_ref[rw, sl] = (dyz * ys * (sg * (1.0 + zv * (1.0 - sg)))).astype(dzx_ref.dtype)

        a_row = -jnp.exp(alog_ref[0:1, :])

        def chunk(cc, carry):
            c = cb - 1 - cc
            r0 = pl.multiple_of(c * CHUNK, CHUNK)
            rows = pl.ds(r0, CHUNK)
            _, dt_c, cs_cols, dt_cols = _ssd_chunk_scalars(dt_scr, a_scr, r_scr, r0)
            lane = lax.broadcasted_iota(jnp.int32, (CHUNK, 128), 1)
            srow = lax.broadcasted_iota(jnp.int32, (CHUNK, 128), 0)
            lo = lane < HEAD_P
            last = srow == CHUNK - 1
            p1_blocks, p3_blocks = [], []
            for g in range(N_GROUPS):
                gs = slice(512 * g, 512 * (g + 1))
                bg = xc_scr[rows, 2048 + 128 * g:2048 + 128 * (g + 1)]
                cg = xc_scr[rows, 2560 + 128 * g:2560 + 128 * (g + 1)]
                b2 = jnp.concatenate([bg, bg], axis=0)
                s2 = _dot_nt(cg, b2)
                s2t = _dot_nt(bg, jnp.concatenate([cg, cg], axis=0))
                hp = st_ref[c, :, gs]
                dht = dht_scr[:, gs]
                yoff = _dot(cg, hp)
                ps = [_ssd_pair(xc_scr, r_scr, cs_cols, dt_cols, s2, r0, 4 * g + jj, s2t) for jj in range(4)]
                dys = [dy_scr[rows, 128 * (4 * g + jj):128 * (4 * g + jj + 1)] for jj in range(4)]
                dye = _cat([dys[jj] * ps[jj]["e"] for jj in range(4)])
                w_g = _cat([p["w"] for p in ps])
                dcg = _dot_nt(dye, hp)
                dht_scr[:, gs] = _dot_tn(cg, dye) + _cat([p["dec"] for p in ps]) * dht
                dxd_state = w_g * _dot(bg, dht)
                dbg = _dot_nt(_cat([p["xd"] for p in ps]) * w_g, dht)
                tsum = _rsum(dht * hp)
                ds2 = jnp.zeros((CHUNK, 128), F32)
                for jj in range(4):
                    j = 4 * g + jj
                    ls = slice(128 * j, 128 * (j + 1))
                    p, dy2 = ps[jj], dys[jj]
                    dy_bd = _block_diag2(dy2)
                    dm2 = _dot_nt(dy2, p["rhs"])
                    ds2 = ds2 + dm2 * p["dm"]
                    gdiff = dm2 * p["m2"] - _dot_nt(p["xd"], dy_bd) * p["mt2"]
                    dxs = dxd_state[:, 128 * jj:128 * (jj + 1)]
                    dxd = _dot(p["mt2"], dy_bd) + dxs
                    end_row = _rsum(p["xd"] * dxs) + p["dec"] * tsum[:, 128 * jj:128 * (jj + 1)]
                    p1_blocks.append(gdiff + dy2 * yoff[:, 128 * jj:128 * (jj + 1)] * p["e"] - p["xd"] * dxs
                                     + jnp.where(last, end_row, 0.0))
                    p3_blocks.append(dxd * p["xs"])
                    dxc_scr[rows, ls] = dxd * p["dtc"] + dy2 * dsk_ref[0:1, ls]
                    gch_ref[8:16, ls] += _rsum8(dy2 * p["xs"])
                dcg = dcg + _dot(ds2, b2)
                rb2 = _dot_tn(ds2, cg)
                dxc_scr[rows, 2048 + 128 * g:2048 + 128 * (g + 1)] = dbg + rb2[0:CHUNK] + rb2[CHUNK:2 * CHUNK]
                dxc_scr[rows, 2560 + 128 * g:2560 + 128 * (g + 1)] = dcg
            dcs = _compact_heads(p1_blocks)
            i0 = lax.broadcasted_iota(jnp.int32, (CHUNK, CHUNK), 0)
            i1 = lax.broadcasted_iota(jnp.int32, (CHUNK, CHUNK), 1)
            triu = jnp.where(i1 >= i0, 1.0, 0.0).astype(F32)
            da = jnp.dot(triu, dcs, precision=_HI, preferred_element_type=F32)
            ddt = _compact_heads(p3_blocks) + da * a_row
            ddtraw = jnp.where(lane < N_HEADS, ddt * _sigmoid(dtraw_ref[rows, :] + dtb_ref[0:1, :]), 0.0)
            ddt_ref[rows, :] = ddtraw.astype(ddt_ref.dtype)
            ghd_ref[0:1, :] += _rsum(ddtraw)
            ghd_ref[1:2, :] += _rsum(da * dt_c) * a_row
            return carry

        lax.fori_loop(0, cb, chunk, 0)
        for j in range(XBC // 128):
            cs_, zc = slice(128 * j, 128 * (j + 1)), slice(2048 + 128 * j, 2048 + 128 * (j + 1))
            dpre = dxc_scr[:, cs_] * dsl_scr[:, cs_]
            xraw = zx_ref[:, zc]
            dx = dpre * cw_ref[3:4, cs_]
            gconv_ref[24:32, cs_] += _rsum8(dpre * xraw)
            for k in (1, 2, 3):
                dpre_k = _shift_up(dpre, dp8_scr[:, cs_], k)
                dx = dx + dpre_k * cw_ref[3 - k:4 - k, cs_]
                gconv_ref[8 * (3 - k):8 * (4 - k), cs_] += _rsum8(dpre_k * xraw)
            dzx_ref[:, zc] = dx.astype(dzx_ref.dtype)
            dp8_scr[:, cs_] = dpre[0:8]
            gconv_ref[32:40, cs_] += _rsum8(dpre)

    rev = lambda i: ns - 1 - i
    full = lambda rows, cols: pl.BlockSpec((rows, cols), lambda i: (0, 0))
    return pl.pallas_call(
        body, name=name, grid=(ns,),
        in_specs=[pl.BlockSpec((rb, ZX_W), lambda i: (rev(i), 0)),
                  pl.BlockSpec((rb, DT_PAD), lambda i: (rev(i), 0)),
                  pl.BlockSpec((rb, SSD_INNER), lambda i: (rev(i), 0)),
                  pl.BlockSpec((cb, N_STATE, SSD_INNER), lambda i: (rev(i), 0, 0)),
                  pl.BlockSpec((rb, XBC), lambda i: (rev(i), 0)), pl.BlockSpec((rb, XBC), lambda i: (rev(i), 0)),
                  pl.BlockSpec((rb, SSD_INNER), lambda i: (rev(i), 0)),
                  full(4, XBC), full(1, DT_PAD), full(1, DT_PAD), full(1, SSD_INNER), full(1, SSD_INNER),
                  pl.BlockSpec(memory_space=pl.ANY)] + c_in_specs,
        out_specs=[pl.BlockSpec((rb, ZX_W), lambda i: (rev(i), 0)), pl.BlockSpec((rb, DT_PAD), lambda i: (rev(i), 0)),
                   full(40, XBC), full(16, SSD_INNER), full(8, DT_PAD)] + c_out_specs,
        out_shape=[jax.ShapeDtypeStruct(dproj.shape, dproj.dtype), jax.ShapeDtypeStruct((t, DT_PAD), _MXU),
                   jax.ShapeDtypeStruct((40, XBC), F32), jax.ShapeDtypeStruct((16, SSD_INNER), F32),
                   jax.ShapeDtypeStruct((8, DT_PAD), F32)] + c_out_shapes,
        scratch_shapes=[pltpu.VMEM((N_STATE, SSD_INNER), F32),
                        pltpu.VMEM((rb, SSD_INNER), F32), pltpu.VMEM((rb, XBC), F32), pltpu.VMEM((rb, DT_PAD), F32),
                        pltpu.VMEM((rb, DT_PAD), F32), pltpu.VMEM((128, 128), F32), pltpu.VMEM((8, XBC), F32)] + c_sems,
        input_output_aliases={n_in - 1: 0},
        compiler_params=_cp(("arbitrary",)),
    )(proj, dtraw, yssd, states, xc, dsl, dyb, sw["cw"], sw["dtb"], sw["alog"], sw["dsk"], sw["ng"], dproj, *c_args)


def _branch_merge(ya, yb, proj, wba, wbb, bgate, *, tm, tn, name):
    t = ya.shape[0]
    nj = D // tn

    def body(ya_ref, yb_ref, ga_ref, gb_ref, wba_ref, wbb_ref, ba_ref, bb_ref, ta_ref, tb_ref, mg_ref):
        ta = _dot(ya_ref[...], wba_ref[...])
        tb = _dot(yb_ref[...], wbb_ref[...])
        ta_ref[...] = ta.astype(ta_ref.dtype)
        tb_ref[...] = tb.astype(tb_ref.dtype)
        ga = _sigmoid(ga_ref[...] + ba_ref[...])
        gb = _sigmoid(gb_ref[...] + bb_ref[...])
        mg_ref[...] = (ga * ta + gb * tb).astype(mg_ref.dtype)

    tile = pl.BlockSpec((tm, tn), lambda i, j: (i, j))
    return pl.pallas_call(
        body, name=name, grid=(t // tm, nj),
        in_specs=[pl.BlockSpec((tm, D), lambda i, j: (i, 0)), pl.BlockSpec((tm, SSD_INNER), lambda i, j: (i, 0)),
                  pl.BlockSpec((tm, tn), lambda i, j: (i, G0 // tn + j)),
                  pl.BlockSpec((tm, tn), lambda i, j: (i, (G0 + D) // tn + j)),
                  pl.BlockSpec((D, tn), lambda i, j: (0, j)), pl.BlockSpec((SSD_INNER, tn), lambda i, j: (0, j)),
                  pl.BlockSpec((1, tn), lambda i, j: (0, j)), pl.BlockSpec((1, tn), lambda i, j: (0, nj + j))],
        out_specs=[tile, tile, tile],
        out_shape=[jax.ShapeDtypeStruct((t, D), _MXU)] * 3,
        compiler_params=_cp(("parallel", "parallel")),
    )(ya, yb, proj, proj, wba, wbb, bgate, bgate)


def _swiglu_mm(gu, wfo, residual, *, tm, tn, name):
    t = gu.shape[0]

    def body(gu_ref, w_ref, r_ref, act_ref, o_ref):
        @pl.when(pl.program_id(1) == 0)
        def _():
            gate = gu_ref[:, 0:D_FF].astype(F32)
            act_ref[...] = (_silu(gate) * gu_ref[:, D_FF:2 * D_FF].astype(F32)).astype(act_ref.dtype)
        o_ref[...] = jnp.dot(act_ref[...], w_ref[...], preferred_element_type=F32) + r_ref[...]

    return pl.pallas_call(
        body, name=name, grid=(t // tm, D // tn),
        in_specs=[pl.BlockSpec((tm, 2 * D_FF), lambda i, j: (i, 0)), pl.BlockSpec((D_FF, tn), lambda i, j: (0, j)),
                  pl.BlockSpec((tm, tn), lambda i, j: (i, j))],
        out_specs=[pl.BlockSpec((tm, D_FF), lambda i, j: (i, 0)), pl.BlockSpec((tm, tn), lambda i, j: (i, j))],
        out_shape=[jax.ShapeDtypeStruct((t, D_FF), _MXU), jax.ShapeDtypeStruct((t, D), F32)],
        compiler_params=_cp(("parallel", "arbitrary")),
    )(gu, wfo, residual)


def _ffn_bwd_act(dh, wfo, gu, *, tm, name):
    t = dh.shape[0]

    def body(dh_ref, w_ref, gu_ref, o_ref):
        dact = _dot_nt(dh_ref[...], w_ref[...])
        g = gu_ref[:, 0:D_FF].astype(F32)
        u = gu_ref[:, D_FF:2 * D_FF].astype(F32)
        sg = _sigmoid(g)
        o_ref[:, 0:D_FF] = (dact * u * (sg * (1.0 + g * (1.0 - sg)))).astype(o_ref.dtype)
        o_ref[:, D_FF:2 * D_FF] = (dact * (g * sg)).astype(o_ref.dtype)

    return pl.pallas_call(
        body, name=name, grid=(t // tm,),
        in_specs=[pl.BlockSpec((tm, D), lambda i: (i, 0)), pl.BlockSpec((D_FF, D), lambda i: (0, 0)),
                  pl.BlockSpec((tm, 2 * D_FF), lambda i: (i, 0))],
        out_specs=pl.BlockSpec((tm, 2 * D_FF), lambda i: (i, 0)),
        out_shape=jax.ShapeDtypeStruct((t, 2 * D_FF), _MXU),
        compiler_params=_cp(("parallel",)),
    )(dh, wfo, gu)


def _outproj_bwd(dh, wout, ta, tb, proj, bgate, dproj, *, tm, name):
    t = dh.shape[0]

    def body(dh_ref, w_ref, ta_ref, tb_ref, g_ref, b_ref, dta_ref, dtb_ref, dg_ref, db_ref):
        @pl.when(pl.program_id(0) == 0)
        def _():
            db_ref[...] = jnp.zeros_like(db_ref)
        dm = _dot_nt(dh_ref[...], w_ref[...])
        ga = _sigmoid(g_ref[:, 0:D] + b_ref[:, 0:D])
        gb = _sigmoid(g_ref[:, D:2 * D] + b_ref[:, D:2 * D])
        dta_ref[...] = (dm * ga).astype(dta_ref.dtype)
        dtb_ref[...] = (dm * gb).astype(dtb_ref.dtype)
        dga = dm * ta_ref[...].astype(F32) * ga * (1.0 - ga)
        dgb = dm * tb_ref[...].astype(F32) * gb * (1.0 - gb)
        dg_ref[:, 0:D] = dga.astype(dg_ref.dtype)
        dg_ref[:, D:2 * D] = dgb.astype(dg_ref.dtype)
        db_ref[0:1, 0:D] += _rsum(dga)
        db_ref[0:1, D:2 * D] += _rsum(dgb)

    row = lambda cols: pl.BlockSpec((tm, cols), lambda i: (i, 0))
    return pl.pallas_call(
        body, name=name, grid=(t // tm,),
        in_specs=[row(D), pl.BlockSpec((D, D), lambda i: (0, 0)), row(D), row(D),
                  pl.BlockSpec((tm, 2 * D), lambda i: (i, G0 // (2 * D))), pl.BlockSpec((1, 2 * D), lambda i: (0, 0))],
        out_specs=[row(D), row(D), pl.BlockSpec((tm, 2 * D), lambda i: (i, G0 // (2 * D))),
                   pl.BlockSpec((8, 2 * D), lambda i: (0, 0))],
        out_shape=[jax.ShapeDtypeStruct((t, D), _MXU), jax.ShapeDtypeStruct((t, D), _MXU),
                   jax.ShapeDtypeStruct(dproj, _MXU), jax.ShapeDtypeStruct((8, 2 * D), F32)],
        compiler_params=_cp(("arbitrary",)),
    )(dh, wout, ta, tb, proj, bgate)


def _loss_head(h, gf, target, *, tm, name):
    t = h.shape[0]

    def body(h_ref, g_ref, t_ref, loss_ref, dg_ref, dh_ref):
        @pl.when(pl.program_id(0) == 0)
        def _():
            loss_ref[...] = jnp.zeros_like(loss_ref)
            dg_ref[...] = jnp.zeros_like(dg_ref)
        x = h_ref[...]
        r = lax.rsqrt(jnp.mean(x * x, axis=-1, keepdims=True) + EPS)
        xh = x * r
        err = xh * g_ref[...] - t_ref[...]
        loss_ref[...] += 0.5 * jnp.sum(jnp.mean(err * err, axis=-1, keepdims=True), axis=0, keepdims=True)
        dy = err * (1.0 / D)
        dg_ref[0:1, :] += _rsum(dy * xh)
        dxh = dy * g_ref[...]
        dh_ref[...] = r * (dxh - xh * jnp.mean(dxh * xh, axis=-1, keepdims=True))

    row = pl.BlockSpec((tm, D), lambda i: (i, 0))
    return pl.pallas_call(
        body, name=name, grid=(t // tm,),
        in_specs=[row, pl.BlockSpec((1, D), lambda i: (0, 0)), row],
        out_specs=[pl.BlockSpec((8, 128), lambda i: (0, 0)), pl.BlockSpec((8, D), lambda i: (0, 0)), row],
        out_shape=[jax.ShapeDtypeStruct((8, 128), F32), jax.ShapeDtypeStruct((8, D), F32), jax.ShapeDtypeStruct((t, D), F32)],
        compiler_params=_cp(("arbitrary",)),
    )(h, gf, target)


def _row_tile(rows, cols, limit_bytes=1 << 20):
    best = None
    for tr in range(8, rows + 1, 8):
        if rows % tr == 0 and tr * cols * 4 <= limit_bytes:
            best = tr
    return best if best is not None else rows


def _adamw(w, g, m, v, *, name):
    rows, cols = w.shape
    tr = _row_tile(rows, cols)

    def body(w_ref, g_ref, m_ref, v_ref, d_ref, nm_ref, nv_ref):
        gv = g_ref[...]
        nm = ADAM_B1 * m_ref[...] + (1.0 - ADAM_B1) * gv
        nv = ADAM_B2 * v_ref[...] + (1.0 - ADAM_B2) * (gv * gv)
        m_hat = nm / (1.0 - ADAM_B1 ** ADAM_STEP)
        v_hat = nv / (1.0 - ADAM_B2 ** ADAM_STEP)
        d_ref[...] = -ADAM_LR * (m_hat / (jnp.sqrt(v_hat) + ADAM_EPS) + ADAM_WD * w_ref[...])
        nm_ref[...] = nm
        nv_ref[...] = nv

    blk = pl.BlockSpec((tr, cols), lambda i: (i, 0))
    shp = jax.ShapeDtypeStruct((rows, cols), F32)
    return pl.pallas_call(
        body, name=name, grid=(rows // tr,), in_specs=[blk] * 4, out_specs=[blk] * 3, out_shape=[shp] * 3,
        compiler_params=_cp(("parallel",)),
    )(w, g, m, v)


def _bd256(w):
    w4 = w.reshape(4, 4, 64, 64)
    eye = jnp.eye(4, dtype=w.dtype)
    return (w4[:, :, :, None, :] * eye[None, :, None, :, None]).reshape(4, 256, 256)


def _bd256_diag(g):
    g5 = g.reshape(4, 4, 64, 4, 64)
    return jnp.stack([g5[:, a, :, a, :] for a in range(4)], axis=1).reshape(16, 64, 64)


FFN_SHARD = 2 * D_FF // 4
W_IN_SHARD = IN_DIM // 4
W_IN_ROWS = 9344


def _w_in_cols(shards, c0, c1):
    out = []
    for p in range(4):
        lo, hi = max(c0, W_IN_SHARD * p), min(c1, W_IN_SHARD * (p + 1))
        if lo < hi:
            out.append(shards[p][:, lo - W_IN_SHARD * p:hi - W_IN_SHARD * p])
    return out


def _in_proj_weights(win):
    lblk = [_w_in_cols(win, 256 * j, 256 * (j + 1)) + _w_in_cols(win, D + 256 * j, D + 256 * (j + 1)) for j in range(4)]
    wp = jnp.concatenate(_w_in_cols(win, 2048, 4096) + _w_in_cols(win, 4096, 7168) + lblk[0] + lblk[1]
                         + _w_in_cols(win, 7200, 9248) + lblk[2] + lblk[3], axis=1)
    wdt = jnp.pad(jnp.concatenate(_w_in_cols(win, 7168, 7200), axis=1), ((0, 0), (0, DT_PAD - N_HEADS)))
    return wp, wdt


def _layer_weights(w, conv, small, l, wp, wdt):
    row = lambda v: v.reshape(1, -1)
    pad_h = lambda v: jnp.pad(v.reshape(1, -1), ((0, 0), (0, DT_PAD - N_HEADS)))
    lw = dict(cw=conv["lru_conv_w"][l], cb=row(small["lru_conv_b"][l]),
              wa=_bd256(small["lru_w_a"][l]).astype(_MXU), wx=_bd256(small["lru_w_x"][l]).astype(_MXU),
              ba=row(small["lru_b_a"][l]), bx=row(small["lru_b_x"][l]), lam=row(small["lru_lambda"][l]))
    sw = dict(cw=conv["ssd_conv_w"][l], cb=row(small["ssd_conv_b"][l]), dtb=pad_h(small["ssd_dt_bias"][l]),
              alog=pad_h(small["ssd_A_log"][l]), dsk=row(jnp.repeat(small["ssd_D"][l], HEAD_P)),
              ng=row(small["ssd_norm_g"][l]))
    return dict(wp=wp, wdt=wdt, lw=lw, sw=sw, wba=w["w_branch"][0:D], wbb=w["w_branch"][D:3 * D],
                wout=w["w_out"], wfi=w["w_ffn_in"], wfo=w["w_ffn_out"],
                g1=row(small["norm1_g"][l]), g2=row(small["norm2_g"][l]), bgate=row(small["b_gate"][l]))


def _tiles(t):
    return dict(tmi=min(2048, t), tmn=min(1024, t), tm=min(512, t), r=min(256, t), rb=min(128, t))


def _layer_fwd(h, w, conv, small, l, carried=None):
    tl = _tiles(h.shape[0])
    n = f"l{l}_"
    carried = carried or {}
    arrived = []

    def carry(kernel, key, n_main, *args, **kw):
        comm, finish = carried.get(key, (None, None))
        outs = list(kernel(*args, comm=comm, **kw))
        if comm is not None:
            arrived.append(finish(outs[n_main:]))
        return outs[:n_main]

    wp, wdt = _in_proj_weights(w["w_in"])
    xn, proj = carry(_norm_mm, "in_proj", 2, h, small["norm1_g"][l].reshape(1, -1), wp, tm=tl["tmi"], tn=1024,
                     name=n + "in_proj")
    w = dict(w)
    for layer, ws in arrived:
        if layer == l:
            w.update(ws)
    lwt = _layer_weights(w, conv, small, l, wp, wdt)
    dtraw = _mm_nn(xn, lwt["wdt"], tm=tl["tm"], tn=DT_PAD, name=n + "dt_proj")
    hl, ya, gates = carry(_lru_fwd, "lru", 3, proj, lwt["lw"], r=tl["r"], name=n + "lru_fwd")
    yssd, yb, states, xc, dsl = carry(_ssd_fwd, "ssd", 5, proj, dtraw, lwt["sw"], rb=tl["rb"], name=n + "ssd_fwd")
    ta, tb, merged = _branch_merge(ya, yb, proj, lwt["wba"], lwt["wbb"], lwt["bgate"], tm=tl["tmn"], tn=D, name=n + "merge")
    hmid = _mm_nn(merged, lwt["wout"], tm=tl["tmn"], tn=D, name=n + "out_proj", residual=h)
    xn2, gu = _norm_mm(hmid, lwt["g2"], lwt["wfi"], tm=tl["tmi"], tn=FFN_SHARD, name=n + "ffn_in", out_dtype=_MXU)
    act, hout = _swiglu_mm(gu, lwt["wfo"], hmid, tm=tl["tm"], tn=D, name=n + "ffn_out")
    saved = dict(h=h, xn=xn, proj=proj, dtraw=dtraw, hl=hl, ya=ya, gates=gates, yssd=yssd, yb=yb, states=states, xc=xc, dsl=dsl, ta=ta, tb=tb,
                 merged=merged, hmid=hmid, xn2=xn2, gu=gu, act=act)
    return hout, saved, lwt, [x for x in arrived if x[0] != l]


def _layer_bwd(dh, s, lwt, l, hooks=None):
    t = dh.shape[0]
    tl = _tiles(t)
    n = f"l{l}_"
    tt = tl["tmn"]
    big = {}
    hooks = hooks or {}

    def wgrad(key, a, b, name, **kw):
        big[key] = _wgrad(a, b, tt=tt, name=n + name, into=big.get(key), **kw)

    dgu = _ffn_bwd_act(dh, lwt["wfo"], s["gu"], tm=tl["tm"], name=n + "ffn_act_bwd")
    wgrad("w_ffn_out", s["act"], dh, "ffn_out_wgrad", ta=D_FF, tn=1024, out_shape=(D_FF, D),
          out_block=(D_FF, 1024), out_index=lambda o, j: (o, j))
    wgrad("w_ffn_in", s["xn2"], dgu, "ffn_in_wgrad", ta=D, tn=FFN_SHARD, out_shape=(4, D, FFN_SHARD),
          out_block=(None, D, FFN_SHARD), out_index=lambda o, j: (j, o, 0))
    dh1, dg2 = _mm_nt_rmsbwd(dgu, lwt["wfi"], s["hmid"], lwt["g2"], dh, tm=tl["tmn"], tk=FFN_SHARD, name=n + "ffn_in_dgrad")
    dta, dtb, dproj, dbg = _outproj_bwd(dh1, lwt["wout"], s["ta"], s["tb"], s["proj"], lwt["bgate"], (t, NP),
                                        tm=tl["tm"], name=n + "out_proj_bwd")
    rows_d = dict(ta=D, tn=D, out_block=(D, D), out_index=lambda o, j: (o, j))
    wgrad("w_out", s["merged"], dh1, "out_proj_wgrad", out_shape=(D, D), **rows_d)
    dya = _mm_nt(dta, lwt["wba"], tm=tl["tm"], name=n + "branch_a_dgrad")
    dyb = _mm_nt(dtb, lwt["wbb"], tm=tl["tm"], name=n + "branch_b_dgrad")
    wgrad("w_branch", s["ya"], dta, "branch_a_wgrad", out_shape=(3 * D, D), a_tab=[0], o_tab=[0], **rows_d)
    wgrad("w_branch", s["yb"], dtb, "branch_b_wgrad", out_shape=(3 * D, D), a_tab=[0, 1], o_tab=[1, 2], **rows_d)
    comm_1 = hooks["lru"](big) if "lru" in hooks else None
    dproj, lsm, dwa, dwx, *got_1 = _lru_bwd(s["proj"], s["hl"], s["gates"], dya, dproj, lwt["lw"], r=tl["r"], name=n + "lru_bwd",
                                            comm=comm_1)
    comm_2 = hooks["ssd"](got_1) if "ssd" in hooks else None
    dproj, ddt, gconv, gch, ghd, *got_2 = _ssd_bwd(s["proj"], s["dtraw"], s["yssd"], s["states"], s["xc"], s["dsl"], dyb, dproj, lwt["sw"],
                                                   rb=tl["rb"], name=n + "ssd_bwd", comm=comm_2)
    lsm = lsm.reshape(8, 8, D).sum(axis=1)
    gconv = gconv.reshape(5, 8, XBC).sum(axis=1)
    gch = gch.reshape(2, 8, SSD_INNER).sum(axis=1)
    w_in = dict(tn=D, out_shape=(W_IN_ROWS, D), out_index=lambda o, j: (o, j))
    wgrad("w_in", dproj, s["xn"], "in_proj_wgrad", ta=1024, out_block=(1024, D),
          a_tab=list(range(9)), o_tab=[2, 3, 4, 5, 6, 0, 7, 8, 1], **w_in)
    wgrad("w_in", ddt, s["xn"], "dt_proj_wgrad", ta=DT_PAD, out_block=(DT_PAD, D), a_tab=[0],
          o_tab=[NP // DT_PAD], **w_in)
    comm_3 = hooks["in_dgrad"](big) if "in_dgrad" in hooks else None
    dh0, dg1, *got_3 = _mm_nt_rmsbwd(dproj, lwt["wp"], s["h"], lwt["g1"], dh1, tm=tl["tmn"], tk=2304,
                                     name=n + "in_proj_dgrad", extra=(ddt, lwt["wdt"]), comm=comm_3)
    grads = dict(
        lru_conv_w=lsm[0:4], lru_conv_b=lsm[4], lru_b_a=lsm[5], lru_b_x=lsm[6], lru_lambda=lsm[7],
        lru_w_a=_bd256_diag(dwa), lru_w_x=_bd256_diag(dwx),
        ssd_conv_w=gconv[0:4], ssd_conv_b=gconv[4], ssd_norm_g=gch[0], ssd_D=gch[1].reshape(N_HEADS, HEAD_P).sum(axis=-1),
        ssd_dt_bias=ghd[0, 0:N_HEADS], ssd_A_log=ghd[1, 0:N_HEADS],
        b_gate=dbg[0], norm1_g=dg1[0], norm2_g=dg2[0])
    return dh0, grads, big, (got_2, got_3)


def _local_step(x, target, w, conv, small, prefetch=None, early_reduce=None):
    h = x
    w = [dict(wl) for wl in w]
    lwts, saved = [], []
    for l in range(N_LAYERS):
        h, s, lwt, arrived = _layer_fwd(h, w[l], conv, small, l, prefetch if l == 0 else None)
        for layer, ws in arrived:
            w[layer].update(ws)
        lwts.append(lwt)
        saved.append(s)
    loss_blk, dgf, dh = _loss_head(h, small["norm_f"].reshape(1, D), target, tm=_tiles(x.shape[0])["tm"], name="loss_head")
    per_layer, big, carried = [None] * N_LAYERS, [None] * N_LAYERS, None
    for l in reversed(range(N_LAYERS)):
        hooks = early_reduce(big[1]) if (early_reduce is not None and l == 0) else None
        dh, per_layer[l], big[l], carried = _layer_bwd(dh, saved[l], lwts[l], l, hooks)
    grads = {k: jnp.stack([per_layer[l][k] for l in range(N_LAYERS)], axis=0) for k in per_layer[0]}
    grads["norm_f"] = dgf[0]
    return loss_blk, dh, grads, big, carried


PACK_W = 1024
BIG = (("w_in", W_IN_SHARD, D, W_IN_SHARD, 256), ("w_branch", 768, D, 256, D), ("w_out", 256, D, 256, D),
       ("w_ffn_in", D, FFN_SHARD, 256, FFN_SHARD), ("w_ffn_out", 704, D, 352, D))
CONV = ("lru_conv_w", "ssd_conv_w")
SMALL = ("norm1_g", "b_gate", "lru_conv_b", "lru_w_a", "lru_b_a", "lru_w_x", "lru_b_x", "lru_lambda", "ssd_conv_b",
         "ssd_dt_bias", "ssd_A_log", "ssd_D", "ssd_norm_g", "norm2_g", "norm_f")
_WIRE = jnp.bfloat16
N_CHIPS = 4
N_DEV = 8


def _mesh_pos():
    return lax.axis_index("x"), lax.axis_index("y"), lax.axis_index("c")


HBM_SPEC = pl.BlockSpec(memory_space=pltpu.HBM)


def _remote(src, dst, send_sems, recv_sems, k, to):
    return pltpu.make_async_remote_copy(src_ref=src, dst_ref=dst, send_sem=send_sems.at[k], recv_sem=recv_sems.at[k],
                                        device_id=to, device_id_type=MESH)


def _other_chips(x, y):
    return [(1 - x, y), (x, 1 - y), (1 - x, 1 - y)]


def _weight_fetch(loc, layer, owner):
    names = list(owner)
    rows = {n: loc[n].shape[1] for n in names}
    by_chip = ("w_in", "w_ffn_in")
    shapes = [((N_CHIPS,) + loc[n].shape[1:]) if n in by_chip else (N_CHIPS * rows[n], D) for n in names]

    def place(o_ref, n, chip):
        if n in by_chip:
            return o_ref.at[chip]
        return o_ref.at[pl.ds(pl.multiple_of(chip * rows[n], 16), rows[n]), :]

    def step(which, in_refs, o_refs, send_sems, recv_sems):
        x, y, c = _mesh_pos()
        s = 2 * x + y
        sib = (x, y, 1 - c)
        chips = _other_chips(x, y)
        for core in (0, 1):
            @pl.when(c == core)
            def _():
                for k, n in enumerate(names):
                    for j, (px, py) in enumerate(chips):
                        landed = place(o_refs[k], n, 2 * px + py)
                        sent = _remote(in_refs[k].at[layer], place(o_refs[k], n, s), send_sems, recv_sems, 3 * k + j,
                                       (px, py, c))
                        arrives = _remote(in_refs[k].at[layer], landed, send_sems, recv_sems, 3 * k + j, (px, py, c))
                        passed = _remote(landed, landed, send_sems, recv_sems, 3 * (len(names) + k) + j, sib)
                        if owner[n] == core:
                            if which == "start":
                                sent.start()
                            elif which == "mid":
                                arrives.wait_recv()
                                passed.start()
                            else:
                                sent.wait_send()
                                passed.wait_send()
                        elif which == "end":
                            passed.wait_recv()

    return dict(inputs=[loc[n] for n in names], names=names,
                out_shapes=[jax.ShapeDtypeStruct(shp, loc[n].dtype) for shp, n in zip(shapes, names)],
                sems=[pltpu.SemaphoreType.DMA((6 * len(names),)), pltpu.SemaphoreType.DMA((6 * len(names),))],
                start=functools.partial(step, "start"), mid=functools.partial(step, "mid"),
                end=functools.partial(step, "end"))


def _comm_now(comm, name):
    n, no = len(comm["inputs"]), len(comm["out_shapes"])

    def body(*refs):
        parts = (refs[:n], refs[n:n + no]) + tuple(refs[n + no:])
        comm["start"](*parts)
        comm["mid"](*parts)
        comm["end"](*parts)

    return pl.pallas_call(
        body, name=name, in_specs=[HBM_SPEC] * n, out_specs=[HBM_SPEC] * no, out_shape=comm["out_shapes"],
        scratch_shapes=comm["sems"],
    )(*comm["inputs"])


def _sibling_send(bufs, layer):
    n = len(bufs)

    def step(which, in_refs, o_refs, send_sems, recv_sems):
        x, y, c = _mesh_pos()
        copies = [_remote(in_refs[k], o_refs[k], send_sems, recv_sems, k, (x, y, 1 - c)) for k in range(n)]

        @pl.when(c != layer)
        def _():
            for cp in copies:
                if which == "start":
                    cp.start()
                elif which == "end":
                    cp.wait_send()

        @pl.when(c == layer)
        def _():
            for cp in copies:
                if which == "end":
                    cp.wait_recv()

    return dict(inputs=list(bufs), out_shapes=[jax.ShapeDtypeStruct(b.shape, b.dtype) for b in bufs],
                sems=[pltpu.SemaphoreType.DMA((n,)), pltpu.SemaphoreType.DMA((n,))],
                start=functools.partial(step, "start"), mid=functools.partial(step, "mid"),
                end=functools.partial(step, "end"))


def _add_cast(g, recv, own, *, a, tr, tc, name):
    wd = g.shape[1]
    nr = a // tr

    def body(own_ref, g_ref, r_ref, o_ref):
        @pl.when(own_ref[0] == 1)
        def _():
            o_ref[...] = (g_ref[...] + r_ref[...]).astype(o_ref.dtype)

    blk = pl.BlockSpec((tr, tc), lambda p, i, j, own_ref: ((p * nr + i) * own_ref[0], j * own_ref[0]))
    return pl.pallas_call(
        body, name=name,
        grid_spec=pltpu.PrefetchScalarGridSpec(
            num_scalar_prefetch=1, grid=(N_CHIPS, nr, wd // tc), in_specs=[blk, blk],
            out_specs=pl.BlockSpec((None, tr, tc), lambda p, i, j, own_ref: (p * own_ref[0], i * own_ref[0], j * own_ref[0]))),
        out_shape=jax.ShapeDtypeStruct((N_CHIPS, a, wd), _WIRE),
        compiler_params=_cp(("arbitrary", "arbitrary", "arbitrary")),
    )(own, g, recv)


def _chip_exchange(parts, layer):
    n = len(parts)

    def step(which, s_refs, o_refs, send_sems, recv_sems):
        x, y, c = _mesh_pos()
        s = 2 * x + y

        @pl.when(c == layer)
        def _():
            for j, (px, py) in enumerate(_other_chips(x, y)):
                for k in range(n):
                    p = 2 * px + py
                    sent = _remote(s_refs[k].at[p], o_refs[k].at[s], send_sems, recv_sems, n * j + k, (px, py, c))
                    if which == "start":
                        sent.start()
                    elif which == "end":
                        _remote(s_refs[k].at[p], o_refs[k].at[p], send_sems, recv_sems, n * j + k, (px, py, c)).wait_recv()
                        sent.wait_send()

    return dict(inputs=list(parts), out_shapes=[jax.ShapeDtypeStruct(p.shape, p.dtype) for p in parts],
                sems=[pltpu.SemaphoreType.DMA((3 * n,)), pltpu.SemaphoreType.DMA((3 * n,))],
                start=functools.partial(step, "start"), mid=functools.partial(step, "mid"),
                end=functools.partial(step, "end"))


def _sum_slots(slots, own, sel, *, tr, tc, name, layer=None, into=None):
    n, rows, wd = slots.shape
    k = own.shape[0]

    def body(sel_ref, s_ref, own_ref, *rest):
        o_ref = rest[-1]

        @pl.when(sel_ref[1] == 1)
        def _():
            mine = sel_ref[0]
            acc = jnp.zeros((tr, tc), F32)
            for p in range(n):
                acc = acc + jnp.where(mine == p, own_ref[...].astype(F32), s_ref[p].astype(F32))
            o_ref[...] = acc

    if layer is not None:
        out_spec = pl.BlockSpec((None, tr, tc), lambda i, j, sel_ref: (layer, i * sel_ref[1], j * sel_ref[1]))
        out_shape = jax.ShapeDtypeStruct((N_LAYERS, rows, wd), F32)
    else:
        out_spec = pl.BlockSpec((tr, tc), lambda i, j, sel_ref: (i * sel_ref[1], j * sel_ref[1]))
        out_shape = jax.ShapeDtypeStruct((rows, wd), F32)
    in_specs = [pl.BlockSpec((n, tr, tc), lambda i, j, sel_ref: (0, i * sel_ref[1], j * sel_ref[1])),
                pl.BlockSpec((None, tr, tc), lambda i, j, sel_ref: (sel_ref[0] if k > 1 else 0, i * sel_ref[1],
                                                                    j * sel_ref[1]))]
    args = [sel, slots, own]
    if into is not None:
        in_specs.append(pl.BlockSpec(memory_space=pl.ANY))
        args.append(into)
    return pl.pallas_call(
        body, name=name,
        grid_spec=pltpu.PrefetchScalarGridSpec(num_scalar_prefetch=1, grid=(rows // tr, wd // tc), in_specs=in_specs,
                                               out_specs=out_spec),
        out_shape=out_shape, input_output_aliases={3: 0} if into is not None else {},
        compiler_params=_cp(("arbitrary", "arbitrary")),
    )(*args)


def _sibling_share(both):
    n = len(both)

    def body(*refs):
        o_refs, (send_sems, recv_sems) = refs[n:2 * n], refs[2 * n:]
        x, y, c = _mesh_pos()
        sends = [_remote(o_refs[k].at[c], o_refs[k].at[c], send_sems, recv_sems, k, (x, y, 1 - c)) for k in range(n)]
        for cp in sends:
            cp.start()
        for k in range(n):
            _remote(o_refs[k].at[1 - c], o_refs[k].at[1 - c], send_sems, recv_sems, k, (x, y, 1 - c)).wait_recv()
        for cp in sends:
            cp.wait_send()

    return pl.pallas_call(
        body, name="grad_sibling_share", in_specs=[HBM_SPEC] * n, out_specs=[HBM_SPEC] * n,
        out_shape=[jax.ShapeDtypeStruct(b.shape, b.dtype) for b in both], input_output_aliases={k: k for k in range(n)},
        scratch_shapes=[pltpu.SemaphoreType.DMA((n,)), pltpu.SemaphoreType.DMA((n,))],
    )(*both)


def _allgather_devices(part):
    rows, wd = part.shape

    def step(which, in_refs, o_refs, send_sems, recv_sems):
        (p_ref,), (o_ref,) = in_refs, o_refs
        x, y, c = _mesh_pos()
        sib = (x, y, 1 - c)
        chips = _other_chips(x, y)
        slot = lambda px, py, pc: o_ref.at[4 * px + 2 * py + pc]
        first = [_remote(p_ref, slot(x, y, c), send_sems, recv_sems, 0, sib)]
        first += [_remote(p_ref, slot(x, y, c), send_sems, recv_sems, 1 + j, (px, py, c)) for j, (px, py) in enumerate(chips)]
        passed = [_remote(slot(px, py, c), slot(px, py, c), send_sems, recv_sems, 4 + j, sib)
                  for j, (px, py) in enumerate(chips)]
        if which == "start":
            for cp in first:
                cp.start()
        elif which == "mid":
            for j, (px, py) in enumerate(chips):
                _remote(p_ref, slot(px, py, c), send_sems, recv_sems, 1 + j, (px, py, c)).wait_recv()
                passed[j].start()
        else:
            _remote(p_ref, slot(x, y, 1 - c), send_sems, recv_sems, 0, sib).wait_recv()
            for j, (px, py) in enumerate(chips):
                _remote(slot(px, py, 1 - c), slot(px, py, 1 - c), send_sems, recv_sems, 4 + j, sib).wait_recv()
            for cp in first + passed:
                cp.wait_send()

    return dict(inputs=[part], out_shapes=[jax.ShapeDtypeStruct((N_DEV, rows, wd), part.dtype)],
                sems=[pltpu.SemaphoreType.DMA((N_DEV - 1,)), pltpu.SemaphoreType.DMA((N_DEV - 1,))],
                start=functools.partial(step, "start"), mid=functools.partial(step, "mid"),
                end=functools.partial(step, "end"))


def _comm_both(a, b):
    na, nao = len(a["inputs"]), len(a["out_shapes"])

    def step(which, in_refs, o_refs, sa, ra, sb, rb_):
        a[which](in_refs[:na], o_refs[:nao], sa, ra)
        b[which](in_refs[na:], o_refs[nao:], sb, rb_)

    return dict(inputs=a["inputs"] + b["inputs"], out_shapes=a["out_shapes"] + b["out_shapes"], sems=a["sems"] + b["sems"],
                start=functools.partial(step, "start"), mid=functools.partial(step, "mid"),
                end=functools.partial(step, "end"))


def _by_chip_to_full(stack):
    _, nl, r, b = stack.shape
    return stack.transpose(1, 2, 0, 3).reshape(nl, r, N_CHIPS * b)


def _sharded_step(a):
    x = a["x"][0]
    target = a["loss_target"][0]
    cx, cy, cc = _mesh_pos()
    chip = (2 * cx + cy).astype(jnp.int32)
    core = cc.astype(jnp.int32)
    me = (4 * cx + 2 * cy + cc).astype(jnp.int32)
    zero = jnp.zeros((), jnp.int32)
    dus = lax.dynamic_update_slice

    loc = {n: a[n].astype(_MXU) for n, *_ in BIG}

    def with_own(got, names, layer):
        out = {}
        for g, n in zip(got, names):
            mine = loc[n][layer]
            out[n] = (dus(g, mine[None], (chip, zero, zero)) if g.ndim == 3 else dus(g, mine, (chip * mine.shape[0], zero)))
        return out

    rest = {"w_ffn_in": 0, "w_branch": 1, "w_out": 1, "w_ffn_out": 1}
    conv_loc = jnp.concatenate([a[n].reshape(-1, PACK_W) for n in CONV], axis=0)
    now = _weight_fetch(loc, 0, {"w_in": 0})
    conv_all, *got_now = _comm_now(_comm_both(_allgather_devices(conv_loc), now), "allgather_weights")
    w0 = with_own(got_now, now["names"], 0)
    later = {"in_proj": (0, _weight_fetch(loc, 0, rest)), "lru": (1, _weight_fetch(loc, 1, {"w_in": 0})),
             "ssd": (1, _weight_fetch(loc, 1, rest))}
    prefetch = {k: (f, functools.partial(lambda got, layer, f: (layer, with_own(got, f["names"], layer)), layer=layer, f=f))
                for k, (layer, f) in later.items()}
    conv_all = dus(conv_all, conv_loc[None], (me, zero, zero))[0::2]
    conv, off = {}, 0
    for n in CONV:
        rows = a[n].size // PACK_W
        conv[n] = _by_chip_to_full(conv_all[:, off:off + rows].reshape((N_CHIPS,) + a[n].shape))
        off += rows
    small = {n: a[n] for n in SMALL}

    views = lambda big_l, specs: [big_l[n].reshape(-1, wd) for n, _, wd, _, _ in specs]
    owns = lambda layer: (core == layer).astype(jnp.int32)
    w_in_only, others = BIG[:1], BIG[1:]

    def partial_sums(big_l, recv, layer, specs):
        return [_add_cast(v, r, owns(layer).reshape(1), a=rows, tr=tr, tc=tc, name=f"grad_add_sibling_l{layer}_{n}")
                for v, r, (n, rows, _, tr, tc) in zip(views(big_l, specs), recv, specs)]

    def reduced(slots, parts, layer, into, specs):
        sel = jnp.stack([chip, owns(layer)])
        return [_sum_slots(s, p, sel, tr=tr, tc=tc, name=f"grad_sum_chips_l{layer}_{n}", layer=layer, into=buf)
                for s, p, buf, (n, _, _, tr, tc) in zip(slots, parts, into, specs)]

    kept = {}

    def early_reduce(big_1):
        def during_lru(big_0):
            kept["big_0"] = dict(big_0)
            return _comm_both(_sibling_send(views(big_1, BIG), 1), _sibling_send(views(big_0, others), 0))

        def during_ssd(recv):
            kept["parts_1"] = partial_sums(big_1, recv[:len(BIG)], 1, BIG)
            kept["parts_0"] = partial_sums(kept["big_0"], recv[len(BIG):], 0, others)
            return _comm_both(_chip_exchange(kept["parts_1"], 1), _chip_exchange(kept["parts_0"], 0))

        return dict(lru=during_lru, ssd=during_ssd, in_dgrad=lambda big_0: _sibling_send(views(big_0, w_in_only), 0))

    loss_blk, grad_x, grads, big, (slots, recv_in) = _local_step(x, target, [w0, {}], conv, small, prefetch, early_reduce)
    loss = lax.psum(loss_blk[0, 0], ("x", "y", "c"))
    both = reduced(slots[:len(BIG)], kept["parts_1"], 1, [None] * len(BIG), BIG)
    both[1:] = reduced(slots[len(BIG):], kept["parts_0"], 0, both[1:], others)
    parts_in = partial_sums(big[0], recv_in, 0, w_in_only)
    names = SMALL + CONV
    srows = -(-sum(grads[n].size for n in names) // (8 * PACK_W)) * 8
    flat = lambda d, ns: jnp.concatenate([d[n].reshape(-1) for n in ns])
    padto = lambda v: jnp.pad(v, (0, srows * PACK_W - v.shape[0])).reshape(srows, PACK_W)
    g_own = padto(flat(grads, names))
    g_all, *slots_in = _comm_now(_comm_both(_allgather_devices(g_own), _chip_exchange(parts_in, 0)), "grad_chip_exchange")
    both[:1] = reduced(slots_in, parts_in, 0, both[:1], w_in_only)
    done = dict(zip([n for n, *_ in BIG], _sibling_share(both)))
    g_big = {n: done[n].reshape(a[n].shape) for n in ("w_branch", "w_out", "w_ffn_in", "w_ffn_out")}
    gt = done["w_in"].transpose(0, 2, 1)
    first = jnp.concatenate([gt[..., 512 * j + 256 * part:512 * j + 256 * (part + 1)] for part in range(2) for j in range(4)]
                            + [gt[..., 2 * D:]], axis=-1)
    tail = W_IN_SHARD - (IN_DIM - 7168)
    last = jnp.concatenate([gt[..., :tail], gt[..., W_IN_SHARD - N_HEADS:], gt[..., tail:W_IN_SHARD - N_HEADS]], axis=-1)
    g_big["w_in"] = jnp.where(chip == 0, first, jnp.where(chip == N_CHIPS - 1, last, gt))

    g_sum = _sum_slots(g_all, g_own[None], jnp.stack([me, zero + 1]), tr=srows, tc=PACK_W, name="small_grad_sum")
    off, g_small = 0, {}
    for n in names:
        g_small[n] = g_sum.reshape(-1)[off:off + grads[n].size].reshape(grads[n].shape)
        off += grads[n].size
    for n in CONV:
        width = a[n].shape[2]
        g_big[n] = lax.dynamic_slice(g_small.pop(n), (zero, zero, chip * width), a[n].shape)

    out_g, out_d, out_m, out_v = {}, {}, {}, {}
    for n in g_big:
        shp = a[n].shape
        two_d = (shp[0] * shp[1], shp[2])
        d_, m_, v_ = _adamw(a[n].reshape(two_d), g_big[n].reshape(two_d), a["m_" + n].reshape(two_d),
                            a["v_" + n].reshape(two_d), name="adamw_" + n)
        out_g[n], out_d[n], out_m[n], out_v[n] = g_big[n], d_.reshape(shp), m_.reshape(shp), v_.reshape(shp)
    d_, m_, v_ = _adamw(padto(flat(a, SMALL)), padto(flat(g_small, SMALL)), padto(flat({n: a["m_" + n] for n in SMALL}, SMALL)),
                        padto(flat({n: a["v_" + n] for n in SMALL}, SMALL)), name="adamw_small")
    off = 0
    for n in SMALL:
        cut = lambda v: v.reshape(-1)[off:off + a[n].size].reshape(a[n].shape)
        out_g[n], out_d[n], out_m[n], out_v[n] = g_small[n], cut(d_), cut(m_), cut(v_)
        off += a[n].size
    return loss, grad_x[None], out_g, out_d, out_m, out_v


WEIGHTS = ("norm1_g", "w_in", "b_gate", "lru_conv_w", "lru_conv_b", "lru_w_a", "lru_b_a", "lru_w_x", "lru_b_x", "lru_lambda",
           "ssd_conv_w", "ssd_conv_b", "ssd_dt_bias", "ssd_A_log", "ssd_D", "ssd_norm_g", "w_branch", "w_out", "norm2_g",
           "w_ffn_in", "w_ffn_out", "norm_f")
INPUTS = ("x",) + WEIGHTS + ("loss_target",) + tuple("m_" + n for n in WEIGHTS) + tuple("v_" + n for n in WEIGHTS)


def kernel(x, norm1_g, w_in, b_gate, lru_conv_w, lru_conv_b, lru_w_a, lru_b_a, lru_w_x, lru_b_x, lru_lambda, ssd_conv_w, ssd_conv_b, ssd_dt_bias, ssd_A_log, ssd_D, ssd_norm_g, w_branch, w_out, norm2_g, w_ffn_in, w_ffn_out, norm_f, loss_target, m_norm1_g, m_w_in, m_b_gate, m_lru_conv_w, m_lru_conv_b, m_lru_w_a, m_lru_b_a, m_lru_w_x, m_lru_b_x, m_lru_lambda, m_ssd_conv_w, m_ssd_conv_b, m_ssd_dt_bias, m_ssd_A_log, m_ssd_D, m_ssd_norm_g, m_w_branch, m_w_out, m_norm2_g, m_w_ffn_in, m_w_ffn_out, m_norm_f, v_norm1_g, v_w_in, v_b_gate, v_lru_conv_w, v_lru_conv_b, v_lru_w_a, v_lru_b_a, v_lru_w_x, v_lru_b_x, v_lru_lambda, v_ssd_conv_w, v_ssd_conv_b, v_ssd_dt_bias, v_ssd_A_log, v_ssd_D, v_ssd_norm_g, v_w_branch, v_w_out, v_norm2_g, v_w_ffn_in, v_w_ffn_out, v_norm_f):
    args = (x, norm1_g, w_in, b_gate, lru_conv_w, lru_conv_b, lru_w_a, lru_b_a, lru_w_x, lru_b_x, lru_lambda, ssd_conv_w, ssd_conv_b, ssd_dt_bias, ssd_A_log, ssd_D, ssd_norm_g, w_branch, w_out, norm2_g, w_ffn_in, w_ffn_out, norm_f, loss_target, m_norm1_g, m_w_in, m_b_gate, m_lru_conv_w, m_lru_conv_b, m_lru_w_a, m_lru_b_a, m_lru_w_x, m_lru_b_x, m_lru_lambda, m_ssd_conv_w, m_ssd_conv_b, m_ssd_dt_bias, m_ssd_A_log, m_ssd_D, m_ssd_norm_g, m_w_branch, m_w_out, m_norm2_g, m_w_ffn_in, m_w_ffn_out, m_norm_f, v_norm1_g, v_w_in, v_b_gate, v_lru_conv_w, v_lru_conv_b, v_lru_w_a, v_lru_b_a, v_lru_w_x, v_lru_b_x, v_lru_lambda, v_ssd_conv_w, v_ssd_conv_b, v_ssd_dt_bias, v_ssd_A_log, v_ssd_D, v_ssd_norm_g, v_w_branch, v_w_out, v_norm2_g, v_w_ffn_in, v_w_ffn_out, v_norm_f)
    assert len(args) == len(INPUTS)
    loss, grad_x, g, d, m, v = _sharded_step(dict(zip(INPUTS, args)))
    return (loss, grad_x, *[g[n] for n in WEIGHTS], *[d[n] for n in WEIGHTS], *[m[n] for n in WEIGHTS],
            *[v[n] for n in WEIGHTS])
```

```python
import functools
import math

import numpy as np
import jax
import jax.numpy as jnp
from jax import lax
from jax.experimental import pallas as pl
from jax.experimental.pallas import tpu as pltpu

F32 = jnp.float32
_MXU = jnp.bfloat16
_HI = lax.Precision.HIGHEST

D = 1024
EPS = 1e-6
N_LAYERS = 2
LRU_C = 8.0
N_HEADS = 32
HEAD_P = 64
N_GROUPS = 4
N_STATE = 128
SSD_INNER = 2048
XBC = 3072
D_FF = 2816
CHUNK = 64
NORM_ROWS = 32
IN_DIM = 9248

NP = 9216
ZX_W = 5120
G0 = 6144
LBLK = 512
DT_PAD = 128

VMEM_LIMIT_BYTES_V7X = 56 * 1024 * 1024

ADAM_LR, ADAM_B1, ADAM_B2, ADAM_EPS, ADAM_WD, ADAM_STEP = 0.001, 0.9, 0.999, 1e-08, 0.01, 10
MESH = pl.DeviceIdType.MESH


def _cp(sem):
    return pltpu.CompilerParams(dimension_semantics=sem, vmem_limit_bytes=VMEM_LIMIT_BYTES_V7X)


def _lblk_col(j):
    return 10 + j + 4 * (j // 2)


def _sigmoid(x):
    return 0.5 * jnp.tanh(0.5 * x) + 0.5


def _softplus(x):
    return jnp.maximum(x, 0.0) + jnp.log(1.0 + jnp.exp(-jnp.abs(x)))


def _silu(x):
    return x * _sigmoid(x)


_GELU_C0 = math.sqrt(2.0 / math.pi)
_GELU_C1 = 0.044715


def _gelu_and_grad(x):
    t = jnp.tanh(_GELU_C0 * (x + _GELU_C1 * x * x * x))
    g = 0.5 * x * (1.0 + t)
    dg = 0.5 * (1.0 + t) + 0.5 * x * (1.0 - t * t) * _GELU_C0 * (1.0 + 3.0 * _GELU_C1 * x * x)
    return g, dg


def _one_minus_exp(x):
    p = 1.0 + x * (1.0 / 7.0)
    p = 1.0 + x * (1.0 / 6.0) * p
    p = 1.0 + x * (1.0 / 5.0) * p
    p = 1.0 + x * (1.0 / 4.0) * p
    p = 1.0 + x * (1.0 / 3.0) * p
    p = 1.0 + x * (1.0 / 2.0) * p
    return jnp.where(x > -0.3, -x * p, 1.0 - jnp.exp(x))


def _dot(a, b):
    return jnp.dot(a.astype(_MXU), b.astype(_MXU), preferred_element_type=F32)


def _dot_nt(a, b):
    return lax.dot_general(a.astype(_MXU), b.astype(_MXU), (((1,), (1,)), ((), ())), preferred_element_type=F32)


def _dot_tn(a, b):
    return lax.dot_general(a.astype(_MXU), b.astype(_MXU), (((0,), (0,)), ((), ())), preferred_element_type=F32)


def _shift_down(x, prev8, k):
    xr = pltpu.roll(x, k, 0)
    pr = pltpu.roll(prev8, k, 0)
    row = lax.broadcasted_iota(jnp.int32, prev8.shape, 0)
    head = jnp.where(row < k, pr, xr[0:8])
    return jnp.concatenate([head, xr[8:]], axis=0)


def _shift_up(x, next8, k):
    r = x.shape[0]
    xr = pltpu.roll(x, r - k, 0)
    nr = pltpu.roll(next8, 8 - k, 0)
    row = lax.broadcasted_iota(jnp.int32, next8.shape, 0)
    tail = jnp.where(row >= 8 - k, nr, xr[r - 8:r])
    return jnp.concatenate([xr[:r - 8], tail], axis=0)


def _conv4(x, prev8, w_ref, b_ref, cols=slice(None)):
    acc = x * w_ref[3:4, cols] + b_ref[0:1, cols]
    for k in (1, 2, 3):
        acc = acc + _shift_down(x, prev8, k) * w_ref[3 - k:4 - k, cols]
    return acc


def _lin_scan(a, b, reverse):
    r = a.shape[0]
    row = lax.broadcasted_iota(jnp.int32, a.shape, 0)
    d = 1
    while d < r:
        sh = (r - d) if reverse else d
        a_s = pltpu.roll(a, sh, 0)
        b_s = pltpu.roll(b, sh, 0)
        m = (row < r - d) if reverse else (row >= d)
        b = jnp.where(m, a * b_s + b, b)
        a = jnp.where(m, a * a_s, a)
        d *= 2
    return a, b


def _rsum(x):
    return jnp.sum(x, axis=0, keepdims=True)


def _comm_specs(comm):
    if comm is None:
        return [], [], [], [], []
    n = len(comm["inputs"])
    return list(comm["inputs"]), [HBM_SPEC] * n, [HBM_SPEC] * len(comm["out_shapes"]), list(comm["out_shapes"]), comm["sems"]


def _comm_steps(comm, refs, n_in, n_out, first, mid, last):
    ni, no = len(comm["inputs"]), len(comm["out_shapes"])
    parts = (refs[n_in:n_in + ni], refs[n_out:n_out + no]) + tuple(refs[len(refs) - len(comm["sems"]):])
    for when, what in ((first, "start"), (mid, "mid"), (last, "end")):
        @pl.when(when)
        def _():
            comm[what](*parts)


def _norm_mm(h, gamma, w, *, tm, tn, name, out_dtype=F32, comm=None):
    m, k = h.shape
    if w.ndim == 3:
        assert w.shape[2] == tn
        n = w.shape[0] * tn
        w_spec = pl.BlockSpec((None, k, tn), lambda i, j: (j, 0, 0))
    else:
        n = w.shape[1]
        w_spec = pl.BlockSpec((k, tn), lambda i, j: (0, j))

    c_args, c_in_specs, c_out_specs, c_out_shapes, c_sems = _comm_specs(comm)
    ni, nj = m // tm, n // tn

    def body(*refs):
        h_ref, g_ref, w_ref = refs[:3]
        xn_ref, o_ref = refs[3 + len(c_args):5 + len(c_args)]
        i, j = pl.program_id(0), pl.program_id(1)
        if comm is not None:
            _comm_steps(comm, refs, 3, 5 + len(c_args), (i == 0) & (j == 0), (i == (3 * ni) // 4) & (j == 0),
                        (i == ni - 1) & (j == nj - 1))

        @pl.when(j == 0)
        def _():
            x = h_ref[...]
            r = lax.rsqrt(jnp.mean(x * x, axis=-1, keepdims=True) + EPS)
            xn_ref[...] = ((x * r) * g_ref[...]).astype(xn_ref.dtype)
        o_ref[...] = jnp.dot(xn_ref[...], w_ref[...], preferred_element_type=F32).astype(o_ref.dtype)

    return pl.pallas_call(
        body, name=name, grid=(ni, nj),
        in_specs=[pl.BlockSpec((tm, k), lambda i, j: (i, 0)), pl.BlockSpec((1, k), lambda i, j: (0, 0)), w_spec] + c_in_specs,
        out_specs=[pl.BlockSpec((tm, k), lambda i, j: (i, 0)), pl.BlockSpec((tm, tn), lambda i, j: (i, j))] + c_out_specs,
        out_shape=[jax.ShapeDtypeStruct((m, k), _MXU), jax.ShapeDtypeStruct((m, n), out_dtype)] + c_out_shapes,
        scratch_shapes=c_sems,
        compiler_params=_cp(("arbitrary", "arbitrary") if comm is not None else ("parallel", "arbitrary")),
    )(h, gamma, w, *c_args)


def _mm_nn(a, w, *, tm, tn, name, residual=None):
    m, k = a.shape
    n = w.shape[1]

    def body(*refs):
        if residual is None:
            a_ref, w_ref, o_ref = refs
            o_ref[...] = _dot(a_ref[...], w_ref[...])
        else:
            a_ref, w_ref, r_ref, o_ref = refs
            o_ref[...] = _dot(a_ref[...], w_ref[...]) + r_ref[...]

    in_specs = [pl.BlockSpec((tm, k), lambda i, j: (i, 0)), pl.BlockSpec((k, tn), lambda i, j: (0, j))]
    args = [a, w]
    if residual is not None:
        in_specs.append(pl.BlockSpec((tm, tn), lambda i, j: (i, j)))
        args.append(residual)
    return pl.pallas_call(
        body, name=name, grid=(m // tm, n // tn), in_specs=in_specs,
        out_specs=pl.BlockSpec((tm, tn), lambda i, j: (i, j)),
        out_shape=jax.ShapeDtypeStruct((m, n), F32),
        compiler_params=_cp(("parallel", "parallel")),
    )(*args)


def _wgrad(a, b, *, tt, ta, tn, name, out_shape, out_block, out_index, a_tab=None, o_tab=None, into=None):
    t = a.shape[0]
    a_tab = list(range(a.shape[1] // ta)) if a_tab is None else a_tab
    o_tab = a_tab if o_tab is None else o_tab
    nb = b.shape[1] // tn

    def body(at_ref, ot_ref, a_ref, b_ref, *rest):
        del at_ref, ot_ref
        o_ref = rest[-1]

        @pl.when(pl.program_id(2) == 0)
        def _():
            o_ref[...] = jnp.zeros_like(o_ref)
        o_ref[...] += _dot_tn(a_ref[...], b_ref[...])

    in_specs = [pl.BlockSpec((tt, ta), lambda r, j, i, at, ot: (i, at[r])),
                pl.BlockSpec((tt, tn), lambda r, j, i, at, ot: (i, j))]
    args = [jnp.asarray(a_tab, jnp.int32), jnp.asarray(o_tab, jnp.int32), a, b]
    aliases = {}
    if into is not None:
        in_specs.append(pl.BlockSpec(memory_space=pl.ANY))
        args.append(into)
        aliases = {4: 0}
    return pl.pallas_call(
        body, name=name,
        grid_spec=pltpu.PrefetchScalarGridSpec(
            num_scalar_prefetch=2, grid=(len(a_tab), nb, t // tt), in_specs=in_specs,
            out_specs=pl.BlockSpec(out_block, lambda r, j, i, at, ot: out_index(ot[r], j))),
        out_shape=jax.ShapeDtypeStruct(out_shape, F32), input_output_aliases=aliases,
        compiler_params=_cp(("parallel", "parallel", "arbitrary")),
    )(*args)


def _mm_nt(a, w, *, tm, name):
    m, kc = a.shape
    n = w.shape[0]

    def body(a_ref, w_ref, o_ref):
        o_ref[...] = _dot_nt(a_ref[...], w_ref[...])

    return pl.pallas_call(
        body, name=name, grid=(m // tm,),
        in_specs=[pl.BlockSpec((tm, kc), lambda i: (i, 0)), pl.BlockSpec((n, kc), lambda i: (0, 0))],
        out_specs=pl.BlockSpec((tm, n), lambda i: (i, 0)),
        out_shape=jax.ShapeDtypeStruct((m, n), F32),
        compiler_params=_cp(("parallel",)),
    )(a, w)


def _mm_nt_rmsbwd(dy, w, x, gamma, dres, *, tm, tk, name, extra=None, comm=None):
    m, kc = dy.shape
    nk = kc // tk
    ni = m // tm
    n_x = 5 if extra is None else 7
    c_args, c_in_specs, c_out_specs, c_out_shapes, c_sems = _comm_specs(comm)
    if w.ndim == 3:
        assert w.shape[0] == nk and w.shape[2] == tk
        d = w.shape[1]
        w_spec = pl.BlockSpec((None, d, tk), lambda i, k: (k, 0, 0))
    else:
        d = w.shape[0]
        w_spec = pl.BlockSpec((d, tk), lambda i, k: (0, k))

    def body(*refs):
        dy_ref, w_ref, x_ref, g_ref, r_ref = refs[:5]
        if extra is not None:
            dy2_ref, w2_ref = refs[5:7]
        n_out = n_x + len(c_args)
        dx_ref, dg_ref = refs[n_out:n_out + 2]
        acc_ref = refs[n_out + 2 + len(c_out_shapes)]
        i, kk = pl.program_id(0), pl.program_id(1)
        if comm is not None:
            _comm_steps(comm, refs, n_x, n_out + 2, (i == 0) & (kk == 0), (i == (3 * ni) // 4) & (kk == 0),
                        (i == ni - 1) & (kk == nk - 1))

        @pl.when(kk == 0)
        def _():
            acc_ref[...] = jnp.zeros_like(acc_ref)

        @pl.when((i == 0) & (kk == 0))
        def _():
            dg_ref[...] = jnp.zeros_like(dg_ref)

        acc_ref[...] += _dot_nt(dy_ref[...], w_ref[...])

        @pl.when(kk == nk - 1)
        def _():
            dxn = acc_ref[...]
            if extra is not None:
                dxn = dxn + _dot_nt(dy2_ref[...], w2_ref[...])
            xv = x_ref[...]
            r = lax.rsqrt(jnp.mean(xv * xv, axis=-1, keepdims=True) + EPS)
            xh = xv * r
            dg_ref[0:1, :] += _rsum(dxn * xh)
            dxh = dxn * g_ref[...]
            dx_ref[...] = r_ref[...] + r * (dxh - xh * jnp.mean(dxh * xh, axis=-1, keepdims=True))

    in_specs = [pl.BlockSpec((tm, tk), lambda i, k: (i, k)), w_spec,
                pl.BlockSpec((tm, d), lambda i, k: (i, 0)), pl.BlockSpec((1, d), lambda i, k: (0, 0)),
                pl.BlockSpec((tm, d), lambda i, k: (i, 0))]
    args = [dy, w, x, gamma, dres]
    if extra is not None:
        k2 = extra[0].shape[1]
        in_specs += [pl.BlockSpec((tm, k2), lambda i, k: (i, 0)), pl.BlockSpec((d, k2), lambda i, k: (0, 0))]
        args += list(extra)
    return pl.pallas_call(
        body, name=name, grid=(ni, nk), in_specs=in_specs + c_in_specs,
        out_specs=[pl.BlockSpec((tm, d), lambda i, k: (i, 0)), pl.BlockSpec((8, d), lambda i, k: (0, 0))] + c_out_specs,
        out_shape=[jax.ShapeDtypeStruct((m, d), F32), jax.ShapeDtypeStruct((8, d), F32)] + c_out_shapes,
        scratch_shapes=[pltpu.VMEM((tm, d), F32)] + c_sems,
        compiler_params=_cp(("arbitrary", "arbitrary")),
    )(*args, *c_args)


def _rsum8(x):
    acc = x[0:8]
    for g in range(1, x.shape[0] // 8):
        acc = acc + x[8 * g:8 * (g + 1)]
    return acc


def _lru_gates(x, prev8, cw_ref, cb_ref, wa_ref, wx_ref, ba_ref, bx_ref, lam_ref):
    u = _conv4(x, prev8, cw_ref, cb_ref)
    ra =_sigmoid(_dot(u, wa_ref[0]) + ba_ref[...])
    ia = _sigmoid(_dot(u, wx_ref[0]) + bx_ref[...])
    sp = _softplus(-lam_ref[...])
    log_a = -LRU_C * ra * sp
    a = jnp.exp(log_a)
    m2 = _one_minus_exp(2.0 * log_a)
    mult = jnp.sqrt(m2)
    return u, ra, ia, sp, a, m2, mult


def _lru_fwd(proj, lw, *, r, name, comm=None):
    t = proj.shape[0]
    nt = t // r
    c_args, c_in_specs, c_out_specs, c_out_shapes, c_sems = _comm_specs(comm)

    def body(*refs):
        xg_ref, xp_ref, cw_ref, cb_ref, wa_ref, wx_ref, ba_ref, bx_ref, lam_ref = refs[:9]
        hl_ref, ya_ref, sv_ref = refs[9 + len(c_args):12 + len(c_args)]
        carry_ref = refs[12 + len(c_args) + len(c_out_shapes)]
        i = pl.program_id(1)
        if comm is not None:
            j = pl.program_id(0)
            _comm_steps(comm, refs, 9, 12 + len(c_args), (j == 0) & (i == 0), (j == 3) & (i == 0), (j == 3) & (i == nt - 1))

        @pl.when(i == 0)
        def _():
            carry_ref[...] = jnp.zeros_like(carry_ref)

        x = xg_ref[:, 0:256]
        lg = xg_ref[:, 256:512]
        prev8 = jnp.where(i == 0, 0.0, xp_ref[:, 0:256])
        u, ra, ia, sp, a, m2, mult = _lru_gates(x, prev8, cw_ref, cb_ref, wa_ref, wx_ref, ba_ref, bx_ref, lam_ref)
        for k, v in enumerate((u, ra, ia, a, mult)):
            sv_ref[k] = v
        ac, hc = _lin_scan(a, mult * ia * u, False)
        h = hc + ac * carry_ref[0:1, :]
        hl_ref[...] = h
        carry_ref[0:1, :] = hl_ref[r - 1:r, :]
        g, _ = _gelu_and_grad(lg)
        ya_ref[...] = (g * h).astype(ya_ref.dtype)

    small = lambda rows: pl.BlockSpec((rows, 256), lambda j, i: (0, j))
    return pl.pallas_call(
        body, name=name, grid=(4, nt),
        in_specs=[pl.BlockSpec((r, LBLK), lambda j, i: (i, _lblk_col(j))),
                  pl.BlockSpec((8, LBLK), lambda j, i: (jnp.maximum(i * (r // 8) - 1, 0), _lblk_col(j))),
                  small(4), small(1),
                  pl.BlockSpec((1, 256, 256), lambda j, i: (j, 0, 0)), pl.BlockSpec((1, 256, 256), lambda j, i: (j, 0, 0)),
                  small(1), small(1), small(1)] + c_in_specs,
        out_specs=[pl.BlockSpec((r, 256), lambda j, i: (i, j)), pl.BlockSpec((r, 256), lambda j, i: (i, j)),
                   pl.BlockSpec((5, r, 256), lambda j, i: (0, i, j))] + c_out_specs,
        out_shape=[jax.ShapeDtypeStruct((t, D), F32), jax.ShapeDtypeStruct((t, D), _MXU),
                   jax.ShapeDtypeStruct((5, t, D), F32)] + c_out_shapes,
        scratch_shapes=[pltpu.VMEM((8, 256), F32)] + c_sems,
        compiler_params=_cp(("arbitrary", "arbitrary") if comm is not None else ("parallel", "arbitrary")),
    )(proj, proj, lw["cw"], lw["cb"], lw["wa"], lw["wx"], lw["ba"], lw["bx"], lw["lam"], *c_args)


def _lru_bwd(proj, hl, gates, dya, dproj, lw, *, r, name, comm=None):
    t = proj.shape[0]
    nt = t // r
    c_args, c_in_specs, c_out_specs, c_out_shapes, c_sems = _comm_specs(comm)
    n_in = 10

    def body(*refs):
        xg_ref, hl_ref, hp_ref, sv_ref, dya_ref, cw_ref, wa_ref, wx_ref, lam_ref = refs[:9]
        n_out = n_in + len(c_args)
        dproj_ref, sm_ref, dwa_ref, dwx_ref = refs[n_out:n_out + 4]
        n_scr = n_out + 4 + len(c_out_shapes)
        carry_ref, du8_ref, row_scr = refs[n_scr:n_scr + 3]
        i = pl.program_id(1)
        if comm is not None:
            j = pl.program_id(0)
            _comm_steps(comm, refs, n_in, n_out + 4, (j == 0) & (i == 0), (j == 3) & (i == 0), (j == 3) & (i == nt - 1))

        @pl.when(i == 0)
        def _():
            carry_ref[...] = jnp.zeros_like(carry_ref)
            du8_ref[...] = jnp.zeros_like(du8_ref)
            sm_ref[...] = jnp.zeros_like(sm_ref)
            dwa_ref[...] = jnp.zeros_like(dwa_ref)
            dwx_ref[...] = jnp.zeros_like(dwx_ref)

        tile0 = i == nt - 1
        xp = xg_ref[:, 0:256]
        lg = xg_ref[:, 256:512]
        u, ra, ia, a, mult = (sv_ref[k] for k in range(5))
        sp = _softplus(-lam_ref[...])
        h = hl_ref[...]
        hprev = _shift_down(h, jnp.where(tile0, 0.0, hp_ref[...]), 1)
        dya_v = dya_ref[...]
        g, dg = _gelu_and_grad(lg)
        ac, lc = _lin_scan(_shift_up(a, carry_ref[...], 1), dya_v * g, True)
        lam_v = lc + ac * carry_ref[1:2, :]
        row_scr[0:8, :] = lam_v[0:8]
        row_scr[8:16, :] = a[0:8]
        carry_ref[1:2, :] = row_scr[0:1, :]
        carry_ref[0:1, :] = row_scr[8:9, :]
        da = lam_v * hprev
        dmult = lam_v * ia * u
        dia = lam_v * mult * u
        dlog = da * a - dmult * (a * a) / mult
        dra = -LRU_C * sp * dlog
        dpa = dra * ra * (1.0 - ra)
        dpx = dia * ia * (1.0 - ia)
        du = lam_v * mult * ia + _dot_nt(dpa, wa_ref[0]) + _dot_nt(dpx, wx_ref[0])
        dwa_ref[0] += _dot_tn(u, dpa)
        dwx_ref[0] += _dot_tn(u, dpx)
        dlx = du * cw_ref[3:4, :]
        sm_ref[24:32, :] += _rsum8(du * xp)
        for k in (1, 2, 3):
            du_k = _shift_up(du, du8_ref[...], k)
            dlx = dlx + du_k * cw_ref[3 - k:4 - k, :]
            sm_ref[8 * (3 - k):8 * (4 - k), :] += _rsum8(du_k * xp)
        du8_ref[...] = du[0:8]
        dproj_ref[:, 0:256] = dlx.astype(dproj_ref.dtype)
        dproj_ref[:, 256:512] = (dya_v * h * dg).astype(dproj_ref.dtype)
        sm_ref[32:40, :] += _rsum8(du)
        sm_ref[40:48, :] += _rsum8(dpa)
        sm_ref[48:56, :] += _rsum8(dpx)
        sm_ref[56:64, :] += _rsum8(-LRU_C * ra * dlog) * (-_sigmoid(-lam_ref[...]))

    rev = lambda i: nt - 1 - i
    small = lambda rows: pl.BlockSpec((rows, 256), lambda j, i: (0, j))
    wblk = pl.BlockSpec((1, 256, 256), lambda j, i: (j, 0, 0))
    return pl.pallas_call(
        body, name=name, grid=(4, nt),
        in_specs=[pl.BlockSpec((r, LBLK), lambda j, i: (rev(i), _lblk_col(j))),
                  pl.BlockSpec((r, 256), lambda j, i: (rev(i), j)),
                  pl.BlockSpec((8, 256), lambda j, i: (jnp.maximum(rev(i) * (r // 8) - 1, 0), j)),
                  pl.BlockSpec((5, r, 256), lambda j, i: (0, rev(i), j)),
                  pl.BlockSpec((r, 256), lambda j, i: (rev(i), j)),
                  small(4), wblk, wblk, small(1),
                  pl.BlockSpec(memory_space=pl.ANY)] + c_in_specs,
        out_specs=[pl.BlockSpec((r, LBLK), lambda j, i: (rev(i), _lblk_col(j))),
                   pl.BlockSpec((64, 256), lambda j, i: (0, j)), wblk, wblk] + c_out_specs,
        out_shape=[jax.ShapeDtypeStruct(dproj.shape, dproj.dtype), jax.ShapeDtypeStruct((64, D), F32),
                   jax.ShapeDtypeStruct((4, 256, 256), F32), jax.ShapeDtypeStruct((4, 256, 256), F32)] + c_out_shapes,
        scratch_shapes=[pltpu.VMEM((8, 256), F32), pltpu.VMEM((8, 256), F32), pltpu.VMEM((16, 256), F32)] + c_sems,
        input_output_aliases={n_in - 1: 0},
        compiler_params=_cp(("arbitrary", "arbitrary") if comm is not None else ("parallel", "arbitrary")),
    )(proj, hl, hl, gates, dya, lw["cw"], lw["wa"], lw["wx"], lw["lam"], dproj, *c_args)


def _head_cols(x):
    lane = lax.broadcasted_iota(jnp.int32, x.shape, 1)
    return [jnp.sum(jnp.where(lane == h, x, 0.0), axis=1, keepdims=True) for h in range(N_HEADS)]


def _compact_heads(blocks):
    lane = lax.broadcasted_iota(jnp.int32, blocks[0].shape, 1)
    lo = lane < HEAD_P
    out = jnp.zeros_like(blocks[0])
    for j, blk in enumerate(blocks):
        s_lo = jnp.sum(jnp.where(lo, blk, 0.0), axis=1, keepdims=True)
        s_hi = jnp.sum(jnp.where(lo, 0.0, blk), axis=1, keepdims=True)
        out = jnp.where(lane == 2 * j, s_lo, out)
        out = jnp.where(lane == 2 * j + 1, s_hi, out)
    return out


def _ssd_prelude(dtraw_ref, dtb_ref, alog_ref, dt_scr, a_scr):
    lane = lax.broadcasted_iota(jnp.int32, dt_scr.shape, 1)
    dt = jnp.where(lane < N_HEADS, _softplus(dtraw_ref[...] + dtb_ref[0:1, :]), 0.0)
    dt_scr[...] = dt
    a_scr[...] = dt * (-jnp.exp(alog_ref[0:1, :]))


def _ssd_chunk_scalars(dt_scr, a_scr, r_scr, r0):
    a_c = a_scr[pl.ds(r0, CHUNK), :]
    dt_c = dt_scr[pl.ds(r0, CHUNK), :]
    i0 = lax.broadcasted_iota(jnp.int32, (CHUNK, CHUNK), 0)
    i1 = lax.broadcasted_iota(jnp.int32, (CHUNK, CHUNK), 1)
    tri = jnp.where(i0 >= i1, 1.0, 0.0).astype(F32)
    cs = jnp.dot(tri, a_c, precision=_HI, preferred_element_type=F32)
    lane = lax.broadcasted_iota(jnp.int32, (CHUNK, 128), 1)
    srow = lax.broadcasted_iota(jnp.int32, (CHUNK, 128), 0)
    t_lo = jnp.where((lane < HEAD_P) & (srow <= lane), 1.0, 0.0).astype(F32)
    t_hi = jnp.where((lane >= HEAD_P) & (srow <= lane - HEAD_P), 1.0, 0.0).astype(F32)
    even = (lane % 2) == 0
    tn = (((0,), (0,)), ((), ()))
    r_scr[...] = (lax.dot_general(jnp.where(even, a_c, 0.0), t_lo, tn, precision=_HI, preferred_element_type=F32)
                  + lax.dot_general(jnp.where(even, 0.0, a_c), t_hi, tn, precision=_HI, preferred_element_type=F32))
    return cs, dt_c, _head_cols(cs), _head_cols(dt_c)


def _block_diag2(v):
    lo = lax.broadcasted_iota(jnp.int32, v.shape, 1) < HEAD_P
    return jnp.concatenate([jnp.where(lo, v, 0.0), jnp.where(lo, 0.0, v)], axis=0).astype(_MXU)


def _ssd_pair(xc_scr, r_scr, cs_cols, dt_cols, s2, r0, j, s2t=None):
    lane = lax.broadcasted_iota(jnp.int32, (CHUNK, 128), 1)
    srow = lax.broadcasted_iota(jnp.int32, (CHUNK, 128), 0)
    lo = lane < HEAD_P
    csc = jnp.where(lo, cs_cols[2 * j], cs_cols[2 * j + 1])
    dtc = jnp.where(lo, dt_cols[2 * j], dt_cols[2 * j + 1])
    csr = r_scr[2 * j:2 * j + 1, :] + r_scr[2 * j + 1:2 * j + 2, :]
    dm = jnp.where((lane & (HEAD_P - 1)) <= srow, jnp.exp(jnp.minimum(csc - csr, 0.0)), 0.0)
    xs = xc_scr[pl.ds(r0, CHUNK), j * 128:(j + 1) * 128]
    xd = xs * dtc
    csl = jnp.sum(jnp.where(srow == CHUNK - 1, csc, 0.0), axis=0, keepdims=True)
    out = dict(csc=csc, dtc=dtc, dm=dm, m2=s2 * dm, xs=xs, xd=xd, rhs=_block_diag2(xd), e=jnp.exp(csc),
               w=jnp.exp(csl - csc), dec=jnp.exp(csl))
    if s2t is not None:
        out["mt2"] = s2t * jnp.where((lane & (HEAD_P - 1)) >= srow, jnp.exp(jnp.minimum(csr - csc, 0.0)), 0.0)
    return out


def _cat(parts):
    return jnp.concatenate(parts, axis=1)


def _ssd_fwd(proj, dtraw, sw, *, rb, name, comm=None):
    t = proj.shape[0]
    ns, cb = t // rb, rb // CHUNK
    c_args, c_in_specs, c_out_specs, c_out_shapes, c_sems = _comm_specs(comm)

    def body(*refs):
        zx_ref, zp_ref, dtraw_ref, cw_ref, cbias_ref, dtb_ref, alog_ref, dsk_ref, ng_ref = refs[:9]
        yssd_ref, yb_ref, st_ref, xc_scr, dsl_ref = refs[9 + len(c_args):14 + len(c_args)]
        n_scr = 14 + len(c_args) + len(c_out_shapes)
        h_scr, dt_scr, a_scr, r_scr = refs[n_scr:n_scr + 4]
        i = pl.program_id(0)
        if comm is not None:
            _comm_steps(comm, refs, 9, 14 + len(c_args), i == 0, i == (3 * ns) // 4, i == ns - 1)

        @pl.when(i == 0)
        def _():
            h_scr[...] = jnp.zeros_like(h_scr)

        for j in range(XBC // 128):
            cs_, zc = slice(128 * j, 128 * (j + 1)), slice(2048 + 128 * j, 2048 + 128 * (j + 1))
            pre = _conv4(zx_ref[:, zc], jnp.where(i == 0, 0.0, zp_ref[:, zc]), cw_ref, cbias_ref, cs_)
            sg = _sigmoid(pre)
            xc_scr[:, cs_] = pre * sg
            dsl_ref[:, cs_] = sg * (1.0 + pre * (1.0 - sg))
        _ssd_prelude(dtraw_ref, dtb_ref, alog_ref, dt_scr, a_scr)

        def chunk(c, carry):
            r0 = pl.multiple_of(c * CHUNK, CHUNK)
            _, _, cs_cols, dt_cols = _ssd_chunk_scalars(dt_scr, a_scr, r_scr, r0)
            st_ref[c] = h_scr[...]
            for g in range(N_GROUPS):
                bg = xc_scr[pl.ds(r0, CHUNK), 2048 + 128 * g:2048 + 128 * (g + 1)]
                cg = xc_scr[pl.ds(r0, CHUNK), 2560 + 128 * g:2560 + 128 * (g + 1)]
                s2 = _dot_nt(cg, jnp.concatenate([bg, bg], axis=0))
                hp = h_scr[:, 512 * g:512 * (g + 1)]
                yoff = _dot(cg, hp)
                xdw, dec = [], []
                for jj in range(4):
                    j = 4 * g + jj
                    p = _ssd_pair(xc_scr, r_scr, cs_cols, dt_cols, s2, r0, j)
                    y = _dot(p["m2"], p["rhs"]) + yoff[:, 128 * jj:128 * (jj + 1)] * p["e"]
                    yssd_ref[pl.ds(r0, CHUNK), 128 * j:128 * (j + 1)] = y + dsk_ref[0:1, 128 * j:128 * (j + 1)] * p["xs"]
                    xdw.append(p["xd"] * p["w"])
                    dec.append(p["dec"])
                h_scr[:, 512 * g:512 * (g + 1)] = hp * _cat(dec) + _dot_tn(bg, _cat(xdw))
            return carry

        lax.fori_loop(0, cb, chunk, 0)
        for g in range(N_GROUPS):
            sl = slice(512 * g, 512 * (g + 1))
            for q in range(rb // NORM_ROWS):
                rw = slice(NORM_ROWS * q, NORM_ROWS * (q + 1))
                yz = yssd_ref[rw, sl] * _silu(zx_ref[rw, sl])
                rg = lax.rsqrt(jnp.mean(yz * yz, axis=-1, keepdims=True) + EPS)
                yb_ref[rw, sl] = (yz * rg * ng_ref[0:1, sl]).astype(yb_ref.dtype)

    full = lambda rows, cols: pl.BlockSpec((rows, cols), lambda i: (0, 0))
    return pl.pallas_call(
        body, name=name, grid=(ns,),
        in_specs=[pl.BlockSpec((rb, ZX_W), lambda i: (i, 0)),
                  pl.BlockSpec((8, ZX_W), lambda i: (jnp.maximum(i * (rb // 8) - 1, 0), 0)),
                  pl.BlockSpec((rb, DT_PAD), lambda i: (i, 0)),
                  full(4, XBC), full(1, XBC), full(1, DT_PAD), full(1, DT_PAD), full(1, SSD_INNER), full(1, SSD_INNER)]
        + c_in_specs,
        out_specs=[pl.BlockSpec((rb, SSD_INNER), lambda i: (i, 0)), pl.BlockSpec((rb, SSD_INNER), lambda i: (i, 0)),
                   pl.BlockSpec((cb, N_STATE, SSD_INNER), lambda i: (i, 0, 0)),
                   pl.BlockSpec((rb, XBC), lambda i: (i, 0)), pl.BlockSpec((rb, XBC), lambda i: (i, 0))] + c_out_specs,
        out_shape=[jax.ShapeDtypeStruct((t, SSD_INNER), F32), jax.ShapeDtypeStruct((t, SSD_INNER), _MXU),
                   jax.ShapeDtypeStruct((t // CHUNK, N_STATE, SSD_INNER), F32),
                   jax.ShapeDtypeStruct((t, XBC), F32), jax.ShapeDtypeStruct((t, XBC), F32)] + c_out_shapes,
        scratch_shapes=[pltpu.VMEM((N_STATE, SSD_INNER), F32), pltpu.VMEM((rb, DT_PAD), F32),
                        pltpu.VMEM((rb, DT_PAD), F32), pltpu.VMEM((128, 128), F32)] + c_sems,
        compiler_params=_cp(("arbitrary",)),
    )(proj, proj, dtraw, sw["cw"], sw["cb"], sw["dtb"], sw["alog"], sw["dsk"], sw["ng"], *c_args)


def _ssd_bwd(proj, dtraw, yssd, states, xc, dsl, dyb, dproj, sw, *, rb, name, comm=None):
    t = proj.shape[0]
    ns, cb = t // rb, rb // CHUNK
    c_args, c_in_specs, c_out_specs, c_out_shapes, c_sems = _comm_specs(comm)
    n_in = 13

    def body(*refs):
        zx_ref, dtraw_ref, yssd_ref, st_ref, xc_scr, dsl_scr, dyb_ref, cw_ref, dtb_ref, alog_ref, dsk_ref, ng_ref = refs[:12]
        n_out = n_in + len(c_args)
        dzx_ref, ddt_ref, gconv_ref, gch_ref, ghd_ref = refs[n_out:n_out + 5]
        n_scr = n_out + 5 + len(c_out_shapes)
        dht_scr, dy_scr, dxc_scr, dt_scr, a_scr, r_scr, dp8_scr = refs[n_scr:n_scr + 7]
        i = pl.program_id(0)
        if comm is not None:
            _comm_steps(comm, refs, n_in, n_out + 5, i == 0, i == (3 * ns) // 4, i == ns - 1)

        @pl.when(i == 0)
        def _():
            dht_scr[...] = jnp.zeros_like(dht_scr)
            dp8_scr[...] = jnp.zeros_like(dp8_scr)
            gconv_ref[...] = jnp.zeros_like(gconv_ref)
            gch_ref[...] = jnp.zeros_like(gch_ref)
            ghd_ref[...] = jnp.zeros_like(ghd_ref)

        _ssd_prelude(dtraw_ref, dtb_ref, alog_ref, dt_scr, a_scr)

        for g in range(N_GROUPS):
            sl = slice(512 * g, 512 * (g + 1))
            for q in range(rb // NORM_ROWS):
                rw = slice(NORM_ROWS * q, NORM_ROWS * (q + 1))
                zv = zx_ref[rw, sl]
                ys = yssd_ref[rw, sl]
                sg = _sigmoid(zv)
                sz = zv * sg
                yz = ys * sz
                rg = lax.rsqrt(jnp.mean(yz * yz, axis=-1, keepdims=True) + EPS)
                yn = yz * rg
                dyb_v = dyb_ref[rw, sl]
                gch_ref[0:8, sl] += _rsum8(dyb_v * yn)
                dyn = dyb_v * ng_ref[0:1, sl]
                dyz = rg * (dyn - yn * jnp.mean(dyn * yn, axis=-1, keepdims=True))
                dy_scr[rw, sl] = dyz * sz
                dzx_ref[rw, sl] = (dyz * ys * (sg * (1.0 + zv * (1.0 - sg)))).astype(dzx_ref.dtype)

        a_row = -jnp.exp(alog_ref[0:1, :])

        def chunk(cc, carry):
            c = cb - 1 - cc
            r0 = pl.multiple_of(c * CHUNK, CHUNK)
            rows = pl.ds(r0, CHUNK)
            _, dt_c, cs_cols, dt_cols = _ssd_chunk_scalars(dt_scr, a_scr, r_scr, r0)
            lane = lax.broadcasted_iota(jnp.int32, (CHUNK, 128), 1)
            srow = lax.broadcasted_iota(jnp.int32, (CHUNK, 128), 0)
            lo = lane < HEAD_P
            last = srow == CHUNK - 1
            p1_blocks, p3_blocks = [], []
            for g in range(N_GROUPS):
                gs = slice(512 * g, 512 * (g + 1))
                bg = xc_scr[rows, 2048 + 128 * g:2048 + 128 * (g + 1)]
                cg = xc_scr[rows, 2560 + 128 * g:2560 + 128 * (g + 1)]
                b2 = jnp.concatenate([bg, bg], axis=0)
                s2 = _dot_nt(cg, b2)
                s2t = _dot_nt(bg, jnp.concatenate([cg, cg], axis=0))
                hp = st_ref[c, :, gs]
                dht = dht_scr[:, gs]
                yoff = _dot(cg, hp)
                ps = [_ssd_pair(xc_scr, r_scr, cs_cols, dt_cols, s2, r0, 4 * g + jj, s2t) for jj in range(4)]
                dys = [dy_scr[rows, 128 * (4 * g + jj):128 * (4 * g + jj + 1)] for jj in range(4)]
                dye = _cat([dys[jj] * ps[jj]["e"] for jj in range(4)])
                w_g = _cat([p["w"] for p in ps])
                dcg = _dot_nt(dye, hp)
                dht_scr[:, gs] = _dot_tn(cg, dye) + _cat([p["dec"] for p in ps]) * dht
                dxd_state = w_g * _dot(bg, dht)
                dbg = _dot_nt(_cat([p["xd"] for p in ps]) * w_g, dht)
                tsum = _rsum(dht * hp)
                ds2 = jnp.zeros((CHUNK, 128), F32)
                for jj in range(4):
                    j = 4 * g + jj
                    ls = slice(128 * j, 128 * (j + 1))
                    p, dy2 = ps[jj], dys[jj]
                    dy_bd = _block_diag2(dy2)
                    dm2 = _dot_nt(dy2, p["rhs"])
                    ds2 = ds2 + dm2 * p["dm"]
                    gdiff = dm2 * p["m2"] - _dot_nt(p["xd"], dy_bd) * p["mt2"]
                    dxs = dxd_state[:, 128 * jj:128 * (jj + 1)]
                    dxd = _dot(p["mt2"], dy_bd) + dxs
                    end_row = _rsum(p["xd"] * dxs) + p["dec"] * tsum[:, 128 * jj:128 * (jj + 1)]
                    p1_blocks.append(gdiff + dy2 * yoff[:, 128 * jj:128 * (jj + 1)] * p["e"] - p["xd"] * dxs
                                     + jnp.where(last, end_row, 0.0))
                    p3_blocks.append(dxd * p["xs"])
                    dxc_scr[rows, ls] = dxd * p["dtc"] + dy2 * dsk_ref[0:1, ls]
                    gch_ref[8:16, ls] += _rsum8(dy2 * p["xs"])
                dcg = dcg + _dot(ds2, b2)
                rb2 = _dot_tn(ds2, cg)
                dxc_scr[rows, 2048 + 128 * g:2048 + 128 * (g + 1)] = dbg + rb2[0:CHUNK] + rb2[CHUNK:2 * CHUNK]
                dxc_scr[rows, 2560 + 128 * g:2560 + 128 * (g + 1)] = dcg
            dcs = _compact_heads(p1_blocks)
            i0 = lax.broadcasted_iota(jnp.int32, (CHUNK, CHUNK), 0)
            i1 = lax.broadcasted_iota(jnp.int32, (CHUNK, CHUNK), 1)
            triu = jnp.where(i1 >= i0, 1.0, 0.0).astype(F32)
            da = jnp.dot(triu, dcs, precision=_HI, preferred_element_type=F32)
            ddt = _compact_heads(p3_blocks) + da * a_row
            ddtraw = jnp.where(lane < N_HEADS, ddt * _sigmoid(dtraw_ref[rows, :] + dtb_ref[0:1, :]), 0.0)
            ddt_ref[rows, :] = ddtraw.astype(ddt_ref.dtype)
            ghd_ref[0:1, :] += _rsum(ddtraw)
            ghd_ref[1:2, :] += _rsum(da * dt_c) * a_row
            return carry

        lax.fori_loop(0, cb, chunk, 0)
        for j in range(XBC // 128):
            cs_, zc = slice(128 * j, 128 * (j + 1)), slice(2048 + 128 * j, 2048 + 128 * (j + 1))
            dpre = dxc_scr[:, cs_] * dsl_scr[:, cs_]
            xraw = zx_ref[:, zc]
            dx = dpre * cw_ref[3:4, cs_]
            gconv_ref[24:32, cs_] += _rsum8(dpre * xraw)
            for k in (1, 2, 3):
                dpre_k = _shift_up(dpre, dp8_scr[:, cs_], k)
                dx = dx + dpre_k * cw_ref[3 - k:4 - k, cs_]
                gconv_ref[8 * (3 - k):8 * (4 - k), cs_] += _rsum8(dpre_k * xraw)
            dzx_ref[:, zc] = dx.astype(dzx_ref.dtype)
            dp8_scr[:, cs_] = dpre[0:8]
            gconv_ref[32:40, cs_] += _rsum8(dpre)

    rev = lambda i: ns - 1 - i
    full = lambda rows, cols: pl.BlockSpec((rows, cols), lambda i: (0, 0))
    return pl.pallas_call(
        body, name=name, grid=(ns,),
        in_specs=[pl.BlockSpec((rb, ZX_W), lambda i: (rev(i), 0)),
                  pl.BlockSpec((rb, DT_PAD), lambda i: (rev(i), 0)),
                  pl.BlockSpec((rb, SSD_INNER), lambda i: (rev(i), 0)),
                  pl.BlockSpec((cb, N_STATE, SSD_INNER), lambda i: (rev(i), 0, 0)),
                  pl.BlockSpec((rb, XBC), lambda i: (rev(i), 0)), pl.BlockSpec((rb, XBC), lambda i: (rev(i), 0)),
                  pl.BlockSpec((rb, SSD_INNER), lambda i: (rev(i), 0)),
                  full(4, XBC), full(1, DT_PAD), full(1, DT_PAD), full(1, SSD_INNER), full(1, SSD_INNER),
                  pl.BlockSpec(memory_space=pl.ANY)] + c_in_specs,
        out_specs=[pl.BlockSpec((rb, ZX_W), lambda i: (rev(i), 0)), pl.BlockSpec((rb, DT_PAD), lambda i: (rev(i), 0)),
                   full(40, XBC), full(16, SSD_INNER), full(8, DT_PAD)] + c_out_specs,
        out_shape=[jax.ShapeDtypeStruct(dproj.shape, dproj.dtype), jax.ShapeDtypeStruct((t, DT_PAD), _MXU),
                   jax.ShapeDtypeStruct((40, XBC), F32), jax.ShapeDtypeStruct((16, SSD_INNER), F32),
                   jax.ShapeDtypeStruct((8, DT_PAD), F32)] + c_out_shapes,
        scratch_shapes=[pltpu.VMEM((N_STATE, SSD_INNER), F32),
                        pltpu.VMEM((rb, SSD_INNER), F32), pltpu.VMEM((rb, XBC), F32), pltpu.VMEM((rb, DT_PAD), F32),
                        pltpu.VMEM((rb, DT_PAD), F32), pltpu.VMEM((128, 128), F32), pltpu.VMEM((8, XBC), F32)] + c_sems,
        input_output_aliases={n_in - 1: 0},
        compiler_params=_cp(("arbitrary",)),
    )(proj, dtraw, yssd, states, xc, dsl, dyb, sw["cw"], sw["dtb"], sw["alog"], sw["dsk"], sw["ng"], dproj, *c_args)


def _branch_merge(ya, yb, proj, wba, wbb, bgate, *, tm, tn, name):
    t = ya.shape[0]
    nj = D // tn

    def body(ya_ref, yb_ref, ga_ref, gb_ref, wba_ref, wbb_ref, ba_ref, bb_ref, ta_ref, tb_ref, mg_ref):
        ta = _dot(ya_ref[...], wba_ref[...])
        tb = _dot(yb_ref[...], wbb_ref[...])
        ta_ref[...] = ta.astype(ta_ref.dtype)
        tb_ref[...] = tb.astype(tb_ref.dtype)
        ga = _sigmoid(ga_ref[...] + ba_ref[...])
        gb = _sigmoid(gb_ref[...] + bb_ref[...])
        mg_ref[...] = (ga * ta + gb * tb).astype(mg_ref.dtype)

    tile = pl.BlockSpec((tm, tn), lambda i, j: (i, j))
    return pl.pallas_call(
        body, name=name, grid=(t // tm, nj),
        in_specs=[pl.BlockSpec((tm, D), lambda i, j: (i, 0)), pl.BlockSpec((tm, SSD_INNER), lambda i, j: (i, 0)),
                  pl.BlockSpec((tm, tn), lambda i, j: (i, G0 // tn + j)),
                  pl.BlockSpec((tm, tn), lambda i, j: (i, (G0 + D) // tn + j)),
                  pl.BlockSpec((D, tn), lambda i, j: (0, j)), pl.BlockSpec((SSD_INNER, tn), lambda i, j: (0, j)),
                  pl.BlockSpec((1, tn), lambda i, j: (0, j)), pl.BlockSpec((1, tn), lambda i, j: (0, nj + j))],
        out_specs=[tile, tile, tile],
        out_shape=[jax.ShapeDtypeStruct((t, D), _MXU)] * 3,
        compiler_params=_cp(("parallel", "parallel")),
    )(ya, yb, proj, proj, wba, wbb, bgate, bgate)


def _swiglu_mm(gu, wfo, residual, *, tm, tn, name):
    t = gu.shape[0]

    def body(gu_ref, w_ref, r_ref, act_ref, o_ref):
        @pl.when(pl.program_id(1) == 0)
        def _():
            gate = gu_ref[:, 0:D_FF].astype(F32)
            act_ref[...] = (_silu(gate) * gu_ref[:, D_FF:2 * D_FF].astype(F32)).astype(act_ref.dtype)
        o_ref[...] = jnp.dot(act_ref[...], w_ref[...], preferred_element_type=F32) + r_ref[...]

    return pl.pallas_call(
        body, name=name, grid=(t // tm, D // tn),
        in_specs=[pl.BlockSpec((tm, 2 * D_FF), lambda i, j: (i, 0)), pl.BlockSpec((D_FF, tn), lambda i, j: (0, j)),
                  pl.BlockSpec((tm, tn), lambda i, j: (i, j))],
        out_specs=[pl.BlockSpec((tm, D_FF), lambda i, j: (i, 0)), pl.BlockSpec((tm, tn), lambda i, j: (i, j))],
        out_shape=[jax.ShapeDtypeStruct((t, D_FF), _MXU), jax.ShapeDtypeStruct((t, D), F32)],
        compiler_params=_cp(("parallel", "arbitrary")),
    )(gu, wfo, residual)


def _ffn_bwd_act(dh, wfo, gu, *, tm, name):
    t = dh.shape[0]

    def body(dh_ref, w_ref, gu_ref, o_ref):
        dact = _dot_nt(dh_ref[...], w_ref[...])
        g = gu_ref[:, 0:D_FF].astype(F32)
        u = gu_ref[:, D_FF:2 * D_FF].astype(F32)
        sg = _sigmoid(g)
        o_ref[:, 0:D_FF] = (dact * u * (sg * (1.0 + g * (1.0 - sg)))).astype(o_ref.dtype)
        o_ref[:, D_FF:2 * D_FF] = (dact * (g * sg)).astype(o_ref.dtype)

    return pl.pallas_call(
        body, name=name, grid=(t // tm,),
        in_specs=[pl.BlockSpec((tm, D), lambda i: (i, 0)), pl.BlockSpec((D_FF, D), lambda i: (0, 0)),
                  pl.BlockSpec((tm, 2 * D_FF), lambda i: (i, 0))],
        out_specs=pl.BlockSpec((tm, 2 * D_FF), lambda i: (i, 0)),
        out_shape=jax.ShapeDtypeStruct((t, 2 * D_FF), _MXU),
        compiler_params=_cp(("parallel",)),
    )(dh, wfo, gu)


def _outproj_bwd(dh, wout, ta, tb, proj, bgate, dproj, *, tm, name):
    t = dh.shape[0]

    def body(dh_ref, w_ref, ta_ref, tb_ref, g_ref, b_ref, dta_ref, dtb_ref, dg_ref, db_ref):
        @pl.when(pl.program_id(0) == 0)
        def _():
            db_ref[...] = jnp.zeros_like(db_ref)
        dm = _dot_nt(dh_ref[...], w_ref[...])
        ga = _sigmoid(g_ref[:, 0:D] + b_ref[:, 0:D])
        gb = _sigmoid(g_ref[:, D:2 * D] + b_ref[:, D:2 * D])
        dta_ref[...] = (dm * ga).astype(dta_ref.dtype)
        dtb_ref[...] = (dm * gb).astype(dtb_ref.dtype)
        dga = dm * ta_ref[...].astype(F32) * ga * (1.0 - ga)
        dgb = dm * tb_ref[...].astype(F32) * gb * (1.0 - gb)
        dg_ref[:, 0:D] = dga.astype(dg_ref.dtype)
        dg_ref[:, D:2 * D] = dgb.astype(dg_ref.dtype)
        db_ref[0:1, 0:D] += _rsum(dga)
        db_ref[0:1, D:2 * D] += _rsum(dgb)

    row = lambda cols: pl.BlockSpec((tm, cols), lambda i: (i, 0))
    return pl.pallas_call(
        body, name=name, grid=(t // tm,),
        in_specs=[row(D), pl.BlockSpec((D, D), lambda i: (0, 0)), row(D), row(D),
                  pl.BlockSpec((tm, 2 * D), lambda i: (i, G0 // (2 * D))), pl.BlockSpec((1, 2 * D), lambda i: (0, 0))],
        out_specs=[row(D), row(D), pl.BlockSpec((tm, 2 * D), lambda i: (i, G0 // (2 * D))),
                   pl.BlockSpec((8, 2 * D), lambda i: (0, 0))],
        out_shape=[jax.ShapeDtypeStruct((t, D), _MXU), jax.ShapeDtypeStruct((t, D), _MXU),
                   jax.ShapeDtypeStruct(dproj, _MXU), jax.ShapeDtypeStruct((8, 2 * D), F32)],
        compiler_params=_cp(("arbitrary",)),
    )(dh, wout, ta, tb, proj, bgate)


def _loss_head(h, gf, target, *, tm, name):
    t = h.shape[0]

    def body(h_ref, g_ref, t_ref, loss_ref, dg_ref, dh_ref):
        @pl.when(pl.program_id(0) == 0)
        def _():
            loss_ref[...] = jnp.zeros_like(loss_ref)
            dg_ref[...] = jnp.zeros_like(dg_ref)
        x = h_ref[...]
        r = lax.rsqrt(jnp.mean(x * x, axis=-1, keepdims=True) + EPS)
        xh = x * r
        err = xh * g_ref[...] - t_ref[...]
        loss_ref[...] += 0.5 * jnp.sum(jnp.mean(err * err, axis=-1, keepdims=True), axis=0, keepdims=True)
        dy = err * (1.0 / D)
        dg_ref[0:1, :] += _rsum(dy * xh)
        dxh = dy * g_ref[...]
        dh_ref[...] = r * (dxh - xh * jnp.mean(dxh * xh, axis=-1, keepdims=True))

    row = pl.BlockSpec((tm, D), lambda i: (i, 0))
    return pl.pallas_call(
        body, name=name, grid=(t // tm,),
        in_specs=[row, pl.BlockSpec((1, D), lambda i: (0, 0)), row],
        out_specs=[pl.BlockSpec((8, 128), lambda i: (0, 0)), pl.BlockSpec((8, D), lambda i: (0, 0)), row],
        out_shape=[jax.ShapeDtypeStruct((8, 128), F32), jax.ShapeDtypeStruct((8, D), F32), jax.ShapeDtypeStruct((t, D), F32)],
        compiler_params=_cp(("arbitrary",)),
    )(h, gf, target)


def _row_tile(rows, cols, limit_bytes=1 << 20):
    best = None
    for tr in range(8, rows + 1, 8):
        if rows % tr == 0 and tr * cols * 4 <= limit_bytes:
            best = tr
    return best if best is not None else rows


def _adamw(w, g, m, v, *, name):
    rows, cols = w.shape
    tr = _row_tile(rows, cols)

    def body(w_ref, g_ref, m_ref, v_ref, d_ref, nm_ref, nv_ref):
        gv = g_ref[...]
        nm = ADAM_B1 * m_ref[...] + (1.0 - ADAM_B1) * gv
        nv = ADAM_B2 * v_ref[...] + (1.0 - ADAM_B2) * (gv * gv)
        m_hat = nm / (1.0 - ADAM_B1 ** ADAM_STEP)
        v_hat = nv / (1.0 - ADAM_B2 ** ADAM_STEP)
        d_ref[...] = -ADAM_LR * (m_hat / (jnp.sqrt(v_hat) + ADAM_EPS) + ADAM_WD * w_ref[...])
        nm_ref[...] = nm
        nv_ref[...] = nv

    blk = pl.BlockSpec((tr, cols), lambda i: (i, 0))
    shp = jax.ShapeDtypeStruct((rows, cols), F32)
    return pl.pallas_call(
        body, name=name, grid=(rows // tr,), in_specs=[blk] * 4, out_specs=[blk] * 3, out_shape=[shp] * 3,
        compiler_params=_cp(("parallel",)),
    )(w, g, m, v)


def _bd256(w):
    w4 = w.reshape(4, 4, 64, 64)
    eye = jnp.eye(4, dtype=w.dtype)
    return (w4[:, :, :, None, :] * eye[None, :, None, :, None]).reshape(4, 256, 256)


def _bd256_diag(g):
    g5 = g.reshape(4, 4, 64, 4, 64)
    return jnp.stack([g5[:, a, :, a, :] for a in range(4)], axis=1).reshape(16, 64, 64)


FFN_SHARD = 2 * D_FF // 4
W_IN_SHARD = IN_DIM // 4
W_IN_ROWS = 9344


def _w_in_cols(shards, c0, c1):
    out = []
    for p in range(4):
        lo, hi = max(c0, W_IN_SHARD * p), min(c1, W_IN_SHARD * (p + 1))
        if lo < hi:
            out.append(shards[p][:, lo - W_IN_SHARD * p:hi - W_IN_SHARD * p])
    return out


def _in_proj_weights(win):
    lblk = [_w_in_cols(win, 256 * j, 256 * (j + 1)) + _w_in_cols(win, D + 256 * j, D + 256 * (j + 1)) for j in range(4)]
    wp = jnp.concatenate(_w_in_cols(win, 2048, 4096) + _w_in_cols(win, 4096, 7168) + lblk[0] + lblk[1]
                         + _w_in_cols(win, 7200, 9248) + lblk[2] + lblk[3], axis=1)
    wdt = jnp.pad(jnp.concatenate(_w_in_cols(win, 7168, 7200), axis=1), ((0, 0), (0, DT_PAD - N_HEADS)))
    return wp, wdt


def _layer_weights(w, conv, small, l, wp, wdt):
    row = lambda v: v.reshape(1, -1)
    pad_h = lambda v: jnp.pad(v.reshape(1, -1), ((0, 0), (0, DT_PAD - N_HEADS)))
    lw = dict(cw=conv["lru_conv_w"][l], cb=row(small["lru_conv_b"][l]),
              wa=_bd256(small["lru_w_a"][l]).astype(_MXU), wx=_bd256(small["lru_w_x"][l]).astype(_MXU),
              ba=row(small["lru_b_a"][l]), bx=row(small["lru_b_x"][l]), lam=row(small["lru_lambda"][l]))
    sw = dict(cw=conv["ssd_conv_w"][l], cb=row(small["ssd_conv_b"][l]), dtb=pad_h(small["ssd_dt_bias"][l]),
              alog=pad_h(small["ssd_A_log"][l]), dsk=row(jnp.repeat(small["ssd_D"][l], HEAD_P)),
              ng=row(small["ssd_norm_g"][l]))
    return dict(wp=wp, wdt=wdt, lw=lw, sw=sw, wba=w["w_branch"][0:D], wbb=w["w_branch"][D:3 * D],
                wout=w["w_out"], wfi=w["w_ffn_in"], wfo=w["w_ffn_out"],
                g1=row(small["norm1_g"][l]), g2=row(small["norm2_g"][l]), bgate=row(small["b_gate"][l]))


def _tiles(t):
    return dict(tmi=min(2048, t), tmn=min(1024, t), tm=min(512, t), r=min(128, t), rb=min(128, t))


def _layer_fwd(h, w, conv, small, l, carried=None):
    tl = _tiles(h.shape[0])
    n = f"l{l}_"
    carried = carried or {}
    arrived = []

    def carry(kernel, key, n_main, *args, **kw):
        comm, finish = carried.get(key, (None, None))
        outs = list(kernel(*args, comm=comm, **kw))
        if comm is not None:
            arrived.append(finish(outs[n_main:]))
        return outs[:n_main]

    wp, wdt = _in_proj_weights(w["w_in"])
    xn, proj = carry(_norm_mm, "in_proj", 2, h, small["norm1_g"][l].reshape(1, -1), wp, tm=tl["tmi"], tn=1024,
                     name=n + "in_proj")
    w = dict(w)
    for layer, ws in arrived:
        if layer == l:
            w.update(ws)
    lwt = _layer_weights(w, conv, small, l, wp, wdt)
    dtraw = _mm_nn(xn, lwt["wdt"], tm=tl["tm"], tn=DT_PAD, name=n + "dt_proj")
    hl, ya, gates = carry(_lru_fwd, "lru", 3, proj, lwt["lw"], r=tl["r"], name=n + "lru_fwd")
    yssd, yb, states, xc, dsl = carry(_ssd_fwd, "ssd", 5, proj, dtraw, lwt["sw"], rb=tl["rb"], name=n + "ssd_fwd")
    ta, tb, merged = _branch_merge(ya, yb, proj, lwt["wba"], lwt["wbb"], lwt["bgate"], tm=tl["tmn"], tn=D, name=n + "merge")
    hmid = _mm_nn(merged, lwt["wout"], tm=tl["tmn"], tn=D, name=n + "out_proj", residual=h)
    xn2, gu = _norm_mm(hmid, lwt["g2"], lwt["wfi"], tm=tl["tmi"], tn=FFN_SHARD, name=n + "ffn_in", out_dtype=_MXU)
    act, hout = _swiglu_mm(gu, lwt["wfo"], hmid, tm=tl["tm"], tn=D, name=n + "ffn_out")
    saved = dict(h=h, xn=xn, proj=proj, dtraw=dtraw, hl=hl, ya=ya, gates=gates, yssd=yssd, yb=yb, states=states, xc=xc, dsl=dsl, ta=ta, tb=tb,
                 merged=merged, hmid=hmid, xn2=xn2, gu=gu, act=act)
    return hout, saved, lwt, [x for x in arrived if x[0] != l]


def _layer_bwd(dh, s, lwt, l, hooks=None):
    t = dh.shape[0]
    tl = _tiles(t)
    n = f"l{l}_"
    tt = tl["tmn"]
    big = {}
    hooks = hooks or {}

    def wgrad(key, a, b, name, **kw):
        big[key] = _wgrad(a, b, tt=tt, name=n + name, into=big.get(key), **kw)

    dgu = _ffn_bwd_act(dh, lwt["wfo"], s["gu"], tm=tl["tm"], name=n + "ffn_act_bwd")
    wgrad("w_ffn_out", s["act"], dh, "ffn_out_wgrad", ta=D_FF, tn=1024, out_shape=(D_FF, D),
          out_block=(D_FF, 1024), out_index=lambda o, j: (o, j))
    wgrad("w_ffn_in", s["xn2"], dgu, "ffn_in_wgrad", ta=D, tn=FFN_SHARD, out_shape=(4, D, FFN_SHARD),
          out_block=(None, D, FFN_SHARD), out_index=lambda o, j: (j, o, 0))
    dh1, dg2 = _mm_nt_rmsbwd(dgu, lwt["wfi"], s["hmid"], lwt["g2"], dh, tm=tl["tmn"], tk=FFN_SHARD, name=n + "ffn_in_dgrad")
    dta, dtb, dproj, dbg = _outproj_bwd(dh1, lwt["wout"], s["ta"], s["tb"], s["proj"], lwt["bgate"], (t, NP),
                                        tm=tl["tm"], name=n + "out_proj_bwd")
    rows_d = dict(ta=D, tn=D, out_block=(D, D), out_index=lambda o, j: (o, j))
    wgrad("w_out", s["merged"], dh1, "out_proj_wgrad", out_shape=(D, D), **rows_d)
    dya = _mm_nt(dta, lwt["wba"], tm=tl["tm"], name=n + "branch_a_dgrad")
    dyb = _mm_nt(dtb, lwt["wbb"], tm=tl["tm"], name=n + "branch_b_dgrad")
    wgrad("w_branch", s["ya"], dta, "branch_a_wgrad", out_shape=(3 * D, D), a_tab=[0], o_tab=[0], **rows_d)
    wgrad("w_branch", s["yb"], dtb, "branch_b_wgrad", out_shape=(3 * D, D), a_tab=[0, 1], o_tab=[1, 2], **rows_d)
    comm_1 = hooks["lru"](big) if "lru" in hooks else None
    dproj, lsm, dwa, dwx, *got_1 = _lru_bwd(s["proj"], s["hl"], s["gates"], dya, dproj, lwt["lw"], r=tl["r"], name=n + "lru_bwd",
                                            comm=comm_1)
    comm_2 = hooks["ssd"](got_1) if "ssd" in hooks else None
    dproj, ddt, gconv, gch, ghd, *got_2 = _ssd_bwd(s["proj"], s["dtraw"], s["yssd"], s["states"], s["xc"], s["dsl"], dyb, dproj, lwt["sw"],
                                                   rb=tl["rb"], name=n + "ssd_bwd", comm=comm_2)
    lsm = lsm.reshape(8, 8, D).sum(axis=1)
    gconv = gconv.reshape(5, 8, XBC).sum(axis=1)
    gch = gch.reshape(2, 8, SSD_INNER).sum(axis=1)
    w_in = dict(tn=D, out_shape=(W_IN_ROWS, D), out_index=lambda o, j: (o, j))
    wgrad("w_in", dproj, s["xn"], "in_proj_wgrad", ta=1024, out_block=(1024, D),
          a_tab=list(range(9)), o_tab=[2, 3, 4, 5, 6, 0, 7, 8, 1], **w_in)
    wgrad("w_in", ddt, s["xn"], "dt_proj_wgrad", ta=DT_PAD, out_block=(DT_PAD, D), a_tab=[0],
          o_tab=[NP // DT_PAD], **w_in)
    comm_3 = hooks["in_dgrad"](big) if "in_dgrad" in hooks else None
    dh0, dg1, *got_3 = _mm_nt_rmsbwd(dproj, lwt["wp"], s["h"], lwt["g1"], dh1, tm=tl["tmn"], tk=2304,
                                     name=n + "in_proj_dgrad", extra=(ddt, lwt["wdt"]), comm=comm_3)
    grads = dict(
        lru_conv_w=lsm[0:4], lru_conv_b=lsm[4], lru_b_a=lsm[5], lru_b_x=lsm[6], lru_lambda=lsm[7],
        lru_w_a=_bd256_diag(dwa), lru_w_x=_bd256_diag(dwx),
        ssd_conv_w=gconv[0:4], ssd_conv_b=gconv[4], ssd_norm_g=gch[0], ssd_D=gch[1].reshape(N_HEADS, HEAD_P).sum(axis=-1),
        ssd_dt_bias=ghd[0, 0:N_HEADS], ssd_A_log=ghd[1, 0:N_HEADS],
        b_gate=dbg[0], norm1_g=dg1[0], norm2_g=dg2[0])
    return dh0, grads, big, (got_2, got_3)


def _local_step(x, target, w, conv, small, prefetch=None, early_reduce=None):
    h = x
    w = [dict(wl) for wl in w]
    lwts, saved = [], []
    for l in range(N_LAYERS):
        h, s, lwt, arrived = _layer_fwd(h, w[l], conv, small, l, prefetch if l == 0 else None)
        for layer, ws in arrived:
            w[layer].update(ws)
        lwts.append(lwt)
        saved.append(s)
    loss_blk, dgf, dh = _loss_head(h, small["norm_f"].reshape(1, D), target, tm=_tiles(x.shape[0])["tm"], name="loss_head")
    per_layer, big, carried = [None] * N_LAYERS, [None] * N_LAYERS, None
    for l in reversed(range(N_LAYERS)):
        hooks = early_reduce(big[1]) if (early_reduce is not None and l == 0) else None
        dh, per_layer[l], big[l], carried = _layer_bwd(dh, saved[l], lwts[l], l, hooks)
    grads = {k: jnp.stack([per_layer[l][k] for l in range(N_LAYERS)], axis=0) for k in per_layer[0]}
    grads["norm_f"] = dgf[0]
    return loss_blk, dh, grads, big, carried


PACK_W = 1024
BIG = (("w_in", W_IN_SHARD, D, W_IN_SHARD, 256), ("w_branch", 768, D, 256, D), ("w_out", 256, D, 256, D),
       ("w_ffn_in", D, FFN_SHARD, 256, FFN_SHARD), ("w_ffn_out", 704, D, 352, D))
CONV = ("lru_conv_w", "ssd_conv_w")
SMALL = ("norm1_g", "b_gate", "lru_conv_b", "lru_w_a", "lru_b_a", "lru_w_x", "lru_b_x", "lru_lambda", "ssd_conv_b",
         "ssd_dt_bias", "ssd_A_log", "ssd_D", "ssd_norm_g", "norm2_g", "norm_f")
_WIRE = jnp.bfloat16
N_CHIPS = 4
N_DEV = 8


def _mesh_pos():
    return lax.axis_index("x"), lax.axis_index("y"), lax.axis_index("c")


HBM_SPEC = pl.BlockSpec(memory_space=pltpu.HBM)


def _remote(src, dst, send_sems, recv_sems, k, to):
    return pltpu.make_async_remote_copy(src_ref=src, dst_ref=dst, send_sem=send_sems.at[k], recv_sem=recv_sems.at[k],
                                        device_id=to, device_id_type=MESH)


def _other_chips(x, y):
    return [(1 - x, y), (x, 1 - y), (1 - x, 1 - y)]


def _weight_fetch(loc, layer, owner):
    names = list(owner)
    rows = {n: loc[n].shape[1] for n in names}
    by_chip = ("w_in", "w_ffn_in")
    shapes = [((N_CHIPS,) + loc[n].shape[1:]) if n in by_chip else (N_CHIPS * rows[n], D) for n in names]

    def place(o_ref, n, chip):
        if n in by_chip:
            return o_ref.at[chip]
        return o_ref.at[pl.ds(pl.multiple_of(chip * rows[n], 16), rows[n]), :]

    def step(which, in_refs, o_refs, send_sems, recv_sems):
        x, y, c = _mesh_pos()
        s = 2 * x + y
        sib = (x, y, 1 - c)
        chips = _other_chips(x, y)
        for core in (0, 1):
            @pl.when(c == core)
            def _():
                for k, n in enumerate(names):
                    for j, (px, py) in enumerate(chips):
                        landed = place(o_refs[k], n, 2 * px + py)
                        sent = _remote(in_refs[k].at[layer], place(o_refs[k], n, s), send_sems, recv_sems, 3 * k + j,
                                       (px, py, c))
                        arrives = _remote(in_refs[k].at[layer], landed, send_sems, recv_sems, 3 * k + j, (px, py, c))
                        passed = _remote(landed, landed, send_sems, recv_sems, 3 * (len(names) + k) + j, sib)
                        if owner[n] == core:
                            if which == "start":
                                sent.start()
                            elif which == "mid":
                                arrives.wait_recv()
                                passed.start()
                            else:
                                sent.wait_send()
                                passed.wait_send()
                        elif which == "end":
                            passed.wait_recv()

    return dict(inputs=[loc[n] for n in names], names=names,
                out_shapes=[jax.ShapeDtypeStruct(shp, loc[n].dtype) for shp, n in zip(shapes, names)],
                sems=[pltpu.SemaphoreType.DMA((6 * len(names),)), pltpu.SemaphoreType.DMA((6 * len(names),))],
                start=functools.partial(step, "start"), mid=functools.partial(step, "mid"),
                end=functools.partial(step, "end"))


def _comm_now(comm, name):
    n, no = len(comm["inputs"]), len(comm["out_shapes"])

    def body(*refs):
        parts = (refs[:n], refs[n:n + no]) + tuple(refs[n + no:])
        comm["start"](*parts)
        comm["mid"](*parts)
        comm["end"](*parts)

    return pl.pallas_call(
        body, name=name, in_specs=[HBM_SPEC] * n, out_specs=[HBM_SPEC] * no, out_shape=comm["out_shapes"],
        scratch_shapes=comm["sems"],
    )(*comm["inputs"])


def _sibling_send(bufs, layer):
    n = len(bufs)

    def step(which, in_refs, o_refs, send_sems, recv_sems):
        x, y, c = _mesh_pos()
        copies = [_remote(in_refs[k], o_refs[k], send_sems, recv_sems, k, (x, y, 1 - c)) for k in range(n)]

        @pl.when(c != layer)
        def _():
            for cp in copies:
                if which == "start":
                    cp.start()
                elif which == "end":
                    cp.wait_send()

        @pl.when(c == layer)
        def _():
            for cp in copies:
                if which == "end":
                    cp.wait_recv()

    return dict(inputs=list(bufs), out_shapes=[jax.ShapeDtypeStruct(b.shape, b.dtype) for b in bufs],
                sems=[pltpu.SemaphoreType.DMA((n,)), pltpu.SemaphoreType.DMA((n,))],
                start=functools.partial(step, "start"), mid=functools.partial(step, "mid"),
                end=functools.partial(step, "end"))


def _add_cast(g, recv, own, *, a, tr, tc, name):
    wd = g.shape[1]
    nr = a // tr

    def body(own_ref, g_ref, r_ref, o_ref):
        @pl.when(own_ref[0] == 1)
        def _():
            o_ref[...] = (g_ref[...] + r_ref[...]).astype(o_ref.dtype)

    blk = pl.BlockSpec((tr, tc), lambda p, i, j, own_ref: ((p * nr + i) * own_ref[0], j * own_ref[0]))
    return pl.pallas_call(
        body, name=name,
        grid_spec=pltpu.PrefetchScalarGridSpec(
            num_scalar_prefetch=1, grid=(N_CHIPS, nr, wd // tc), in_specs=[blk, blk],
            out_specs=pl.BlockSpec((None, tr, tc), lambda p, i, j, own_ref: (p * own_ref[0], i * own_ref[0], j * own_ref[0]))),
        out_shape=jax.ShapeDtypeStruct((N_CHIPS, a, wd), _WIRE),
        compiler_params=_cp(("arbitrary", "arbitrary", "arbitrary")),
    )(own, g, recv)


def _chip_exchange(parts, layer):
    n = len(parts)

    def step(which, s_refs, o_refs, send_sems, recv_sems):
        x, y, c = _mesh_pos()
        s = 2 * x + y

        @pl.when(c == layer)
        def _():
            for j, (px, py) in enumerate(_other_chips(x, y)):
                for k in range(n):
                    p = 2 * px + py
                    sent = _remote(s_refs[k].at[p], o_refs[k].at[s], send_sems, recv_sems, n * j + k, (px, py, c))
                    if which == "start":
                        sent.start()
                    elif which == "end":
                        _remote(s_refs[k].at[p], o_refs[k].at[p], send_sems, recv_sems, n * j + k, (px, py, c)).wait_recv()
                        sent.wait_send()

    return dict(inputs=list(parts), out_shapes=[jax.ShapeDtypeStruct(p.shape, p.dtype) for p in parts],
                sems=[pltpu.SemaphoreType.DMA((3 * n,)), pltpu.SemaphoreType.DMA((3 * n,))],
                start=functools.partial(step, "start"), mid=functools.partial(step, "mid"),
                end=functools.partial(step, "end"))


def _sum_slots(slots, own, sel, *, tr, tc, name, layer=None, into=None):
    n, rows, wd = slots.shape
    k = own.shape[0]

    def body(sel_ref, s_ref, own_ref, *rest):
        o_ref = rest[-1]

        @pl.when(sel_ref[1] == 1)
        def _():
            mine = sel_ref[0]
            acc = jnp.zeros((tr, tc), F32)
            for p in range(n):
                acc = acc + jnp.where(mine == p, own_ref[...].astype(F32), s_ref[p].astype(F32))
            o_ref[...] = acc

    if layer is not None:
        out_spec = pl.BlockSpec((None, tr, tc), lambda i, j, sel_ref: (layer, i * sel_ref[1], j * sel_ref[1]))
        out_shape = jax.ShapeDtypeStruct((N_LAYERS, rows, wd), F32)
    else:
        out_spec = pl.BlockSpec((tr, tc), lambda i, j, sel_ref: (i * sel_ref[1], j * sel_ref[1]))
        out_shape = jax.ShapeDtypeStruct((rows, wd), F32)
    in_specs = [pl.BlockSpec((n, tr, tc), lambda i, j, sel_ref: (0, i * sel_ref[1], j * sel_ref[1])),
                pl.BlockSpec((None, tr, tc), lambda i, j, sel_ref: (sel_ref[0] if k > 1 else 0, i * sel_ref[1],
                                                                    j * sel_ref[1]))]
    args = [sel, slots, own]
    if into is not None:
        in_specs.append(pl.BlockSpec(memory_space=pl.ANY))
        args.append(into)
    return pl.pallas_call(
        body, name=name,
        grid_spec=pltpu.PrefetchScalarGridSpec(num_scalar_prefetch=1, grid=(rows // tr, wd // tc), in_specs=in_specs,
                                               out_specs=out_spec),
        out_shape=out_shape, input_output_aliases={3: 0} if into is not None else {},
        compiler_params=_cp(("arbitrary", "arbitrary")),
    )(*args)


def _sibling_share(both):
    n = len(both)

    def body(*refs):
        o_refs, (send_sems, recv_sems) = refs[n:2 * n], refs[2 * n:]
        x, y, c = _mesh_pos()
        sends = [_remote(o_refs[k].at[c], o_refs[k].at[c], send_sems, recv_sems, k, (x, y, 1 - c)) for k in range(n)]
        for cp in sends:
            cp.start()
        for k in range(n):
            _remote(o_refs[k].at[1 - c], o_refs[k].at[1 - c], send_sems, recv_sems, k, (x, y, 1 - c)).wait_recv()
        for cp in sends:
            cp.wait_send()

    return pl.pallas_call(
        body, name="grad_sibling_share", in_specs=[HBM_SPEC] * n, out_specs=[HBM_SPEC] * n,
        out_shape=[jax.ShapeDtypeStruct(b.shape, b.dtype) for b in both], input_output_aliases={k: k for k in range(n)},
        scratch_shapes=[pltpu.SemaphoreType.DMA((n,)), pltpu.SemaphoreType.DMA((n,))],
    )(*both)


def _allgather_devices(part):
    rows, wd = part.shape

    def step(which, in_refs, o_refs, send_sems, recv_sems):
        (p_ref,), (o_ref,) = in_refs, o_refs
        x, y, c = _mesh_pos()
        sib = (x, y, 1 - c)
        chips = _other_chips(x, y)
        slot = lambda px, py, pc: o_ref.at[4 * px + 2 * py + pc]
        first = [_remote(p_ref, slot(x, y, c), send_sems, recv_sems, 0, sib)]
        first += [_remote(p_ref, slot(x, y, c), send_sems, recv_sems, 1 + j, (px, py, c)) for j, (px, py) in enumerate(chips)]
        passed = [_remote(slot(px, py, c), slot(px, py, c), send_sems, recv_sems, 4 + j, sib)
                  for j, (px, py) in enumerate(chips)]
        if which == "start":
            for cp in first:
                cp.start()
        elif which == "mid":
            for j, (px, py) in enumerate(chips):
                _remote(p_ref, slot(px, py, c), send_sems, recv_sems, 1 + j, (px, py, c)).wait_recv()
                passed[j].start()
        else:
            _remote(p_ref, slot(x, y, 1 - c), send_sems, recv_sems, 0, sib).wait_recv()
            for j, (px, py) in enumerate(chips):
                _remote(slot(px, py, 1 - c), slot(px, py, 1 - c), send_sems, recv_sems, 4 + j, sib).wait_recv()
            for cp in first + passed:
                cp.wait_send()

    return dict(inputs=[part], out_shapes=[jax.ShapeDtypeStruct((N_DEV, rows, wd), part.dtype)],
                sems=[pltpu.SemaphoreType.DMA((N_DEV - 1,)), pltpu.SemaphoreType.DMA((N_DEV - 1,))],
                start=functools.partial(step, "start"), mid=functools.partial(step, "mid"),
                end=functools.partial(step, "end"))


def _comm_both(a, b):
    na, nao = len(a["inputs"]), len(a["out_shapes"])

    def step(which, in_refs, o_refs, sa, ra, sb, rb_):
        a[which](in_refs[:na], o_refs[:nao], sa, ra)
        b[which](in_refs[na:], o_refs[nao:], sb, rb_)

    return dict(inputs=a["inputs"] + b["inputs"], out_shapes=a["out_shapes"] + b["out_shapes"], sems=a["sems"] + b["sems"],
                start=functools.partial(step, "start"), mid=functools.partial(step, "mid"),
                end=functools.partial(step, "end"))


def _by_chip_to_full(stack):
    _, nl, r, b = stack.shape
    return stack.transpose(1, 2, 0, 3).reshape(nl, r, N_CHIPS * b)


def _sharded_step(a):
    x = a["x"][0]
    target = a["loss_target"][0]
    cx, cy, cc = _mesh_pos()
    chip = (2 * cx + cy).astype(jnp.int32)
    core = cc.astype(jnp.int32)
    me = (4 * cx + 2 * cy + cc).astype(jnp.int32)
    zero = jnp.zeros((), jnp.int32)
    dus = lax.dynamic_update_slice

    loc = {n: a[n].astype(_MXU) for n, *_ in BIG}

    def with_own(got, names, layer):
        out = {}
        for g, n in zip(got, names):
            mine = loc[n][layer]
            out[n] = (dus(g, mine[None], (chip, zero, zero)) if g.ndim == 3 else dus(g, mine, (chip * mine.shape[0], zero)))
        return out

    rest = {"w_ffn_in": 0, "w_branch": 1, "w_out": 1, "w_ffn_out": 1}
    conv_loc = jnp.concatenate([a[n].reshape(-1, PACK_W) for n in CONV], axis=0)
    now = _weight_fetch(loc, 0, {"w_in": 0})
    conv_all, *got_now = _comm_now(_comm_both(_allgather_devices(conv_loc), now), "allgather_weights")
    w0 = with_own(got_now, now["names"], 0)
    later = {"in_proj": (0, _weight_fetch(loc, 0, rest)), "lru": (1, _weight_fetch(loc, 1, {"w_in": 0})),
             "ssd": (1, _weight_fetch(loc, 1, rest))}
    prefetch = {k: (f, functools.partial(lambda got, layer, f: (layer, with_own(got, f["names"], layer)), layer=layer, f=f))
                for k, (layer, f) in later.items()}
    conv_all = dus(conv_all, conv_loc[None], (me, zero, zero))[0::2]
    conv, off = {}, 0
    for n in CONV:
        rows = a[n].size // PACK_W
        conv[n] = _by_chip_to_full(conv_all[:, off:off + rows].reshape((N_CHIPS,) + a[n].shape))
        off += rows
    small = {n: a[n] for n in SMALL}

    views = lambda big_l, specs: [big_l[n].reshape(-1, wd) for n, _, wd, _, _ in specs]
    owns = lambda layer: (core == layer).astype(jnp.int32)
    w_in_only, others = BIG[:1], BIG[1:]

    def partial_sums(big_l, recv, layer, specs):
        return [_add_cast(v, r, owns(layer).reshape(1), a=rows, tr=tr, tc=tc, name=f"grad_add_sibling_l{layer}_{n}")
                for v, r, (n, rows, _, tr, tc) in zip(views(big_l, specs), recv, specs)]

    def reduced(slots, parts, layer, into, specs):
        sel = jnp.stack([chip, owns(layer)])
        return [_sum_slots(s, p, sel, tr=tr, tc=tc, name=f"grad_sum_chips_l{layer}_{n}", layer=layer, into=buf)
                for s, p, buf, (n, _, _, tr, tc) in zip(slots, parts, into, specs)]

    kept = {}

    def early_reduce(big_1):
        def during_lru(big_0):
            kept["big_0"] = dict(big_0)
            return _comm_both(_sibling_send(views(big_1, BIG), 1), _sibling_send(views(big_0, others), 0))

        def during_ssd(recv):
            kept["parts_1"] = partial_sums(big_1, recv[:len(BIG)], 1, BIG)
            kept["parts_0"] = partial_sums(kept["big_0"], recv[len(BIG):], 0, others)
            return _comm_both(_chip_exchange(kept["parts_1"], 1), _chip_exchange(kept["parts_0"], 0))

        return dict(lru=during_lru, ssd=during_ssd, in_dgrad=lambda big_0: _sibling_send(views(big_0, w_in_only), 0))

    loss_blk, grad_x, grads, big, (slots, recv_in) = _local_step(x, target, [w0, {}], conv, small, prefetch, early_reduce)
    loss = lax.psum(loss_blk[0, 0], ("x", "y", "c"))
    both = reduced(slots[:len(BIG)], kept["parts_1"], 1, [None] * len(BIG), BIG)
    both[1:] = reduced(slots[len(BIG):], kept["parts_0"], 0, both[1:], others)
    parts_in = partial_sums(big[0], recv_in, 0, w_in_only)
    names = SMALL + CONV
    srows = -(-sum(grads[n].size for n in names) // (8 * PACK_W)) * 8
    flat = lambda d, ns: jnp.concatenate([d[n].reshape(-1) for n in ns])
    padto = lambda v: jnp.pad(v, (0, srows * PACK_W - v.shape[0])).reshape(srows, PACK_W)
    g_own = padto(flat(grads, names))
    g_all, *slots_in = _comm_now(_comm_both(_allgather_devices(g_own), _chip_exchange(parts_in, 0)), "grad_chip_exchange")
    both[:1] = reduced(slots_in, parts_in, 0, both[:1], w_in_only)
    done = dict(zip([n for n, *_ in BIG], _sibling_share(both)))
    g_big = {n: done[n].reshape(a[n].shape) for n in ("w_branch", "w_out", "w_ffn_in", "w_ffn_out")}
    gt = done["w_in"].transpose(0, 2, 1)
    first = jnp.concatenate([gt[..., 512 * j + 256 * part:512 * j + 256 * (part + 1)] for part in range(2) for j in range(4)]
                            + [gt[..., 2 * D:]], axis=-1)
    tail = W_IN_SHARD - (IN_DIM - 7168)
    last = jnp.concatenate([gt[..., :tail], gt[..., W_IN_SHARD - N_HEADS:], gt[..., tail:W_IN_SHARD - N_HEADS]], axis=-1)
    g_big["w_in"] = jnp.where(chip == 0, first, jnp.where(chip == N_CHIPS - 1, last, gt))

    g_sum = _sum_slots(g_all, g_own[None], jnp.stack([me, zero + 1]), tr=srows, tc=PACK_W, name="small_grad_sum")
    off, g_small = 0, {}
    for n in names:
        g_small[n] = g_sum.reshape(-1)[off:off + grads[n].size].reshape(grads[n].shape)
        off += grads[n].size
    for n in CONV:
        width = a[n].shape[2]
        g_big[n] = lax.dynamic_slice(g_small.pop(n), (zero, zero, chip * width), a[n].shape)

    out_g, out_d, out_m, out_v = {}, {}, {}, {}
    for n in g_big:
        shp = a[n].shape
        two_d = (shp[0] * shp[1], shp[2])
        d_, m_, v_ = _adamw(a[n].reshape(two_d), g_big[n].reshape(two_d), a["m_" + n].reshape(two_d),
                            a["v_" + n].reshape(two_d), name="adamw_" + n)
        out_g[n], out_d[n], out_m[n], out_v[n] = g_big[n], d_.reshape(shp), m_.reshape(shp), v_.reshape(shp)
    d_, m_, v_ = _adamw(padto(flat(a, SMALL)), padto(flat(g_small, SMALL)), padto(flat({n: a["m_" + n] for n in SMALL}, SMALL)),
                        padto(flat({n: a["v_" + n] for n in SMALL}, SMALL)), name="adamw_small")
    off = 0
    for n in SMALL:
        cut = lambda v: v.reshape(-1)[off:off + a[n].size].reshape(a[n].shape)
        out_g[n], out_d[n], out_m[n], out_v[n] = g_small[n], cut(d_), cut(m_), cut(v_)
        off += a[n].size
    return loss, grad_x[None], out_g, out_d, out_m, out_v


WEIGHTS = ("norm1_g", "w_in", "b_gate", "lru_conv_w", "lru_conv_b", "lru_w_a", "lru_b_a", "lru_w_x", "lru_b_x", "lru_lambda",
           "ssd_conv_w", "ssd_conv_b", "ssd_dt_bias", "ssd_A_log", "ssd_D", "ssd_norm_g", "w_branch", "w_out", "norm2_g",
           "w_ffn_in", "w_ffn_out", "norm_f")
INPUTS = ("x",) + WEIGHTS + ("loss_target",) + tuple("m_" + n for n in WEIGHTS) + tuple("v_" + n for n in WEIGHTS)


def kernel(x, norm1_g, w_in, b_gate, lru_conv_w, lru_conv_b, lru_w_a, lru_b_a, lru_w_x, lru_b_x, lru_lambda, ssd_conv_w, ssd_conv_b, ssd_dt_bias, ssd_A_log, ssd_D, ssd_norm_g, w_branch, w_out, norm2_g, w_ffn_in, w_ffn_out, norm_f, loss_target, m_norm1_g, m_w_in, m_b_gate, m_lru_conv_w, m_lru_conv_b, m_lru_w_a, m_lru_b_a, m_lru_w_x, m_lru_b_x, m_lru_lambda, m_ssd_conv_w, m_ssd_conv_b, m_ssd_dt_bias, m_ssd_A_log, m_ssd_D, m_ssd_norm_g, m_w_branch, m_w_out, m_norm2_g, m_w_ffn_in, m_w_ffn_out, m_norm_f, v_norm1_g, v_w_in, v_b_gate, v_lru_conv_w, v_lru_conv_b, v_lru_w_a, v_lru_b_a, v_lru_w_x, v_lru_b_x, v_lru_lambda, v_ssd_conv_w, v_ssd_conv_b, v_ssd_dt_bias, v_ssd_A_log, v_ssd_D, v_ssd_norm_g, v_w_branch, v_w_out, v_norm2_g, v_w_ffn_in, v_w_ffn_out, v_norm_f):
    args = (x, norm1_g, w_in, b_gate, lru_conv_w, lru_conv_b, lru_w_a, lru_b_a, lru_w_x, lru_b_x, lru_lambda, ssd_conv_w, ssd_conv_b, ssd_dt_bias, ssd_A_log, ssd_D, ssd_norm_g, w_branch, w_out, norm2_g, w_ffn_in, w_ffn_out, norm_f, loss_target, m_norm1_g, m_w_in, m_b_gate, m_lru_conv_w, m_lru_conv_b, m_lru_w_a, m_lru_b_a, m_lru_w_x, m_lru_b_x, m_lru_lambda, m_ssd_conv_w, m_ssd_conv_b, m_ssd_dt_bias, m_ssd_A_log, m_ssd_D, m_ssd_norm_g, m_w_branch, m_w_out, m_norm2_g, m_w_ffn_in, m_w_ffn_out, m_norm_f, v_norm1_g, v_w_in, v_b_gate, v_lru_conv_w, v_lru_conv_b, v_lru_w_a, v_lru_b_a, v_lru_w_x, v_lru_b_x, v_lru_lambda, v_ssd_conv_w, v_ssd_conv_b, v_ssd_dt_bias, v_ssd_A_log, v_ssd_D, v_ssd_norm_g, v_w_branch, v_w_out, v_norm2_g, v_w_ffn_in, v_w_ffn_out, v_norm_f)
    assert len(args) == len(INPUTS)
    loss, grad_x, g, d, m, v = _sharded_step(dict(zip(INPUTS, args)))
    return (loss, grad_x, *[g[n] for n in WEIGHTS], *[d[n] for n in WEIGHTS], *[m[n] for n in WEIGHTS],
            *[v[n] for n in WEIGHTS])
```
